```python
import jax, jax.numpy as jnp
from jax import lax
import numpy as np


D_MODEL = 1024
BATCH = 8
SEQ = 16384
DEPTH = 2

N_MIXERS = 2
FOX_HEADS = 16
FOX_HEAD_DIM = D_MODEL // FOX_HEADS
Q_BLOCK = 128
SSM_D_INNER = 2 * D_MODEL
SSM_HEAD_DIM = 64
SSM_HEADS = SSM_D_INNER // SSM_HEAD_DIM
SSM_GROUPS = 8
SSM_HEADS_PER_GROUP = SSM_HEADS // SSM_GROUPS
SSM_STATE = 128
SSM_CONV = 4
SSM_CHUNK = 128
SSM_CONV_DIM = SSM_D_INNER + 2 * SSM_GROUPS * SSM_STATE
SSM_IN_DIM = SSM_D_INNER + SSM_CONV_DIM + SSM_HEADS
D_FF = 4 * D_MODEL
LN_EPS = 1e-5
RMS_EPS = 1e-5

kernel_name = 'hybrid_fox_ssd_sqrelu_deepnorm_adaln'


def layer_norm(x, g, b):
    xf = x.astype(jnp.float32)
    mu = jnp.mean(xf, axis=-1, keepdims=True)
    var = jnp.mean(jnp.square(xf - mu), axis=-1, keepdims=True)
    return ((xf - mu) * lax.rsqrt(var + LN_EPS) * g.astype(jnp.float32) + b.astype(jnp.float32)).astype(x.dtype)


def fox_attention(u, w_in, b_f, w_o):
    bsz, s, _ = u.shape
    h, dh = FOX_HEADS, FOX_HEAD_DIM
    proj = u @ w_in
    q, k, v, f = jnp.split(proj, [D_MODEL, 2 * D_MODEL, 3 * D_MODEL], axis=-1)
    q = q.reshape(bsz, s, h, dh).transpose(0, 2, 1, 3)
    k = k.reshape(bsz, s, h, dh).transpose(0, 2, 1, 3)
    v = v.reshape(bsz, s, h, dh).transpose(0, 2, 1, 3)
    log_f = jax.nn.log_sigmoid((f + b_f).astype(jnp.float32))
    cum = jnp.cumsum(log_f, axis=1).transpose(0, 2, 1)
    nb = s // Q_BLOCK
    q_blocks = q.reshape(bsz, h, nb, Q_BLOCK, dh).transpose(2, 0, 1, 3, 4)
    cum_blocks = cum.reshape(bsz, h, nb, Q_BLOCK).transpose(2, 0, 1, 3)
    pos_blocks = jnp.arange(s, dtype=jnp.int32).reshape(nb, Q_BLOCK)
    key_pos = jnp.arange(s, dtype=jnp.int32)
    scale = dh ** -0.5

    def block(args):
        qb, cb, pb = args
        logits = jnp.einsum('bhqd,bhkd->bhqk', qb, k).astype(jnp.float32) * scale
        logits = logits + cb[..., :, None] - cum[:, :, None, :]
        causal = pb[:, None] >= key_pos[None, :]
        logits = jnp.where(causal, logits, -jnp.inf)
        p = jax.nn.softmax(logits, axis=-1).astype(v.dtype)
        return jnp.einsum('bhqk,bhkd->bhqd', p, v)

    out = lax.map(block, (q_blocks, cum_blocks, pos_blocks))
    out = out.transpose(1, 0, 3, 2, 4).reshape(bsz, s, D_MODEL)
    return out @ w_o


def causal_depthwise_conv(x, w, b):
    kw, ch = w.shape
    y = lax.conv_general_dilated(x, w[:, None, :], window_strides=(1,), padding=[(kw - 1, 0)],
                                 dimension_numbers=('NWC', 'WIO', 'NWC'), feature_group_count=ch)
    return y + b


def ssd_mixer(u, w_in, conv_w, conv_b, dt_bias, a_log, d_skip, norm_w, w_out):
    bsz, s, _ = u.shape
    g, r, n, p, l = SSM_GROUPS, SSM_HEADS_PER_GROUP, SSM_STATE, SSM_HEAD_DIM, SSM_CHUNK
    f32 = jnp.float32
    proj = u @ w_in
    z, xbc, dt = jnp.split(proj, [SSM_D_INNER, SSM_D_INNER + SSM_CONV_DIM], axis=-1)
    xbc = jax.nn.silu(causal_depthwise_conv(xbc, conv_w, conv_b))
    xs, bm, cm = jnp.split(xbc, [SSM_D_INNER, SSM_D_INNER + g * n], axis=-1)
    nc = s // l
    xc = xs.astype(f32).reshape(bsz, nc, l, g, r, p)
    bc = bm.astype(f32).reshape(bsz, nc, l, g, n)
    cc = cm.astype(f32).reshape(bsz, nc, l, g, n)
    dt = jax.nn.softplus((dt + dt_bias).astype(f32))
    a = -jnp.exp(a_log.astype(f32))
    dt_c = dt.reshape(bsz, nc, l, g, r)
    a_cs = jnp.cumsum((dt_c * a.reshape(g, r)).transpose(0, 1, 3, 4, 2), axis=-1)
    xdt = xc * dt_c[..., None]
    cb = jnp.einsum('bclgn,bcsgn->bcgls', cc, bc)
    seg = a_cs[..., :, None] - a_cs[..., None, :]
    tril = jnp.tril(jnp.ones((l, l), dtype=bool))
    decay_in = jnp.exp(jnp.where(tril, seg, -jnp.inf))
    y_diag = jnp.einsum('bcgls,bcgrls,bcsgrp->bclgrp', cb, decay_in, xdt)
    decay_to_end = jnp.exp(a_cs[..., -1:] - a_cs)
    states = jnp.einsum('bclgn,bcgrl,bclgrp->bcgrpn', bc, decay_to_end, xdt)
    chunk_decay = jnp.exp(a_cs[..., -1])

    def step(hst, inp):
        s_c, d_c = inp
        return hst * d_c[..., None, None] + s_c, hst

    h0 = jnp.zeros_like(states[:, 0])
    _, prev = lax.scan(step, h0, (states.transpose(1, 0, 2, 3, 4, 5), chunk_decay.transpose(1, 0, 2, 3)))
    prev = prev.transpose(1, 0, 2, 3, 4, 5)
    y_off = jnp.einsum('bclgn,bcgrpn,bcgrl->bclgrp', cc, prev, jnp.exp(a_cs))
    y = y_diag + y_off + xc * d_skip.astype(f32).reshape(g, r)[:, :, None]
    y = y.reshape(bsz, s, SSM_D_INNER)
    yg = (y * jax.nn.silu(z.astype(f32))).reshape(bsz, s, g, SSM_D_INNER // g)
    yg = yg * lax.rsqrt(jnp.mean(jnp.square(yg), axis=-1, keepdims=True) + RMS_EPS)
    y = (yg.reshape(bsz, s, SSM_D_INNER) * norm_w.astype(f32)).astype(u.dtype)
    return y @ w_out


def sq_relu_mlp(u, w1, w2):
    return jnp.square(jax.nn.relu(u @ w1)) @ w2


def _fwd_setup_inputs(seed: int = 0) -> dict:
    key = jax.random.key(seed)
    ks = jax.random.split(key, 24)
    n_a = (DEPTH + 1) // 2
    n_b = DEPTH // 2
    beta = (8.0 * DEPTH) ** -0.25
    f32 = jnp.float32

    def nrm(k, shape, fan_in, s=1.0):
        return jax.random.normal(k, shape, f32) * (s * fan_in ** -0.5)

    def small(k, shape, s=0.02):
        return jax.random.normal(k, shape, f32) * s

    dt0 = jnp.exp(jax.random.uniform(ks[17], (n_b, SSM_HEADS), f32, np.log(1e-3), np.log(1e-1)))
    dt_bias = dt0 + jnp.log(-jnp.expm1(-dt0))
    a_log = jnp.log(jax.random.uniform(ks[18], (n_b, SSM_HEADS), f32, 1.0, 16.0))
    return {
        'x': jax.random.normal(ks[0], (BATCH, SEQ, D_MODEL), f32),
        'c': jax.random.normal(ks[1], (BATCH, D_MODEL), f32),
        'ada_w': nrm(ks[2], (DEPTH, D_MODEL, 6 * D_MODEL), D_MODEL, 0.1),
        'ada_b': small(ks[3], (DEPTH, 6 * D_MODEL), 0.01),
        'ln_mix_g': 1.0 + small(ks[4], (DEPTH, D_MODEL)),
        'ln_mix_b': small(ks[5], (DEPTH, D_MODEL)),
        'ln_mlp_g': 1.0 + small(ks[6], (DEPTH, D_MODEL)),
        'ln_mlp_b': small(ks[7], (DEPTH, D_MODEL)),
        'mlp_w1': nrm(ks[8], (DEPTH, D_MODEL, D_FF), D_MODEL),
        'mlp_w2': nrm(ks[9], (DEPTH, D_FF, D_MODEL), D_FF, beta),
        'fox_w_in': nrm(ks[10], (n_a, D_MODEL, 3 * D_MODEL + FOX_HEADS), D_MODEL),
        'fox_b_f': 2.0 + small(ks[11], (n_a, FOX_HEADS), 0.1),
        'fox_w_o': nrm(ks[12], (n_a, D_MODEL, D_MODEL), D_MODEL, beta),
        'ssm_w_in': nrm(ks[13], (n_b, D_MODEL, SSM_IN_DIM), D_MODEL),
        'ssm_conv_w': nrm(ks[14], (n_b, SSM_CONV, SSM_CONV_DIM), SSM_CONV),
        'ssm_conv_b': small(ks[15], (n_b, SSM_CONV_DIM)),
        'ssm_dt_bias': dt_bias,
        'ssm_a_log': a_log,
        'ssm_d': 1.0 + small(ks[16], (n_b, SSM_HEADS)),
        'ssm_norm_w': 1.0 + small(ks[19], (n_b, SSM_D_INNER)),
        'ssm_w_out': nrm(ks[20], (n_b, SSM_D_INNER, D_MODEL), SSM_D_INNER, beta),
    }


def _fwd_reference(x, c, ada_w, ada_b, ln_mix_g, ln_mix_b, ln_mlp_g, ln_mlp_b, mlp_w1, mlp_w2,
              fox_w_in, fox_b_f, fox_w_o, ssm_w_in, ssm_conv_w, ssm_conv_b, ssm_dt_bias,
              ssm_a_log, ssm_d, ssm_norm_w, ssm_w_out):
    alpha = (2.0 * DEPTH) ** 0.25
    cond = jax.nn.silu(c)
    for i in range(DEPTH):
        mod = (cond @ ada_w[i] + ada_b[i])[:, None, :]
        sh_a, sc_a, g_a, sh_m, sc_m, g_m = jnp.split(mod, 6, axis=-1)
        u = x * (1.0 + sc_a) + sh_a
        j = i // N_MIXERS
        if i % N_MIXERS == 0:
            y = fox_attention(u, fox_w_in[j], fox_b_f[j], fox_w_o[j])
        else:
            y = ssd_mixer(u, ssm_w_in[j], ssm_conv_w[j], ssm_conv_b[j], ssm_dt_bias[j],
                          ssm_a_log[j], ssm_d[j], ssm_norm_w[j], ssm_w_out[j])
        x = layer_norm(alpha * x + (1.0 + g_a) * y, ln_mix_g[i], ln_mix_b[i])
        u = x * (1.0 + sc_m) + sh_m
        y = sq_relu_mlp(u, mlp_w1[i], mlp_w2[i])
        x = layer_norm(alpha * x + (1.0 + g_m) * y, ln_mlp_g[i], ln_mlp_b[i])
    return x


import jax as _jax
import jax.numpy as _jnp

TWIN_FORMAT = 'train_step'
FWD_PARAMS = ['x', 'c', 'ada_w', 'ada_b', 'ln_mix_g', 'ln_mix_b', 'ln_mlp_g', 'ln_mlp_b', 'mlp_w1', 'mlp_w2', 'fox_w_in', 'fox_b_f', 'fox_w_o', 'ssm_w_in', 'ssm_conv_w', 'ssm_conv_b', 'ssm_dt_bias', 'ssm_a_log', 'ssm_d', 'ssm_norm_w', 'ssm_w_out']
TWIN_WEIGHTS = ['ada_w', 'ada_b', 'ln_mix_g', 'ln_mix_b', 'ln_mlp_g', 'ln_mlp_b', 'mlp_w1', 'mlp_w2', 'fox_w_in', 'fox_b_f', 'fox_w_o', 'ssm_w_in', 'ssm_conv_w', 'ssm_conv_b', 'ssm_dt_bias', 'ssm_a_log', 'ssm_d', 'ssm_norm_w', 'ssm_w_out']
TWIN_DIFF_INPUT = 'x'
TWIN_INPUTS = ['x', 'c', 'ada_w', 'ada_b', 'ln_mix_g', 'ln_mix_b', 'ln_mlp_g', 'ln_mlp_b', 'mlp_w1', 'mlp_w2', 'fox_w_in', 'fox_b_f', 'fox_w_o', 'ssm_w_in', 'ssm_conv_w', 'ssm_conv_b', 'ssm_dt_bias', 'ssm_a_log', 'ssm_d', 'ssm_norm_w', 'ssm_w_out', 'loss_target', 'm_ada_w', 'm_ada_b', 'm_ln_mix_g', 'm_ln_mix_b', 'm_ln_mlp_g', 'm_ln_mlp_b', 'm_mlp_w1', 'm_mlp_w2', 'm_fox_w_in', 'm_fox_b_f', 'm_fox_w_o', 'm_ssm_w_in', 'm_ssm_conv_w', 'm_ssm_conv_b', 'm_ssm_dt_bias', 'm_ssm_a_log', 'm_ssm_d', 'm_ssm_norm_w', 'm_ssm_w_out', 'v_ada_w', 'v_ada_b', 'v_ln_mix_g', 'v_ln_mix_b', 'v_ln_mlp_g', 'v_ln_mlp_b', 'v_mlp_w1', 'v_mlp_w2', 'v_fox_w_in', 'v_fox_b_f', 'v_fox_w_o', 'v_ssm_w_in', 'v_ssm_conv_w', 'v_ssm_conv_b', 'v_ssm_dt_bias', 'v_ssm_a_log', 'v_ssm_d', 'v_ssm_norm_w', 'v_ssm_w_out']
TWIN_OUTPUTS = ['loss', 'grad_x', 'grad_ada_w', 'grad_ada_b', 'grad_ln_mix_g', 'grad_ln_mix_b', 'grad_ln_mlp_g', 'grad_ln_mlp_b', 'grad_mlp_w1', 'grad_mlp_w2', 'grad_fox_w_in', 'grad_fox_b_f', 'grad_fox_w_o', 'grad_ssm_w_in', 'grad_ssm_conv_w', 'grad_ssm_conv_b', 'grad_ssm_dt_bias', 'grad_ssm_a_log', 'grad_ssm_d', 'grad_ssm_norm_w', 'grad_ssm_w_out', 'delta_ada_w', 'delta_ada_b', 'delta_ln_mix_g', 'delta_ln_mix_b', 'delta_ln_mlp_g', 'delta_ln_mlp_b', 'delta_mlp_w1', 'delta_mlp_w2', 'delta_fox_w_in', 'delta_fox_b_f', 'delta_fox_w_o', 'delta_ssm_w_in', 'delta_ssm_conv_w', 'delta_ssm_conv_b', 'delta_ssm_dt_bias', 'delta_ssm_a_log', 'delta_ssm_d', 'delta_ssm_norm_w', 'delta_ssm_w_out', 'new_m_ada_w', 'new_m_ada_b', 'new_m_ln_mix_g', 'new_m_ln_mix_b', 'new_m_ln_mlp_g', 'new_m_ln_mlp_b', 'new_m_mlp_w1', 'new_m_mlp_w2', 'new_m_fox_w_in', 'new_m_fox_b_f', 'new_m_fox_w_o', 'new_m_ssm_w_in', 'new_m_ssm_conv_w', 'new_m_ssm_conv_b', 'new_m_ssm_dt_bias', 'new_m_ssm_a_log', 'new_m_ssm_d', 'new_m_ssm_norm_w', 'new_m_ssm_w_out', 'new_v_ada_w', 'new_v_ada_b', 'new_v_ln_mix_g', 'new_v_ln_mix_b', 'new_v_ln_mlp_g', 'new_v_ln_mlp_b', 'new_v_mlp_w1', 'new_v_mlp_w2', 'new_v_fox_w_in', 'new_v_fox_b_f', 'new_v_fox_w_o', 'new_v_ssm_w_in', 'new_v_ssm_conv_w', 'new_v_ssm_conv_b', 'new_v_ssm_dt_bias', 'new_v_ssm_a_log', 'new_v_ssm_d', 'new_v_ssm_norm_w', 'new_v_ssm_w_out']
TWIN_LEAF_KINDS = {'loss': 'loss', 'grad_x': 'grad_x', 'grad_ada_w': 'grad_w', 'grad_ada_b': 'grad_w', 'grad_ln_mix_g': 'grad_w', 'grad_ln_mix_b': 'grad_w', 'grad_ln_mlp_g': 'grad_w', 'grad_ln_mlp_b': 'grad_w', 'grad_mlp_w1': 'grad_w', 'grad_mlp_w2': 'grad_w', 'grad_fox_w_in': 'grad_w', 'grad_fox_b_f': 'grad_w', 'grad_fox_w_o': 'grad_w', 'grad_ssm_w_in': 'grad_w', 'grad_ssm_conv_w': 'grad_w', 'grad_ssm_conv_b': 'grad_w', 'grad_ssm_dt_bias': 'grad_w', 'grad_ssm_a_log': 'grad_w', 'grad_ssm_d': 'grad_w', 'grad_ssm_norm_w': 'grad_w', 'grad_ssm_w_out': 'grad_w', 'delta_ada_w': 'delta_w', 'delta_ada_b': 'delta_w', 'delta_ln_mix_g': 'delta_w', 'delta_ln_mix_b': 'delta_w', 'delta_ln_mlp_g': 'delta_w', 'delta_ln_mlp_b': 'delta_w', 'delta_mlp_w1': 'delta_w', 'delta_mlp_w2': 'delta_w', 'delta_fox_w_in': 'delta_w', 'delta_fox_b_f': 'delta_w', 'delta_fox_w_o': 'delta_w', 'delta_ssm_w_in': 'delta_w', 'delta_ssm_conv_w': 'delta_w', 'delta_ssm_conv_b': 'delta_w', 'delta_ssm_dt_bias': 'delta_w', 'delta_ssm_a_log': 'delta_w', 'delta_ssm_d': 'delta_w', 'delta_ssm_norm_w': 'delta_w', 'delta_ssm_w_out': 'delta_w', 'new_m_ada_w': 'new_m', 'new_m_ada_b': 'new_m', 'new_m_ln_mix_g': 'new_m', 'new_m_ln_mix_b': 'new_m', 'new_m_ln_mlp_g': 'new_m', 'new_m_ln_mlp_b': 'new_m', 'new_m_mlp_w1': 'new_m', 'new_m_mlp_w2': 'new_m', 'new_m_fox_w_in': 'new_m', 'new_m_fox_b_f': 'new_m', 'new_m_fox_w_o': 'new_m', 'new_m_ssm_w_in': 'new_m', 'new_m_ssm_conv_w': 'new_m', 'new_m_ssm_conv_b': 'new_m', 'new_m_ssm_dt_bias': 'new_m', 'new_m_ssm_a_log': 'new_m', 'new_m_ssm_d': 'new_m', 'new_m_ssm_norm_w': 'new_m', 'new_m_ssm_w_out': 'new_m', 'new_v_ada_w': 'new_v', 'new_v_ada_b': 'new_v', 'new_v_ln_mix_g': 'new_v', 'new_v_ln_mix_b': 'new_v', 'new_v_ln_mlp_g': 'new_v', 'new_v_ln_mlp_b': 'new_v', 'new_v_mlp_w1': 'new_v', 'new_v_mlp_w2': 'new_v', 'new_v_fox_w_in': 'new_v', 'new_v_fox_b_f': 'new_v', 'new_v_fox_w_o': 'new_v', 'new_v_ssm_w_in': 'new_v', 'new_v_ssm_conv_w': 'new_v', 'new_v_ssm_conv_b': 'new_v', 'new_v_ssm_dt_bias': 'new_v', 'new_v_ssm_a_log': 'new_v', 'new_v_ssm_d': 'new_v', 'new_v_ssm_norm_w': 'new_v', 'new_v_ssm_w_out': 'new_v'}


def _forward(args):
    return _fwd_reference(*[args[k] for k in FWD_PARAMS])


def _output_shape():
    def fwd():
        inp = _fwd_setup_inputs(0)
        return _fwd_reference(*[inp[k] for k in FWD_PARAMS])
    out = _jax.eval_shape(fwd)
    return out.shape, out.dtype

N_MICROBATCH = 1
ADAM_LR = 0.001
ADAM_B1 = 0.9
ADAM_B2 = 0.999
ADAM_EPS = 1e-08
ADAM_WD = 0.01
ADAM_STEP = 10
PER_EXAMPLE_BATCH_AXIS = {'x': 0, 'c': 0, 'loss_target': 0}
SHARED_INPUTS = []
_WEIGHT_DTYPES = {'ada_w': _jnp.float32, 'ada_b': _jnp.float32, 'ln_mix_g': _jnp.float32, 'ln_mix_b': _jnp.float32, 'ln_mlp_g': _jnp.float32, 'ln_mlp_b': _jnp.float32, 'mlp_w1': _jnp.float32, 'mlp_w2': _jnp.float32, 'fox_w_in': _jnp.float32, 'fox_b_f': _jnp.float32, 'fox_w_o': _jnp.float32, 'ssm_w_in': _jnp.float32, 'ssm_conv_w': _jnp.float32, 'ssm_conv_b': _jnp.float32, 'ssm_dt_bias': _jnp.float32, 'ssm_a_log': _jnp.float32, 'ssm_d': _jnp.float32, 'ssm_norm_w': _jnp.float32, 'ssm_w_out': _jnp.float32}
MOMENT_SCALE = {'ada_w': 1.117075e-01, 'ada_b': 2.937756e-01, 'ln_mix_g': 2.968893e+00, 'ln_mix_b': 1.656973e+00, 'ln_mlp_g': 9.097002e+01, 'ln_mlp_b': 2.112006e+01, 'mlp_w1': 8.813181e-02, 'mlp_w2': 5.085335e-01, 'fox_w_in': 5.036582e-02, 'fox_b_f': 2.462956e-01, 'fox_w_o': 1.192920e-01, 'ssm_w_in': 7.475190e-02, 'ssm_conv_w': 6.607876e-02, 'ssm_conv_b': 1.146597e-01, 'ssm_dt_bias': 1.425299e-01, 'ssm_a_log': 2.157842e-01, 'ssm_d': 3.898535e-01, 'ssm_norm_w': 9.995516e-02, 'ssm_w_out': 2.903554e-01}


def _to_microbatches(a, axis):
    t = _jnp.moveaxis(a, axis, 0)
    t = t.reshape((N_MICROBATCH, t.shape[0] // N_MICROBATCH) + t.shape[1:])
    return _jnp.moveaxis(t, 1, axis + 1)


def setup_inputs(seed: int = 0) -> dict:
    inp = _fwd_setup_inputs(seed)
    key = _jax.random.fold_in(_jax.random.key(seed), 7919)
    shape, _ = _output_shape()
    out = dict(inp)
    out["loss_target"] = _jax.random.normal(_jax.random.fold_in(key, 0), shape, _jnp.float32)
    for i, name in enumerate(TWIN_WEIGHTS):
        w = inp[name].astype(_jnp.float32)
        if MOMENT_SCALE is None:
            s = _jnp.sqrt(_jnp.mean(_jnp.square(w)) + 1e-30)
        else:
            s = MOMENT_SCALE[name]
        km, kv = _jax.random.split(_jax.random.fold_in(key, i + 1))
        out[name] = w
        out["m_" + name] = s * _jax.random.normal(km, w.shape, _jnp.float32)
        out["v_" + name] = (s * s) * _jax.random.uniform(kv, w.shape, _jnp.float32, 0.5, 1.5)
    if N_MICROBATCH > 1:
        for name, axis in PER_EXAMPLE_BATCH_AXIS.items():
            out[name] = _to_microbatches(out[name], axis)
    return {'x': out['x'], 'c': out['c'], 'ada_w': out['ada_w'], 'ada_b': out['ada_b'], 'ln_mix_g': out['ln_mix_g'], 'ln_mix_b': out['ln_mix_b'], 'ln_mlp_g': out['ln_mlp_g'], 'ln_mlp_b': out['ln_mlp_b'], 'mlp_w1': out['mlp_w1'], 'mlp_w2': out['mlp_w2'], 'fox_w_in': out['fox_w_in'], 'fox_b_f': out['fox_b_f'], 'fox_w_o': out['fox_w_o'], 'ssm_w_in': out['ssm_w_in'], 'ssm_conv_w': out['ssm_conv_w'], 'ssm_conv_b': out['ssm_conv_b'], 'ssm_dt_bias': out['ssm_dt_bias'], 'ssm_a_log': out['ssm_a_log'], 'ssm_d': out['ssm_d'], 'ssm_norm_w': out['ssm_norm_w'], 'ssm_w_out': out['ssm_w_out'], 'loss_target': out['loss_target'], 'm_ada_w': out['m_ada_w'], 'm_ada_b': out['m_ada_b'], 'm_ln_mix_g': out['m_ln_mix_g'], 'm_ln_mix_b': out['m_ln_mix_b'], 'm_ln_mlp_g': out['m_ln_mlp_g'], 'm_ln_mlp_b': out['m_ln_mlp_b'], 'm_mlp_w1': out['m_mlp_w1'], 'm_mlp_w2': out['m_mlp_w2'], 'm_fox_w_in': out['m_fox_w_in'], 'm_fox_b_f': out['m_fox_b_f'], 'm_fox_w_o': out['m_fox_w_o'], 'm_ssm_w_in': out['m_ssm_w_in'], 'm_ssm_conv_w': out['m_ssm_conv_w'], 'm_ssm_conv_b': out['m_ssm_conv_b'], 'm_ssm_dt_bias': out['m_ssm_dt_bias'], 'm_ssm_a_log': out['m_ssm_a_log'], 'm_ssm_d': out['m_ssm_d'], 'm_ssm_norm_w': out['m_ssm_norm_w'], 'm_ssm_w_out': out['m_ssm_w_out'], 'v_ada_w': out['v_ada_w'], 'v_ada_b': out['v_ada_b'], 'v_ln_mix_g': out['v_ln_mix_g'], 'v_ln_mix_b': out['v_ln_mix_b'], 'v_ln_mlp_g': out['v_ln_mlp_g'], 'v_ln_mlp_b': out['v_ln_mlp_b'], 'v_mlp_w1': out['v_mlp_w1'], 'v_mlp_w2': out['v_mlp_w2'], 'v_fox_w_in': out['v_fox_w_in'], 'v_fox_b_f': out['v_fox_b_f'], 'v_fox_w_o': out['v_fox_w_o'], 'v_ssm_w_in': out['v_ssm_w_in'], 'v_ssm_conv_w': out['v_ssm_conv_w'], 'v_ssm_conv_b': out['v_ssm_conv_b'], 'v_ssm_dt_bias': out['v_ssm_dt_bias'], 'v_ssm_a_log': out['v_ssm_a_log'], 'v_ssm_d': out['v_ssm_d'], 'v_ssm_norm_w': out['v_ssm_norm_w'], 'v_ssm_w_out': out['v_ssm_w_out']}


def _loss(weights, diff, rest, loss_target):
    with _jax.named_scope("forward"):
        args = {**rest, TWIN_DIFF_INPUT: diff, **{k: w.astype(_WEIGHT_DTYPES[k]) for k, w in weights.items()}}
        y = _forward(args)
    with _jax.named_scope("loss_head"):
        err = _jnp.square(y.astype(_jnp.float32) - loss_target)
        return 0.5 * _jnp.sum(_jnp.mean(err, axis=-1)) if err.ndim else 0.5 * err


def _adamw(w, g, m, v):
    m = ADAM_B1 * m + (1.0 - ADAM_B1) * g
    v = ADAM_B2 * v + (1.0 - ADAM_B2) * _jnp.square(g)
    m_hat = m / (1.0 - ADAM_B1 ** ADAM_STEP)
    v_hat = v / (1.0 - ADAM_B2 ** ADAM_STEP)
    delta = -ADAM_LR * (m_hat / (_jnp.sqrt(v_hat) + ADAM_EPS) + ADAM_WD * w)
    return delta, m, v


def reference(x, c, ada_w, ada_b, ln_mix_g, ln_mix_b, ln_mlp_g, ln_mlp_b, mlp_w1, mlp_w2, fox_w_in, fox_b_f, fox_w_o, ssm_w_in, ssm_conv_w, ssm_conv_b, ssm_dt_bias, ssm_a_log, ssm_d, ssm_norm_w, ssm_w_out, loss_target, m_ada_w, m_ada_b, m_ln_mix_g, m_ln_mix_b, m_ln_mlp_g, m_ln_mlp_b, m_mlp_w1, m_mlp_w2, m_fox_w_in, m_fox_b_f, m_fox_w_o, m_ssm_w_in, m_ssm_conv_w, m_ssm_conv_b, m_ssm_dt_bias, m_ssm_a_log, m_ssm_d, m_ssm_norm_w, m_ssm_w_out, v_ada_w, v_ada_b, v_ln_mix_g, v_ln_mix_b, v_ln_mlp_g, v_ln_mlp_b, v_mlp_w1, v_mlp_w2, v_fox_w_in, v_fox_b_f, v_fox_w_o, v_ssm_w_in, v_ssm_conv_w, v_ssm_conv_b, v_ssm_dt_bias, v_ssm_a_log, v_ssm_d, v_ssm_norm_w, v_ssm_w_out):
    given = dict(x=x, c=c, ada_w=ada_w, ada_b=ada_b, ln_mix_g=ln_mix_g, ln_mix_b=ln_mix_b, ln_mlp_g=ln_mlp_g, ln_mlp_b=ln_mlp_b, mlp_w1=mlp_w1, mlp_w2=mlp_w2, fox_w_in=fox_w_in, fox_b_f=fox_b_f, fox_w_o=fox_w_o, ssm_w_in=ssm_w_in, ssm_conv_w=ssm_conv_w, ssm_conv_b=ssm_conv_b, ssm_dt_bias=ssm_dt_bias, ssm_a_log=ssm_a_log, ssm_d=ssm_d, ssm_norm_w=ssm_norm_w, ssm_w_out=ssm_w_out, loss_target=loss_target, m_ada_w=m_ada_w, m_ada_b=m_ada_b, m_ln_mix_g=m_ln_mix_g, m_ln_mix_b=m_ln_mix_b, m_ln_mlp_g=m_ln_mlp_g, m_ln_mlp_b=m_ln_mlp_b, m_mlp_w1=m_mlp_w1, m_mlp_w2=m_mlp_w2, m_fox_w_in=m_fox_w_in, m_fox_b_f=m_fox_b_f, m_fox_w_o=m_fox_w_o, m_ssm_w_in=m_ssm_w_in, m_ssm_conv_w=m_ssm_conv_w, m_ssm_conv_b=m_ssm_conv_b, m_ssm_dt_bias=m_ssm_dt_bias, m_ssm_a_log=m_ssm_a_log, m_ssm_d=m_ssm_d, m_ssm_norm_w=m_ssm_norm_w, m_ssm_w_out=m_ssm_w_out, v_ada_w=v_ada_w, v_ada_b=v_ada_b, v_ln_mix_g=v_ln_mix_g, v_ln_mix_b=v_ln_mix_b, v_ln_mlp_g=v_ln_mlp_g, v_ln_mlp_b=v_ln_mlp_b, v_mlp_w1=v_mlp_w1, v_mlp_w2=v_mlp_w2, v_fox_w_in=v_fox_w_in, v_fox_b_f=v_fox_b_f, v_fox_w_o=v_fox_w_o, v_ssm_w_in=v_ssm_w_in, v_ssm_conv_w=v_ssm_conv_w, v_ssm_conv_b=v_ssm_conv_b, v_ssm_dt_bias=v_ssm_dt_bias, v_ssm_a_log=v_ssm_a_log, v_ssm_d=v_ssm_d, v_ssm_norm_w=v_ssm_norm_w, v_ssm_w_out=v_ssm_w_out)
    weights = {n: given[n] for n in TWIN_WEIGHTS}
    shared = {n: given[n] for n in SHARED_INPUTS}
    per_example = {n: given[n] for n in ['x', 'c']}
    grad_fn = _jax.value_and_grad(_loss, argnums=(0, 1))

    def one_microbatch(ex, loss_target):
        ex = dict(ex)
        diff = ex.pop(TWIN_DIFF_INPUT)
        return grad_fn(weights, diff, {**shared, **ex}, loss_target)

    if N_MICROBATCH == 1:
        loss, (grad_w, grad_x) = one_microbatch(per_example, given["loss_target"])
    else:
        def body(carry, xs):
            loss_sum, grad_sum = carry
            l_k, (gw_k, gx_k) = one_microbatch(xs[0], xs[1])
            with _jax.named_scope("update"):
                return (loss_sum + l_k, _jax.tree.map(_jnp.add, grad_sum, gw_k)), gx_k

        init = (_jnp.zeros((), _jnp.float32), _jax.tree.map(_jnp.zeros_like, weights))
        (loss, grad_w), grad_x = _jax.lax.scan(body, init, (per_example, given["loss_target"]))
    with _jax.named_scope("update"):
        delta_w, new_m, new_v = {}, {}, {}
        for n in TWIN_WEIGHTS:
            delta_w[n], new_m[n], new_v[n] = _adamw(weights[n], grad_w[n], given["m_" + n], given["v_" + n])
    return (loss, grad_x, *[grad_w[n] for n in TWIN_WEIGHTS], *[delta_w[n] for n in TWIN_WEIGHTS],
            *[new_m[n] for n in TWIN_WEIGHTS], *[new_v[n] for n in TWIN_WEIGHTS])
```

```python
import functools

import jax
import jax.numpy as jnp
from jax import lax
from jax.experimental import pallas as pl
from jax.experimental.pallas import tpu as pltpu

F32, BF16 = jnp.float32, jnp.bfloat16
MESH = pl.DeviceIdType.MESH
HBM_SPEC = pl.BlockSpec(memory_space=pltpu.HBM)

VMEM_LIMIT_BYTES = 52 * 2**20
LANES = 128

FOX_HEADS, HEAD_DIM = 16, 64
SSM_HEADS, SSM_GROUPS, SSM_STATE, SSM_CHUNK, SSM_CONV = 32, 8, 128, 128, 4
SSM_GROUP_WIDTH = 256
LN_EPS, RMS_EPS = 1e-5, 1e-5
DEPTH = 2
ALPHA = (2.0 * DEPTH) ** 0.25
ADAM_LR, ADAM_B1, ADAM_B2, ADAM_EPS, ADAM_WD, ADAM_STEP = 0.001, 0.9, 0.999, 1e-08, 0.01, 10

ATT_TILE = 512
ROW_TILE = 256
SCAN_TILE = 512
MM_TM, MM_TN, MM_TK = 512, 512, 1024

NT_DIMS = (((1,), (1,)), ((), ()))
TN_DIMS = (((0,), (0,)), ((), ()))
NN_DIMS = (((1,), (0,)), ((), ()))


def _pcall(body, *, name, out_shape, grid=(), in_specs=None, out_specs=None, scratch=(), sem=None):
    params = dict(vmem_limit_bytes=VMEM_LIMIT_BYTES)
    if sem is not None:
        params["dimension_semantics"] = sem
    kwargs = {}
    if in_specs is not None:
        kwargs["in_specs"] = in_specs
    if out_specs is not None:
        kwargs["out_specs"] = out_specs
    return pl.pallas_call(body, out_shape=out_shape, grid=grid, scratch_shapes=scratch, name=name,
                          compiler_params=pltpu.CompilerParams(**params), **kwargs)


def _sds(shape, dtype):
    return jax.ShapeDtypeStruct(tuple(shape), dtype)


def _dot(a, b, dims=NN_DIMS):
    return lax.dot_general(a, b, dims, preferred_element_type=F32)


def _sigmoid(x):
    return 1.0 / (1.0 + jnp.exp(-x))


def _silu(x):
    return x * _sigmoid(x)


def _dsilu(x):
    s = _sigmoid(x)
    return s * (1.0 + x * (1.0 - s))


def _dot_split(x, m16, dims=NN_DIMS):
    hi = x.astype(BF16)
    r1 = x - hi.astype(F32)
    mid = r1.astype(BF16)
    lo = (r1 - mid.astype(F32)).astype(BF16)
    return _dot(hi, m16, dims) + _dot(mid, m16, dims) + _dot(lo, m16, dims)


def _mm(a, b, dims, outs, *, name, tm=MM_TM, tn=MM_TN, tk=MM_TK, epi=None, extra=()):
    if dims == "nn":
        (m, k), (k2, n) = a.shape, b.shape
    elif dims == "nt":
        (m, k), (n, k2) = a.shape, b.shape
    else:
        (k, m), (k2, n) = a.shape, b.shape
    assert k == k2, (a.shape, b.shape, dims)
    tm, tn, tk = min(tm, m), min(tn, n), min(tk, k)
    assert m % tm == 0 and n % tn == 0 and k % tk == 0, (m, n, k, tm, tn, tk)
    nk = k // tk
    dn = {"nn": NN_DIMS, "nt": NT_DIMS, "tn": TN_DIMS}[dims]
    n_extra, n_out = len(extra), len(outs)
    if epi is None:
        epi = lambda acc: (acc,) * n_out

    def body(a_ref, b_ref, *rest):
        extra_refs, out_refs, acc_ref = rest[:n_extra], rest[n_extra:n_extra + n_out], rest[-1]
        kk = pl.program_id(2)

        @pl.when(kk == 0)
        def _():
            acc_ref[...] = jnp.zeros_like(acc_ref)

        acc_ref[...] += _dot(a_ref[...].astype(BF16), b_ref[...].astype(BF16), dn)

        @pl.when(kk == nk - 1)
        def _():
            res = epi(acc_ref[...], *[e[...] for e in extra_refs])
            for o, r in zip(out_refs, res):
                o[...] = r.astype(o.dtype)

    if dims == "tn":
        a_spec = pl.BlockSpec((tk, tm), lambda i, j, kk: (kk, i))
    else:
        a_spec = pl.BlockSpec((tm, tk), lambda i, j, kk: (i, kk))
    if dims == "nt":
        b_spec = pl.BlockSpec((tn, tk), lambda i, j, kk: (j, kk))
    else:
        b_spec = pl.BlockSpec((tk, tn), lambda i, j, kk: (kk, j))
    o_spec = pl.BlockSpec((tm, tn), lambda i, j, kk: (i, j))
    res = _pcall(body, name=name, grid=(m // tm, n // tn, nk),
                 in_specs=[a_spec, b_spec] + [o_spec] * n_extra,
                 out_specs=[o_spec] * n_out,
                 out_shape=[_sds((m, n), d) for d in outs],
                 scratch=[pltpu.VMEM((tm, tn), F32)],
                 sem=("parallel", "parallel", "arbitrary"))(a, b, *extra)
    return res[0] if n_out == 1 else res


def _rowwise(fn, rows, consts, row_outs, acc_outs, *, name, tm=ROW_TILE):
    s = rows[0].shape[0]
    tm = min(tm, s)
    assert s % tm == 0
    n_in, n_o = len(rows) + len(consts), len(row_outs)

    def body(*refs):
        ins, outs = refs[:n_in], refs[n_in:]
        res = fn(*[r[...] for r in ins])
        if not isinstance(res, (tuple, list)):
            res = (res,)
        for o, val in zip(outs[:n_o], res[:n_o]):
            o[...] = val.astype(o.dtype)
        if acc_outs:
            @pl.when(pl.program_id(0) == 0)
            def _():
                for o in outs[n_o:]:
                    o[...] = jnp.zeros_like(o)
            for o, val in zip(outs[n_o:], res[n_o:]):
                o[...] += val

    in_specs = [pl.BlockSpec((tm, r.shape[1]), lambda i: (i, 0)) for r in rows]
    in_specs += [pl.BlockSpec(c.shape, functools.partial(lambda nd, i: (0,) * nd, c.ndim)) for c in consts]
    out_specs = [pl.BlockSpec((tm, c), lambda i: (i, 0)) for c, _ in row_outs]
    out_specs += [pl.BlockSpec(tuple(sh), lambda i: (0, 0)) for sh in acc_outs]
    out_shape = [_sds((s, c), d) for c, d in row_outs] + [_sds(sh, F32) for sh in acc_outs]
    res = _pcall(body, name=name, grid=(s // tm,), in_specs=in_specs, out_specs=out_specs,
                 out_shape=out_shape, sem=("arbitrary",))(*rows, *consts)
    return res


def _colsum(x):
    return jnp.sum(x, axis=0, keepdims=True)


def _ln_stats(r):
    mu = jnp.mean(r, axis=-1, keepdims=True)
    xc = r - mu
    var = jnp.mean(xc * xc, axis=-1, keepdims=True)
    rstd = lax.rsqrt(var + LN_EPS)
    return xc * rstd, rstd


def _ln_bwd(dy, xhat, rstd, gamma):
    dyg = dy * gamma
    m1 = jnp.mean(dyg, axis=-1, keepdims=True)
    m2 = jnp.mean(dyg * xhat, axis=-1, keepdims=True)
    return rstd * (dyg - m1 - xhat * m2)


def _modulate_in(x, mod, name):
    def fn(xv, m):
        return (xv * (1.0 + m[1:2]) + m[0:1],)
    return _rowwise(fn, [x], [mod], [(x.shape[1], BF16)], [], name=name)[0]


def _res_ln_mod(x, y, mod, g, b, name):
    d = x.shape[1]

    def fn(xv, yv, m, gv, bv):
        r = ALPHA * xv + (1.0 + m[2:3]) * yv
        xhat, _ = _ln_stats(r)
        x1 = xhat * gv + bv
        u2 = x1 * (1.0 + m[4:5]) + m[3:4]
        return r, x1, u2
    return _rowwise(fn, [x, y], [mod, g, b], [(d, F32), (d, F32), (d, BF16)], [], name=name)


def _res_ln(x, y, mod, g, b, name):
    d = x.shape[1]

    def fn(xv, yv, m, gv, bv):
        r = ALPHA * xv + (1.0 + m[5:6]) * yv
        xhat, _ = _ln_stats(r)
        return r, xhat * gv + bv
    return _rowwise(fn, [x, y], [mod, g, b], [(d, F32), (d, F32)], [], name=name)


def _loss_grad(xf, target, name):
    d = xf.shape[1]

    def fn(xv, tv):
        e = xv - tv
        return e * (1.0 / d), _colsum(e * e)
    return _rowwise(fn, [xf, target], [], [(d, F32)], [(1, d)], name=name)


def _ln2_bwd(dx, r2, y2, mod, g, name):
    d = dx.shape[1]

    def fn(dxv, rv, yv, m, gv):
        xhat, rstd = _ln_stats(rv)
        dr = _ln_bwd(dxv, xhat, rstd, gv)
        return (dr * (1.0 + m[5:6]), ALPHA * dr,
                _colsum(dxv * xhat), _colsum(dxv), _colsum(dr * yv))
    return _rowwise(fn, [dx, r2, y2], [mod, g], [(d, BF16), (d, F32)], [(1, d)] * 3, name=name)


def _ln1_bwd(du2, dres, r, y, mod, g, b, name):
    d = du2.shape[1]

    def fn(duv, drv, rv, yv, m, gv, bv):
        xhat, rstd = _ln_stats(rv)
        x1 = xhat * gv + bv
        dx1 = duv * (1.0 + m[4:5]) + drv
        dr = _ln_bwd(dx1, xhat, rstd, gv)
        return (dr * (1.0 + m[2:3]), ALPHA * dr,
                _colsum(duv * x1), _colsum(duv), _colsum(dx1 * xhat), _colsum(dx1), _colsum(dr * yv))
    return _rowwise(fn, [du2, dres, r, y], [mod, g, b], [(d, BF16), (d, F32)], [(1, d)] * 5, name=name)


def _mod_in_bwd(du, dres, x, mod, name):
    d = du.shape[1]

    def fn(duv, drv, xv, m):
        return duv * (1.0 + m[1:2]) + drv, _colsum(duv * xv), _colsum(duv)
    return _rowwise(fn, [du, dres, x], [mod], [(d, F32)], [(1, d)] * 2, name=name)


def _fox_gate(fraw, b_pad, name):
    s = fraw.shape[0]
    tb = min(SCAN_TILE, s)

    def body(f_ref, b_ref, cum_ref, carry):
        @pl.when(pl.program_id(0) == 0)
        def _():
            carry[...] = jnp.zeros_like(carry)
        z = f_ref[...] + b_ref[...]
        lf = jnp.minimum(z, 0.0) - jnp.log(1.0 + jnp.exp(-jnp.abs(z)))
        lane = lax.broadcasted_iota(jnp.int32, (tb, LANES), 1)
        row = lax.broadcasted_iota(jnp.int32, (tb, LANES), 0)
        c = jnp.where(lane < FOX_HEADS, lf, 0.0)
        sh = 1
        while sh < tb:
            c = c + jnp.where(row >= sh, pltpu.roll(c, sh, 0), 0.0)
            sh *= 2
        c = c + carry[0:1, :]
        cum_ref[...] = c
        carry[0:1, :] = c[tb - 1:tb, :]

    return _pcall(body, name=name, grid=(s // tb,),
                  in_specs=[pl.BlockSpec((tb, LANES), lambda i: (i, 0)), pl.BlockSpec((1, LANES), lambda i: (0, 0))],
                  out_specs=pl.BlockSpec((tb, LANES), lambda i: (i, 0)),
                  out_shape=_sds((s, LANES), F32), scratch=[pltpu.VMEM((8, LANES), F32)],
                  sem=("arbitrary",))(fraw, b_pad)


def _fox_gate_bwd(drow, dcol, fraw, b_pad, name):
    s = fraw.shape[0]
    tb = min(SCAN_TILE, s)
    n = s // tb

    def body(dr_ref, dc_ref, f_ref, b_ref, df_ref, db_ref, carry):
        @pl.when(pl.program_id(0) == 0)
        def _():
            carry[...] = jnp.zeros_like(carry)
            db_ref[...] = jnp.zeros_like(db_ref)
        row = lax.broadcasted_iota(jnp.int32, (tb, LANES), 0)
        c = dr_ref[...] + dc_ref[...]
        sh = 1
        while sh < tb:
            c = c + jnp.where(row + sh < tb, pltpu.roll(c, tb - sh, 0), 0.0)
            sh *= 2
        c = c + carry[0:1, :]
        carry[0:1, :] = c[0:1, :]
        z = f_ref[...] + b_ref[...]
        df = c * (1.0 / (1.0 + jnp.exp(z)))
        df_ref[...] = df.astype(df_ref.dtype)
        db_ref[...] += _colsum(df)

    rev = lambda i: (n - 1 - i, 0)
    return _pcall(body, name=name, grid=(n,),
                  in_specs=[pl.BlockSpec((tb, LANES), rev)] * 3 + [pl.BlockSpec((1, LANES), lambda i: (0, 0))],
                  out_specs=[pl.BlockSpec((tb, LANES), rev), pl.BlockSpec((1, LANES), lambda i: (0, 0))],
                  out_shape=[_sds((s, LANES), BF16), _sds((1, LANES), F32)],
                  scratch=[pltpu.VMEM((8, LANES), F32)], sem=("arbitrary",))(drow, dcol, fraw, b_pad)


def _head_pair_masks(t):
    lane = lax.broadcasted_iota(jnp.int32, (t, LANES), 1)
    return lane < HEAD_DIM


def _flash_fwd(qkv, ck_rows, name):
    s = qkv.shape[0]
    t = min(ATT_TILE, s)
    nq = s // t
    scale = HEAD_DIM ** -0.5
    hp_blocks = FOX_HEADS // 2

    def body(q_ref, k_ref, v_ref, ck_ref, o_ref, lse_ref, acc_ref, m_ref, l_ref):
        qb = pl.program_id(1)
        q2 = q_ref[...]
        first = _head_pair_masks(t)
        zero = jnp.zeros_like(q2)
        qs = (jnp.where(first, q2, zero), jnp.where(first, zero, q2))
        m_ref[...] = jnp.full_like(m_ref, -jnp.inf)
        l_ref[...] = jnp.zeros_like(l_ref)
        acc_ref[...] = jnp.zeros_like(acc_ref)
        rel = (lax.broadcasted_iota(jnp.int32, (t, t), 0) - lax.broadcasted_iota(jnp.int32, (t, t), 1))

        def step(kb, carry):
            off = pl.multiple_of(kb * t, t)
            k2 = k_ref[pl.ds(off, t), :]
            v2 = v_ref[pl.ds(off, t), :]
            ck = ck_ref[kb]
            mask = rel >= (kb - qb) * t
            for j in range(2):
                sc = _dot(qs[j], k2, NT_DIMS) * scale - ck[j:j + 1, :]
                sc = jnp.where(mask, sc, -jnp.inf)
                m_old = m_ref[j]
                m_new = jnp.maximum(m_old, jnp.max(sc, axis=1, keepdims=True))
                p = jnp.exp(sc - m_new)
                a = jnp.exp(m_old - m_new)
                l_ref[j] = a * l_ref[j] + jnp.sum(p, axis=1, keepdims=True)
                acc_ref[j] = a * acc_ref[j] + _dot(p.astype(BF16), v2)
                m_ref[j] = m_new
            return carry

        lax.fori_loop(0, qb + 1, step, 0)
        o_ref[...] = jnp.where(first, acc_ref[0] / l_ref[0], acc_ref[1] / l_ref[1])
        lse_ref[:, 0:1] = m_ref[0] + jnp.log(l_ref[0])
        lse_ref[:, 1:2] = m_ref[1] + jnp.log(l_ref[1])

    return _pcall(
        body, name=name, grid=(hp_blocks, nq),
        in_specs=[pl.BlockSpec((t, LANES), lambda h, i: (i, h)),
                  pl.BlockSpec((s, LANES), lambda h, i: (0, hp_blocks + h)),
                  pl.BlockSpec((s, LANES), lambda h, i: (0, 2 * hp_blocks + h)),
                  pl.BlockSpec((None, nq, 2, t), lambda h, i: (h, 0, 0, 0))],
        out_specs=[pl.BlockSpec((t, LANES), lambda h, i: (i, h)),
                   pl.BlockSpec((None, t, 2), lambda h, i: (h, i, 0))],
        out_shape=[_sds((s, hp_blocks * LANES), F32), _sds((hp_blocks, s, 2), F32)],
        scratch=[pltpu.VMEM((2, t, LANES), F32), pltpu.VMEM((2, t, 1), F32), pltpu.VMEM((2, t, 1), F32)],
        sem=("parallel", "arbitrary"))(qkv, qkv, qkv, ck_rows)


def _flash_dq(qkv, do16, ck_rows, lse_c, dl_c, name):
    s = qkv.shape[0]
    t = min(ATT_TILE, s)
    nq = s // t
    scale = HEAD_DIM ** -0.5
    hp_blocks = FOX_HEADS // 2

    def body(q_ref, do_ref, k_ref, v_ref, ck_ref, lse_ref, dl_ref, dq_ref, drow_ref, acc_ref, row_acc):
        qb = pl.program_id(1)
        q2, do2 = q_ref[...], do_ref[...]
        first = _head_pair_masks(t)
        zero = jnp.zeros_like(q2)
        qs = (jnp.where(first, q2, zero), jnp.where(first, zero, q2))
        dos = (jnp.where(first, do2, zero), jnp.where(first, zero, do2))
        lse, dl = lse_ref[...], dl_ref[...]
        acc_ref[...] = jnp.zeros_like(acc_ref)
        row_acc[...] = jnp.zeros_like(row_acc)
        rel = (lax.broadcasted_iota(jnp.int32, (t, t), 0) - lax.broadcasted_iota(jnp.int32, (t, t), 1))

        def step(kb, carry):
            off = pl.multiple_of(kb * t, t)
            k2 = k_ref[pl.ds(off, t), :]
            v2 = v_ref[pl.ds(off, t), :]
            ck = ck_ref[kb]
            mask = rel >= (kb - qb) * t
            for j in range(2):
                sc = _dot(qs[j], k2, NT_DIMS) * scale - ck[j:j + 1, :] - lse[:, j:j + 1]
                p = jnp.exp(jnp.where(mask, sc, -jnp.inf))
                dp = _dot(dos[j], v2, NT_DIMS)
                ds = p * (dp - dl[:, j:j + 1])
                acc_ref[j] += _dot(ds.astype(BF16), k2)
                row_acc[j] += jnp.sum(ds, axis=1, keepdims=True)
            return carry

        lax.fori_loop(0, qb + 1, step, 0)
        dq_ref[...] = (jnp.where(first, acc_ref[0], acc_ref[1]) * scale).astype(dq_ref.dtype)
        drow_ref[:, 0:1] = row_acc[0]
        drow_ref[:, 1:2] = row_acc[1]

    return _pcall(
        body, name=name, grid=(hp_blocks, nq),
        in_specs=[pl.BlockSpec((t, LANES), lambda h, i: (i, h)),
                  pl.BlockSpec((t, LANES), lambda h, i: (i, h)),
                  pl.BlockSpec((s, LANES), lambda h, i: (0, hp_blocks + h)),
                  pl.BlockSpec((s, LANES), lambda h, i: (0, 2 * hp_blocks + h)),
                  pl.BlockSpec((None, nq, 2, t), lambda h, i: (h, 0, 0, 0)),
                  pl.BlockSpec((None, t, 2), lambda h, i: (h, i, 0)),
                  pl.BlockSpec((None, t, 2), lambda h, i: (h, i, 0))],
        out_specs=[pl.BlockSpec((t, LANES), lambda h, i: (i, h)),
                   pl.BlockSpec((None, t, 2), lambda h, i: (h, i, 0))],
        out_shape=[_sds((s, hp_blocks * LANES), BF16), _sds((hp_blocks, s, 2), F32)],
        scratch=[pltpu.VMEM((2, t, LANES), F32), pltpu.VMEM((2, t, 1), F32)],
        sem=("parallel", "arbitrary"))(qkv, do16, qkv, qkv, ck_rows, lse_c, dl_c)


def _flash_dkv(qkv, do16, cum, lse_rows, dl_rows, name):
    s = qkv.shape[0]
    t = min(ATT_TILE, s)
    nq = s // t
    scale = HEAD_DIM ** -0.5
    hp_blocks = FOX_HEADS // 2

    def body(k_ref, v_ref, cum_ref, q_ref, do_ref, lse_ref, dl_ref, dk_ref, dv_ref, dck_ref,
             dk_acc, dv_acc, dck_acc):
        hp, kb = pl.program_id(0), pl.program_id(1)
        k2, v2 = k_ref[...], v_ref[...]
        first = _head_pair_masks(t)
        zero = jnp.zeros_like(k2)
        ks = (jnp.where(first, k2, zero), jnp.where(first, zero, k2))
        vs = (jnp.where(first, v2, zero), jnp.where(first, zero, v2))
        cumv = cum_ref[...]
        lane = lax.broadcasted_iota(jnp.int32, (t, LANES), 1)
        ckc = [jnp.sum(jnp.where(lane == 2 * hp + j, cumv, 0.0), axis=1, keepdims=True) for j in range(2)]
        dk_acc[...] = jnp.zeros_like(dk_acc)
        dv_acc[...] = jnp.zeros_like(dv_acc)
        dck_acc[...] = jnp.zeros_like(dck_acc)
        rel = (lax.broadcasted_iota(jnp.int32, (t, t), 1) - lax.broadcasted_iota(jnp.int32, (t, t), 0))

        def step(qb, carry):
            off = pl.multiple_of(qb * t, t)
            q2 = q_ref[pl.ds(off, t), :]
            do2 = do_ref[pl.ds(off, t), :]
            lse, dl = lse_ref[qb], dl_ref[qb]
            mask = rel >= (kb - qb) * t
            for j in range(2):
                sc = _dot(ks[j], q2, NT_DIMS) * scale - ckc[j] - lse[j:j + 1, :]
                p = jnp.exp(jnp.where(mask, sc, -jnp.inf))
                dv_acc[j] += _dot(p.astype(BF16), do2)
                dp = _dot(vs[j], do2, NT_DIMS)
                ds = p * (dp - dl[j:j + 1, :])
                dk_acc[j] += _dot(ds.astype(BF16), q2)
                dck_acc[j] += jnp.sum(ds, axis=1, keepdims=True)
            return carry

        lax.fori_loop(kb, nq, step, 0)
        dk_ref[...] = (jnp.where(first, dk_acc[0], dk_acc[1]) * scale).astype(dk_ref.dtype)
        dv_ref[...] = jnp.where(first, dv_acc[0], dv_acc[1]).astype(dv_ref.dtype)
        dck_ref[:, 0:1] = -dck_acc[0]
        dck_ref[:, 1:2] = -dck_acc[1]

    return _pcall(
        body, name=name, grid=(hp_blocks, nq),
        in_specs=[pl.BlockSpec((t, LANES), lambda h, j: (j, hp_blocks + h)),
                  pl.BlockSpec((t, LANES), lambda h, j: (j, 2 * hp_blocks + h)),
                  pl.BlockSpec((t, LANES), lambda h, j: (j, 0)),
                  pl.BlockSpec((s, LANES), lambda h, j: (0, h)),
                  pl.BlockSpec((s, LANES), lambda h, j: (0, h)),
                  pl.BlockSpec((None, nq, 2, t), lambda h, j: (h, 0, 0, 0)),
                  pl.BlockSpec((None, nq, 2, t), lambda h, j: (h, 0, 0, 0))],
        out_specs=[pl.BlockSpec((t, LANES), lambda h, j: (j, h)),
                   pl.BlockSpec((t, LANES), lambda h, j: (j, h)),
                   pl.BlockSpec((None, t, 2), lambda h, j: (h, j, 0))],
        out_shape=[_sds((s, hp_blocks * LANES), BF16), _sds((s, hp_blocks * LANES), BF16),
                   _sds((hp_blocks, s, 2), F32)],
        scratch=[pltpu.VMEM((2, t, LANES), F32), pltpu.VMEM((2, t, LANES), F32), pltpu.VMEM((2, t, 1), F32)],
        sem=("parallel", "arbitrary"))(qkv, qkv, cum, qkv, do16, lse_rows, dl_rows)


def _head_rowsum(prod, ind16, name):
    def fn(a, b, ind):
        return (_dot_split(a * b, ind),)
    return _rowwise(fn, list(prod), [ind16], [(LANES, F32)], [], name=name)[0]


def _pairs_cols(x16):
    s = x16.shape[0]
    return x16.reshape(s, FOX_HEADS // 2, 2).transpose(1, 0, 2)


def _pairs_rows(x16, t):
    s = x16.shape[0]
    return x16.reshape(s // t, t, FOX_HEADS // 2, 2).transpose(2, 0, 3, 1)


def _fox_forward(u, w):
    s = u.shape[0]
    t = min(ATT_TILE, s)
    qkv = _mm(u, w["fox_qkv"], "nn", [BF16], name="fox_qkv")
    fraw = _mm(u, w["fox_f"], "nn", [F32], name="fox_fproj")
    cum = _fox_gate(fraw, w["fox_bf"], "fox_gate")
    ck_rows = _pairs_rows(cum[:, :FOX_HEADS], t)
    o, lse = _flash_fwd(qkv, ck_rows, "fox_flash_fwd")
    y = _mm(o, w["fox_o"], "nn", [F32], name="fox_oproj")
    return y, dict(u=u, qkv=qkv, fraw=fraw, cum=cum, ck_rows=ck_rows, o=o, lse=lse)


def _fox_backward(dy, sv, w):
    s = dy.shape[0]
    t = min(ATT_TILE, s)
    do32, do16 = _mm(dy, w["fox_o"], "nt", [F32, BF16], name="fox_do")
    g_wo = _mm(sv["o"], dy, "tn", [F32], name="fox_gwo")
    delta = _head_rowsum((do32, sv["o"]), w["head_ind"], "fox_delta")[:, :FOX_HEADS]
    lse16 = sv["lse"].transpose(1, 0, 2).reshape(s, FOX_HEADS)
    dq, drow = _flash_dq(sv["qkv"], do16, sv["ck_rows"], sv["lse"], _pairs_cols(delta), "fox_flash_dq")
    dk, dv, dck = _flash_dkv(sv["qkv"], do16, sv["cum"], _pairs_rows(lse16, t), _pairs_rows(delta, t),
                             "fox_flash_dkv")
    pad = ((0, 0), (0, LANES - FOX_HEADS))
    dcol = jnp.pad(dck.transpose(1, 0, 2).reshape(s, FOX_HEADS), pad)
    drow = jnp.pad(drow.transpose(1, 0, 2).reshape(s, FOX_HEADS), pad)
    df, db_f = _fox_gate_bwd(drow, dcol, sv["fraw"], w["fox_bf"], "fox_gate_bwd")
    dproj = jnp.concatenate([dq, dk, dv, df], axis=1)
    du = _mm(dproj, w["fox_in_pad"], "nt", [F32], name="fox_du", tk=640)
    g_win = _mm(sv["u"], dproj, "tn", [F32], name="fox_gwin", tn=640)
    return du, dict(fox_w_in=g_win[:, :3 * FOX_HEADS * HEAD_DIM + FOX_HEADS], fox_w_o=g_wo,
                    fox_b_f=db_f[:, :FOX_HEADS])


def _conv_fwd(xpre, w8, b, name):
    s, c = xpre.shape
    tm, tc = min(ROW_TILE, s), min(1024, c)
    hb = tm // 8

    def body(x_ref, h_ref, w_ref, b_ref, xc_ref, xa_ref):
        i = pl.program_id(1)
        x = x_ref[...]
        halo = jnp.where(i > 0, h_ref[...], 0.0)
        w = w_ref[...]
        row = lax.broadcasted_iota(jnp.int32, (tm, tc), 0)
        row8 = lax.broadcasted_iota(jnp.int32, (8, tc), 0)
        acc = x * w[3:4] + b_ref[...]
        x8 = x[0:8]
        acc8 = x8 * w[3:4] + b_ref[...]
        for j in range(1, SSM_CONV):
            acc = acc + w[3 - j:4 - j] * pltpu.roll(x, j, 0)
            acc8 = acc8 + w[3 - j:4 - j] * jnp.where(row8 < j, pltpu.roll(halo, j, 0), pltpu.roll(x8, j, 0))
        xc_ref[...] = acc
        xc_ref[0:8, :] = acc8
        xc = xc_ref[...]
        xa_ref[...] = _silu(xc)

    tile = pl.BlockSpec((tm, tc), lambda jc, i: (i, jc))
    return _pcall(body, name=name, grid=(c // tc, s // tm),
                  in_specs=[tile, pl.BlockSpec((8, tc), lambda jc, i: (jnp.maximum(i * hb - 1, 0), jc)),
                            pl.BlockSpec((8, tc), lambda jc, i: (0, jc)), pl.BlockSpec((1, tc), lambda jc, i: (0, jc))],
                  out_specs=[tile, tile], out_shape=[_sds((s, c), F32), _sds((s, c), F32)],
                  sem=("parallel", "arbitrary"))(xpre, xpre, w8, b)


def _conv_bwd(dxa, xc, xpre, w8, name):
    s, c = xpre.shape
    tm, tc = min(ROW_TILE, s), min(1024, c)
    hb = tm // 8
    n = s // tm

    def body(d_ref, xc_ref, x_ref, xh_ref, dn_ref, xcn_ref, w_ref, dx_ref, dw_ref, db_ref, scr):
        i = pl.program_id(1)

        @pl.when(i == 0)
        def _():
            dw_ref[...] = jnp.zeros_like(dw_ref)
            db_ref[...] = jnp.zeros_like(db_ref)
        w = w_ref[...]
        x = x_ref[...]
        g = d_ref[...] * _dsilu(xc_ref[...])
        gn = jnp.where(i < n - 1, dn_ref[...] * _dsilu(xcn_ref[...]), 0.0)
        halo = jnp.where(i > 0, xh_ref[...], 0.0)
        row = lax.broadcasted_iota(jnp.int32, (tm, tc), 0)
        row8 = lax.broadcasted_iota(jnp.int32, (8, tc), 0)
        db_ref[...] += _colsum(g)
        dw_ref[3:4, :] += _colsum(g * x)
        g8 = g[0:8]
        acc = g * w[3:4]
        corr = jnp.zeros((8, tc), F32)
        for j in range(1, SSM_CONV):
            xs = pltpu.roll(x, j, 0)
            dwj = _colsum(jnp.where(row >= j, g * xs, 0.0))
            dwj = dwj + _colsum(jnp.where(row8 < j, g8 * pltpu.roll(halo, j, 0), 0.0))
            dw_ref[3 - j:4 - j, :] += dwj
            gs = pltpu.roll(g, tm - j, 0)
            acc = acc + w[3 - j:4 - j] * jnp.where(row < tm - j, gs, 0.0)
            corr = corr + w[3 - j:4 - j] * jnp.where(row8 >= 8 - j, pltpu.roll(gn, 8 - j, 0), 0.0)
        scr[...] = acc
        scr[tm - 8:tm, :] += corr
        dx_ref[...] = scr[...].astype(dx_ref.dtype)

    tile = pl.BlockSpec((tm, tc), lambda jc, i: (i, jc))
    prev8 = pl.BlockSpec((8, tc), lambda jc, i: (jnp.maximum(i * hb - 1, 0), jc))
    next8 = pl.BlockSpec((8, tc), lambda jc, i: (jnp.minimum((i + 1) * hb, n * hb - 1), jc))
    return _pcall(body, name=name, grid=(c // tc, n),
                  in_specs=[tile, tile, tile, prev8, next8, next8, pl.BlockSpec((8, tc), lambda jc, i: (0, jc))],
                  out_specs=[tile, pl.BlockSpec((8, tc), lambda jc, i: (0, jc)), pl.BlockSpec((1, tc), lambda jc, i: (0, jc))],
                  out_shape=[_sds((s, c), BF16), _sds((8, c), F32), _sds((1, c), F32)],
                  scratch=[pltpu.VMEM((tm, tc), F32)],
                  sem=("parallel", "arbitrary"))(dxa, xc, xpre, xpre, dxa, xc, w8)


def _ssd_pre(dtraw, dt_bias, a_log, name):
    def fn(raw, bias, alog):
        tm = raw.shape[0]
        z = raw + bias
        dt = jnp.maximum(z, 0.0) + jnp.log(1.0 + jnp.exp(-jnp.abs(z)))
        lane = lax.broadcasted_iota(jnp.int32, (tm, LANES), 1)
        pos = lax.broadcasted_iota(jnp.int32, (tm, LANES), 0) & (SSM_CHUNK - 1)
        dt = jnp.where(lane < SSM_HEADS, dt, 0.0)
        c = dt * (-jnp.exp(alog))
        sh = 1
        while sh < SSM_CHUNK:
            c = c + jnp.where(pos >= sh, pltpu.roll(c, sh, 0), 0.0)
            sh *= 2
        return dt, c
    return _rowwise(fn, [dtraw], [dt_bias, a_log], [(LANES, F32), (LANES, F32)], [], name=name)


def _ssd_post(dacs, ddt, dtraw, dt, dt_bias, a_log, name):
    def fn(dacs_v, ddt_v, raw, dt_v, bias, alog):
        tm = raw.shape[0]
        pos = lax.broadcasted_iota(jnp.int32, (tm, LANES), 0) & (SSM_CHUNK - 1)
        a = -jnp.exp(alog)
        c = dacs_v
        sh = 1
        while sh < SSM_CHUNK:
            c = c + jnp.where(pos + sh < SSM_CHUNK, pltpu.roll(c, tm - sh, 0), 0.0)
            sh *= 2
        draw = (ddt_v + c * a) * _sigmoid(raw + bias)
        return draw, _colsum(draw), _colsum(c * dt_v * a)
    return _rowwise(fn, [dacs, ddt, dtraw, dt], [dt_bias, a_log], [(LANES, BF16)], [(1, LANES)] * 2, name=name)


def _heads_cols(x, s):
    return x[:, :SSM_HEADS].reshape(s, SSM_GROUPS, 4).transpose(1, 0, 2)


def _heads_rows(x, s):
    return x[:, :SSM_HEADS].reshape(s // SSM_CHUNK, SSM_CHUNK, SSM_GROUPS, 4).transpose(2, 0, 3, 1)


def _expand_heads(cols, lane):
    out = cols[:, 3:4]
    for r in (2, 1, 0):
        out = jnp.where(lane < HEAD_DIM * (r + 1), cols[:, r:r + 1], out)
    return out


def _ssd_common(x, dtc, acsc, acsr):
    l = SSM_CHUNK
    lane = lax.broadcasted_iota(jnp.int32, (l, SSM_GROUP_WIDTH), 1)
    dt_e = _expand_heads(dtc, lane)
    acs_e = _expand_heads(acsc, lane)
    last = acsr[:, l - 1:l]
    lane1 = lax.broadcasted_iota(jnp.int32, (1, SSM_GROUP_WIDTH), 1)
    last_e = last[3:4, :]
    for r in (2, 1, 0):
        last_e = jnp.where(lane1 < HEAD_DIM * (r + 1), last[r:r + 1, :], last_e)
    e_e = jnp.exp(acs_e)
    dte_e = jnp.exp(last_e - acs_e)
    rowg = lax.broadcasted_iota(jnp.int32, (SSM_GROUP_WIDTH, SSM_STATE), 0)
    cd = jnp.exp(last)
    cd_mat = cd[3:4, :]
    for r in (2, 1, 0):
        cd_mat = jnp.where(rowg < HEAD_DIM * (r + 1), cd[r:r + 1, :], cd_mat)
    return lane, dt_e, e_e, dte_e, cd, cd_mat


def _ssd_fwd(xa, dtc, acsc, acsr, d_e, name):
    s = xa.shape[0]
    l, gw, ns = SSM_CHUNK, SSM_GROUP_WIDTH, SSM_STATE
    nc = s // l
    xb, bb = 2048 // gw, 2048 // ns

    def body(x_ref, b_ref, c_ref, dtc_ref, acsc_ref, acsr_ref, d_ref, y_ref, hp_ref, h_sc):
        @pl.when(pl.program_id(1) == 0)
        def _():
            h_sc[...] = jnp.zeros_like(h_sc)
        x = x_ref[...]
        bm, cm = b_ref[...].astype(BF16), c_ref[...].astype(BF16)
        acsc, acsr = acsc_ref[...], acsr_ref[...]
        lane, dt_e, e_e, dte_e, _, cd_mat = _ssd_common(x, dtc_ref[...], acsc, acsr)
        xdt = x * dt_e
        xdt16 = xdt.astype(BF16)
        cb = _dot(cm, bm, NT_DIMS)
        tril = lax.broadcasted_iota(jnp.int32, (l, l), 0) >= lax.broadcasted_iota(jnp.int32, (l, l), 1)
        yd = jnp.zeros((l, gw), F32)
        for r in range(4):
            lm = jnp.exp(jnp.where(tril, acsc[:, r:r + 1] - acsr[r:r + 1, :], -jnp.inf))
            yr = _dot((cb * lm).astype(BF16), xdt16)
            yd = jnp.where((lane >= HEAD_DIM * r) & (lane < HEAD_DIM * (r + 1)), yr, yd)
        hp = h_sc[...]
        hp_ref[...] = hp
        yoff = _dot(cm, hp.astype(BF16), NT_DIMS) * e_e
        y_ref[...] = yd + yoff + x * d_ref[...]
        st = _dot((xdt * dte_e).astype(BF16), bm, TN_DIMS)
        h_sc[...] = hp * cd_mat + st

    return _pcall(
        body, name=name, grid=(SSM_GROUPS, nc),
        in_specs=[pl.BlockSpec((l, gw), lambda g, c: (c, g)),
                  pl.BlockSpec((l, ns), lambda g, c: (c, bb + g)),
                  pl.BlockSpec((l, ns), lambda g, c: (c, bb + SSM_GROUPS + g)),
                  pl.BlockSpec((None, l, 4), lambda g, c: (g, c, 0)),
                  pl.BlockSpec((None, l, 4), lambda g, c: (g, c, 0)),
                  pl.BlockSpec((None, None, 4, l), lambda g, c: (g, c, 0, 0)),
                  pl.BlockSpec((None, 1, gw), lambda g, c: (g, 0, 0))],
        out_specs=[pl.BlockSpec((l, gw), lambda g, c: (c, g)),
                   pl.BlockSpec((None, None, gw, ns), lambda g, c: (g, c, 0, 0))],
        out_shape=[_sds((s, xb * gw), F32), _sds((SSM_GROUPS, nc, gw, ns), F32)],
        scratch=[pltpu.VMEM((gw, ns), F32)],
        sem=("parallel", "arbitrary"))(xa, xa, xa, dtc, acsc, acsr, d_e)


def _ssd_bwd(dy, xa, dtc, acsc, acsr, d_e, hprev, name):
    s = xa.shape[0]
    l, gw, ns = SSM_CHUNK, SSM_GROUP_WIDTH, SSM_STATE
    nc = s // l
    bb = 2048 // ns

    def body(dy_ref, x_ref, b_ref, c_ref, dtc_ref, acsc_ref, acsr_ref, d_ref, hp_ref,
             dx_ref, db_ref, dc_ref, dacs_ref, ddt_ref, dd_ref, dh_sc):
        @pl.when(pl.program_id(1) == 0)
        def _():
            dh_sc[...] = jnp.zeros_like(dh_sc)
            dd_ref[...] = jnp.zeros_like(dd_ref)
        dyv, x = dy_ref[...], x_ref[...]
        bm, cm = b_ref[...].astype(BF16), c_ref[...].astype(BF16)
        acsc, acsr = acsc_ref[...], acsr_ref[...]
        lane, dt_e, e_e, dte_e, cd, cd_mat = _ssd_common(x, dtc_ref[...], acsc, acsr)
        xdt = x * dt_e
        xdt16 = xdt.astype(BF16)
        dy16 = dyv.astype(BF16)
        cb = _dot(cm, bm, NT_DIMS)
        cbt = _dot(bm, cm, NT_DIMS)
        hp = hp_ref[...]
        hp16 = hp.astype(BF16)
        g = dh_sc[...]
        g16 = g.astype(BF16)
        t_all = _dot(cm, hp16, NT_DIMS)
        dt16 = (dyv * e_e).astype(BF16)
        dc = _dot(dt16, hp16)
        dhp = _dot(dt16, cm, TN_DIMS)
        yoff_term = dyv * t_all * e_e
        wv = xdt * dte_e
        dw = _dot(bm, g16, NT_DIMS)
        db = _dot(wv.astype(BF16), g16)
        dxdt = dw * dte_e
        dte_term = dw * wv
        gh = g * hp
        dh_sc[...] = g * cd_mat + dhp
        ri = lax.broadcasted_iota(jnp.int32, (l, l), 0)
        ci = lax.broadcasted_iota(jnp.int32, (l, l), 1)
        tril, triu = ri >= ci, ci >= ri
        dcb = jnp.zeros((l, l), F32)
        dcbt = jnp.zeros((l, l), F32)
        q_rows, q_cols = [], []
        for r in range(4):
            in_head = (lane >= HEAD_DIM * r) & (lane < HEAD_DIM * (r + 1))
            lm = jnp.exp(jnp.where(tril, acsc[:, r:r + 1] - acsr[r:r + 1, :], -jnp.inf))
            lmt = jnp.exp(jnp.where(triu, acsr[r:r + 1, :] - acsc[:, r:r + 1], -jnp.inf))
            mm_, mt = cb * lm, cbt * lmt
            dyr = jnp.where(in_head, dy16, jnp.zeros_like(dy16))
            dm = _dot(dyr, xdt16, NT_DIMS)
            dmt = _dot(xdt16, dyr, NT_DIMS)
            dxdt = dxdt + jnp.where(in_head, _dot(mt.astype(BF16), dy16), 0.0)
            dcb = dcb + dm * lm
            dcbt = dcbt + dmt * lmt
            q_rows.append(jnp.sum(dm * mm_, axis=1, keepdims=True))
            q_cols.append(jnp.sum(dmt * mt, axis=1, keepdims=True))
        dc = dc + _dot(dcb.astype(BF16), bm)
        db = db + _dot(dcbt.astype(BF16), cm)
        dxdt_x = dxdt * x
        dy_x = dyv * x
        rowc = lax.broadcasted_iota(jnp.int32, (l, 1), 0)
        lane128 = lax.broadcasted_iota(jnp.int32, (1, LANES), 1)
        dd_row = jnp.zeros((1, LANES), F32)
        for r in range(4):
            in_head = (lane >= HEAD_DIM * r) & (lane < HEAD_DIM * (r + 1))
            seg = lambda v: jnp.sum(jnp.where(in_head, v, 0.0), axis=1, keepdims=True)
            s1, s2, s3 = seg(yoff_term), seg(dte_term), seg(dxdt_x)
            dcd = jnp.sum(_colsum(gh[HEAD_DIM * r:HEAD_DIM * (r + 1), :]), axis=1, keepdims=True)
            last_add = _colsum(s2) + dcd * cd[r:r + 1, :]
            dacs_r = q_rows[r] - q_cols[r] + s1 - s2 + jnp.where(rowc == l - 1, last_add, 0.0)
            dacs_ref[:, r:r + 1] = dacs_r
            ddt_ref[:, r:r + 1] = s3
            dd_row = dd_row + jnp.where(lane128 == r, _colsum(seg(dy_x)), 0.0)
        dd_ref[0:1, :] += dd_row
        dx_ref[...] = dxdt * dt_e + dyv * d_ref[...]
        db_ref[...] = db
        dc_ref[...] = dc

    rc = lambda c: nc - 1 - c
    return _pcall(
        body, name=name, grid=(SSM_GROUPS, nc),
        in_specs=[pl.BlockSpec((l, gw), lambda g, c: (rc(c), g)),
                  pl.BlockSpec((l, gw), lambda g, c: (rc(c), g)),
                  pl.BlockSpec((l, ns), lambda g, c: (rc(c), bb + g)),
                  pl.BlockSpec((l, ns), lambda g, c: (rc(c), bb + SSM_GROUPS + g)),
                  pl.BlockSpec((None, l, 4), lambda g, c: (g, rc(c), 0)),
                  pl.BlockSpec((None, l, 4), lambda g, c: (g, rc(c), 0)),
                  pl.BlockSpec((None, None, 4, l), lambda g, c: (g, rc(c), 0, 0)),
                  pl.BlockSpec((None, 1, gw), lambda g, c: (g, 0, 0)),
                  pl.BlockSpec((None, None, gw, ns), lambda g, c: (g, rc(c), 0, 0))],
        out_specs=[pl.BlockSpec((l, gw), lambda g, c: (rc(c), g)),
                   pl.BlockSpec((l, ns), lambda g, c: (rc(c), g)),
                   pl.BlockSpec((l, ns), lambda g, c: (rc(c), g)),
                   pl.BlockSpec((None, l, 4), lambda g, c: (g, rc(c), 0)),
                   pl.BlockSpec((None, l, 4), lambda g, c: (g, rc(c), 0)),
                   pl.BlockSpec((None, 8, LANES), lambda g, c: (g, 0, 0))],
        out_shape=[_sds((s, 2048), F32), _sds((s, SSM_GROUPS * ns), F32), _sds((s, SSM_GROUPS * ns), F32),
                   _sds((SSM_GROUPS, s, 4), F32), _sds((SSM_GROUPS, s, 4), F32), _sds((SSM_GROUPS, 8, LANES), F32)],
        scratch=[pltpu.VMEM((gw, ns), F32)],
        sem=("parallel", "arbitrary"))(dy, xa, xa, xa, dtc, acsc, acsr, d_e, hprev)


def _gate_norm(y, z, nw, name):
    c = y.shape[1]

    def fn(yv, zv, w):
        outs = []
        for k in range(c // SSM_GROUP_WIDTH):
            sl = slice(k * SSM_GROUP_WIDTH, (k + 1) * SSM_GROUP_WIDTH)
            yg = yv[:, sl] * _silu(zv[:, sl])
            rinv = lax.rsqrt(jnp.mean(yg * yg, axis=-1, keepdims=True) + RMS_EPS)
            outs.append(yg * rinv * w[:, sl])
        return (jnp.concatenate(outs, axis=1),)
    return _rowwise(fn, [y, z], [nw], [(c, BF16)], [], name=name)[0]


def _gate_norm_bwd(dyn, y, z, nw, name):
    c = y.shape[1]

    def fn(dv, yv, zv, w):
        dys, dzs, dws = [], [], []
        for k in range(c // SSM_GROUP_WIDTH):
            sl = slice(k * SSM_GROUP_WIDTH, (k + 1) * SSM_GROUP_WIDTH)
            ys, zs, ds = yv[:, sl], zv[:, sl], dv[:, sl]
            sz = _silu(zs)
            yg = ys * sz
            rinv = lax.rsqrt(jnp.mean(yg * yg, axis=-1, keepdims=True) + RMS_EPS)
            nrm = yg * rinv
            dn = ds * w[:, sl]
            dyg = rinv * (dn - nrm * jnp.mean(dn * nrm, axis=-1, keepdims=True))
            dys.append(dyg * sz)
            dzs.append(dyg * ys * _dsilu(zs))
            dws.append(_colsum(ds * nrm))
        return jnp.concatenate(dys, axis=1), jnp.concatenate(dzs, axis=1), jnp.concatenate(dws, axis=1)
    return _rowwise(fn, [dyn, y, z], [nw], [(c, F32), (c, BF16)], [(1, c)], name=name, tm=128)


def _ssd_forward(u, w):
    s = u.shape[0]
    z = _mm(u, w["ssm_z"], "nn", [F32], name="ssm_zproj")
    xpre = _mm(u, w["ssm_xbc"], "nn", [F32], name="ssm_xproj")
    dtraw = _mm(u, w["ssm_dt"], "nn", [F32], name="ssm_dtproj")
    xc, xa = _conv_fwd(xpre, w["conv_w8"], w["conv_b"], "ssm_conv")
    dt, acs = _ssd_pre(dtraw, w["dt_bias"], w["a_log"], "ssm_pre")
    dtc, acsc, acsr = _heads_cols(dt, s), _heads_cols(acs, s), _heads_rows(acs, s)
    y, hprev = _ssd_fwd(xa, dtc, acsc, acsr, w["d_e"], "ssm_scan")
    yn = _gate_norm(y, z, w["norm_w"], "ssm_gate_norm")
    out = _mm(yn, w["ssm_out"], "nn", [F32], name="ssm_oproj")
    return out, dict(u=u, z=z, xpre=xpre, xc=xc, xa=xa, dtraw=dtraw, dt=dt, dtc=dtc, acsc=acsc, acsr=acsr,
                     y=y, hprev=hprev, yn=yn)


def _ssd_backward(dy, sv, w):
    s = dy.shape[0]
    dyn = _mm(dy, w["ssm_out"], "nt", [F32], name="ssm_dyn")
    g_wout = _mm(sv["yn"], dy, "tn", [F32], name="ssm_gwout")
    dys, dz, dnw = _gate_norm_bwd(dyn, sv["y"], sv["z"], w["norm_w"], "ssm_gate_norm_bwd")
    dx, dbm, dcm, dacs_c, ddt_c, dd = _ssd_bwd(dys, sv["xa"], sv["dtc"], sv["acsc"], sv["acsr"], w["d_e"],
                                               sv["hprev"], "ssm_scan_bwd")
    pad = ((0, 0), (0, LANES - SSM_HEADS))
    dacs = jnp.pad(dacs_c.transpose(1, 0, 2).reshape(s, SSM_HEADS), pad)
    ddt = jnp.pad(ddt_c.transpose(1, 0, 2).reshape(s, SSM_HEADS), pad)
    draw, dbias, dalog = _ssd_post(dacs, ddt, sv["dtraw"], sv["dt"], w["dt_bias"], w["a_log"], "ssm_post")
    dxa = jnp.concatenate([dx, dbm, dcm], axis=1)
    dxpre, dcw, dcb = _conv_bwd(dxa, sv["xc"], sv["xpre"], w["conv_w8"], "ssm_conv_bwd")
    dproj = jnp.concatenate([dz, dxpre, draw], axis=1)
    du = _mm(dproj, w["ssm_in_pad"], "nt", [F32], name="ssm_du", tk=896)
    g_win = _mm(sv["u"], dproj, "tn", [F32], name="ssm_gwin", tn=896)
    n_in = 2048 + 4096 + SSM_HEADS
    return du, dict(ssm_w_in=g_win[:, :n_in], ssm_w_out=g_wout, ssm_conv_w=dcw[:SSM_CONV], ssm_conv_b=dcb,
                    ssm_norm_w=dnw, ssm_dt_bias=dbias[:, :SSM_HEADS], ssm_a_log=dalog[:, :SSM_HEADS],
                    ssm_d=dd[:, 0, :4].reshape(1, SSM_HEADS))


def _mlp_forward(u2, w1, w2, tag):
    def epi(acc):
        hr = jnp.maximum(acc, 0.0)
        return hr, hr * hr
    hr, a = _mm(u2, w1, "nn", [BF16, BF16], name=tag + "_mlp_up", epi=epi)
    y2 = _mm(a, w2, "nn", [F32], name=tag + "_mlp_down")
    return y2, hr, a


def _mlp_backward(dy2, u2, hr, a, w1, w2, tag):
    dh = _mm(dy2, w2, "nt", [BF16], name=tag + "_mlp_dh", extra=(hr,),
             epi=lambda acc, h: (acc * (2.0 * h.astype(F32)),))
    g_w2 = _mm(a, dy2, "tn", [F32], name=tag + "_mlp_gw2")
    g_w1 = _mm(u2, dh, "tn", [F32], name=tag + "_mlp_gw1")
    du2 = _mm(dh, w1, "nt", [F32], name=tag + "_mlp_du")
    return du2, g_w1, g_w2


def _ada_forward(c16, ada_w, ada_b_cols, name):
    nl, d, cols = ada_w.shape
    tn = 512

    def body(c_ref, w_ref, b_ref, o_ref):
        cond = _silu(c_ref[...]).astype(BF16)
        o_ref[...] = _dot(cond, w_ref[...].astype(BF16)) + b_ref[...]

    return _pcall(body, name=name, grid=(nl, cols // tn),
                  in_specs=[pl.BlockSpec((16, d), lambda i, j: (0, 0)),
                            pl.BlockSpec((None, d, tn), lambda i, j: (i, 0, j)),
                            pl.BlockSpec((None, 1, tn), lambda i, j: (i, 0, j))],
                  out_specs=pl.BlockSpec((None, 16, tn), lambda i, j: (i, 0, j)),
                  out_shape=_sds((nl, 16, cols), F32), sem=("parallel", "parallel"))(c16, ada_w, ada_b_cols)


def _ada_backward(c_t, dmod_cols, name):
    d, nb = c_t.shape
    nl, _, cols = dmod_cols.shape
    tn = 512

    def body(c_ref, dm_ref, o_ref):
        cond = _silu(c_ref[...])
        dm = dm_ref[...]
        acc = cond[:, 0:1] * dm[0:1, :]
        for b in range(1, nb):
            acc = acc + cond[:, b:b + 1] * dm[b:b + 1, :]
        o_ref[...] = acc

    return _pcall(body, name=name, grid=(nl, cols // tn),
                  in_specs=[pl.BlockSpec((d, nb), lambda i, j: (0, 0)),
                            pl.BlockSpec((None, nb, tn), lambda i, j: (i, 0, j))],
                  out_specs=pl.BlockSpec((None, d, tn), lambda i, j: (i, 0, j)),
                  out_shape=_sds((nl, d, cols), F32), sem=("parallel", "parallel"))(c_t, dmod_cols)


def _adamw(w, g, m, v, name):
    rows, cols = w.shape
    tm = rows
    for cand in (256, 128, 64, 32, 16, 8):
        if rows % cand == 0 and rows > cand:
            tm = cand
            break
    c1 = 1.0 / (1.0 - ADAM_B1 ** ADAM_STEP)
    c2 = 1.0 / (1.0 - ADAM_B2 ** ADAM_STEP)

    def fn(wv, gv, mv, vv):
        mn = ADAM_B1 * mv + (1.0 - ADAM_B1) * gv
        vn = ADAM_B2 * vv + (1.0 - ADAM_B2) * (gv * gv)
        delta = -ADAM_LR * ((mn * c1) / (jnp.sqrt(vn * c2) + ADAM_EPS) + ADAM_WD * wv)
        return delta, mn, vn
    return _rowwise(fn, [w, g, m, v], [], [(cols, F32)] * 3, [], name=name, tm=tm)


def _my_pos():
    return lax.axis_index("x"), lax.axis_index("y"), lax.axis_index("c")


def _allgather8(x, name):
    r, c = x.shape

    def body(x_ref, out_ref, send_sems, recv_sems, local_sem):
        mx, my, mc = _my_pos()
        me = 4 * mx + 2 * my + mc
        mine = pltpu.make_async_copy(x_ref, out_ref.at[me], local_sem)
        mine.start()
        copies = []
        for k in range(1, 8):
            fx, fy, fc = (k >> 2) & 1, (k >> 1) & 1, k & 1
            px = 1 - mx if fx else mx
            py = 1 - my if fy else my
            pc = 1 - mc if fc else mc
            peer = 4 * px + 2 * py + pc
            send = pltpu.make_async_remote_copy(src_ref=x_ref, dst_ref=out_ref.at[me], send_sem=send_sems.at[k - 1],
                                                recv_sem=recv_sems.at[k - 1], device_id=(px, py, pc),
                                                device_id_type=MESH)
            send.start()
            recv = pltpu.make_async_remote_copy(src_ref=x_ref, dst_ref=out_ref.at[peer], send_sem=send_sems.at[k - 1],
                                                recv_sem=recv_sems.at[k - 1], device_id=(px, py, pc),
                                                device_id_type=MESH)
            copies.append((send, recv))
        for send, recv in copies:
            recv.wait_recv()
        for send, recv in copies:
            send.wait_send()
        mine.wait()

    vm = pl.BlockSpec(memory_space=pltpu.VMEM)
    return _pcall(body, name=name, in_specs=[vm], out_specs=vm, out_shape=_sds((8, r, c), x.dtype),
                  scratch=[pltpu.SemaphoreType.DMA((7,)), pltpu.SemaphoreType.DMA((7,)), pltpu.SemaphoreType.DMA])(x)


def _chip_flips(mx, my):
    out = []
    for fx, fy in ((1, 0), (0, 1), (1, 1)):
        px = 1 - mx if fx else mx
        py = 1 - my if fy else my
        out.append((px, py, 2 * px + py))
    return out


def _gather_chips(shard, name):
    r, c = shard.shape

    def body(x_ref, out_ref, send_sems, recv_sems, local_sem):
        mx, my, mc = _my_pos()
        mk = 2 * mx + my
        mine = pltpu.make_async_copy(x_ref, out_ref.at[mk], local_sem)
        mine.start()
        copies = []
        for j, (px, py, pk) in enumerate(_chip_flips(mx, my)):
            send = pltpu.make_async_remote_copy(src_ref=x_ref, dst_ref=out_ref.at[mk], send_sem=send_sems.at[j],
                                                recv_sem=recv_sems.at[j], device_id=(px, py, mc), device_id_type=MESH)
            send.start()
            recv = pltpu.make_async_remote_copy(src_ref=x_ref, dst_ref=out_ref.at[pk], send_sem=send_sems.at[j],
                                                recv_sem=recv_sems.at[j], device_id=(px, py, mc), device_id_type=MESH)
            copies.append((send, recv))
        for send, recv in copies:
            recv.wait_recv()
        for send, recv in copies:
            send.wait_send()
        mine.wait()

    return _pcall(body, name=name, in_specs=[HBM_SPEC], out_specs=HBM_SPEC, out_shape=_sds((4, r, c), shard.dtype),
                  scratch=[pltpu.SemaphoreType.DMA((3,)), pltpu.SemaphoreType.DMA((3,)), pltpu.SemaphoreType.DMA])(shard)


def _pair_exchange(g4, name):
    n, _, h, c = g4.shape

    def body(g_ref, out_ref, send_sem, recv_sem):
        mx, my, mc = _my_pos()
        oc = 1 - mc
        copies = []
        for k in range(n):
            cp = pltpu.make_async_remote_copy(src_ref=g_ref.at[k, oc], dst_ref=out_ref.at[k], send_sem=send_sem.at[k],
                                              recv_sem=recv_sem.at[k], device_id=(mx, my, oc), device_id_type=MESH)
            cp.start()
            copies.append(cp)
        for cp in copies:
            cp.wait_recv()
        for cp in copies:
            cp.wait_send()

    return _pcall(body, name=name, in_specs=[HBM_SPEC], out_specs=HBM_SPEC, out_shape=_sds((n, h, c), g4.dtype),
                  scratch=[pltpu.SemaphoreType.DMA((n,)), pltpu.SemaphoreType.DMA((n,))])(g4)


def _pair_add(g4, recv, core, name):
    n, _, h, c = g4.shape
    tm = 8
    for cand in (400, 200, 120, 80, 40, 16, 8):
        if h % cand == 0:
            tm = cand
            break

    def body(core_ref, a_ref, b_ref, o_ref):
        o_ref[...] = a_ref[...] + b_ref[...]

    grid_spec = pltpu.PrefetchScalarGridSpec(
        num_scalar_prefetch=1, grid=(n, h // tm),
        in_specs=[pl.BlockSpec((None, None, tm, c), lambda k, i, cr: (k, cr[0], i, 0)),
                  pl.BlockSpec((None, tm, c), lambda k, i, cr: (k, i, 0))],
        out_specs=pl.BlockSpec((None, tm, c), lambda k, i, cr: (k, i, 0)))
    return pl.pallas_call(body, out_shape=_sds((n, h, c), F32), grid_spec=grid_spec, name=name,
                          compiler_params=pltpu.CompilerParams(vmem_limit_bytes=VMEM_LIMIT_BYTES,
                                                               dimension_semantics=("parallel", "parallel")))(core, g4, recv)


def _chip_exchange(p, name):
    n, h, c = p.shape

    def body(p_ref, out_ref, send_sems, recv_sems, local_sem):
        mx, my, mc = _my_pos()
        mk = 2 * mx + my
        mine = pltpu.make_async_copy(p_ref.at[mk], out_ref.at[mk], local_sem)
        mine.start()
        copies = []
        for j, (px, py, pk) in enumerate(_chip_flips(mx, my)):
            send = pltpu.make_async_remote_copy(src_ref=p_ref.at[pk], dst_ref=out_ref.at[mk], send_sem=send_sems.at[j],
                                                recv_sem=recv_sems.at[j], device_id=(px, py, mc), device_id_type=MESH)
            send.start()
            recv = pltpu.make_async_remote_copy(src_ref=p_ref.at[pk], dst_ref=out_ref.at[pk], send_sem=send_sems.at[j],
                                                recv_sem=recv_sems.at[j], device_id=(px, py, mc), device_id_type=MESH)
            copies.append((send, recv))
        for send, recv in copies:
            recv.wait_recv()
        for send, recv in copies:
            send.wait_send()
        mine.wait()

    return _pcall(body, name=name, in_specs=[HBM_SPEC], out_specs=HBM_SPEC, out_shape=_sds((n, h, c), p.dtype),
                  scratch=[pltpu.SemaphoreType.DMA((3,)), pltpu.SemaphoreType.DMA((3,)), pltpu.SemaphoreType.DMA])(p)


def _sum_slots(q, name):
    n, h, c = q.shape
    tm = 8
    for cand in (400, 200, 120, 80, 40, 16, 8):
        if h % cand == 0:
            tm = cand
            break

    def body(q_ref, o_ref):
        acc = q_ref[0]
        for k in range(1, n):
            acc = acc + q_ref[k]
        o_ref[...] = acc

    return _pcall(body, name=name, grid=(h // tm,),
                  in_specs=[pl.BlockSpec((n, tm, c), lambda i: (0, i, 0))],
                  out_specs=pl.BlockSpec((tm, c), lambda i: (i, 0)),
                  out_shape=_sds((h, c), F32), sem=("parallel",))(q)


def _pair_share(f, name):
    h, c = f.shape

    def body(f_ref, out_ref, send_sem, recv_sem, local_sem):
        mx, my, mc = _my_pos()
        oc = 1 - mc
        mine = pltpu.make_async_copy(f_ref, out_ref.at[mc], local_sem)
        mine.start()
        send = pltpu.make_async_remote_copy(src_ref=f_ref, dst_ref=out_ref.at[mc], send_sem=send_sem, recv_sem=recv_sem,
                                            device_id=(mx, my, oc), device_id_type=MESH)
        send.start()
        recv = pltpu.make_async_remote_copy(src_ref=f_ref, dst_ref=out_ref.at[oc], send_sem=send_sem, recv_sem=recv_sem,
                                            device_id=(mx, my, oc), device_id_type=MESH)
        recv.wait_recv()
        send.wait_send()
        mine.wait()

    return _pcall(body, name=name, in_specs=[HBM_SPEC], out_specs=HBM_SPEC, out_shape=_sds((2, h, c), f.dtype),
                  scratch=[pltpu.SemaphoreType.DMA, pltpu.SemaphoreType.DMA, pltpu.SemaphoreType.DMA])(f)


BIG = ("mlp_w1", "mlp_w2", "fox_w_in", "fox_w_o", "ssm_w_in", "ssm_w_out")
SMALL_SHARDED = ("ssm_conv_w", "ssm_conv_b", "ssm_norm_w")
PACK_COLS = 1024


def _pack_rows(parts, rows_multiple, dtype):
    flat = jnp.concatenate([p.reshape(-1).astype(dtype) for p in parts])
    unit = rows_multiple * PACK_COLS
    total = -(-flat.shape[0] // unit) * unit
    flat = jnp.pad(flat, (0, total - flat.shape[0]))
    return flat.reshape(total // PACK_COLS, PACK_COLS)


def _unpack(flat, shapes):
    out, off = [], 0
    for sh in shapes:
        n = 1
        for d_ in sh:
            n *= d_
        out.append(flat[off:off + n].reshape(sh))
        off += n
    return out


def _chip_slice(full, axis, k, width):
    idx = [slice(None)] * full.ndim
    idx[axis] = slice(k * width, (k + 1) * width)
    return full[tuple(idx)]


SHARD_AXIS = dict(mlp_w1=2, mlp_w2=1, fox_w_in=2, fox_w_o=1, ssm_w_in=2, ssm_w_out=1, ssm_conv_w=2,
                  ssm_conv_b=1, ssm_norm_w=1, ada_w=2)


def kernel(x, c, ada_w, ada_b, ln_mix_g, ln_mix_b, ln_mlp_g, ln_mlp_b, mlp_w1, mlp_w2, fox_w_in, fox_b_f, fox_w_o, ssm_w_in, ssm_conv_w, ssm_conv_b, ssm_dt_bias, ssm_a_log, ssm_d, ssm_norm_w, ssm_w_out, loss_target, m_ada_w, m_ada_b, m_ln_mix_g, m_ln_mix_b, m_ln_mlp_g, m_ln_mlp_b, m_mlp_w1, m_mlp_w2, m_fox_w_in, m_fox_b_f, m_fox_w_o, m_ssm_w_in, m_ssm_conv_w, m_ssm_conv_b, m_ssm_dt_bias, m_ssm_a_log, m_ssm_d, m_ssm_norm_w, m_ssm_w_out, v_ada_w, v_ada_b, v_ln_mix_g, v_ln_mix_b, v_ln_mlp_g, v_ln_mlp_b, v_mlp_w1, v_mlp_w2, v_fox_w_in, v_fox_b_f, v_fox_w_o, v_ssm_w_in, v_ssm_conv_w, v_ssm_conv_b, v_ssm_dt_bias, v_ssm_a_log, v_ssm_d, v_ssm_norm_w, v_ssm_w_out):
    names = ("ada_w", "ada_b", "ln_mix_g", "ln_mix_b", "ln_mlp_g", "ln_mlp_b", "mlp_w1", "mlp_w2", "fox_w_in",
             "fox_b_f", "fox_w_o", "ssm_w_in", "ssm_conv_w", "ssm_conv_b", "ssm_dt_bias", "ssm_a_log", "ssm_d",
             "ssm_norm_w", "ssm_w_out")
    weights = dict(zip(names, (ada_w, ada_b, ln_mix_g, ln_mix_b, ln_mlp_g, ln_mlp_b, mlp_w1, mlp_w2, fox_w_in,
                               fox_b_f, fox_w_o, ssm_w_in, ssm_conv_w, ssm_conv_b, ssm_dt_bias, ssm_a_log, ssm_d,
                               ssm_norm_w, ssm_w_out)))
    m_in = dict(zip(names, (m_ada_w, m_ada_b, m_ln_mix_g, m_ln_mix_b, m_ln_mlp_g, m_ln_mlp_b, m_mlp_w1, m_mlp_w2,
                            m_fox_w_in, m_fox_b_f, m_fox_w_o, m_ssm_w_in, m_ssm_conv_w, m_ssm_conv_b, m_ssm_dt_bias,
                            m_ssm_a_log, m_ssm_d, m_ssm_norm_w, m_ssm_w_out)))
    v_in = dict(zip(names, (v_ada_w, v_ada_b, v_ln_mix_g, v_ln_mix_b, v_ln_mlp_g, v_ln_mlp_b, v_mlp_w1, v_mlp_w2,
                            v_fox_w_in, v_fox_b_f, v_fox_w_o, v_ssm_w_in, v_ssm_conv_w, v_ssm_conv_b, v_ssm_dt_bias,
                            v_ssm_a_log, v_ssm_d, v_ssm_norm_w, v_ssm_w_out)))

    mx, my, mc = _my_pos()
    chip = 2 * mx + my
    me = 4 * mx + 2 * my + mc
    x0 = x[0]
    target = loss_target[0]
    s, d = x0.shape
    n_qkv = 3 * FOX_HEADS * HEAD_DIM

    big_shapes = [weights[n].shape for n in BIG]
    packed = _pack_rows([weights[n] for n in BIG], 16, BF16)
    gathered = _gather_chips(packed, "gather_weights")
    per_chip = [_unpack(gathered[k].reshape(-1), big_shapes) for k in range(4)]
    full = {n: jnp.concatenate([per_chip[k][i] for k in range(4)], axis=SHARD_AXIS[n]) for i, n in enumerate(BIG)}

    small_shapes = [weights[n].shape for n in SMALL_SHARDED]
    small_packed = _pack_rows([weights[n] for n in SMALL_SHARDED] + [c], 8, F32).reshape(-1, LANES)
    small_all = _allgather8(small_packed, "gather_small")
    small_chip = [_unpack(small_all[2 * k].reshape(-1), small_shapes) for k in range(4)]
    small_full = {n: jnp.concatenate([small_chip[k][i] for k in range(4)], axis=SHARD_AXIS[n])
                  for i, n in enumerate(SMALL_SHARDED)}
    n_small = sum(weights[n].size for n in SMALL_SHARDED)
    c_all = small_all.reshape(8, -1)[:, n_small:n_small + d]

    cols = ada_w.shape[2]
    ada_b_cols = lax.dynamic_slice_in_dim(ada_b, chip * cols, cols, axis=1)[:, None, :]
    c16 = jnp.pad(c_all, ((0, 8), (0, 0)))
    mod_part = _ada_forward(c16, ada_w, ada_b_cols, "ada_fwd")[:, :8, :]
    mod_all = _allgather8(mod_part.reshape(-1, LANES), "gather_mod").reshape(8, DEPTH, 8, cols)
    mod_mine = jnp.stack([lax.dynamic_index_in_dim(mod_all[2 * k], me, axis=1, keepdims=False) for k in range(4)], axis=1)
    mods = [jnp.pad(mod_mine[i].reshape(6, d), ((0, 2), (0, 0))) for i in range(DEPTH)]

    w = dict(
        fox_qkv=full["fox_w_in"][0][:, :n_qkv],
        fox_f=jnp.pad(full["fox_w_in"][0][:, n_qkv:], ((0, 0), (0, LANES - FOX_HEADS))),
        fox_in_pad=jnp.pad(full["fox_w_in"][0], ((0, 0), (0, LANES - FOX_HEADS))),
        fox_o=full["fox_w_o"][0],
        fox_bf=jnp.pad(fox_b_f, ((0, 0), (0, LANES - FOX_HEADS))),
        head_ind=(jnp.arange(d)[:, None] // HEAD_DIM == jnp.arange(LANES)[None, :]).astype(BF16),
        ssm_z=full["ssm_w_in"][0][:, :2048],
        ssm_xbc=full["ssm_w_in"][0][:, 2048:6144],
        ssm_dt=jnp.pad(full["ssm_w_in"][0][:, 6144:], ((0, 0), (0, LANES - SSM_HEADS))),
        ssm_in_pad=jnp.pad(full["ssm_w_in"][0], ((0, 0), (0, LANES - SSM_HEADS))),
        ssm_out=full["ssm_w_out"][0],
        conv_w8=jnp.pad(small_full["ssm_conv_w"][0], ((0, 8 - SSM_CONV), (0, 0))),
        conv_b=small_full["ssm_conv_b"],
        norm_w=small_full["ssm_norm_w"],
        dt_bias=jnp.pad(ssm_dt_bias, ((0, 0), (0, LANES - SSM_HEADS))),
        a_log=jnp.pad(ssm_a_log, ((0, 0), (0, LANES - SSM_HEADS))),
        d_e=jnp.repeat(ssm_d.reshape(SSM_GROUPS, 4), HEAD_DIM, axis=1)[:, None, :],
    )
    mixers = ((_fox_forward, _fox_backward), (_ssd_forward, _ssd_backward))

    saved = []
    xin = x0
    for i in range(DEPTH):
        tag = "l%d" % i
        u = _modulate_in(xin, mods[i], tag + "_mod_in")
        y, sv = mixers[i % 2][0](u, w)
        r, x1, u2 = _res_ln_mod(xin, y, mods[i], ln_mix_g[i:i + 1], ln_mix_b[i:i + 1], tag + "_res_ln1")
        y2, hr, a = _mlp_forward(u2, full["mlp_w1"][i], full["mlp_w2"][i], tag)
        r2, x2 = _res_ln(x1, y2, mods[i], ln_mlp_g[i:i + 1], ln_mlp_b[i:i + 1], tag + "_res_ln2")
        saved.append(dict(xin=xin, y=y, r=r, u2=u2, hr=hr, a=a, y2=y2, r2=r2, mix=sv))
        xin = x2
    dx, sq = _loss_grad(xin, target, "loss_grad")
    loss = lax.psum(0.5 * jnp.sum(sq) / d, ("x", "y", "c"))

    grads = {}
    dmods = [None] * DEPTH
    ln_grads = {n: [None] * DEPTH for n in ("ln_mix_g", "ln_mix_b", "ln_mlp_g", "ln_mlp_b")}
    g_w1, g_w2 = [None] * DEPTH, [None] * DEPTH
    for i in reversed(range(DEPTH)):
        tag = "l%d" % i
        sv = saved[i]
        dy2, dres, dg2, db2, dgm = _ln2_bwd(dx, sv["r2"], sv["y2"], mods[i], ln_mlp_g[i:i + 1], tag + "_ln2_bwd")
        du2, g_w1[i], g_w2[i] = _mlp_backward(dy2, sv["u2"], sv["hr"], sv["a"], full["mlp_w1"][i], full["mlp_w2"][i], tag)
        dy, dres0, dscm, dshm, dg1, db1, dga = _ln1_bwd(du2, dres, sv["r"], sv["y"], mods[i], ln_mix_g[i:i + 1],
                                                        ln_mix_b[i:i + 1], tag + "_ln1_bwd")
        du, mg = mixers[i % 2][1](dy, sv["mix"], w)
        grads.update(mg)
        dx, dsca, dsha = _mod_in_bwd(du, dres0, sv["xin"], mods[i], tag + "_mod_in_bwd")
        dmods[i] = jnp.concatenate([dsha, dsca, dga, dshm, dscm, dgm], axis=1)
        ln_grads["ln_mix_g"][i], ln_grads["ln_mix_b"][i] = dg1, db1
        ln_grads["ln_mlp_g"][i], ln_grads["ln_mlp_b"][i] = dg2, db2
    grad_x = dx[None]
    grads["mlp_w1"] = jnp.stack(g_w1)
    grads["mlp_w2"] = jnp.stack(g_w2)
    for n in ("fox_w_in", "fox_w_o", "ssm_w_in", "ssm_w_out", "ssm_conv_w"):
        grads[n] = grads[n][None]

    small_names = ("ln_mix_g", "ln_mix_b", "ln_mlp_g", "ln_mlp_b", "fox_b_f", "ssm_dt_bias", "ssm_a_log", "ssm_d")
    small_parts = list(dmods)
    for n in small_names[:4]:
        small_parts.append(jnp.concatenate(ln_grads[n], axis=0))
    for n in small_names[4:]:
        small_parts.append(jnp.pad(grads[n], ((0, 0), (0, LANES - grads[n].shape[1]))))
    small_vec = _pack_rows(small_parts, 1, F32).reshape(-1, LANES)
    small_vec = jnp.pad(small_vec, ((0, -small_vec.shape[0] % 8), (0, 0)))
    small_g_all = _allgather8(small_vec, "gather_small_grads")
    small_sum = _sum_slots(small_g_all, "sum_small_grads").reshape(-1)
    dmod_sum = small_sum[:DEPTH * 6 * d].reshape(DEPTH, 6 * d)
    off = DEPTH * 6 * d
    final = {"ada_b": dmod_sum}
    for n in small_names[:4]:
        final[n] = small_sum[off:off + DEPTH * d].reshape(DEPTH, d)
        off += DEPTH * d
    for n in small_names[4:]:
        width = weights[n].shape[1]
        final[n] = small_sum[off:off + width].reshape(1, width)
        off += LANES

    dmod_all = small_g_all.reshape(8, -1)[:, :DEPTH * 6 * d].reshape(8, DEPTH, 6 * d)
    dmod_cols = lax.dynamic_slice_in_dim(dmod_all, chip * cols, cols, axis=2).transpose(1, 0, 2)
    final["ada_w"] = _ada_backward(c_all.T, dmod_cols, "ada_bwd")

    sharded = BIG + SMALL_SHARDED
    shard_shapes = [weights[n].shape for n in sharded]
    per_target = []
    for k in range(4):
        parts = [_chip_slice(grads[n], SHARD_AXIS[n], k, weights[n].shape[SHARD_AXIS[n]]) for n in sharded]
        per_target.append(_pack_rows(parts, 16, F32))
    g_all = jnp.stack(per_target)
    rows = g_all.shape[1]
    g4 = g_all.reshape(4, 2, rows // 2, PACK_COLS)
    recv = _pair_exchange(g4, "rs_pair_exchange")
    part = _pair_add(g4, recv, jnp.reshape(mc, (1,)).astype(jnp.int32), "rs_pair_add")
    slots = _chip_exchange(part, "rs_chip_exchange")
    half = _sum_slots(slots, "rs_chip_sum")
    both = _pair_share(half, "rs_pair_share")
    for n, g_shard in zip(sharded, _unpack(both.reshape(-1), shard_shapes)):
        final[n] = g_shard

    outs_g, outs_d, outs_m, outs_v = [], [], [], []
    for n in names:
        wv = weights[n]
        two_d = (-1, wv.shape[-1])
        delta, mn, vn = _adamw(wv.reshape(two_d), final[n].reshape(two_d), m_in[n].reshape(two_d),
                               v_in[n].reshape(two_d), "adamw_" + n)
        outs_g.append(final[n].reshape(wv.shape))
        outs_d.append(delta.reshape(wv.shape))
        outs_m.append(mn.reshape(wv.shape))
        outs_v.append(vn.reshape(wv.shape))
    return (loss, grad_x, *outs_g, *outs_d, *outs_m, *outs_v)
```

```python
import functools

import jax
import jax.numpy as jnp
from jax import lax
from jax.experimental import pallas as pl
from jax.experimental.pallas import tpu as pltpu

F32, BF16 = jnp.float32, jnp.bfloat16
MESH = pl.DeviceIdType.MESH
HBM_SPEC = pl.BlockSpec(memory_space=pltpu.HBM)

VMEM_LIMIT_BYTES = 52 * 2**20
LANES = 128

FOX_HEADS, HEAD_DIM = 16, 64
SSM_HEADS, SSM_GROUPS, SSM_STATE, SSM_CHUNK, SSM_CONV = 32, 8, 128, 128, 4
SSM_GROUP_WIDTH = 256
LN_EPS, RMS_EPS = 1e-5, 1e-5
DEPTH = 2
ALPHA = (2.0 * DEPTH) ** 0.25
ADAM_LR, ADAM_B1, ADAM_B2, ADAM_EPS, ADAM_WD, ADAM_STEP = 0.001, 0.9, 0.999, 1e-08, 0.01, 10

ATT_TILE = 512
ROW_TILE = 256
SCAN_TILE = 512
MM_TM, MM_TN, MM_TK = 1024, 1024, 1024

NT_DIMS = (((1,), (1,)), ((), ()))
TN_DIMS = (((0,), (0,)), ((), ()))
NN_DIMS = (((1,), (0,)), ((), ()))


def _pcall(body, *, name, out_shape, grid=(), in_specs=None, out_specs=None, scratch=(), sem=None, prefetch=0):
    params = dict(vmem_limit_bytes=VMEM_LIMIT_BYTES)
    if sem is not None:
        params["dimension_semantics"] = sem
    if prefetch:
        grid_spec = pltpu.PrefetchScalarGridSpec(num_scalar_prefetch=prefetch, grid=grid, in_specs=in_specs,
                                                 out_specs=out_specs, scratch_shapes=scratch)
        return pl.pallas_call(body, out_shape=out_shape, grid_spec=grid_spec, name=name,
                              compiler_params=pltpu.CompilerParams(**params))
    kwargs = {}
    if in_specs is not None:
        kwargs["in_specs"] = in_specs
    if out_specs is not None:
        kwargs["out_specs"] = out_specs
    return pl.pallas_call(body, out_shape=out_shape, grid=grid, scratch_shapes=scratch, name=name,
                          compiler_params=pltpu.CompilerParams(**params), **kwargs)


def _sds(shape, dtype):
    return jax.ShapeDtypeStruct(tuple(shape), dtype)


def _dot(a, b, dims=NN_DIMS):
    return lax.dot_general(a, b, dims, preferred_element_type=F32)


def _sigmoid(x):
    return 1.0 / (1.0 + jnp.exp(-x))


def _silu(x):
    return x * _sigmoid(x)


def _dsilu(x):
    s = _sigmoid(x)
    return s * (1.0 + x * (1.0 - s))


def _dot_split(x, m16, dims=NN_DIMS):
    hi = x.astype(BF16)
    r1 = x - hi.astype(F32)
    mid = r1.astype(BF16)
    lo = (r1 - mid.astype(F32)).astype(BF16)
    return _dot(hi, m16, dims) + _dot(mid, m16, dims) + _dot(lo, m16, dims)


def _mm(a, b, dims, outs, *, name, tm=MM_TM, tn=MM_TN, tk=MM_TK, epi=None, extra=()):
    if dims == "nn":
        (m, k), (k2, n) = a.shape, b.shape
    elif dims == "nt":
        (m, k), (n, k2) = a.shape, b.shape
    else:
        (k, m), (k2, n) = a.shape, b.shape
    assert k == k2, (a.shape, b.shape, dims)
    tm, tn, tk = min(tm, m), min(tn, n), min(tk, k)
    assert m % tm == 0 and n % tn == 0 and k % tk == 0, (m, n, k, tm, tn, tk)
    nk = k // tk
    dn = {"nn": NN_DIMS, "nt": NT_DIMS, "tn": TN_DIMS}[dims]
    n_extra, n_out = len(extra), len(outs)
    if epi is None:
        epi = lambda acc: (acc,) * n_out

    def body(a_ref, b_ref, *rest):
        extra_refs, out_refs, acc_ref = rest[:n_extra], rest[n_extra:n_extra + n_out], rest[-1]
        kk = pl.program_id(2)

        @pl.when(kk == 0)
        def _():
            acc_ref[...] = jnp.zeros_like(acc_ref)

        acc_ref[...] += _dot(a_ref[...].astype(BF16), b_ref[...].astype(BF16), dn)

        @pl.when(kk == nk - 1)
        def _():
            res = epi(acc_ref[...], *[e[...] for e in extra_refs])
            for o, r in zip(out_refs, res):
                o[...] = r.astype(o.dtype)

    if dims == "tn":
        a_spec = pl.BlockSpec((tk, tm), lambda i, j, kk: (kk, i))
    else:
        a_spec = pl.BlockSpec((tm, tk), lambda i, j, kk: (i, kk))
    if dims == "nt":
        b_spec = pl.BlockSpec((tn, tk), lambda i, j, kk: (j, kk))
    else:
        b_spec = pl.BlockSpec((tk, tn), lambda i, j, kk: (kk, j))
    o_spec = pl.BlockSpec((tm, tn), lambda i, j, kk: (i, j))
    res = _pcall(body, name=name, grid=(m // tm, n // tn, nk),
                 in_specs=[a_spec, b_spec] + [o_spec] * n_extra,
                 out_specs=[o_spec] * n_out,
                 out_shape=[_sds((m, n), d) for d in outs],
                 scratch=[pltpu.VMEM((tm, tn), F32)],
                 sem=("parallel", "parallel", "arbitrary"))(a, b, *extra)
    return res[0] if n_out == 1 else res


def _rowwise(fn, rows, consts, row_outs, acc_outs, *, name, tm=ROW_TILE):
    s = rows[0].shape[0]
    tm = min(tm, s)
    assert s % tm == 0
    n_in, n_o = len(rows) + len(consts), len(row_outs)

    def body(*refs):
        ins, outs = refs[:n_in], refs[n_in:]
        res = fn(*[r[...] for r in ins])
        if not isinstance(res, (tuple, list)):
            res = (res,)
        for o, val in zip(outs[:n_o], res[:n_o]):
            o[...] = val.astype(o.dtype)
        if acc_outs:
            @pl.when(pl.program_id(0) == 0)
            def _():
                for o in outs[n_o:]:
                    o[...] = jnp.zeros_like(o)
            for o, val in zip(outs[n_o:], res[n_o:]):
                o[...] += val

    in_specs = [pl.BlockSpec((tm, r.shape[1]), lambda i: (i, 0)) for r in rows]
    in_specs += [pl.BlockSpec(c.shape, functools.partial(lambda nd, i: (0,) * nd, c.ndim)) for c in consts]
    out_specs = [pl.BlockSpec((tm, c), lambda i: (i, 0)) for c, _ in row_outs]
    out_specs += [pl.BlockSpec(tuple(sh), lambda i: (0, 0)) for sh in acc_outs]
    out_shape = [_sds((s, c), d) for c, d in row_outs] + [_sds(sh, F32) for sh in acc_outs]
    res = _pcall(body, name=name, grid=(s // tm,), in_specs=in_specs, out_specs=out_specs,
                 out_shape=out_shape, sem=("arbitrary",))(*rows, *consts)
    return res


def _colsum(x):
    return jnp.sum(x, axis=0, keepdims=True)


def _ln_stats(r):
    mu = jnp.mean(r, axis=-1, keepdims=True)
    xc = r - mu
    var = jnp.mean(xc * xc, axis=-1, keepdims=True)
    rstd = lax.rsqrt(var + LN_EPS)
    return xc * rstd, rstd


def _ln_bwd(dy, xhat, rstd, gamma):
    dyg = dy * gamma
    m1 = jnp.mean(dyg, axis=-1, keepdims=True)
    m2 = jnp.mean(dyg * xhat, axis=-1, keepdims=True)
    return rstd * (dyg - m1 - xhat * m2)


def _modulate_in(x, mod, name):
    def fn(xv, m):
        return (xv * (1.0 + m[1:2]) + m[0:1],)
    return _rowwise(fn, [x], [mod], [(x.shape[1], BF16)], [], name=name)[0]


def _res_ln_mod(x, y, mod, g, b, name):
    d = x.shape[1]

    def fn(xv, yv, m, gv, bv):
        r = ALPHA * xv + (1.0 + m[2:3]) * yv
        xhat, _ = _ln_stats(r)
        x1 = xhat * gv + bv
        u2 = x1 * (1.0 + m[4:5]) + m[3:4]
        return r, x1, u2
    return _rowwise(fn, [x, y], [mod, g, b], [(d, F32), (d, F32), (d, BF16)], [], name=name)


def _res_ln(x, y, mod, g, b, name):
    d = x.shape[1]

    def fn(xv, yv, m, gv, bv):
        r = ALPHA * xv + (1.0 + m[5:6]) * yv
        xhat, _ = _ln_stats(r)
        return r, xhat * gv + bv
    return _rowwise(fn, [x, y], [mod, g, b], [(d, F32), (d, F32)], [], name=name)


def _loss_grad(xf, target, name):
    d = xf.shape[1]

    def fn(xv, tv):
        e = xv - tv
        return e * (1.0 / d), _colsum(e * e)
    return _rowwise(fn, [xf, target], [], [(d, F32)], [(1, d)], name=name)


def _ln2_bwd(dx, r2, y2, mod, g, name):
    d = dx.shape[1]

    def fn(dxv, rv, yv, m, gv):
        xhat, rstd = _ln_stats(rv)
        dr = _ln_bwd(dxv, xhat, rstd, gv)
        return (dr * (1.0 + m[5:6]), ALPHA * dr,
                _colsum(dxv * xhat), _colsum(dxv), _colsum(dr * yv))
    return _rowwise(fn, [dx, r2, y2], [mod, g], [(d, BF16), (d, F32)], [(1, d)] * 3, name=name)


def _ln1_bwd(du2, dres, r, y, mod, g, b, name):
    d = du2.shape[1]

    def fn(duv, drv, rv, yv, m, gv, bv):
        xhat, rstd = _ln_stats(rv)
        x1 = xhat * gv + bv
        dx1 = duv * (1.0 + m[4:5]) + drv
        dr = _ln_bwd(dx1, xhat, rstd, gv)
        return (dr * (1.0 + m[2:3]), ALPHA * dr,
                _colsum(duv * x1), _colsum(duv), _colsum(dx1 * xhat), _colsum(dx1), _colsum(dr * yv))
    return _rowwise(fn, [du2, dres, r, y], [mod, g, b], [(d, BF16), (d, F32)], [(1, d)] * 5, name=name)


def _mod_in_bwd(du, dres, x, mod, name):
    d = du.shape[1]

    def fn(duv, drv, xv, m):
        return duv * (1.0 + m[1:2]) + drv, _colsum(duv * xv), _colsum(duv)
    return _rowwise(fn, [du, dres, x], [mod], [(d, F32)], [(1, d)] * 2, name=name)


def _fox_gate(fraw, b_pad, name):
    s = fraw.shape[0]
    tb = min(SCAN_TILE, s)

    def body(f_ref, b_ref, cum_ref, carry):
        @pl.when(pl.program_id(0) == 0)
        def _():
            carry[...] = jnp.zeros_like(carry)
        z = f_ref[...] + b_ref[...]
        lf = jnp.minimum(z, 0.0) - jnp.log(1.0 + jnp.exp(-jnp.abs(z)))
        lane = lax.broadcasted_iota(jnp.int32, (tb, LANES), 1)
        row = lax.broadcasted_iota(jnp.int32, (tb, LANES), 0)
        c = jnp.where(lane < FOX_HEADS, lf, 0.0)
        sh = 1
        while sh < tb:
            c = c + jnp.where(row >= sh, pltpu.roll(c, sh, 0), 0.0)
            sh *= 2
        c = c + carry[0:1, :]
        cum_ref[...] = c
        carry[0:1, :] = c[tb - 1:tb, :]

    return _pcall(body, name=name, grid=(s // tb,),
                  in_specs=[pl.BlockSpec((tb, LANES), lambda i: (i, 0)), pl.BlockSpec((1, LANES), lambda i: (0, 0))],
                  out_specs=pl.BlockSpec((tb, LANES), lambda i: (i, 0)),
                  out_shape=_sds((s, LANES), F32), scratch=[pltpu.VMEM((8, LANES), F32)],
                  sem=("arbitrary",))(fraw, b_pad)


def _fox_gate_bwd(drow, dcol, fraw, b_pad, name):
    s = fraw.shape[0]
    tb = min(SCAN_TILE, s)
    n = s // tb

    def body(dr_ref, dc_ref, f_ref, b_ref, df_ref, db_ref, carry):
        @pl.when(pl.program_id(0) == 0)
        def _():
            carry[...] = jnp.zeros_like(carry)
            db_ref[...] = jnp.zeros_like(db_ref)
        row = lax.broadcasted_iota(jnp.int32, (tb, LANES), 0)
        c = dr_ref[...] + dc_ref[...]
        sh = 1
        while sh < tb:
            c = c + jnp.where(row + sh < tb, pltpu.roll(c, tb - sh, 0), 0.0)
            sh *= 2
        c = c + carry[0:1, :]
        carry[0:1, :] = c[0:1, :]
        z = f_ref[...] + b_ref[...]
        df = c * (1.0 / (1.0 + jnp.exp(z)))
        df_ref[...] = df.astype(df_ref.dtype)
        db_ref[...] += _colsum(df)

    rev = lambda i: (n - 1 - i, 0)
    return _pcall(body, name=name, grid=(n,),
                  in_specs=[pl.BlockSpec((tb, LANES), rev)] * 3 + [pl.BlockSpec((1, LANES), lambda i: (0, 0))],
                  out_specs=[pl.BlockSpec((tb, LANES), rev), pl.BlockSpec((1, LANES), lambda i: (0, 0))],
                  out_shape=[_sds((s, LANES), BF16), _sds((1, LANES), F32)],
                  scratch=[pltpu.VMEM((8, LANES), F32)], sem=("arbitrary",))(drow, dcol, fraw, b_pad)


def _head_pair_masks(t):
    lane = lax.broadcasted_iota(jnp.int32, (t, LANES), 1)
    return lane < HEAD_DIM


def _lane_blocks(x):
    return [x[:, c * LANES:(c + 1) * LANES] for c in range(x.shape[1] // LANES)]


def _sum_list(xs):
    acc = xs[0]
    for x in xs[1:]:
        acc = acc + x
    return acc


def _causal(t, transposed=False):
    ri = lax.broadcasted_iota(jnp.int32, (t, t), 0)
    ci = lax.broadcasted_iota(jnp.int32, (t, t), 1)
    return ci >= ri if transposed else ri >= ci


def _flash_fwd(qkv, ck_rows, kb_start, name):
    s = qkv.shape[0]
    t = min(ATT_TILE, s)
    nq = s // t
    scale = HEAD_DIM ** -0.5
    hp_blocks = FOX_HEADS // 2

    def body(ks_ref, q_ref, k_ref, v_ref, ck_ref, o_ref, lse_ref, acc_ref, m_ref, l_ref):
        hp, qb = pl.program_id(0), pl.program_id(1)
        q2 = q_ref[...] * scale
        first = _head_pair_masks(t)
        zero = jnp.zeros_like(q2)
        qs = (jnp.where(first, q2, zero), jnp.where(first, zero, q2))
        m_ref[...] = jnp.full_like(m_ref, -jnp.inf)
        l_ref[...] = jnp.zeros_like(l_ref)
        acc_ref[...] = jnp.zeros_like(acc_ref)

        def tile(kb, diagonal):
            off = pl.multiple_of(kb * t, t)
            k2 = k_ref[pl.ds(off, t), :]
            v2 = v_ref[pl.ds(off, t), :]
            ck = ck_ref[kb]
            pvs, als = [], []
            for j in range(2):
                sc = _dot(qs[j], k2, NT_DIMS) - ck[j:j + 1, :]
                if diagonal:
                    sc = jnp.where(_causal(t), sc, -jnp.inf)
                blocks = _lane_blocks(sc)
                mx = blocks[0]
                for b in blocks[1:]:
                    mx = jnp.maximum(mx, b)
                m_old = m_ref[j]
                m_new = jnp.maximum(m_old, jnp.max(mx, axis=1, keepdims=True))
                ps = [jnp.exp(b - m_new) for b in blocks]
                a = jnp.exp(m_old - m_new)
                l_ref[j] = a * l_ref[j] + _sum_list(ps)
                m_ref[j] = m_new
                pvs.append(_dot(jnp.concatenate(ps, axis=1).astype(BF16), v2))
                als.append(a)
            acc_ref[...] = jnp.where(first, als[0], als[1]) * acc_ref[...] + jnp.where(first, pvs[0], pvs[1])

        def step(kb, carry):
            tile(kb, False)
            return carry

        lax.fori_loop(ks_ref[hp, qb], qb, step, 0)
        tile(qb, True)
        l0 = jnp.sum(l_ref[0], axis=1, keepdims=True)
        l1 = jnp.sum(l_ref[1], axis=1, keepdims=True)
        o_ref[...] = acc_ref[...] / jnp.where(first, l0, l1)
        lse_ref[:, 0:1] = m_ref[0][:, 0:1] + jnp.log(l0)
        lse_ref[:, 1:2] = m_ref[1][:, 0:1] + jnp.log(l1)

    return _pcall(
        body, name=name, grid=(hp_blocks, nq), prefetch=1,
        in_specs=[pl.BlockSpec((t, LANES), lambda h, i, ks: (i, h)),
                  pl.BlockSpec((s, LANES), lambda h, i, ks: (0, hp_blocks + h)),
                  pl.BlockSpec((s, LANES), lambda h, i, ks: (0, 2 * hp_blocks + h)),
                  pl.BlockSpec((None, nq, 2, t), lambda h, i, ks: (h, 0, 0, 0))],
        out_specs=[pl.BlockSpec((t, LANES), lambda h, i, ks: (i, h)),
                   pl.BlockSpec((None, t, 2), lambda h, i, ks: (h, i, 0))],
        out_shape=[_sds((s, hp_blocks * LANES), F32), _sds((hp_blocks, s, 2), F32)],
        scratch=[pltpu.VMEM((t, LANES), F32), pltpu.VMEM((2, t, LANES), F32), pltpu.VMEM((2, t, LANES), F32)],
        sem=("parallel", "arbitrary"))(kb_start, qkv, qkv, qkv, ck_rows)


def _flash_dq(qkv, do16, ck_rows, lse_c, dl_c, kb_start, name):
    s = qkv.shape[0]
    t = min(ATT_TILE, s)
    nq = s // t
    scale = HEAD_DIM ** -0.5
    hp_blocks = FOX_HEADS // 2

    def body(ks_ref, q_ref, do_ref, k_ref, v_ref, ck_ref, lse_ref, dl_ref, dq_ref, drow_ref, acc_ref, row_acc):
        hp, qb = pl.program_id(0), pl.program_id(1)
        q2, do2 = q_ref[...] * scale, do_ref[...]
        first = _head_pair_masks(t)
        zero = jnp.zeros_like(q2)
        qs = (jnp.where(first, q2, zero), jnp.where(first, zero, q2))
        dos = (jnp.where(first, do2, zero), jnp.where(first, zero, do2))
        lse, dl = lse_ref[...], dl_ref[...]
        lse_b = [jnp.broadcast_to(lse[:, j:j + 1], (t, LANES)) for j in range(2)]
        dl_b = [jnp.broadcast_to(dl[:, j:j + 1], (t, LANES)) for j in range(2)]
        acc_ref[...] = jnp.zeros_like(acc_ref)
        row_acc[...] = jnp.zeros_like(row_acc)

        def tile(kb, diagonal):
            off = pl.multiple_of(kb * t, t)
            k2 = k_ref[pl.ds(off, t), :]
            v2 = v_ref[pl.ds(off, t), :]
            ck = ck_ref[kb]
            dqs = []
            for j in range(2):
                sc = _dot(qs[j], k2, NT_DIMS) - ck[j:j + 1, :]
                if diagonal:
                    sc = jnp.where(_causal(t), sc, -jnp.inf)
                dp = _dot(dos[j], v2, NT_DIMS)
                dsb = [jnp.exp(x - lse_b[j]) * (d - dl_b[j]) for x, d in zip(_lane_blocks(sc), _lane_blocks(dp))]
                row_acc[j] += _sum_list(dsb)
                dqs.append(_dot(jnp.concatenate(dsb, axis=1).astype(BF16), k2))
            acc_ref[...] += jnp.where(first, dqs[0], dqs[1])

        def step(kb, carry):
            tile(kb, False)
            return carry

        lax.fori_loop(ks_ref[hp, qb], qb, step, 0)
        tile(qb, True)
        dq_ref[...] = (acc_ref[...] * scale).astype(dq_ref.dtype)
        drow_ref[:, 0:1] = jnp.sum(row_acc[0], axis=1, keepdims=True)
        drow_ref[:, 1:2] = jnp.sum(row_acc[1], axis=1, keepdims=True)

    return _pcall(
        body, name=name, grid=(hp_blocks, nq), prefetch=1,
        in_specs=[pl.BlockSpec((t, LANES), lambda h, i, ks: (i, h)),
                  pl.BlockSpec((t, LANES), lambda h, i, ks: (i, h)),
                  pl.BlockSpec((s, LANES), lambda h, i, ks: (0, hp_blocks + h)),
                  pl.BlockSpec((s, LANES), lambda h, i, ks: (0, 2 * hp_blocks + h)),
                  pl.BlockSpec((None, nq, 2, t), lambda h, i, ks: (h, 0, 0, 0)),
                  pl.BlockSpec((None, t, 2), lambda h, i, ks: (h, i, 0)),
                  pl.BlockSpec((None, t, 2), lambda h, i, ks: (h, i, 0))],
        out_specs=[pl.BlockSpec((t, LANES), lambda h, i, ks: (i, h)),
                   pl.BlockSpec((None, t, 2), lambda h, i, ks: (h, i, 0))],
        out_shape=[_sds((s, hp_blocks * LANES), BF16), _sds((hp_blocks, s, 2), F32)],
        scratch=[pltpu.VMEM((t, LANES), F32), pltpu.VMEM((2, t, LANES), F32)],
        sem=("parallel", "arbitrary"))(kb_start, qkv, do16, qkv, qkv, ck_rows, lse_c, dl_c)


def _flash_dkv(qkv, do16, cum, lse_rows, dl_rows, qb_end, name):
    s = qkv.shape[0]
    t = min(ATT_TILE, s)
    nq = s // t
    scale = HEAD_DIM ** -0.5
    hp_blocks = FOX_HEADS // 2

    def body(qe_ref, k_ref, v_ref, cum_ref, q_ref, do_ref, lse_ref, dl_ref, dk_ref, dv_ref, dck_ref,
             dk_acc, dv_acc, dck_acc):
        hp, kb = pl.program_id(0), pl.program_id(1)
        k2, v2 = k_ref[...] * scale, v_ref[...]
        first = _head_pair_masks(t)
        zero = jnp.zeros_like(k2)
        ks = (jnp.where(first, k2, zero), jnp.where(first, zero, k2))
        vs = (jnp.where(first, v2, zero), jnp.where(first, zero, v2))
        cumv = cum_ref[...]
        lane = lax.broadcasted_iota(jnp.int32, (t, LANES), 1)
        ck_b = [jnp.broadcast_to(jnp.sum(jnp.where(lane == 2 * hp + j, cumv, 0.0), axis=1, keepdims=True), (t, LANES))
                for j in range(2)]
        dk_acc[...] = jnp.zeros_like(dk_acc)
        dv_acc[...] = jnp.zeros_like(dv_acc)
        dck_acc[...] = jnp.zeros_like(dck_acc)

        def tile(qb, diagonal):
            off = pl.multiple_of(qb * t, t)
            q2 = q_ref[pl.ds(off, t), :]
            do2 = do_ref[pl.ds(off, t), :]
            lse, dl = lse_ref[qb], dl_ref[qb]
            dvs, dks = [], []
            for j in range(2):
                sc = _dot(ks[j], q2, NT_DIMS) - lse[j:j + 1, :]
                if diagonal:
                    sc = jnp.where(_causal(t, transposed=True), sc, -jnp.inf)
                dp = _dot(vs[j], do2, NT_DIMS) - dl[j:j + 1, :]
                pb = [jnp.exp(x - ck_b[j]) for x in _lane_blocks(sc)]
                dsb = [p * d for p, d in zip(pb, _lane_blocks(dp))]
                dck_acc[j] += _sum_list(dsb)
                dvs.append(_dot(jnp.concatenate(pb, axis=1).astype(BF16), do2))
                dks.append(_dot(jnp.concatenate(dsb, axis=1).astype(BF16), q2))
            dv_acc[...] += jnp.where(first, dvs[0], dvs[1])
            dk_acc[...] += jnp.where(first, dks[0], dks[1])

        def step(qb, carry):
            tile(qb, False)
            return carry

        tile(kb, True)
        lax.fori_loop(kb + 1, qe_ref[hp, kb] + 1, step, 0)
        dk_ref[...] = (dk_acc[...] * scale).astype(dk_ref.dtype)
        dv_ref[...] = dv_acc[...].astype(dv_ref.dtype)
        dck_ref[:, 0:1] = -jnp.sum(dck_acc[0], axis=1, keepdims=True)
        dck_ref[:, 1:2] = -jnp.sum(dck_acc[1], axis=1, keepdims=True)

    return _pcall(
        body, name=name, grid=(hp_blocks, nq), prefetch=1,
        in_specs=[pl.BlockSpec((t, LANES), lambda h, j, qe: (j, hp_blocks + h)),
                  pl.BlockSpec((t, LANES), lambda h, j, qe: (j, 2 * hp_blocks + h)),
                  pl.BlockSpec((t, LANES), lambda h, j, qe: (j, 0)),
                  pl.BlockSpec((s, LANES), lambda h, j, qe: (0, h)),
                  pl.BlockSpec((s, LANES), lambda h, j, qe: (0, h)),
                  pl.BlockSpec((None, nq, 2, t), lambda h, j, qe: (h, 0, 0, 0)),
                  pl.BlockSpec((None, nq, 2, t), lambda h, j, qe: (h, 0, 0, 0))],
        out_specs=[pl.BlockSpec((t, LANES), lambda h, j, qe: (j, h)),
                   pl.BlockSpec((t, LANES), lambda h, j, qe: (j, h)),
                   pl.BlockSpec((None, t, 2), lambda h, j, qe: (h, j, 0))],
        out_shape=[_sds((s, hp_blocks * LANES), BF16), _sds((s, hp_blocks * LANES), BF16),
                   _sds((hp_blocks, s, 2), F32)],
        scratch=[pltpu.VMEM((t, LANES), F32), pltpu.VMEM((t, LANES), F32), pltpu.VMEM((2, t, LANES), F32)],
        sem=("parallel", "arbitrary"))(qb_end, qkv, qkv, cum, qkv, do16, lse_rows, dl_rows)


SKIP_NATS = 110.0


def _qk_norms(qkv, ind16, name):
    d = FOX_HEADS * HEAD_DIM

    def fn(tile, ind):
        q = tile[:, :d].astype(F32)
        k = tile[:, d:2 * d].astype(F32)
        return _dot_split(q * q, ind), _dot_split(k * k, ind)
    return _rowwise(fn, [qkv], [ind16], [(LANES, F32), (LANES, F32)], [], name=name)


def _skip_bounds(qn, kn, cum, t):
    s = qn.shape[0]
    nq = s // t
    hp = FOX_HEADS // 2
    scale = HEAD_DIM ** -0.5
    qmax = jnp.sqrt(jnp.max(qn.reshape(nq, t, FOX_HEADS), axis=1))
    kmax = jnp.sqrt(jnp.max(kn, axis=0))
    bound = qmax * kmax[None, :] * (scale * 1.01) + 1e-3
    gap = cum[0::t][:, None, :] - cum[t - 1::t][None, :, :]
    idx = jnp.arange(nq, dtype=jnp.int32)
    needed = (gap + 2.0 * bound[:, None, :]) > -SKIP_NATS
    needed = needed.reshape(nq, nq, hp, 2).any(axis=-1) & (idx[None, :] <= idx[:, None])[:, :, None]
    first = jnp.min(jnp.where(needed, idx[None, :, None], nq), axis=1)
    first = jnp.minimum(first, idx[:, None])
    start = lax.cummin(first, axis=0, reverse=True)
    uses = start[:, None, :] <= idx[None, :, None]
    last = jnp.max(jnp.where(uses, idx[:, None, None], 0), axis=0)
    last = jnp.maximum(last, idx[:, None])
    return start.T.astype(jnp.int32), last.T.astype(jnp.int32)


def _head_rowsum(prod, ind16, name):
    def fn(a, b, ind):
        return (_dot_split(a * b, ind),)
    return _rowwise(fn, list(prod), [ind16], [(LANES, F32)], [], name=name)[0]


def _pairs_cols(x16):
    s = x16.shape[0]
    return x16.reshape(s, FOX_HEADS // 2, 2).transpose(1, 0, 2)


def _pairs_rows(x16, t):
    s = x16.shape[0]
    return x16.reshape(s // t, t, FOX_HEADS // 2, 2).transpose(2, 0, 3, 1)


def _fox_forward(u, w):
    s = u.shape[0]
    t = min(ATT_TILE, s)
    qkv = _mm(u, w["fox_qkv"], "nn", [BF16], name="fox_qkv")
    fraw = _mm(u, w["fox_f"], "nn", [F32], name="fox_fproj")
    cum = _fox_gate(fraw, w["fox_bf"], "fox_gate")
    ck_rows = _pairs_rows(cum[:, :FOX_HEADS], t)
    qn, kn = _qk_norms(qkv, w["head_ind"], "fox_qk_norms")
    kb_start, qb_end = _skip_bounds(qn[:, :FOX_HEADS], kn[:, :FOX_HEADS], cum[:, :FOX_HEADS], t)
    o, lse = _flash_fwd(qkv, ck_rows, kb_start, "fox_flash_fwd")
    y = _mm(o, w["fox_o"], "nn", [F32], name="fox_oproj")
    return y, dict(u=u, qkv=qkv, fraw=fraw, cum=cum, ck_rows=ck_rows, o=o, lse=lse, kb_start=kb_start,
                   qb_end=qb_end)


def _fox_backward(dy, sv, w):
    s = dy.shape[0]
    t = min(ATT_TILE, s)
    do32, do16 = _mm(dy, w["fox_o"], "nt", [F32, BF16], name="fox_do")
    g_wo = _mm(sv["o"], dy, "tn", [F32], name="fox_gwo")
    delta = _head_rowsum((do32, sv["o"]), w["head_ind"], "fox_delta")[:, :FOX_HEADS]
    lse16 = sv["lse"].transpose(1, 0, 2).reshape(s, FOX_HEADS)
    dq, drow = _flash_dq(sv["qkv"], do16, sv["ck_rows"], sv["lse"], _pairs_cols(delta), sv["kb_start"],
                         "fox_flash_dq")
    dk, dv, dck = _flash_dkv(sv["qkv"], do16, sv["cum"], _pairs_rows(lse16, t), _pairs_rows(delta, t),
                             sv["qb_end"], "fox_flash_dkv")
    pad = ((0, 0), (0, LANES - FOX_HEADS))
    dcol = jnp.pad(dck.transpose(1, 0, 2).reshape(s, FOX_HEADS), pad)
    drow = jnp.pad(drow.transpose(1, 0, 2).reshape(s, FOX_HEADS), pad)
    df, db_f = _fox_gate_bwd(drow, dcol, sv["fraw"], w["fox_bf"], "fox_gate_bwd")
    dproj = jnp.concatenate([dq, dk, dv, df], axis=1)
    du = _mm(dproj, w["fox_in_pad"], "nt", [F32], name="fox_du", tk=640)
    g_win = _mm(sv["u"], dproj, "tn", [F32], name="fox_gwin", tn=640)
    return du, dict(fox_w_in=g_win[:, :3 * FOX_HEADS * HEAD_DIM + FOX_HEADS], fox_w_o=g_wo,
                    fox_b_f=db_f[:, :FOX_HEADS])


def _conv_fwd(xpre, w8, b, name):
    s, c = xpre.shape
    tm, tc = min(ROW_TILE, s), min(1024, c)
    hb = tm // 8

    def body(x_ref, h_ref, w_ref, b_ref, xc_ref, xa_ref):
        i = pl.program_id(1)
        x = x_ref[...]
        halo = jnp.where(i > 0, h_ref[...], 0.0)
        w = w_ref[...]
        row = lax.broadcasted_iota(jnp.int32, (tm, tc), 0)
        row8 = lax.broadcasted_iota(jnp.int32, (8, tc), 0)
        acc = x * w[3:4] + b_ref[...]
        x8 = x[0:8]
        acc8 = x8 * w[3:4] + b_ref[...]
        for j in range(1, SSM_CONV):
            acc = acc + w[3 - j:4 - j] * pltpu.roll(x, j, 0)
            acc8 = acc8 + w[3 - j:4 - j] * jnp.where(row8 < j, pltpu.roll(halo, j, 0), pltpu.roll(x8, j, 0))
        xc_ref[...] = acc
        xc_ref[0:8, :] = acc8
        xc = xc_ref[...]
        xa_ref[...] = _silu(xc)

    tile = pl.BlockSpec((tm, tc), lambda jc, i: (i, jc))
    return _pcall(body, name=name, grid=(c // tc, s // tm),
                  in_specs=[tile, pl.BlockSpec((8, tc), lambda jc, i: (jnp.maximum(i * hb - 1, 0), jc)),
                            pl.BlockSpec((8, tc), lambda jc, i: (0, jc)), pl.BlockSpec((1, tc), lambda jc, i: (0, jc))],
                  out_specs=[tile, tile], out_shape=[_sds((s, c), F32), _sds((s, c), F32)],
                  sem=("parallel", "arbitrary"))(xpre, xpre, w8, b)


def _conv_bwd(dxa, xc, xpre, w8, name):
    s, c = xpre.shape
    tm, tc = min(ROW_TILE, s), min(1024, c)
    hb = tm // 8
    n = s // tm

    def body(d_ref, xc_ref, x_ref, xh_ref, dn_ref, xcn_ref, w_ref, dx_ref, dw_ref, db_ref, scr):
        i = pl.program_id(1)

        @pl.when(i == 0)
        def _():
            dw_ref[...] = jnp.zeros_like(dw_ref)
            db_ref[...] = jnp.zeros_like(db_ref)
        w = w_ref[...]
        x = x_ref[...]
        g = d_ref[...] * _dsilu(xc_ref[...])
        gn = jnp.where(i < n - 1, dn_ref[...] * _dsilu(xcn_ref[...]), 0.0)
        halo = jnp.where(i > 0, xh_ref[...], 0.0)
        row = lax.broadcasted_iota(jnp.int32, (tm, tc), 0)
        row8 = lax.broadcasted_iota(jnp.int32, (8, tc), 0)
        db_ref[...] += _colsum(g)
        dw_ref[3:4, :] += _colsum(g * x)
        g8 = g[0:8]
        acc = g * w[3:4]
        corr = jnp.zeros((8, tc), F32)
        for j in range(1, SSM_CONV):
            xs = pltpu.roll(x, j, 0)
            dwj = _colsum(jnp.where(row >= j, g * xs, 0.0))
            dwj = dwj + _colsum(jnp.where(row8 < j, g8 * pltpu.roll(halo, j, 0), 0.0))
            dw_ref[3 - j:4 - j, :] += dwj
            gs = pltpu.roll(g, tm - j, 0)
            acc = acc + w[3 - j:4 - j] * jnp.where(row < tm - j, gs, 0.0)
            corr = corr + w[3 - j:4 - j] * jnp.where(row8 >= 8 - j, pltpu.roll(gn, 8 - j, 0), 0.0)
        scr[...] = acc
        scr[tm - 8:tm, :] += corr
        dx_ref[...] = scr[...].astype(dx_ref.dtype)

    tile = pl.BlockSpec((tm, tc), lambda jc, i: (i, jc))
    prev8 = pl.BlockSpec((8, tc), lambda jc, i: (jnp.maximum(i * hb - 1, 0), jc))
    next8 = pl.BlockSpec((8, tc), lambda jc, i: (jnp.minimum((i + 1) * hb, n * hb - 1), jc))
    return _pcall(body, name=name, grid=(c // tc, n),
                  in_specs=[tile, tile, tile, prev8, next8, next8, pl.BlockSpec((8, tc), lambda jc, i: (0, jc))],
                  out_specs=[tile, pl.BlockSpec((8, tc), lambda jc, i: (0, jc)), pl.BlockSpec((1, tc), lambda jc, i: (0, jc))],
                  out_shape=[_sds((s, c), BF16), _sds((8, c), F32), _sds((1, c), F32)],
                  scratch=[pltpu.VMEM((tm, tc), F32)],
                  sem=("parallel", "arbitrary"))(dxa, xc, xpre, xpre, dxa, xc, w8)


def _ssd_pre(dtraw, dt_bias, a_log, name):
    def fn(raw, bias, alog):
        tm = raw.shape[0]
        z = raw + bias
        dt = jnp.maximum(z, 0.0) + jnp.log(1.0 + jnp.exp(-jnp.abs(z)))
        lane = lax.broadcasted_iota(jnp.int32, (tm, LANES), 1)
        pos = lax.broadcasted_iota(jnp.int32, (tm, LANES), 0) & (SSM_CHUNK - 1)
        dt = jnp.where(lane < SSM_HEADS, dt, 0.0)
        c = dt * (-jnp.exp(alog))
        sh = 1
        while sh < SSM_CHUNK:
            c = c + jnp.where(pos >= sh, pltpu.roll(c, sh, 0), 0.0)
            sh *= 2
        return dt, c
    return _rowwise(fn, [dtraw], [dt_bias, a_log], [(LANES, F32), (LANES, F32)], [], name=name)


def _ssd_post(dacs, ddt, dtraw, dt, dt_bias, a_log, name):
    def fn(dacs_v, ddt_v, raw, dt_v, bias, alog):
        tm = raw.shape[0]
        pos = lax.broadcasted_iota(jnp.int32, (tm, LANES), 0) & (SSM_CHUNK - 1)
        a = -jnp.exp(alog)
        c = dacs_v
        sh = 1
        while sh < SSM_CHUNK:
            c = c + jnp.where(pos + sh < SSM_CHUNK, pltpu.roll(c, tm - sh, 0), 0.0)
            sh *= 2
        draw = (ddt_v + c * a) * _sigmoid(raw + bias)
        return draw, _colsum(draw), _colsum(c * dt_v * a)
    return _rowwise(fn, [dacs, ddt, dtraw, dt], [dt_bias, a_log], [(LANES, BF16)], [(1, LANES)] * 2, name=name)


def _heads_cols(x, s):
    return x[:, :SSM_HEADS].reshape(s, SSM_GROUPS, 4).transpose(1, 0, 2)


def _heads_rows(x, s):
    return x[:, :SSM_HEADS].reshape(s // SSM_CHUNK, SSM_CHUNK, SSM_GROUPS, 4).transpose(2, 0, 3, 1)


def _expand_heads(cols, lane):
    out = cols[:, 3:4]
    for r in (2, 1, 0):
        out = jnp.where(lane < HEAD_DIM * (r + 1), cols[:, r:r + 1], out)
    return out


def _ssd_common(x, dtc, acsc, acsr):
    l = SSM_CHUNK
    lane = lax.broadcasted_iota(jnp.int32, (l, SSM_GROUP_WIDTH), 1)
    dt_e = _expand_heads(dtc, lane)
    acs_e = _expand_heads(acsc, lane)
    last = acsr[:, l - 1:l]
    lane1 = lax.broadcasted_iota(jnp.int32, (1, SSM_GROUP_WIDTH), 1)
    last_e = last[3:4, :]
    for r in (2, 1, 0):
        last_e = jnp.where(lane1 < HEAD_DIM * (r + 1), last[r:r + 1, :], last_e)
    e_e = jnp.exp(acs_e)
    dte_e = jnp.exp(last_e - acs_e)
    rowg = lax.broadcasted_iota(jnp.int32, (SSM_GROUP_WIDTH, SSM_STATE), 0)
    cd = jnp.exp(last)
    cd_mat = cd[3:4, :]
    for r in (2, 1, 0):
        cd_mat = jnp.where(rowg < HEAD_DIM * (r + 1), cd[r:r + 1, :], cd_mat)
    return lane, dt_e, e_e, dte_e, cd, cd_mat


def _ssd_fwd(xa, dtc, acsc, acsr, d_e, name):
    s = xa.shape[0]
    l, gw, ns = SSM_CHUNK, SSM_GROUP_WIDTH, SSM_STATE
    nc = s // l
    xb, bb = 2048 // gw, 2048 // ns

    def body(x_ref, b_ref, c_ref, dtc_ref, acsc_ref, acsr_ref, d_ref, y_ref, hp_ref, h_sc):
        @pl.when(pl.program_id(1) == 0)
        def _():
            h_sc[...] = jnp.zeros_like(h_sc)
        x = x_ref[...]
        bm, cm = b_ref[...].astype(BF16), c_ref[...].astype(BF16)
        acsc, acsr = acsc_ref[...], acsr_ref[...]
        lane, dt_e, e_e, dte_e, _, cd_mat = _ssd_common(x, dtc_ref[...], acsc, acsr)
        xdt = x * dt_e
        xdt16 = xdt.astype(BF16)
        cb = _dot(cm, bm, NT_DIMS)
        tril = lax.broadcasted_iota(jnp.int32, (l, l), 0) >= lax.broadcasted_iota(jnp.int32, (l, l), 1)
        yd = jnp.zeros((l, gw), F32)
        for r in range(4):
            lm = jnp.exp(jnp.where(tril, acsc[:, r:r + 1] - acsr[r:r + 1, :], -jnp.inf))
            yr = _dot((cb * lm).astype(BF16), xdt16)
            yd = jnp.where((lane >= HEAD_DIM * r) & (lane < HEAD_DIM * (r + 1)), yr, yd)
        hp = h_sc[...]
        hp_ref[...] = hp
        yoff = _dot(cm, hp.astype(BF16), NT_DIMS) * e_e
        y_ref[...] = yd + yoff + x * d_ref[...]
        st = _dot((xdt * dte_e).astype(BF16), bm, TN_DIMS)
        h_sc[...] = hp * cd_mat + st

    return _pcall(
        body, name=name, grid=(SSM_GROUPS, nc),
        in_specs=[pl.BlockSpec((l, gw), lambda g, c: (c, g)),
                  pl.BlockSpec((l, ns), lambda g, c: (c, bb + g)),
                  pl.BlockSpec((l, ns), lambda g, c: (c, bb + SSM_GROUPS + g)),
                  pl.BlockSpec((None, l, 4), lambda g, c: (g, c, 0)),
                  pl.BlockSpec((None, l, 4), lambda g, c: (g, c, 0)),
                  pl.BlockSpec((None, None, 4, l), lambda g, c: (g, c, 0, 0)),
                  pl.BlockSpec((None, 1, gw), lambda g, c: (g, 0, 0))],
        out_specs=[pl.BlockSpec((l, gw), lambda g, c: (c, g)),
                   pl.BlockSpec((None, None, gw, ns), lambda g, c: (g, c, 0, 0))],
        out_shape=[_sds((s, xb * gw), F32), _sds((SSM_GROUPS, nc, gw, ns), F32)],
        scratch=[pltpu.VMEM((gw, ns), F32)],
        sem=("parallel", "arbitrary"))(xa, xa, xa, dtc, acsc, acsr, d_e)


def _ssd_bwd(dy, xa, dtc, acsc, acsr, d_e, hprev, name):
    s = xa.shape[0]
    l, gw, ns = SSM_CHUNK, SSM_GROUP_WIDTH, SSM_STATE
    nc = s // l
    bb = 2048 // ns

    def body(dy_ref, x_ref, b_ref, c_ref, dtc_ref, acsc_ref, acsr_ref, d_ref, hp_ref,
             dx_ref, db_ref, dc_ref, dacs_ref, ddt_ref, dd_ref, dh_sc):
        @pl.when(pl.program_id(1) == 0)
        def _():
            dh_sc[...] = jnp.zeros_like(dh_sc)
            dd_ref[...] = jnp.zeros_like(dd_ref)
        dyv, x = dy_ref[...], x_ref[...]
        bm, cm = b_ref[...].astype(BF16), c_ref[...].astype(BF16)
        acsc, acsr = acsc_ref[...], acsr_ref[...]
        lane, dt_e, e_e, dte_e, cd, cd_mat = _ssd_common(x, dtc_ref[...], acsc, acsr)
        xdt = x * dt_e
        xdt16 = xdt.astype(BF16)
        dy16 = dyv.astype(BF16)
        cb = _dot(cm, bm, NT_DIMS)
        cbt = _dot(bm, cm, NT_DIMS)
        hp = hp_ref[...]
        hp16 = hp.astype(BF16)
        g = dh_sc[...]
        g16 = g.astype(BF16)
        t_all = _dot(cm, hp16, NT_DIMS)
        dt16 = (dyv * e_e).astype(BF16)
        dc = _dot(dt16, hp16)
        dhp = _dot(dt16, cm, TN_DIMS)
        yoff_term = dyv * t_all * e_e
        wv = xdt * dte_e
        dw = _dot(bm, g16, NT_DIMS)
        db = _dot(wv.astype(BF16), g16)
        dxdt = dw * dte_e
        dte_term = dw * wv
        gh = g * hp
        dh_sc[...] = g * cd_mat + dhp
        ri = lax.broadcasted_iota(jnp.int32, (l, l), 0)
        ci = lax.broadcasted_iota(jnp.int32, (l, l), 1)
        tril, triu = ri >= ci, ci >= ri
        dcb = jnp.zeros((l, l), F32)
        dcbt = jnp.zeros((l, l), F32)
        q_rows, q_cols = [], []
        for r in range(4):
            in_head = (lane >= HEAD_DIM * r) & (lane < HEAD_DIM * (r + 1))
            lm = jnp.exp(jnp.where(tril, acsc[:, r:r + 1] - acsr[r:r + 1, :], -jnp.inf))
            lmt = jnp.exp(jnp.where(triu, acsr[r:r + 1, :] - acsc[:, r:r + 1], -jnp.inf))
            mm_, mt = cb * lm, cbt * lmt
            dyr = jnp.where(in_head, dy16, jnp.zeros_like(dy16))
            dm = _dot(dyr, xdt16, NT_DIMS)
            dmt = _dot(xdt16, dyr, NT_DIMS)
            dxdt = dxdt + jnp.where(in_head, _dot(mt.astype(BF16), dy16), 0.0)
            dcb = dcb + dm * lm
            dcbt = dcbt + dmt * lmt
            q_rows.append(jnp.sum(dm * mm_, axis=1, keepdims=True))
            q_cols.append(jnp.sum(dmt * mt, axis=1, keepdims=True))
        dc = dc + _dot(dcb.astype(BF16), bm)
        db = db + _dot(dcbt.astype(BF16), cm)
        dxdt_x = dxdt * x
        dy_x = dyv * x
        rowc = lax.broadcasted_iota(jnp.int32, (l, 1), 0)
        lane128 = lax.broadcasted_iota(jnp.int32, (1, LANES), 1)
        dd_row = jnp.zeros((1, LANES), F32)
        for r in range(4):
            in_head = (lane >= HEAD_DIM * r) & (lane < HEAD_DIM * (r + 1))
            seg = lambda v: jnp.sum(jnp.where(in_head, v, 0.0), axis=1, keepdims=True)
            s1, s2, s3 = seg(yoff_term), seg(dte_term), seg(dxdt_x)
            dcd = jnp.sum(_colsum(gh[HEAD_DIM * r:HEAD_DIM * (r + 1), :]), axis=1, keepdims=True)
            last_add = _colsum(s2) + dcd * cd[r:r + 1, :]
            dacs_r = q_rows[r] - q_cols[r] + s1 - s2 + jnp.where(rowc == l - 1, last_add, 0.0)
            dacs_ref[:, r:r + 1] = dacs_r
            ddt_ref[:, r:r + 1] = s3
            dd_row = dd_row + jnp.where(lane128 == r, _colsum(seg(dy_x)), 0.0)
        dd_ref[0:1, :] += dd_row
        dx_ref[...] = dxdt * dt_e + dyv * d_ref[...]
        db_ref[...] = db
        dc_ref[...] = dc

    rc = lambda c: nc - 1 - c
    return _pcall(
        body, name=name, grid=(SSM_GROUPS, nc),
        in_specs=[pl.BlockSpec((l, gw), lambda g, c: (rc(c), g)),
                  pl.BlockSpec((l, gw), lambda g, c: (rc(c), g)),
                  pl.BlockSpec((l, ns), lambda g, c: (rc(c), bb + g)),
                  pl.BlockSpec((l, ns), lambda g, c: (rc(c), bb + SSM_GROUPS + g)),
                  pl.BlockSpec((None, l, 4), lambda g, c: (g, rc(c), 0)),
                  pl.BlockSpec((None, l, 4), lambda g, c: (g, rc(c), 0)),
                  pl.BlockSpec((None, None, 4, l), lambda g, c: (g, rc(c), 0, 0)),
                  pl.BlockSpec((None, 1, gw), lambda g, c: (g, 0, 0)),
                  pl.BlockSpec((None, None, gw, ns), lambda g, c: (g, rc(c), 0, 0))],
        out_specs=[pl.BlockSpec((l, gw), lambda g, c: (rc(c), g)),
                   pl.BlockSpec((l, ns), lambda g, c: (rc(c), g)),
                   pl.BlockSpec((l, ns), lambda g, c: (rc(c), g)),
                   pl.BlockSpec((None, l, 4), lambda g, c: (g, rc(c), 0)),
                   pl.BlockSpec((None, l, 4), lambda g, c: (g, rc(c), 0)),
                   pl.BlockSpec((None, 8, LANES), lambda g, c: (g, 0, 0))],
        out_shape=[_sds((s, 2048), F32), _sds((s, SSM_GROUPS * ns), F32), _sds((s, SSM_GROUPS * ns), F32),
                   _sds((SSM_GROUPS, s, 4), F32), _sds((SSM_GROUPS, s, 4), F32), _sds((SSM_GROUPS, 8, LANES), F32)],
        scratch=[pltpu.VMEM((gw, ns), F32)],
        sem=("parallel", "arbitrary"))(dy, xa, xa, xa, dtc, acsc, acsr, d_e, hprev)


def _gate_norm(y, z, nw, name):
    c = y.shape[1]

    def fn(yv, zv, w):
        outs = []
        for k in range(c // SSM_GROUP_WIDTH):
            sl = slice(k * SSM_GROUP_WIDTH, (k + 1) * SSM_GROUP_WIDTH)
            yg = yv[:, sl] * _silu(zv[:, sl])
            rinv = lax.rsqrt(jnp.mean(yg * yg, axis=-1, keepdims=True) + RMS_EPS)
            outs.append(yg * rinv * w[:, sl])
        return (jnp.concatenate(outs, axis=1),)
    return _rowwise(fn, [y, z], [nw], [(c, BF16)], [], name=name)[0]


def _gate_norm_bwd(dyn, y, z, nw, name):
    c = y.shape[1]

    def fn(dv, yv, zv, w):
        dys, dzs, dws = [], [], []
        for k in range(c // SSM_GROUP_WIDTH):
            sl = slice(k * SSM_GROUP_WIDTH, (k + 1) * SSM_GROUP_WIDTH)
            ys, zs, ds = yv[:, sl], zv[:, sl], dv[:, sl]
            sz = _silu(zs)
            yg = ys * sz
            rinv = lax.rsqrt(jnp.mean(yg * yg, axis=-1, keepdims=True) + RMS_EPS)
            nrm = yg * rinv
            dn = ds * w[:, sl]
            dyg = rinv * (dn - nrm * jnp.mean(dn * nrm, axis=-1, keepdims=True))
            dys.append(dyg * sz)
            dzs.append(dyg * ys * _dsilu(zs))
            dws.append(_colsum(ds * nrm))
        return jnp.concatenate(dys, axis=1), jnp.concatenate(dzs, axis=1), jnp.concatenate(dws, axis=1)
    return _rowwise(fn, [dyn, y, z], [nw], [(c, F32), (c, BF16)], [(1, c)], name=name, tm=128)


def _ssd_forward(u, w):
    s = u.shape[0]
    z = _mm(u, w["ssm_z"], "nn", [F32], name="ssm_zproj")
    xpre = _mm(u, w["ssm_xbc"], "nn", [F32], name="ssm_xproj")
    dtraw = _mm(u, w["ssm_dt"], "nn", [F32], name="ssm_dtproj")
    xc, xa = _conv_fwd(xpre, w["conv_w8"], w["conv_b"], "ssm_conv")
    dt, acs = _ssd_pre(dtraw, w["dt_bias"], w["a_log"], "ssm_pre")
    dtc, acsc, acsr = _heads_cols(dt, s), _heads_cols(acs, s), _heads_rows(acs, s)
    y, hprev = _ssd_fwd(xa, dtc, acsc, acsr, w["d_e"], "ssm_scan")
    yn = _gate_norm(y, z, w["norm_w"], "ssm_gate_norm")
    out = _mm(yn, w["ssm_out"], "nn", [F32], name="ssm_oproj")
    return out, dict(u=u, z=z, xpre=xpre, xc=xc, xa=xa, dtraw=dtraw, dt=dt, dtc=dtc, acsc=acsc, acsr=acsr,
                     y=y, hprev=hprev, yn=yn)


def _ssd_backward(dy, sv, w):
    s = dy.shape[0]
    dyn = _mm(dy, w["ssm_out"], "nt", [F32], name="ssm_dyn")
    g_wout = _mm(sv["yn"], dy, "tn", [F32], name="ssm_gwout")
    dys, dz, dnw = _gate_norm_bwd(dyn, sv["y"], sv["z"], w["norm_w"], "ssm_gate_norm_bwd")
    dx, dbm, dcm, dacs_c, ddt_c, dd = _ssd_bwd(dys, sv["xa"], sv["dtc"], sv["acsc"], sv["acsr"], w["d_e"],
                                               sv["hprev"], "ssm_scan_bwd")
    pad = ((0, 0), (0, LANES - SSM_HEADS))
    dacs = jnp.pad(dacs_c.transpose(1, 0, 2).reshape(s, SSM_HEADS), pad)
    ddt = jnp.pad(ddt_c.transpose(1, 0, 2).reshape(s, SSM_HEADS), pad)
    draw, dbias, dalog = _ssd_post(dacs, ddt, sv["dtraw"], sv["dt"], w["dt_bias"], w["a_log"], "ssm_post")
    dxa = jnp.concatenate([dx, dbm, dcm], axis=1)
    dxpre, dcw, dcb = _conv_bwd(dxa, sv["xc"], sv["xpre"], w["conv_w8"], "ssm_conv_bwd")
    dproj = jnp.concatenate([dz, dxpre, draw], axis=1)
    du = _mm(dproj, w["ssm_in_pad"], "nt", [F32], name="ssm_du", tk=896)
    g_win = _mm(sv["u"], dproj, "tn", [F32], name="ssm_gwin", tn=896)
    n_in = 2048 + 4096 + SSM_HEADS
    return du, dict(ssm_w_in=g_win[:, :n_in], ssm_w_out=g_wout, ssm_conv_w=dcw[:SSM_CONV], ssm_conv_b=dcb,
                    ssm_norm_w=dnw, ssm_dt_bias=dbias[:, :SSM_HEADS], ssm_a_log=dalog[:, :SSM_HEADS],
                    ssm_d=dd[:, 0, :4].reshape(1, SSM_HEADS))


def _mlp_forward(u2, w1, w2, tag):
    def epi(acc):
        hr = jnp.maximum(acc, 0.0)
        return hr, hr * hr
    hr, a = _mm(u2, w1, "nn", [BF16, BF16], name=tag + "_mlp_up", epi=epi)
    y2 = _mm(a, w2, "nn", [F32], name=tag + "_mlp_down")
    return y2, hr, a


def _mlp_backward(dy2, u2, hr, a, w1, w2, tag):
    dh = _mm(dy2, w2, "nt", [BF16], name=tag + "_mlp_dh", extra=(hr,),
             epi=lambda acc, h: (acc * (2.0 * h.astype(F32)),))
    g_w2 = _mm(a, dy2, "tn", [F32], name=tag + "_mlp_gw2")
    g_w1 = _mm(u2, dh, "tn", [F32], name=tag + "_mlp_gw1")
    du2 = _mm(dh, w1, "nt", [F32], name=tag + "_mlp_du")
    return du2, g_w1, g_w2


def _ada_forward(c16, ada_w, ada_b_cols, name):
    nl, d, cols = ada_w.shape
    tn = 512

    def body(c_ref, w_ref, b_ref, o_ref):
        cond = _silu(c_ref[...]).astype(BF16)
        o_ref[...] = _dot(cond, w_ref[...].astype(BF16)) + b_ref[...]

    return _pcall(body, name=name, grid=(nl, cols // tn),
                  in_specs=[pl.BlockSpec((16, d), lambda i, j: (0, 0)),
                            pl.BlockSpec((None, d, tn), lambda i, j: (i, 0, j)),
                            pl.BlockSpec((None, 1, tn), lambda i, j: (i, 0, j))],
                  out_specs=pl.BlockSpec((None, 16, tn), lambda i, j: (i, 0, j)),
                  out_shape=_sds((nl, 16, cols), F32), sem=("parallel", "parallel"))(c16, ada_w, ada_b_cols)


def _ada_backward(c_t, dmod_cols, name):
    d, nb = c_t.shape
    nl, _, cols = dmod_cols.shape
    tn = 512

    def body(c_ref, dm_ref, o_ref):
        cond = _silu(c_ref[...])
        dm = dm_ref[...]
        acc = cond[:, 0:1] * dm[0:1, :]
        for b in range(1, nb):
            acc = acc + cond[:, b:b + 1] * dm[b:b + 1, :]
        o_ref[...] = acc

    return _pcall(body, name=name, grid=(nl, cols // tn),
                  in_specs=[pl.BlockSpec((d, nb), lambda i, j: (0, 0)),
                            pl.BlockSpec((None, nb, tn), lambda i, j: (i, 0, j))],
                  out_specs=pl.BlockSpec((None, d, tn), lambda i, j: (i, 0, j)),
                  out_shape=_sds((nl, d, cols), F32), sem=("parallel", "parallel"))(c_t, dmod_cols)


def _adamw(w, g, m, v, name):
    rows, cols = w.shape
    tm = rows
    for cand in (256, 128, 64, 32, 16, 8):
        if rows % cand == 0 and rows > cand:
            tm = cand
            break
    c1 = 1.0 / (1.0 - ADAM_B1 ** ADAM_STEP)
    c2 = 1.0 / (1.0 - ADAM_B2 ** ADAM_STEP)

    def fn(wv, gv, mv, vv):
        mn = ADAM_B1 * mv + (1.0 - ADAM_B1) * gv
        vn = ADAM_B2 * vv + (1.0 - ADAM_B2) * (gv * gv)
        delta = -ADAM_LR * ((mn * c1) / (jnp.sqrt(vn * c2) + ADAM_EPS) + ADAM_WD * wv)
        return delta, mn, vn
    return _rowwise(fn, [w, g, m, v], [], [(cols, F32)] * 3, [], name=name, tm=tm)


def _my_pos():
    return lax.axis_index("x"), lax.axis_index("y"), lax.axis_index("c")


def _allgather8(x, name):
    r, c = x.shape

    def body(x_ref, out_ref, send_sems, recv_sems, local_sem):
        mx, my, mc = _my_pos()
        me = 4 * mx + 2 * my + mc
        mine = pltpu.make_async_copy(x_ref, out_ref.at[me], local_sem)
        mine.start()
        copies = []
        for k in range(1, 8):
            fx, fy, fc = (k >> 2) & 1, (k >> 1) & 1, k & 1
            px = 1 - mx if fx else mx
            py = 1 - my if fy else my
            pc = 1 - mc if fc else mc
            peer = 4 * px + 2 * py + pc
            send = pltpu.make_async_remote_copy(src_ref=x_ref, dst_ref=out_ref.at[me], send_sem=send_sems.at[k - 1],
                                                recv_sem=recv_sems.at[k - 1], device_id=(px, py, pc),
                                                device_id_type=MESH)
            send.start()
            recv = pltpu.make_async_remote_copy(src_ref=x_ref, dst_ref=out_ref.at[peer], send_sem=send_sems.at[k - 1],
                                                recv_sem=recv_sems.at[k - 1], device_id=(px, py, pc),
                                                device_id_type=MESH)
            copies.append((send, recv))
        for send, recv in copies:
            recv.wait_recv()
        for send, recv in copies:
            send.wait_send()
        mine.wait()

    vm = pl.BlockSpec(memory_space=pltpu.VMEM)
    return _pcall(body, name=name, in_specs=[vm], out_specs=vm, out_shape=_sds((8, r, c), x.dtype),
                  scratch=[pltpu.SemaphoreType.DMA((7,)), pltpu.SemaphoreType.DMA((7,)), pltpu.SemaphoreType.DMA])(x)


def _chip_flips(mx, my):
    out = []
    for fx, fy in ((1, 0), (0, 1), (1, 1)):
        px = 1 - mx if fx else mx
        py = 1 - my if fy else my
        out.append((px, py, 2 * px + py))
    return out


def _gather_chips(shard, name):
    r, c = shard.shape

    def body(x_ref, out_ref, send_sems, recv_sems, local_sem):
        mx, my, mc = _my_pos()
        mk = 2 * mx + my
        mine = pltpu.make_async_copy(x_ref, out_ref.at[mk], local_sem)
        mine.start()
        copies = []
        for j, (px, py, pk) in enumerate(_chip_flips(mx, my)):
            send = pltpu.make_async_remote_copy(src_ref=x_ref, dst_ref=out_ref.at[mk], send_sem=send_sems.at[j],
                                                recv_sem=recv_sems.at[j], device_id=(px, py, mc), device_id_type=MESH)
            send.start()
            recv = pltpu.make_async_remote_copy(src_ref=x_ref, dst_ref=out_ref.at[pk], send_sem=send_sems.at[j],
                                                recv_sem=recv_sems.at[j], device_id=(px, py, mc), device_id_type=MESH)
            copies.append((send, recv))
        for send, recv in copies:
            recv.wait_recv()
        for send, recv in copies:
            send.wait_send()
        mine.wait()

    return _pcall(body, name=name, in_specs=[HBM_SPEC], out_specs=HBM_SPEC, out_shape=_sds((4, r, c), shard.dtype),
                  scratch=[pltpu.SemaphoreType.DMA((3,)), pltpu.SemaphoreType.DMA((3,)), pltpu.SemaphoreType.DMA])(shard)


def _pair_exchange(g4, name):
    n, _, h, c = g4.shape

    def body(g_ref, out_ref, send_sem, recv_sem):
        mx, my, mc = _my_pos()
        oc = 1 - mc
        copies = []
        for k in range(n):
            cp = pltpu.make_async_remote_copy(src_ref=g_ref.at[k, oc], dst_ref=out_ref.at[k], send_sem=send_sem.at[k],
                                              recv_sem=recv_sem.at[k], device_id=(mx, my, oc), device_id_type=MESH)
            cp.start()
            copies.append(cp)
        for cp in copies:
            cp.wait_recv()
        for cp in copies:
            cp.wait_send()

    return _pcall(body, name=name, in_specs=[HBM_SPEC], out_specs=HBM_SPEC, out_shape=_sds((n, h, c), g4.dtype),
                  scratch=[pltpu.SemaphoreType.DMA((n,)), pltpu.SemaphoreType.DMA((n,))])(g4)


def _pair_add(g4, recv, core, name):
    n, _, h, c = g4.shape
    tm = 8
    for cand in (400, 200, 120, 80, 40, 16, 8):
        if h % cand == 0:
            tm = cand
            break

    def body(core_ref, a_ref, b_ref, o_ref):
        o_ref[...] = a_ref[...] + b_ref[...]

    grid_spec = pltpu.PrefetchScalarGridSpec(
        num_scalar_prefetch=1, grid=(n, h // tm),
        in_specs=[pl.BlockSpec((None, None, tm, c), lambda k, i, cr: (k, cr[0], i, 0)),
                  pl.BlockSpec((None, tm, c), lambda k, i, cr: (k, i, 0))],
        out_specs=pl.BlockSpec((None, tm, c), lambda k, i, cr: (k, i, 0)))
    return pl.pallas_call(body, out_shape=_sds((n, h, c), F32), grid_spec=grid_spec, name=name,
                          compiler_params=pltpu.CompilerParams(vmem_limit_bytes=VMEM_LIMIT_BYTES,
                                                               dimension_semantics=("parallel", "parallel")))(core, g4, recv)


def _chip_exchange(p, name):
    n, h, c = p.shape

    def body(p_ref, out_ref, send_sems, recv_sems, local_sem):
        mx, my, mc = _my_pos()
        mk = 2 * mx + my
        mine = pltpu.make_async_copy(p_ref.at[mk], out_ref.at[mk], local_sem)
        mine.start()
        copies = []
        for j, (px, py, pk) in enumerate(_chip_flips(mx, my)):
            send = pltpu.make_async_remote_copy(src_ref=p_ref.at[pk], dst_ref=out_ref.at[mk], send_sem=send_sems.at[j],
                                                recv_sem=recv_sems.at[j], device_id=(px, py, mc), device_id_type=MESH)
            send.start()
            recv = pltpu.make_async_remote_copy(src_ref=p_ref.at[pk], dst_ref=out_ref.at[pk], send_sem=send_sems.at[j],
                                                recv_sem=recv_sems.at[j], device_id=(px, py, mc), device_id_type=MESH)
            copies.append((send, recv))
        for send, recv in copies:
            recv.wait_recv()
        for send, recv in copies:
            send.wait_send()
        mine.wait()

    return _pcall(body, name=name, in_specs=[HBM_SPEC], out_specs=HBM_SPEC, out_shape=_sds((n, h, c), p.dtype),
                  scratch=[pltpu.SemaphoreType.DMA((3,)), pltpu.SemaphoreType.DMA((3,)), pltpu.SemaphoreType.DMA])(p)


def _sum_slots(q, name):
    n, h, c = q.shape
    tm = 8
    for cand in (400, 200, 120, 80, 40, 16, 8):
        if h % cand == 0:
            tm = cand
            break

    def body(q_ref, o_ref):
        acc = q_ref[0]
        for k in range(1, n):
            acc = acc + q_ref[k]
        o_ref[...] = acc

    return _pcall(body, name=name, grid=(h // tm,),
                  in_specs=[pl.BlockSpec((n, tm, c), lambda i: (0, i, 0))],
                  out_specs=pl.BlockSpec((tm, c), lambda i: (i, 0)),
                  out_shape=_sds((h, c), F32), sem=("parallel",))(q)


def _pair_share(f, name):
    h, c = f.shape

    def body(f_ref, out_ref, send_sem, recv_sem, local_sem):
        mx, my, mc = _my_pos()
        oc = 1 - mc
        mine = pltpu.make_async_copy(f_ref, out_ref.at[mc], local_sem)
        mine.start()
        send = pltpu.make_async_remote_copy(src_ref=f_ref, dst_ref=out_ref.at[mc], send_sem=send_sem, recv_sem=recv_sem,
                                            device_id=(mx, my, oc), device_id_type=MESH)
        send.start()
        recv = pltpu.make_async_remote_copy(src_ref=f_ref, dst_ref=out_ref.at[oc], send_sem=send_sem, recv_sem=recv_sem,
                                            device_id=(mx, my, oc), device_id_type=MESH)
        recv.wait_recv()
        send.wait_send()
        mine.wait()

    return _pcall(body, name=name, in_specs=[HBM_SPEC], out_specs=HBM_SPEC, out_shape=_sds((2, h, c), f.dtype),
                  scratch=[pltpu.SemaphoreType.DMA, pltpu.SemaphoreType.DMA, pltpu.SemaphoreType.DMA])(f)


BIG = ("mlp_w1", "mlp_w2", "fox_w_in", "fox_w_o", "ssm_w_in", "ssm_w_out")
SMALL_SHARDED = ("ssm_conv_w", "ssm_conv_b", "ssm_norm_w")
PACK_COLS = 1024


def _pack_rows(parts, rows_multiple, dtype):
    flat = jnp.concatenate([p.reshape(-1).astype(dtype) for p in parts])
    unit = rows_multiple * PACK_COLS
    total = -(-flat.shape[0] // unit) * unit
    flat = jnp.pad(flat, (0, total - flat.shape[0]))
    return flat.reshape(total // PACK_COLS, PACK_COLS)


def _unpack(flat, shapes):
    out, off = [], 0
    for sh in shapes:
        n = 1
        for d_ in sh:
            n *= d_
        out.append(flat[off:off + n].reshape(sh))
        off += n
    return out


def _chip_slice(full, axis, k, width):
    idx = [slice(None)] * full.ndim
    idx[axis] = slice(k * width, (k + 1) * width)
    return full[tuple(idx)]


SHARD_AXIS = dict(mlp_w1=2, mlp_w2=1, fox_w_in=2, fox_w_o=1, ssm_w_in=2, ssm_w_out=1, ssm_conv_w=2,
                  ssm_conv_b=1, ssm_norm_w=1, ada_w=2)


def kernel(x, c, ada_w, ada_b, ln_mix_g, ln_mix_b, ln_mlp_g, ln_mlp_b, mlp_w1, mlp_w2, fox_w_in, fox_b_f, fox_w_o, ssm_w_in, ssm_conv_w, ssm_conv_b, ssm_dt_bias, ssm_a_log, ssm_d, ssm_norm_w, ssm_w_out, loss_target, m_ada_w, m_ada_b, m_ln_mix_g, m_ln_mix_b, m_ln_mlp_g, m_ln_mlp_b, m_mlp_w1, m_mlp_w2, m_fox_w_in, m_fox_b_f, m_fox_w_o, m_ssm_w_in, m_ssm_conv_w, m_ssm_conv_b, m_ssm_dt_bias, m_ssm_a_log, m_ssm_d, m_ssm_norm_w, m_ssm_w_out, v_ada_w, v_ada_b, v_ln_mix_g, v_ln_mix_b, v_ln_mlp_g, v_ln_mlp_b, v_mlp_w1, v_mlp_w2, v_fox_w_in, v_fox_b_f, v_fox_w_o, v_ssm_w_in, v_ssm_conv_w, v_ssm_conv_b, v_ssm_dt_bias, v_ssm_a_log, v_ssm_d, v_ssm_norm_w, v_ssm_w_out):
    names = ("ada_w", "ada_b", "ln_mix_g", "ln_mix_b", "ln_mlp_g", "ln_mlp_b", "mlp_w1", "mlp_w2", "fox_w_in",
             "fox_b_f", "fox_w_o", "ssm_w_in", "ssm_conv_w", "ssm_conv_b", "ssm_dt_bias", "ssm_a_log", "ssm_d",
             "ssm_norm_w", "ssm_w_out")
    weights = dict(zip(names, (ada_w, ada_b, ln_mix_g, ln_mix_b, ln_mlp_g, ln_mlp_b, mlp_w1, mlp_w2, fox_w_in,
                               fox_b_f, fox_w_o, ssm_w_in, ssm_conv_w, ssm_conv_b, ssm_dt_bias, ssm_a_log, ssm_d,
                               ssm_norm_w, ssm_w_out)))
    m_in = dict(zip(names, (m_ada_w, m_ada_b, m_ln_mix_g, m_ln_mix_b, m_ln_mlp_g, m_ln_mlp_b, m_mlp_w1, m_mlp_w2,
                            m_fox_w_in, m_fox_b_f, m_fox_w_o, m_ssm_w_in, m_ssm_conv_w, m_ssm_conv_b, m_ssm_dt_bias,
                            m_ssm_a_log, m_ssm_d, m_ssm_norm_w, m_ssm_w_out)))
    v_in = dict(zip(names, (v_ada_w, v_ada_b, v_ln_mix_g, v_ln_mix_b, v_ln_mlp_g, v_ln_mlp_b, v_mlp_w1, v_mlp_w2,
                            v_fox_w_in, v_fox_b_f, v_fox_w_o, v_ssm_w_in, v_ssm_conv_w, v_ssm_conv_b, v_ssm_dt_bias,
                            v_ssm_a_log, v_ssm_d, v_ssm_norm_w, v_ssm_w_out)))

    mx, my, mc = _my_pos()
    chip = 2 * mx + my
    me = 4 * mx + 2 * my + mc
    x0 = x[0]
    target = loss_target[0]
    s, d = x0.shape
    n_qkv = 3 * FOX_HEADS * HEAD_DIM

    big_shapes = [weights[n].shape for n in BIG]
    packed = _pack_rows([weights[n] for n in BIG], 16, BF16)
    gathered = _gather_chips(packed, "gather_weights")
    per_chip = [_unpack(gathered[k].reshape(-1), big_shapes) for k in range(4)]
    full = {n: jnp.concatenate([per_chip[k][i] for k in range(4)], axis=SHARD_AXIS[n]) for i, n in enumerate(BIG)}

    small_shapes = [weights[n].shape for n in SMALL_SHARDED]
    small_packed = _pack_rows([weights[n] for n in SMALL_SHARDED] + [c], 8, F32).reshape(-1, LANES)
    small_all = _allgather8(small_packed, "gather_small")
    small_chip = [_unpack(small_all[2 * k].reshape(-1), small_shapes) for k in range(4)]
    small_full = {n: jnp.concatenate([small_chip[k][i] for k in range(4)], axis=SHARD_AXIS[n])
                  for i, n in enumerate(SMALL_SHARDED)}
    n_small = sum(weights[n].size for n in SMALL_SHARDED)
    c_all = small_all.reshape(8, -1)[:, n_small:n_small + d]

    cols = ada_w.shape[2]
    ada_b_cols = lax.dynamic_slice_in_dim(ada_b, chip * cols, cols, axis=1)[:, None, :]
    c16 = jnp.pad(c_all, ((0, 8), (0, 0)))
    mod_part = _ada_forward(c16, ada_w, ada_b_cols, "ada_fwd")[:, :8, :]
    mod_all = _allgather8(mod_part.reshape(-1, LANES), "gather_mod").reshape(8, DEPTH, 8, cols)
    mod_mine = jnp.stack([lax.dynamic_index_in_dim(mod_all[2 * k], me, axis=1, keepdims=False) for k in range(4)], axis=1)
    mods = [jnp.pad(mod_mine[i].reshape(6, d), ((0, 2), (0, 0))) for i in range(DEPTH)]

    w = dict(
        fox_qkv=full["fox_w_in"][0][:, :n_qkv],
        fox_f=jnp.pad(full["fox_w_in"][0][:, n_qkv:], ((0, 0), (0, LANES - FOX_HEADS))),
        fox_in_pad=jnp.pad(full["fox_w_in"][0], ((0, 0), (0, LANES - FOX_HEADS))),
        fox_o=full["fox_w_o"][0],
        fox_bf=jnp.pad(fox_b_f, ((0, 0), (0, LANES - FOX_HEADS))),
        head_ind=(jnp.arange(d)[:, None] // HEAD_DIM == jnp.arange(LANES)[None, :]).astype(BF16),
        ssm_z=full["ssm_w_in"][0][:, :2048],
        ssm_xbc=full["ssm_w_in"][0][:, 2048:6144],
        ssm_dt=jnp.pad(full["ssm_w_in"][0][:, 6144:], ((0, 0), (0, LANES - SSM_HEADS))),
        ssm_in_pad=jnp.pad(full["ssm_w_in"][0], ((0, 0), (0, LANES - SSM_HEADS))),
        ssm_out=full["ssm_w_out"][0],
        conv_w8=jnp.pad(small_full["ssm_conv_w"][0], ((0, 8 - SSM_CONV), (0, 0))),
        conv_b=small_full["ssm_conv_b"],
        norm_w=small_full["ssm_norm_w"],
        dt_bias=jnp.pad(ssm_dt_bias, ((0, 0), (0, LANES - SSM_HEADS))),
        a_log=jnp.pad(ssm_a_log, ((0, 0), (0, LANES - SSM_HEADS))),
        d_e=jnp.repeat(ssm_d.reshape(SSM_GROUPS, 4), HEAD_DIM, axis=1)[:, None, :],
    )
    mixers = ((_fox_forward, _fox_backward), (_ssd_forward, _ssd_backward))

    saved = []
    xin = x0
    for i in range(DEPTH):
        tag = "l%d" % i
        u = _modulate_in(xin, mods[i], tag + "_mod_in")
        y, sv = mixers[i % 2][0](u, w)
        r, x1, u2 = _res_ln_mod(xin, y, mods[i], ln_mix_g[i:i + 1], ln_mix_b[i:i + 1], tag + "_res_ln1")
        y2, hr, a = _mlp_forward(u2, full["mlp_w1"][i], full["mlp_w2"][i], tag)
        r2, x2 = _res_ln(x1, y2, mods[i], ln_mlp_g[i:i + 1], ln_mlp_b[i:i + 1], tag + "_res_ln2")
        saved.append(dict(xin=xin, y=y, r=r, u2=u2, hr=hr, a=a, y2=y2, r2=r2, mix=sv))
        xin = x2
    dx, sq = _loss_grad(xin, target, "loss_grad")
    loss = lax.psum(0.5 * jnp.sum(sq) / d, ("x", "y", "c"))

    grads = {}
    dmods = [None] * DEPTH
    ln_grads = {n: [None] * DEPTH for n in ("ln_mix_g", "ln_mix_b", "ln_mlp_g", "ln_mlp_b")}
    g_w1, g_w2 = [None] * DEPTH, [None] * DEPTH
    for i in reversed(range(DEPTH)):
        tag = "l%d" % i
        sv = saved[i]
        dy2, dres, dg2, db2, dgm = _ln2_bwd(dx, sv["r2"], sv["y2"], mods[i], ln_mlp_g[i:i + 1], tag + "_ln2_bwd")
        du2, g_w1[i], g_w2[i] = _mlp_backward(dy2, sv["u2"], sv["hr"], sv["a"], full["mlp_w1"][i], full["mlp_w2"][i], tag)
        dy, dres0, dscm, dshm, dg1, db1, dga = _ln1_bwd(du2, dres, sv["r"], sv["y"], mods[i], ln_mix_g[i:i + 1],
                                                        ln_mix_b[i:i + 1], tag + "_ln1_bwd")
        du, mg = mixers[i % 2][1](dy, sv["mix"], w)
        grads.update(mg)
        dx, dsca, dsha = _mod_in_bwd(du, dres0, sv["xin"], mods[i], tag + "_mod_in_bwd")
        dmods[i] = jnp.concatenate([dsha, dsca, dga, dshm, dscm, dgm], axis=1)
        ln_grads["ln_mix_g"][i], ln_grads["ln_mix_b"][i] = dg1, db1
        ln_grads["ln_mlp_g"][i], ln_grads["ln_mlp_b"][i] = dg2, db2
    grad_x = dx[None]
    grads["mlp_w1"] = jnp.stack(g_w1)
    grads["mlp_w2"] = jnp.stack(g_w2)
    for n in ("fox_w_in", "fox_w_o", "ssm_w_in", "ssm_w_out", "ssm_conv_w"):
        grads[n] = grads[n][None]

    small_names = ("ln_mix_g", "ln_mix_b", "ln_mlp_g", "ln_mlp_b", "fox_b_f", "ssm_dt_bias", "ssm_a_log", "ssm_d")
    small_parts = list(dmods)
    for n in small_names[:4]:
        small_parts.append(jnp.concatenate(ln_grads[n], axis=0))
    for n in small_names[4:]:
        small_parts.append(jnp.pad(grads[n], ((0, 0), (0, LANES - grads[n].shape[1]))))
    small_vec = _pack_rows(small_parts, 1, F32).reshape(-1, LANES)
    small_vec = jnp.pad(small_vec, ((0, -small_vec.shape[0] % 8), (0, 0)))
    small_g_all = _allgather8(small_vec, "gather_small_grads")
    small_sum = _sum_slots(small_g_all, "sum_small_grads").reshape(-1)
    dmod_sum = small_sum[:DEPTH * 6 * d].reshape(DEPTH, 6 * d)
    off = DEPTH * 6 * d
    final = {"ada_b": dmod_sum}
    for n in small_names[:4]:
        final[n] = small_sum[off:off + DEPTH * d].reshape(DEPTH, d)
        off += DEPTH * d
    for n in small_names[4:]:
        width = weights[n].shape[1]
        final[n] = small_sum[off:off + width].reshape(1, width)
        off += LANES

    dmod_all = small_g_all.reshape(8, -1)[:, :DEPTH * 6 * d].reshape(8, DEPTH, 6 * d)
    dmod_cols = lax.dynamic_slice_in_dim(dmod_all, chip * cols, cols, axis=2).transpose(1, 0, 2)
    final["ada_w"] = _ada_backward(c_all.T, dmod_cols, "ada_bwd")

    sharded = BIG + SMALL_SHARDED
    shard_shapes = [weights[n].shape for n in sharded]
    per_target = []
    for k in range(4):
        parts = [_chip_slice(grads[n], SHARD_AXIS[n], k, weights[n].shape[SHARD_AXIS[n]]) for n in sharded]
        per_target.append(_pack_rows(parts, 16, F32))
    g_all = jnp.stack(per_target)
    rows = g_all.shape[1]
    g4 = g_all.reshape(4, 2, rows // 2, PACK_COLS)
    recv = _pair_exchange(g4, "rs_pair_exchange")
    part = _pair_add(g4, recv, jnp.reshape(mc, (1,)).astype(jnp.int32), "rs_pair_add")
    slots = _chip_exchange(part, "rs_chip_exchange")
    half = _sum_slots(slots, "rs_chip_sum")
    both = _pair_share(half, "rs_pair_share")
    for n, g_shard in zip(sharded, _unpack(both.reshape(-1), shard_shapes)):
        final[n] = g_shard

    outs_g, outs_d, outs_m, outs_v = [], [], [], []
    for n in names:
        wv = weights[n]
        two_d = (-1, wv.shape[-1])
        delta, mn, vn = _adamw(wv.reshape(two_d), final[n].reshape(two_d), m_in[n].reshape(two_d),
                               v_in[n].reshape(two_d), "adamw_" + n)
        outs_g.append(final[n].reshape(wv.shape))
        outs_d.append(delta.reshape(wv.shape))
        outs_m.append(mn.reshape(wv.shape))
        outs_v.append(vn.reshape(wv.shape))
    return (loss, grad_x, *outs_g, *outs_d, *outs_m, *outs_v)
```

```python
import functools

import jax
import jax.numpy as jnp
from jax import lax
from jax.experimental import pallas as pl
from jax.experimental.pallas import tpu as pltpu

F32, BF16 = jnp.float32, jnp.bfloat16
MESH = pl.DeviceIdType.MESH
HBM_SPEC = pl.BlockSpec(memory_space=pltpu.HBM)

VMEM_LIMIT_BYTES = 52 * 2**20
LANES = 128

FOX_HEADS, HEAD_DIM = 16, 64
SSM_HEADS, SSM_GROUPS, SSM_STATE, SSM_CHUNK, SSM_CONV = 32, 8, 128, 128, 4
SSM_GROUP_WIDTH = 256
LN_EPS, RMS_EPS = 1e-5, 1e-5
DEPTH = 2
ALPHA = (2.0 * DEPTH) ** 0.25
ADAM_LR, ADAM_B1, ADAM_B2, ADAM_EPS, ADAM_WD, ADAM_STEP = 0.001, 0.9, 0.999, 1e-08, 0.01, 10

ATT_TILE = 512
ROW_TILE = 256
SCAN_TILE = 512
MM_TM, MM_TN, MM_TK = 1024, 1024, 1024

NT_DIMS = (((1,), (1,)), ((), ()))
TN_DIMS = (((0,), (0,)), ((), ()))
NN_DIMS = (((1,), (0,)), ((), ()))


def _pcall(body, *, name, out_shape, grid=(), in_specs=None, out_specs=None, scratch=(), sem=None, prefetch=0):
    params = dict(vmem_limit_bytes=VMEM_LIMIT_BYTES)
    if sem is not None:
        params["dimension_semantics"] = sem
    if prefetch:
        grid_spec = pltpu.PrefetchScalarGridSpec(num_scalar_prefetch=prefetch, grid=grid, in_specs=in_specs,
                                                 out_specs=out_specs, scratch_shapes=scratch)
        return pl.pallas_call(body, out_shape=out_shape, grid_spec=grid_spec, name=name,
                              compiler_params=pltpu.CompilerParams(**params))
    kwargs = {}
    if in_specs is not None:
        kwargs["in_specs"] = in_specs
    if out_specs is not None:
        kwargs["out_specs"] = out_specs
    return pl.pallas_call(body, out_shape=out_shape, grid=grid, scratch_shapes=scratch, name=name,
                          compiler_params=pltpu.CompilerParams(**params), **kwargs)


def _sds(shape, dtype):
    return jax.ShapeDtypeStruct(tuple(shape), dtype)


def _dot(a, b, dims=NN_DIMS):
    return lax.dot_general(a, b, dims, preferred_element_type=F32)


def _sigmoid(x):
    return 1.0 / (1.0 + jnp.exp(-x))


def _silu(x):
    return x * _sigmoid(x)


def _dsilu(x):
    s = _sigmoid(x)
    return s * (1.0 + x * (1.0 - s))


def _dot_split(x, m16, dims=NN_DIMS):
    hi = x.astype(BF16)
    r1 = x - hi.astype(F32)
    mid = r1.astype(BF16)
    lo = (r1 - mid.astype(F32)).astype(BF16)
    return _dot(hi, m16, dims) + _dot(mid, m16, dims) + _dot(lo, m16, dims)


def _mm(a, b, dims, outs, *, name, tm=MM_TM, tn=MM_TN, tk=MM_TK, epi=None, extra=()):
    if dims == "nn":
        (m, k), (k2, n) = a.shape, b.shape
    elif dims == "nt":
        (m, k), (n, k2) = a.shape, b.shape
    else:
        (k, m), (k2, n) = a.shape, b.shape
    assert k == k2, (a.shape, b.shape, dims)
    tm, tn, tk = min(tm, m), min(tn, n), min(tk, k)
    assert m % tm == 0 and n % tn == 0 and k % tk == 0, (m, n, k, tm, tn, tk)
    nk = k // tk
    dn = {"nn": NN_DIMS, "nt": NT_DIMS, "tn": TN_DIMS}[dims]
    n_extra, n_out = len(extra), len(outs)
    if epi is None:
        epi = lambda acc: (acc,) * n_out

    def body(a_ref, b_ref, *rest):
        extra_refs, out_refs, acc_ref = rest[:n_extra], rest[n_extra:n_extra + n_out], rest[-1]
        kk = pl.program_id(2)

        @pl.when(kk == 0)
        def _():
            acc_ref[...] = jnp.zeros_like(acc_ref)

        acc_ref[...] += _dot(a_ref[...].astype(BF16), b_ref[...].astype(BF16), dn)

        @pl.when(kk == nk - 1)
        def _():
            res = epi(acc_ref[...], *[e[...] for e in extra_refs])
            for o, r in zip(out_refs, res):
                o[...] = r.astype(o.dtype)

    if dims == "tn":
        a_spec = pl.BlockSpec((tk, tm), lambda i, j, kk: (kk, i))
    else:
        a_spec = pl.BlockSpec((tm, tk), lambda i, j, kk: (i, kk))
    if dims == "nt":
        b_spec = pl.BlockSpec((tn, tk), lambda i, j, kk: (j, kk))
    else:
        b_spec = pl.BlockSpec((tk, tn), lambda i, j, kk: (kk, j))
    o_spec = pl.BlockSpec((tm, tn), lambda i, j, kk: (i, j))
    res = _pcall(body, name=name, grid=(m // tm, n // tn, nk),
                 in_specs=[a_spec, b_spec] + [o_spec] * n_extra,
                 out_specs=[o_spec] * n_out,
                 out_shape=[_sds((m, n), d) for d in outs],
                 scratch=[pltpu.VMEM((tm, tn), F32)],
                 sem=("parallel", "parallel", "arbitrary"))(a, b, *extra)
    return res[0] if n_out == 1 else res


def _rowwise(fn, rows, consts, row_outs, acc_outs, *, name, tm=ROW_TILE):
    s = rows[0].shape[0]
    tm = min(tm, s)
    assert s % tm == 0
    n_in, n_o = len(rows) + len(consts), len(row_outs)

    def body(*refs):
        ins, outs = refs[:n_in], refs[n_in:]
        res = fn(*[r[...] for r in ins])
        if not isinstance(res, (tuple, list)):
            res = (res,)
        for o, val in zip(outs[:n_o], res[:n_o]):
            o[...] = val.astype(o.dtype)
        if acc_outs:
            @pl.when(pl.program_id(0) == 0)
            def _():
                for o in outs[n_o:]:
                    o[...] = jnp.zeros_like(o)
            for o, val in zip(outs[n_o:], res[n_o:]):
                o[...] += val

    in_specs = [pl.BlockSpec((tm, r.shape[1]), lambda i: (i, 0)) for r in rows]
    in_specs += [pl.BlockSpec(c.shape, functools.partial(lambda nd, i: (0,) * nd, c.ndim)) for c in consts]
    out_specs = [pl.BlockSpec((tm, c), lambda i: (i, 0)) for c, _ in row_outs]
    out_specs += [pl.BlockSpec(tuple(sh), lambda i: (0, 0)) for sh in acc_outs]
    out_shape = [_sds((s, c), d) for c, d in row_outs] + [_sds(sh, F32) for sh in acc_outs]
    res = _pcall(body, name=name, grid=(s // tm,), in_specs=in_specs, out_specs=out_specs,
                 out_shape=out_shape, sem=("arbitrary",))(*rows, *consts)
    return res


def _colsum(x):
    return jnp.sum(x, axis=0, keepdims=True)


def _ln_stats(r):
    mu = jnp.mean(r, axis=-1, keepdims=True)
    xc = r - mu
    var = jnp.mean(xc * xc, axis=-1, keepdims=True)
    rstd = lax.rsqrt(var + LN_EPS)
    return xc * rstd, rstd


def _ln_bwd(dy, xhat, rstd, gamma):
    dyg = dy * gamma
    m1 = jnp.mean(dyg, axis=-1, keepdims=True)
    m2 = jnp.mean(dyg * xhat, axis=-1, keepdims=True)
    return rstd * (dyg - m1 - xhat * m2)


def _modulate_in(x, mod, name):
    def fn(xv, m):
        return (xv * (1.0 + m[1:2]) + m[0:1],)
    return _rowwise(fn, [x], [mod], [(x.shape[1], BF16)], [], name=name)[0]


def _res_ln_mod(x, y, mod, g, b, name):
    d = x.shape[1]

    def fn(xv, yv, m, gv, bv):
        r = ALPHA * xv + (1.0 + m[2:3]) * yv
        xhat, _ = _ln_stats(r)
        x1 = xhat * gv + bv
        u2 = x1 * (1.0 + m[4:5]) + m[3:4]
        return r, x1, u2
    return _rowwise(fn, [x, y], [mod, g, b], [(d, F32), (d, F32), (d, BF16)], [], name=name)


def _res_ln(x, y, mod, g, b, name):
    d = x.shape[1]

    def fn(xv, yv, m, gv, bv):
        r = ALPHA * xv + (1.0 + m[5:6]) * yv
        xhat, _ = _ln_stats(r)
        return r, xhat * gv + bv
    return _rowwise(fn, [x, y], [mod, g, b], [(d, F32), (d, F32)], [], name=name)


def _loss_grad(xf, target, name):
    d = xf.shape[1]

    def fn(xv, tv):
        e = xv - tv
        return e * (1.0 / d), _colsum(e * e)
    return _rowwise(fn, [xf, target], [], [(d, F32)], [(1, d)], name=name)


def _ln2_bwd(dx, r2, y2, mod, g, name):
    d = dx.shape[1]

    def fn(dxv, rv, yv, m, gv):
        xhat, rstd = _ln_stats(rv)
        dr = _ln_bwd(dxv, xhat, rstd, gv)
        return (dr * (1.0 + m[5:6]), ALPHA * dr,
                _colsum(dxv * xhat), _colsum(dxv), _colsum(dr * yv))
    return _rowwise(fn, [dx, r2, y2], [mod, g], [(d, BF16), (d, F32)], [(1, d)] * 3, name=name)


def _ln1_bwd(du2, dres, r, y, mod, g, b, name):
    d = du2.shape[1]

    def fn(duv, drv, rv, yv, m, gv, bv):
        xhat, rstd = _ln_stats(rv)
        x1 = xhat * gv + bv
        dx1 = duv * (1.0 + m[4:5]) + drv
        dr = _ln_bwd(dx1, xhat, rstd, gv)
        return (dr * (1.0 + m[2:3]), ALPHA * dr,
                _colsum(duv * x1), _colsum(duv), _colsum(dx1 * xhat), _colsum(dx1), _colsum(dr * yv))
    return _rowwise(fn, [du2, dres, r, y], [mod, g, b], [(d, BF16), (d, F32)], [(1, d)] * 5, name=name)


def _mod_in_bwd(du, dres, x, mod, name):
    d = du.shape[1]

    def fn(duv, drv, xv, m):
        return duv * (1.0 + m[1:2]) + drv, _colsum(duv * xv), _colsum(duv)
    return _rowwise(fn, [du, dres, x], [mod], [(d, F32)], [(1, d)] * 2, name=name)


def _fox_gate(fraw, b_pad, name):
    s = fraw.shape[0]
    tb = min(SCAN_TILE, s)

    def body(f_ref, b_ref, cum_ref, carry):
        @pl.when(pl.program_id(0) == 0)
        def _():
            carry[...] = jnp.zeros_like(carry)
        z = f_ref[...] + b_ref[...]
        lf = jnp.minimum(z, 0.0) - jnp.log(1.0 + jnp.exp(-jnp.abs(z)))
        lane = lax.broadcasted_iota(jnp.int32, (tb, LANES), 1)
        row = lax.broadcasted_iota(jnp.int32, (tb, LANES), 0)
        c = jnp.where(lane < FOX_HEADS, lf, 0.0)
        sh = 1
        while sh < tb:
            c = c + jnp.where(row >= sh, pltpu.roll(c, sh, 0), 0.0)
            sh *= 2
        c = c + carry[0:1, :]
        cum_ref[...] = c
        carry[0:1, :] = c[tb - 1:tb, :]

    return _pcall(body, name=name, grid=(s // tb,),
                  in_specs=[pl.BlockSpec((tb, LANES), lambda i: (i, 0)), pl.BlockSpec((1, LANES), lambda i: (0, 0))],
                  out_specs=pl.BlockSpec((tb, LANES), lambda i: (i, 0)),
                  out_shape=_sds((s, LANES), F32), scratch=[pltpu.VMEM((8, LANES), F32)],
                  sem=("arbitrary",))(fraw, b_pad)


def _fox_gate_bwd(drow, dcol, fraw, b_pad, name):
    s = fraw.shape[0]
    tb = min(SCAN_TILE, s)
    n = s // tb

    def body(dr_ref, dc_ref, f_ref, b_ref, df_ref, db_ref, carry):
        @pl.when(pl.program_id(0) == 0)
        def _():
            carry[...] = jnp.zeros_like(carry)
            db_ref[...] = jnp.zeros_like(db_ref)
        row = lax.broadcasted_iota(jnp.int32, (tb, LANES), 0)
        c = dr_ref[...] + dc_ref[...]
        sh = 1
        while sh < tb:
            c = c + jnp.where(row + sh < tb, pltpu.roll(c, tb - sh, 0), 0.0)
            sh *= 2
        c = c + carry[0:1, :]
        carry[0:1, :] = c[0:1, :]
        z = f_ref[...] + b_ref[...]
        df = c * (1.0 / (1.0 + jnp.exp(z)))
        df_ref[...] = df.astype(df_ref.dtype)
        db_ref[...] += _colsum(df)

    rev = lambda i: (n - 1 - i, 0)
    return _pcall(body, name=name, grid=(n,),
                  in_specs=[pl.BlockSpec((tb, LANES), rev)] * 3 + [pl.BlockSpec((1, LANES), lambda i: (0, 0))],
                  out_specs=[pl.BlockSpec((tb, LANES), rev), pl.BlockSpec((1, LANES), lambda i: (0, 0))],
                  out_shape=[_sds((s, LANES), BF16), _sds((1, LANES), F32)],
                  scratch=[pltpu.VMEM((8, LANES), F32)], sem=("arbitrary",))(drow, dcol, fraw, b_pad)


def _head_pair_masks(t):
    lane = lax.broadcasted_iota(jnp.int32, (t, LANES), 1)
    return lane < HEAD_DIM


def _lane_blocks(x):
    return [x[:, c * LANES:(c + 1) * LANES] for c in range(x.shape[1] // LANES)]


def _sum_list(xs):
    acc = xs[0]
    for x in xs[1:]:
        acc = acc + x
    return acc


def _causal(t, transposed=False):
    ri = lax.broadcasted_iota(jnp.int32, (t, t), 0)
    ci = lax.broadcasted_iota(jnp.int32, (t, t), 1)
    return ci >= ri if transposed else ri >= ci


def _flash_fwd(qkv, ck_rows, kb_start, name):
    s = qkv.shape[0]
    t = min(ATT_TILE, s)
    nq = s // t
    scale = HEAD_DIM ** -0.5
    hp_blocks = FOX_HEADS // 2

    def body(ks_ref, q_ref, k_ref, v_ref, ck_ref, o_ref, lse_ref, acc_ref, m_ref, l_ref):
        hp, qb = pl.program_id(0), pl.program_id(1)
        q2 = q_ref[...] * scale
        first = _head_pair_masks(t)
        zero = jnp.zeros_like(q2)
        qs = (jnp.where(first, q2, zero), jnp.where(first, zero, q2))
        m_ref[...] = jnp.full_like(m_ref, -jnp.inf)
        l_ref[...] = jnp.zeros_like(l_ref)
        acc_ref[...] = jnp.zeros_like(acc_ref)

        def tile(kb, diagonal):
            off = pl.multiple_of(kb * t, t)
            k2 = k_ref[pl.ds(off, t), :]
            v2 = v_ref[pl.ds(off, t), :]
            ck = ck_ref[kb]
            pvs, als = [], []
            for j in range(2):
                sc = _dot(qs[j], k2, NT_DIMS) - ck[j:j + 1, :]
                if diagonal:
                    sc = jnp.where(_causal(t), sc, -jnp.inf)
                blocks = _lane_blocks(sc)
                mx = blocks[0]
                for b in blocks[1:]:
                    mx = jnp.maximum(mx, b)
                m_old = m_ref[j]
                m_new = jnp.maximum(m_old, jnp.max(mx, axis=1, keepdims=True))
                ps = [jnp.exp(b - m_new) for b in blocks]
                a = jnp.exp(m_old - m_new)
                l_ref[j] = a * l_ref[j] + _sum_list(ps)
                m_ref[j] = m_new
                pvs.append(_dot(jnp.concatenate(ps, axis=1).astype(BF16), v2))
                als.append(a)
            acc_ref[...] = jnp.where(first, als[0], als[1]) * acc_ref[...] + jnp.where(first, pvs[0], pvs[1])

        def step(kb, carry):
            tile(kb, False)
            return carry

        lax.fori_loop(ks_ref[hp, qb], qb, step, 0)
        tile(qb, True)
        l0 = jnp.sum(l_ref[0], axis=1, keepdims=True)
        l1 = jnp.sum(l_ref[1], axis=1, keepdims=True)
        o_ref[...] = acc_ref[...] / jnp.where(first, l0, l1)
        lse_ref[:, 0:1] = m_ref[0][:, 0:1] + jnp.log(l0)
        lse_ref[:, 1:2] = m_ref[1][:, 0:1] + jnp.log(l1)

    return _pcall(
        body, name=name, grid=(hp_blocks, nq), prefetch=1,
        in_specs=[pl.BlockSpec((t, LANES), lambda h, i, ks: (i, h)),
                  pl.BlockSpec((s, LANES), lambda h, i, ks: (0, hp_blocks + h)),
                  pl.BlockSpec((s, LANES), lambda h, i, ks: (0, 2 * hp_blocks + h)),
                  pl.BlockSpec((None, nq, 2, t), lambda h, i, ks: (h, 0, 0, 0))],
        out_specs=[pl.BlockSpec((t, LANES), lambda h, i, ks: (i, h)),
                   pl.BlockSpec((None, t, 2), lambda h, i, ks: (h, i, 0))],
        out_shape=[_sds((s, hp_blocks * LANES), F32), _sds((hp_blocks, s, 2), F32)],
        scratch=[pltpu.VMEM((t, LANES), F32), pltpu.VMEM((2, t, LANES), F32), pltpu.VMEM((2, t, LANES), F32)],
        sem=("parallel", "arbitrary"))(kb_start, qkv, qkv, qkv, ck_rows)


def _flash_dq(qkv, do16, ck_rows, lse_c, dl_c, kb_start, name):
    s = qkv.shape[0]
    t = min(ATT_TILE, s)
    nq = s // t
    scale = HEAD_DIM ** -0.5
    hp_blocks = FOX_HEADS // 2

    def body(ks_ref, q_ref, do_ref, k_ref, v_ref, ck_ref, lse_ref, dl_ref, dq_ref, drow_ref, acc_ref, row_acc):
        hp, qb = pl.program_id(0), pl.program_id(1)
        q2, do2 = q_ref[...] * scale, do_ref[...]
        first = _head_pair_masks(t)
        zero = jnp.zeros_like(q2)
        qs = (jnp.where(first, q2, zero), jnp.where(first, zero, q2))
        dos = (jnp.where(first, do2, zero), jnp.where(first, zero, do2))
        lse, dl = lse_ref[...], dl_ref[...]
        lse_b = [jnp.broadcast_to(lse[:, j:j + 1], (t, LANES)) for j in range(2)]
        dl_b = [jnp.broadcast_to(dl[:, j:j + 1], (t, LANES)) for j in range(2)]
        acc_ref[...] = jnp.zeros_like(acc_ref)
        row_acc[...] = jnp.zeros_like(row_acc)

        def tile(kb, diagonal):
            off = pl.multiple_of(kb * t, t)
            k2 = k_ref[pl.ds(off, t), :]
            v2 = v_ref[pl.ds(off, t), :]
            ck = ck_ref[kb]
            dqs = []
            for j in range(2):
                sc = _dot(qs[j], k2, NT_DIMS) - ck[j:j + 1, :]
                if diagonal:
                    sc = jnp.where(_causal(t), sc, -jnp.inf)
                dp = _dot(dos[j], v2, NT_DIMS)
                dsb = [jnp.exp(x - lse_b[j]) * (d - dl_b[j]) for x, d in zip(_lane_blocks(sc), _lane_blocks(dp))]
                row_acc[j] += _sum_list(dsb)
                dqs.append(_dot(jnp.concatenate(dsb, axis=1).astype(BF16), k2))
            acc_ref[...] += jnp.where(first, dqs[0], dqs[1])

        def step(kb, carry):
            tile(kb, False)
            return carry

        lax.fori_loop(ks_ref[hp, qb], qb, step, 0)
        tile(qb, True)
        dq_ref[...] = (acc_ref[...] * scale).astype(dq_ref.dtype)
        drow_ref[:, 0:1] = jnp.sum(row_acc[0], axis=1, keepdims=True)
        drow_ref[:, 1:2] = jnp.sum(row_acc[1], axis=1, keepdims=True)

    return _pcall(
        body, name=name, grid=(hp_blocks, nq), prefetch=1,
        in_specs=[pl.BlockSpec((t, LANES), lambda h, i, ks: (i, h)),
                  pl.BlockSpec((t, LANES), lambda h, i, ks: (i, h)),
                  pl.BlockSpec((s, LANES), lambda h, i, ks: (0, hp_blocks + h)),
                  pl.BlockSpec((s, LANES), lambda h, i, ks: (0, 2 * hp_blocks + h)),
                  pl.BlockSpec((None, nq, 2, t), lambda h, i, ks: (h, 0, 0, 0)),
                  pl.BlockSpec((None, t, 2), lambda h, i, ks: (h, i, 0)),
                  pl.BlockSpec((None, t, 2), lambda h, i, ks: (h, i, 0))],
        out_specs=[pl.BlockSpec((t, LANES), lambda h, i, ks: (i, h)),
                   pl.BlockSpec((None, t, 2), lambda h, i, ks: (h, i, 0))],
        out_shape=[_sds((s, hp_blocks * LANES), BF16), _sds((hp_blocks, s, 2), F32)],
        scratch=[pltpu.VMEM((t, LANES), F32), pltpu.VMEM((2, t, LANES), F32)],
        sem=("parallel", "arbitrary"))(kb_start, qkv, do16, qkv, qkv, ck_rows, lse_c, dl_c)


def _flash_dkv(qkv, do16, cum, lse_rows, dl_rows, qb_end, name):
    s = qkv.shape[0]
    t = min(ATT_TILE, s)
    nq = s // t
    scale = HEAD_DIM ** -0.5
    hp_blocks = FOX_HEADS // 2

    def body(qe_ref, k_ref, v_ref, cum_ref, q_ref, do_ref, lse_ref, dl_ref, dk_ref, dv_ref, dck_ref,
             dk_acc, dv_acc, dck_acc):
        hp, kb = pl.program_id(0), pl.program_id(1)
        k2, v2 = k_ref[...] * scale, v_ref[...]
        first = _head_pair_masks(t)
        zero = jnp.zeros_like(k2)
        ks = (jnp.where(first, k2, zero), jnp.where(first, zero, k2))
        vs = (jnp.where(first, v2, zero), jnp.where(first, zero, v2))
        cumv = cum_ref[...]
        lane = lax.broadcasted_iota(jnp.int32, (t, LANES), 1)
        ck_b = [jnp.broadcast_to(jnp.sum(jnp.where(lane == 2 * hp + j, cumv, 0.0), axis=1, keepdims=True), (t, LANES))
                for j in range(2)]
        dk_acc[...] = jnp.zeros_like(dk_acc)
        dv_acc[...] = jnp.zeros_like(dv_acc)
        dck_acc[...] = jnp.zeros_like(dck_acc)

        def tile(qb, diagonal):
            off = pl.multiple_of(qb * t, t)
            q2 = q_ref[pl.ds(off, t), :]
            do2 = do_ref[pl.ds(off, t), :]
            lse, dl = lse_ref[qb], dl_ref[qb]
            dvs, dks = [], []
            for j in range(2):
                sc = _dot(ks[j], q2, NT_DIMS)
                if diagonal:
                    sc = jnp.where(_causal(t, transposed=True), sc, -jnp.inf)
                dp = _dot(vs[j], do2, NT_DIMS) - dl[j:j + 1, :]
                pb = [jnp.exp((x - ck_b[j]) - l) for x, l in zip(_lane_blocks(sc), _lane_blocks(lse[j:j + 1, :]))]
                dsb = [p * d for p, d in zip(pb, _lane_blocks(dp))]
                dck_acc[j] += _sum_list(dsb)
                dvs.append(_dot(jnp.concatenate(pb, axis=1).astype(BF16), do2))
                dks.append(_dot(jnp.concatenate(dsb, axis=1).astype(BF16), q2))
            dv_acc[...] += jnp.where(first, dvs[0], dvs[1])
            dk_acc[...] += jnp.where(first, dks[0], dks[1])

        def step(qb, carry):
            tile(qb, False)
            return carry

        tile(kb, True)
        lax.fori_loop(kb + 1, qe_ref[hp, kb] + 1, step, 0)
        dk_ref[...] = (dk_acc[...] * scale).astype(dk_ref.dtype)
        dv_ref[...] = dv_acc[...].astype(dv_ref.dtype)
        dck_ref[:, 0:1] = -jnp.sum(dck_acc[0], axis=1, keepdims=True)
        dck_ref[:, 1:2] = -jnp.sum(dck_acc[1], axis=1, keepdims=True)

    return _pcall(
        body, name=name, grid=(hp_blocks, nq), prefetch=1,
        in_specs=[pl.BlockSpec((t, LANES), lambda h, j, qe: (j, hp_blocks + h)),
                  pl.BlockSpec((t, LANES), lambda h, j, qe: (j, 2 * hp_blocks + h)),
                  pl.BlockSpec((t, LANES), lambda h, j, qe: (j, 0)),
                  pl.BlockSpec((s, LANES), lambda h, j, qe: (0, h)),
                  pl.BlockSpec((s, LANES), lambda h, j, qe: (0, h)),
                  pl.BlockSpec((None, nq, 2, t), lambda h, j, qe: (h, 0, 0, 0)),
                  pl.BlockSpec((None, nq, 2, t), lambda h, j, qe: (h, 0, 0, 0))],
        out_specs=[pl.BlockSpec((t, LANES), lambda h, j, qe: (j, h)),
                   pl.BlockSpec((t, LANES), lambda h, j, qe: (j, h)),
                   pl.BlockSpec((None, t, 2), lambda h, j, qe: (h, j, 0))],
        out_shape=[_sds((s, hp_blocks * LANES), BF16), _sds((s, hp_blocks * LANES), BF16),
                   _sds((hp_blocks, s, 2), F32)],
        scratch=[pltpu.VMEM((t, LANES), F32), pltpu.VMEM((t, LANES), F32), pltpu.VMEM((2, t, LANES), F32)],
        sem=("parallel", "arbitrary"))(qb_end, qkv, qkv, cum, qkv, do16, lse_rows, dl_rows)


SKIP_NATS = 110.0


def _qk_norms(qkv, ind16, name):
    d = FOX_HEADS * HEAD_DIM

    def fn(tile, ind):
        q = tile[:, :d].astype(F32)
        k = tile[:, d:2 * d].astype(F32)
        return _dot_split(q * q, ind), _dot_split(k * k, ind)
    return _rowwise(fn, [qkv], [ind16], [(LANES, F32), (LANES, F32)], [], name=name)


def _skip_bounds(qn, kn, cum, t):
    s = qn.shape[0]
    nq = s // t
    hp = FOX_HEADS // 2
    scale = HEAD_DIM ** -0.5
    qmax = jnp.sqrt(jnp.max(qn.reshape(nq, t, FOX_HEADS), axis=1))
    kmax = jnp.sqrt(jnp.max(kn, axis=0))
    bound = qmax * kmax[None, :] * (scale * 1.01) + 1e-3
    gap = cum[0::t][:, None, :] - cum[t - 1::t][None, :, :]
    idx = jnp.arange(nq, dtype=jnp.int32)
    needed = (gap + 2.0 * bound[:, None, :]) > -SKIP_NATS
    needed = needed.reshape(nq, nq, hp, 2).any(axis=-1) & (idx[None, :] <= idx[:, None])[:, :, None]
    first = jnp.min(jnp.where(needed, idx[None, :, None], nq), axis=1)
    first = jnp.minimum(first, idx[:, None])
    start = lax.cummin(first, axis=0, reverse=True)
    uses = start[:, None, :] <= idx[None, :, None]
    last = jnp.max(jnp.where(uses, idx[:, None, None], 0), axis=0)
    last = jnp.maximum(last, idx[:, None])
    return start.T.astype(jnp.int32), last.T.astype(jnp.int32)


def _head_rowsum(prod, ind16, name):
    def fn(a, b, ind):
        return (_dot_split(a * b, ind),)
    return _rowwise(fn, list(prod), [ind16], [(LANES, F32)], [], name=name)[0]


def _pairs_cols(x16):
    s = x16.shape[0]
    return x16.reshape(s, FOX_HEADS // 2, 2).transpose(1, 0, 2)


def _pairs_rows(x16, t):
    s = x16.shape[0]
    return x16.reshape(s // t, t, FOX_HEADS // 2, 2).transpose(2, 0, 3, 1)


def _fox_forward(u, w):
    s = u.shape[0]
    t = min(ATT_TILE, s)
    qkv = _mm(u, w["fox_qkv"], "nn", [BF16], name="fox_qkv")
    fraw = _mm(u, w["fox_f"], "nn", [F32], name="fox_fproj")
    cum = _fox_gate(fraw, w["fox_bf"], "fox_gate")
    ck_rows = _pairs_rows(cum[:, :FOX_HEADS], t)
    qn, kn = _qk_norms(qkv, w["head_ind"], "fox_qk_norms")
    kb_start, qb_end = _skip_bounds(qn[:, :FOX_HEADS], kn[:, :FOX_HEADS], cum[:, :FOX_HEADS], t)
    o, lse = _flash_fwd(qkv, ck_rows, kb_start, "fox_flash_fwd")
    y = _mm(o, w["fox_o"], "nn", [F32], name="fox_oproj")
    return y, dict(u=u, qkv=qkv, fraw=fraw, cum=cum, ck_rows=ck_rows, o=o, lse=lse, kb_start=kb_start,
                   qb_end=qb_end)


def _fox_backward(dy, sv, w):
    s = dy.shape[0]
    t = min(ATT_TILE, s)
    do32, do16 = _mm(dy, w["fox_o"], "nt", [F32, BF16], name="fox_do")
    g_wo = _mm(sv["o"], dy, "tn", [F32], name="fox_gwo")
    delta = _head_rowsum((do32, sv["o"]), w["head_ind"], "fox_delta")[:, :FOX_HEADS]
    lse16 = sv["lse"].transpose(1, 0, 2).reshape(s, FOX_HEADS)
    dq, drow = _flash_dq(sv["qkv"], do16, sv["ck_rows"], sv["lse"], _pairs_cols(delta), sv["kb_start"],
                         "fox_flash_dq")
    dk, dv, dck = _flash_dkv(sv["qkv"], do16, sv["cum"], _pairs_rows(lse16, t), _pairs_rows(delta, t),
                             sv["qb_end"], "fox_flash_dkv")
    pad = ((0, 0), (0, LANES - FOX_HEADS))
    dcol = jnp.pad(dck.transpose(1, 0, 2).reshape(s, FOX_HEADS), pad)
    drow = jnp.pad(drow.transpose(1, 0, 2).reshape(s, FOX_HEADS), pad)
    df, db_f = _fox_gate_bwd(drow, dcol, sv["fraw"], w["fox_bf"], "fox_gate_bwd")
    dproj = jnp.concatenate([dq, dk, dv, df], axis=1)
    du = _mm(dproj, w["fox_in_pad"], "nt", [F32], name="fox_du", tk=640)
    g_win = _mm(sv["u"], dproj, "tn", [F32], name="fox_gwin", tn=640)
    return du, dict(fox_w_in=g_win[:, :3 * FOX_HEADS * HEAD_DIM + FOX_HEADS], fox_w_o=g_wo,
                    fox_b_f=db_f[:, :FOX_HEADS])


def _conv_fwd(xpre, w8, b, name):
    s, c = xpre.shape
    tm, tc = min(ROW_TILE, s), min(1024, c)
    hb = tm // 8

    def body(x_ref, h_ref, w_ref, b_ref, xc_ref, xa_ref):
        i = pl.program_id(1)
        x = x_ref[...]
        halo = jnp.where(i > 0, h_ref[...], 0.0)
        w = w_ref[...]
        row = lax.broadcasted_iota(jnp.int32, (tm, tc), 0)
        row8 = lax.broadcasted_iota(jnp.int32, (8, tc), 0)
        acc = x * w[3:4] + b_ref[...]
        x8 = x[0:8]
        acc8 = x8 * w[3:4] + b_ref[...]
        for j in range(1, SSM_CONV):
            acc = acc + w[3 - j:4 - j] * pltpu.roll(x, j, 0)
            acc8 = acc8 + w[3 - j:4 - j] * jnp.where(row8 < j, pltpu.roll(halo, j, 0), pltpu.roll(x8, j, 0))
        xc_ref[...] = acc
        xc_ref[0:8, :] = acc8
        xc = xc_ref[...]
        xa_ref[...] = _silu(xc)

    tile = pl.BlockSpec((tm, tc), lambda jc, i: (i, jc))
    return _pcall(body, name=name, grid=(c // tc, s // tm),
                  in_specs=[tile, pl.BlockSpec((8, tc), lambda jc, i: (jnp.maximum(i * hb - 1, 0), jc)),
                            pl.BlockSpec((8, tc), lambda jc, i: (0, jc)), pl.BlockSpec((1, tc), lambda jc, i: (0, jc))],
                  out_specs=[tile, tile], out_shape=[_sds((s, c), F32), _sds((s, c), F32)],
                  sem=("parallel", "arbitrary"))(xpre, xpre, w8, b)


def _conv_bwd(dxa, xc, xpre, w8, name):
    s, c = xpre.shape
    tm, tc = min(ROW_TILE, s), min(1024, c)
    hb = tm // 8
    n = s // tm

    def body(d_ref, xc_ref, x_ref, xh_ref, dn_ref, xcn_ref, w_ref, dx_ref, dw_ref, db_ref, scr):
        i = pl.program_id(1)

        @pl.when(i == 0)
        def _():
            dw_ref[...] = jnp.zeros_like(dw_ref)
            db_ref[...] = jnp.zeros_like(db_ref)
        w = w_ref[...]
        x = x_ref[...]
        g = d_ref[...] * _dsilu(xc_ref[...])
        gn = jnp.where(i < n - 1, dn_ref[...] * _dsilu(xcn_ref[...]), 0.0)
        halo = jnp.where(i > 0, xh_ref[...], 0.0)
        row = lax.broadcasted_iota(jnp.int32, (tm, tc), 0)
        row8 = lax.broadcasted_iota(jnp.int32, (8, tc), 0)
        db_ref[...] += _colsum(g)
        dw_ref[3:4, :] += _colsum(g * x)
        g8 = g[0:8]
        acc = g * w[3:4]
        corr = jnp.zeros((8, tc), F32)
        for j in range(1, SSM_CONV):
            xs = pltpu.roll(x, j, 0)
            dwj = _colsum(jnp.where(row >= j, g * xs, 0.0))
            dwj = dwj + _colsum(jnp.where(row8 < j, g8 * pltpu.roll(halo, j, 0), 0.0))
            dw_ref[3 - j:4 - j, :] += dwj
            gs = pltpu.roll(g, tm - j, 0)
            acc = acc + w[3 - j:4 - j] * jnp.where(row < tm - j, gs, 0.0)
            corr = corr + w[3 - j:4 - j] * jnp.where(row8 >= 8 - j, pltpu.roll(gn, 8 - j, 0), 0.0)
        scr[...] = acc
        scr[tm - 8:tm, :] += corr
        dx_ref[...] = scr[...].astype(dx_ref.dtype)

    tile = pl.BlockSpec((tm, tc), lambda jc, i: (i, jc))
    prev8 = pl.BlockSpec((8, tc), lambda jc, i: (jnp.maximum(i * hb - 1, 0), jc))
    next8 = pl.BlockSpec((8, tc), lambda jc, i: (jnp.minimum((i + 1) * hb, n * hb - 1), jc))
    return _pcall(body, name=name, grid=(c // tc, n),
                  in_specs=[tile, tile, tile, prev8, next8, next8, pl.BlockSpec((8, tc), lambda jc, i: (0, jc))],
                  out_specs=[tile, pl.BlockSpec((8, tc), lambda jc, i: (0, jc)), pl.BlockSpec((1, tc), lambda jc, i: (0, jc))],
                  out_shape=[_sds((s, c), BF16), _sds((8, c), F32), _sds((1, c), F32)],
                  scratch=[pltpu.VMEM((tm, tc), F32)],
                  sem=("parallel", "arbitrary"))(dxa, xc, xpre, xpre, dxa, xc, w8)


def _ssd_pre(dtraw, dt_bias, a_log, name):
    def fn(raw, bias, alog):
        tm = raw.shape[0]
        z = raw + bias
        dt = jnp.maximum(z, 0.0) + jnp.log(1.0 + jnp.exp(-jnp.abs(z)))
        lane = lax.broadcasted_iota(jnp.int32, (tm, LANES), 1)
        pos = lax.broadcasted_iota(jnp.int32, (tm, LANES), 0) & (SSM_CHUNK - 1)
        dt = jnp.where(lane < SSM_HEADS, dt, 0.0)
        c = dt * (-jnp.exp(alog))
        sh = 1
        while sh < SSM_CHUNK:
            c = c + jnp.where(pos >= sh, pltpu.roll(c, sh, 0), 0.0)
            sh *= 2
        return dt, c
    return _rowwise(fn, [dtraw], [dt_bias, a_log], [(LANES, F32), (LANES, F32)], [], name=name)


def _ssd_post(dacs, ddt, dtraw, dt, dt_bias, a_log, name):
    def fn(dacs_v, ddt_v, raw, dt_v, bias, alog):
        tm = raw.shape[0]
        pos = lax.broadcasted_iota(jnp.int32, (tm, LANES), 0) & (SSM_CHUNK - 1)
        a = -jnp.exp(alog)
        c = dacs_v
        sh = 1
        while sh < SSM_CHUNK:
            c = c + jnp.where(pos + sh < SSM_CHUNK, pltpu.roll(c, tm - sh, 0), 0.0)
            sh *= 2
        draw = (ddt_v + c * a) * _sigmoid(raw + bias)
        return draw, _colsum(draw), _colsum(c * dt_v * a)
    return _rowwise(fn, [dacs, ddt, dtraw, dt], [dt_bias, a_log], [(LANES, BF16)], [(1, LANES)] * 2, name=name)


def _heads_cols(x, s):
    return x[:, :SSM_HEADS].reshape(s, SSM_GROUPS, 4).transpose(1, 0, 2)


def _heads_rows(x, s):
    return x[:, :SSM_HEADS].reshape(s // SSM_CHUNK, SSM_CHUNK, SSM_GROUPS, 4).transpose(2, 0, 3, 1)


def _expand_heads(cols, lane):
    out = cols[:, 3:4]
    for r in (2, 1, 0):
        out = jnp.where(lane < HEAD_DIM * (r + 1), cols[:, r:r + 1], out)
    return out


def _ssd_common(x, dtc, acsc, acsr):
    l = SSM_CHUNK
    lane = lax.broadcasted_iota(jnp.int32, (l, SSM_GROUP_WIDTH), 1)
    dt_e = _expand_heads(dtc, lane)
    acs_e = _expand_heads(acsc, lane)
    last = acsr[:, l - 1:l]
    lane1 = lax.broadcasted_iota(jnp.int32, (1, SSM_GROUP_WIDTH), 1)
    last_e = last[3:4, :]
    for r in (2, 1, 0):
        last_e = jnp.where(lane1 < HEAD_DIM * (r + 1), last[r:r + 1, :], last_e)
    e_e = jnp.exp(acs_e)
    dte_e = jnp.exp(last_e - acs_e)
    rowg = lax.broadcasted_iota(jnp.int32, (SSM_GROUP_WIDTH, SSM_STATE), 0)
    cd = jnp.exp(last)
    cd_mat = cd[3:4, :]
    for r in (2, 1, 0):
        cd_mat = jnp.where(rowg < HEAD_DIM * (r + 1), cd[r:r + 1, :], cd_mat)
    return lane, dt_e, e_e, dte_e, cd, cd_mat


def _ssd_fwd(xa, dtc, acsc, acsr, d_e, name):
    s = xa.shape[0]
    l, gw, ns = SSM_CHUNK, SSM_GROUP_WIDTH, SSM_STATE
    nc = s // l
    xb, bb = 2048 // gw, 2048 // ns

    def body(x_ref, b_ref, c_ref, dtc_ref, acsc_ref, acsr_ref, d_ref, y_ref, hp_ref, h_sc):
        @pl.when(pl.program_id(1) == 0)
        def _():
            h_sc[...] = jnp.zeros_like(h_sc)
        x = x_ref[...]
        bm, cm = b_ref[...].astype(BF16), c_ref[...].astype(BF16)
        acsc, acsr = acsc_ref[...], acsr_ref[...]
        lane, dt_e, e_e, dte_e, _, cd_mat = _ssd_common(x, dtc_ref[...], acsc, acsr)
        xdt = x * dt_e
        xdt16 = xdt.astype(BF16)
        cb = _dot(cm, bm, NT_DIMS)
        tril = lax.broadcasted_iota(jnp.int32, (l, l), 0) >= lax.broadcasted_iota(jnp.int32, (l, l), 1)
        yd = jnp.zeros((l, gw), F32)
        for r in range(4):
            lm = jnp.exp(jnp.where(tril, acsc[:, r:r + 1] - acsr[r:r + 1, :], -jnp.inf))
            yr = _dot((cb * lm).astype(BF16), xdt16)
            yd = jnp.where((lane >= HEAD_DIM * r) & (lane < HEAD_DIM * (r + 1)), yr, yd)
        hp = h_sc[...]
        hp_ref[...] = hp
        yoff = _dot(cm, hp.astype(BF16), NT_DIMS) * e_e
        y_ref[...] = yd + yoff + x * d_ref[...]
        st = _dot((xdt * dte_e).astype(BF16), bm, TN_DIMS)
        h_sc[...] = hp * cd_mat + st

    return _pcall(
        body, name=name, grid=(SSM_GROUPS, nc),
        in_specs=[pl.BlockSpec((l, gw), lambda g, c: (c, g)),
                  pl.BlockSpec((l, ns), lambda g, c: (c, bb + g)),
                  pl.BlockSpec((l, ns), lambda g, c: (c, bb + SSM_GROUPS + g)),
                  pl.BlockSpec((None, l, 4), lambda g, c: (g, c, 0)),
                  pl.BlockSpec((None, l, 4), lambda g, c: (g, c, 0)),
                  pl.BlockSpec((None, None, 4, l), lambda g, c: (g, c, 0, 0)),
                  pl.BlockSpec((None, 1, gw), lambda g, c: (g, 0, 0))],
        out_specs=[pl.BlockSpec((l, gw), lambda g, c: (c, g)),
                   pl.BlockSpec((None, None, gw, ns), lambda g, c: (g, c, 0, 0))],
        out_shape=[_sds((s, xb * gw), F32), _sds((SSM_GROUPS, nc, gw, ns), F32)],
        scratch=[pltpu.VMEM((gw, ns), F32)],
        sem=("parallel", "arbitrary"))(xa, xa, xa, dtc, acsc, acsr, d_e)


def _ssd_bwd(dy, xa, dtc, acsc, acsr, d_e, hprev, name):
    s = xa.shape[0]
    l, gw, ns = SSM_CHUNK, SSM_GROUP_WIDTH, SSM_STATE
    nc = s // l
    bb = 2048 // ns

    def body(dy_ref, x_ref, b_ref, c_ref, dtc_ref, acsc_ref, acsr_ref, d_ref, hp_ref,
             dx_ref, db_ref, dc_ref, dacs_ref, ddt_ref, dd_ref, dh_sc):
        @pl.when(pl.program_id(1) == 0)
        def _():
            dh_sc[...] = jnp.zeros_like(dh_sc)
            dd_ref[...] = jnp.zeros_like(dd_ref)
        dyv, x = dy_ref[...], x_ref[...]
        bm, cm = b_ref[...].astype(BF16), c_ref[...].astype(BF16)
        acsc, acsr = acsc_ref[...], acsr_ref[...]
        lane, dt_e, e_e, dte_e, cd, cd_mat = _ssd_common(x, dtc_ref[...], acsc, acsr)
        xdt = x * dt_e
        xdt16 = xdt.astype(BF16)
        dy16 = dyv.astype(BF16)
        cb = _dot(cm, bm, NT_DIMS)
        cbt = _dot(bm, cm, NT_DIMS)
        hp = hp_ref[...]
        hp16 = hp.astype(BF16)
        g = dh_sc[...]
        g16 = g.astype(BF16)
        t_all = _dot(cm, hp16, NT_DIMS)
        dt16 = (dyv * e_e).astype(BF16)
        dc = _dot(dt16, hp16)
        dhp = _dot(dt16, cm, TN_DIMS)
        yoff_term = dyv * t_all * e_e
        wv = xdt * dte_e
        dw = _dot(bm, g16, NT_DIMS)
        db = _dot(wv.astype(BF16), g16)
        dxdt = dw * dte_e
        dte_term = dw * wv
        gh = g * hp
        dh_sc[...] = g * cd_mat + dhp
        ri = lax.broadcasted_iota(jnp.int32, (l, l), 0)
        ci = lax.broadcasted_iota(jnp.int32, (l, l), 1)
        tril, triu = ri >= ci, ci >= ri
        dcb = jnp.zeros((l, l), F32)
        dcbt = jnp.zeros((l, l), F32)
        q_rows, q_cols = [], []
        for r in range(4):
            in_head = (lane >= HEAD_DIM * r) & (lane < HEAD_DIM * (r + 1))
            lm = jnp.exp(jnp.where(tril, acsc[:, r:r + 1] - acsr[r:r + 1, :], -jnp.inf))
            lmt = jnp.exp(jnp.where(triu, acsr[r:r + 1, :] - acsc[:, r:r + 1], -jnp.inf))
            mm_, mt = cb * lm, cbt * lmt
            dyr = jnp.where(in_head, dy16, jnp.zeros_like(dy16))
            dm = _dot(dyr, xdt16, NT_DIMS)
            dmt = _dot(xdt16, dyr, NT_DIMS)
            dxdt = dxdt + jnp.where(in_head, _dot(mt.astype(BF16), dy16), 0.0)
            dcb = dcb + dm * lm
            dcbt = dcbt + dmt * lmt
            q_rows.append(jnp.sum(dm * mm_, axis=1, keepdims=True))
            q_cols.append(jnp.sum(dmt * mt, axis=1, keepdims=True))
        dc = dc + _dot(dcb.astype(BF16), bm)
        db = db + _dot(dcbt.astype(BF16), cm)
        dxdt_x = dxdt * x
        dy_x = dyv * x
        rowc = lax.broadcasted_iota(jnp.int32, (l, 1), 0)
        lane128 = lax.broadcasted_iota(jnp.int32, (1, LANES), 1)
        dd_row = jnp.zeros((1, LANES), F32)
        for r in range(4):
            in_head = (lane >= HEAD_DIM * r) & (lane < HEAD_DIM * (r + 1))
            seg = lambda v: jnp.sum(jnp.where(in_head, v, 0.0), axis=1, keepdims=True)
            s1, s2, s3 = seg(yoff_term), seg(dte_term), seg(dxdt_x)
            dcd = jnp.sum(_colsum(gh[HEAD_DIM * r:HEAD_DIM * (r + 1), :]), axis=1, keepdims=True)
            last_add = _colsum(s2) + dcd * cd[r:r + 1, :]
            dacs_r = q_rows[r] - q_cols[r] + s1 - s2 + jnp.where(rowc == l - 1, last_add, 0.0)
            dacs_ref[:, r:r + 1] = dacs_r
            ddt_ref[:, r:r + 1] = s3
            dd_row = dd_row + jnp.where(lane128 == r, _colsum(seg(dy_x)), 0.0)
        dd_ref[0:1, :] += dd_row
        dx_ref[...] = dxdt * dt_e + dyv * d_ref[...]
        db_ref[...] = db
        dc_ref[...] = dc

    rc = lambda c: nc - 1 - c
    return _pcall(
        body, name=name, grid=(SSM_GROUPS, nc),
        in_specs=[pl.BlockSpec((l, gw), lambda g, c: (rc(c), g)),
                  pl.BlockSpec((l, gw), lambda g, c: (rc(c), g)),
                  pl.BlockSpec((l, ns), lambda g, c: (rc(c), bb + g)),
                  pl.BlockSpec((l, ns), lambda g, c: (rc(c), bb + SSM_GROUPS + g)),
                  pl.BlockSpec((None, l, 4), lambda g, c: (g, rc(c), 0)),
                  pl.BlockSpec((None, l, 4), lambda g, c: (g, rc(c), 0)),
                  pl.BlockSpec((None, None, 4, l), lambda g, c: (g, rc(c), 0, 0)),
                  pl.BlockSpec((None, 1, gw), lambda g, c: (g, 0, 0)),
                  pl.BlockSpec((None, None, gw, ns), lambda g, c: (g, rc(c), 0, 0))],
        out_specs=[pl.BlockSpec((l, gw), lambda g, c: (rc(c), g)),
                   pl.BlockSpec((l, ns), lambda g, c: (rc(c), g)),
                   pl.BlockSpec((l, ns), lambda g, c: (rc(c), g)),
                   pl.BlockSpec((None, l, 4), lambda g, c: (g, rc(c), 0)),
                   pl.BlockSpec((None, l, 4), lambda g, c: (g, rc(c), 0)),
                   pl.BlockSpec((None, 8, LANES), lambda g, c: (g, 0, 0))],
        out_shape=[_sds((s, 2048), F32), _sds((s, SSM_GROUPS * ns), F32), _sds((s, SSM_GROUPS * ns), F32),
                   _sds((SSM_GROUPS, s, 4), F32), _sds((SSM_GROUPS, s, 4), F32), _sds((SSM_GROUPS, 8, LANES), F32)],
        scratch=[pltpu.VMEM((gw, ns), F32)],
        sem=("parallel", "arbitrary"))(dy, xa, xa, xa, dtc, acsc, acsr, d_e, hprev)


def _gate_norm(y, z, nw, name):
    c = y.shape[1]

    def fn(yv, zv, w):
        outs = []
        for k in range(c // SSM_GROUP_WIDTH):
            sl = slice(k * SSM_GROUP_WIDTH, (k + 1) * SSM_GROUP_WIDTH)
            yg = yv[:, sl] * _silu(zv[:, sl])
            rinv = lax.rsqrt(jnp.mean(yg * yg, axis=-1, keepdims=True) + RMS_EPS)
            outs.append(yg * rinv * w[:, sl])
        return (jnp.concatenate(outs, axis=1),)
    return _rowwise(fn, [y, z], [nw], [(c, BF16)], [], name=name)[0]


def _gate_norm_bwd(dyn, y, z, nw, name):
    c = y.shape[1]

    def fn(dv, yv, zv, w):
        dys, dzs, dws = [], [], []
        for k in range(c // SSM_GROUP_WIDTH):
            sl = slice(k * SSM_GROUP_WIDTH, (k + 1) * SSM_GROUP_WIDTH)
            ys, zs, ds = yv[:, sl], zv[:, sl], dv[:, sl]
            sz = _silu(zs)
            yg = ys * sz
            rinv = lax.rsqrt(jnp.mean(yg * yg, axis=-1, keepdims=True) + RMS_EPS)
            nrm = yg * rinv
            dn = ds * w[:, sl]
            dyg = rinv * (dn - nrm * jnp.mean(dn * nrm, axis=-1, keepdims=True))
            dys.append(dyg * sz)
            dzs.append(dyg * ys * _dsilu(zs))
            dws.append(_colsum(ds * nrm))
        return jnp.concatenate(dys, axis=1), jnp.concatenate(dzs, axis=1), jnp.concatenate(dws, axis=1)
    return _rowwise(fn, [dyn, y, z], [nw], [(c, F32), (c, BF16)], [(1, c)], name=name, tm=128)


def _ssd_forward(u, w):
    s = u.shape[0]
    z = _mm(u, w["ssm_z"], "nn", [F32], name="ssm_zproj")
    xpre = _mm(u, w["ssm_xbc"], "nn", [F32], name="ssm_xproj")
    dtraw = _mm(u, w["ssm_dt"], "nn", [F32], name="ssm_dtproj")
    xc, xa = _conv_fwd(xpre, w["conv_w8"], w["conv_b"], "ssm_conv")
    dt, acs = _ssd_pre(dtraw, w["dt_bias"], w["a_log"], "ssm_pre")
    dtc, acsc, acsr = _heads_cols(dt, s), _heads_cols(acs, s), _heads_rows(acs, s)
    y, hprev = _ssd_fwd(xa, dtc, acsc, acsr, w["d_e"], "ssm_scan")
    yn = _gate_norm(y, z, w["norm_w"], "ssm_gate_norm")
    out = _mm(yn, w["ssm_out"], "nn", [F32], name="ssm_oproj")
    return out, dict(u=u, z=z, xpre=xpre, xc=xc, xa=xa, dtraw=dtraw, dt=dt, dtc=dtc, acsc=acsc, acsr=acsr,
                     y=y, hprev=hprev, yn=yn)


def _ssd_backward(dy, sv, w):
    s = dy.shape[0]
    dyn = _mm(dy, w["ssm_out"], "nt", [F32], name="ssm_dyn")
    g_wout = _mm(sv["yn"], dy, "tn", [F32], name="ssm_gwout")
    dys, dz, dnw = _gate_norm_bwd(dyn, sv["y"], sv["z"], w["norm_w"], "ssm_gate_norm_bwd")
    dx, dbm, dcm, dacs_c, ddt_c, dd = _ssd_bwd(dys, sv["xa"], sv["dtc"], sv["acsc"], sv["acsr"], w["d_e"],
                                               sv["hprev"], "ssm_scan_bwd")
    pad = ((0, 0), (0, LANES - SSM_HEADS))
    dacs = jnp.pad(dacs_c.transpose(1, 0, 2).reshape(s, SSM_HEADS), pad)
    ddt = jnp.pad(ddt_c.transpose(1, 0, 2).reshape(s, SSM_HEADS), pad)
    draw, dbias, dalog = _ssd_post(dacs, ddt, sv["dtraw"], sv["dt"], w["dt_bias"], w["a_log"], "ssm_post")
    dxa = jnp.concatenate([dx, dbm, dcm], axis=1)
    dxpre, dcw, dcb = _conv_bwd(dxa, sv["xc"], sv["xpre"], w["conv_w8"], "ssm_conv_bwd")
    dproj = jnp.concatenate([dz, dxpre, draw], axis=1)
    du = _mm(dproj, w["ssm_in_pad"], "nt", [F32], name="ssm_du", tk=896)
    g_win = _mm(sv["u"], dproj, "tn", [F32], name="ssm_gwin", tn=896)
    n_in = 2048 + 4096 + SSM_HEADS
    return du, dict(ssm_w_in=g_win[:, :n_in], ssm_w_out=g_wout, ssm_conv_w=dcw[:SSM_CONV], ssm_conv_b=dcb,
                    ssm_norm_w=dnw, ssm_dt_bias=dbias[:, :SSM_HEADS], ssm_a_log=dalog[:, :SSM_HEADS],
                    ssm_d=dd[:, 0, :4].reshape(1, SSM_HEADS))


def _mlp_forward(u2, w1, w2, tag):
    def epi(acc):
        hr = jnp.maximum(acc, 0.0)
        return hr, hr * hr
    hr, a = _mm(u2, w1, "nn", [BF16, BF16], name=tag + "_mlp_up", epi=epi)
    y2 = _mm(a, w2, "nn", [F32], name=tag + "_mlp_down")
    return y2, hr, a


def _mlp_backward(dy2, u2, hr, a, w1, w2, tag):
    dh = _mm(dy2, w2, "nt", [BF16], name=tag + "_mlp_dh", extra=(hr,),
             epi=lambda acc, h: (acc * (2.0 * h.astype(F32)),))
    g_w2 = _mm(a, dy2, "tn", [F32], name=tag + "_mlp_gw2")
    g_w1 = _mm(u2, dh, "tn", [F32], name=tag + "_mlp_gw1")
    du2 = _mm(dh, w1, "nt", [F32], name=tag + "_mlp_du")
    return du2, g_w1, g_w2


def _ada_forward(c16, ada_w, ada_b_cols, name):
    nl, d, cols = ada_w.shape
    tn = 512

    def body(c_ref, w_ref, b_ref, o_ref):
        cond = _silu(c_ref[...]).astype(BF16)
        o_ref[...] = _dot(cond, w_ref[...].astype(BF16)) + b_ref[...]

    return _pcall(body, name=name, grid=(nl, cols // tn),
                  in_specs=[pl.BlockSpec((16, d), lambda i, j: (0, 0)),
                            pl.BlockSpec((None, d, tn), lambda i, j: (i, 0, j)),
                            pl.BlockSpec((None, 1, tn), lambda i, j: (i, 0, j))],
                  out_specs=pl.BlockSpec((None, 16, tn), lambda i, j: (i, 0, j)),
                  out_shape=_sds((nl, 16, cols), F32), sem=("parallel", "parallel"))(c16, ada_w, ada_b_cols)


def _ada_backward(c_t, dmod_cols, name):
    d, nb = c_t.shape
    nl, _, cols = dmod_cols.shape
    tn = 512

    def body(c_ref, dm_ref, o_ref):
        cond = _silu(c_ref[...])
        dm = dm_ref[...]
        acc = cond[:, 0:1] * dm[0:1, :]
        for b in range(1, nb):
            acc = acc + cond[:, b:b + 1] * dm[b:b + 1, :]
        o_ref[...] = acc

    return _pcall(body, name=name, grid=(nl, cols // tn),
                  in_specs=[pl.BlockSpec((d, nb), lambda i, j: (0, 0)),
                            pl.BlockSpec((None, nb, tn), lambda i, j: (i, 0, j))],
                  out_specs=pl.BlockSpec((None, d, tn), lambda i, j: (i, 0, j)),
                  out_shape=_sds((nl, d, cols), F32), sem=("parallel", "parallel"))(c_t, dmod_cols)


def _adamw(w, g, m, v, name):
    rows, cols = w.shape
    tm = rows
    for cand in (256, 128, 64, 32, 16, 8):
        if rows % cand == 0 and rows > cand:
            tm = cand
            break
    c1 = 1.0 / (1.0 - ADAM_B1 ** ADAM_STEP)
    c2 = 1.0 / (1.0 - ADAM_B2 ** ADAM_STEP)

    def fn(wv, gv, mv, vv):
        mn = ADAM_B1 * mv + (1.0 - ADAM_B1) * gv
        vn = ADAM_B2 * vv + (1.0 - ADAM_B2) * (gv * gv)
        delta = -ADAM_LR * ((mn * c1) / (jnp.sqrt(vn * c2) + ADAM_EPS) + ADAM_WD * wv)
        return delta, mn, vn
    return _rowwise(fn, [w, g, m, v], [], [(cols, F32)] * 3, [], name=name, tm=tm)


def _my_pos():
    return lax.axis_index("x"), lax.axis_index("y"), lax.axis_index("c")


def _allgather8(x, name):
    r, c = x.shape

    def body(x_ref, out_ref, send_sems, recv_sems, local_sem):
        mx, my, mc = _my_pos()
        me = 4 * mx + 2 * my + mc
        mine = pltpu.make_async_copy(x_ref, out_ref.at[me], local_sem)
        mine.start()
        copies = []
        for k in range(1, 8):
            fx, fy, fc = (k >> 2) & 1, (k >> 1) & 1, k & 1
            px = 1 - mx if fx else mx
            py = 1 - my if fy else my
            pc = 1 - mc if fc else mc
            peer = 4 * px + 2 * py + pc
            send = pltpu.make_async_remote_copy(src_ref=x_ref, dst_ref=out_ref.at[me], send_sem=send_sems.at[k - 1],
                                                recv_sem=recv_sems.at[k - 1], device_id=(px, py, pc),
                                                device_id_type=MESH)
            send.start()
            recv = pltpu.make_async_remote_copy(src_ref=x_ref, dst_ref=out_ref.at[peer], send_sem=send_sems.at[k - 1],
                                                recv_sem=recv_sems.at[k - 1], device_id=(px, py, pc),
                                                device_id_type=MESH)
            copies.append((send, recv))
        for send, recv in copies:
            recv.wait_recv()
        for send, recv in copies:
            send.wait_send()
        mine.wait()

    vm = pl.BlockSpec(memory_space=pltpu.VMEM)
    return _pcall(body, name=name, in_specs=[vm], out_specs=vm, out_shape=_sds((8, r, c), x.dtype),
                  scratch=[pltpu.SemaphoreType.DMA((7,)), pltpu.SemaphoreType.DMA((7,)), pltpu.SemaphoreType.DMA])(x)


def _chip_flips(mx, my):
    out = []
    for fx, fy in ((1, 0), (0, 1), (1, 1)):
        px = 1 - mx if fx else mx
        py = 1 - my if fy else my
        out.append((px, py, 2 * px + py))
    return out


def _gather_chips(shard2, name):
    _, h, c = shard2.shape

    def body(x_ref, out_ref, send_sems, recv_sems):
        mx, my, mc = _my_pos()
        oc = 1 - mc
        mk = 2 * mx + my
        flips = _chip_flips(mx, my)

        def copy(k, src, dst, to):
            return pltpu.make_async_remote_copy(src_ref=src, dst_ref=dst, send_sem=send_sems.at[k],
                                                recv_sem=recv_sems.at[k], device_id=to, device_id_type=MESH)

        first = [copy(j, x_ref.at[mc], out_ref.at[mk, mc], (px, py, mc)) for j, (px, py, pk) in enumerate(flips)]
        for cp in first:
            cp.start()
        passed = []
        for j, (px, py, pk) in enumerate(flips):
            copy(j, x_ref.at[mc], out_ref.at[pk, mc], (px, py, mc)).wait_recv()
            fw = copy(3 + j, out_ref.at[pk, mc], out_ref.at[pk, mc], (mx, my, oc))
            fw.start()
            passed.append(fw)
        for j, (px, py, pk) in enumerate(flips):
            copy(3 + j, out_ref.at[pk, oc], out_ref.at[pk, oc], (mx, my, oc)).wait_recv()
        for cp in first + passed:
            cp.wait_send()

    return _pcall(body, name=name, in_specs=[HBM_SPEC], out_specs=HBM_SPEC, out_shape=_sds((4, 2, h, c), shard2.dtype),
                  scratch=[pltpu.SemaphoreType.DMA((6,)), pltpu.SemaphoreType.DMA((6,))])(shard2)


def _pair_exchange(g4, name):
    n, _, h, c = g4.shape

    def body(g_ref, out_ref, send_sem, recv_sem):
        mx, my, mc = _my_pos()
        oc = 1 - mc
        copies = []
        for k in range(n):
            cp = pltpu.make_async_remote_copy(src_ref=g_ref.at[k, oc], dst_ref=out_ref.at[k], send_sem=send_sem.at[k],
                                              recv_sem=recv_sem.at[k], device_id=(mx, my, oc), device_id_type=MESH)
            cp.start()
            copies.append(cp)
        for cp in copies:
            cp.wait_recv()
        for cp in copies:
            cp.wait_send()

    return _pcall(body, name=name, in_specs=[HBM_SPEC], out_specs=HBM_SPEC, out_shape=_sds((n, h, c), g4.dtype),
                  scratch=[pltpu.SemaphoreType.DMA((n,)), pltpu.SemaphoreType.DMA((n,))])(g4)


def _pair_add(g4, recv, core, name):
    n, _, h, c = g4.shape
    tm = 8
    for cand in (400, 200, 120, 80, 40, 16, 8):
        if h % cand == 0:
            tm = cand
            break

    def body(core_ref, a_ref, b_ref, o_ref):
        o_ref[...] = a_ref[...] + b_ref[...]

    grid_spec = pltpu.PrefetchScalarGridSpec(
        num_scalar_prefetch=1, grid=(n, h // tm),
        in_specs=[pl.BlockSpec((None, None, tm, c), lambda k, i, cr: (k, cr[0], i, 0)),
                  pl.BlockSpec((None, tm, c), lambda k, i, cr: (k, i, 0))],
        out_specs=pl.BlockSpec((None, tm, c), lambda k, i, cr: (k, i, 0)))
    return pl.pallas_call(body, out_shape=_sds((n, h, c), F32), grid_spec=grid_spec, name=name,
                          compiler_params=pltpu.CompilerParams(vmem_limit_bytes=VMEM_LIMIT_BYTES,
                                                               dimension_semantics=("parallel", "parallel")))(core, g4, recv)


def _chip_exchange(p, name):
    n, h, c = p.shape

    def body(p_ref, out_ref, send_sems, recv_sems):
        mx, my, mc = _my_pos()
        copies = []
        for j, (px, py, pk) in enumerate(_chip_flips(mx, my)):
            cp = pltpu.make_async_remote_copy(src_ref=p_ref.at[pk], dst_ref=out_ref.at[j], send_sem=send_sems.at[j],
                                              recv_sem=recv_sems.at[j], device_id=(px, py, mc), device_id_type=MESH)
            cp.start()
            copies.append(cp)
        for cp in copies:
            cp.wait_recv()
        for cp in copies:
            cp.wait_send()

    return _pcall(body, name=name, in_specs=[HBM_SPEC], out_specs=HBM_SPEC, out_shape=_sds((3, h, c), p.dtype),
                  scratch=[pltpu.SemaphoreType.DMA((3,)), pltpu.SemaphoreType.DMA((3,))])(p)


def _chip_sum(p, slots, chip, name):
    _, h, c = p.shape
    tm = 8
    for cand in (400, 200, 120, 80, 40, 16, 8):
        if h % cand == 0:
            tm = cand
            break

    def body(chip_ref, p_ref, q_ref, o_ref):
        o_ref[...] = ((p_ref[...] + q_ref[0]) + q_ref[1]) + q_ref[2]

    return _pcall(body, name=name, grid=(h // tm,), prefetch=1,
                  in_specs=[pl.BlockSpec((None, tm, c), lambda i, ch: (ch[0], i, 0)),
                            pl.BlockSpec((3, tm, c), lambda i, ch: (0, i, 0))],
                  out_specs=pl.BlockSpec((tm, c), lambda i, ch: (i, 0)),
                  out_shape=_sds((h, c), F32), sem=("parallel",))(chip, p, slots)


def _sum_slots(q, name):
    n, h, c = q.shape
    tm = 8
    for cand in (400, 200, 120, 80, 40, 16, 8):
        if h % cand == 0:
            tm = cand
            break

    def body(q_ref, o_ref):
        acc = q_ref[0]
        for k in range(1, n):
            acc = acc + q_ref[k]
        o_ref[...] = acc

    return _pcall(body, name=name, grid=(h // tm,),
                  in_specs=[pl.BlockSpec((n, tm, c), lambda i: (0, i, 0))],
                  out_specs=pl.BlockSpec((tm, c), lambda i: (i, 0)),
                  out_shape=_sds((h, c), F32), sem=("parallel",))(q)


def _pair_share(f, name):
    h, c = f.shape

    def body(f_ref, out_ref, send_sem, recv_sem):
        mx, my, mc = _my_pos()
        cp = pltpu.make_async_remote_copy(src_ref=f_ref, dst_ref=out_ref, send_sem=send_sem, recv_sem=recv_sem,
                                          device_id=(mx, my, 1 - mc), device_id_type=MESH)
        cp.start()
        cp.wait_recv()
        cp.wait_send()

    return _pcall(body, name=name, in_specs=[HBM_SPEC], out_specs=HBM_SPEC, out_shape=_sds((h, c), f.dtype),
                  scratch=[pltpu.SemaphoreType.DMA, pltpu.SemaphoreType.DMA])(f)


BIG = ("mlp_w1", "mlp_w2", "fox_w_in", "fox_w_o", "ssm_w_in", "ssm_w_out")
SMALL_SHARDED = ("ssm_conv_w", "ssm_conv_b", "ssm_norm_w")
PACK_COLS = 1024


def _pack_rows(parts, rows_multiple, dtype):
    flat = jnp.concatenate([p.reshape(-1).astype(dtype) for p in parts])
    unit = rows_multiple * PACK_COLS
    total = -(-flat.shape[0] // unit) * unit
    flat = jnp.pad(flat, (0, total - flat.shape[0]))
    return flat.reshape(total // PACK_COLS, PACK_COLS)


def _unpack(flat, shapes):
    out, off = [], 0
    for sh in shapes:
        n = 1
        for d_ in sh:
            n *= d_
        out.append(flat[off:off + n].reshape(sh))
        off += n
    return out


def _chip_slice(full, axis, k, width):
    idx = [slice(None)] * full.ndim
    idx[axis] = slice(k * width, (k + 1) * width)
    return full[tuple(idx)]


SHARD_AXIS = dict(mlp_w1=2, mlp_w2=1, fox_w_in=2, fox_w_o=1, ssm_w_in=2, ssm_w_out=1, ssm_conv_w=2,
                  ssm_conv_b=1, ssm_norm_w=1, ada_w=2)


def kernel(x, c, ada_w, ada_b, ln_mix_g, ln_mix_b, ln_mlp_g, ln_mlp_b, mlp_w1, mlp_w2, fox_w_in, fox_b_f, fox_w_o, ssm_w_in, ssm_conv_w, ssm_conv_b, ssm_dt_bias, ssm_a_log, ssm_d, ssm_norm_w, ssm_w_out, loss_target, m_ada_w, m_ada_b, m_ln_mix_g, m_ln_mix_b, m_ln_mlp_g, m_ln_mlp_b, m_mlp_w1, m_mlp_w2, m_fox_w_in, m_fox_b_f, m_fox_w_o, m_ssm_w_in, m_ssm_conv_w, m_ssm_conv_b, m_ssm_dt_bias, m_ssm_a_log, m_ssm_d, m_ssm_norm_w, m_ssm_w_out, v_ada_w, v_ada_b, v_ln_mix_g, v_ln_mix_b, v_ln_mlp_g, v_ln_mlp_b, v_mlp_w1, v_mlp_w2, v_fox_w_in, v_fox_b_f, v_fox_w_o, v_ssm_w_in, v_ssm_conv_w, v_ssm_conv_b, v_ssm_dt_bias, v_ssm_a_log, v_ssm_d, v_ssm_norm_w, v_ssm_w_out):
    names = ("ada_w", "ada_b", "ln_mix_g", "ln_mix_b", "ln_mlp_g", "ln_mlp_b", "mlp_w1", "mlp_w2", "fox_w_in",
             "fox_b_f", "fox_w_o", "ssm_w_in", "ssm_conv_w", "ssm_conv_b", "ssm_dt_bias", "ssm_a_log", "ssm_d",
             "ssm_norm_w", "ssm_w_out")
    weights = dict(zip(names, (ada_w, ada_b, ln_mix_g, ln_mix_b, ln_mlp_g, ln_mlp_b, mlp_w1, mlp_w2, fox_w_in,
                               fox_b_f, fox_w_o, ssm_w_in, ssm_conv_w, ssm_conv_b, ssm_dt_bias, ssm_a_log, ssm_d,
                               ssm_norm_w, ssm_w_out)))
    m_in = dict(zip(names, (m_ada_w, m_ada_b, m_ln_mix_g, m_ln_mix_b, m_ln_mlp_g, m_ln_mlp_b, m_mlp_w1, m_mlp_w2,
                            m_fox_w_in, m_fox_b_f, m_fox_w_o, m_ssm_w_in, m_ssm_conv_w, m_ssm_conv_b, m_ssm_dt_bias,
                            m_ssm_a_log, m_ssm_d, m_ssm_norm_w, m_ssm_w_out)))
    v_in = dict(zip(names, (v_ada_w, v_ada_b, v_ln_mix_g, v_ln_mix_b, v_ln_mlp_g, v_ln_mlp_b, v_mlp_w1, v_mlp_w2,
                            v_fox_w_in, v_fox_b_f, v_fox_w_o, v_ssm_w_in, v_ssm_conv_w, v_ssm_conv_b, v_ssm_dt_bias,
                            v_ssm_a_log, v_ssm_d, v_ssm_norm_w, v_ssm_w_out)))

    mx, my, mc = _my_pos()
    chip = 2 * mx + my
    me = 4 * mx + 2 * my + mc
    x0 = x[0]
    target = loss_target[0]
    s, d = x0.shape
    n_qkv = 3 * FOX_HEADS * HEAD_DIM

    big_shapes = [weights[n].shape for n in BIG]
    packed = _pack_rows([weights[n] for n in BIG], 32, BF16)
    gathered = _gather_chips(packed.reshape(2, packed.shape[0] // 2, PACK_COLS), "gather_weights")
    gathered = gathered.reshape(4, packed.shape[0], PACK_COLS)
    per_chip = [_unpack(jnp.where(chip == k, packed, gathered[k]).reshape(-1), big_shapes) for k in range(4)]
    full = {n: jnp.concatenate([per_chip[k][i] for k in range(4)], axis=SHARD_AXIS[n]) for i, n in enumerate(BIG)}

    small_shapes = [weights[n].shape for n in SMALL_SHARDED]
    small_packed = _pack_rows([weights[n] for n in SMALL_SHARDED] + [c], 8, F32).reshape(-1, LANES)
    small_all = _allgather8(small_packed, "gather_small")
    small_chip = [_unpack(small_all[2 * k].reshape(-1), small_shapes) for k in range(4)]
    small_full = {n: jnp.concatenate([small_chip[k][i] for k in range(4)], axis=SHARD_AXIS[n])
                  for i, n in enumerate(SMALL_SHARDED)}
    n_small = sum(weights[n].size for n in SMALL_SHARDED)
    c_all = small_all.reshape(8, -1)[:, n_small:n_small + d]

    cols = ada_w.shape[2]
    ada_b_cols = lax.dynamic_slice_in_dim(ada_b, chip * cols, cols, axis=1)[:, None, :]
    c16 = jnp.pad(c_all, ((0, 8), (0, 0)))
    mod_part = _ada_forward(c16, ada_w, ada_b_cols, "ada_fwd")[:, :8, :]
    mod_all = _allgather8(mod_part.reshape(-1, LANES), "gather_mod").reshape(8, DEPTH, 8, cols)
    mod_mine = jnp.stack([lax.dynamic_index_in_dim(mod_all[2 * k], me, axis=1, keepdims=False) for k in range(4)], axis=1)
    mods = [jnp.pad(mod_mine[i].reshape(6, d), ((0, 2), (0, 0))) for i in range(DEPTH)]

    w = dict(
        fox_qkv=full["fox_w_in"][0][:, :n_qkv],
        fox_f=jnp.pad(full["fox_w_in"][0][:, n_qkv:], ((0, 0), (0, LANES - FOX_HEADS))),
        fox_in_pad=jnp.pad(full["fox_w_in"][0], ((0, 0), (0, LANES - FOX_HEADS))),
        fox_o=full["fox_w_o"][0],
        fox_bf=jnp.pad(fox_b_f, ((0, 0), (0, LANES - FOX_HEADS))),
        head_ind=(jnp.arange(d)[:, None] // HEAD_DIM == jnp.arange(LANES)[None, :]).astype(BF16),
        ssm_z=full["ssm_w_in"][0][:, :2048],
        ssm_xbc=full["ssm_w_in"][0][:, 2048:6144],
        ssm_dt=jnp.pad(full["ssm_w_in"][0][:, 6144:], ((0, 0), (0, LANES - SSM_HEADS))),
        ssm_in_pad=jnp.pad(full["ssm_w_in"][0], ((0, 0), (0, LANES - SSM_HEADS))),
        ssm_out=full["ssm_w_out"][0],
        conv_w8=jnp.pad(small_full["ssm_conv_w"][0], ((0, 8 - SSM_CONV), (0, 0))),
        conv_b=small_full["ssm_conv_b"],
        norm_w=small_full["ssm_norm_w"],
        dt_bias=jnp.pad(ssm_dt_bias, ((0, 0), (0, LANES - SSM_HEADS))),
        a_log=jnp.pad(ssm_a_log, ((0, 0), (0, LANES - SSM_HEADS))),
        d_e=jnp.repeat(ssm_d.reshape(SSM_GROUPS, 4), HEAD_DIM, axis=1)[:, None, :],
    )
    mixers = ((_fox_forward, _fox_backward), (_ssd_forward, _ssd_backward))

    saved = []
    xin = x0
    for i in range(DEPTH):
        tag = "l%d" % i
        u = _modulate_in(xin, mods[i], tag + "_mod_in")
        y, sv = mixers[i % 2][0](u, w)
        r, x1, u2 = _res_ln_mod(xin, y, mods[i], ln_mix_g[i:i + 1], ln_mix_b[i:i + 1], tag + "_res_ln1")
        y2, hr, a = _mlp_forward(u2, full["mlp_w1"][i], full["mlp_w2"][i], tag)
        r2, x2 = _res_ln(x1, y2, mods[i], ln_mlp_g[i:i + 1], ln_mlp_b[i:i + 1], tag + "_res_ln2")
        saved.append(dict(xin=xin, y=y, r=r, u2=u2, hr=hr, a=a, y2=y2, r2=r2, mix=sv))
        xin = x2
    dx, sq = _loss_grad(xin, target, "loss_grad")
    loss = lax.psum(0.5 * jnp.sum(sq) / d, ("x", "y", "c"))

    grads = {}
    dmods = [None] * DEPTH
    ln_grads = {n: [None] * DEPTH for n in ("ln_mix_g", "ln_mix_b", "ln_mlp_g", "ln_mlp_b")}
    g_w1, g_w2 = [None] * DEPTH, [None] * DEPTH
    for i in reversed(range(DEPTH)):
        tag = "l%d" % i
        sv = saved[i]
        dy2, dres, dg2, db2, dgm = _ln2_bwd(dx, sv["r2"], sv["y2"], mods[i], ln_mlp_g[i:i + 1], tag + "_ln2_bwd")
        du2, g_w1[i], g_w2[i] = _mlp_backward(dy2, sv["u2"], sv["hr"], sv["a"], full["mlp_w1"][i], full["mlp_w2"][i], tag)
        dy, dres0, dscm, dshm, dg1, db1, dga = _ln1_bwd(du2, dres, sv["r"], sv["y"], mods[i], ln_mix_g[i:i + 1],
                                                        ln_mix_b[i:i + 1], tag + "_ln1_bwd")
        du, mg = mixers[i % 2][1](dy, sv["mix"], w)
        grads.update(mg)
        dx, dsca, dsha = _mod_in_bwd(du, dres0, sv["xin"], mods[i], tag + "_mod_in_bwd")
        dmods[i] = jnp.concatenate([dsha, dsca, dga, dshm, dscm, dgm], axis=1)
        ln_grads["ln_mix_g"][i], ln_grads["ln_mix_b"][i] = dg1, db1
        ln_grads["ln_mlp_g"][i], ln_grads["ln_mlp_b"][i] = dg2, db2
    grad_x = dx[None]
    grads["mlp_w1"] = jnp.stack(g_w1)
    grads["mlp_w2"] = jnp.stack(g_w2)
    for n in ("fox_w_in", "fox_w_o", "ssm_w_in", "ssm_w_out", "ssm_conv_w"):
        grads[n] = grads[n][None]

    small_names = ("ln_mix_g", "ln_mix_b", "ln_mlp_g", "ln_mlp_b", "fox_b_f", "ssm_dt_bias", "ssm_a_log", "ssm_d")
    small_parts = list(dmods)
    for n in small_names[:4]:
        small_parts.append(jnp.concatenate(ln_grads[n], axis=0))
    for n in small_names[4:]:
        small_parts.append(jnp.pad(grads[n], ((0, 0), (0, LANES - grads[n].shape[1]))))
    small_vec = _pack_rows(small_parts, 1, F32).reshape(-1, LANES)
    small_vec = jnp.pad(small_vec, ((0, -small_vec.shape[0] % 8), (0, 0)))
    small_g_all = _allgather8(small_vec, "gather_small_grads")
    small_sum = _sum_slots(small_g_all, "sum_small_grads").reshape(-1)
    dmod_sum = small_sum[:DEPTH * 6 * d].reshape(DEPTH, 6 * d)
    off = DEPTH * 6 * d
    final = {"ada_b": dmod_sum}
    for n in small_names[:4]:
        final[n] = small_sum[off:off + DEPTH * d].reshape(DEPTH, d)
        off += DEPTH * d
    for n in small_names[4:]:
        width = weights[n].shape[1]
        final[n] = small_sum[off:off + width].reshape(1, width)
        off += LANES

    dmod_all = small_g_all.reshape(8, -1)[:, :DEPTH * 6 * d].reshape(8, DEPTH, 6 * d)
    dmod_cols = lax.dynamic_slice_in_dim(dmod_all, chip * cols, cols, axis=2).transpose(1, 0, 2)
    final["ada_w"] = _ada_backward(c_all.T, dmod_cols, "ada_bwd")

    sharded = BIG + SMALL_SHARDED
    shard_shapes = [weights[n].shape for n in sharded]
    per_target = []
    for k in range(4):
        parts = [_chip_slice(grads[n], SHARD_AXIS[n], k, weights[n].shape[SHARD_AXIS[n]]) for n in sharded]
        per_target.append(_pack_rows(parts, 16, F32))
    g_all = jnp.stack(per_target)
    rows = g_all.shape[1]
    g4 = g_all.reshape(4, 2, rows // 2, PACK_COLS)
    recv = _pair_exchange(g4, "rs_pair_exchange")
    part = _pair_add(g4, recv, jnp.reshape(mc, (1,)).astype(jnp.int32), "rs_pair_add")
    slots = _chip_exchange(part, "rs_chip_exchange")
    half = _chip_sum(part, slots, jnp.reshape(chip, (1,)).astype(jnp.int32), "rs_chip_sum")
    other = _pair_share(half, "rs_pair_share")
    both = jnp.concatenate([jnp.where(mc == 0, half, other), jnp.where(mc == 0, other, half)], axis=0)
    for n, g_shard in zip(sharded, _unpack(both.reshape(-1), shard_shapes)):
        final[n] = g_shard

    outs_g, outs_d, outs_m, outs_v = [], [], [], []
    for n in names:
        wv = weights[n]
        two_d = (-1, wv.shape[-1])
        delta, mn, vn = _adamw(wv.reshape(two_d), final[n].reshape(two_d), m_in[n].reshape(two_d),
                               v_in[n].reshape(two_d), "adamw_" + n)
        outs_g.append(final[n].reshape(wv.shape))
        outs_d.append(delta.reshape(wv.shape))
        outs_m.append(mn.reshape(wv.shape))
        outs_v.append(vn.reshape(wv.shape))
    return (loss, grad_x, *outs_g, *outs_d, *outs_m, *outs_v)
```

```python
import functools

import jax
import jax.numpy as jnp
from jax import lax
from jax.experimental import pallas as pl
from jax.experimental.pallas import tpu as pltpu

F32, BF16 = jnp.float32, jnp.bfloat16
MESH = pl.DeviceIdType.MESH
HBM_SPEC = pl.BlockSpec(memory_space=pltpu.HBM)

VMEM_LIMIT_BYTES = 52 * 2**20
LANES = 128

FOX_HEADS, HEAD_DIM = 16, 64
SSM_HEADS, SSM_GROUPS, SSM_STATE, SSM_CHUNK, SSM_CONV = 32, 8, 128, 128, 4
SSM_GROUP_WIDTH = 256
LN_EPS, RMS_EPS = 1e-5, 1e-5
DEPTH = 2
ALPHA = (2.0 * DEPTH) ** 0.25
ADAM_LR, ADAM_B1, ADAM_B2, ADAM_EPS, ADAM_WD, ADAM_STEP = 0.001, 0.9, 0.999, 1e-08, 0.01, 10

ATT_TILE = 512
ROW_TILE = 256
SCAN_TILE = 512
MM_TM, MM_TN, MM_TK = 1024, 1024, 1024

NT_DIMS = (((1,), (1,)), ((), ()))
TN_DIMS = (((0,), (0,)), ((), ()))
NN_DIMS = (((1,), (0,)), ((), ()))


def _pcall(body, *, name, out_shape, grid=(), in_specs=None, out_specs=None, scratch=(), sem=None, prefetch=0):
    params = dict(vmem_limit_bytes=VMEM_LIMIT_BYTES)
    if sem is not None:
        params["dimension_semantics"] = sem
    if prefetch:
        grid_spec = pltpu.PrefetchScalarGridSpec(num_scalar_prefetch=prefetch, grid=grid, in_specs=in_specs,
                                                 out_specs=out_specs, scratch_shapes=scratch)
        return pl.pallas_call(body, out_shape=out_shape, grid_spec=grid_spec, name=name,
                              compiler_params=pltpu.CompilerParams(**params))
    kwargs = {}
    if in_specs is not None:
        kwargs["in_specs"] = in_specs
    if out_specs is not None:
        kwargs["out_specs"] = out_specs
    return pl.pallas_call(body, out_shape=out_shape, grid=grid, scratch_shapes=scratch, name=name,
                          compiler_params=pltpu.CompilerParams(**params), **kwargs)


def _sds(shape, dtype):
    return jax.ShapeDtypeStruct(tuple(shape), dtype)


def _dot(a, b, dims=NN_DIMS):
    return lax.dot_general(a, b, dims, preferred_element_type=F32)


def _sigmoid(x):
    return 1.0 / (1.0 + jnp.exp(-x))


def _silu(x):
    return x * _sigmoid(x)


def _dsilu(x):
    s = _sigmoid(x)
    return s * (1.0 + x * (1.0 - s))


def _dot_split(x, m16, dims=NN_DIMS, passes=3):
    hi = x.astype(BF16)
    r1 = x - hi.astype(F32)
    mid = r1.astype(BF16)
    out = _dot(hi, m16, dims) + _dot(mid, m16, dims)
    if passes == 3:
        lo = (r1 - mid.astype(F32)).astype(BF16)
        out = out + _dot(lo, m16, dims)
    return out


def _mm(a, b, dims, outs, *, name, tm=MM_TM, tn=MM_TN, tk=MM_TK, epi=None, extra=()):
    if dims == "nn":
        (m, k), (k2, n) = a.shape, b.shape
    elif dims == "nt":
        (m, k), (n, k2) = a.shape, b.shape
    else:
        (k, m), (k2, n) = a.shape, b.shape
    assert k == k2, (a.shape, b.shape, dims)
    tm, tn, tk = min(tm, m), min(tn, n), min(tk, k)
    assert m % tm == 0 and n % tn == 0 and k % tk == 0, (m, n, k, tm, tn, tk)
    nk = k // tk
    dn = {"nn": NN_DIMS, "nt": NT_DIMS, "tn": TN_DIMS}[dims]
    n_extra, n_out = len(extra), len(outs)
    if epi is None:
        epi = lambda acc: (acc,) * n_out

    def body(a_ref, b_ref, *rest):
        extra_refs, out_refs, acc_ref = rest[:n_extra], rest[n_extra:n_extra + n_out], rest[-1]
        kk = pl.program_id(2)

        @pl.when(kk == 0)
        def _():
            acc_ref[...] = jnp.zeros_like(acc_ref)

        acc_ref[...] += _dot(a_ref[...].astype(BF16), b_ref[...].astype(BF16), dn)

        @pl.when(kk == nk - 1)
        def _():
            res = epi(acc_ref[...], *[e[...] for e in extra_refs])
            for o, r in zip(out_refs, res):
                o[...] = r.astype(o.dtype)

    if dims == "tn":
        a_spec = pl.BlockSpec((tk, tm), lambda i, j, kk: (kk, i))
    else:
        a_spec = pl.BlockSpec((tm, tk), lambda i, j, kk: (i, kk))
    if dims == "nt":
        b_spec = pl.BlockSpec((tn, tk), lambda i, j, kk: (j, kk))
    else:
        b_spec = pl.BlockSpec((tk, tn), lambda i, j, kk: (kk, j))
    o_spec = pl.BlockSpec((tm, tn), lambda i, j, kk: (i, j))
    res = _pcall(body, name=name, grid=(m // tm, n // tn, nk),
                 in_specs=[a_spec, b_spec] + [o_spec] * n_extra,
                 out_specs=[o_spec] * n_out,
                 out_shape=[_sds((m, n), d) for d in outs],
                 scratch=[pltpu.VMEM((tm, tn), F32)],
                 sem=("parallel", "parallel", "arbitrary"))(a, b, *extra)
    return res[0] if n_out == 1 else res


def _rowwise(fn, rows, consts, row_outs, acc_outs, *, name, tm=ROW_TILE):
    s = rows[0].shape[0]
    tm = min(tm, s)
    assert s % tm == 0
    n_in, n_o = len(rows) + len(consts), len(row_outs)

    def body(*refs):
        ins, outs = refs[:n_in], refs[n_in:]
        res = fn(*[r[...] for r in ins])
        if not isinstance(res, (tuple, list)):
            res = (res,)
        for o, val in zip(outs[:n_o], res[:n_o]):
            o[...] = val.astype(o.dtype)
        if acc_outs:
            @pl.when(pl.program_id(0) == 0)
            def _():
                for o in outs[n_o:]:
                    o[...] = jnp.zeros_like(o)
            for o, val in zip(outs[n_o:], res[n_o:]):
                o[...] += val

    in_specs = [pl.BlockSpec((tm, r.shape[1]), lambda i: (i, 0)) for r in rows]
    in_specs += [pl.BlockSpec(c.shape, functools.partial(lambda nd, i: (0,) * nd, c.ndim)) for c in consts]
    out_specs = [pl.BlockSpec((tm, c), lambda i: (i, 0)) for c, _ in row_outs]
    out_specs += [pl.BlockSpec(tuple(sh), lambda i: (0, 0)) for sh in acc_outs]
    out_shape = [_sds((s, c), d) for c, d in row_outs] + [_sds(sh, F32) for sh in acc_outs]
    res = _pcall(body, name=name, grid=(s // tm,), in_specs=in_specs, out_specs=out_specs,
                 out_shape=out_shape, sem=("arbitrary",))(*rows, *consts)
    return res


def _colsum(x):
    return jnp.sum(x, axis=0, keepdims=True)


def _ln_stats(r):
    mu = jnp.mean(r, axis=-1, keepdims=True)
    xc = r - mu
    var = jnp.mean(xc * xc, axis=-1, keepdims=True)
    rstd = lax.rsqrt(var + LN_EPS)
    return xc * rstd, rstd


def _ln_bwd(dy, xhat, rstd, gamma):
    dyg = dy * gamma
    m1 = jnp.mean(dyg, axis=-1, keepdims=True)
    m2 = jnp.mean(dyg * xhat, axis=-1, keepdims=True)
    return rstd * (dyg - m1 - xhat * m2)


def _modulate_in(x, mod, name):
    def fn(xv, m):
        return (xv * (1.0 + m[1:2]) + m[0:1],)
    return _rowwise(fn, [x], [mod], [(x.shape[1], BF16)], [], name=name)[0]


def _res_ln_mod(x, y, mod, g, b, name):
    d = x.shape[1]

    def fn(xv, yv, m, gv, bv):
        r = ALPHA * xv + (1.0 + m[2:3]) * yv
        xhat, _ = _ln_stats(r)
        x1 = xhat * gv + bv
        u2 = x1 * (1.0 + m[4:5]) + m[3:4]
        return r, x1, u2
    return _rowwise(fn, [x, y], [mod, g, b], [(d, F32), (d, F32), (d, BF16)], [], name=name)


def _res_ln(x, y, mod, g, b, name):
    d = x.shape[1]

    def fn(xv, yv, m, gv, bv):
        r = ALPHA * xv + (1.0 + m[5:6]) * yv
        xhat, _ = _ln_stats(r)
        return r, xhat * gv + bv
    return _rowwise(fn, [x, y], [mod, g, b], [(d, F32), (d, F32)], [], name=name)


def _res_ln_next(x, y, mod, g, b, mod_next, name):
    d = x.shape[1]

    def fn(xv, yv, m, gv, bv, mn):
        r = ALPHA * xv + (1.0 + m[5:6]) * yv
        xhat, _ = _ln_stats(r)
        out = xhat * gv + bv
        return r, out, out * (1.0 + mn[1:2]) + mn[0:1]
    return _rowwise(fn, [x, y], [mod, g, b, mod_next], [(d, F32), (d, F32), (d, BF16)], [], name=name)


def _loss_ln2_bwd(r2, y2, target, mod, g, b, name):
    d = r2.shape[1]

    def fn(rv, yv, tv, m, gv, bv):
        xhat, rstd = _ln_stats(rv)
        e = xhat * gv + bv - tv
        dxv = e * (1.0 / d)
        dr = _ln_bwd(dxv, xhat, rstd, gv)
        return (dr * (1.0 + m[5:6]), ALPHA * dr,
                _colsum(e * e), _colsum(dxv * xhat), _colsum(dxv), _colsum(dr * yv))
    return _rowwise(fn, [r2, y2, target], [mod, g, b], [(d, BF16), (d, F32)], [(1, d)] * 4, name=name)


def _mod_in_ln2_bwd(du, dres, r2, y2, mod, g, b, mod_next, name):
    d = du.shape[1]

    def fn(duv, drv, rv, yv, m, gv, bv, mn):
        xhat, rstd = _ln_stats(rv)
        xout = xhat * gv + bv
        dxv = duv * (1.0 + mn[1:2]) + drv
        dr = _ln_bwd(dxv, xhat, rstd, gv)
        return (dr * (1.0 + m[5:6]), ALPHA * dr,
                _colsum(duv * xout), _colsum(duv), _colsum(dxv * xhat), _colsum(dxv), _colsum(dr * yv))
    return _rowwise(fn, [du, dres, r2, y2], [mod, g, b, mod_next], [(d, BF16), (d, F32)], [(1, d)] * 5, name=name)


def _ln2_bwd(dx, r2, y2, mod, g, name):
    d = dx.shape[1]

    def fn(dxv, rv, yv, m, gv):
        xhat, rstd = _ln_stats(rv)
        dr = _ln_bwd(dxv, xhat, rstd, gv)
        return (dr * (1.0 + m[5:6]), ALPHA * dr,
                _colsum(dxv * xhat), _colsum(dxv), _colsum(dr * yv))
    return _rowwise(fn, [dx, r2, y2], [mod, g], [(d, BF16), (d, F32)], [(1, d)] * 3, name=name)


def _ln1_bwd(du2, dres, r, y, mod, g, b, name):
    d = du2.shape[1]

    def fn(duv, drv, rv, yv, m, gv, bv):
        xhat, rstd = _ln_stats(rv)
        x1 = xhat * gv + bv
        dx1 = duv * (1.0 + m[4:5]) + drv
        dr = _ln_bwd(dx1, xhat, rstd, gv)
        return (dr * (1.0 + m[2:3]), ALPHA * dr,
                _colsum(duv * x1), _colsum(duv), _colsum(dx1 * xhat), _colsum(dx1), _colsum(dr * yv))
    return _rowwise(fn, [du2, dres, r, y], [mod, g, b], [(d, BF16), (d, F32)], [(1, d)] * 5, name=name)


def _mod_in_bwd(du, dres, x, mod, name):
    d = du.shape[1]

    def fn(duv, drv, xv, m):
        return duv * (1.0 + m[1:2]) + drv, _colsum(duv * xv), _colsum(duv)
    return _rowwise(fn, [du, dres, x], [mod], [(d, F32)], [(1, d)] * 2, name=name)


def _fox_gate(fraw, b_pad, name):
    s = fraw.shape[0]
    tb = min(SCAN_TILE, s)

    def body(f_ref, b_ref, cum_ref, carry):
        @pl.when(pl.program_id(0) == 0)
        def _():
            carry[...] = jnp.zeros_like(carry)
        z = f_ref[...] + b_ref[...]
        lf = jnp.minimum(z, 0.0) - jnp.log(1.0 + jnp.exp(-jnp.abs(z)))
        lane = lax.broadcasted_iota(jnp.int32, (tb, LANES), 1)
        row = lax.broadcasted_iota(jnp.int32, (tb, LANES), 0)
        c = jnp.where(lane < FOX_HEADS, lf, 0.0)
        sh = 1
        while sh < tb:
            c = c + jnp.where(row >= sh, pltpu.roll(c, sh, 0), 0.0)
            sh *= 2
        c = c + carry[0:1, :]
        cum_ref[...] = c
        carry[0:1, :] = c[tb - 1:tb, :]

    return _pcall(body, name=name, grid=(s // tb,),
                  in_specs=[pl.BlockSpec((tb, LANES), lambda i: (i, 0)), pl.BlockSpec((1, LANES), lambda i: (0, 0))],
                  out_specs=pl.BlockSpec((tb, LANES), lambda i: (i, 0)),
                  out_shape=_sds((s, LANES), F32), scratch=[pltpu.VMEM((8, LANES), F32)],
                  sem=("arbitrary",))(fraw, b_pad)


def _fox_gate_bwd(drow, dcol, fraw, b_pad, name):
    s = fraw.shape[0]
    tb = min(SCAN_TILE, s)
    n = s // tb

    def body(dr_ref, dc_ref, f_ref, b_ref, df_ref, db_ref, carry):
        @pl.when(pl.program_id(0) == 0)
        def _():
            carry[...] = jnp.zeros_like(carry)
            db_ref[...] = jnp.zeros_like(db_ref)
        row = lax.broadcasted_iota(jnp.int32, (tb, LANES), 0)
        c = dr_ref[...] + dc_ref[...]
        sh = 1
        while sh < tb:
            c = c + jnp.where(row + sh < tb, pltpu.roll(c, tb - sh, 0), 0.0)
            sh *= 2
        c = c + carry[0:1, :]
        carry[0:1, :] = c[0:1, :]
        z = f_ref[...] + b_ref[...]
        df = c * (1.0 / (1.0 + jnp.exp(z)))
        df_ref[...] = df.astype(df_ref.dtype)
        db_ref[...] += _colsum(df)

    rev = lambda i: (n - 1 - i, 0)
    return _pcall(body, name=name, grid=(n,),
                  in_specs=[pl.BlockSpec((tb, LANES), rev)] * 3 + [pl.BlockSpec((1, LANES), lambda i: (0, 0))],
                  out_specs=[pl.BlockSpec((tb, LANES), rev), pl.BlockSpec((1, LANES), lambda i: (0, 0))],
                  out_shape=[_sds((s, LANES), BF16), _sds((1, LANES), F32)],
                  scratch=[pltpu.VMEM((8, LANES), F32)], sem=("arbitrary",))(drow, dcol, fraw, b_pad)


def _head_pair_masks(t):
    lane = lax.broadcasted_iota(jnp.int32, (t, LANES), 1)
    return lane < HEAD_DIM


def _lane_blocks(x):
    return [x[:, c * LANES:(c + 1) * LANES] for c in range(x.shape[1] // LANES)]


def _sum_list(xs):
    acc = xs[0]
    for x in xs[1:]:
        acc = acc + x
    return acc


def _causal(t, transposed=False):
    ri = lax.broadcasted_iota(jnp.int32, (t, t), 0)
    ci = lax.broadcasted_iota(jnp.int32, (t, t), 1)
    return ci >= ri if transposed else ri >= ci


def _flash_fwd(qkv, ck_rows, kb_start, name):
    s = qkv.shape[0]
    t = min(ATT_TILE, s)
    nq = s // t
    scale = HEAD_DIM ** -0.5
    hp_blocks = FOX_HEADS // 2

    def body(ks_ref, q_ref, k_ref, v_ref, ck_ref, o_ref, lse_ref, acc_ref, m_ref, l_ref):
        hp, qb = pl.program_id(0), pl.program_id(1)
        q2 = q_ref[...] * scale
        first = _head_pair_masks(t)
        zero = jnp.zeros_like(q2)
        qs = (jnp.where(first, q2, zero), jnp.where(first, zero, q2))
        m_ref[...] = jnp.full_like(m_ref, -jnp.inf)
        l_ref[...] = jnp.zeros_like(l_ref)
        acc_ref[...] = jnp.zeros_like(acc_ref)

        def tile(kb, diagonal):
            off = pl.multiple_of(kb * t, t)
            k2 = k_ref[pl.ds(off, t), :]
            v2 = v_ref[pl.ds(off, t), :]
            ck = ck_ref[kb]
            pvs, als = [], []
            for j in range(2):
                sc = _dot(qs[j], k2, NT_DIMS) - ck[j:j + 1, :]
                if diagonal:
                    sc = jnp.where(_causal(t), sc, -jnp.inf)
                blocks = _lane_blocks(sc)
                mx = blocks[0]
                for b in blocks[1:]:
                    mx = jnp.maximum(mx, b)
                m_old = m_ref[j]
                m_new = jnp.maximum(m_old, jnp.max(mx, axis=1, keepdims=True))
                ps = [jnp.exp(b - m_new) for b in blocks]
                a = jnp.exp(m_old - m_new)
                l_ref[j] = a * l_ref[j] + _sum_list(ps)
                m_ref[j] = m_new
                pvs.append(_dot(jnp.concatenate(ps, axis=1).astype(BF16), v2))
                als.append(a)
            acc_ref[...] = jnp.where(first, als[0], als[1]) * acc_ref[...] + jnp.where(first, pvs[0], pvs[1])

        def step(kb, carry):
            tile(kb, False)
            return carry

        lax.fori_loop(ks_ref[hp, qb], qb, step, 0)
        tile(qb, True)
        l0 = jnp.sum(l_ref[0], axis=1, keepdims=True)
        l1 = jnp.sum(l_ref[1], axis=1, keepdims=True)
        o_ref[...] = acc_ref[...] / jnp.where(first, l0, l1)
        lse_ref[:, 0:1] = m_ref[0][:, 0:1] + jnp.log(l0)
        lse_ref[:, 1:2] = m_ref[1][:, 0:1] + jnp.log(l1)

    return _pcall(
        body, name=name, grid=(hp_blocks, nq), prefetch=1,
        in_specs=[pl.BlockSpec((t, LANES), lambda h, i, ks: (i, h)),
                  pl.BlockSpec((s, LANES), lambda h, i, ks: (0, hp_blocks + h)),
                  pl.BlockSpec((s, LANES), lambda h, i, ks: (0, 2 * hp_blocks + h)),
                  pl.BlockSpec((None, nq, 2, t), lambda h, i, ks: (h, 0, 0, 0))],
        out_specs=[pl.BlockSpec((t, LANES), lambda h, i, ks: (i, h)),
                   pl.BlockSpec((None, t, 2), lambda h, i, ks: (h, i, 0))],
        out_shape=[_sds((s, hp_blocks * LANES), F32), _sds((hp_blocks, s, 2), F32)],
        scratch=[pltpu.VMEM((t, LANES), F32), pltpu.VMEM((2, t, LANES), F32), pltpu.VMEM((2, t, LANES), F32)],
        sem=("parallel", "arbitrary"))(kb_start, qkv, qkv, qkv, ck_rows)


def _flash_dq(qkv, do16, ck_rows, lse_c, dl_c, kb_start, name):
    s = qkv.shape[0]
    t = min(ATT_TILE, s)
    nq = s // t
    scale = HEAD_DIM ** -0.5
    hp_blocks = FOX_HEADS // 2

    def body(ks_ref, q_ref, do_ref, k_ref, v_ref, ck_ref, lse_ref, dl_ref, dq_ref, drow_ref, acc_ref, row_acc):
        hp, qb = pl.program_id(0), pl.program_id(1)
        q2, do2 = q_ref[...] * scale, do_ref[...]
        first = _head_pair_masks(t)
        zero = jnp.zeros_like(q2)
        qs = (jnp.where(first, q2, zero), jnp.where(first, zero, q2))
        dos = (jnp.where(first, do2, zero), jnp.where(first, zero, do2))
        lse, dl = lse_ref[...], dl_ref[...]
        lse_b = [jnp.broadcast_to(lse[:, j:j + 1], (t, LANES)) for j in range(2)]
        dl_b = [jnp.broadcast_to(dl[:, j:j + 1], (t, LANES)) for j in range(2)]
        acc_ref[...] = jnp.zeros_like(acc_ref)
        row_acc[...] = jnp.zeros_like(row_acc)

        def tile(kb, diagonal):
            off = pl.multiple_of(kb * t, t)
            k2 = k_ref[pl.ds(off, t), :]
            v2 = v_ref[pl.ds(off, t), :]
            ck = ck_ref[kb]
            dqs = []
            for j in range(2):
                sc = _dot(qs[j], k2, NT_DIMS) - ck[j:j + 1, :]
                if diagonal:
                    sc = jnp.where(_causal(t), sc, -jnp.inf)
                dp = _dot(dos[j], v2, NT_DIMS)
                dsb = [jnp.exp(x - lse_b[j]) * (d - dl_b[j]) for x, d in zip(_lane_blocks(sc), _lane_blocks(dp))]
                row_acc[j] += _sum_list(dsb)
                dqs.append(_dot(jnp.concatenate(dsb, axis=1).astype(BF16), k2))
            acc_ref[...] += jnp.where(first, dqs[0], dqs[1])

        def step(kb, carry):
            tile(kb, False)
            return carry

        lax.fori_loop(ks_ref[hp, qb], qb, step, 0)
        tile(qb, True)
        dq_ref[...] = (acc_ref[...] * scale).astype(dq_ref.dtype)
        drow_ref[:, 0:1] = jnp.sum(row_acc[0], axis=1, keepdims=True)
        drow_ref[:, 1:2] = jnp.sum(row_acc[1], axis=1, keepdims=True)

    return _pcall(
        body, name=name, grid=(hp_blocks, nq), prefetch=1,
        in_specs=[pl.BlockSpec((t, LANES), lambda h, i, ks: (i, h)),
                  pl.BlockSpec((t, LANES), lambda h, i, ks: (i, h)),
                  pl.BlockSpec((s, LANES), lambda h, i, ks: (0, hp_blocks + h)),
                  pl.BlockSpec((s, LANES), lambda h, i, ks: (0, 2 * hp_blocks + h)),
                  pl.BlockSpec((None, nq, 2, t), lambda h, i, ks: (h, 0, 0, 0)),
                  pl.BlockSpec((None, t, 2), lambda h, i, ks: (h, i, 0)),
                  pl.BlockSpec((None, t, 2), lambda h, i, ks: (h, i, 0))],
        out_specs=[pl.BlockSpec((t, LANES), lambda h, i, ks: (i, h)),
                   pl.BlockSpec((None, t, 2), lambda h, i, ks: (h, i, 0))],
        out_shape=[_sds((s, hp_blocks * LANES), BF16), _sds((hp_blocks, s, 2), F32)],
        scratch=[pltpu.VMEM((t, LANES), F32), pltpu.VMEM((2, t, LANES), F32)],
        sem=("parallel", "arbitrary"))(kb_start, qkv, do16, qkv, qkv, ck_rows, lse_c, dl_c)


def _flash_dkv(qkv, do16, cum, lse_rows, dl_rows, qb_end, name):
    s = qkv.shape[0]
    t = min(ATT_TILE, s)
    nq = s // t
    scale = HEAD_DIM ** -0.5
    hp_blocks = FOX_HEADS // 2

    def body(qe_ref, k_ref, v_ref, cum_ref, q_ref, do_ref, lse_ref, dl_ref, dk_ref, dv_ref, dck_ref,
             dk_acc, dv_acc, dck_acc):
        hp, kb = pl.program_id(0), pl.program_id(1)
        k2, v2 = k_ref[...] * scale, v_ref[...]
        first = _head_pair_masks(t)
        zero = jnp.zeros_like(k2)
        ks = (jnp.where(first, k2, zero), jnp.where(first, zero, k2))
        vs = (jnp.where(first, v2, zero), jnp.where(first, zero, v2))
        cumv = cum_ref[...]
        lane = lax.broadcasted_iota(jnp.int32, (t, LANES), 1)
        ck_b = [jnp.broadcast_to(jnp.sum(jnp.where(lane == 2 * hp + j, cumv, 0.0), axis=1, keepdims=True), (t, LANES))
                for j in range(2)]
        dk_acc[...] = jnp.zeros_like(dk_acc)
        dv_acc[...] = jnp.zeros_like(dv_acc)
        dck_acc[...] = jnp.zeros_like(dck_acc)

        def tile(qb, diagonal):
            off = pl.multiple_of(qb * t, t)
            q2 = q_ref[pl.ds(off, t), :]
            do2 = do_ref[pl.ds(off, t), :]
            lse, dl = lse_ref[qb], dl_ref[qb]
            dvs, dks = [], []
            for j in range(2):
                sc = _dot(ks[j], q2, NT_DIMS)
                if diagonal:
                    sc = jnp.where(_causal(t, transposed=True), sc, -jnp.inf)
                dp = _dot(vs[j], do2, NT_DIMS) - dl[j:j + 1, :]
                pb = [jnp.exp((x - ck_b[j]) - l) for x, l in zip(_lane_blocks(sc), _lane_blocks(lse[j:j + 1, :]))]
                dsb = [p * d for p, d in zip(pb, _lane_blocks(dp))]
                dck_acc[j] += _sum_list(dsb)
                dvs.append(_dot(jnp.concatenate(pb, axis=1).astype(BF16), do2))
                dks.append(_dot(jnp.concatenate(dsb, axis=1).astype(BF16), q2))
            dv_acc[...] += jnp.where(first, dvs[0], dvs[1])
            dk_acc[...] += jnp.where(first, dks[0], dks[1])

        def step(qb, carry):
            tile(qb, False)
            return carry

        tile(kb, True)
        lax.fori_loop(kb + 1, qe_ref[hp, kb] + 1, step, 0)
        dk_ref[...] = (dk_acc[...] * scale).astype(dk_ref.dtype)
        dv_ref[...] = dv_acc[...].astype(dv_ref.dtype)
        dck_ref[:, 0:1] = -jnp.sum(dck_acc[0], axis=1, keepdims=True)
        dck_ref[:, 1:2] = -jnp.sum(dck_acc[1], axis=1, keepdims=True)

    return _pcall(
        body, name=name, grid=(hp_blocks, nq), prefetch=1,
        in_specs=[pl.BlockSpec((t, LANES), lambda h, j, qe: (j, hp_blocks + h)),
                  pl.BlockSpec((t, LANES), lambda h, j, qe: (j, 2 * hp_blocks + h)),
                  pl.BlockSpec((t, LANES), lambda h, j, qe: (j, 0)),
                  pl.BlockSpec((s, LANES), lambda h, j, qe: (0, h)),
                  pl.BlockSpec((s, LANES), lambda h, j, qe: (0, h)),
                  pl.BlockSpec((None, nq, 2, t), lambda h, j, qe: (h, 0, 0, 0)),
                  pl.BlockSpec((None, nq, 2, t), lambda h, j, qe: (h, 0, 0, 0))],
        out_specs=[pl.BlockSpec((t, LANES), lambda h, j, qe: (j, h)),
                   pl.BlockSpec((t, LANES), lambda h, j, qe: (j, h)),
                   pl.BlockSpec((None, t, 2), lambda h, j, qe: (h, j, 0))],
        out_shape=[_sds((s, hp_blocks * LANES), BF16), _sds((s, hp_blocks * LANES), BF16),
                   _sds((hp_blocks, s, 2), F32)],
        scratch=[pltpu.VMEM((t, LANES), F32), pltpu.VMEM((t, LANES), F32), pltpu.VMEM((2, t, LANES), F32)],
        sem=("parallel", "arbitrary"))(qb_end, qkv, qkv, cum, qkv, do16, lse_rows, dl_rows)


SKIP_NATS = 110.0


def _qk_norms(qkv, ind16, name):
    d = FOX_HEADS * HEAD_DIM

    def fn(tile, ind):
        q = tile[:, :d].astype(F32)
        k = tile[:, d:2 * d].astype(F32)
        return _dot_split(q * q, ind), _dot_split(k * k, ind)
    return _rowwise(fn, [qkv], [ind16], [(LANES, F32), (LANES, F32)], [], name=name)


def _skip_bounds(qn, kn, cum, t):
    s = qn.shape[0]
    nq = s // t
    hp = FOX_HEADS // 2
    scale = HEAD_DIM ** -0.5
    qmax = jnp.sqrt(jnp.max(qn.reshape(nq, t, FOX_HEADS), axis=1))
    kmax = jnp.sqrt(jnp.max(kn, axis=0))
    bound = qmax * kmax[None, :] * (scale * 1.01) + 1e-3
    gap = cum[0::t][:, None, :] - cum[t - 1::t][None, :, :]
    idx = jnp.arange(nq, dtype=jnp.int32)
    needed = (gap + 2.0 * bound[:, None, :]) > -SKIP_NATS
    needed = needed.reshape(nq, nq, hp, 2).any(axis=-1) & (idx[None, :] <= idx[:, None])[:, :, None]
    first = jnp.min(jnp.where(needed, idx[None, :, None], nq), axis=1)
    first = jnp.minimum(first, idx[:, None])
    start = lax.cummin(first, axis=0, reverse=True)
    uses = start[:, None, :] <= idx[None, :, None]
    last = jnp.max(jnp.where(uses, idx[:, None, None], 0), axis=0)
    last = jnp.maximum(last, idx[:, None])
    return start.T.astype(jnp.int32), last.T.astype(jnp.int32)


def _head_rowsum(prod, ind16, name):
    def fn(a, b, ind):
        return (_dot_split(a * b, ind),)
    return _rowwise(fn, list(prod), [ind16], [(LANES, F32)], [], name=name)[0]


def _pairs_cols(x16):
    s = x16.shape[0]
    return x16.reshape(s, FOX_HEADS // 2, 2).transpose(1, 0, 2)


def _pairs_rows(x16, t):
    s = x16.shape[0]
    return x16.reshape(s // t, t, FOX_HEADS // 2, 2).transpose(2, 0, 3, 1)


def _fox_forward(u, w):
    s = u.shape[0]
    t = min(ATT_TILE, s)
    qkv = _mm(u, w["fox_qkv"], "nn", [BF16], name="fox_qkv")
    fraw = _mm(u, w["fox_f"], "nn", [F32], name="fox_fproj")
    cum = _fox_gate(fraw, w["fox_bf"], "fox_gate")
    ck_rows = _pairs_rows(cum[:, :FOX_HEADS], t)
    qn, kn = _qk_norms(qkv, w["head_ind"], "fox_qk_norms")
    kb_start, qb_end = _skip_bounds(qn[:, :FOX_HEADS], kn[:, :FOX_HEADS], cum[:, :FOX_HEADS], t)
    o, lse = _flash_fwd(qkv, ck_rows, kb_start, "fox_flash_fwd")
    y = _mm(o, w["fox_o"], "nn", [F32], name="fox_oproj")
    return y, dict(u=u, qkv=qkv, fraw=fraw, cum=cum, ck_rows=ck_rows, o=o, lse=lse, kb_start=kb_start,
                   qb_end=qb_end)


def _fox_backward(dy, sv, w):
    s = dy.shape[0]
    t = min(ATT_TILE, s)
    do32, do16 = _mm(dy, w["fox_o"], "nt", [F32, BF16], name="fox_do")
    g_wo = _mm(sv["o"], dy, "tn", [F32], name="fox_gwo")
    delta = _head_rowsum((do32, sv["o"]), w["head_ind"], "fox_delta")[:, :FOX_HEADS]
    lse16 = sv["lse"].transpose(1, 0, 2).reshape(s, FOX_HEADS)
    dq, drow = _flash_dq(sv["qkv"], do16, sv["ck_rows"], sv["lse"], _pairs_cols(delta), sv["kb_start"],
                         "fox_flash_dq")
    dk, dv, dck = _flash_dkv(sv["qkv"], do16, sv["cum"], _pairs_rows(lse16, t), _pairs_rows(delta, t),
                             sv["qb_end"], "fox_flash_dkv")
    pad = ((0, 0), (0, LANES - FOX_HEADS))
    dcol = jnp.pad(dck.transpose(1, 0, 2).reshape(s, FOX_HEADS), pad)
    drow = jnp.pad(drow.transpose(1, 0, 2).reshape(s, FOX_HEADS), pad)
    df, db_f = _fox_gate_bwd(drow, dcol, sv["fraw"], w["fox_bf"], "fox_gate_bwd")
    dproj = jnp.concatenate([dq, dk, dv, df], axis=1)
    du = _mm(dproj, w["fox_in_pad"], "nt", [F32], name="fox_du", tk=640)
    g_win = _mm(sv["u"], dproj, "tn", [F32], name="fox_gwin", tn=640)
    return du, dict(fox_w_in=g_win[:, :3 * FOX_HEADS * HEAD_DIM + FOX_HEADS], fox_w_o=g_wo,
                    fox_b_f=db_f[:, :FOX_HEADS])


def _conv_fwd(xpre, w8, b, name):
    s, c = xpre.shape
    tm, tc = min(ROW_TILE, s), min(1024, c)
    hb = tm // 8

    def body(x_ref, h_ref, w_ref, b_ref, xc_ref, xa_ref):
        i = pl.program_id(1)
        x = x_ref[...]
        halo = jnp.where(i > 0, h_ref[...], 0.0)
        w = w_ref[...]
        row = lax.broadcasted_iota(jnp.int32, (tm, tc), 0)
        row8 = lax.broadcasted_iota(jnp.int32, (8, tc), 0)
        acc = x * w[3:4] + b_ref[...]
        x8 = x[0:8]
        acc8 = x8 * w[3:4] + b_ref[...]
        for j in range(1, SSM_CONV):
            acc = acc + w[3 - j:4 - j] * pltpu.roll(x, j, 0)
            acc8 = acc8 + w[3 - j:4 - j] * jnp.where(row8 < j, pltpu.roll(halo, j, 0), pltpu.roll(x8, j, 0))
        xc_ref[...] = acc
        xc_ref[0:8, :] = acc8
        xc = xc_ref[...]
        xa_ref[...] = _silu(xc)

    tile = pl.BlockSpec((tm, tc), lambda jc, i: (i, jc))
    return _pcall(body, name=name, grid=(c // tc, s // tm),
                  in_specs=[tile, pl.BlockSpec((8, tc), lambda jc, i: (jnp.maximum(i * hb - 1, 0), jc)),
                            pl.BlockSpec((8, tc), lambda jc, i: (0, jc)), pl.BlockSpec((1, tc), lambda jc, i: (0, jc))],
                  out_specs=[tile, tile], out_shape=[_sds((s, c), F32), _sds((s, c), F32)],
                  sem=("parallel", "arbitrary"))(xpre, xpre, w8, b)


def _conv_bwd(dxa, xc, xpre, w8, name):
    s, c = xpre.shape
    tm, tc = min(ROW_TILE, s), min(1024, c)
    hb = tm // 8
    n = s // tm

    def body(d_ref, xc_ref, x_ref, xh_ref, dn_ref, xcn_ref, w_ref, dx_ref, dw_ref, db_ref, scr):
        i = pl.program_id(1)

        @pl.when(i == 0)
        def _():
            dw_ref[...] = jnp.zeros_like(dw_ref)
            db_ref[...] = jnp.zeros_like(db_ref)
        w = w_ref[...]
        x = x_ref[...]
        g = d_ref[...] * _dsilu(xc_ref[...])
        gn = jnp.where(i < n - 1, dn_ref[...] * _dsilu(xcn_ref[...]), 0.0)
        halo = jnp.where(i > 0, xh_ref[...], 0.0)
        row = lax.broadcasted_iota(jnp.int32, (tm, tc), 0)
        row8 = lax.broadcasted_iota(jnp.int32, (8, tc), 0)
        db_ref[...] += _colsum(g)
        dw_ref[3:4, :] += _colsum(g * x)
        g8 = g[0:8]
        acc = g * w[3:4]
        corr = jnp.zeros((8, tc), F32)
        for j in range(1, SSM_CONV):
            xs = pltpu.roll(x, j, 0)
            dwj = _colsum(jnp.where(row >= j, g * xs, 0.0))
            dwj = dwj + _colsum(jnp.where(row8 < j, g8 * pltpu.roll(halo, j, 0), 0.0))
            dw_ref[3 - j:4 - j, :] += dwj
            gs = pltpu.roll(g, tm - j, 0)
            acc = acc + w[3 - j:4 - j] * jnp.where(row < tm - j, gs, 0.0)
            corr = corr + w[3 - j:4 - j] * jnp.where(row8 >= 8 - j, pltpu.roll(gn, 8 - j, 0), 0.0)
        scr[...] = acc
        scr[tm - 8:tm, :] += corr
        dx_ref[...] = scr[...].astype(dx_ref.dtype)

    tile = pl.BlockSpec((tm, tc), lambda jc, i: (i, jc))
    prev8 = pl.BlockSpec((8, tc), lambda jc, i: (jnp.maximum(i * hb - 1, 0), jc))
    next8 = pl.BlockSpec((8, tc), lambda jc, i: (jnp.minimum((i + 1) * hb, n * hb - 1), jc))
    return _pcall(body, name=name, grid=(c // tc, n),
                  in_specs=[tile, tile, tile, prev8, next8, next8, pl.BlockSpec((8, tc), lambda jc, i: (0, jc))],
                  out_specs=[tile, pl.BlockSpec((8, tc), lambda jc, i: (0, jc)), pl.BlockSpec((1, tc), lambda jc, i: (0, jc))],
                  out_shape=[_sds((s, c), BF16), _sds((8, c), F32), _sds((1, c), F32)],
                  scratch=[pltpu.VMEM((tm, tc), F32)],
                  sem=("parallel", "arbitrary"))(dxa, xc, xpre, xpre, dxa, xc, w8)


def _ssd_pre(dtraw, dt_bias, a_log, cst, name):
    def fn(raw, bias, alog, expand, bcast):
        tm = raw.shape[0]
        z = raw + bias
        dt = jnp.maximum(z, 0.0) + jnp.log(1.0 + jnp.exp(-jnp.abs(z)))
        lane = lax.broadcasted_iota(jnp.int32, (tm, LANES), 1)
        pos = lax.broadcasted_iota(jnp.int32, (tm, LANES), 0) & (SSM_CHUNK - 1)
        dt = jnp.where(lane < SSM_HEADS, dt, 0.0)
        c = dt * (-jnp.exp(alog))
        sh = 1
        while sh < SSM_CHUNK:
            c = c + jnp.where(pos >= sh, pltpu.roll(c, sh, 0), 0.0)
            sh *= 2
        return dt, c, _dot_split(dt, expand), _dot_split(c, expand), _dot_split(c, bcast)
    wide = SSM_HEADS * HEAD_DIM
    return _rowwise(fn, [dtraw], [dt_bias, a_log, cst["expand"], cst["bcast"]],
                    [(LANES, F32), (LANES, F32), (wide, F32), (wide, F32), (SSM_HEADS * LANES, F32)], [], name=name)


def _ssd_post(dacs, ddt, dtraw, dt, dt_bias, a_log, name):
    def fn(dacs_v, ddt_v, raw, dt_v, bias, alog):
        tm = raw.shape[0]
        pos = lax.broadcasted_iota(jnp.int32, (tm, LANES), 0) & (SSM_CHUNK - 1)
        a = -jnp.exp(alog)
        c = dacs_v
        sh = 1
        while sh < SSM_CHUNK:
            c = c + jnp.where(pos + sh < SSM_CHUNK, pltpu.roll(c, tm - sh, 0), 0.0)
            sh *= 2
        draw = (ddt_v + c * a) * _sigmoid(raw + bias)
        return draw, _colsum(draw), _colsum(c * dt_v * a)
    return _rowwise(fn, [dacs, ddt, dtraw, dt], [dt_bias, a_log], [(LANES, BF16)], [(1, LANES)] * 2, name=name)


def _heads_cols(x, s):
    return x[:, :SSM_HEADS].reshape(s, SSM_GROUPS, 4).transpose(1, 0, 2)


def _heads_rows(x, s):
    return x[:, :SSM_HEADS].reshape(s // SSM_CHUNK, SSM_CHUNK, SSM_GROUPS, 4).transpose(2, 0, 3, 1)


def _expand_heads(cols, lane):
    out = cols[:, 3:4]
    for r in (2, 1, 0):
        out = jnp.where(lane < HEAD_DIM * (r + 1), cols[:, r:r + 1], out)
    return out


def _ssd_common(x, dtc, acsc, acsr):
    l = SSM_CHUNK
    lane = lax.broadcasted_iota(jnp.int32, (l, SSM_GROUP_WIDTH), 1)
    dt_e = _expand_heads(dtc, lane)
    acs_e = _expand_heads(acsc, lane)
    last = acsr[:, l - 1:l]
    lane1 = lax.broadcasted_iota(jnp.int32, (1, SSM_GROUP_WIDTH), 1)
    last_e = last[3:4, :]
    for r in (2, 1, 0):
        last_e = jnp.where(lane1 < HEAD_DIM * (r + 1), last[r:r + 1, :], last_e)
    e_e = jnp.exp(acs_e)
    dte_e = jnp.exp(last_e - acs_e)
    rowg = lax.broadcasted_iota(jnp.int32, (SSM_GROUP_WIDTH, SSM_STATE), 0)
    cd = jnp.exp(last)
    cd_mat = cd[3:4, :]
    for r in (2, 1, 0):
        cd_mat = jnp.where(rowg < HEAD_DIM * (r + 1), cd[r:r + 1, :], cd_mat)
    return lane, dt_e, e_e, dte_e, cd, cd_mat


def _ssd_fwd(xa, dtc, acsc, acsr, d_e, name):
    s = xa.shape[0]
    l, gw, ns = SSM_CHUNK, SSM_GROUP_WIDTH, SSM_STATE
    nc = s // l
    xb, bb = 2048 // gw, 2048 // ns

    def body(x_ref, b_ref, c_ref, dtc_ref, acsc_ref, acsr_ref, d_ref, y_ref, hp_ref, h_sc):
        @pl.when(pl.program_id(1) == 0)
        def _():
            h_sc[...] = jnp.zeros_like(h_sc)
        x = x_ref[...]
        bm, cm = b_ref[...].astype(BF16), c_ref[...].astype(BF16)
        acsc, acsr = acsc_ref[...], acsr_ref[...]
        lane, dt_e, e_e, dte_e, _, cd_mat = _ssd_common(x, dtc_ref[...], acsc, acsr)
        xdt = x * dt_e
        xdt16 = xdt.astype(BF16)
        cb = _dot(cm, bm, NT_DIMS)
        tril = lax.broadcasted_iota(jnp.int32, (l, l), 0) >= lax.broadcasted_iota(jnp.int32, (l, l), 1)
        yd = jnp.zeros((l, gw), F32)
        for r in range(4):
            lm = jnp.exp(jnp.where(tril, acsc[:, r:r + 1] - acsr[r:r + 1, :], -jnp.inf))
            yr = _dot((cb * lm).astype(BF16), xdt16)
            yd = jnp.where((lane >= HEAD_DIM * r) & (lane < HEAD_DIM * (r + 1)), yr, yd)
        hp = h_sc[...]
        hp_ref[...] = hp
        yoff = _dot(cm, hp.astype(BF16), NT_DIMS) * e_e
        y_ref[...] = yd + yoff + x * d_ref[...]
        st = _dot((xdt * dte_e).astype(BF16), bm, TN_DIMS)
        h_sc[...] = hp * cd_mat + st

    return _pcall(
        body, name=name, grid=(SSM_GROUPS, nc),
        in_specs=[pl.BlockSpec((l, gw), lambda g, c: (c, g)),
                  pl.BlockSpec((l, ns), lambda g, c: (c, bb + g)),
                  pl.BlockSpec((l, ns), lambda g, c: (c, bb + SSM_GROUPS + g)),
                  pl.BlockSpec((None, l, 4), lambda g, c: (g, c, 0)),
                  pl.BlockSpec((None, l, 4), lambda g, c: (g, c, 0)),
                  pl.BlockSpec((None, None, 4, l), lambda g, c: (g, c, 0, 0)),
                  pl.BlockSpec((None, 1, gw), lambda g, c: (g, 0, 0))],
        out_specs=[pl.BlockSpec((l, gw), lambda g, c: (c, g)),
                   pl.BlockSpec((None, None, gw, ns), lambda g, c: (g, c, 0, 0))],
        out_shape=[_sds((s, xb * gw), F32), _sds((SSM_GROUPS, nc, gw, ns), F32)],
        scratch=[pltpu.VMEM((gw, ns), F32)],
        sem=("parallel", "arbitrary"))(xa, xa, xa, dtc, acsc, acsr, d_e)


def _ssd_bwd(dy, xa, dtc, acsc, acsr, d_e, hprev, name):
    s = xa.shape[0]
    l, gw, ns = SSM_CHUNK, SSM_GROUP_WIDTH, SSM_STATE
    nc = s // l
    bb = 2048 // ns

    def body(dy_ref, x_ref, b_ref, c_ref, dtc_ref, acsc_ref, acsr_ref, d_ref, hp_ref,
             dx_ref, db_ref, dc_ref, dacs_ref, ddt_ref, dd_ref, dh_sc):
        @pl.when(pl.program_id(1) == 0)
        def _():
            dh_sc[...] = jnp.zeros_like(dh_sc)
            dd_ref[...] = jnp.zeros_like(dd_ref)
        dyv, x = dy_ref[...], x_ref[...]
        bm, cm = b_ref[...].astype(BF16), c_ref[...].astype(BF16)
        acsc, acsr = acsc_ref[...], acsr_ref[...]
        lane, dt_e, e_e, dte_e, cd, cd_mat = _ssd_common(x, dtc_ref[...], acsc, acsr)
        xdt = x * dt_e
        xdt16 = xdt.astype(BF16)
        dy16 = dyv.astype(BF16)
        cb = _dot(cm, bm, NT_DIMS)
        cbt = _dot(bm, cm, NT_DIMS)
        hp = hp_ref[...]
        hp16 = hp.astype(BF16)
        g = dh_sc[...]
        g16 = g.astype(BF16)
        t_all = _dot(cm, hp16, NT_DIMS)
        dt16 = (dyv * e_e).astype(BF16)
        dc = _dot(dt16, hp16)
        dhp = _dot(dt16, cm, TN_DIMS)
        yoff_term = dyv * t_all * e_e
        wv = xdt * dte_e
        dw = _dot(bm, g16, NT_DIMS)
        db = _dot(wv.astype(BF16), g16)
        dxdt = dw * dte_e
        dte_term = dw * wv
        gh = g * hp
        dh_sc[...] = g * cd_mat + dhp
        ri = lax.broadcasted_iota(jnp.int32, (l, l), 0)
        ci = lax.broadcasted_iota(jnp.int32, (l, l), 1)
        tril, triu = ri >= ci, ci >= ri
        dcb = jnp.zeros((l, l), F32)
        dcbt = jnp.zeros((l, l), F32)
        q_rows, q_cols = [], []
        for r in range(4):
            in_head = (lane >= HEAD_DIM * r) & (lane < HEAD_DIM * (r + 1))
            lm = jnp.exp(jnp.where(tril, acsc[:, r:r + 1] - acsr[r:r + 1, :], -jnp.inf))
            lmt = jnp.exp(jnp.where(triu, acsr[r:r + 1, :] - acsc[:, r:r + 1], -jnp.inf))
            mm_, mt = cb * lm, cbt * lmt
            dyr = jnp.where(in_head, dy16, jnp.zeros_like(dy16))
            dm = _dot(dyr, xdt16, NT_DIMS)
            dmt = _dot(xdt16, dyr, NT_DIMS)
            dxdt = dxdt + jnp.where(in_head, _dot(mt.astype(BF16), dy16), 0.0)
            dcb = dcb + dm * lm
            dcbt = dcbt + dmt * lmt
            q_rows.append(jnp.sum(dm * mm_, axis=1, keepdims=True))
            q_cols.append(jnp.sum(dmt * mt, axis=1, keepdims=True))
        dc = dc + _dot(dcb.astype(BF16), bm)
        db = db + _dot(dcbt.astype(BF16), cm)
        dxdt_x = dxdt * x
        dy_x = dyv * x
        rowc = lax.broadcasted_iota(jnp.int32, (l, 1), 0)
        lane128 = lax.broadcasted_iota(jnp.int32, (1, LANES), 1)
        dd_row = jnp.zeros((1, LANES), F32)
        for r in range(4):
            in_head = (lane >= HEAD_DIM * r) & (lane < HEAD_DIM * (r + 1))
            seg = lambda v: jnp.sum(jnp.where(in_head, v, 0.0), axis=1, keepdims=True)
            s1, s2, s3 = seg(yoff_term), seg(dte_term), seg(dxdt_x)
            dcd = jnp.sum(_colsum(gh[HEAD_DIM * r:HEAD_DIM * (r + 1), :]), axis=1, keepdims=True)
            last_add = _colsum(s2) + dcd * cd[r:r + 1, :]
            dacs_r = q_rows[r] - q_cols[r] + s1 - s2 + jnp.where(rowc == l - 1, last_add, 0.0)
            dacs_ref[:, r:r + 1] = dacs_r
            ddt_ref[:, r:r + 1] = s3
            dd_row = dd_row + jnp.where(lane128 == r, _colsum(seg(dy_x)), 0.0)
        dd_ref[0:1, :] += dd_row
        dx_ref[...] = dxdt * dt_e + dyv * d_ref[...]
        db_ref[...] = db
        dc_ref[...] = dc

    rc = lambda c: nc - 1 - c
    return _pcall(
        body, name=name, grid=(SSM_GROUPS, nc),
        in_specs=[pl.BlockSpec((l, gw), lambda g, c: (rc(c), g)),
                  pl.BlockSpec((l, gw), lambda g, c: (rc(c), g)),
                  pl.BlockSpec((l, ns), lambda g, c: (rc(c), bb + g)),
                  pl.BlockSpec((l, ns), lambda g, c: (rc(c), bb + SSM_GROUPS + g)),
                  pl.BlockSpec((None, l, 4), lambda g, c: (g, rc(c), 0)),
                  pl.BlockSpec((None, l, 4), lambda g, c: (g, rc(c), 0)),
                  pl.BlockSpec((None, None, 4, l), lambda g, c: (g, rc(c), 0, 0)),
                  pl.BlockSpec((None, 1, gw), lambda g, c: (g, 0, 0)),
                  pl.BlockSpec((None, None, gw, ns), lambda g, c: (g, rc(c), 0, 0))],
        out_specs=[pl.BlockSpec((l, gw), lambda g, c: (rc(c), g)),
                   pl.BlockSpec((l, ns), lambda g, c: (rc(c), g)),
                   pl.BlockSpec((l, ns), lambda g, c: (rc(c), g)),
                   pl.BlockSpec((None, l, 4), lambda g, c: (g, rc(c), 0)),
                   pl.BlockSpec((None, l, 4), lambda g, c: (g, rc(c), 0)),
                   pl.BlockSpec((None, 8, LANES), lambda g, c: (g, 0, 0))],
        out_shape=[_sds((s, 2048), F32), _sds((s, SSM_GROUPS * ns), F32), _sds((s, SSM_GROUPS * ns), F32),
                   _sds((SSM_GROUPS, s, 4), F32), _sds((SSM_GROUPS, s, 4), F32), _sds((SSM_GROUPS, 8, LANES), F32)],
        scratch=[pltpu.VMEM((gw, ns), F32)],
        sem=("parallel", "arbitrary"))(dy, xa, xa, xa, dtc, acsc, acsr, d_e, hprev)


def _ssd_constants():
    src = jnp.arange(LANES)[:, None]
    expand = (src == jnp.arange(SSM_HEADS * HEAD_DIM)[None, :] // HEAD_DIM).astype(BF16)
    bcast = (src == jnp.arange(SSM_HEADS * LANES)[None, :] // LANES).astype(BF16)
    seg =(jnp.arange(SSM_GROUP_WIDTH)[:, None] // HEAD_DIM == jnp.arange(LANES)[None, :]).astype(BF16)
    seg4 = (jnp.arange(4 * LANES)[:, None] // LANES == jnp.arange(LANES)[None, :]).astype(BF16)
    return dict(expand=expand, bcast=bcast, seg=seg, seg4=seg4)


def _ssd_setup(acs_e, acsr):
    l = SSM_CHUNK
    last = acsr[:, l - 1:l]
    lane1 = lax.broadcasted_iota(jnp.int32, (1, SSM_GROUP_WIDTH), 1)
    last_e = last[3:4, :]
    for r in (2, 1, 0):
        last_e = jnp.where(lane1 < HEAD_DIM * (r + 1), last[r:r + 1, :], last_e)
    return jnp.exp(acs_e), jnp.exp(last_e - acs_e), jnp.exp(last_e)


def _ssd_fwd2(xa, bt, dte, acse, acsbc, acsr, d_e, name):
    s = xa.shape[0]
    l, gw, ns = SSM_CHUNK, SSM_GROUP_WIDTH, SSM_STATE
    nc = s // l
    bb = 2048 // ns

    def body(x_ref, b_ref, c_ref, bt_ref, dt_ref, acs_ref, abc_ref, acsr_ref, d_ref, y_ref, hp_ref, h_sc):
        @pl.when(pl.program_id(1) == 0)
        def _():
            h_sc[...] = jnp.zeros_like(h_sc)
        x = x_ref[...]
        bm, cm, btm = b_ref[...].astype(BF16), c_ref[...].astype(BF16), bt_ref[...].astype(BF16)
        acsr = acsr_ref[...]
        dt_e, acs_bc = dt_ref[...], abc_ref[...]
        e_e, dte_e, cd_e = _ssd_setup(acs_ref[...], acsr)
        lane = lax.broadcasted_iota(jnp.int32, (l, gw), 1)
        xdt = x * dt_e
        xdt16 = xdt.astype(BF16)
        cb = _dot(cm, bm, NT_DIMS)
        tril = _causal(l)
        yd = jnp.zeros((l, gw), F32)
        for r in range(4):
            lm = jnp.exp(jnp.where(tril, acs_bc[:, r * LANES:(r + 1) * LANES] - acsr[r:r + 1, :], -jnp.inf))
            yr = _dot((cb * lm).astype(BF16), xdt16)
            yd = jnp.where((lane >= HEAD_DIM * r) & (lane < HEAD_DIM * (r + 1)), yr, yd)
        hp = h_sc[...]
        hp_ref[...] = hp
        y_ref[...] = yd + _dot(cm, hp.astype(BF16)) * e_e + x * d_ref[...]
        h_sc[...] = hp * cd_e + _dot(btm, (xdt * dte_e).astype(BF16))

    return _pcall(
        body, name=name, grid=(SSM_GROUPS, nc),
        in_specs=[pl.BlockSpec((l, gw), lambda g, c: (c, g)),
                  pl.BlockSpec((l, ns), lambda g, c: (c, bb + g)),
                  pl.BlockSpec((l, ns), lambda g, c: (c, bb + SSM_GROUPS + g)),
                  pl.BlockSpec((ns, l), lambda g, c: (g, c)),
                  pl.BlockSpec((l, gw), lambda g, c: (c, g)),
                  pl.BlockSpec((l, gw), lambda g, c: (c, g)),
                  pl.BlockSpec((l, 4 * LANES), lambda g, c: (c, g)),
                  pl.BlockSpec((None, None, 4, l), lambda g, c: (g, c, 0, 0)),
                  pl.BlockSpec((None, 1, gw), lambda g, c: (g, 0, 0))],
        out_specs=[pl.BlockSpec((l, gw), lambda g, c: (c, g)),
                   pl.BlockSpec((None, None, ns, gw), lambda g, c: (g, c, 0, 0))],
        out_shape=[_sds((s, 2048), F32), _sds((SSM_GROUPS, nc, ns, gw), F32)],
        scratch=[pltpu.VMEM((ns, gw), F32)],
        sem=("parallel", "arbitrary"))(xa, xa, xa, bt, dte, acse, acsbc, acsr, d_e)


def _ssd_bwd2(dy, xa, ct, dte, acse, acsbc, acsr, d_e, hprev, cst, name):
    s = xa.shape[0]
    l, gw, ns = SSM_CHUNK, SSM_GROUP_WIDTH, SSM_STATE
    nc = s // l
    bb = 2048 // ns

    def body(dy_ref, x_ref, b_ref, c_ref, ct_ref, dt_ref, acs_ref, abc_ref, acsr_ref, d_ref, hp_ref,
             seg_ref, seg4_ref, dx_ref, db_ref, dc_ref, dacs_ref, ddt_ref, dd_ref, dh_sc):
        @pl.when(pl.program_id(1) == 0)
        def _():
            dh_sc[...] = jnp.zeros_like(dh_sc)
            dd_ref[...] = jnp.zeros_like(dd_ref)
        dyv, x = dy_ref[...], x_ref[...]
        bm, cm, ctm = b_ref[...].astype(BF16), c_ref[...].astype(BF16), ct_ref[...].astype(BF16)
        acsr = acsr_ref[...]
        dt_e, acs_bc = dt_ref[...], abc_ref[...]
        e_e, dte_e, cd_e = _ssd_setup(acs_ref[...], acsr)
        seg, seg4 = seg_ref[...], seg4_ref[...]
        lane = lax.broadcasted_iota(jnp.int32, (l, gw), 1)
        xdt = x * dt_e
        xdt16 = xdt.astype(BF16)
        dy16 = dyv.astype(BF16)
        cb = _dot(cm, bm, NT_DIMS)
        cbt = _dot(bm, cm, NT_DIMS)
        hp = hp_ref[...]
        hp16 = hp.astype(BF16)
        g = dh_sc[...]
        g16 = g.astype(BF16)
        t_all = _dot(cm, hp16)
        dt16 = (dyv * e_e).astype(BF16)
        dc = _dot(dt16, hp16, NT_DIMS)
        dhp = _dot(ctm, dt16)
        wv = xdt * dte_e
        dw = _dot(bm, g16)
        db = _dot(wv.astype(BF16), g16, NT_DIMS)
        dxdt = dw * dte_e
        acs_term = dyv * t_all * e_e - dw * wv
        last_term = _colsum(dw * wv) + _colsum(g * hp) * cd_e
        dh_sc[...] = g * cd_e + dhp
        tril, triu = _causal(l), _causal(l, transposed=True)
        dcb = jnp.zeros((l, l), F32)
        dcbt = jnp.zeros((l, l), F32)
        qd = []
        for r in range(4):
            in_head = (lane >= HEAD_DIM * r) & (lane < HEAD_DIM * (r + 1))
            a_col = acs_bc[:, r * LANES:(r + 1) * LANES]
            lm = jnp.exp(jnp.where(tril, a_col - acsr[r:r + 1, :], -jnp.inf))
            lmt = jnp.exp(jnp.where(triu, acsr[r:r + 1, :] - a_col, -jnp.inf))
            mm_, mt = cb * lm, cbt * lmt
            dyr = jnp.where(in_head, dy16, jnp.zeros_like(dy16))
            dm = _dot(dyr, xdt16, NT_DIMS)
            dmt = _dot(xdt16, dyr, NT_DIMS)
            dxdt = dxdt + jnp.where(in_head, _dot(mt.astype(BF16), dy16), 0.0)
            dcb = dcb + dm * lm
            dcbt = dcbt + dmt * lmt
            qd.append(dm * mm_ - dmt * mt)
        dc = dc + _dot(dcb.astype(BF16), bm)
        db = db + _dot(dcbt.astype(BF16), cm)
        rowl = lax.broadcasted_iota(jnp.int32, (l, LANES), 0)
        row8 = lax.broadcasted_iota(jnp.int32, (8, gw), 0)
        small = _dot_split(jnp.where(row8 == 0, last_term, jnp.where(row8 == 1, _colsum(dyv * x), 0.0)), seg, passes=2)
        big = _dot_split(jnp.concatenate([acs_term, dxdt * x], axis=0), seg, passes=2)
        dacs = (big[0:l] + _dot_split(jnp.concatenate(qd, axis=1), seg4, passes=2)
                + jnp.where(rowl == l - 1, small[0:1, :], 0.0))
        dacs_ref[...] = dacs[:, 0:4]
        ddt_ref[...] = big[l:2 * l, 0:4]
        dd_ref[0:1, :] += small[1:2, :]
        dx_ref[...] = dxdt * dt_e + dyv * d_ref[...]
        db_ref[...] = db
        dc_ref[...] = dc

    rc = lambda c: nc - 1 - c
    return _pcall(
        body, name=name, grid=(SSM_GROUPS, nc),
        in_specs=[pl.BlockSpec((l, gw), lambda g, c: (rc(c), g)),
                  pl.BlockSpec((l, gw), lambda g, c: (rc(c), g)),
                  pl.BlockSpec((l, ns), lambda g, c: (rc(c), bb + g)),
                  pl.BlockSpec((l, ns), lambda g, c: (rc(c), bb + SSM_GROUPS + g)),
                  pl.BlockSpec((ns, l), lambda g, c: (g, rc(c))),
                  pl.BlockSpec((l, gw), lambda g, c: (rc(c), g)),
                  pl.BlockSpec((l, gw), lambda g, c: (rc(c), g)),
                  pl.BlockSpec((l, 4 * LANES), lambda g, c: (rc(c), g)),
                  pl.BlockSpec((None, None, 4, l), lambda g, c: (g, rc(c), 0, 0)),
                  pl.BlockSpec((None, 1, gw), lambda g, c: (g, 0, 0)),
                  pl.BlockSpec((None, None, ns, gw), lambda g, c: (g, rc(c), 0, 0)),
                  pl.BlockSpec((gw, LANES), lambda g, c: (0, 0)),
                  pl.BlockSpec((4 * LANES, LANES), lambda g, c: (0, 0))],
        out_specs=[pl.BlockSpec((l, gw), lambda g, c: (rc(c), g)),
                   pl.BlockSpec((l, ns), lambda g, c: (rc(c), g)),
                   pl.BlockSpec((l, ns), lambda g, c: (rc(c), g)),
                   pl.BlockSpec((None, l, 4), lambda g, c: (g, rc(c), 0)),
                   pl.BlockSpec((None, l, 4), lambda g, c: (g, rc(c), 0)),
                   pl.BlockSpec((None, 8, LANES), lambda g, c: (g, 0, 0))],
        out_shape=[_sds((s, 2048), F32), _sds((s, SSM_GROUPS * ns), F32), _sds((s, SSM_GROUPS * ns), F32),
                   _sds((SSM_GROUPS, s, 4), F32), _sds((SSM_GROUPS, s, 4), F32), _sds((SSM_GROUPS, 8, LANES), F32)],
        scratch=[pltpu.VMEM((ns, gw), F32)],
        sem=("parallel", "arbitrary"))(dy, xa, xa, xa, ct, dte, acse, acsbc, acsr, d_e, hprev,
                                       cst["seg"], cst["seg4"])


def _gate_norm(y, z, nw, name):
    c = y.shape[1]

    def fn(yv, zv, w):
        outs = []
        for k in range(c // SSM_GROUP_WIDTH):
            sl = slice(k * SSM_GROUP_WIDTH, (k + 1) * SSM_GROUP_WIDTH)
            yg = yv[:, sl] * _silu(zv[:, sl])
            rinv = lax.rsqrt(jnp.mean(yg * yg, axis=-1, keepdims=True) + RMS_EPS)
            outs.append(yg * rinv * w[:, sl])
        return (jnp.concatenate(outs, axis=1),)
    return _rowwise(fn, [y, z], [nw], [(c, BF16)], [], name=name)[0]


def _gate_norm_bwd(dyn, y, z, nw, name):
    c = y.shape[1]

    def fn(dv, yv, zv, w):
        dys, dzs, dws = [], [], []
        for k in range(c // SSM_GROUP_WIDTH):
            sl = slice(k * SSM_GROUP_WIDTH, (k + 1) * SSM_GROUP_WIDTH)
            ys, zs, ds = yv[:, sl], zv[:, sl], dv[:, sl]
            sz = _silu(zs)
            yg = ys * sz
            rinv = lax.rsqrt(jnp.mean(yg * yg, axis=-1, keepdims=True) + RMS_EPS)
            nrm = yg * rinv
            dn = ds * w[:, sl]
            dyg = rinv * (dn - nrm * jnp.mean(dn * nrm, axis=-1, keepdims=True))
            dys.append(dyg * sz)
            dzs.append(dyg * ys * _dsilu(zs))
            dws.append(_colsum(ds * nrm))
        return jnp.concatenate(dys, axis=1), jnp.concatenate(dzs, axis=1), jnp.concatenate(dws, axis=1)
    return _rowwise(fn, [dyn, y, z], [nw], [(c, F32), (c, BF16)], [(1, c)], name=name, tm=128)


def _ssd_forward(u, w):
    s = u.shape[0]
    z = _mm(u, w["ssm_z"], "nn", [F32], name="ssm_zproj")
    xpre = _mm(u, w["ssm_xbc"], "nn", [F32], name="ssm_xproj")
    dtraw = _mm(u, w["ssm_dt"], "nn", [F32], name="ssm_dtproj")
    xc, xa = _conv_fwd(xpre, w["conv_w8"], w["conv_b"], "ssm_conv")
    dt, acs, dte, acse, acsbc = _ssd_pre(dtraw, w["dt_bias"], w["a_log"], w["ssd_cst"], "ssm_pre")
    acsr = _heads_rows(acs, s)
    y, hprev = _ssd_fwd2(xa, xa[:, 2048:3072].T, dte, acse, acsbc, acsr, w["d_e"], "ssm_scan")
    yn = _gate_norm(y, z, w["norm_w"], "ssm_gate_norm")
    out = _mm(yn, w["ssm_out"], "nn", [F32], name="ssm_oproj")
    return out, dict(u=u, z=z, xpre=xpre, xc=xc, xa=xa, dtraw=dtraw, dt=dt, dte=dte, acse=acse, acsbc=acsbc,
                     acsr=acsr, y=y, hprev=hprev, yn=yn)


def _ssd_backward(dy, sv, w):
    s = dy.shape[0]
    dyn = _mm(dy, w["ssm_out"], "nt", [F32], name="ssm_dyn")
    g_wout = _mm(sv["yn"], dy, "tn", [F32], name="ssm_gwout")
    dys, dz, dnw = _gate_norm_bwd(dyn, sv["y"], sv["z"], w["norm_w"], "ssm_gate_norm_bwd")
    ct = sv["xa"][:, 3072:].T
    dx, dbm, dcm, dacs_c, ddt_c, dd = _ssd_bwd2(dys, sv["xa"], ct, sv["dte"], sv["acse"], sv["acsbc"], sv["acsr"],
                                                w["d_e"], sv["hprev"], w["ssd_cst"], "ssm_scan_bwd")
    pad = ((0, 0), (0, LANES - SSM_HEADS))
    dacs = jnp.pad(dacs_c.transpose(1, 0, 2).reshape(s, SSM_HEADS), pad)
    ddt = jnp.pad(ddt_c.transpose(1, 0, 2).reshape(s, SSM_HEADS), pad)
    draw, dbias, dalog = _ssd_post(dacs, ddt, sv["dtraw"], sv["dt"], w["dt_bias"], w["a_log"], "ssm_post")
    dxa = jnp.concatenate([dx, dbm, dcm], axis=1)
    dxpre, dcw, dcb = _conv_bwd(dxa, sv["xc"], sv["xpre"], w["conv_w8"], "ssm_conv_bwd")
    dproj = jnp.concatenate([dz, dxpre, draw], axis=1)
    du = _mm(dproj, w["ssm_in_pad"], "nt", [F32], name="ssm_du", tk=896)
    g_win = _mm(sv["u"], dproj, "tn", [F32], name="ssm_gwin", tn=896)
    n_in = 2048 + 4096 + SSM_HEADS
    return du, dict(ssm_w_in=g_win[:, :n_in], ssm_w_out=g_wout, ssm_conv_w=dcw[:SSM_CONV], ssm_conv_b=dcb,
                    ssm_norm_w=dnw, ssm_dt_bias=dbias[:, :SSM_HEADS], ssm_a_log=dalog[:, :SSM_HEADS],
                    ssm_d=dd[:, 0, :4].reshape(1, SSM_HEADS))


def _mlp_forward(u2, w1, w2, tag):
    def epi(acc):
        hr = jnp.maximum(acc, 0.0)
        return hr, hr * hr
    hr, a = _mm(u2, w1, "nn", [BF16, BF16], name=tag + "_mlp_up", epi=epi)
    y2 = _mm(a, w2, "nn", [F32], name=tag + "_mlp_down")
    return y2, hr, a


def _mlp_backward(dy2, u2, hr, a, w1, w2, tag):
    dh = _mm(dy2, w2, "nt", [BF16], name=tag + "_mlp_dh", extra=(hr,),
             epi=lambda acc, h: (acc * (2.0 * h.astype(F32)),))
    g_w2 = _mm(a, dy2, "tn", [F32], name=tag + "_mlp_gw2")
    g_w1 = _mm(u2, dh, "tn", [F32], name=tag + "_mlp_gw1")
    du2 = _mm(dh, w1, "nt", [F32], name=tag + "_mlp_du")
    return du2, g_w1, g_w2


def _ada_forward(c16, ada_w, ada_b_cols, name):
    nl, d, cols = ada_w.shape
    tn = 512

    def body(c_ref, w_ref, b_ref, o_ref):
        cond = _silu(c_ref[...]).astype(BF16)
        o_ref[...] = _dot(cond, w_ref[...].astype(BF16)) + b_ref[...]

    return _pcall(body, name=name, grid=(nl, cols // tn),
                  in_specs=[pl.BlockSpec((16, d), lambda i, j: (0, 0)),
                            pl.BlockSpec((None, d, tn), lambda i, j: (i, 0, j)),
                            pl.BlockSpec((None, 1, tn), lambda i, j: (i, 0, j))],
                  out_specs=pl.BlockSpec((None, 16, tn), lambda i, j: (i, 0, j)),
                  out_shape=_sds((nl, 16, cols), F32), sem=("parallel", "parallel"))(c16, ada_w, ada_b_cols)


def _ada_backward(c_t, dmod_cols, name):
    d, nb = c_t.shape
    nl, _, cols = dmod_cols.shape
    tn = 512

    def body(c_ref, dm_ref, o_ref):
        cond = _silu(c_ref[...])
        dm = dm_ref[...]
        acc = cond[:, 0:1] * dm[0:1, :]
        for b in range(1, nb):
            acc = acc + cond[:, b:b + 1] * dm[b:b + 1, :]
        o_ref[...] = acc

    return _pcall(body, name=name, grid=(nl, cols // tn),
                  in_specs=[pl.BlockSpec((d, nb), lambda i, j: (0, 0)),
                            pl.BlockSpec((None, nb, tn), lambda i, j: (i, 0, j))],
                  out_specs=pl.BlockSpec((None, d, tn), lambda i, j: (i, 0, j)),
                  out_shape=_sds((nl, d, cols), F32), sem=("parallel", "parallel"))(c_t, dmod_cols)


def _adamw(w, g, m, v, name):
    rows, cols = w.shape
    tm = rows
    for cand in (256, 128, 64, 32, 16, 8):
        if rows % cand == 0 and rows > cand:
            tm = cand
            break
    c1 = 1.0 / (1.0 - ADAM_B1 ** ADAM_STEP)
    c2 = 1.0 / (1.0 - ADAM_B2 ** ADAM_STEP)

    def fn(wv, gv, mv, vv):
        mn = ADAM_B1 * mv + (1.0 - ADAM_B1) * gv
        vn = ADAM_B2 * vv + (1.0 - ADAM_B2) * (gv * gv)
        delta = -ADAM_LR * ((mn * c1) / (jnp.sqrt(vn * c2) + ADAM_EPS) + ADAM_WD * wv)
        return delta, mn, vn
    return _rowwise(fn, [w, g, m, v], [], [(cols, F32)] * 3, [], name=name, tm=tm)


def _my_pos():
    return lax.axis_index("x"), lax.axis_index("y"), lax.axis_index("c")


def _allgather8(x, name):
    r, c = x.shape

    def body(x_ref, out_ref, send_sems, recv_sems, local_sem):
        mx, my, mc = _my_pos()
        me = 4 * mx + 2 * my + mc
        mine = pltpu.make_async_copy(x_ref, out_ref.at[me], local_sem)
        mine.start()
        copies = []
        for k in range(1, 8):
            fx, fy, fc = (k >> 2) & 1, (k >> 1) & 1, k & 1
            px = 1 - mx if fx else mx
            py = 1 - my if fy else my
            pc = 1 - mc if fc else mc
            peer = 4 * px + 2 * py + pc
            send = pltpu.make_async_remote_copy(src_ref=x_ref, dst_ref=out_ref.at[me], send_sem=send_sems.at[k - 1],
                                                recv_sem=recv_sems.at[k - 1], device_id=(px, py, pc),
                                                device_id_type=MESH)
            send.start()
            recv = pltpu.make_async_remote_copy(src_ref=x_ref, dst_ref=out_ref.at[peer], send_sem=send_sems.at[k - 1],
                                                recv_sem=recv_sems.at[k - 1], device_id=(px, py, pc),
                                                device_id_type=MESH)
            copies.append((send, recv))
        for send, recv in copies:
            recv.wait_recv()
        for send, recv in copies:
            send.wait_send()
        mine.wait()

    vm = pl.BlockSpec(memory_space=pltpu.VMEM)
    return _pcall(body, name=name, in_specs=[vm], out_specs=vm, out_shape=_sds((8, r, c), x.dtype),
                  scratch=[pltpu.SemaphoreType.DMA((7,)), pltpu.SemaphoreType.DMA((7,)), pltpu.SemaphoreType.DMA])(x)


def _chip_flips(mx, my):
    out = []
    for fx, fy in ((1, 0), (0, 1), (1, 1)):
        px = 1 - mx if fx else mx
        py = 1 - my if fy else my
        out.append((px, py, 2 * px + py))
    return out


def _gather_chips(shard2, name):
    _, h, c = shard2.shape

    def body(x_ref, out_ref, send_sems, recv_sems):
        mx, my, mc = _my_pos()
        oc = 1 - mc
        mk = 2 * mx + my
        flips = _chip_flips(mx, my)

        def copy(k, src, dst, to):
            return pltpu.make_async_remote_copy(src_ref=src, dst_ref=dst, send_sem=send_sems.at[k],
                                                recv_sem=recv_sems.at[k], device_id=to, device_id_type=MESH)

        first = [copy(j, x_ref.at[mc], out_ref.at[mk, mc], (px, py, mc)) for j, (px, py, pk) in enumerate(flips)]
        for cp in first:
            cp.start()
        passed = []
        for j, (px, py, pk) in enumerate(flips):
            copy(j, x_ref.at[mc], out_ref.at[pk, mc], (px, py, mc)).wait_recv()
            fw = copy(3 + j, out_ref.at[pk, mc], out_ref.at[pk, mc], (mx, my, oc))
            fw.start()
            passed.append(fw)
        for j, (px, py, pk) in enumerate(flips):
            copy(3 + j, out_ref.at[pk, oc], out_ref.at[pk, oc], (mx, my, oc)).wait_recv()
        for cp in first + passed:
            cp.wait_send()

    return _pcall(body, name=name, in_specs=[HBM_SPEC], out_specs=HBM_SPEC, out_shape=_sds((4, 2, h, c), shard2.dtype),
                  scratch=[pltpu.SemaphoreType.DMA((6,)), pltpu.SemaphoreType.DMA((6,))])(shard2)


def _pair_exchange(g4, name):
    n, _, h, c = g4.shape

    def body(g_ref, out_ref, send_sem, recv_sem):
        mx, my, mc = _my_pos()
        oc = 1 - mc
        copies = []
        for k in range(n):
            cp = pltpu.make_async_remote_copy(src_ref=g_ref.at[k, oc], dst_ref=out_ref.at[k], send_sem=send_sem.at[k],
                                              recv_sem=recv_sem.at[k], device_id=(mx, my, oc), device_id_type=MESH)
            cp.start()
            copies.append(cp)
        for cp in copies:
            cp.wait_recv()
        for cp in copies:
            cp.wait_send()

    return _pcall(body, name=name, in_specs=[HBM_SPEC], out_specs=HBM_SPEC, out_shape=_sds((n, h, c), g4.dtype),
                  scratch=[pltpu.SemaphoreType.DMA((n,)), pltpu.SemaphoreType.DMA((n,))])(g4)


def _pair_add(g4, recv, core, name):
    n, _, h, c = g4.shape
    tm = _row_tile(h)

    def body(core_ref, a_ref, b_ref, o_ref):
        o_ref[...] = a_ref[...] + b_ref[...]

    grid_spec = pltpu.PrefetchScalarGridSpec(
        num_scalar_prefetch=1, grid=(n, h // tm),
        in_specs=[pl.BlockSpec((None, None, tm, c), lambda k, i, cr: (k, cr[0], i, 0)),
                  pl.BlockSpec((None, tm, c), lambda k, i, cr: (k, i, 0))],
        out_specs=pl.BlockSpec((None, tm, c), lambda k, i, cr: (k, i, 0)))
    return pl.pallas_call(body, out_shape=_sds((n, h, c), F32), grid_spec=grid_spec, name=name,
                          compiler_params=pltpu.CompilerParams(vmem_limit_bytes=VMEM_LIMIT_BYTES,
                                                               dimension_semantics=("parallel", "parallel")))(core, g4, recv)


def _chip_exchange(p, name):
    n, h, c = p.shape

    def body(p_ref, out_ref, send_sems, recv_sems):
        mx, my, mc = _my_pos()
        copies = []
        for j, (px, py, pk) in enumerate(_chip_flips(mx, my)):
            cp = pltpu.make_async_remote_copy(src_ref=p_ref.at[pk], dst_ref=out_ref.at[j], send_sem=send_sems.at[j],
                                              recv_sem=recv_sems.at[j], device_id=(px, py, mc), device_id_type=MESH)
            cp.start()
            copies.append(cp)
        for cp in copies:
            cp.wait_recv()
        for cp in copies:
            cp.wait_send()

    return _pcall(body, name=name, in_specs=[HBM_SPEC], out_specs=HBM_SPEC, out_shape=_sds((3, h, c), p.dtype),
                  scratch=[pltpu.SemaphoreType.DMA((3,)), pltpu.SemaphoreType.DMA((3,))])(p)


def _chip_sum(p, slots, chip, name):
    _, h, c = p.shape
    tm = _row_tile(h)

    def body(chip_ref, p_ref, q_ref, o_ref):
        o_ref[...] = ((p_ref[...] + q_ref[0]) + q_ref[1]) + q_ref[2]

    return _pcall(body, name=name, grid=(h // tm,), prefetch=1,
                  in_specs=[pl.BlockSpec((None, tm, c), lambda i, ch: (ch[0], i, 0)),
                            pl.BlockSpec((3, tm, c), lambda i, ch: (0, i, 0))],
                  out_specs=pl.BlockSpec((tm, c), lambda i, ch: (i, 0)),
                  out_shape=_sds((h, c), F32), sem=("parallel",))(chip, p, slots)


def _sum_slots(q, name):
    n, h, c = q.shape
    tm = _row_tile(h)

    def body(q_ref, o_ref):
        acc = q_ref[0]
        for k in range(1, n):
            acc = acc + q_ref[k]
        o_ref[...] = acc

    return _pcall(body, name=name, grid=(h // tm,),
                  in_specs=[pl.BlockSpec((n, tm, c), lambda i: (0, i, 0))],
                  out_specs=pl.BlockSpec((tm, c), lambda i: (i, 0)),
                  out_shape=_sds((h, c), F32), sem=("parallel",))(q)


def _pair_share(f, name):
    h, c = f.shape

    def body(f_ref, out_ref, send_sem, recv_sem):
        mx, my, mc = _my_pos()
        cp = pltpu.make_async_remote_copy(src_ref=f_ref, dst_ref=out_ref, send_sem=send_sem, recv_sem=recv_sem,
                                          device_id=(mx, my, 1 - mc), device_id_type=MESH)
        cp.start()
        cp.wait_recv()
        cp.wait_send()

    return _pcall(body, name=name, in_specs=[HBM_SPEC], out_specs=HBM_SPEC, out_shape=_sds((h, c), f.dtype),
                  scratch=[pltpu.SemaphoreType.DMA, pltpu.SemaphoreType.DMA])(f)


BIG = ("mlp_w1", "mlp_w2", "fox_w_in", "fox_w_o", "ssm_w_in", "ssm_w_out")
SMALL_SHARDED = ("ssm_conv_w", "ssm_conv_b", "ssm_norm_w")
PACK_COLS = 1024


def _pack_rows(parts, rows_multiple, dtype):
    flat = jnp.concatenate([p.reshape(-1).astype(dtype) for p in parts])
    unit = rows_multiple * PACK_COLS
    total = -(-flat.shape[0] // unit) * unit
    flat = jnp.pad(flat, (0, total - flat.shape[0]))
    return flat.reshape(total // PACK_COLS, PACK_COLS)


def _unpack(flat, shapes):
    out, off = [], 0
    for sh in shapes:
        n = 1
        for d_ in sh:
            n *= d_
        out.append(flat[off:off + n].reshape(sh))
        off += n
    return out


PIECE_ROWS = 16


def _piece_rows(shape):
    n = 1
    for d_ in shape:
        n *= d_
    rows = -(-n // PACK_COLS)
    return n, -(-rows // PIECE_ROWS) * PIECE_ROWS


def _pack2d(parts, rows_multiple, dtype):
    blocks = []
    for p in parts:
        n, rows = _piece_rows(p.shape)
        a = p.astype(dtype)
        if p.shape[-1] != PACK_COLS or n % PACK_COLS:
            a = jnp.pad(a.reshape(-1), (0, -n % PACK_COLS))
        a = a.reshape(-1, PACK_COLS)
        blocks.append(jnp.pad(a, ((0, rows - a.shape[0]), (0, 0))))
    total = sum(b.shape[0] for b in blocks)
    pad = -total % rows_multiple
    if pad:
        blocks.append(jnp.zeros((pad, PACK_COLS), dtype))
    return jnp.concatenate(blocks, axis=0)


def _unpack2d(buf, shapes):
    out, off = [], 0
    for sh in shapes:
        n, rows = _piece_rows(sh)
        piece = buf[off:off + rows]
        if sh[-1] == PACK_COLS and n % PACK_COLS == 0:
            out.append(piece[:n // PACK_COLS].reshape(sh))
        else:
            out.append(piece.reshape(-1)[:n].reshape(sh))
        off += rows
    return out


def _row_tile(h, cap=512):
    best = 8
    for cand in range(8, cap + 1, 8):
        if h % cand == 0:
            best = cand
    return best


def _chip_slice(full, axis, k, width):
    idx = [slice(None)] * full.ndim
    idx[axis] = slice(k * width, (k + 1) * width)
    return full[tuple(idx)]


SHARD_AXIS = dict(mlp_w1=2, mlp_w2=1, fox_w_in=2, fox_w_o=1, ssm_w_in=2, ssm_w_out=1, ssm_conv_w=2,
                  ssm_conv_b=1, ssm_norm_w=1, ada_w=2)


def kernel(x, c, ada_w, ada_b, ln_mix_g, ln_mix_b, ln_mlp_g, ln_mlp_b, mlp_w1, mlp_w2, fox_w_in, fox_b_f, fox_w_o, ssm_w_in, ssm_conv_w, ssm_conv_b, ssm_dt_bias, ssm_a_log, ssm_d, ssm_norm_w, ssm_w_out, loss_target, m_ada_w, m_ada_b, m_ln_mix_g, m_ln_mix_b, m_ln_mlp_g, m_ln_mlp_b, m_mlp_w1, m_mlp_w2, m_fox_w_in, m_fox_b_f, m_fox_w_o, m_ssm_w_in, m_ssm_conv_w, m_ssm_conv_b, m_ssm_dt_bias, m_ssm_a_log, m_ssm_d, m_ssm_norm_w, m_ssm_w_out, v_ada_w, v_ada_b, v_ln_mix_g, v_ln_mix_b, v_ln_mlp_g, v_ln_mlp_b, v_mlp_w1, v_mlp_w2, v_fox_w_in, v_fox_b_f, v_fox_w_o, v_ssm_w_in, v_ssm_conv_w, v_ssm_conv_b, v_ssm_dt_bias, v_ssm_a_log, v_ssm_d, v_ssm_norm_w, v_ssm_w_out):
    names = ("ada_w", "ada_b", "ln_mix_g", "ln_mix_b", "ln_mlp_g", "ln_mlp_b", "mlp_w1", "mlp_w2", "fox_w_in",
             "fox_b_f", "fox_w_o", "ssm_w_in", "ssm_conv_w", "ssm_conv_b", "ssm_dt_bias", "ssm_a_log", "ssm_d",
             "ssm_norm_w", "ssm_w_out")
    weights = dict(zip(names, (ada_w, ada_b, ln_mix_g, ln_mix_b, ln_mlp_g, ln_mlp_b, mlp_w1, mlp_w2, fox_w_in,
                               fox_b_f, fox_w_o, ssm_w_in, ssm_conv_w, ssm_conv_b, ssm_dt_bias, ssm_a_log, ssm_d,
                               ssm_norm_w, ssm_w_out)))
    m_in = dict(zip(names, (m_ada_w, m_ada_b, m_ln_mix_g, m_ln_mix_b, m_ln_mlp_g, m_ln_mlp_b, m_mlp_w1, m_mlp_w2,
                            m_fox_w_in, m_fox_b_f, m_fox_w_o, m_ssm_w_in, m_ssm_conv_w, m_ssm_conv_b, m_ssm_dt_bias,
                            m_ssm_a_log, m_ssm_d, m_ssm_norm_w, m_ssm_w_out)))
    v_in = dict(zip(names, (v_ada_w, v_ada_b, v_ln_mix_g, v_ln_mix_b, v_ln_mlp_g, v_ln_mlp_b, v_mlp_w1, v_mlp_w2,
                            v_fox_w_in, v_fox_b_f, v_fox_w_o, v_ssm_w_in, v_ssm_conv_w, v_ssm_conv_b, v_ssm_dt_bias,
                            v_ssm_a_log, v_ssm_d, v_ssm_norm_w, v_ssm_w_out)))

    mx, my, mc = _my_pos()
    chip = 2 * mx + my
    me = 4 * mx + 2 * my + mc
    x0 = x[0]
    target = loss_target[0]
    s, d = x0.shape
    n_qkv = 3 * FOX_HEADS * HEAD_DIM

    big_shapes = [weights[n].shape for n in BIG]
    packed = _pack2d([weights[n] for n in BIG], 32, BF16)
    gathered = _gather_chips(packed.reshape(2, packed.shape[0] // 2, PACK_COLS), "gather_weights")
    gathered = gathered.reshape(4, packed.shape[0], PACK_COLS)
    per_chip = [_unpack2d(jnp.where(chip == k, packed, gathered[k]), big_shapes) for k in range(4)]
    full = {n: jnp.concatenate([per_chip[k][i] for k in range(4)], axis=SHARD_AXIS[n]) for i, n in enumerate(BIG)}

    small_shapes = [weights[n].shape for n in SMALL_SHARDED]
    small_packed = _pack_rows([weights[n] for n in SMALL_SHARDED] + [c], 8, F32).reshape(-1, LANES)
    small_all = _allgather8(small_packed, "gather_small")
    small_chip = [_unpack(small_all[2 * k].reshape(-1), small_shapes) for k in range(4)]
    small_full = {n: jnp.concatenate([small_chip[k][i] for k in range(4)], axis=SHARD_AXIS[n])
                  for i, n in enumerate(SMALL_SHARDED)}
    n_small = sum(weights[n].size for n in SMALL_SHARDED)
    c_all = small_all.reshape(8, -1)[:, n_small:n_small + d]

    cols = ada_w.shape[2]
    ada_b_cols = lax.dynamic_slice_in_dim(ada_b, chip * cols, cols, axis=1)[:, None, :]
    c16 = jnp.pad(c_all, ((0, 8), (0, 0)))
    mod_part = _ada_forward(c16, ada_w, ada_b_cols, "ada_fwd")[:, :8, :]
    mod_all = _allgather8(mod_part.reshape(-1, LANES), "gather_mod").reshape(8, DEPTH, 8, cols)
    mod_mine = jnp.stack([lax.dynamic_index_in_dim(mod_all[2 * k], me, axis=1, keepdims=False) for k in range(4)], axis=1)
    mods = [jnp.pad(mod_mine[i].reshape(6, d), ((0, 2), (0, 0))) for i in range(DEPTH)]

    w = dict(
        fox_qkv=full["fox_w_in"][0][:, :n_qkv],
        fox_f=jnp.pad(full["fox_w_in"][0][:, n_qkv:], ((0, 0), (0, LANES - FOX_HEADS))),
        fox_in_pad=jnp.pad(full["fox_w_in"][0], ((0, 0), (0, LANES - FOX_HEADS))),
        fox_o=full["fox_w_o"][0],
        fox_bf=jnp.pad(fox_b_f, ((0, 0), (0, LANES - FOX_HEADS))),
        head_ind=(jnp.arange(d)[:, None] // HEAD_DIM == jnp.arange(LANES)[None, :]).astype(BF16),
        ssm_z=full["ssm_w_in"][0][:, :2048],
        ssm_xbc=full["ssm_w_in"][0][:, 2048:6144],
        ssm_dt=jnp.pad(full["ssm_w_in"][0][:, 6144:], ((0, 0), (0, LANES - SSM_HEADS))),
        ssm_in_pad=jnp.pad(full["ssm_w_in"][0], ((0, 0), (0, LANES - SSM_HEADS))),
        ssm_out=full["ssm_w_out"][0],
        conv_w8=jnp.pad(small_full["ssm_conv_w"][0], ((0, 8 - SSM_CONV), (0, 0))),
        conv_b=small_full["ssm_conv_b"],
        norm_w=small_full["ssm_norm_w"],
        dt_bias=jnp.pad(ssm_dt_bias, ((0, 0), (0, LANES - SSM_HEADS))),
        a_log=jnp.pad(ssm_a_log, ((0, 0), (0, LANES - SSM_HEADS))),
        d_e=jnp.repeat(ssm_d.reshape(SSM_GROUPS, 4), HEAD_DIM, axis=1)[:, None, :],
        ssd_cst=_ssd_constants(),
    )
    mixers = ((_fox_forward, _fox_backward), (_ssd_forward, _ssd_backward))

    saved = []
    xin = x0
    u = _modulate_in(x0, mods[0], "l0_mod_in")
    for i in range(DEPTH):
        tag = "l%d" % i
        y, sv = mixers[i % 2][0](u, w)
        r, x1, u2 = _res_ln_mod(xin, y, mods[i], ln_mix_g[i:i + 1], ln_mix_b[i:i + 1], tag + "_res_ln1")
        y2, hr, a = _mlp_forward(u2, full["mlp_w1"][i], full["mlp_w2"][i], tag)
        if i + 1 < DEPTH:
            r2, xin, u = _res_ln_next(x1, y2, mods[i], ln_mlp_g[i:i + 1], ln_mlp_b[i:i + 1], mods[i + 1],
                                      tag + "_res_ln2")
        else:
            r2 = _rowwise(lambda xv, yv, m: (ALPHA * xv + (1.0 + m[5:6]) * yv,), [x1, y2], [mods[i]], [(d, F32)], [],
                          name=tag + "_res2")[0]
        saved.append(dict(y=y, r=r, u2=u2, hr=hr, a=a, y2=y2, r2=r2, mix=sv))

    grads = {}
    dmod_parts = [dict() for _ in range(DEPTH)]
    ln_grads = {n: [None] * DEPTH for n in ("ln_mix_g", "ln_mix_b", "ln_mlp_g", "ln_mlp_b")}
    g_w1, g_w2 = [None] * DEPTH, [None] * DEPTH
    du = dres0 = None
    for i in reversed(range(DEPTH)):
        tag = "l%d" % i
        sv = saved[i]
        if i + 1 == DEPTH:
            dy2, dres, sq, dg2, db2, dgm = _loss_ln2_bwd(sv["r2"], sv["y2"], target, mods[i], ln_mlp_g[i:i + 1],
                                                         ln_mlp_b[i:i + 1], "loss_ln2_bwd")
            loss = lax.psum(0.5 * jnp.sum(sq) / d, ("x", "y", "c"))
        else:
            dy2, dres, dsca, dsha, dg2, db2, dgm = _mod_in_ln2_bwd(du, dres0, sv["r2"], sv["y2"], mods[i],
                                                                   ln_mlp_g[i:i + 1], ln_mlp_b[i:i + 1], mods[i + 1],
                                                                   tag + "_ln2_bwd")
            dmod_parts[i + 1].update(sc_a=dsca, sh_a=dsha)
        du2, g_w1[i], g_w2[i] = _mlp_backward(dy2, sv["u2"], sv["hr"], sv["a"], full["mlp_w1"][i], full["mlp_w2"][i], tag)
        dy, dres0, dscm, dshm, dg1, db1, dga = _ln1_bwd(du2, dres, sv["r"], sv["y"], mods[i], ln_mix_g[i:i + 1],
                                                        ln_mix_b[i:i + 1], tag + "_ln1_bwd")
        du, mg = mixers[i % 2][1](dy, sv["mix"], w)
        grads.update(mg)
        dmod_parts[i].update(g_a=dga, sh_m=dshm, sc_m=dscm, g_m=dgm)
        ln_grads["ln_mix_g"][i], ln_grads["ln_mix_b"][i] = dg1, db1
        ln_grads["ln_mlp_g"][i], ln_grads["ln_mlp_b"][i] = dg2, db2
    dx, dsca, dsha = _mod_in_bwd(du, dres0, x0, mods[0], "l0_mod_in_bwd")
    dmod_parts[0].update(sc_a=dsca, sh_a=dsha)
    dmods = [jnp.concatenate([p["sh_a"], p["sc_a"], p["g_a"], p["sh_m"], p["sc_m"], p["g_m"]], axis=1)
             for p in dmod_parts]
    grad_x = dx[None]
    grads["mlp_w1"] = jnp.stack(g_w1)
    grads["mlp_w2"] = jnp.stack(g_w2)
    for n in ("fox_w_in", "fox_w_o", "ssm_w_in", "ssm_w_out", "ssm_conv_w"):
        grads[n] = grads[n][None]

    small_names = ("ln_mix_g", "ln_mix_b", "ln_mlp_g", "ln_mlp_b", "fox_b_f", "ssm_dt_bias", "ssm_a_log", "ssm_d")
    small_parts = list(dmods)
    for n in small_names[:4]:
        small_parts.append(jnp.concatenate(ln_grads[n], axis=0))
    for n in small_names[4:]:
        small_parts.append(jnp.pad(grads[n], ((0, 0), (0, LANES - grads[n].shape[1]))))
    small_vec = _pack_rows(small_parts, 1, F32).reshape(-1, LANES)
    small_vec = jnp.pad(small_vec, ((0, -small_vec.shape[0] % 8), (0, 0)))
    small_g_all = _allgather8(small_vec, "gather_small_grads")
    small_sum = _sum_slots(small_g_all, "sum_small_grads").reshape(-1)
    dmod_sum = small_sum[:DEPTH * 6 * d].reshape(DEPTH, 6 * d)
    off = DEPTH * 6 * d
    final = {"ada_b": dmod_sum}
    for n in small_names[:4]:
        final[n] = small_sum[off:off + DEPTH * d].reshape(DEPTH, d)
        off += DEPTH * d
    for n in small_names[4:]:
        width = weights[n].shape[1]
        final[n] = small_sum[off:off + width].reshape(1, width)
        off += LANES

    dmod_all = small_g_all.reshape(8, -1)[:, :DEPTH * 6 * d].reshape(8, DEPTH, 6 * d)
    dmod_cols = lax.dynamic_slice_in_dim(dmod_all, chip * cols, cols, axis=2).transpose(1, 0, 2)
    final["ada_w"] = _ada_backward(c_all.T, dmod_cols, "ada_bwd")

    sharded = BIG + SMALL_SHARDED
    shard_shapes = [weights[n].shape for n in sharded]
    per_target = []
    for k in range(4):
        parts = [_chip_slice(grads[n], SHARD_AXIS[n], k, weights[n].shape[SHARD_AXIS[n]]) for n in sharded]
        per_target.append(_pack2d(parts, 128, F32))
    g_all = jnp.stack(per_target)
    rows = g_all.shape[1]
    g4 = g_all.reshape(4, 2, rows // 2, PACK_COLS)
    recv = _pair_exchange(g4, "rs_pair_exchange")
    part = _pair_add(g4, recv, jnp.reshape(mc, (1,)).astype(jnp.int32), "rs_pair_add")
    slots = _chip_exchange(part, "rs_chip_exchange")
    half = _chip_sum(part, slots, jnp.reshape(chip, (1,)).astype(jnp.int32), "rs_chip_sum")
    other = _pair_share(half, "rs_pair_share")
    both = jnp.concatenate([jnp.where(mc == 0, half, other), jnp.where(mc == 0, other, half)], axis=0)
    for n, g_shard in zip(sharded, _unpack2d(both, shard_shapes)):
        final[n] = g_shard

    outs_g, outs_d, outs_m, outs_v = [], [], [], []
    for n in names:
        wv = weights[n]
        two_d = (-1, wv.shape[-1])
        delta, mn, vn = _adamw(wv.reshape(two_d), final[n].reshape(two_d), m_in[n].reshape(two_d),
                               v_in[n].reshape(two_d), "adamw_" + n)
        outs_g.append(final[n].reshape(wv.shape))
        outs_d.append(delta.reshape(wv.shape))
        outs_m.append(mn.reshape(wv.shape))
        outs_v.append(vn.reshape(wv.shape))
    return (loss, grad_x, *outs_g, *outs_d, *outs_m, *outs_v)
```

```python
import functools

import jax
import jax.numpy as jnp
from jax import lax
from jax.experimental import pallas as pl
from jax.experimental.pallas import tpu as pltpu

F32, BF16 = jnp.float32, jnp.bfloat16
MESH = pl.DeviceIdType.MESH
HBM_SPEC = pl.BlockSpec(memory_space=pltpu.HBM)

VMEM_LIMIT_BYTES = 52 * 2**20
LANES = 128

FOX_HEADS, HEAD_DIM = 16, 64
SSM_HEADS, SSM_GROUPS, SSM_STATE, SSM_CHUNK, SSM_CONV = 32, 8, 128, 128, 4
SSM_GROUP_WIDTH = 256
LN_EPS, RMS_EPS = 1e-5, 1e-5
DEPTH = 2
ALPHA = (2.0 * DEPTH) ** 0.25
ADAM_LR, ADAM_B1, ADAM_B2, ADAM_EPS, ADAM_WD, ADAM_STEP = 0.001, 0.9, 0.999, 1e-08, 0.01, 10

ATT_TILE = 512
ROW_TILE = 256
SCAN_TILE = 512
MM_TM, MM_TN, MM_TK = 1024, 1024, 1024

NT_DIMS = (((1,), (1,)), ((), ()))
TN_DIMS = (((0,), (0,)), ((), ()))
NN_DIMS = (((1,), (0,)), ((), ()))


def _pcall(body, *, name, out_shape, grid=(), in_specs=None, out_specs=None, scratch=(), sem=None, prefetch=0):
    params = dict(vmem_limit_bytes=VMEM_LIMIT_BYTES)
    if sem is not None:
        params["dimension_semantics"] = sem
    if prefetch:
        grid_spec = pltpu.PrefetchScalarGridSpec(num_scalar_prefetch=prefetch, grid=grid, in_specs=in_specs,
                                                 out_specs=out_specs, scratch_shapes=scratch)
        return pl.pallas_call(body, out_shape=out_shape, grid_spec=grid_spec, name=name,
                              compiler_params=pltpu.CompilerParams(**params))
    kwargs = {}
    if in_specs is not None:
        kwargs["in_specs"] = in_specs
    if out_specs is not None:
        kwargs["out_specs"] = out_specs
    return pl.pallas_call(body, out_shape=out_shape, grid=grid, scratch_shapes=scratch, name=name,
                          compiler_params=pltpu.CompilerParams(**params), **kwargs)


def _sds(shape, dtype):
    return jax.ShapeDtypeStruct(tuple(shape), dtype)


def _dot(a, b, dims=NN_DIMS):
    return lax.dot_general(a, b, dims, preferred_element_type=F32)


def _sigmoid(x):
    return 1.0 / (1.0 + jnp.exp(-x))


def _silu(x):
    return x * _sigmoid(x)


def _dsilu(x):
    s = _sigmoid(x)
    return s * (1.0 + x * (1.0 - s))


def _dot_split(x, m16, dims=NN_DIMS, passes=3):
    hi = x.astype(BF16)
    r1 = x - hi.astype(F32)
    mid = r1.astype(BF16)
    out = _dot(hi, m16, dims) + _dot(mid, m16, dims)
    if passes == 3:
        lo = (r1 - mid.astype(F32)).astype(BF16)
        out = out + _dot(lo, m16, dims)
    return out


def _mm(a, b, dims, outs, *, name, tm=MM_TM, tn=MM_TN, tk=MM_TK, epi=None, extra=()):
    if dims == "nn":
        (m, k), (k2, n) = a.shape, b.shape
    elif dims == "nt":
        (m, k), (n, k2) = a.shape, b.shape
    else:
        (k, m), (k2, n) = a.shape, b.shape
    assert k == k2, (a.shape, b.shape, dims)
    tm, tn, tk = min(tm, m), min(tn, n), min(tk, k)
    assert m % tm == 0 and n % tn == 0 and k % tk == 0, (m, n, k, tm, tn, tk)
    nk = k // tk
    dn = {"nn": NN_DIMS, "nt": NT_DIMS, "tn": TN_DIMS}[dims]
    n_extra, n_out = len(extra), len(outs)
    if epi is None:
        epi = lambda acc: (acc,) * n_out

    def body(a_ref, b_ref, *rest):
        extra_refs, out_refs, acc_ref = rest[:n_extra], rest[n_extra:n_extra + n_out], rest[-1]
        kk = pl.program_id(2)

        @pl.when(kk == 0)
        def _():
            acc_ref[...] = jnp.zeros_like(acc_ref)

        acc_ref[...] += _dot(a_ref[...].astype(BF16), b_ref[...].astype(BF16), dn)

        @pl.when(kk == nk - 1)
        def _():
            res = epi(acc_ref[...], *[e[...] for e in extra_refs])
            for o, r in zip(out_refs, res):
                o[...] = r.astype(o.dtype)

    if dims == "tn":
        a_spec = pl.BlockSpec((tk, tm), lambda i, j, kk: (kk, i))
    else:
        a_spec = pl.BlockSpec((tm, tk), lambda i, j, kk: (i, kk))
    if dims == "nt":
        b_spec = pl.BlockSpec((tn, tk), lambda i, j, kk: (j, kk))
    else:
        b_spec = pl.BlockSpec((tk, tn), lambda i, j, kk: (kk, j))
    o_spec = pl.BlockSpec((tm, tn), lambda i, j, kk: (i, j))
    res = _pcall(body, name=name, grid=(m // tm, n // tn, nk),
                 in_specs=[a_spec, b_spec] + [o_spec] * n_extra,
                 out_specs=[o_spec] * n_out,
                 out_shape=[_sds((m, n), d) for d in outs],
                 scratch=[pltpu.VMEM((tm, tn), F32)],
                 sem=("parallel", "parallel", "arbitrary"))(a, b, *extra)
    return res[0] if n_out == 1 else res


def _rowwise(fn, rows, consts, row_outs, acc_outs, *, name, tm=ROW_TILE):
    s = rows[0].shape[0]
    tm = min(tm, s)
    assert s % tm == 0
    n_in, n_o = len(rows) + len(consts), len(row_outs)

    def body(*refs):
        ins, outs = refs[:n_in], refs[n_in:]
        res = fn(*[r[...] for r in ins])
        if not isinstance(res, (tuple, list)):
            res = (res,)
        for o, val in zip(outs[:n_o], res[:n_o]):
            o[...] = val.astype(o.dtype)
        if acc_outs:
            @pl.when(pl.program_id(0) == 0)
            def _():
                for o in outs[n_o:]:
                    o[...] = jnp.zeros_like(o)
            for o, val in zip(outs[n_o:], res[n_o:]):
                o[...] += val

    in_specs = [pl.BlockSpec((tm, r.shape[1]), lambda i: (i, 0)) for r in rows]
    in_specs += [pl.BlockSpec(c.shape, functools.partial(lambda nd, i: (0,) * nd, c.ndim)) for c in consts]
    out_specs = [pl.BlockSpec((tm, c), lambda i: (i, 0)) for c, _ in row_outs]
    out_specs += [pl.BlockSpec(tuple(sh), lambda i: (0, 0)) for sh in acc_outs]
    out_shape = [_sds((s, c), d) for c, d in row_outs] + [_sds(sh, F32) for sh in acc_outs]
    res = _pcall(body, name=name, grid=(s // tm,), in_specs=in_specs, out_specs=out_specs,
                 out_shape=out_shape, sem=("arbitrary",))(*rows, *consts)
    return res


def _colsum(x):
    return jnp.sum(x, axis=0, keepdims=True)


def _ln_stats(r):
    mu = jnp.mean(r, axis=-1, keepdims=True)
    xc = r - mu
    var = jnp.mean(xc * xc, axis=-1, keepdims=True)
    rstd = lax.rsqrt(var + LN_EPS)
    return xc * rstd, rstd


def _ln_bwd(dy, xhat, rstd, gamma):
    dyg = dy * gamma
    m1 = jnp.mean(dyg, axis=-1, keepdims=True)
    m2 = jnp.mean(dyg * xhat, axis=-1, keepdims=True)
    return rstd * (dyg - m1 - xhat * m2)


def _modulate_in(x, mod, name):
    def fn(xv, m):
        return (xv * (1.0 + m[1:2]) + m[0:1],)
    return _rowwise(fn, [x], [mod], [(x.shape[1], BF16)], [], name=name)[0]


def _res_ln_mod(x, y, mod, g, b, name):
    d = x.shape[1]

    def fn(xv, yv, m, gv, bv):
        r = ALPHA * xv + (1.0 + m[2:3]) * yv
        xhat, _ = _ln_stats(r)
        x1 = xhat * gv + bv
        u2 = x1 * (1.0 + m[4:5]) + m[3:4]
        return r, x1, u2
    return _rowwise(fn, [x, y], [mod, g, b], [(d, F32), (d, F32), (d, BF16)], [], name=name)


def _res_ln(x, y, mod, g, b, name):
    d = x.shape[1]

    def fn(xv, yv, m, gv, bv):
        r = ALPHA * xv + (1.0 + m[5:6]) * yv
        xhat, _ = _ln_stats(r)
        return r, xhat * gv + bv
    return _rowwise(fn, [x, y], [mod, g, b], [(d, F32), (d, F32)], [], name=name)


def _res_ln_next(x, y, mod, g, b, mod_next, name):
    d = x.shape[1]

    def fn(xv, yv, m, gv, bv, mn):
        r = ALPHA * xv + (1.0 + m[5:6]) * yv
        xhat, _ = _ln_stats(r)
        out = xhat * gv + bv
        return r, out, out * (1.0 + mn[1:2]) + mn[0:1]
    return _rowwise(fn, [x, y], [mod, g, b, mod_next], [(d, F32), (d, F32), (d, BF16)], [], name=name)


def _loss_ln2_bwd(r2, y2, target, mod, g, b, name):
    d = r2.shape[1]

    def fn(rv, yv, tv, m, gv, bv):
        xhat, rstd = _ln_stats(rv)
        e = xhat * gv + bv - tv
        dxv = e * (1.0 / d)
        dr = _ln_bwd(dxv, xhat, rstd, gv)
        return (dr * (1.0 + m[5:6]), ALPHA * dr,
                _colsum(e * e), _colsum(dxv * xhat), _colsum(dxv), _colsum(dr * yv))
    return _rowwise(fn, [r2, y2, target], [mod, g, b], [(d, BF16), (d, F32)], [(1, d)] * 4, name=name)


def _mod_in_ln2_bwd(du, dres, r2, y2, mod, g, b, mod_next, name):
    d = du.shape[1]

    def fn(duv, drv, rv, yv, m, gv, bv, mn):
        xhat, rstd = _ln_stats(rv)
        xout = xhat * gv + bv
        dxv = duv * (1.0 + mn[1:2]) + drv
        dr = _ln_bwd(dxv, xhat, rstd, gv)
        return (dr * (1.0 + m[5:6]), ALPHA * dr,
                _colsum(duv * xout), _colsum(duv), _colsum(dxv * xhat), _colsum(dxv), _colsum(dr * yv))
    return _rowwise(fn, [du, dres, r2, y2], [mod, g, b, mod_next], [(d, BF16), (d, F32)], [(1, d)] * 5, name=name)


def _ln2_bwd(dx, r2, y2, mod, g, name):
    d = dx.shape[1]

    def fn(dxv, rv, yv, m, gv):
        xhat, rstd = _ln_stats(rv)
        dr = _ln_bwd(dxv, xhat, rstd, gv)
        return (dr * (1.0 + m[5:6]), ALPHA * dr,
                _colsum(dxv * xhat), _colsum(dxv), _colsum(dr * yv))
    return _rowwise(fn, [dx, r2, y2], [mod, g], [(d, BF16), (d, F32)], [(1, d)] * 3, name=name)


def _ln1_bwd(du2, dres, r, y, mod, g, b, name):
    d = du2.shape[1]

    def fn(duv, drv, rv, yv, m, gv, bv):
        xhat, rstd = _ln_stats(rv)
        x1 = xhat * gv + bv
        dx1 = duv * (1.0 + m[4:5]) + drv
        dr = _ln_bwd(dx1, xhat, rstd, gv)
        return (dr * (1.0 + m[2:3]), ALPHA * dr,
                _colsum(duv * x1), _colsum(duv), _colsum(dx1 * xhat), _colsum(dx1), _colsum(dr * yv))
    return _rowwise(fn, [du2, dres, r, y], [mod, g, b], [(d, BF16), (d, F32)], [(1, d)] * 5, name=name)


def _mod_in_bwd(du, dres, x, mod, name):
    d = du.shape[1]

    def fn(duv, drv, xv, m):
        return duv * (1.0 + m[1:2]) + drv, _colsum(duv * xv), _colsum(duv)
    return _rowwise(fn, [du, dres, x], [mod], [(d, F32)], [(1, d)] * 2, name=name)


def _fox_gate(fraw, b_pad, name):
    s = fraw.shape[0]
    tb = min(SCAN_TILE, s)

    def body(f_ref, b_ref, cum_ref, carry):
        @pl.when(pl.program_id(0) == 0)
        def _():
            carry[...] = jnp.zeros_like(carry)
        z = f_ref[...] + b_ref[...]
        lf = jnp.minimum(z, 0.0) - jnp.log(1.0 + jnp.exp(-jnp.abs(z)))
        lane = lax.broadcasted_iota(jnp.int32, (tb, LANES), 1)
        row = lax.broadcasted_iota(jnp.int32, (tb, LANES), 0)
        c = jnp.where(lane < FOX_HEADS, lf, 0.0)
        sh = 1
        while sh < tb:
            c = c + jnp.where(row >= sh, pltpu.roll(c, sh, 0), 0.0)
            sh *= 2
        c = c + carry[0:1, :]
        cum_ref[...] = c
        carry[0:1, :] = c[tb - 1:tb, :]

    return _pcall(body, name=name, grid=(s // tb,),
                  in_specs=[pl.BlockSpec((tb, LANES), lambda i: (i, 0)), pl.BlockSpec((1, LANES), lambda i: (0, 0))],
                  out_specs=pl.BlockSpec((tb, LANES), lambda i: (i, 0)),
                  out_shape=_sds((s, LANES), F32), scratch=[pltpu.VMEM((8, LANES), F32)],
                  sem=("arbitrary",))(fraw, b_pad)


def _fox_gate_bwd(drow, dcol, fraw, b_pad, name):
    s = fraw.shape[0]
    tb = min(SCAN_TILE, s)
    n = s // tb

    def body(dr_ref, dc_ref, f_ref, b_ref, df_ref, db_ref, carry):
        @pl.when(pl.program_id(0) == 0)
        def _():
            carry[...] = jnp.zeros_like(carry)
            db_ref[...] = jnp.zeros_like(db_ref)
        row = lax.broadcasted_iota(jnp.int32, (tb, LANES), 0)
        c = dr_ref[...] + dc_ref[...]
        sh = 1
        while sh < tb:
            c = c + jnp.where(row + sh < tb, pltpu.roll(c, tb - sh, 0), 0.0)
            sh *= 2
        c = c + carry[0:1, :]
        carry[0:1, :] = c[0:1, :]
        z = f_ref[...] + b_ref[...]
        df = c * (1.0 / (1.0 + jnp.exp(z)))
        df_ref[...] = df.astype(df_ref.dtype)
        db_ref[...] += _colsum(df)

    rev = lambda i: (n - 1 - i, 0)
    return _pcall(body, name=name, grid=(n,),
                  in_specs=[pl.BlockSpec((tb, LANES), rev)] * 3 + [pl.BlockSpec((1, LANES), lambda i: (0, 0))],
                  out_specs=[pl.BlockSpec((tb, LANES), rev), pl.BlockSpec((1, LANES), lambda i: (0, 0))],
                  out_shape=[_sds((s, LANES), BF16), _sds((1, LANES), F32)],
                  scratch=[pltpu.VMEM((8, LANES), F32)], sem=("arbitrary",))(drow, dcol, fraw, b_pad)


def _head_pair_masks(t):
    lane = lax.broadcasted_iota(jnp.int32, (t, LANES), 1)
    return lane < HEAD_DIM


def _lane_blocks(x):
    return [x[:, c * LANES:(c + 1) * LANES] for c in range(x.shape[1] // LANES)]


def _sum_list(xs):
    acc = xs[0]
    for x in xs[1:]:
        acc = acc + x
    return acc


def _causal(t, transposed=False):
    ri = lax.broadcasted_iota(jnp.int32, (t, t), 0)
    ci = lax.broadcasted_iota(jnp.int32, (t, t), 1)
    return ci >= ri if transposed else ri >= ci


def _flash_fwd(qkv, ck_rows, kb_start, name):
    s = qkv.shape[0]
    t = min(ATT_TILE, s)
    nq = s // t
    scale = HEAD_DIM ** -0.5
    hp_blocks = FOX_HEADS // 2

    def body(ks_ref, q_ref, k_ref, v_ref, ck_ref, o_ref, lse_ref, acc_ref, m_ref, l_ref):
        hp, qb = pl.program_id(0), pl.program_id(1)
        q2 = q_ref[...] * scale
        first = _head_pair_masks(t)
        zero = jnp.zeros_like(q2)
        qs = (jnp.where(first, q2, zero), jnp.where(first, zero, q2))
        m_ref[...] = jnp.full_like(m_ref, -jnp.inf)
        l_ref[...] = jnp.zeros_like(l_ref)
        acc_ref[...] = jnp.zeros_like(acc_ref)

        def tile(kb, diagonal):
            off = pl.multiple_of(kb * t, t)
            k2 = k_ref[pl.ds(off, t), :]
            v2 = v_ref[pl.ds(off, t), :]
            ck = ck_ref[kb]
            pvs, als = [], []
            for j in range(2):
                sc = _dot(qs[j], k2, NT_DIMS) - ck[j:j + 1, :]
                if diagonal:
                    sc = jnp.where(_causal(t), sc, -jnp.inf)
                blocks = _lane_blocks(sc)
                mx = blocks[0]
                for b in blocks[1:]:
                    mx = jnp.maximum(mx, b)
                m_old = m_ref[j]
                m_new = jnp.maximum(m_old, jnp.max(mx, axis=1, keepdims=True))
                ps = [jnp.exp(b - m_new) for b in blocks]
                a = jnp.exp(m_old - m_new)
                l_ref[j] = a * l_ref[j] + _sum_list(ps)
                m_ref[j] = m_new
                pvs.append(_dot(jnp.concatenate(ps, axis=1).astype(BF16), v2))
                als.append(a)
            acc_ref[...] = jnp.where(first, als[0], als[1]) * acc_ref[...] + jnp.where(first, pvs[0], pvs[1])

        def step(kb, carry):
            tile(kb, False)
            return carry

        lax.fori_loop(ks_ref[hp, qb], qb, step, 0)
        tile(qb, True)
        l0 = jnp.sum(l_ref[0], axis=1, keepdims=True)
        l1 = jnp.sum(l_ref[1], axis=1, keepdims=True)
        o_ref[...] = acc_ref[...] / jnp.where(first, l0, l1)
        lse_ref[:, 0:1] = m_ref[0][:, 0:1] + jnp.log(l0)
        lse_ref[:, 1:2] = m_ref[1][:, 0:1] + jnp.log(l1)

    return _pcall(
        body, name=name, grid=(hp_blocks, nq), prefetch=1,
        in_specs=[pl.BlockSpec((t, LANES), lambda h, i, ks: (i, h)),
                  pl.BlockSpec((s, LANES), lambda h, i, ks: (0, hp_blocks + h)),
                  pl.BlockSpec((s, LANES), lambda h, i, ks: (0, 2 * hp_blocks + h)),
                  pl.BlockSpec((None, nq, 2, t), lambda h, i, ks: (h, 0, 0, 0))],
        out_specs=[pl.BlockSpec((t, LANES), lambda h, i, ks: (i, h)),
                   pl.BlockSpec((None, t, 2), lambda h, i, ks: (h, i, 0))],
        out_shape=[_sds((s, hp_blocks * LANES), F32), _sds((hp_blocks, s, 2), F32)],
        scratch=[pltpu.VMEM((t, LANES), F32), pltpu.VMEM((2, t, LANES), F32), pltpu.VMEM((2, t, LANES), F32)],
        sem=("parallel", "arbitrary"))(kb_start, qkv, qkv, qkv, ck_rows)


def _flash_dq(qkv, do16, ck_rows, lse_c, dl_c, kb_start, name):
    s = qkv.shape[0]
    t = min(ATT_TILE, s)
    nq = s // t
    scale = HEAD_DIM ** -0.5
    hp_blocks = FOX_HEADS // 2

    def body(ks_ref, q_ref, do_ref, k_ref, v_ref, ck_ref, lse_ref, dl_ref, dq_ref, drow_ref, acc_ref, row_acc):
        hp, qb = pl.program_id(0), pl.program_id(1)
        q2, do2 = q_ref[...] * scale, do_ref[...]
        first = _head_pair_masks(t)
        zero = jnp.zeros_like(q2)
        qs = (jnp.where(first, q2, zero), jnp.where(first, zero, q2))
        dos = (jnp.where(first, do2, zero), jnp.where(first, zero, do2))
        lse, dl = lse_ref[...], dl_ref[...]
        lse_b = [jnp.broadcast_to(lse[:, j:j + 1], (t, LANES)) for j in range(2)]
        dl_b = [jnp.broadcast_to(dl[:, j:j + 1], (t, LANES)) for j in range(2)]
        acc_ref[...] = jnp.zeros_like(acc_ref)
        row_acc[...] = jnp.zeros_like(row_acc)

        def tile(kb, diagonal):
            off = pl.multiple_of(kb * t, t)
            k2 = k_ref[pl.ds(off, t), :]
            v2 = v_ref[pl.ds(off, t), :]
            ck = ck_ref[kb]
            dqs = []
            for j in range(2):
                sc = _dot(qs[j], k2, NT_DIMS) - ck[j:j + 1, :]
                if diagonal:
                    sc = jnp.where(_causal(t), sc, -jnp.inf)
                dp = _dot(dos[j], v2, NT_DIMS)
                dsb = [jnp.exp(x - lse_b[j]) * (d - dl_b[j]) for x, d in zip(_lane_blocks(sc), _lane_blocks(dp))]
                row_acc[j] += _sum_list(dsb)
                dqs.append(_dot(jnp.concatenate(dsb, axis=1).astype(BF16), k2))
            acc_ref[...] += jnp.where(first, dqs[0], dqs[1])

        def step(kb, carry):
            tile(kb, False)
            return carry

        lax.fori_loop(ks_ref[hp, qb], qb, step, 0)
        tile(qb, True)
        dq_ref[...] = (acc_ref[...] * scale).astype(dq_ref.dtype)
        drow_ref[:, 0:1] = jnp.sum(row_acc[0], axis=1, keepdims=True)
        drow_ref[:, 1:2] = jnp.sum(row_acc[1], axis=1, keepdims=True)

    return _pcall(
        body, name=name, grid=(hp_blocks, nq), prefetch=1,
        in_specs=[pl.BlockSpec((t, LANES), lambda h, i, ks: (i, h)),
                  pl.BlockSpec((t, LANES), lambda h, i, ks: (i, h)),
                  pl.BlockSpec((s, LANES), lambda h, i, ks: (0, hp_blocks + h)),
                  pl.BlockSpec((s, LANES), lambda h, i, ks: (0, 2 * hp_blocks + h)),
                  pl.BlockSpec((None, nq, 2, t), lambda h, i, ks: (h, 0, 0, 0)),
                  pl.BlockSpec((None, t, 2), lambda h, i, ks: (h, i, 0)),
                  pl.BlockSpec((None, t, 2), lambda h, i, ks: (h, i, 0))],
        out_specs=[pl.BlockSpec((t, LANES), lambda h, i, ks: (i, h)),
                   pl.BlockSpec((None, t, 2), lambda h, i, ks: (h, i, 0))],
        out_shape=[_sds((s, hp_blocks * LANES), BF16), _sds((hp_blocks, s, 2), F32)],
        scratch=[pltpu.VMEM((t, LANES), F32), pltpu.VMEM((2, t, LANES), F32)],
        sem=("parallel", "arbitrary"))(kb_start, qkv, do16, qkv, qkv, ck_rows, lse_c, dl_c)


def _flash_dkv(qkv, do16, cum, lse_rows, dl_rows, qb_end, name):
    s = qkv.shape[0]
    t = min(ATT_TILE, s)
    nq = s // t
    scale = HEAD_DIM ** -0.5
    hp_blocks = FOX_HEADS // 2

    def body(qe_ref, k_ref, v_ref, cum_ref, q_ref, do_ref, lse_ref, dl_ref, dk_ref, dv_ref, dck_ref,
             dk_acc, dv_acc, dck_acc):
        hp, kb = pl.program_id(0), pl.program_id(1)
        k2, v2 = k_ref[...] * scale, v_ref[...]
        first = _head_pair_masks(t)
        zero = jnp.zeros_like(k2)
        ks = (jnp.where(first, k2, zero), jnp.where(first, zero, k2))
        vs = (jnp.where(first, v2, zero), jnp.where(first, zero, v2))
        cumv = cum_ref[...]
        lane = lax.broadcasted_iota(jnp.int32, (t, LANES), 1)
        ck_b = [jnp.broadcast_to(jnp.sum(jnp.where(lane == 2 * hp + j, cumv, 0.0), axis=1, keepdims=True), (t, LANES))
                for j in range(2)]
        dk_acc[...] = jnp.zeros_like(dk_acc)
        dv_acc[...] = jnp.zeros_like(dv_acc)
        dck_acc[...] = jnp.zeros_like(dck_acc)

        def tile(qb, diagonal):
            off = pl.multiple_of(qb * t, t)
            q2 = q_ref[pl.ds(off, t), :]
            do2 = do_ref[pl.ds(off, t), :]
            lse, dl = lse_ref[qb], dl_ref[qb]
            dvs, dks = [], []
            for j in range(2):
                sc = _dot(ks[j], q2, NT_DIMS)
                if diagonal:
                    sc = jnp.where(_causal(t, transposed=True), sc, -jnp.inf)
                dp = _dot(vs[j], do2, NT_DIMS) - dl[j:j + 1, :]
                pb = [jnp.exp((x - ck_b[j]) - l) for x, l in zip(_lane_blocks(sc), _lane_blocks(lse[j:j + 1, :]))]
                dsb = [p * d for p, d in zip(pb, _lane_blocks(dp))]
                dck_acc[j] += _sum_list(dsb)
                dvs.append(_dot(jnp.concatenate(pb, axis=1).astype(BF16), do2))
                dks.append(_dot(jnp.concatenate(dsb, axis=1).astype(BF16), q2))
            dv_acc[...] += jnp.where(first, dvs[0], dvs[1])
            dk_acc[...] += jnp.where(first, dks[0], dks[1])

        def step(qb, carry):
            tile(qb, False)
            return carry

        tile(kb, True)
        lax.fori_loop(kb + 1, qe_ref[hp, kb] + 1, step, 0)
        dk_ref[...] = (dk_acc[...] * scale).astype(dk_ref.dtype)
        dv_ref[...] = dv_acc[...].astype(dv_ref.dtype)
        dck_ref[:, 0:1] = -jnp.sum(dck_acc[0], axis=1, keepdims=True)
        dck_ref[:, 1:2] = -jnp.sum(dck_acc[1], axis=1, keepdims=True)

    return _pcall(
        body, name=name, grid=(hp_blocks, nq), prefetch=1,
        in_specs=[pl.BlockSpec((t, LANES), lambda h, j, qe: (j, hp_blocks + h)),
                  pl.BlockSpec((t, LANES), lambda h, j, qe: (j, 2 * hp_blocks + h)),
                  pl.BlockSpec((t, LANES), lambda h, j, qe: (j, 0)),
                  pl.BlockSpec((s, LANES), lambda h, j, qe: (0, h)),
                  pl.BlockSpec((s, LANES), lambda h, j, qe: (0, h)),
                  pl.BlockSpec((None, nq, 2, t), lambda h, j, qe: (h, 0, 0, 0)),
                  pl.BlockSpec((None, nq, 2, t), lambda h, j, qe: (h, 0, 0, 0))],
        out_specs=[pl.BlockSpec((t, LANES), lambda h, j, qe: (j, h)),
                   pl.BlockSpec((t, LANES), lambda h, j, qe: (j, h)),
                   pl.BlockSpec((None, t, 2), lambda h, j, qe: (h, j, 0))],
        out_shape=[_sds((s, hp_blocks * LANES), BF16), _sds((s, hp_blocks * LANES), BF16),
                   _sds((hp_blocks, s, 2), F32)],
        scratch=[pltpu.VMEM((t, LANES), F32), pltpu.VMEM((t, LANES), F32), pltpu.VMEM((2, t, LANES), F32)],
        sem=("parallel", "arbitrary"))(qb_end, qkv, qkv, cum, qkv, do16, lse_rows, dl_rows)


SKIP_NATS = 110.0


def _qk_norms(qkv, ind16, name):
    d = FOX_HEADS * HEAD_DIM

    def fn(tile, ind):
        q = tile[:, :d].astype(F32)
        k = tile[:, d:2 * d].astype(F32)
        return _dot_split(q * q, ind), _dot_split(k * k, ind)
    return _rowwise(fn, [qkv], [ind16], [(LANES, F32), (LANES, F32)], [], name=name)


def _skip_bounds(qn, kn, cum, t):
    s = qn.shape[0]
    nq = s // t
    hp = FOX_HEADS // 2
    scale = HEAD_DIM ** -0.5
    qmax = jnp.sqrt(jnp.max(qn.reshape(nq, t, FOX_HEADS), axis=1))
    kmax = jnp.sqrt(jnp.max(kn, axis=0))
    bound = qmax * kmax[None, :] * (scale * 1.01) + 1e-3
    gap = cum[0::t][:, None, :] - cum[t - 1::t][None, :, :]
    idx = jnp.arange(nq, dtype=jnp.int32)
    needed = (gap + 2.0 * bound[:, None, :]) > -SKIP_NATS
    needed = needed.reshape(nq, nq, hp, 2).any(axis=-1) & (idx[None, :] <= idx[:, None])[:, :, None]
    first = jnp.min(jnp.where(needed, idx[None, :, None], nq), axis=1)
    first = jnp.minimum(first, idx[:, None])
    start = lax.cummin(first, axis=0, reverse=True)
    uses = start[:, None, :] <= idx[None, :, None]
    last = jnp.max(jnp.where(uses, idx[:, None, None], 0), axis=0)
    last = jnp.maximum(last, idx[:, None])
    return start.T.astype(jnp.int32), last.T.astype(jnp.int32)


def _head_rowsum(prod, ind16, name):
    def fn(a, b, ind):
        return (_dot_split(a * b, ind),)
    return _rowwise(fn, list(prod), [ind16], [(LANES, F32)], [], name=name)[0]


def _pairs_cols(x16):
    s = x16.shape[0]
    return x16.reshape(s, FOX_HEADS // 2, 2).transpose(1, 0, 2)


def _pairs_rows(x16, t):
    s = x16.shape[0]
    return x16.reshape(s // t, t, FOX_HEADS // 2, 2).transpose(2, 0, 3, 1)


def _fox_forward(u, w):
    s = u.shape[0]
    t = min(ATT_TILE, s)
    qkv = _mm(u, w["fox_qkv"], "nn", [BF16], name="fox_qkv")
    fraw = _mm(u, w["fox_f"], "nn", [F32], name="fox_fproj")
    cum = _fox_gate(fraw, w["fox_bf"], "fox_gate")
    ck_rows = _pairs_rows(cum[:, :FOX_HEADS], t)
    qn, kn = _qk_norms(qkv, w["head_ind"], "fox_qk_norms")
    kb_start, qb_end = _skip_bounds(qn[:, :FOX_HEADS], kn[:, :FOX_HEADS], cum[:, :FOX_HEADS], t)
    o, lse = _flash_fwd(qkv, ck_rows, kb_start, "fox_flash_fwd")
    y = _mm(o, w["fox_o"], "nn", [F32], name="fox_oproj")
    return y, dict(u=u, qkv=qkv, fraw=fraw, cum=cum, ck_rows=ck_rows, o=o, lse=lse, kb_start=kb_start,
                   qb_end=qb_end)


def _fox_backward(dy, sv, w):
    s = dy.shape[0]
    t = min(ATT_TILE, s)
    do32, do16 = _mm(dy, w["fox_o"], "nt", [F32, BF16], name="fox_do")
    g_wo = _mm(sv["o"], dy, "tn", [F32], name="fox_gwo")
    delta = _head_rowsum((do32, sv["o"]), w["head_ind"], "fox_delta")[:, :FOX_HEADS]
    lse16 = sv["lse"].transpose(1, 0, 2).reshape(s, FOX_HEADS)
    dq, drow = _flash_dq(sv["qkv"], do16, sv["ck_rows"], sv["lse"], _pairs_cols(delta), sv["kb_start"],
                         "fox_flash_dq")
    dk, dv, dck = _flash_dkv(sv["qkv"], do16, sv["cum"], _pairs_rows(lse16, t), _pairs_rows(delta, t),
                             sv["qb_end"], "fox_flash_dkv")
    pad = ((0, 0), (0, LANES - FOX_HEADS))
    dcol = jnp.pad(dck.transpose(1, 0, 2).reshape(s, FOX_HEADS), pad)
    drow = jnp.pad(drow.transpose(1, 0, 2).reshape(s, FOX_HEADS), pad)
    df, db_f = _fox_gate_bwd(drow, dcol, sv["fraw"], w["fox_bf"], "fox_gate_bwd")
    dproj = jnp.concatenate([dq, dk, dv, df], axis=1)
    du = _mm(dproj, w["fox_in_pad"], "nt", [F32], name="fox_du", tk=640)
    g_win = _mm(sv["u"], dproj, "tn", [F32], name="fox_gwin", tn=640)
    return du, dict(fox_w_in=g_win[:, :3 * FOX_HEADS * HEAD_DIM + FOX_HEADS], fox_w_o=g_wo,
                    fox_b_f=db_f[:, :FOX_HEADS])


def _conv_fwd(xpre, w8, b, name):
    s, c = xpre.shape
    tm, tc = min(ROW_TILE, s), min(1024, c)
    hb = tm // 8

    def body(x_ref, h_ref, w_ref, b_ref, xc_ref, xa_ref):
        i = pl.program_id(1)
        x = x_ref[...]
        halo = jnp.where(i > 0, h_ref[...], 0.0)
        w = w_ref[...]
        row = lax.broadcasted_iota(jnp.int32, (tm, tc), 0)
        row8 = lax.broadcasted_iota(jnp.int32, (8, tc), 0)
        acc = x * w[3:4] + b_ref[...]
        x8 = x[0:8]
        acc8 = x8 * w[3:4] + b_ref[...]
        for j in range(1, SSM_CONV):
            acc = acc + w[3 - j:4 - j] * pltpu.roll(x, j, 0)
            acc8 = acc8 + w[3 - j:4 - j] * jnp.where(row8 < j, pltpu.roll(halo, j, 0), pltpu.roll(x8, j, 0))
        xc_ref[...] = acc
        xc_ref[0:8, :] = acc8
        xc = xc_ref[...]
        xa_ref[...] = _silu(xc)

    tile = pl.BlockSpec((tm, tc), lambda jc, i: (i, jc))
    return _pcall(body, name=name, grid=(c // tc, s // tm),
                  in_specs=[tile, pl.BlockSpec((8, tc), lambda jc, i: (jnp.maximum(i * hb - 1, 0), jc)),
                            pl.BlockSpec((8, tc), lambda jc, i: (0, jc)), pl.BlockSpec((1, tc), lambda jc, i: (0, jc))],
                  out_specs=[tile, tile], out_shape=[_sds((s, c), F32), _sds((s, c), F32)],
                  sem=("parallel", "arbitrary"))(xpre, xpre, w8, b)


def _conv_bwd(dxa, xc, xpre, w8, name):
    s, c = xpre.shape
    tm, tc = min(ROW_TILE, s), min(1024, c)
    hb = tm // 8
    n = s // tm

    def body(d_ref, xc_ref, x_ref, xh_ref, dn_ref, xcn_ref, w_ref, dx_ref, dw_ref, db_ref, scr):
        i = pl.program_id(1)

        @pl.when(i == 0)
        def _():
            dw_ref[...] = jnp.zeros_like(dw_ref)
            db_ref[...] = jnp.zeros_like(db_ref)
        w = w_ref[...]
        x = x_ref[...]
        g = d_ref[...] * _dsilu(xc_ref[...])
        gn = jnp.where(i < n - 1, dn_ref[...] * _dsilu(xcn_ref[...]), 0.0)
        halo = jnp.where(i > 0, xh_ref[...], 0.0)
        row = lax.broadcasted_iota(jnp.int32, (tm, tc), 0)
        row8 = lax.broadcasted_iota(jnp.int32, (8, tc), 0)
        db_ref[...] += _colsum(g)
        dw_ref[3:4, :] += _colsum(g * x)
        g8 = g[0:8]
        acc = g * w[3:4]
        corr = jnp.zeros((8, tc), F32)
        for j in range(1, SSM_CONV):
            xs = pltpu.roll(x, j, 0)
            dwj = _colsum(jnp.where(row >= j, g * xs, 0.0))
            dwj = dwj + _colsum(jnp.where(row8 < j, g8 * pltpu.roll(halo, j, 0), 0.0))
            dw_ref[3 - j:4 - j, :] += dwj
            gs = pltpu.roll(g, tm - j, 0)
            acc = acc + w[3 - j:4 - j] * jnp.where(row < tm - j, gs, 0.0)
            corr = corr + w[3 - j:4 - j] * jnp.where(row8 >= 8 - j, pltpu.roll(gn, 8 - j, 0), 0.0)
        scr[...] = acc
        scr[tm - 8:tm, :] += corr
        dx_ref[...] = scr[...].astype(dx_ref.dtype)

    tile = pl.BlockSpec((tm, tc), lambda jc, i: (i, jc))
    prev8 = pl.BlockSpec((8, tc), lambda jc, i: (jnp.maximum(i * hb - 1, 0), jc))
    next8 = pl.BlockSpec((8, tc), lambda jc, i: (jnp.minimum((i + 1) * hb, n * hb - 1), jc))
    return _pcall(body, name=name, grid=(c // tc, n),
                  in_specs=[tile, tile, tile, prev8, next8, next8, pl.BlockSpec((8, tc), lambda jc, i: (0, jc))],
                  out_specs=[tile, pl.BlockSpec((8, tc), lambda jc, i: (0, jc)), pl.BlockSpec((1, tc), lambda jc, i: (0, jc))],
                  out_shape=[_sds((s, c), BF16), _sds((8, c), F32), _sds((1, c), F32)],
                  scratch=[pltpu.VMEM((tm, tc), F32)],
                  sem=("parallel", "arbitrary"))(dxa, xc, xpre, xpre, dxa, xc, w8)


def _ssd_pre(dtraw, dt_bias, a_log, cst, name):
    def fn(raw, bias, alog, expand):
        tm = raw.shape[0]
        z = raw + bias
        dt = jnp.maximum(z, 0.0) + jnp.log(1.0 + jnp.exp(-jnp.abs(z)))
        lane = lax.broadcasted_iota(jnp.int32, (tm, LANES), 1)
        pos = lax.broadcasted_iota(jnp.int32, (tm, LANES), 0) & (SSM_CHUNK - 1)
        dt = jnp.where(lane < SSM_HEADS, dt, 0.0)
        c = dt * (-jnp.exp(alog))
        sh = 1
        while sh < SSM_CHUNK:
            c = c + jnp.where(pos >= sh, pltpu.roll(c, sh, 0), 0.0)
            sh *= 2
        return dt, c, _dot_split(dt, expand), _dot_split(c, expand)
    wide = SSM_HEADS * HEAD_DIM
    return _rowwise(fn, [dtraw], [dt_bias, a_log, cst["expand"]],
                    [(LANES, F32), (LANES, F32), (wide, F32), (wide, F32)], [], name=name)


def _ssd_post(dacs, ddt, dtraw, dt, dt_bias, a_log, name):
    def fn(dacs_v, ddt_v, raw, dt_v, bias, alog):
        tm = raw.shape[0]
        pos = lax.broadcasted_iota(jnp.int32, (tm, LANES), 0) & (SSM_CHUNK - 1)
        a = -jnp.exp(alog)
        c = dacs_v
        sh = 1
        while sh < SSM_CHUNK:
            c = c + jnp.where(pos + sh < SSM_CHUNK, pltpu.roll(c, tm - sh, 0), 0.0)
            sh *= 2
        draw = (ddt_v + c * a) * _sigmoid(raw + bias)
        return draw, _colsum(draw), _colsum(c * dt_v * a)
    return _rowwise(fn, [dacs, ddt, dtraw, dt], [dt_bias, a_log], [(LANES, BF16)], [(1, LANES)] * 2, name=name)


def _heads_cols(x, s):
    return x[:, :SSM_HEADS].reshape(s, SSM_GROUPS, 4).transpose(1, 0, 2)


def _heads_rows(x, s):
    return x[:, :SSM_HEADS].reshape(s // SSM_CHUNK, SSM_CHUNK, SSM_GROUPS, 4).transpose(2, 0, 3, 1)


def _expand_heads(cols, lane):
    out = cols[:, 3:4]
    for r in (2, 1, 0):
        out = jnp.where(lane < HEAD_DIM * (r + 1), cols[:, r:r + 1], out)
    return out


def _ssd_common(x, dtc, acsc, acsr):
    l = SSM_CHUNK
    lane = lax.broadcasted_iota(jnp.int32, (l, SSM_GROUP_WIDTH), 1)
    dt_e = _expand_heads(dtc, lane)
    acs_e = _expand_heads(acsc, lane)
    last = acsr[:, l - 1:l]
    lane1 = lax.broadcasted_iota(jnp.int32, (1, SSM_GROUP_WIDTH), 1)
    last_e = last[3:4, :]
    for r in (2, 1, 0):
        last_e = jnp.where(lane1 < HEAD_DIM * (r + 1), last[r:r + 1, :], last_e)
    e_e = jnp.exp(acs_e)
    dte_e = jnp.exp(last_e - acs_e)
    rowg = lax.broadcasted_iota(jnp.int32, (SSM_GROUP_WIDTH, SSM_STATE), 0)
    cd = jnp.exp(last)
    cd_mat = cd[3:4, :]
    for r in (2, 1, 0):
        cd_mat = jnp.where(rowg < HEAD_DIM * (r + 1), cd[r:r + 1, :], cd_mat)
    return lane, dt_e, e_e, dte_e, cd, cd_mat


def _ssd_fwd(xa, dtc, acsc, acsr, d_e, name):
    s = xa.shape[0]
    l, gw, ns = SSM_CHUNK, SSM_GROUP_WIDTH, SSM_STATE
    nc = s // l
    xb, bb = 2048 // gw, 2048 // ns

    def body(x_ref, b_ref, c_ref, dtc_ref, acsc_ref, acsr_ref, d_ref, y_ref, hp_ref, h_sc):
        @pl.when(pl.program_id(1) == 0)
        def _():
            h_sc[...] = jnp.zeros_like(h_sc)
        x = x_ref[...]
        bm, cm = b_ref[...].astype(BF16), c_ref[...].astype(BF16)
        acsc, acsr = acsc_ref[...], acsr_ref[...]
        lane, dt_e, e_e, dte_e, _, cd_mat = _ssd_common(x, dtc_ref[...], acsc, acsr)
        xdt = x * dt_e
        xdt16 = xdt.astype(BF16)
        cb = _dot(cm, bm, NT_DIMS)
        tril = lax.broadcasted_iota(jnp.int32, (l, l), 0) >= lax.broadcasted_iota(jnp.int32, (l, l), 1)
        yd = jnp.zeros((l, gw), F32)
        for r in range(4):
            lm = jnp.exp(jnp.where(tril, acsc[:, r:r + 1] - acsr[r:r + 1, :], -jnp.inf))
            yr = _dot((cb * lm).astype(BF16), xdt16)
            yd = jnp.where((lane >= HEAD_DIM * r) & (lane < HEAD_DIM * (r + 1)), yr, yd)
        hp = h_sc[...]
        hp_ref[...] = hp
        yoff = _dot(cm, hp.astype(BF16), NT_DIMS) * e_e
        y_ref[...] = yd + yoff + x * d_ref[...]
        st = _dot((xdt * dte_e).astype(BF16), bm, TN_DIMS)
        h_sc[...] = hp * cd_mat + st

    return _pcall(
        body, name=name, grid=(SSM_GROUPS, nc),
        in_specs=[pl.BlockSpec((l, gw), lambda g, c: (c, g)),
                  pl.BlockSpec((l, ns), lambda g, c: (c, bb + g)),
                  pl.BlockSpec((l, ns), lambda g, c: (c, bb + SSM_GROUPS + g)),
                  pl.BlockSpec((None, l, 4), lambda g, c: (g, c, 0)),
                  pl.BlockSpec((None, l, 4), lambda g, c: (g, c, 0)),
                  pl.BlockSpec((None, None, 4, l), lambda g, c: (g, c, 0, 0)),
                  pl.BlockSpec((None, 1, gw), lambda g, c: (g, 0, 0))],
        out_specs=[pl.BlockSpec((l, gw), lambda g, c: (c, g)),
                   pl.BlockSpec((None, None, gw, ns), lambda g, c: (g, c, 0, 0))],
        out_shape=[_sds((s, xb * gw), F32), _sds((SSM_GROUPS, nc, gw, ns), F32)],
        scratch=[pltpu.VMEM((gw, ns), F32)],
        sem=("parallel", "arbitrary"))(xa, xa, xa, dtc, acsc, acsr, d_e)


def _ssd_bwd(dy, xa, dtc, acsc, acsr, d_e, hprev, name):
    s = xa.shape[0]
    l, gw, ns = SSM_CHUNK, SSM_GROUP_WIDTH, SSM_STATE
    nc = s // l
    bb = 2048 // ns

    def body(dy_ref, x_ref, b_ref, c_ref, dtc_ref, acsc_ref, acsr_ref, d_ref, hp_ref,
             dx_ref, db_ref, dc_ref, dacs_ref, ddt_ref, dd_ref, dh_sc):
        @pl.when(pl.program_id(1) == 0)
        def _():
            dh_sc[...] = jnp.zeros_like(dh_sc)
            dd_ref[...] = jnp.zeros_like(dd_ref)
        dyv, x = dy_ref[...], x_ref[...]
        bm, cm = b_ref[...].astype(BF16), c_ref[...].astype(BF16)
        acsc, acsr = acsc_ref[...], acsr_ref[...]
        lane, dt_e, e_e, dte_e, cd, cd_mat = _ssd_common(x, dtc_ref[...], acsc, acsr)
        xdt = x * dt_e
        xdt16 = xdt.astype(BF16)
        dy16 = dyv.astype(BF16)
        cb = _dot(cm, bm, NT_DIMS)
        cbt = _dot(bm, cm, NT_DIMS)
        hp = hp_ref[...]
        hp16 = hp.astype(BF16)
        g = dh_sc[...]
        g16 = g.astype(BF16)
        t_all = _dot(cm, hp16, NT_DIMS)
        dt16 = (dyv * e_e).astype(BF16)
        dc = _dot(dt16, hp16)
        dhp = _dot(dt16, cm, TN_DIMS)
        yoff_term = dyv * t_all * e_e
        wv = xdt * dte_e
        dw = _dot(bm, g16, NT_DIMS)
        db = _dot(wv.astype(BF16), g16)
        dxdt = dw * dte_e
        dte_term = dw * wv
        gh = g * hp
        dh_sc[...] = g * cd_mat + dhp
        ri = lax.broadcasted_iota(jnp.int32, (l, l), 0)
        ci = lax.broadcasted_iota(jnp.int32, (l, l), 1)
        tril, triu = ri >= ci, ci >= ri
        dcb = jnp.zeros((l, l), F32)
        dcbt = jnp.zeros((l, l), F32)
        q_rows, q_cols = [], []
        for r in range(4):
            in_head = (lane >= HEAD_DIM * r) & (lane < HEAD_DIM * (r + 1))
            lm = jnp.exp(jnp.where(tril, acsc[:, r:r + 1] - acsr[r:r + 1, :], -jnp.inf))
            lmt = jnp.exp(jnp.where(triu, acsr[r:r + 1, :] - acsc[:, r:r + 1], -jnp.inf))
            mm_, mt = cb * lm, cbt * lmt
            dyr = jnp.where(in_head, dy16, jnp.zeros_like(dy16))
            dm = _dot(dyr, xdt16, NT_DIMS)
            dmt = _dot(xdt16, dyr, NT_DIMS)
            dxdt = dxdt + jnp.where(in_head, _dot(mt.astype(BF16), dy16), 0.0)
            dcb = dcb + dm * lm
            dcbt = dcbt + dmt * lmt
            q_rows.append(jnp.sum(dm * mm_, axis=1, keepdims=True))
            q_cols.append(jnp.sum(dmt * mt, axis=1, keepdims=True))
        dc = dc + _dot(dcb.astype(BF16), bm)
        db = db + _dot(dcbt.astype(BF16), cm)
        dxdt_x = dxdt * x
        dy_x = dyv * x
        rowc = lax.broadcasted_iota(jnp.int32, (l, 1), 0)
        lane128 = lax.broadcasted_iota(jnp.int32, (1, LANES), 1)
        dd_row = jnp.zeros((1, LANES), F32)
        for r in range(4):
            in_head = (lane >= HEAD_DIM * r) & (lane < HEAD_DIM * (r + 1))
            seg = lambda v: jnp.sum(jnp.where(in_head, v, 0.0), axis=1, keepdims=True)
            s1, s2, s3 = seg(yoff_term), seg(dte_term), seg(dxdt_x)
            dcd = jnp.sum(_colsum(gh[HEAD_DIM * r:HEAD_DIM * (r + 1), :]), axis=1, keepdims=True)
            last_add = _colsum(s2) + dcd * cd[r:r + 1, :]
            dacs_r = q_rows[r] - q_cols[r] + s1 - s2 + jnp.where(rowc == l - 1, last_add, 0.0)
            dacs_ref[:, r:r + 1] = dacs_r
            ddt_ref[:, r:r + 1] = s3
            dd_row = dd_row + jnp.where(lane128 == r, _colsum(seg(dy_x)), 0.0)
        dd_ref[0:1, :] += dd_row
        dx_ref[...] = dxdt * dt_e + dyv * d_ref[...]
        db_ref[...] = db
        dc_ref[...] = dc

    rc = lambda c: nc - 1 - c
    return _pcall(
        body, name=name, grid=(SSM_GROUPS, nc),
        in_specs=[pl.BlockSpec((l, gw), lambda g, c: (rc(c), g)),
                  pl.BlockSpec((l, gw), lambda g, c: (rc(c), g)),
                  pl.BlockSpec((l, ns), lambda g, c: (rc(c), bb + g)),
                  pl.BlockSpec((l, ns), lambda g, c: (rc(c), bb + SSM_GROUPS + g)),
                  pl.BlockSpec((None, l, 4), lambda g, c: (g, rc(c), 0)),
                  pl.BlockSpec((None, l, 4), lambda g, c: (g, rc(c), 0)),
                  pl.BlockSpec((None, None, 4, l), lambda g, c: (g, rc(c), 0, 0)),
                  pl.BlockSpec((None, 1, gw), lambda g, c: (g, 0, 0)),
                  pl.BlockSpec((None, None, gw, ns), lambda g, c: (g, rc(c), 0, 0))],
        out_specs=[pl.BlockSpec((l, gw), lambda g, c: (rc(c), g)),
                   pl.BlockSpec((l, ns), lambda g, c: (rc(c), g)),
                   pl.BlockSpec((l, ns), lambda g, c: (rc(c), g)),
                   pl.BlockSpec((None, l, 4), lambda g, c: (g, rc(c), 0)),
                   pl.BlockSpec((None, l, 4), lambda g, c: (g, rc(c), 0)),
                   pl.BlockSpec((None, 8, LANES), lambda g, c: (g, 0, 0))],
        out_shape=[_sds((s, 2048), F32), _sds((s, SSM_GROUPS * ns), F32), _sds((s, SSM_GROUPS * ns), F32),
                   _sds((SSM_GROUPS, s, 4), F32), _sds((SSM_GROUPS, s, 4), F32), _sds((SSM_GROUPS, 8, LANES), F32)],
        scratch=[pltpu.VMEM((gw, ns), F32)],
        sem=("parallel", "arbitrary"))(dy, xa, xa, xa, dtc, acsc, acsr, d_e, hprev)


def _ssd_constants():
    src = jnp.arange(LANES)[:, None]
    expand = (src == jnp.arange(SSM_HEADS * HEAD_DIM)[None, :] // HEAD_DIM).astype(BF16)
    seg = (jnp.arange(SSM_GROUP_WIDTH)[:, None] // HEAD_DIM == jnp.arange(LANES)[None, :]).astype(BF16)
    seg4 = (jnp.arange(4 * LANES)[:, None] // LANES == jnp.arange(LANES)[None, :]).astype(BF16)
    return dict(expand=expand, seg=seg, seg4=seg4)


def _ssm_weights(w_in, conv_w, conv_b, dt_bias, a_log, d_skip, norm_w, w_out):
    pad = ((0, 0), (0, LANES - SSM_HEADS))
    w_xbc = _group_cols(w_in[:, 2048:6144])
    w_dt = jnp.pad(w_in[:, 6144:], pad)
    return dict(
        ssm_z=w_in[:, :2048], ssm_xbc=w_xbc, ssm_dt=w_dt,
        ssm_in_pad=jnp.concatenate([w_in[:, :2048], w_xbc, w_dt], axis=1),
        ssm_out=w_out,
        conv_w8=_group_cols(jnp.pad(conv_w, ((0, 8 - SSM_CONV), (0, 0)))),
        conv_b=_group_cols(conv_b), norm_w=norm_w,
        dt_bias=jnp.pad(dt_bias, pad), a_log=jnp.pad(a_log, pad),
        d_e=jnp.repeat(d_skip.reshape(SSM_GROUPS, 4), HEAD_DIM, axis=1)[:, None, :],
        ssd_cst=_ssd_constants())


def _ssd_setup(acs_e, acsr):
    l = SSM_CHUNK
    last = acsr[:, l - 1:l]
    lane1 = lax.broadcasted_iota(jnp.int32, (1, SSM_GROUP_WIDTH), 1)
    last_e = last[3:4, :]
    for r in (2, 1, 0):
        last_e = jnp.where(lane1 < HEAD_DIM * (r + 1), last[r:r + 1, :], last_e)
    return jnp.exp(acs_e), jnp.exp(last_e - acs_e), jnp.exp(last_e)


def _head_bcast(acs_e):
    lo = lax.broadcasted_iota(jnp.int32, (acs_e.shape[0], LANES), 1) < HEAD_DIM
    out = []
    for p in range(2):
        blk = acs_e[:, p * LANES:(p + 1) * LANES]
        rolled = pltpu.roll(blk, HEAD_DIM, 1)
        out += [jnp.where(lo, blk, rolled), jnp.where(lo, rolled, blk)]
    return out


def _group_cols(a):
    lead = a.shape[:-1]
    x = a[..., :2048].reshape(lead + (SSM_GROUPS, SSM_GROUP_WIDTH))
    b = a[..., 2048:3072].reshape(lead + (SSM_GROUPS, SSM_STATE))
    c = a[..., 3072:].reshape(lead + (SSM_GROUPS, SSM_STATE))
    return jnp.concatenate([x, b, c], axis=-1).reshape(lead + (4096,))


def _ungroup_cols(a):
    lead = a.shape[:-1]
    y = a.reshape(lead + (SSM_GROUPS, SSM_GROUP_WIDTH + 2 * SSM_STATE))
    return jnp.concatenate([y[..., :256].reshape(lead + (2048,)), y[..., 256:384].reshape(lead + (1024,)),
                            y[..., 384:].reshape(lead + (1024,))], axis=-1)


def _ssd_fwd2(xa, dte, acse, acsr, d_e, name):
    s = xa.shape[0]
    l, gw, ns = SSM_CHUNK, SSM_GROUP_WIDTH, SSM_STATE
    nc = s // l

    def body(x_ref, b_ref, c_ref, dt_ref, acs_ref, acsr_ref, d_ref, y_ref, hp_ref, h_sc):
        @pl.when(pl.program_id(1) == 0)
        def _():
            h_sc[...] = jnp.zeros_like(h_sc)
        x = x_ref[...]
        bm, cm = b_ref[...].astype(BF16), c_ref[...].astype(BF16)
        acsr = acsr_ref[...]
        dt_e, acs_e = dt_ref[...], acs_ref[...]
        acs_bc = _head_bcast(acs_e)
        e_e, dte_e, cd_e = _ssd_setup(acs_e, acsr)
        lane = lax.broadcasted_iota(jnp.int32, (l, gw), 1)
        xdt = x * dt_e
        xdt16 = xdt.astype(BF16)
        cb = _dot(cm, bm, NT_DIMS)
        tril = _causal(l)
        yd = jnp.zeros((l, gw), F32)
        for r in range(4):
            lm = jnp.exp(jnp.where(tril, acs_bc[r] - acsr[r:r + 1, :], -jnp.inf))
            yr = _dot((cb * lm).astype(BF16), xdt16)
            yd = jnp.where((lane >= HEAD_DIM * r) & (lane < HEAD_DIM * (r + 1)), yr, yd)
        hp = h_sc[...]
        hp_ref[...] = hp
        y_ref[...] = yd + _dot(cm, hp.astype(BF16)) * e_e + x * d_ref[...]
        h_sc[...] = hp * cd_e + _dot(bm, (xdt * dte_e).astype(BF16), TN_DIMS)

    return _pcall(
        body, name=name, grid=(SSM_GROUPS, nc),
        in_specs=[pl.BlockSpec((l, gw), lambda g, c: (c, 2 * g)),
                  pl.BlockSpec((l, ns), lambda g, c: (c, 4 * g + 2)),
                  pl.BlockSpec((l, ns), lambda g, c: (c, 4 * g + 3)),
                  pl.BlockSpec((l, gw), lambda g, c: (c, g)),
                  pl.BlockSpec((l, gw), lambda g, c: (c, g)),
                  pl.BlockSpec((None, None, 4, l), lambda g, c: (g, c, 0, 0)),
                  pl.BlockSpec((None, 1, gw), lambda g, c: (g, 0, 0))],
        out_specs=[pl.BlockSpec((l, gw), lambda g, c: (c, g)),
                   pl.BlockSpec((None, None, ns, gw), lambda g, c: (g, c, 0, 0))],
        out_shape=[_sds((s, 2048), F32), _sds((SSM_GROUPS, nc, ns, gw), F32)],
        scratch=[pltpu.VMEM((ns, gw), F32)],
        sem=("parallel", "arbitrary"))(xa, xa, xa, dte, acse, acsr, d_e)


def _ssd_bwd2(dy, xa, dte, acse, acsr, d_e, hprev, cst, name):
    s = xa.shape[0]
    l, gw, ns = SSM_CHUNK, SSM_GROUP_WIDTH, SSM_STATE
    nc = s // l

    def body(dy_ref, x_ref, b_ref, c_ref, dt_ref, acs_ref, acsr_ref, d_ref, hp_ref,
             seg_ref, seg4_ref, dxa_ref, dacs_ref, ddt_ref, dd_ref, dh_sc):
        @pl.when(pl.program_id(1) == 0)
        def _():
            dh_sc[...] = jnp.zeros_like(dh_sc)
            dd_ref[...] = jnp.zeros_like(dd_ref)
        dyv, x = dy_ref[...], x_ref[...]
        bm, cm = b_ref[...].astype(BF16), c_ref[...].astype(BF16)
        acsr = acsr_ref[...]
        dt_e, acs_e = dt_ref[...], acs_ref[...]
        acs_bc = _head_bcast(acs_e)
        e_e, dte_e, cd_e = _ssd_setup(acs_e, acsr)
        seg, seg4 = seg_ref[...], seg4_ref[...]
        lane = lax.broadcasted_iota(jnp.int32, (l, gw), 1)
        xdt = x * dt_e
        xdt16 = xdt.astype(BF16)
        dy16 = dyv.astype(BF16)
        cb = _dot(cm, bm, NT_DIMS)
        cbt = _dot(bm, cm, NT_DIMS)
        hp = hp_ref[...]
        hp16 = hp.astype(BF16)
        g = dh_sc[...]
        g16 = g.astype(BF16)
        t_all = _dot(cm, hp16)
        dt16 = (dyv * e_e).astype(BF16)
        dc = _dot(dt16, hp16, NT_DIMS)
        dhp = _dot(cm, dt16, TN_DIMS)
        wv = xdt * dte_e
        dw = _dot(bm, g16)
        db = _dot(wv.astype(BF16), g16, NT_DIMS)
        dxdt = dw * dte_e
        acs_term = dyv * t_all * e_e - dw * wv
        last_term = _colsum(dw * wv) + _colsum(g * hp) * cd_e
        dh_sc[...] = g * cd_e + dhp
        tril, triu = _causal(l), _causal(l, transposed=True)
        dcb = jnp.zeros((l, l), F32)
        dcbt = jnp.zeros((l, l), F32)
        qd = []
        for r in range(4):
            in_head = (lane >= HEAD_DIM * r) & (lane < HEAD_DIM * (r + 1))
            a_col = acs_bc[r]
            lm = jnp.exp(jnp.where(tril, a_col - acsr[r:r + 1, :], -jnp.inf))
            lmt = jnp.exp(jnp.where(triu, acsr[r:r + 1, :] - a_col, -jnp.inf))
            mm_, mt = cb * lm, cbt * lmt
            dyr = jnp.where(in_head, dy16, jnp.zeros_like(dy16))
            dm = _dot(dyr, xdt16, NT_DIMS)
            dmt = _dot(xdt16, dyr, NT_DIMS)
            dxdt = dxdt + jnp.where(in_head, _dot(mt.astype(BF16), dy16), 0.0)
            dcb = dcb + dm * lm
            dcbt = dcbt + dmt * lmt
            qd.append(dm * mm_ - dmt * mt)
        dc = dc + _dot(dcb.astype(BF16), bm)
        db = db + _dot(dcbt.astype(BF16), cm)
        rowl = lax.broadcasted_iota(jnp.int32, (l, LANES), 0)
        row8 = lax.broadcasted_iota(jnp.int32, (8, gw), 0)
        small = _dot_split(jnp.where(row8 == 0, last_term, jnp.where(row8 == 1, _colsum(dyv * x), 0.0)), seg, passes=2)
        big = _dot_split(jnp.concatenate([acs_term, dxdt * x], axis=0), seg, passes=2)
        dacs = (big[0:l] + _dot_split(jnp.concatenate(qd, axis=1), seg4, passes=2)
                + jnp.where(rowl == l - 1, small[0:1, :], 0.0))
        dacs_ref[...] = dacs[:, 0:4]
        ddt_ref[...] = big[l:2 * l, 0:4]
        dd_ref[0:1, :] += small[1:2, :]
        dxa_ref[...] = jnp.concatenate([dxdt * dt_e + dyv * d_ref[...], db, dc], axis=1)

    rc = lambda c: nc - 1 - c
    return _pcall(
        body, name=name, grid=(SSM_GROUPS, nc),
        in_specs=[pl.BlockSpec((l, gw), lambda g, c: (rc(c), g)),
                  pl.BlockSpec((l, gw), lambda g, c: (rc(c), 2 * g)),
                  pl.BlockSpec((l, ns), lambda g, c: (rc(c), 4 * g + 2)),
                  pl.BlockSpec((l, ns), lambda g, c: (rc(c), 4 * g + 3)),
                  pl.BlockSpec((l, gw), lambda g, c: (rc(c), g)),
                  pl.BlockSpec((l, gw), lambda g, c: (rc(c), g)),
                  pl.BlockSpec((None, None, 4, l), lambda g, c: (g, rc(c), 0, 0)),
                  pl.BlockSpec((None, 1, gw), lambda g, c: (g, 0, 0)),
                  pl.BlockSpec((None, None, ns, gw), lambda g, c: (g, rc(c), 0, 0)),
                  pl.BlockSpec((gw, LANES), lambda g, c: (0, 0)),
                  pl.BlockSpec((4 * LANES, LANES), lambda g, c: (0, 0))],
        out_specs=[pl.BlockSpec((l, gw + 2 * ns), lambda g, c: (rc(c), g)),
                   pl.BlockSpec((None, l, 4), lambda g, c: (g, rc(c), 0)),
                   pl.BlockSpec((None, l, 4), lambda g, c: (g, rc(c), 0)),
                   pl.BlockSpec((None, 8, LANES), lambda g, c: (g, 0, 0))],
        out_shape=[_sds((s, 4096), F32), _sds((SSM_GROUPS, s, 4), F32), _sds((SSM_GROUPS, s, 4), F32),
                   _sds((SSM_GROUPS, 8, LANES), F32)],
        scratch=[pltpu.VMEM((ns, gw), F32)],
        sem=("parallel", "arbitrary"))(dy, xa, xa, xa, dte, acse, acsr, d_e, hprev, cst["seg"], cst["seg4"])


def _gate_norm(y, z, nw, name):
    c = y.shape[1]

    def fn(yv, zv, w):
        outs = []
        for k in range(c // SSM_GROUP_WIDTH):
            sl = slice(k * SSM_GROUP_WIDTH, (k + 1) * SSM_GROUP_WIDTH)
            yg = yv[:, sl] * _silu(zv[:, sl])
            rinv = lax.rsqrt(jnp.mean(yg * yg, axis=-1, keepdims=True) + RMS_EPS)
            outs.append(yg * rinv * w[:, sl])
        return (jnp.concatenate(outs, axis=1),)
    return _rowwise(fn, [y, z], [nw], [(c, BF16)], [], name=name)[0]


def _gate_norm_bwd(dyn, y, z, nw, name):
    c = y.shape[1]

    def fn(dv, yv, zv, w):
        dys, dzs, dws = [], [], []
        for k in range(c // SSM_GROUP_WIDTH):
            sl = slice(k * SSM_GROUP_WIDTH, (k + 1) * SSM_GROUP_WIDTH)
            ys, zs, ds = yv[:, sl], zv[:, sl], dv[:, sl]
            sz = _silu(zs)
            yg = ys * sz
            rinv = lax.rsqrt(jnp.mean(yg * yg, axis=-1, keepdims=True) + RMS_EPS)
            nrm = yg * rinv
            dn = ds * w[:, sl]
            dyg = rinv * (dn - nrm * jnp.mean(dn * nrm, axis=-1, keepdims=True))
            dys.append(dyg * sz)
            dzs.append(dyg * ys * _dsilu(zs))
            dws.append(_colsum(ds * nrm))
        return jnp.concatenate(dys, axis=1), jnp.concatenate(dzs, axis=1), jnp.concatenate(dws, axis=1)
    return _rowwise(fn, [dyn, y, z], [nw], [(c, F32), (c, BF16)], [(1, c)], name=name, tm=128)


def _ssd_forward(u, w):
    s = u.shape[0]
    z = _mm(u, w["ssm_z"], "nn", [F32], name="ssm_zproj")
    xpre = _mm(u, w["ssm_xbc"], "nn", [F32], name="ssm_xproj")
    dtraw = _mm(u, w["ssm_dt"], "nn", [F32], name="ssm_dtproj")
    xc, xa = _conv_fwd(xpre, w["conv_w8"], w["conv_b"], "ssm_conv")
    dt, acs, dte, acse = _ssd_pre(dtraw, w["dt_bias"], w["a_log"], w["ssd_cst"], "ssm_pre")
    acsr = _heads_rows(acs, s)
    y, hprev = _ssd_fwd2(xa, dte, acse, acsr, w["d_e"], "ssm_scan")
    yn = _gate_norm(y, z, w["norm_w"], "ssm_gate_norm")
    out = _mm(yn, w["ssm_out"], "nn", [F32], name="ssm_oproj")
    return out, dict(u=u, z=z, xpre=xpre, xc=xc, xa=xa, dtraw=dtraw, dt=dt, dte=dte, acse=acse,
                     acsr=acsr, y=y, hprev=hprev, yn=yn)


def _ssd_backward(dy, sv, w):
    s = dy.shape[0]
    dyn = _mm(dy, w["ssm_out"], "nt", [F32], name="ssm_dyn")
    g_wout = _mm(sv["yn"], dy, "tn", [F32], name="ssm_gwout")
    dys, dz, dnw = _gate_norm_bwd(dyn, sv["y"], sv["z"], w["norm_w"], "ssm_gate_norm_bwd")
    dxa, dacs_c, ddt_c, dd = _ssd_bwd2(dys, sv["xa"], sv["dte"], sv["acse"], sv["acsr"], w["d_e"], sv["hprev"],
                                       w["ssd_cst"], "ssm_scan_bwd")
    pad = ((0, 0), (0, LANES - SSM_HEADS))
    dacs = jnp.pad(dacs_c.transpose(1, 0, 2).reshape(s, SSM_HEADS), pad)
    ddt = jnp.pad(ddt_c.transpose(1, 0, 2).reshape(s, SSM_HEADS), pad)
    draw, dbias, dalog = _ssd_post(dacs, ddt, sv["dtraw"], sv["dt"], w["dt_bias"], w["a_log"], "ssm_post")
    dxpre, dcw, dcb = _conv_bwd(dxa, sv["xc"], sv["xpre"], w["conv_w8"], "ssm_conv_bwd")
    dproj = jnp.concatenate([dz, dxpre, draw], axis=1)
    du = _mm(dproj, w["ssm_in_pad"], "nt", [F32], name="ssm_du", tk=896)
    g_win = _mm(sv["u"], dproj, "tn", [F32], name="ssm_gwin", tn=896)
    g_win = jnp.concatenate([g_win[:, :2048], _ungroup_cols(g_win[:, 2048:6144]), g_win[:, 6144:6144 + SSM_HEADS]],
                            axis=1)
    return du, dict(ssm_w_in=g_win, ssm_w_out=g_wout, ssm_conv_w=_ungroup_cols(dcw[:SSM_CONV]),
                    ssm_conv_b=_ungroup_cols(dcb), ssm_norm_w=dnw, ssm_dt_bias=dbias[:, :SSM_HEADS],
                    ssm_a_log=dalog[:, :SSM_HEADS], ssm_d=dd[:, 0, :4].reshape(1, SSM_HEADS))


def _mlp_forward(u2, w1, w2, tag):
    def epi(acc):
        hr = jnp.maximum(acc, 0.0)
        return hr, hr * hr
    hr, a = _mm(u2, w1, "nn", [BF16, BF16], name=tag + "_mlp_up", epi=epi)
    y2 = _mm(a, w2, "nn", [F32], name=tag + "_mlp_down")
    return y2, hr, a


def _mlp_backward(dy2, u2, hr, a, w1, w2, tag):
    dh = _mm(dy2, w2, "nt", [BF16], name=tag + "_mlp_dh", extra=(hr,),
             epi=lambda acc, h: (acc * (2.0 * h.astype(F32)),))
    g_w2 = _mm(a, dy2, "tn", [F32], name=tag + "_mlp_gw2")
    g_w1 = _mm(u2, dh, "tn", [F32], name=tag + "_mlp_gw1")
    du2 = _mm(dh, w1, "nt", [F32], name=tag + "_mlp_du")
    return du2, g_w1, g_w2


def _ada_forward(c16, ada_w, ada_b_cols, name):
    nl, d, cols = ada_w.shape
    tn = 512

    def body(c_ref, w_ref, b_ref, o_ref):
        cond = _silu(c_ref[...]).astype(BF16)
        o_ref[...] = _dot(cond, w_ref[...].astype(BF16)) + b_ref[...]

    return _pcall(body, name=name, grid=(nl, cols // tn),
                  in_specs=[pl.BlockSpec((16, d), lambda i, j: (0, 0)),
                            pl.BlockSpec((None, d, tn), lambda i, j: (i, 0, j)),
                            pl.BlockSpec((None, 1, tn), lambda i, j: (i, 0, j))],
                  out_specs=pl.BlockSpec((None, 16, tn), lambda i, j: (i, 0, j)),
                  out_shape=_sds((nl, 16, cols), F32), sem=("parallel", "parallel"))(c16, ada_w, ada_b_cols)


def _ada_backward(c_t, dmod_cols, name):
    d, nb = c_t.shape
    nl, _, cols = dmod_cols.shape
    tn = 512

    def body(c_ref, dm_ref, o_ref):
        cond = _silu(c_ref[...])
        dm = dm_ref[...]
        acc = cond[:, 0:1] * dm[0:1, :]
        for b in range(1, nb):
            acc = acc + cond[:, b:b + 1] * dm[b:b + 1, :]
        o_ref[...] = acc

    return _pcall(body, name=name, grid=(nl, cols // tn),
                  in_specs=[pl.BlockSpec((d, nb), lambda i, j: (0, 0)),
                            pl.BlockSpec((None, nb, tn), lambda i, j: (i, 0, j))],
                  out_specs=pl.BlockSpec((None, d, tn), lambda i, j: (i, 0, j)),
                  out_shape=_sds((nl, d, cols), F32), sem=("parallel", "parallel"))(c_t, dmod_cols)


def _adamw(w, g, m, v, name):
    rows, cols = w.shape
    tm = rows
    for cand in (256, 128, 64, 32, 16, 8):
        if rows % cand == 0 and rows > cand:
            tm = cand
            break
    c1 = 1.0 / (1.0 - ADAM_B1 ** ADAM_STEP)
    c2 = 1.0 / (1.0 - ADAM_B2 ** ADAM_STEP)

    def fn(wv, gv, mv, vv):
        mn = ADAM_B1 * mv + (1.0 - ADAM_B1) * gv
        vn = ADAM_B2 * vv + (1.0 - ADAM_B2) * (gv * gv)
        delta = -ADAM_LR * ((mn * c1) / (jnp.sqrt(vn * c2) + ADAM_EPS) + ADAM_WD * wv)
        return delta, mn, vn
    return _rowwise(fn, [w, g, m, v], [], [(cols, F32)] * 3, [], name=name, tm=tm)


def _my_pos():
    return lax.axis_index("x"), lax.axis_index("y"), lax.axis_index("c")


def _allgather8(x, name):
    r, c = x.shape

    def body(x_ref, out_ref, send_sems, recv_sems, local_sem):
        mx, my, mc = _my_pos()
        me = 4 * mx + 2 * my + mc
        mine = pltpu.make_async_copy(x_ref, out_ref.at[me], local_sem)
        mine.start()
        copies = []
        for k in range(1, 8):
            fx, fy, fc = (k >> 2) & 1, (k >> 1) & 1, k & 1
            px = 1 - mx if fx else mx
            py = 1 - my if fy else my
            pc = 1 - mc if fc else mc
            peer = 4 * px + 2 * py + pc
            send = pltpu.make_async_remote_copy(src_ref=x_ref, dst_ref=out_ref.at[me], send_sem=send_sems.at[k - 1],
                                                recv_sem=recv_sems.at[k - 1], device_id=(px, py, pc),
                                                device_id_type=MESH)
            send.start()
            recv = pltpu.make_async_remote_copy(src_ref=x_ref, dst_ref=out_ref.at[peer], send_sem=send_sems.at[k - 1],
                                                recv_sem=recv_sems.at[k - 1], device_id=(px, py, pc),
                                                device_id_type=MESH)
            copies.append((send, recv))
        for send, recv in copies:
            recv.wait_recv()
        for send, recv in copies:
            send.wait_send()
        mine.wait()

    vm = pl.BlockSpec(memory_space=pltpu.VMEM)
    return _pcall(body, name=name, in_specs=[vm], out_specs=vm, out_shape=_sds((8, r, c), x.dtype),
                  scratch=[pltpu.SemaphoreType.DMA((7,)), pltpu.SemaphoreType.DMA((7,)), pltpu.SemaphoreType.DMA])(x)


def _chip_flips(mx, my):
    out = []
    for fx, fy in ((1, 0), (0, 1), (1, 1)):
        px = 1 - mx if fx else mx
        py = 1 - my if fy else my
        out.append((px, py, 2 * px + py))
    return out


def _gather_chips(shard2, name):
    _, h, c = shard2.shape

    def body(x_ref, out_ref, send_sems, recv_sems):
        mx, my, mc = _my_pos()
        oc = 1 - mc
        mk = 2 * mx + my
        flips = _chip_flips(mx, my)

        def copy(k, src, dst, to):
            return pltpu.make_async_remote_copy(src_ref=src, dst_ref=dst, send_sem=send_sems.at[k],
                                                recv_sem=recv_sems.at[k], device_id=to, device_id_type=MESH)

        first = [copy(j, x_ref.at[mc], out_ref.at[mk, mc], (px, py, mc)) for j, (px, py, pk) in enumerate(flips)]
        for cp in first:
            cp.start()
        passed = []
        for j, (px, py, pk) in enumerate(flips):
            copy(j, x_ref.at[mc], out_ref.at[pk, mc], (px, py, mc)).wait_recv()
            fw = copy(3 + j, out_ref.at[pk, mc], out_ref.at[pk, mc], (mx, my, oc))
            fw.start()
            passed.append(fw)
        for j, (px, py, pk) in enumerate(flips):
            copy(3 + j, out_ref.at[pk, oc], out_ref.at[pk, oc], (mx, my, oc)).wait_recv()
        for cp in first + passed:
            cp.wait_send()

    return _pcall(body, name=name, in_specs=[HBM_SPEC], out_specs=HBM_SPEC, out_shape=_sds((4, 2, h, c), shard2.dtype),
                  scratch=[pltpu.SemaphoreType.DMA((6,)), pltpu.SemaphoreType.DMA((6,))])(shard2)


def _pair_exchange(g4, name):
    n, _, h, c = g4.shape

    def body(g_ref, out_ref, send_sem, recv_sem):
        mx, my, mc = _my_pos()
        oc = 1 - mc
        copies = []
        for k in range(n):
            cp = pltpu.make_async_remote_copy(src_ref=g_ref.at[k, oc], dst_ref=out_ref.at[k], send_sem=send_sem.at[k],
                                              recv_sem=recv_sem.at[k], device_id=(mx, my, oc), device_id_type=MESH)
            cp.start()
            copies.append(cp)
        for cp in copies:
            cp.wait_recv()
        for cp in copies:
            cp.wait_send()

    return _pcall(body, name=name, in_specs=[HBM_SPEC], out_specs=HBM_SPEC, out_shape=_sds((n, h, c), g4.dtype),
                  scratch=[pltpu.SemaphoreType.DMA((n,)), pltpu.SemaphoreType.DMA((n,))])(g4)


def _pair_add(g4, recv, core, name):
    n, _, h, c = g4.shape
    tm = _row_tile(h)

    def body(core_ref, a_ref, b_ref, o_ref, o16_ref):
        acc = a_ref[...] + b_ref[...]
        o_ref[...] = acc
        o16_ref[...] = acc.astype(BF16)

    out_spec = pl.BlockSpec((None, tm, c), lambda k, i, cr: (k, i, 0))
    return _pcall(body, name=name, grid=(n, h // tm), prefetch=1,
                  in_specs=[pl.BlockSpec((None, None, tm, c), lambda k, i, cr: (k, cr[0], i, 0)), out_spec],
                  out_specs=[out_spec, out_spec], out_shape=[_sds((n, h, c), F32), _sds((n, h, c), BF16)],
                  sem=("parallel", "parallel"))(core, g4, recv)


def _chip_exchange(p, name):
    n, h, c = p.shape

    def body(p_ref, out_ref, send_sems, recv_sems):
        mx, my, mc = _my_pos()
        copies = []
        for j, (px, py, pk) in enumerate(_chip_flips(mx, my)):
            cp = pltpu.make_async_remote_copy(src_ref=p_ref.at[pk], dst_ref=out_ref.at[j], send_sem=send_sems.at[j],
                                              recv_sem=recv_sems.at[j], device_id=(px, py, mc), device_id_type=MESH)
            cp.start()
            copies.append(cp)
        for cp in copies:
            cp.wait_recv()
        for cp in copies:
            cp.wait_send()

    return _pcall(body, name=name, in_specs=[HBM_SPEC], out_specs=HBM_SPEC, out_shape=_sds((3, h, c), p.dtype),
                  scratch=[pltpu.SemaphoreType.DMA((3,)), pltpu.SemaphoreType.DMA((3,))])(p)


def _chip_sum(p, slots, chip, name):
    _, h, c = p.shape
    tm = _row_tile(h)

    def body(chip_ref, p_ref, q_ref, o_ref):
        o_ref[...] = ((p_ref[...] + q_ref[0].astype(F32)) + q_ref[1].astype(F32)) + q_ref[2].astype(F32)

    return _pcall(body, name=name, grid=(h // tm,), prefetch=1,
                  in_specs=[pl.BlockSpec((None, tm, c), lambda i, ch: (ch[0], i, 0)),
                            pl.BlockSpec((3, tm, c), lambda i, ch: (0, i, 0))],
                  out_specs=pl.BlockSpec((tm, c), lambda i, ch: (i, 0)),
                  out_shape=_sds((h, c), F32), sem=("parallel",))(chip, p, slots)


def _sum_slots(q, name):
    n, h, c = q.shape
    tm = _row_tile(h)

    def body(q_ref, o_ref):
        acc = q_ref[0]
        for k in range(1, n):
            acc = acc + q_ref[k]
        o_ref[...] = acc

    return _pcall(body, name=name, grid=(h // tm,),
                  in_specs=[pl.BlockSpec((n, tm, c), lambda i: (0, i, 0))],
                  out_specs=pl.BlockSpec((tm, c), lambda i: (i, 0)),
                  out_shape=_sds((h, c), F32), sem=("parallel",))(q)


def _pair_share(f, name):
    h, c = f.shape

    def body(f_ref, out_ref, send_sem, recv_sem):
        mx, my, mc = _my_pos()
        cp = pltpu.make_async_remote_copy(src_ref=f_ref, dst_ref=out_ref, send_sem=send_sem, recv_sem=recv_sem,
                                          device_id=(mx, my, 1 - mc), device_id_type=MESH)
        cp.start()
        cp.wait_recv()
        cp.wait_send()

    return _pcall(body, name=name, in_specs=[HBM_SPEC], out_specs=HBM_SPEC, out_shape=_sds((h, c), f.dtype),
                  scratch=[pltpu.SemaphoreType.DMA, pltpu.SemaphoreType.DMA])(f)


BIG = ("mlp_w1", "mlp_w2", "fox_w_in", "fox_w_o", "ssm_w_in", "ssm_w_out")
SMALL_SHARDED = ("ssm_conv_w", "ssm_conv_b", "ssm_norm_w")
PACK_COLS = 1024


def _pack_rows(parts, rows_multiple, dtype):
    flat = jnp.concatenate([p.reshape(-1).astype(dtype) for p in parts])
    unit = rows_multiple * PACK_COLS
    total = -(-flat.shape[0] // unit) * unit
    flat = jnp.pad(flat, (0, total - flat.shape[0]))
    return flat.reshape(total // PACK_COLS, PACK_COLS)


def _unpack(flat, shapes):
    out, off = [], 0
    for sh in shapes:
        n = 1
        for d_ in sh:
            n *= d_
        out.append(flat[off:off + n].reshape(sh))
        off += n
    return out


PIECE_ROWS = 16


def _piece_rows(shape):
    n = 1
    for d_ in shape:
        n *= d_
    rows = -(-n // PACK_COLS)
    return n, -(-rows // PIECE_ROWS) * PIECE_ROWS


def _pack2d(parts, rows_multiple, dtype):
    blocks = []
    for p in parts:
        n, rows = _piece_rows(p.shape)
        a = p.astype(dtype)
        if p.shape[-1] != PACK_COLS or n % PACK_COLS:
            a = jnp.pad(a.reshape(-1), (0, -n % PACK_COLS))
        a = a.reshape(-1, PACK_COLS)
        blocks.append(jnp.pad(a, ((0, rows - a.shape[0]), (0, 0))))
    total = sum(b.shape[0] for b in blocks)
    pad = -total % rows_multiple
    if pad:
        blocks.append(jnp.zeros((pad, PACK_COLS), dtype))
    return jnp.concatenate(blocks, axis=0)


def _unpack2d(buf, shapes):
    out, off = [], 0
    for sh in shapes:
        n, rows = _piece_rows(sh)
        piece = buf[off:off + rows]
        if sh[-1] == PACK_COLS and n % PACK_COLS == 0:
            out.append(piece[:n // PACK_COLS].reshape(sh))
        else:
            out.append(piece.reshape(-1)[:n].reshape(sh))
        off += rows
    return out


def _row_tile(h, cap=512):
    for step in (16, 8):
        best = 0
        for cand in range(step, cap + 1, step):
            if h % cand == 0:
                best = cand
        if best:
            return best
    return h


def _chip_slice(full, axis, k, width):
    idx = [slice(None)] * full.ndim
    idx[axis] = slice(k * width, (k + 1) * width)
    return full[tuple(idx)]


SHARD_AXIS = dict(mlp_w1=2, mlp_w2=1, fox_w_in=2, fox_w_o=1, ssm_w_in=2, ssm_w_out=1, ssm_conv_w=2,
                  ssm_conv_b=1, ssm_norm_w=1, ada_w=2)


def kernel(x, c, ada_w, ada_b, ln_mix_g, ln_mix_b, ln_mlp_g, ln_mlp_b, mlp_w1, mlp_w2, fox_w_in, fox_b_f, fox_w_o, ssm_w_in, ssm_conv_w, ssm_conv_b, ssm_dt_bias, ssm_a_log, ssm_d, ssm_norm_w, ssm_w_out, loss_target, m_ada_w, m_ada_b, m_ln_mix_g, m_ln_mix_b, m_ln_mlp_g, m_ln_mlp_b, m_mlp_w1, m_mlp_w2, m_fox_w_in, m_fox_b_f, m_fox_w_o, m_ssm_w_in, m_ssm_conv_w, m_ssm_conv_b, m_ssm_dt_bias, m_ssm_a_log, m_ssm_d, m_ssm_norm_w, m_ssm_w_out, v_ada_w, v_ada_b, v_ln_mix_g, v_ln_mix_b, v_ln_mlp_g, v_ln_mlp_b, v_mlp_w1, v_mlp_w2, v_fox_w_in, v_fox_b_f, v_fox_w_o, v_ssm_w_in, v_ssm_conv_w, v_ssm_conv_b, v_ssm_dt_bias, v_ssm_a_log, v_ssm_d, v_ssm_norm_w, v_ssm_w_out):
    names = ("ada_w", "ada_b", "ln_mix_g", "ln_mix_b", "ln_mlp_g", "ln_mlp_b", "mlp_w1", "mlp_w2", "fox_w_in",
             "fox_b_f", "fox_w_o", "ssm_w_in", "ssm_conv_w", "ssm_conv_b", "ssm_dt_bias", "ssm_a_log", "ssm_d",
             "ssm_norm_w", "ssm_w_out")
    weights = dict(zip(names, (ada_w, ada_b, ln_mix_g, ln_mix_b, ln_mlp_g, ln_mlp_b, mlp_w1, mlp_w2, fox_w_in,
                               fox_b_f, fox_w_o, ssm_w_in, ssm_conv_w, ssm_conv_b, ssm_dt_bias, ssm_a_log, ssm_d,
                               ssm_norm_w, ssm_w_out)))
    m_in = dict(zip(names, (m_ada_w, m_ada_b, m_ln_mix_g, m_ln_mix_b, m_ln_mlp_g, m_ln_mlp_b, m_mlp_w1, m_mlp_w2,
                            m_fox_w_in, m_fox_b_f, m_fox_w_o, m_ssm_w_in, m_ssm_conv_w, m_ssm_conv_b, m_ssm_dt_bias,
                            m_ssm_a_log, m_ssm_d, m_ssm_norm_w, m_ssm_w_out)))
    v_in = dict(zip(names, (v_ada_w, v_ada_b, v_ln_mix_g, v_ln_mix_b, v_ln_mlp_g, v_ln_mlp_b, v_mlp_w1, v_mlp_w2,
                            v_fox_w_in, v_fox_b_f, v_fox_w_o, v_ssm_w_in, v_ssm_conv_w, v_ssm_conv_b, v_ssm_dt_bias,
                            v_ssm_a_log, v_ssm_d, v_ssm_norm_w, v_ssm_w_out)))

    mx, my, mc = _my_pos()
    chip = 2 * mx + my
    me = 4 * mx + 2 * my + mc
    x0 = x[0]
    target = loss_target[0]
    s, d = x0.shape
    n_qkv = 3 * FOX_HEADS * HEAD_DIM

    big_shapes = [weights[n].shape for n in BIG]
    packed = _pack2d([weights[n] for n in BIG], 32, BF16)
    gathered = _gather_chips(packed.reshape(2, packed.shape[0] // 2, PACK_COLS), "gather_weights")
    gathered = gathered.reshape(4, packed.shape[0], PACK_COLS)
    per_chip = [_unpack2d(jnp.where(chip == k, packed, gathered[k]), big_shapes) for k in range(4)]
    full = {n: jnp.concatenate([per_chip[k][i] for k in range(4)], axis=SHARD_AXIS[n]) for i, n in enumerate(BIG)}

    small_shapes = [weights[n].shape for n in SMALL_SHARDED]
    small_packed = _pack_rows([weights[n] for n in SMALL_SHARDED] + [c], 8, F32).reshape(-1, LANES)
    small_all = _allgather8(small_packed, "gather_small")
    small_chip = [_unpack(small_all[2 * k].reshape(-1), small_shapes) for k in range(4)]
    small_full = {n: jnp.concatenate([small_chip[k][i] for k in range(4)], axis=SHARD_AXIS[n])
                  for i, n in enumerate(SMALL_SHARDED)}
    n_small = sum(weights[n].size for n in SMALL_SHARDED)
    c_all = small_all.reshape(8, -1)[:, n_small:n_small + d]

    cols = ada_w.shape[2]
    ada_b_cols = lax.dynamic_slice_in_dim(ada_b, chip * cols, cols, axis=1)[:, None, :]
    c16 = jnp.pad(c_all, ((0, 8), (0, 0)))
    mod_part = _ada_forward(c16, ada_w, ada_b_cols, "ada_fwd")[:, :8, :]
    mod_all = _allgather8(mod_part.reshape(-1, LANES), "gather_mod").reshape(8, DEPTH, 8, cols)
    mod_mine = jnp.stack([lax.dynamic_index_in_dim(mod_all[2 * k], me, axis=1, keepdims=False) for k in range(4)], axis=1)
    mods = [jnp.pad(mod_mine[i].reshape(6, d), ((0, 2), (0, 0))) for i in range(DEPTH)]

    w = dict(
        fox_qkv=full["fox_w_in"][0][:, :n_qkv],
        fox_f=jnp.pad(full["fox_w_in"][0][:, n_qkv:], ((0, 0), (0, LANES - FOX_HEADS))),
        fox_in_pad=jnp.pad(full["fox_w_in"][0], ((0, 0), (0, LANES - FOX_HEADS))),
        fox_o=full["fox_w_o"][0],
        fox_bf=jnp.pad(fox_b_f, ((0, 0), (0, LANES - FOX_HEADS))),
        head_ind=(jnp.arange(d)[:, None] // HEAD_DIM == jnp.arange(LANES)[None, :]).astype(BF16),
    )
    w.update(_ssm_weights(full["ssm_w_in"][0], small_full["ssm_conv_w"][0], small_full["ssm_conv_b"], ssm_dt_bias,
                          ssm_a_log, ssm_d, small_full["ssm_norm_w"], full["ssm_w_out"][0]))
    mixers = ((_fox_forward, _fox_backward), (_ssd_forward, _ssd_backward))

    saved = []
    xin = x0
    u = _modulate_in(x0, mods[0], "l0_mod_in")
    for i in range(DEPTH):
        tag = "l%d" % i
        y, sv = mixers[i % 2][0](u, w)
        r, x1, u2 = _res_ln_mod(xin, y, mods[i], ln_mix_g[i:i + 1], ln_mix_b[i:i + 1], tag + "_res_ln1")
        y2, hr, a = _mlp_forward(u2, full["mlp_w1"][i], full["mlp_w2"][i], tag)
        if i + 1 < DEPTH:
            r2, xin, u = _res_ln_next(x1, y2, mods[i], ln_mlp_g[i:i + 1], ln_mlp_b[i:i + 1], mods[i + 1],
                                      tag + "_res_ln2")
        else:
            r2 = _rowwise(lambda xv, yv, m: (ALPHA * xv + (1.0 + m[5:6]) * yv,), [x1, y2], [mods[i]], [(d, F32)], [],
                          name=tag + "_res2")[0]
        saved.append(dict(y=y, r=r, u2=u2, hr=hr, a=a, y2=y2, r2=r2, mix=sv))

    grads = {}
    dmod_parts = [dict() for _ in range(DEPTH)]
    ln_grads = {n: [None] * DEPTH for n in ("ln_mix_g", "ln_mix_b", "ln_mlp_g", "ln_mlp_b")}
    g_w1, g_w2 = [None] * DEPTH, [None] * DEPTH
    du = dres0 = None
    for i in reversed(range(DEPTH)):
        tag = "l%d" % i
        sv = saved[i]
        if i + 1 == DEPTH:
            dy2, dres, sq, dg2, db2, dgm = _loss_ln2_bwd(sv["r2"], sv["y2"], target, mods[i], ln_mlp_g[i:i + 1],
                                                         ln_mlp_b[i:i + 1], "loss_ln2_bwd")
            loss = lax.psum(0.5 * jnp.sum(sq) / d, ("x", "y", "c"))
        else:
            dy2, dres, dsca, dsha, dg2, db2, dgm = _mod_in_ln2_bwd(du, dres0, sv["r2"], sv["y2"], mods[i],
                                                                   ln_mlp_g[i:i + 1], ln_mlp_b[i:i + 1], mods[i + 1],
                                                                   tag + "_ln2_bwd")
            dmod_parts[i + 1].update(sc_a=dsca, sh_a=dsha)
        du2, g_w1[i], g_w2[i] = _mlp_backward(dy2, sv["u2"], sv["hr"], sv["a"], full["mlp_w1"][i], full["mlp_w2"][i], tag)
        dy, dres0, dscm, dshm, dg1, db1, dga = _ln1_bwd(du2, dres, sv["r"], sv["y"], mods[i], ln_mix_g[i:i + 1],
                                                        ln_mix_b[i:i + 1], tag + "_ln1_bwd")
        du, mg = mixers[i % 2][1](dy, sv["mix"], w)
        grads.update(mg)
        dmod_parts[i].update(g_a=dga, sh_m=dshm, sc_m=dscm, g_m=dgm)
        ln_grads["ln_mix_g"][i], ln_grads["ln_mix_b"][i] = dg1, db1
        ln_grads["ln_mlp_g"][i], ln_grads["ln_mlp_b"][i] = dg2, db2
    dx, dsca, dsha = _mod_in_bwd(du, dres0, x0, mods[0], "l0_mod_in_bwd")
    dmod_parts[0].update(sc_a=dsca, sh_a=dsha)
    dmods = [jnp.concatenate([p["sh_a"], p["sc_a"], p["g_a"], p["sh_m"], p["sc_m"], p["g_m"]], axis=1)
             for p in dmod_parts]
    grad_x = dx[None]
    grads["mlp_w1"] = jnp.stack(g_w1)
    grads["mlp_w2"] = jnp.stack(g_w2)
    for n in ("fox_w_in", "fox_w_o", "ssm_w_in", "ssm_w_out", "ssm_conv_w"):
        grads[n] = grads[n][None]

    small_names = ("ln_mix_g", "ln_mix_b", "ln_mlp_g", "ln_mlp_b", "fox_b_f", "ssm_dt_bias", "ssm_a_log", "ssm_d")
    small_parts = list(dmods)
    for n in small_names[:4]:
        small_parts.append(jnp.concatenate(ln_grads[n], axis=0))
    for n in small_names[4:]:
        small_parts.append(jnp.pad(grads[n], ((0, 0), (0, LANES - grads[n].shape[1]))))
    small_vec = _pack_rows(small_parts, 1, F32).reshape(-1, LANES)
    small_vec = jnp.pad(small_vec, ((0, -small_vec.shape[0] % 8), (0, 0)))
    small_g_all = _allgather8(small_vec, "gather_small_grads")
    small_sum = _sum_slots(small_g_all, "sum_small_grads").reshape(-1)
    dmod_sum = small_sum[:DEPTH * 6 * d].reshape(DEPTH, 6 * d)
    off = DEPTH * 6 * d
    final = {"ada_b": dmod_sum}
    for n in small_names[:4]:
        final[n] = small_sum[off:off + DEPTH * d].reshape(DEPTH, d)
        off += DEPTH * d
    for n in small_names[4:]:
        width = weights[n].shape[1]
        final[n] = small_sum[off:off + width].reshape(1, width)
        off += LANES

    dmod_all = small_g_all.reshape(8, -1)[:, :DEPTH * 6 * d].reshape(8, DEPTH, 6 * d)
    dmod_cols = lax.dynamic_slice_in_dim(dmod_all, chip * cols, cols, axis=2).transpose(1, 0, 2)
    final["ada_w"] = _ada_backward(c_all.T, dmod_cols, "ada_bwd")

    sharded = BIG + SMALL_SHARDED
    shard_shapes = [weights[n].shape for n in sharded]
    per_target = []
    for k in range(4):
        parts = [_chip_slice(grads[n], SHARD_AXIS[n], k, weights[n].shape[SHARD_AXIS[n]]) for n in sharded]
        per_target.append(_pack2d(parts, 128, F32))
    g_all = jnp.stack(per_target)
    rows = g_all.shape[1]
    g4 = g_all.reshape(4, 2, rows // 2, PACK_COLS)
    recv = _pair_exchange(g4, "rs_pair_exchange")
    part, part16 = _pair_add(g4, recv, jnp.reshape(mc, (1,)).astype(jnp.int32), "rs_pair_add")
    slots = _chip_exchange(part16, "rs_chip_exchange")
    half = _chip_sum(part, slots, jnp.reshape(chip, (1,)).astype(jnp.int32), "rs_chip_sum")
    other = _pair_share(half, "rs_pair_share")
    both = jnp.concatenate([jnp.where(mc == 0, half, other), jnp.where(mc == 0, other, half)], axis=0)
    for n, g_shard in zip(sharded, _unpack2d(both, shard_shapes)):
        final[n] = g_shard

    outs_g, outs_d, outs_m, outs_v = [], [], [], []
    for n in names:
        wv = weights[n]
        two_d = (-1, wv.shape[-1])
        delta, mn, vn = _adamw(wv.reshape(two_d), final[n].reshape(two_d), m_in[n].reshape(two_d),
                               v_in[n].reshape(two_d), "adamw_" + n)
        outs_g.append(final[n].reshape(wv.shape))
        outs_d.append(delta.reshape(wv.shape))
        outs_m.append(mn.reshape(wv.shape))
        outs_v.append(vn.reshape(wv.shape))
    return (loss, grad_x, *outs_g, *outs_d, *outs_m, *outs_v)
```

```python
import functools

import jax
import jax.numpy as jnp
from jax import lax
from jax.experimental import pallas as pl
from jax.experimental.pallas import tpu as pltpu

F32, BF16 = jnp.float32, jnp.bfloat16
MESH = pl.DeviceIdType.MESH
HBM_SPEC = pl.BlockSpec(memory_space=pltpu.HBM)

VMEM_LIMIT_BYTES = 52 * 2**20
LANES = 128

FOX_HEADS, HEAD_DIM = 16, 64
SSM_HEADS, SSM_GROUPS, SSM_STATE, SSM_CHUNK, SSM_CONV = 32, 8, 128, 128, 4
SSM_GROUP_WIDTH = 256
LN_EPS, RMS_EPS = 1e-5, 1e-5
DEPTH = 2
ALPHA = (2.0 * DEPTH) ** 0.25
ADAM_LR, ADAM_B1, ADAM_B2, ADAM_EPS, ADAM_WD, ADAM_STEP = 0.001, 0.9, 0.999, 1e-08, 0.01, 10

ATT_TILE = 512
ROW_TILE = 256
SCAN_TILE = 512
SSD_GROUPS_PER_STEP = 4
MM_TM, MM_TN, MM_TK = 1024, 1024, 1024

NT_DIMS = (((1,), (1,)), ((), ()))
TN_DIMS = (((0,), (0,)), ((), ()))
NN_DIMS = (((1,), (0,)), ((), ()))


def _pcall(body, *, name, out_shape, grid=(), in_specs=None, out_specs=None, scratch=(), sem=None, prefetch=0):
    params = dict(vmem_limit_bytes=VMEM_LIMIT_BYTES)
    if sem is not None:
        params["dimension_semantics"] = sem
    if prefetch:
        grid_spec = pltpu.PrefetchScalarGridSpec(num_scalar_prefetch=prefetch, grid=grid, in_specs=in_specs,
                                                 out_specs=out_specs, scratch_shapes=scratch)
        return pl.pallas_call(body, out_shape=out_shape, grid_spec=grid_spec, name=name,
                              compiler_params=pltpu.CompilerParams(**params))
    kwargs = {}
    if in_specs is not None:
        kwargs["in_specs"] = in_specs
    if out_specs is not None:
        kwargs["out_specs"] = out_specs
    return pl.pallas_call(body, out_shape=out_shape, grid=grid, scratch_shapes=scratch, name=name,
                          compiler_params=pltpu.CompilerParams(**params), **kwargs)


def _sds(shape, dtype):
    return jax.ShapeDtypeStruct(tuple(shape), dtype)


def _dot(a, b, dims=NN_DIMS):
    return lax.dot_general(a, b, dims, preferred_element_type=F32)


def _sigmoid(x):
    return 1.0 / (1.0 + jnp.exp(-x))


def _silu(x):
    return x * _sigmoid(x)


def _dsilu(x):
    s = _sigmoid(x)
    return s * (1.0 + x * (1.0 - s))


def _dot_split(x, m16, dims=NN_DIMS, passes=3):
    hi = x.astype(BF16)
    r1 = x - hi.astype(F32)
    mid = r1.astype(BF16)
    out = _dot(hi, m16, dims) + _dot(mid, m16, dims)
    if passes == 3:
        lo = (r1 - mid.astype(F32)).astype(BF16)
        out = out + _dot(lo, m16, dims)
    return out


def _mm(a, b, dims, outs, *, name, tm=MM_TM, tn=MM_TN, tk=MM_TK, epi=None, extra=()):
    if dims == "nn":
        (m, k), (k2, n) = a.shape, b.shape
    elif dims == "nt":
        (m, k), (n, k2) = a.shape, b.shape
    else:
        (k, m), (k2, n) = a.shape, b.shape
    assert k == k2, (a.shape, b.shape, dims)
    tm, tn, tk = min(tm, m), min(tn, n), min(tk, k)
    assert m % tm == 0 and n % tn == 0 and k % tk == 0, (m, n, k, tm, tn, tk)
    nk = k // tk
    dn = {"nn": NN_DIMS, "nt": NT_DIMS, "tn": TN_DIMS}[dims]
    n_extra, n_out = len(extra), len(outs)
    if epi is None:
        epi = lambda acc: (acc,) * n_out

    def body(a_ref, b_ref, *rest):
        extra_refs, out_refs, acc_ref = rest[:n_extra], rest[n_extra:n_extra + n_out], rest[-1]
        kk = pl.program_id(2)

        @pl.when(kk == 0)
        def _():
            acc_ref[...] = jnp.zeros_like(acc_ref)

        acc_ref[...] += _dot(a_ref[...].astype(BF16), b_ref[...].astype(BF16), dn)

        @pl.when(kk == nk - 1)
        def _():
            res = epi(acc_ref[...], *[e[...] for e in extra_refs])
            for o, r in zip(out_refs, res):
                o[...] = r.astype(o.dtype)

    if dims == "tn":
        a_spec = pl.BlockSpec((tk, tm), lambda i, j, kk: (kk, i))
    else:
        a_spec = pl.BlockSpec((tm, tk), lambda i, j, kk: (i, kk))
    if dims == "nt":
        b_spec = pl.BlockSpec((tn, tk), lambda i, j, kk: (j, kk))
    else:
        b_spec = pl.BlockSpec((tk, tn), lambda i, j, kk: (kk, j))
    o_spec = pl.BlockSpec((tm, tn), lambda i, j, kk: (i, j))
    res = _pcall(body, name=name, grid=(m // tm, n // tn, nk),
                 in_specs=[a_spec, b_spec] + [o_spec] * n_extra,
                 out_specs=[o_spec] * n_out,
                 out_shape=[_sds((m, n), d) for d in outs],
                 scratch=[pltpu.VMEM((tm, tn), F32)],
                 sem=("parallel", "parallel", "arbitrary"))(a, b, *extra)
    return res[0] if n_out == 1 else res


def _k_blocks(a, tk=None):
    tk = tk or MM_TK
    return [(a, kb) for kb in range(a.shape[1] // tk)]


def _mm_nt_blocks(a_blocks, b, start, *, name, tm=MM_TM, tk=None):
    tk = tk or MM_TK
    tm = min(tm, a_blocks[0][0].shape[0])
    m, n, p_n = a_blocks[0][0].shape[0], b.shape[0], len(a_blocks)
    assert b.shape[1] == p_n * tk and m % tm == 0

    def body(*refs):
        a_refs = refs[:p_n]
        b_ref, s_ref, o_ref, acc_ref = refs[p_n:]
        kk = pl.program_id(1)

        @pl.when(kk == 0)
        def _():
            acc_ref[...] = s_ref[...]
        for p in range(p_n):
            @pl.when(kk == p)
            def _(p=p):
                acc_ref[...] += _dot(a_refs[p][...].astype(BF16), b_ref[...].astype(BF16), NT_DIMS)

        @pl.when(kk == p_n - 1)
        def _():
            o_ref[...] = acc_ref[...]

    in_specs = [pl.BlockSpec((tm, tk), functools.partial(lambda kb, i, kk: (i, kb), kb)) for _, kb in a_blocks]
    in_specs += [pl.BlockSpec((n, tk), lambda i, kk: (0, kk)), pl.BlockSpec((tm, n), lambda i, kk: (i, 0))]
    return _pcall(body, name=name, grid=(m // tm, p_n), in_specs=in_specs,
                  out_specs=pl.BlockSpec((tm, n), lambda i, kk: (i, 0)), out_shape=_sds((m, n), F32),
                  scratch=[pltpu.VMEM((tm, n), F32)], sem=("parallel", "arbitrary"))(*[a for a, _ in a_blocks], b, start)


def _rowwise(fn, rows, consts, row_outs, acc_outs, *, name, tm=ROW_TILE):
    s = rows[0].shape[0]
    tm = min(tm, s)
    assert s % tm == 0
    n_in, n_o = len(rows) + len(consts), len(row_outs)

    def body(*refs):
        ins, outs = refs[:n_in], refs[n_in:]
        res = fn(*[r[...] for r in ins])
        if not isinstance(res, (tuple, list)):
            res = (res,)
        for o, val in zip(outs[:n_o], res[:n_o]):
            o[...] = val.astype(o.dtype)
        if acc_outs:
            @pl.when(pl.program_id(0) == 0)
            def _():
                for o in outs[n_o:]:
                    o[...] = jnp.zeros_like(o)
            for o, val in zip(outs[n_o:], res[n_o:]):
                o[...] += val

    in_specs = [pl.BlockSpec((tm, r.shape[1]), lambda i: (i, 0)) for r in rows]
    in_specs += [pl.BlockSpec(c.shape, functools.partial(lambda nd, i: (0,) * nd, c.ndim)) for c in consts]
    out_specs = [pl.BlockSpec((tm, c), lambda i: (i, 0)) for c, _ in row_outs]
    out_specs += [pl.BlockSpec(tuple(sh), lambda i: (0, 0)) for sh in acc_outs]
    out_shape = [_sds((s, c), d) for c, d in row_outs] + [_sds(sh, F32) for sh in acc_outs]
    res = _pcall(body, name=name, grid=(s // tm,), in_specs=in_specs, out_specs=out_specs,
                 out_shape=out_shape, sem=("arbitrary",))(*rows, *consts)
    return res


def _colsum(x):
    return jnp.sum(x, axis=0, keepdims=True)


def _ln_stats(r):
    mu = jnp.mean(r, axis=-1, keepdims=True)
    xc = r - mu
    var = jnp.mean(xc * xc, axis=-1, keepdims=True)
    rstd = lax.rsqrt(var + LN_EPS)
    return xc * rstd, rstd


def _ln_bwd(dy, xhat, rstd, gamma):
    dyg = dy * gamma
    m1 = jnp.mean(dyg, axis=-1, keepdims=True)
    m2 = jnp.mean(dyg * xhat, axis=-1, keepdims=True)
    return rstd * (dyg - m1 - xhat * m2)


def _modulate_in(x, mod, name):
    def fn(xv, m):
        return (xv * (1.0 + m[1:2]) + m[0:1],)
    return _rowwise(fn, [x], [mod], [(x.shape[1], BF16)], [], name=name)[0]


def _res_ln_mod(x, y, mod, g, b, name):
    d = x.shape[1]

    def fn(xv, yv, m, gv, bv):
        r = ALPHA * xv + (1.0 + m[2:3]) * yv
        xhat, _ = _ln_stats(r)
        x1 = xhat * gv + bv
        u2 = x1 * (1.0 + m[4:5]) + m[3:4]
        return r, x1, u2
    return _rowwise(fn, [x, y], [mod, g, b], [(d, F32), (d, F32), (d, BF16)], [], name=name)


def _res_ln(x, y, mod, g, b, name):
    d = x.shape[1]

    def fn(xv, yv, m, gv, bv):
        r = ALPHA * xv + (1.0 + m[5:6]) * yv
        xhat, _ = _ln_stats(r)
        return r, xhat * gv + bv
    return _rowwise(fn, [x, y], [mod, g, b], [(d, F32), (d, F32)], [], name=name)


def _res_ln_next(x, y, mod, g, b, mod_next, name):
    d = x.shape[1]

    def fn(xv, yv, m, gv, bv, mn):
        r = ALPHA * xv + (1.0 + m[5:6]) * yv
        xhat, _ = _ln_stats(r)
        out = xhat * gv + bv
        return r, out, out * (1.0 + mn[1:2]) + mn[0:1]
    return _rowwise(fn, [x, y], [mod, g, b, mod_next], [(d, F32), (d, F32), (d, BF16)], [], name=name)


def _loss_ln2_bwd(r2, y2, target, mod, g, b, name):
    d = r2.shape[1]

    def fn(rv, yv, tv, m, gv, bv):
        xhat, rstd = _ln_stats(rv)
        e = xhat * gv + bv - tv
        dxv = e * (1.0 / d)
        dr = _ln_bwd(dxv, xhat, rstd, gv)
        return (dr * (1.0 + m[5:6]), ALPHA * dr,
                _colsum(e * e), _colsum(dxv * xhat), _colsum(dxv), _colsum(dr * yv))
    return _rowwise(fn, [r2, y2, target], [mod, g, b], [(d, BF16), (d, F32)], [(1, d)] * 4, name=name)


def _mod_in_ln2_bwd(du, dres, r2, y2, mod, g, b, mod_next, name):
    d = du.shape[1]

    def fn(duv, drv, rv, yv, m, gv, bv, mn):
        xhat, rstd = _ln_stats(rv)
        xout = xhat * gv + bv
        dxv = duv * (1.0 + mn[1:2]) + drv
        dr = _ln_bwd(dxv, xhat, rstd, gv)
        return (dr * (1.0 + m[5:6]), ALPHA * dr,
                _colsum(duv * xout), _colsum(duv), _colsum(dxv * xhat), _colsum(dxv), _colsum(dr * yv))
    return _rowwise(fn, [du, dres, r2, y2], [mod, g, b, mod_next], [(d, BF16), (d, F32)], [(1, d)] * 5, name=name)


def _ln2_bwd(dx, r2, y2, mod, g, name):
    d = dx.shape[1]

    def fn(dxv, rv, yv, m, gv):
        xhat, rstd = _ln_stats(rv)
        dr = _ln_bwd(dxv, xhat, rstd, gv)
        return (dr * (1.0 + m[5:6]), ALPHA * dr,
                _colsum(dxv * xhat), _colsum(dxv), _colsum(dr * yv))
    return _rowwise(fn, [dx, r2, y2], [mod, g], [(d, BF16), (d, F32)], [(1, d)] * 3, name=name)


def _ln1_bwd(du2, dres, r, y, mod, g, b, name):
    d = du2.shape[1]

    def fn(duv, drv, rv, yv, m, gv, bv):
        xhat, rstd = _ln_stats(rv)
        x1 = xhat * gv + bv
        dx1 = duv * (1.0 + m[4:5]) + drv
        dr = _ln_bwd(dx1, xhat, rstd, gv)
        return (dr * (1.0 + m[2:3]), ALPHA * dr,
                _colsum(duv * x1), _colsum(duv), _colsum(dx1 * xhat), _colsum(dx1), _colsum(dr * yv))
    return _rowwise(fn, [du2, dres, r, y], [mod, g, b], [(d, BF16), (d, F32)], [(1, d)] * 5, name=name)


def _mod_in_bwd(du, dres, x, mod, name):
    d = du.shape[1]

    def fn(duv, drv, xv, m):
        return duv * (1.0 + m[1:2]) + drv, _colsum(duv * xv), _colsum(duv)
    return _rowwise(fn, [du, dres, x], [mod], [(d, F32)], [(1, d)] * 2, name=name)


def _fox_gate(fraw, b_pad, name):
    s = fraw.shape[0]
    tb = min(SCAN_TILE, s)

    def body(f_ref, b_ref, cum_ref, carry):
        @pl.when(pl.program_id(0) == 0)
        def _():
            carry[...] = jnp.zeros_like(carry)
        z = f_ref[...] + b_ref[...]
        lf = jnp.minimum(z, 0.0) - jnp.log(1.0 + jnp.exp(-jnp.abs(z)))
        lane = lax.broadcasted_iota(jnp.int32, (tb, LANES), 1)
        row = lax.broadcasted_iota(jnp.int32, (tb, LANES), 0)
        c = jnp.where(lane < FOX_HEADS, lf, 0.0)
        sh = 1
        while sh < tb:
            c = c + jnp.where(row >= sh, pltpu.roll(c, sh, 0), 0.0)
            sh *= 2
        c = c + carry[0:1, :]
        cum_ref[...] = c
        carry[0:1, :] = c[tb - 1:tb, :]

    return _pcall(body, name=name, grid=(s // tb,),
                  in_specs=[pl.BlockSpec((tb, LANES), lambda i: (i, 0)), pl.BlockSpec((1, LANES), lambda i: (0, 0))],
                  out_specs=pl.BlockSpec((tb, LANES), lambda i: (i, 0)),
                  out_shape=_sds((s, LANES), F32), scratch=[pltpu.VMEM((8, LANES), F32)],
                  sem=("arbitrary",))(fraw, b_pad)


def _fox_gate_bwd(drow, dcol, fraw, b_pad, name):
    s = fraw.shape[0]
    tb = min(SCAN_TILE, s)
    n = s // tb

    def body(dr_ref, dc_ref, f_ref, b_ref, df_ref, db_ref, carry):
        @pl.when(pl.program_id(0) == 0)
        def _():
            carry[...] = jnp.zeros_like(carry)
            db_ref[...] = jnp.zeros_like(db_ref)
        row = lax.broadcasted_iota(jnp.int32, (tb, LANES), 0)
        c = dr_ref[...] + dc_ref[...]
        sh = 1
        while sh < tb:
            c = c + jnp.where(row + sh < tb, pltpu.roll(c, tb - sh, 0), 0.0)
            sh *= 2
        c = c + carry[0:1, :]
        carry[0:1, :] = c[0:1, :]
        z = f_ref[...] + b_ref[...]
        df = c * (1.0 / (1.0 + jnp.exp(z)))
        df_ref[...] = df.astype(df_ref.dtype)
        db_ref[...] += _colsum(df)

    rev = lambda i: (n - 1 - i, 0)
    return _pcall(body, name=name, grid=(n,),
                  in_specs=[pl.BlockSpec((tb, LANES), rev)] * 3 + [pl.BlockSpec((1, LANES), lambda i: (0, 0))],
                  out_specs=[pl.BlockSpec((tb, LANES), rev), pl.BlockSpec((1, LANES), lambda i: (0, 0))],
                  out_shape=[_sds((s, LANES), BF16), _sds((1, LANES), F32)],
                  scratch=[pltpu.VMEM((8, LANES), F32)], sem=("arbitrary",))(drow, dcol, fraw, b_pad)


def _head_pair_masks(t):
    lane = lax.broadcasted_iota(jnp.int32, (t, LANES), 1)
    return lane < HEAD_DIM


def _lane_blocks(x):
    return [x[:, c * LANES:(c + 1) * LANES] for c in range(x.shape[1] // LANES)]


def _sum_list(xs):
    acc = xs[0]
    for x in xs[1:]:
        acc = acc + x
    return acc


def _causal(t, transposed=False):
    ri = lax.broadcasted_iota(jnp.int32, (t, t), 0)
    ci = lax.broadcasted_iota(jnp.int32, (t, t), 1)
    return ci >= ri if transposed else ri >= ci


def _flash_fwd(qkv, ck_rows, kb_start, name):
    s = qkv.shape[0]
    t = min(ATT_TILE, s)
    nq = s // t
    scale = HEAD_DIM ** -0.5
    hp_blocks = FOX_HEADS // 2

    def body(ks_ref, q_ref, k_ref, v_ref, ck_ref, o_ref, lse_ref, acc_ref, m_ref, l_ref):
        hp, qb = pl.program_id(0), pl.program_id(1)
        q2 = q_ref[...] * scale
        first = _head_pair_masks(t)
        zero = jnp.zeros_like(q2)
        qs = (jnp.where(first, q2, zero), jnp.where(first, zero, q2))
        m_ref[...] = jnp.full_like(m_ref, -jnp.inf)
        l_ref[...] = jnp.zeros_like(l_ref)
        acc_ref[...] = jnp.zeros_like(acc_ref)

        def tile(kb, diagonal):
            off = pl.multiple_of(kb * t, t)
            k2 = k_ref[pl.ds(off, t), :]
            v2 = v_ref[pl.ds(off, t), :]
            ck = ck_ref[kb]
            pvs, als = [], []
            for j in range(2):
                sc = _dot(qs[j], k2, NT_DIMS) - ck[j:j + 1, :]
                if diagonal:
                    sc = jnp.where(_causal(t), sc, -jnp.inf)
                blocks = _lane_blocks(sc)
                mx = blocks[0]
                for b in blocks[1:]:
                    mx = jnp.maximum(mx, b)
                m_old = m_ref[j]
                m_new = jnp.maximum(m_old, jnp.max(mx, axis=1, keepdims=True))
                ps = [jnp.exp(b - m_new) for b in blocks]
                a = jnp.exp(m_old - m_new)
                l_ref[j] = a * l_ref[j] + _sum_list(ps)
                m_ref[j] = m_new
                pvs.append(_dot(jnp.concatenate(ps, axis=1).astype(BF16), v2))
                als.append(a)
            acc_ref[...] = jnp.where(first, als[0], als[1]) * acc_ref[...] + jnp.where(first, pvs[0], pvs[1])

        def step(kb, carry):
            tile(kb, False)
            return carry

        lax.fori_loop(ks_ref[hp, qb], qb, step, 0)
        tile(qb, True)
        l0 = jnp.sum(l_ref[0], axis=1, keepdims=True)
        l1 = jnp.sum(l_ref[1], axis=1, keepdims=True)
        o_ref[...] = acc_ref[...] / jnp.where(first, l0, l1)
        lse_ref[:, 0:1] = m_ref[0][:, 0:1] + jnp.log(l0)
        lse_ref[:, 1:2] = m_ref[1][:, 0:1] + jnp.log(l1)

    return _pcall(
        body, name=name, grid=(hp_blocks, nq), prefetch=1,
        in_specs=[pl.BlockSpec((t, LANES), lambda h, i, ks: (i, h)),
                  pl.BlockSpec((s, LANES), lambda h, i, ks: (0, hp_blocks + h)),
                  pl.BlockSpec((s, LANES), lambda h, i, ks: (0, 2 * hp_blocks + h)),
                  pl.BlockSpec((None, nq, 2, t), lambda h, i, ks: (h, 0, 0, 0))],
        out_specs=[pl.BlockSpec((t, LANES), lambda h, i, ks: (i, h)),
                   pl.BlockSpec((None, t, 2), lambda h, i, ks: (h, i, 0))],
        out_shape=[_sds((s, hp_blocks * LANES), F32), _sds((hp_blocks, s, 2), F32)],
        scratch=[pltpu.VMEM((t, LANES), F32), pltpu.VMEM((2, t, LANES), F32), pltpu.VMEM((2, t, LANES), F32)],
        sem=("parallel", "arbitrary"))(kb_start, qkv, qkv, qkv, ck_rows)


def _flash_dq(qkv, do16, ck_rows, lse_c, dl_c, kb_start, name):
    s = qkv.shape[0]
    t = min(ATT_TILE, s)
    nq = s // t
    scale = HEAD_DIM ** -0.5
    hp_blocks = FOX_HEADS // 2

    def body(ks_ref, q_ref, do_ref, k_ref, v_ref, ck_ref, lse_ref, dl_ref, dq_ref, drow_ref, acc_ref, row_acc):
        hp, qb = pl.program_id(0), pl.program_id(1)
        q2, do2 = q_ref[...] * scale, do_ref[...]
        first = _head_pair_masks(t)
        zero = jnp.zeros_like(q2)
        qs = (jnp.where(first, q2, zero), jnp.where(first, zero, q2))
        dos = (jnp.where(first, do2, zero), jnp.where(first, zero, do2))
        lse, dl = lse_ref[...], dl_ref[...]
        lse_b = [jnp.broadcast_to(lse[:, j:j + 1], (t, LANES)) for j in range(2)]
        dl_b = [jnp.broadcast_to(dl[:, j:j + 1], (t, LANES)) for j in range(2)]
        acc_ref[...] = jnp.zeros_like(acc_ref)
        row_acc[...] = jnp.zeros_like(row_acc)

        def tile(kb, diagonal):
            off = pl.multiple_of(kb * t, t)
            k2 = k_ref[pl.ds(off, t), :]
            v2 = v_ref[pl.ds(off, t), :]
            ck = ck_ref[kb]
            dqs = []
            for j in range(2):
                sc = _dot(qs[j], k2, NT_DIMS) - ck[j:j + 1, :]
                if diagonal:
                    sc = jnp.where(_causal(t), sc, -jnp.inf)
                dp = _dot(dos[j], v2, NT_DIMS)
                dsb = [jnp.exp(x - lse_b[j]) * (d - dl_b[j]) for x, d in zip(_lane_blocks(sc), _lane_blocks(dp))]
                row_acc[j] += _sum_list(dsb)
                dqs.append(_dot(jnp.concatenate(dsb, axis=1).astype(BF16), k2))
            acc_ref[...] += jnp.where(first, dqs[0], dqs[1])

        def step(kb, carry):
            tile(kb, False)
            return carry

        lax.fori_loop(ks_ref[hp, qb], qb, step, 0)
        tile(qb, True)
        dq_ref[...] = (acc_ref[...] * scale).astype(dq_ref.dtype)
        drow_ref[:, 0:1] = jnp.sum(row_acc[0], axis=1, keepdims=True)
        drow_ref[:, 1:2] = jnp.sum(row_acc[1], axis=1, keepdims=True)

    return _pcall(
        body, name=name, grid=(hp_blocks, nq), prefetch=1,
        in_specs=[pl.BlockSpec((t, LANES), lambda h, i, ks: (i, h)),
                  pl.BlockSpec((t, LANES), lambda h, i, ks: (i, h)),
                  pl.BlockSpec((s, LANES), lambda h, i, ks: (0, hp_blocks + h)),
                  pl.BlockSpec((s, LANES), lambda h, i, ks: (0, 2 * hp_blocks + h)),
                  pl.BlockSpec((None, nq, 2, t), lambda h, i, ks: (h, 0, 0, 0)),
                  pl.BlockSpec((None, t, 2), lambda h, i, ks: (h, i, 0)),
                  pl.BlockSpec((None, t, 2), lambda h, i, ks: (h, i, 0))],
        out_specs=[pl.BlockSpec((t, LANES), lambda h, i, ks: (i, h)),
                   pl.BlockSpec((None, t, 2), lambda h, i, ks: (h, i, 0))],
        out_shape=[_sds((s, hp_blocks * LANES), BF16), _sds((hp_blocks, s, 2), F32)],
        scratch=[pltpu.VMEM((t, LANES), F32), pltpu.VMEM((2, t, LANES), F32)],
        sem=("parallel", "arbitrary"))(kb_start, qkv, do16, qkv, qkv, ck_rows, lse_c, dl_c)


def _flash_dkv(qkv, do16, cum, lse_rows, dl_rows, qb_end, name):
    s = qkv.shape[0]
    t = min(ATT_TILE, s)
    nq = s // t
    scale = HEAD_DIM ** -0.5
    hp_blocks = FOX_HEADS // 2

    def body(qe_ref, k_ref, v_ref, cum_ref, q_ref, do_ref, lse_ref, dl_ref, dk_ref, dv_ref, dck_ref,
             dk_acc, dv_acc, dck_acc):
        hp, kb = pl.program_id(0), pl.program_id(1)
        k2, v2 = k_ref[...] * scale, v_ref[...]
        first = _head_pair_masks(t)
        zero = jnp.zeros_like(k2)
        ks = (jnp.where(first, k2, zero), jnp.where(first, zero, k2))
        vs = (jnp.where(first, v2, zero), jnp.where(first, zero, v2))
        cumv = cum_ref[...]
        lane = lax.broadcasted_iota(jnp.int32, (t, LANES), 1)
        ck_b = [jnp.broadcast_to(jnp.sum(jnp.where(lane == 2 * hp + j, cumv, 0.0), axis=1, keepdims=True), (t, LANES))
                for j in range(2)]
        dk_acc[...] = jnp.zeros_like(dk_acc)
        dv_acc[...] = jnp.zeros_like(dv_acc)
        dck_acc[...] = jnp.zeros_like(dck_acc)

        def tile(qb, diagonal):
            off = pl.multiple_of(qb * t, t)
            q2 = q_ref[pl.ds(off, t), :]
            do2 = do_ref[pl.ds(off, t), :]
            lse, dl = lse_ref[qb], dl_ref[qb]
            dvs, dks = [], []
            for j in range(2):
                sc = _dot(ks[j], q2, NT_DIMS)
                if diagonal:
                    sc = jnp.where(_causal(t, transposed=True), sc, -jnp.inf)
                dp = _dot(vs[j], do2, NT_DIMS) - dl[j:j + 1, :]
                pb = [jnp.exp((x - ck_b[j]) - l) for x, l in zip(_lane_blocks(sc), _lane_blocks(lse[j:j + 1, :]))]
                dsb = [p * d for p, d in zip(pb, _lane_blocks(dp))]
                dck_acc[j] += _sum_list(dsb)
                dvs.append(_dot(jnp.concatenate(pb, axis=1).astype(BF16), do2))
                dks.append(_dot(jnp.concatenate(dsb, axis=1).astype(BF16), q2))
            dv_acc[...] += jnp.where(first, dvs[0], dvs[1])
            dk_acc[...] += jnp.where(first, dks[0], dks[1])

        def step(qb, carry):
            tile(qb, False)
            return carry

        tile(kb, True)
        lax.fori_loop(kb + 1, qe_ref[hp, kb] + 1, step, 0)
        dk_ref[...] = (dk_acc[...] * scale).astype(dk_ref.dtype)
        dv_ref[...] = dv_acc[...].astype(dv_ref.dtype)
        dck_ref[:, 0:1] = -jnp.sum(dck_acc[0], axis=1, keepdims=True)
        dck_ref[:, 1:2] = -jnp.sum(dck_acc[1], axis=1, keepdims=True)

    return _pcall(
        body, name=name, grid=(hp_blocks, nq), prefetch=1,
        in_specs=[pl.BlockSpec((t, LANES), lambda h, j, qe: (j, hp_blocks + h)),
                  pl.BlockSpec((t, LANES), lambda h, j, qe: (j, 2 * hp_blocks + h)),
                  pl.BlockSpec((t, LANES), lambda h, j, qe: (j, 0)),
                  pl.BlockSpec((s, LANES), lambda h, j, qe: (0, h)),
                  pl.BlockSpec((s, LANES), lambda h, j, qe: (0, h)),
                  pl.BlockSpec((None, nq, 2, t), lambda h, j, qe: (h, 0, 0, 0)),
                  pl.BlockSpec((None, nq, 2, t), lambda h, j, qe: (h, 0, 0, 0))],
        out_specs=[pl.BlockSpec((t, LANES), lambda h, j, qe: (j, h)),
                   pl.BlockSpec((t, LANES), lambda h, j, qe: (j, h)),
                   pl.BlockSpec((None, t, 2), lambda h, j, qe: (h, j, 0))],
        out_shape=[_sds((s, hp_blocks * LANES), BF16), _sds((s, hp_blocks * LANES), BF16),
                   _sds((hp_blocks, s, 2), F32)],
        scratch=[pltpu.VMEM((t, LANES), F32), pltpu.VMEM((t, LANES), F32), pltpu.VMEM((2, t, LANES), F32)],
        sem=("parallel", "arbitrary"))(qb_end, qkv, qkv, cum, qkv, do16, lse_rows, dl_rows)


SKIP_NATS = 110.0


def _qk_norms(qkv, ind16, name):
    d = FOX_HEADS * HEAD_DIM

    def fn(tile, ind):
        q = tile[:, :d].astype(F32)
        k = tile[:, d:2 * d].astype(F32)
        return _dot_split(q * q, ind), _dot_split(k * k, ind)
    return _rowwise(fn, [qkv], [ind16], [(LANES, F32), (LANES, F32)], [], name=name)


def _skip_bounds(qn, kn, cum, t):
    s = qn.shape[0]
    nq = s // t
    hp = FOX_HEADS // 2
    scale = HEAD_DIM ** -0.5
    qmax = jnp.sqrt(jnp.max(qn.reshape(nq, t, FOX_HEADS), axis=1))
    kmax = jnp.sqrt(jnp.max(kn, axis=0))
    bound = qmax * kmax[None, :] * (scale * 1.01) + 1e-3
    gap = cum[0::t][:, None, :] - cum[t - 1::t][None, :, :]
    idx = jnp.arange(nq, dtype=jnp.int32)
    needed = (gap + 2.0 * bound[:, None, :]) > -SKIP_NATS
    needed = needed.reshape(nq, nq, hp, 2).any(axis=-1) & (idx[None, :] <= idx[:, None])[:, :, None]
    first = jnp.min(jnp.where(needed, idx[None, :, None], nq), axis=1)
    first = jnp.minimum(first, idx[:, None])
    start = lax.cummin(first, axis=0, reverse=True)
    uses = start[:, None, :] <= idx[None, :, None]
    last = jnp.max(jnp.where(uses, idx[:, None, None], 0), axis=0)
    last = jnp.maximum(last, idx[:, None])
    return start.T.astype(jnp.int32), last.T.astype(jnp.int32)


def _head_rowsum(prod, ind16, name):
    def fn(a, b, ind):
        return (_dot_split(a * b, ind),)
    return _rowwise(fn, list(prod), [ind16], [(LANES, F32)], [], name=name)[0]


def _pairs_cols(x16):
    s = x16.shape[0]
    return x16.reshape(s, FOX_HEADS // 2, 2).transpose(1, 0, 2)


def _pairs_rows(x16, t):
    s = x16.shape[0]
    return x16.reshape(s // t, t, FOX_HEADS // 2, 2).transpose(2, 0, 3, 1)


def _fox_forward(u, w):
    s = u.shape[0]
    t = min(ATT_TILE, s)
    qkv = _mm(u, w["fox_qkv"], "nn", [BF16], name="fox_qkv")
    fraw = _mm(u, w["fox_f"], "nn", [F32], name="fox_fproj")
    cum = _fox_gate(fraw, w["fox_bf"], "fox_gate")
    ck_rows = _pairs_rows(cum[:, :FOX_HEADS], t)
    qn, kn = _qk_norms(qkv, w["head_ind"], "fox_qk_norms")
    kb_start, qb_end = _skip_bounds(qn[:, :FOX_HEADS], kn[:, :FOX_HEADS], cum[:, :FOX_HEADS], t)
    o, lse = _flash_fwd(qkv, ck_rows, kb_start, "fox_flash_fwd")
    y = _mm(o, w["fox_o"], "nn", [F32], name="fox_oproj")
    return y, dict(u=u, qkv=qkv, fraw=fraw, cum=cum, ck_rows=ck_rows, o=o, lse=lse, kb_start=kb_start,
                   qb_end=qb_end)


def _fox_backward(dy, sv, w):
    s = dy.shape[0]
    t = min(ATT_TILE, s)
    do32, do16 = _mm(dy, w["fox_o"], "nt", [F32, BF16], name="fox_do")
    g_wo = _mm(sv["o"], dy, "tn", [F32], name="fox_gwo")
    delta = _head_rowsum((do32, sv["o"]), w["head_ind"], "fox_delta")[:, :FOX_HEADS]
    lse16 = sv["lse"].transpose(1, 0, 2).reshape(s, FOX_HEADS)
    dq, drow = _flash_dq(sv["qkv"], do16, sv["ck_rows"], sv["lse"], _pairs_cols(delta), sv["kb_start"],
                         "fox_flash_dq")
    dk, dv, dck = _flash_dkv(sv["qkv"], do16, sv["cum"], _pairs_rows(lse16, t), _pairs_rows(delta, t),
                             sv["qb_end"], "fox_flash_dkv")
    pad = ((0, 0), (0, LANES - FOX_HEADS))
    dcol = jnp.pad(dck.transpose(1, 0, 2).reshape(s, FOX_HEADS), pad)
    drow = jnp.pad(drow.transpose(1, 0, 2).reshape(s, FOX_HEADS), pad)
    df, db_f = _fox_gate_bwd(drow, dcol, sv["fraw"], w["fox_bf"], "fox_gate_bwd")
    du = _mm(df, w["fox_f"], "nt", [F32], name="fox_du_f")
    du = _mm_nt_blocks(_k_blocks(dq) + _k_blocks(dk) + _k_blocks(dv), w["fox_qkv"], du, name="fox_du", tm=512)
    g_win = jnp.concatenate([_mm(sv["u"], dq, "tn", [F32], name="fox_gwin_q"),
                             _mm(sv["u"], dk, "tn", [F32], name="fox_gwin_k"),
                             _mm(sv["u"], dv, "tn", [F32], name="fox_gwin_v"),
                             _mm(sv["u"], df, "tn", [F32], name="fox_gwin_f")[:, :FOX_HEADS]], axis=1)
    return du, dict(fox_w_in=g_win, fox_w_o=g_wo, fox_b_f=db_f[:, :FOX_HEADS])


def _conv_fwd(xpre, w8, b, name):
    s, c = xpre.shape
    tm, tc = min(ROW_TILE, s), min(1024, c)
    hb = tm // 8

    def body(x_ref, h_ref, w_ref, b_ref, xc_ref, xa_ref):
        i = pl.program_id(1)
        x = x_ref[...]
        halo = jnp.where(i > 0, h_ref[...], 0.0)
        w = w_ref[...]
        row = lax.broadcasted_iota(jnp.int32, (tm, tc), 0)
        row8 = lax.broadcasted_iota(jnp.int32, (8, tc), 0)
        acc = x * w[3:4] + b_ref[...]
        x8 = x[0:8]
        acc8 = x8 * w[3:4] + b_ref[...]
        for j in range(1, SSM_CONV):
            acc = acc + w[3 - j:4 - j] * pltpu.roll(x, j, 0)
            acc8 = acc8 + w[3 - j:4 - j] * jnp.where(row8 < j, pltpu.roll(halo, j, 0), pltpu.roll(x8, j, 0))
        xc_ref[...] = acc
        xc_ref[0:8, :] = acc8
        xc = xc_ref[...]
        xa_ref[...] = _silu(xc)

    tile = pl.BlockSpec((tm, tc), lambda jc, i: (i, jc))
    return _pcall(body, name=name, grid=(c // tc, s // tm),
                  in_specs=[tile, pl.BlockSpec((8, tc), lambda jc, i: (jnp.maximum(i * hb - 1, 0), jc)),
                            pl.BlockSpec((8, tc), lambda jc, i: (0, jc)), pl.BlockSpec((1, tc), lambda jc, i: (0, jc))],
                  out_specs=[tile, tile], out_shape=[_sds((s, c), F32), _sds((s, c), F32)],
                  sem=("parallel", "arbitrary"))(xpre, xpre, w8, b)


def _conv_bwd(dxa, xc, xpre, w8, name):
    s, c = xpre.shape
    tm, tc = min(ROW_TILE, s), min(1024, c)
    hb = tm // 8
    n = s // tm

    def body(d_ref, xc_ref, x_ref, xh_ref, dn_ref, xcn_ref, w_ref, dx_ref, dw_ref, db_ref, scr):
        i = pl.program_id(1)

        @pl.when(i == 0)
        def _():
            dw_ref[...] = jnp.zeros_like(dw_ref)
            db_ref[...] = jnp.zeros_like(db_ref)
        w = w_ref[...]
        x = x_ref[...]
        g = d_ref[...] * _dsilu(xc_ref[...])
        gn = jnp.where(i < n - 1, dn_ref[...] * _dsilu(xcn_ref[...]), 0.0)
        halo = jnp.where(i > 0, xh_ref[...], 0.0)
        row = lax.broadcasted_iota(jnp.int32, (tm, tc), 0)
        row8 = lax.broadcasted_iota(jnp.int32, (8, tc), 0)
        db_ref[...] += _colsum(g)
        dw_ref[3:4, :] += _colsum(g * x)
        g8 = g[0:8]
        acc = g * w[3:4]
        corr = jnp.zeros((8, tc), F32)
        for j in range(1, SSM_CONV):
            xs = pltpu.roll(x, j, 0)
            dwj = _colsum(jnp.where(row >= j, g * xs, 0.0))
            dwj = dwj + _colsum(jnp.where(row8 < j, g8 * pltpu.roll(halo, j, 0), 0.0))
            dw_ref[3 - j:4 - j, :] += dwj
            gs = pltpu.roll(g, tm - j, 0)
            acc = acc + w[3 - j:4 - j] * jnp.where(row < tm - j, gs, 0.0)
            corr = corr + w[3 - j:4 - j] * jnp.where(row8 >= 8 - j, pltpu.roll(gn, 8 - j, 0), 0.0)
        scr[...] = acc
        scr[tm - 8:tm, :] += corr
        dx_ref[...] = scr[...].astype(dx_ref.dtype)

    tile = pl.BlockSpec((tm, tc), lambda jc, i: (i, jc))
    prev8 = pl.BlockSpec((8, tc), lambda jc, i: (jnp.maximum(i * hb - 1, 0), jc))
    next8 = pl.BlockSpec((8, tc), lambda jc, i: (jnp.minimum((i + 1) * hb, n * hb - 1), jc))
    return _pcall(body, name=name, grid=(c // tc, n),
                  in_specs=[tile, tile, tile, prev8, next8, next8, pl.BlockSpec((8, tc), lambda jc, i: (0, jc))],
                  out_specs=[tile, pl.BlockSpec((8, tc), lambda jc, i: (0, jc)), pl.BlockSpec((1, tc), lambda jc, i: (0, jc))],
                  out_shape=[_sds((s, c), BF16), _sds((8, c), F32), _sds((1, c), F32)],
                  scratch=[pltpu.VMEM((tm, tc), F32)],
                  sem=("parallel", "arbitrary"))(dxa, xc, xpre, xpre, dxa, xc, w8)


def _ssd_pre(dtraw, dt_bias, a_log, cst, name):
    def fn(raw, bias, alog, expand):
        tm = raw.shape[0]
        z = raw + bias
        dt = jnp.maximum(z, 0.0) + jnp.log(1.0 + jnp.exp(-jnp.abs(z)))
        lane = lax.broadcasted_iota(jnp.int32, (tm, LANES), 1)
        pos = lax.broadcasted_iota(jnp.int32, (tm, LANES), 0) & (SSM_CHUNK - 1)
        dt = jnp.where(lane < SSM_HEADS, dt, 0.0)
        c = dt * (-jnp.exp(alog))
        sh = 1
        while sh < SSM_CHUNK:
            c = c + jnp.where(pos >= sh, pltpu.roll(c, sh, 0), 0.0)
            sh *= 2
        return dt, c, _dot_split(dt, expand), _dot_split(c, expand)
    wide = SSM_HEADS * HEAD_DIM
    return _rowwise(fn, [dtraw], [dt_bias, a_log, cst["expand"]],
                    [(LANES, F32), (LANES, F32), (wide, F32), (wide, F32)], [], name=name)


def _ssd_post(dacs, ddt, dtraw, dt, dt_bias, a_log, name):
    def fn(dacs_v, ddt_v, raw, dt_v, bias, alog):
        tm = raw.shape[0]
        pos = lax.broadcasted_iota(jnp.int32, (tm, LANES), 0) & (SSM_CHUNK - 1)
        a = -jnp.exp(alog)
        c = dacs_v
        sh = 1
        while sh < SSM_CHUNK:
            c = c + jnp.where(pos + sh < SSM_CHUNK, pltpu.roll(c, tm - sh, 0), 0.0)
            sh *= 2
        draw = (ddt_v + c * a) * _sigmoid(raw + bias)
        return draw, _colsum(draw), _colsum(c * dt_v * a)
    return _rowwise(fn, [dacs, ddt, dtraw, dt], [dt_bias, a_log], [(LANES, BF16)], [(1, LANES)] * 2, name=name)


def _heads_cols(x, s):
    return x[:, :SSM_HEADS].reshape(s, SSM_GROUPS, 4).transpose(1, 0, 2)


def _heads_rows(x, s):
    return x[:, :SSM_HEADS].reshape(s // SSM_CHUNK, SSM_CHUNK, SSM_GROUPS, 4).transpose(2, 0, 3, 1)


def _expand_heads(cols, lane):
    out = cols[:, 3:4]
    for r in (2, 1, 0):
        out = jnp.where(lane < HEAD_DIM * (r + 1), cols[:, r:r + 1], out)
    return out


def _ssd_common(x, dtc, acsc, acsr):
    l = SSM_CHUNK
    lane = lax.broadcasted_iota(jnp.int32, (l, SSM_GROUP_WIDTH), 1)
    dt_e = _expand_heads(dtc, lane)
    acs_e = _expand_heads(acsc, lane)
    last = acsr[:, l - 1:l]
    lane1 = lax.broadcasted_iota(jnp.int32, (1, SSM_GROUP_WIDTH), 1)
    last_e = last[3:4, :]
    for r in (2, 1, 0):
        last_e = jnp.where(lane1 < HEAD_DIM * (r + 1), last[r:r + 1, :], last_e)
    e_e = jnp.exp(acs_e)
    dte_e = jnp.exp(last_e - acs_e)
    rowg = lax.broadcasted_iota(jnp.int32, (SSM_GROUP_WIDTH, SSM_STATE), 0)
    cd = jnp.exp(last)
    cd_mat = cd[3:4, :]
    for r in (2, 1, 0):
        cd_mat = jnp.where(rowg < HEAD_DIM * (r + 1), cd[r:r + 1, :], cd_mat)
    return lane, dt_e, e_e, dte_e, cd, cd_mat


def _ssd_fwd(xa, dtc, acsc, acsr, d_e, name):
    s = xa.shape[0]
    l, gw, ns = SSM_CHUNK, SSM_GROUP_WIDTH, SSM_STATE
    nc = s // l
    xb, bb = 2048 // gw, 2048 // ns

    def body(x_ref, b_ref, c_ref, dtc_ref, acsc_ref, acsr_ref, d_ref, y_ref, hp_ref, h_sc):
        @pl.when(pl.program_id(1) == 0)
        def _():
            h_sc[...] = jnp.zeros_like(h_sc)
        x = x_ref[...]
        bm, cm = b_ref[...].astype(BF16), c_ref[...].astype(BF16)
        acsc, acsr = acsc_ref[...], acsr_ref[...]
        lane, dt_e, e_e, dte_e, _, cd_mat = _ssd_common(x, dtc_ref[...], acsc, acsr)
        xdt = x * dt_e
        xdt16 = xdt.astype(BF16)
        cb = _dot(cm, bm, NT_DIMS)
        tril = lax.broadcasted_iota(jnp.int32, (l, l), 0) >= lax.broadcasted_iota(jnp.int32, (l, l), 1)
        yd = jnp.zeros((l, gw), F32)
        for r in range(4):
            lm = jnp.exp(jnp.where(tril, acsc[:, r:r + 1] - acsr[r:r + 1, :], -jnp.inf))
            yr = _dot((cb * lm).astype(BF16), xdt16)
            yd = jnp.where((lane >= HEAD_DIM * r) & (lane < HEAD_DIM * (r + 1)), yr, yd)
        hp = h_sc[...]
        hp_ref[...] = hp
        yoff = _dot(cm, hp.astype(BF16), NT_DIMS) * e_e
        y_ref[...] = yd + yoff + x * d_ref[...]
        st = _dot((xdt * dte_e).astype(BF16), bm, TN_DIMS)
        h_sc[...] = hp * cd_mat + st

    return _pcall(
        body, name=name, grid=(SSM_GROUPS, nc),
        in_specs=[pl.BlockSpec((l, gw), lambda g, c: (c, g)),
                  pl.BlockSpec((l, ns), lambda g, c: (c, bb + g)),
                  pl.BlockSpec((l, ns), lambda g, c: (c, bb + SSM_GROUPS + g)),
                  pl.BlockSpec((None, l, 4), lambda g, c: (g, c, 0)),
                  pl.BlockSpec((None, l, 4), lambda g, c: (g, c, 0)),
                  pl.BlockSpec((None, None, 4, l), lambda g, c: (g, c, 0, 0)),
                  pl.BlockSpec((None, 1, gw), lambda g, c: (g, 0, 0))],
        out_specs=[pl.BlockSpec((l, gw), lambda g, c: (c, g)),
                   pl.BlockSpec((None, None, gw, ns), lambda g, c: (g, c, 0, 0))],
        out_shape=[_sds((s, xb * gw), F32), _sds((SSM_GROUPS, nc, gw, ns), F32)],
        scratch=[pltpu.VMEM((gw, ns), F32)],
        sem=("parallel", "arbitrary"))(xa, xa, xa, dtc, acsc, acsr, d_e)


def _ssd_bwd(dy, xa, dtc, acsc, acsr, d_e, hprev, name):
    s = xa.shape[0]
    l, gw, ns = SSM_CHUNK, SSM_GROUP_WIDTH, SSM_STATE
    nc = s // l
    bb = 2048 // ns

    def body(dy_ref, x_ref, b_ref, c_ref, dtc_ref, acsc_ref, acsr_ref, d_ref, hp_ref,
             dx_ref, db_ref, dc_ref, dacs_ref, ddt_ref, dd_ref, dh_sc):
        @pl.when(pl.program_id(1) == 0)
        def _():
            dh_sc[...] = jnp.zeros_like(dh_sc)
            dd_ref[...] = jnp.zeros_like(dd_ref)
        dyv, x = dy_ref[...], x_ref[...]
        bm, cm = b_ref[...].astype(BF16), c_ref[...].astype(BF16)
        acsc, acsr = acsc_ref[...], acsr_ref[...]
        lane, dt_e, e_e, dte_e, cd, cd_mat = _ssd_common(x, dtc_ref[...], acsc, acsr)
        xdt = x * dt_e
        xdt16 = xdt.astype(BF16)
        dy16 = dyv.astype(BF16)
        cb = _dot(cm, bm, NT_DIMS)
        cbt = _dot(bm, cm, NT_DIMS)
        hp = hp_ref[...]
        hp16 = hp.astype(BF16)
        g = dh_sc[...]
        g16 = g.astype(BF16)
        t_all = _dot(cm, hp16, NT_DIMS)
        dt16 = (dyv * e_e).astype(BF16)
        dc = _dot(dt16, hp16)
        dhp = _dot(dt16, cm, TN_DIMS)
        yoff_term = dyv * t_all * e_e
        wv = xdt * dte_e
        dw = _dot(bm, g16, NT_DIMS)
        db = _dot(wv.astype(BF16), g16)
        dxdt = dw * dte_e
        dte_term = dw * wv
        gh = g * hp
        dh_sc[...] = g * cd_mat + dhp
        ri = lax.broadcasted_iota(jnp.int32, (l, l), 0)
        ci = lax.broadcasted_iota(jnp.int32, (l, l), 1)
        tril, triu = ri >= ci, ci >= ri
        dcb = jnp.zeros((l, l), F32)
        dcbt = jnp.zeros((l, l), F32)
        q_rows, q_cols = [], []
        for r in range(4):
            in_head = (lane >= HEAD_DIM * r) & (lane < HEAD_DIM * (r + 1))
            lm = jnp.exp(jnp.where(tril, acsc[:, r:r + 1] - acsr[r:r + 1, :], -jnp.inf))
            lmt = jnp.exp(jnp.where(triu, acsr[r:r + 1, :] - acsc[:, r:r + 1], -jnp.inf))
            mm_, mt = cb * lm, cbt * lmt
            dyr = jnp.where(in_head, dy16, jnp.zeros_like(dy16))
            dm = _dot(dyr, xdt16, NT_DIMS)
            dmt = _dot(xdt16, dyr, NT_DIMS)
            dxdt = dxdt + jnp.where(in_head, _dot(mt.astype(BF16), dy16), 0.0)
            dcb = dcb + dm * lm
            dcbt = dcbt + dmt * lmt
            q_rows.append(jnp.sum(dm * mm_, axis=1, keepdims=True))
            q_cols.append(jnp.sum(dmt * mt, axis=1, keepdims=True))
        dc = dc + _dot(dcb.astype(BF16), bm)
        db = db + _dot(dcbt.astype(BF16), cm)
        dxdt_x = dxdt * x
        dy_x = dyv * x
        rowc = lax.broadcasted_iota(jnp.int32, (l, 1), 0)
        lane128 = lax.broadcasted_iota(jnp.int32, (1, LANES), 1)
        dd_row = jnp.zeros((1, LANES), F32)
        for r in range(4):
            in_head = (lane >= HEAD_DIM * r) & (lane < HEAD_DIM * (r + 1))
            seg = lambda v: jnp.sum(jnp.where(in_head, v, 0.0), axis=1, keepdims=True)
            s1, s2, s3 = seg(yoff_term), seg(dte_term), seg(dxdt_x)
            dcd = jnp.sum(_colsum(gh[HEAD_DIM * r:HEAD_DIM * (r + 1), :]), axis=1, keepdims=True)
            last_add = _colsum(s2) + dcd * cd[r:r + 1, :]
            dacs_r = q_rows[r] - q_cols[r] + s1 - s2 + jnp.where(rowc == l - 1, last_add, 0.0)
            dacs_ref[:, r:r + 1] = dacs_r
            ddt_ref[:, r:r + 1] = s3
            dd_row = dd_row + jnp.where(lane128 == r, _colsum(seg(dy_x)), 0.0)
        dd_ref[0:1, :] += dd_row
        dx_ref[...] = dxdt * dt_e + dyv * d_ref[...]
        db_ref[...] = db
        dc_ref[...] = dc

    rc = lambda c: nc - 1 - c
    return _pcall(
        body, name=name, grid=(SSM_GROUPS, nc),
        in_specs=[pl.BlockSpec((l, gw), lambda g, c: (rc(c), g)),
                  pl.BlockSpec((l, gw), lambda g, c: (rc(c), g)),
                  pl.BlockSpec((l, ns), lambda g, c: (rc(c), bb + g)),
                  pl.BlockSpec((l, ns), lambda g, c: (rc(c), bb + SSM_GROUPS + g)),
                  pl.BlockSpec((None, l, 4), lambda g, c: (g, rc(c), 0)),
                  pl.BlockSpec((None, l, 4), lambda g, c: (g, rc(c), 0)),
                  pl.BlockSpec((None, None, 4, l), lambda g, c: (g, rc(c), 0, 0)),
                  pl.BlockSpec((None, 1, gw), lambda g, c: (g, 0, 0)),
                  pl.BlockSpec((None, None, gw, ns), lambda g, c: (g, rc(c), 0, 0))],
        out_specs=[pl.BlockSpec((l, gw), lambda g, c: (rc(c), g)),
                   pl.BlockSpec((l, ns), lambda g, c: (rc(c), g)),
                   pl.BlockSpec((l, ns), lambda g, c: (rc(c), g)),
                   pl.BlockSpec((None, l, 4), lambda g, c: (g, rc(c), 0)),
                   pl.BlockSpec((None, l, 4), lambda g, c: (g, rc(c), 0)),
                   pl.BlockSpec((None, 8, LANES), lambda g, c: (g, 0, 0))],
        out_shape=[_sds((s, 2048), F32), _sds((s, SSM_GROUPS * ns), F32), _sds((s, SSM_GROUPS * ns), F32),
                   _sds((SSM_GROUPS, s, 4), F32), _sds((SSM_GROUPS, s, 4), F32), _sds((SSM_GROUPS, 8, LANES), F32)],
        scratch=[pltpu.VMEM((gw, ns), F32)],
        sem=("parallel", "arbitrary"))(dy, xa, xa, xa, dtc, acsc, acsr, d_e, hprev)


def _ssd_constants():
    src = jnp.arange(LANES)[:, None]
    expand = (src == jnp.arange(SSM_HEADS * HEAD_DIM)[None, :] // HEAD_DIM).astype(BF16)
    seg = (jnp.arange(SSM_GROUP_WIDTH)[:, None] // HEAD_DIM == jnp.arange(LANES)[None, :]).astype(BF16)
    seg4 = (jnp.arange(4 * LANES)[:, None] // LANES == jnp.arange(LANES)[None, :]).astype(BF16)
    return dict(expand=expand, seg=seg, seg4=seg4)


def _ssm_weights(w_in, conv_w, conv_b, dt_bias, a_log, d_skip, norm_w, w_out):
    pad = ((0, 0), (0, LANES - SSM_HEADS))
    w_xbc = _group_cols(w_in[:, 2048:6144])
    w_dt = jnp.pad(w_in[:, 6144:], pad)
    return dict(
        ssm_z=w_in[:, :2048], ssm_xbc=w_xbc, ssm_dt=w_dt,
        ssm_zx=jnp.concatenate([w_in[:, :2048], w_xbc], axis=1),
        ssm_out=w_out,
        conv_w8=_group_cols(jnp.pad(conv_w, ((0, 8 - SSM_CONV), (0, 0)))),
        conv_b=_group_cols(conv_b), norm_w=norm_w,
        dt_bias=jnp.pad(dt_bias, pad), a_log=jnp.pad(a_log, pad),
        d_e=jnp.repeat(d_skip.reshape(SSM_GROUPS, 4), HEAD_DIM, axis=1)[:, None, :],
        ssd_cst=_ssd_constants())


def _ssd_setup(acs_e, acsr):
    l = SSM_CHUNK
    last = acsr[:, l - 1:l]
    lane1 = lax.broadcasted_iota(jnp.int32, (1, SSM_GROUP_WIDTH), 1)
    last_e = last[3:4, :]
    for r in (2, 1, 0):
        last_e = jnp.where(lane1 < HEAD_DIM * (r + 1), last[r:r + 1, :], last_e)
    return jnp.exp(acs_e), jnp.exp(last_e - acs_e), jnp.exp(last_e)


def _head_bcast(acs_e):
    lo = lax.broadcasted_iota(jnp.int32, (acs_e.shape[0], LANES), 1) < HEAD_DIM
    out = []
    for p in range(2):
        blk = acs_e[:, p * LANES:(p + 1) * LANES]
        rolled = pltpu.roll(blk, HEAD_DIM, 1)
        out += [jnp.where(lo, blk, rolled), jnp.where(lo, rolled, blk)]
    return out


def _group_cols(a):
    lead = a.shape[:-1]
    x = a[..., :2048].reshape(lead + (SSM_GROUPS, SSM_GROUP_WIDTH))
    b = a[..., 2048:3072].reshape(lead + (SSM_GROUPS, SSM_STATE))
    c = a[..., 3072:].reshape(lead + (SSM_GROUPS, SSM_STATE))
    return jnp.concatenate([x, b, c], axis=-1).reshape(lead + (4096,))


def _ungroup_cols(a):
    lead = a.shape[:-1]
    y = a.reshape(lead + (SSM_GROUPS, SSM_GROUP_WIDTH + 2 * SSM_STATE))
    return jnp.concatenate([y[..., :256].reshape(lead + (2048,)), y[..., 256:384].reshape(lead + (1024,)),
                            y[..., 384:].reshape(lead + (1024,))], axis=-1)


def _ssd_fwd2(xa, dte, acse, acsr, d_e, name):
    s = xa.shape[0]
    l, gw, ns = SSM_CHUNK, SSM_GROUP_WIDTH, SSM_STATE
    nc = s // l

    gb = gw + 2 * ns
    gp = SSD_GROUPS_PER_STEP

    def body(xa_ref, dt_ref, acs_ref, acsr_ref, d_ref, y_ref, hp_ref, h_sc):
        @pl.when(pl.program_id(1) == 0)
        def _():
            h_sc[...] = jnp.zeros_like(h_sc)
        lane = lax.broadcasted_iota(jnp.int32, (l, gw), 1)
        tril = _causal(l)
        for gi in range(gp):
            x = xa_ref[:, gi * gb:gi * gb + gw]
            bm = xa_ref[:, gi * gb + gw:gi * gb + gw + ns].astype(BF16)
            cm = xa_ref[:, gi * gb + gw + ns:(gi + 1) * gb].astype(BF16)
            acsr = acsr_ref[gi]
            dt_e, acs_e = dt_ref[:, gi * gw:(gi + 1) * gw], acs_ref[:, gi * gw:(gi + 1) * gw]
            acs_bc = _head_bcast(acs_e)
            e_e, dte_e, cd_e = _ssd_setup(acs_e, acsr)
            xdt = x * dt_e
            xdt16 = xdt.astype(BF16)
            cb = _dot(cm, bm, NT_DIMS)
            yd = jnp.zeros((l, gw), F32)
            for r in range(4):
                lm = jnp.exp(jnp.where(tril, acs_bc[r] - acsr[r:r + 1, :], -jnp.inf))
                yr = _dot((cb * lm).astype(BF16), xdt16)
                yd = jnp.where((lane >= HEAD_DIM * r) & (lane < HEAD_DIM * (r + 1)), yr, yd)
            hp = h_sc[gi]
            hp_ref[gi] = hp
            y_ref[:, gi * gw:(gi + 1) * gw] = yd + _dot(cm, hp.astype(BF16)) * e_e + x * d_ref[gi]
            h_sc[gi] = hp * cd_e + _dot(bm, (xdt * dte_e).astype(BF16), TN_DIMS)

    return _pcall(
        body, name=name, grid=(SSM_GROUPS // gp, nc),
        in_specs=[pl.BlockSpec((l, gp * gb), lambda g, c: (c, g)),
                  pl.BlockSpec((l, gp * gw), lambda g, c: (c, g)),
                  pl.BlockSpec((l, gp * gw), lambda g, c: (c, g)),
                  pl.BlockSpec((gp, None, 4, l), lambda g, c: (g, c, 0, 0)),
                  pl.BlockSpec((gp, 1, gw), lambda g, c: (g, 0, 0))],
        out_specs=[pl.BlockSpec((l, gp * gw), lambda g, c: (c, g)),
                   pl.BlockSpec((gp, None, ns, gw), lambda g, c: (g, c, 0, 0))],
        out_shape=[_sds((s, 2048), F32), _sds((SSM_GROUPS, nc, ns, gw), F32)],
        scratch=[pltpu.VMEM((gp, ns, gw), F32)],
        sem=("parallel", "arbitrary"))(xa, dte, acse, acsr, d_e)


def _ssd_bwd2(dy, xa, dte, acse, acsr, d_e, hprev, cst, name):
    s = xa.shape[0]
    l, gw, ns = SSM_CHUNK, SSM_GROUP_WIDTH, SSM_STATE
    nc = s // l

    gb = gw + 2 * ns
    gp = SSD_GROUPS_PER_STEP

    def body(dy_ref, xa_ref, dt_ref, acs_ref, acsr_ref, d_ref, hp_ref,
             seg_ref, seg4_ref, dxa_ref, dacs_ref, ddt_ref, dd_ref, dh_sc):
        @pl.when(pl.program_id(1) == 0)
        def _():
            dh_sc[...] = jnp.zeros_like(dh_sc)
            dd_ref[...] = jnp.zeros_like(dd_ref)
        for gi in range(gp):
            one_group(gi, dy_ref, xa_ref, dt_ref, acs_ref, acsr_ref, d_ref, hp_ref, seg_ref, seg4_ref,
                      dxa_ref, dacs_ref, ddt_ref, dd_ref, dh_sc)

    def one_group(gi, dy_ref, xa_ref, dt_ref, acs_ref, acsr_ref, d_ref, hp_ref, seg_ref, seg4_ref,
                  dxa_ref, dacs_ref, ddt_ref, dd_ref, dh_sc):
        dyv = dy_ref[:, gi * gw:(gi + 1) * gw]
        x = xa_ref[:, gi * gb:gi * gb + gw]
        bm = xa_ref[:, gi * gb + gw:gi * gb + gw + ns].astype(BF16)
        cm = xa_ref[:, gi * gb + gw + ns:(gi + 1) * gb].astype(BF16)
        acsr = acsr_ref[gi]
        dt_e, acs_e = dt_ref[:, gi * gw:(gi + 1) * gw], acs_ref[:, gi * gw:(gi + 1) * gw]
        acs_bc = _head_bcast(acs_e)
        e_e, dte_e, cd_e = _ssd_setup(acs_e, acsr)
        seg, seg4 = seg_ref[...], seg4_ref[...]
        lane = lax.broadcasted_iota(jnp.int32, (l, gw), 1)
        xdt = x * dt_e
        xdt16 = xdt.astype(BF16)
        dy16 = dyv.astype(BF16)
        cb = _dot(cm, bm, NT_DIMS)
        cbt = _dot(bm, cm, NT_DIMS)
        hp = hp_ref[gi]
        hp16 = hp.astype(BF16)
        g = dh_sc[gi]
        g16 = g.astype(BF16)
        t_all = _dot(cm, hp16)
        dt16 = (dyv * e_e).astype(BF16)
        dc = _dot(dt16, hp16, NT_DIMS)
        dhp = _dot(cm, dt16, TN_DIMS)
        wv = xdt * dte_e
        dw = _dot(bm, g16)
        db = _dot(wv.astype(BF16), g16, NT_DIMS)
        dxdt = dw * dte_e
        acs_term = dyv * t_all * e_e - dw * wv
        last_term = _colsum(dw * wv) + _colsum(g * hp) * cd_e
        dh_sc[gi] = g * cd_e + dhp
        tril, triu = _causal(l), _causal(l, transposed=True)
        dcb = jnp.zeros((l, l), F32)
        dcbt = jnp.zeros((l, l), F32)
        qd = []
        for r in range(4):
            in_head = (lane >= HEAD_DIM * r) & (lane < HEAD_DIM * (r + 1))
            a_col = acs_bc[r]
            lm = jnp.exp(jnp.where(tril, a_col - acsr[r:r + 1, :], -jnp.inf))
            lmt = jnp.exp(jnp.where(triu, acsr[r:r + 1, :] - a_col, -jnp.inf))
            mm_, mt = cb * lm, cbt * lmt
            dyr = jnp.where(in_head, dy16, jnp.zeros_like(dy16))
            dm = _dot(dyr, xdt16, NT_DIMS)
            dmt = _dot(xdt16, dyr, NT_DIMS)
            dxdt = dxdt + jnp.where(in_head, _dot(mt.astype(BF16), dy16), 0.0)
            dcb = dcb + dm * lm
            dcbt = dcbt + dmt * lmt
            qd.append(dm * mm_ - dmt * mt)
        dc = dc + _dot(dcb.astype(BF16), bm)
        db = db + _dot(dcbt.astype(BF16), cm)
        rowl = lax.broadcasted_iota(jnp.int32, (l, LANES), 0)
        row8 = lax.broadcasted_iota(jnp.int32, (8, gw), 0)
        small = _dot_split(jnp.where(row8 == 0, last_term, jnp.where(row8 == 1, _colsum(dyv * x), 0.0)), seg, passes=2)
        big = _dot_split(jnp.concatenate([acs_term, dxdt * x], axis=0), seg, passes=2)
        dacs = (big[0:l] + _dot_split(jnp.concatenate(qd, axis=1), seg4, passes=2)
                + jnp.where(rowl == l - 1, small[0:1, :], 0.0))
        dacs_ref[gi] = dacs[:, 0:4]
        ddt_ref[gi] = big[l:2 * l, 0:4]
        dd_ref[gi, 0:1, :] += small[1:2, :]
        dxa_ref[:, gi * gb:(gi + 1) * gb] = jnp.concatenate([dxdt * dt_e + dyv * d_ref[gi], db, dc], axis=1)

    rc = lambda c: nc - 1 - c
    return _pcall(
        body, name=name, grid=(SSM_GROUPS // gp, nc),
        in_specs=[pl.BlockSpec((l, gp * gw), lambda g, c: (rc(c), g)),
                  pl.BlockSpec((l, gp * gb), lambda g, c: (rc(c), g)),
                  pl.BlockSpec((l, gp * gw), lambda g, c: (rc(c), g)),
                  pl.BlockSpec((l, gp * gw), lambda g, c: (rc(c), g)),
                  pl.BlockSpec((gp, None, 4, l), lambda g, c: (g, rc(c), 0, 0)),
                  pl.BlockSpec((gp, 1, gw), lambda g, c: (g, 0, 0)),
                  pl.BlockSpec((gp, None, ns, gw), lambda g, c: (g, rc(c), 0, 0)),
                  pl.BlockSpec((gw, LANES), lambda g, c: (0, 0)),
                  pl.BlockSpec((4 * LANES, LANES), lambda g, c: (0, 0))],
        out_specs=[pl.BlockSpec((l, gp * gb), lambda g, c: (rc(c), g)),
                   pl.BlockSpec((gp, l, 4), lambda g, c: (g, rc(c), 0)),
                   pl.BlockSpec((gp, l, 4), lambda g, c: (g, rc(c), 0)),
                   pl.BlockSpec((gp, 8, LANES), lambda g, c: (g, 0, 0))],
        out_shape=[_sds((s, 4096), F32), _sds((SSM_GROUPS, s, 4), F32), _sds((SSM_GROUPS, s, 4), F32),
                   _sds((SSM_GROUPS, 8, LANES), F32)],
        scratch=[pltpu.VMEM((gp, ns, gw), F32)],
        sem=("parallel", "arbitrary"))(dy, xa, dte, acse, acsr, d_e, hprev, cst["seg"], cst["seg4"])


def _gate_norm(y, z, nw, name):
    c = y.shape[1]

    def fn(yv, zv, w):
        outs = []
        for k in range(c // SSM_GROUP_WIDTH):
            sl = slice(k * SSM_GROUP_WIDTH, (k + 1) * SSM_GROUP_WIDTH)
            yg = yv[:, sl] * _silu(zv[:, sl])
            rinv = lax.rsqrt(jnp.mean(yg * yg, axis=-1, keepdims=True) + RMS_EPS)
            outs.append(yg * rinv * w[:, sl])
        return (jnp.concatenate(outs, axis=1),)
    return _rowwise(fn, [y, z], [nw], [(c, BF16)], [], name=name)[0]


def _gate_norm_bwd(dyn, y, z, nw, name):
    c = y.shape[1]

    def fn(dv, yv, zv, w):
        dys, dzs, dws = [], [], []
        for k in range(c // SSM_GROUP_WIDTH):
            sl = slice(k * SSM_GROUP_WIDTH, (k + 1) * SSM_GROUP_WIDTH)
            ys, zs, ds = yv[:, sl], zv[:, sl], dv[:, sl]
            sz = _silu(zs)
            yg = ys * sz
            rinv = lax.rsqrt(jnp.mean(yg * yg, axis=-1, keepdims=True) + RMS_EPS)
            nrm = yg * rinv
            dn = ds * w[:, sl]
            dyg = rinv * (dn - nrm * jnp.mean(dn * nrm, axis=-1, keepdims=True))
            dys.append(dyg * sz)
            dzs.append(dyg * ys * _dsilu(zs))
            dws.append(_colsum(ds * nrm))
        return jnp.concatenate(dys, axis=1), jnp.concatenate(dzs, axis=1), jnp.concatenate(dws, axis=1)
    return _rowwise(fn, [dyn, y, z], [nw], [(c, F32), (c, BF16)], [(1, c)], name=name, tm=128)


def _ssd_forward(u, w):
    s = u.shape[0]
    z = _mm(u, w["ssm_z"], "nn", [F32], name="ssm_zproj")
    xpre = _mm(u, w["ssm_xbc"], "nn", [F32], name="ssm_xproj")
    dtraw = _mm(u, w["ssm_dt"], "nn", [F32], name="ssm_dtproj")
    xc, xa = _conv_fwd(xpre, w["conv_w8"], w["conv_b"], "ssm_conv")
    dt, acs, dte, acse = _ssd_pre(dtraw, w["dt_bias"], w["a_log"], w["ssd_cst"], "ssm_pre")
    acsr = _heads_rows(acs, s)
    y, hprev = _ssd_fwd2(xa, dte, acse, acsr, w["d_e"], "ssm_scan")
    yn = _gate_norm(y, z, w["norm_w"], "ssm_gate_norm")
    out = _mm(yn, w["ssm_out"], "nn", [F32], name="ssm_oproj")
    return out, dict(u=u, z=z, xpre=xpre, xc=xc, xa=xa, dtraw=dtraw, dt=dt, dte=dte, acse=acse,
                     acsr=acsr, y=y, hprev=hprev, yn=yn)


def _ssd_backward(dy, sv, w):
    s = dy.shape[0]
    dyn = _mm(dy, w["ssm_out"], "nt", [F32], name="ssm_dyn")
    g_wout = _mm(sv["yn"], dy, "tn", [F32], name="ssm_gwout")
    dys, dz, dnw = _gate_norm_bwd(dyn, sv["y"], sv["z"], w["norm_w"], "ssm_gate_norm_bwd")
    dxa, dacs_c, ddt_c, dd = _ssd_bwd2(dys, sv["xa"], sv["dte"], sv["acse"], sv["acsr"], w["d_e"], sv["hprev"],
                                       w["ssd_cst"], "ssm_scan_bwd")
    pad = ((0, 0), (0, LANES - SSM_HEADS))
    dacs = jnp.pad(dacs_c.transpose(1, 0, 2).reshape(s, SSM_HEADS), pad)
    ddt = jnp.pad(ddt_c.transpose(1, 0, 2).reshape(s, SSM_HEADS), pad)
    draw, dbias, dalog = _ssd_post(dacs, ddt, sv["dtraw"], sv["dt"], w["dt_bias"], w["a_log"], "ssm_post")
    dxpre, dcw, dcb = _conv_bwd(dxa, sv["xc"], sv["xpre"], w["conv_w8"], "ssm_conv_bwd")
    du = _mm(draw, w["ssm_dt"], "nt", [F32], name="ssm_du_dt")
    du = _mm_nt_blocks(_k_blocks(dz) + _k_blocks(dxpre), w["ssm_zx"], du, name="ssm_du", tm=512)
    g_win = jnp.concatenate([_mm(sv["u"], dz, "tn", [F32], name="ssm_gwin_z"),
                             _ungroup_cols(_mm(sv["u"], dxpre, "tn", [F32], name="ssm_gwin_x")),
                             _mm(sv["u"], draw, "tn", [F32], name="ssm_gwin_dt")[:, :SSM_HEADS]], axis=1)
    return du, dict(ssm_w_in=g_win, ssm_w_out=g_wout, ssm_conv_w=_ungroup_cols(dcw[:SSM_CONV]),
                    ssm_conv_b=_ungroup_cols(dcb), ssm_norm_w=dnw, ssm_dt_bias=dbias[:, :SSM_HEADS],
                    ssm_a_log=dalog[:, :SSM_HEADS], ssm_d=dd[:, 0, :4].reshape(1, SSM_HEADS))


def _mlp_forward(u2, w1, w2, tag):
    def epi(acc):
        hr = jnp.maximum(acc, 0.0)
        return hr, hr * hr
    hr, a = _mm(u2, w1, "nn", [BF16, BF16], name=tag + "_mlp_up", epi=epi)
    y2 = _mm(a, w2, "nn", [F32], name=tag + "_mlp_down")
    return y2, hr, a


def _mlp_backward(dy2, u2, hr, a, w1, w2, tag):
    dh = _mm(dy2, w2, "nt", [BF16], name=tag + "_mlp_dh", extra=(hr,),
             epi=lambda acc, h: (acc * (2.0 * h.astype(F32)),))
    g_w2 = _mm(a, dy2, "tn", [F32], name=tag + "_mlp_gw2")
    g_w1 = _mm(u2, dh, "tn", [F32], name=tag + "_mlp_gw1")
    du2 = _mm(dh, w1, "nt", [F32], name=tag + "_mlp_du")
    return du2, g_w1, g_w2


def _ada_forward(c16, ada_w, ada_b_cols, name):
    nl, d, cols = ada_w.shape
    tn = 512

    def body(c_ref, w_ref, b_ref, o_ref):
        cond = _silu(c_ref[...]).astype(BF16)
        o_ref[...] = _dot(cond, w_ref[...].astype(BF16)) + b_ref[...]

    return _pcall(body, name=name, grid=(nl, cols // tn),
                  in_specs=[pl.BlockSpec((16, d), lambda i, j: (0, 0)),
                            pl.BlockSpec((None, d, tn), lambda i, j: (i, 0, j)),
                            pl.BlockSpec((None, 1, tn), lambda i, j: (i, 0, j))],
                  out_specs=pl.BlockSpec((None, 16, tn), lambda i, j: (i, 0, j)),
                  out_shape=_sds((nl, 16, cols), F32), sem=("parallel", "parallel"))(c16, ada_w, ada_b_cols)


def _ada_backward(c_t, dmod_cols, name):
    d, nb = c_t.shape
    nl, _, cols = dmod_cols.shape
    tn = 512

    def body(c_ref, dm_ref, o_ref):
        cond = _silu(c_ref[...])
        dm = dm_ref[...]
        acc = cond[:, 0:1] * dm[0:1, :]
        for b in range(1, nb):
            acc = acc + cond[:, b:b + 1] * dm[b:b + 1, :]
        o_ref[...] = acc

    return _pcall(body, name=name, grid=(nl, cols // tn),
                  in_specs=[pl.BlockSpec((d, nb), lambda i, j: (0, 0)),
                            pl.BlockSpec((None, nb, tn), lambda i, j: (i, 0, j))],
                  out_specs=pl.BlockSpec((None, d, tn), lambda i, j: (i, 0, j)),
                  out_shape=_sds((nl, d, cols), F32), sem=("parallel", "parallel"))(c_t, dmod_cols)


def _adamw(w, g, m, v, name):
    rows, cols = w.shape
    tm = rows
    for cand in (256, 128, 64, 32, 16, 8):
        if rows % cand == 0 and rows > cand:
            tm = cand
            break
    c1 = 1.0 / (1.0 - ADAM_B1 ** ADAM_STEP)
    c2 = 1.0 / (1.0 - ADAM_B2 ** ADAM_STEP)

    def fn(wv, gv, mv, vv):
        mn = ADAM_B1 * mv + (1.0 - ADAM_B1) * gv
        vn = ADAM_B2 * vv + (1.0 - ADAM_B2) * (gv * gv)
        delta = -ADAM_LR * ((mn * c1) / (jnp.sqrt(vn * c2) + ADAM_EPS) + ADAM_WD * wv)
        return delta, mn, vn
    return _rowwise(fn, [w, g, m, v], [], [(cols, F32)] * 3, [], name=name, tm=tm)


def _my_pos():
    return lax.axis_index("x"), lax.axis_index("y"), lax.axis_index("c")


def _allgather8(x, name):
    r, c = x.shape

    def body(x_ref, out_ref, send_sems, recv_sems, local_sem):
        mx, my, mc = _my_pos()
        me = 4 * mx + 2 * my + mc
        mine = pltpu.make_async_copy(x_ref, out_ref.at[me], local_sem)
        mine.start()
        copies = []
        for k in range(1, 8):
            fx, fy, fc = (k >> 2) & 1, (k >> 1) & 1, k & 1
            px = 1 - mx if fx else mx
            py = 1 - my if fy else my
            pc = 1 - mc if fc else mc
            peer = 4 * px + 2 * py + pc
            send = pltpu.make_async_remote_copy(src_ref=x_ref, dst_ref=out_ref.at[me], send_sem=send_sems.at[k - 1],
                                                recv_sem=recv_sems.at[k - 1], device_id=(px, py, pc),
                                                device_id_type=MESH)
            send.start()
            recv = pltpu.make_async_remote_copy(src_ref=x_ref, dst_ref=out_ref.at[peer], send_sem=send_sems.at[k - 1],
                                                recv_sem=recv_sems.at[k - 1], device_id=(px, py, pc),
                                                device_id_type=MESH)
            copies.append((send, recv))
        for send, recv in copies:
            recv.wait_recv()
        for send, recv in copies:
            send.wait_send()
        mine.wait()

    vm = pl.BlockSpec(memory_space=pltpu.VMEM)
    return _pcall(body, name=name, in_specs=[vm], out_specs=vm, out_shape=_sds((8, r, c), x.dtype),
                  scratch=[pltpu.SemaphoreType.DMA((7,)), pltpu.SemaphoreType.DMA((7,)), pltpu.SemaphoreType.DMA])(x)


def _chip_flips(mx, my):
    out = []
    for fx, fy in ((1, 0), (0, 1), (1, 1)):
        px = 1 - mx if fx else mx
        py = 1 - my if fy else my
        out.append((px, py, 2 * px + py))
    return out


def _gather_chips(shard2, name):
    _, h, c = shard2.shape

    def body(x_ref, out_ref, send_sems, recv_sems):
        mx, my, mc = _my_pos()
        oc = 1 - mc
        mk = 2 * mx + my
        flips = _chip_flips(mx, my)

        def copy(k, src, dst, to):
            return pltpu.make_async_remote_copy(src_ref=src, dst_ref=dst, send_sem=send_sems.at[k],
                                                recv_sem=recv_sems.at[k], device_id=to, device_id_type=MESH)

        first = [copy(j, x_ref.at[mc], out_ref.at[mk, mc], (px, py, mc)) for j, (px, py, pk) in enumerate(flips)]
        for cp in first:
            cp.start()
        passed = []
        for j, (px, py, pk) in enumerate(flips):
            copy(j, x_ref.at[mc], out_ref.at[pk, mc], (px, py, mc)).wait_recv()
            fw = copy(3 + j, out_ref.at[pk, mc], out_ref.at[pk, mc], (mx, my, oc))
            fw.start()
            passed.append(fw)
        for j, (px, py, pk) in enumerate(flips):
            copy(3 + j, out_ref.at[pk, oc], out_ref.at[pk, oc], (mx, my, oc)).wait_recv()
        for cp in first + passed:
            cp.wait_send()

    return _pcall(body, name=name, in_specs=[HBM_SPEC], out_specs=HBM_SPEC, out_shape=_sds((4, 2, h, c), shard2.dtype),
                  scratch=[pltpu.SemaphoreType.DMA((6,)), pltpu.SemaphoreType.DMA((6,))])(shard2)


def _pair_exchange(g4, name):
    n, _, h, c = g4.shape

    def body(g_ref, out_ref, send_sem, recv_sem):
        mx, my, mc = _my_pos()
        oc = 1 - mc
        copies = []
        for k in range(n):
            cp = pltpu.make_async_remote_copy(src_ref=g_ref.at[k, oc], dst_ref=out_ref.at[k], send_sem=send_sem.at[k],
                                              recv_sem=recv_sem.at[k], device_id=(mx, my, oc), device_id_type=MESH)
            cp.start()
            copies.append(cp)
        for cp in copies:
            cp.wait_recv()
        for cp in copies:
            cp.wait_send()

    return _pcall(body, name=name, in_specs=[HBM_SPEC], out_specs=HBM_SPEC, out_shape=_sds((n, h, c), g4.dtype),
                  scratch=[pltpu.SemaphoreType.DMA((n,)), pltpu.SemaphoreType.DMA((n,))])(g4)


def _pair_add(g4, recv, core, name):
    n, _, h, c = g4.shape
    tm = _row_tile(h)

    def body(core_ref, a_ref, b_ref, o_ref, o16_ref):
        acc = a_ref[...] + b_ref[...]
        o_ref[...] = acc
        o16_ref[...] = acc.astype(BF16)

    out_spec = pl.BlockSpec((None, tm, c), lambda k, i, cr: (k, i, 0))
    return _pcall(body, name=name, grid=(n, h // tm), prefetch=1,
                  in_specs=[pl.BlockSpec((None, None, tm, c), lambda k, i, cr: (k, cr[0], i, 0)), out_spec],
                  out_specs=[out_spec, out_spec], out_shape=[_sds((n, h, c), F32), _sds((n, h, c), BF16)],
                  sem=("parallel", "parallel"))(core, g4, recv)


def _chip_exchange(p, name):
    n, h, c = p.shape

    def body(p_ref, out_ref, send_sems, recv_sems):
        mx, my, mc = _my_pos()
        copies = []
        for j, (px, py, pk) in enumerate(_chip_flips(mx, my)):
            cp = pltpu.make_async_remote_copy(src_ref=p_ref.at[pk], dst_ref=out_ref.at[j], send_sem=send_sems.at[j],
                                              recv_sem=recv_sems.at[j], device_id=(px, py, mc), device_id_type=MESH)
            cp.start()
            copies.append(cp)
        for cp in copies:
            cp.wait_recv()
        for cp in copies:
            cp.wait_send()

    return _pcall(body, name=name, in_specs=[HBM_SPEC], out_specs=HBM_SPEC, out_shape=_sds((3, h, c), p.dtype),
                  scratch=[pltpu.SemaphoreType.DMA((3,)), pltpu.SemaphoreType.DMA((3,))])(p)


def _chip_sum(p, slots, chip, name):
    _, h, c = p.shape
    tm = _row_tile(h)

    def body(chip_ref, p_ref, q_ref, o_ref):
        o_ref[...] = ((p_ref[...] + q_ref[0].astype(F32)) + q_ref[1].astype(F32)) + q_ref[2].astype(F32)

    return _pcall(body, name=name, grid=(h // tm,), prefetch=1,
                  in_specs=[pl.BlockSpec((None, tm, c), lambda i, ch: (ch[0], i, 0)),
                            pl.BlockSpec((3, tm, c), lambda i, ch: (0, i, 0))],
                  out_specs=pl.BlockSpec((tm, c), lambda i, ch: (i, 0)),
                  out_shape=_sds((h, c), F32), sem=("parallel",))(chip, p, slots)


def _sum_slots(q, name):
    n, h, c = q.shape
    tm = _row_tile(h)

    def body(q_ref, o_ref):
        acc = q_ref[0]
        for k in range(1, n):
            acc = acc + q_ref[k]
        o_ref[...] = acc

    return _pcall(body, name=name, grid=(h // tm,),
                  in_specs=[pl.BlockSpec((n, tm, c), lambda i: (0, i, 0))],
                  out_specs=pl.BlockSpec((tm, c), lambda i: (i, 0)),
                  out_shape=_sds((h, c), F32), sem=("parallel",))(q)


def _pair_share(f, name):
    h, c = f.shape

    def body(f_ref, out_ref, send_sem, recv_sem):
        mx, my, mc = _my_pos()
        cp = pltpu.make_async_remote_copy(src_ref=f_ref, dst_ref=out_ref, send_sem=send_sem, recv_sem=recv_sem,
                                          device_id=(mx, my, 1 - mc), device_id_type=MESH)
        cp.start()
        cp.wait_recv()
        cp.wait_send()

    return _pcall(body, name=name, in_specs=[HBM_SPEC], out_specs=HBM_SPEC, out_shape=_sds((h, c), f.dtype),
                  scratch=[pltpu.SemaphoreType.DMA, pltpu.SemaphoreType.DMA])(f)


BIG = ("mlp_w1", "mlp_w2", "fox_w_in", "fox_w_o", "ssm_w_in", "ssm_w_out")
SMALL_SHARDED = ("ssm_conv_w", "ssm_conv_b", "ssm_norm_w")
PACK_COLS = 1024


def _pack_rows(parts, rows_multiple, dtype):
    flat = jnp.concatenate([p.reshape(-1).astype(dtype) for p in parts])
    unit = rows_multiple * PACK_COLS
    total = -(-flat.shape[0] // unit) * unit
    flat = jnp.pad(flat, (0, total - flat.shape[0]))
    return flat.reshape(total // PACK_COLS, PACK_COLS)


def _unpack(flat, shapes):
    out, off = [], 0
    for sh in shapes:
        n = 1
        for d_ in sh:
            n *= d_
        out.append(flat[off:off + n].reshape(sh))
        off += n
    return out


PIECE_ROWS = 16


def _piece_rows(shape):
    n = 1
    for d_ in shape:
        n *= d_
    rows = -(-n // PACK_COLS)
    return n, -(-rows // PIECE_ROWS) * PIECE_ROWS


def _pack2d(parts, rows_multiple, dtype):
    blocks = []
    for p in parts:
        n, rows = _piece_rows(p.shape)
        a = p.astype(dtype)
        if p.shape[-1] != PACK_COLS or n % PACK_COLS:
            a = jnp.pad(a.reshape(-1), (0, -n % PACK_COLS))
        a = a.reshape(-1, PACK_COLS)
        blocks.append(jnp.pad(a, ((0, rows - a.shape[0]), (0, 0))))
    total = sum(b.shape[0] for b in blocks)
    pad = -total % rows_multiple
    if pad:
        blocks.append(jnp.zeros((pad, PACK_COLS), dtype))
    return jnp.concatenate(blocks, axis=0)


def _unpack2d(buf, shapes):
    out, off = [], 0
    for sh in shapes:
        n, rows = _piece_rows(sh)
        piece = buf[off:off + rows]
        if sh[-1] == PACK_COLS and n % PACK_COLS == 0:
            out.append(piece[:n // PACK_COLS].reshape(sh))
        else:
            out.append(piece.reshape(-1)[:n].reshape(sh))
        off += rows
    return out


def _row_tile(h, cap=512):
    for step in (16, 8):
        best = 0
        for cand in range(step, cap + 1, step):
            if h % cand == 0:
                best = cand
        if best:
            return best
    return h


def _chip_slice(full, axis, k, width):
    idx = [slice(None)] * full.ndim
    idx[axis] = slice(k * width, (k + 1) * width)
    return full[tuple(idx)]


SHARD_AXIS = dict(mlp_w1=2, mlp_w2=1, fox_w_in=2, fox_w_o=1, ssm_w_in=2, ssm_w_out=1, ssm_conv_w=2,
                  ssm_conv_b=1, ssm_norm_w=1, ada_w=2)


def kernel(x, c, ada_w, ada_b, ln_mix_g, ln_mix_b, ln_mlp_g, ln_mlp_b, mlp_w1, mlp_w2, fox_w_in, fox_b_f, fox_w_o, ssm_w_in, ssm_conv_w, ssm_conv_b, ssm_dt_bias, ssm_a_log, ssm_d, ssm_norm_w, ssm_w_out, loss_target, m_ada_w, m_ada_b, m_ln_mix_g, m_ln_mix_b, m_ln_mlp_g, m_ln_mlp_b, m_mlp_w1, m_mlp_w2, m_fox_w_in, m_fox_b_f, m_fox_w_o, m_ssm_w_in, m_ssm_conv_w, m_ssm_conv_b, m_ssm_dt_bias, m_ssm_a_log, m_ssm_d, m_ssm_norm_w, m_ssm_w_out, v_ada_w, v_ada_b, v_ln_mix_g, v_ln_mix_b, v_ln_mlp_g, v_ln_mlp_b, v_mlp_w1, v_mlp_w2, v_fox_w_in, v_fox_b_f, v_fox_w_o, v_ssm_w_in, v_ssm_conv_w, v_ssm_conv_b, v_ssm_dt_bias, v_ssm_a_log, v_ssm_d, v_ssm_norm_w, v_ssm_w_out):
    names = ("ada_w", "ada_b", "ln_mix_g", "ln_mix_b", "ln_mlp_g", "ln_mlp_b", "mlp_w1", "mlp_w2", "fox_w_in",
             "fox_b_f", "fox_w_o", "ssm_w_in", "ssm_conv_w", "ssm_conv_b", "ssm_dt_bias", "ssm_a_log", "ssm_d",
             "ssm_norm_w", "ssm_w_out")
    weights = dict(zip(names, (ada_w, ada_b, ln_mix_g, ln_mix_b, ln_mlp_g, ln_mlp_b, mlp_w1, mlp_w2, fox_w_in,
                               fox_b_f, fox_w_o, ssm_w_in, ssm_conv_w, ssm_conv_b, ssm_dt_bias, ssm_a_log, ssm_d,
                               ssm_norm_w, ssm_w_out)))
    m_in = dict(zip(names, (m_ada_w, m_ada_b, m_ln_mix_g, m_ln_mix_b, m_ln_mlp_g, m_ln_mlp_b, m_mlp_w1, m_mlp_w2,
                            m_fox_w_in, m_fox_b_f, m_fox_w_o, m_ssm_w_in, m_ssm_conv_w, m_ssm_conv_b, m_ssm_dt_bias,
                            m_ssm_a_log, m_ssm_d, m_ssm_norm_w, m_ssm_w_out)))
    v_in = dict(zip(names, (v_ada_w, v_ada_b, v_ln_mix_g, v_ln_mix_b, v_ln_mlp_g, v_ln_mlp_b, v_mlp_w1, v_mlp_w2,
                            v_fox_w_in, v_fox_b_f, v_fox_w_o, v_ssm_w_in, v_ssm_conv_w, v_ssm_conv_b, v_ssm_dt_bias,
                            v_ssm_a_log, v_ssm_d, v_ssm_norm_w, v_ssm_w_out)))

    mx, my, mc = _my_pos()
    chip = 2 * mx + my
    me = 4 * mx + 2 * my + mc
    x0 = x[0]
    target = loss_target[0]
    s, d = x0.shape
    n_qkv = 3 * FOX_HEADS * HEAD_DIM

    big_shapes = [weights[n].shape for n in BIG]
    packed = _pack2d([weights[n] for n in BIG], 32, BF16)
    gathered = _gather_chips(packed.reshape(2, packed.shape[0] // 2, PACK_COLS), "gather_weights")
    gathered = gathered.reshape(4, packed.shape[0], PACK_COLS)
    per_chip = [_unpack2d(jnp.where(chip == k, packed, gathered[k]), big_shapes) for k in range(4)]
    full = {n: jnp.concatenate([per_chip[k][i] for k in range(4)], axis=SHARD_AXIS[n]) for i, n in enumerate(BIG)}

    small_shapes = [weights[n].shape for n in SMALL_SHARDED]
    small_packed = _pack_rows([weights[n] for n in SMALL_SHARDED] + [c], 8, F32).reshape(-1, LANES)
    small_all = _allgather8(small_packed, "gather_small")
    small_chip = [_unpack(small_all[2 * k].reshape(-1), small_shapes) for k in range(4)]
    small_full = {n: jnp.concatenate([small_chip[k][i] for k in range(4)], axis=SHARD_AXIS[n])
                  for i, n in enumerate(SMALL_SHARDED)}
    n_small = sum(weights[n].size for n in SMALL_SHARDED)
    c_all = small_all.reshape(8, -1)[:, n_small:n_small + d]

    cols = ada_w.shape[2]
    ada_b_cols = lax.dynamic_slice_in_dim(ada_b, chip * cols, cols, axis=1)[:, None, :]
    c16 = jnp.pad(c_all, ((0, 8), (0, 0)))
    mod_part = _ada_forward(c16, ada_w, ada_b_cols, "ada_fwd")[:, :8, :]
    mod_all = _allgather8(mod_part.reshape(-1, LANES), "gather_mod").reshape(8, DEPTH, 8, cols)
    mod_mine = jnp.stack([lax.dynamic_index_in_dim(mod_all[2 * k], me, axis=1, keepdims=False) for k in range(4)], axis=1)
    mods = [jnp.pad(mod_mine[i].reshape(6, d), ((0, 2), (0, 0))) for i in range(DEPTH)]

    w = dict(
        fox_qkv=full["fox_w_in"][0][:, :n_qkv],
        fox_f=jnp.pad(full["fox_w_in"][0][:, n_qkv:], ((0, 0), (0, LANES - FOX_HEADS))),
        fox_o=full["fox_w_o"][0],
        fox_bf=jnp.pad(fox_b_f, ((0, 0), (0, LANES - FOX_HEADS))),
        head_ind=(jnp.arange(d)[:, None] // HEAD_DIM == jnp.arange(LANES)[None, :]).astype(BF16),
    )
    w.update(_ssm_weights(full["ssm_w_in"][0], small_full["ssm_conv_w"][0], small_full["ssm_conv_b"], ssm_dt_bias,
                          ssm_a_log, ssm_d, small_full["ssm_norm_w"], full["ssm_w_out"][0]))
    mixers = ((_fox_forward, _fox_backward), (_ssd_forward, _ssd_backward))

    saved = []
    xin = x0
    u = _modulate_in(x0, mods[0], "l0_mod_in")
    for i in range(DEPTH):
        tag = "l%d" % i
        y, sv = mixers[i % 2][0](u, w)
        r, x1, u2 = _res_ln_mod(xin, y, mods[i], ln_mix_g[i:i + 1], ln_mix_b[i:i + 1], tag + "_res_ln1")
        y2, hr, a = _mlp_forward(u2, full["mlp_w1"][i], full["mlp_w2"][i], tag)
        if i + 1 < DEPTH:
            r2, xin, u = _res_ln_next(x1, y2, mods[i], ln_mlp_g[i:i + 1], ln_mlp_b[i:i + 1], mods[i + 1],
                                      tag + "_res_ln2")
        else:
            r2 = _rowwise(lambda xv, yv, m: (ALPHA * xv + (1.0 + m[5:6]) * yv,), [x1, y2], [mods[i]], [(d, F32)], [],
                          name=tag + "_res2")[0]
        saved.append(dict(y=y, r=r, u2=u2, hr=hr, a=a, y2=y2, r2=r2, mix=sv))

    grads = {}
    dmod_parts = [dict() for _ in range(DEPTH)]
    ln_grads = {n: [None] * DEPTH for n in ("ln_mix_g", "ln_mix_b", "ln_mlp_g", "ln_mlp_b")}
    g_w1, g_w2 = [None] * DEPTH, [None] * DEPTH
    du = dres0 = None
    for i in reversed(range(DEPTH)):
        tag = "l%d" % i
        sv = saved[i]
        if i + 1 == DEPTH:
            dy2, dres, sq, dg2, db2, dgm = _loss_ln2_bwd(sv["r2"], sv["y2"], target, mods[i], ln_mlp_g[i:i + 1],
                                                         ln_mlp_b[i:i + 1], "loss_ln2_bwd")
            loss = lax.psum(0.5 * jnp.sum(sq) / d, ("x", "y", "c"))
        else:
            dy2, dres, dsca, dsha, dg2, db2, dgm = _mod_in_ln2_bwd(du, dres0, sv["r2"], sv["y2"], mods[i],
                                                                   ln_mlp_g[i:i + 1], ln_mlp_b[i:i + 1], mods[i + 1],
                                                                   tag + "_ln2_bwd")
            dmod_parts[i + 1].update(sc_a=dsca, sh_a=dsha)
        du2, g_w1[i], g_w2[i] = _mlp_backward(dy2, sv["u2"], sv["hr"], sv["a"], full["mlp_w1"][i], full["mlp_w2"][i], tag)
        dy, dres0, dscm, dshm, dg1, db1, dga = _ln1_bwd(du2, dres, sv["r"], sv["y"], mods[i], ln_mix_g[i:i + 1],
                                                        ln_mix_b[i:i + 1], tag + "_ln1_bwd")
        du, mg = mixers[i % 2][1](dy, sv["mix"], w)
        grads.update(mg)
        dmod_parts[i].update(g_a=dga, sh_m=dshm, sc_m=dscm, g_m=dgm)
        ln_grads["ln_mix_g"][i], ln_grads["ln_mix_b"][i] = dg1, db1
        ln_grads["ln_mlp_g"][i], ln_grads["ln_mlp_b"][i] = dg2, db2
    dx, dsca, dsha = _mod_in_bwd(du, dres0, x0, mods[0], "l0_mod_in_bwd")
    dmod_parts[0].update(sc_a=dsca, sh_a=dsha)
    dmods = [jnp.concatenate([p["sh_a"], p["sc_a"], p["g_a"], p["sh_m"], p["sc_m"], p["g_m"]], axis=1)
             for p in dmod_parts]
    grad_x = dx[None]
    grads["mlp_w1"] = jnp.stack(g_w1)
    grads["mlp_w2"] = jnp.stack(g_w2)
    for n in ("fox_w_in", "fox_w_o", "ssm_w_in", "ssm_w_out", "ssm_conv_w"):
        grads[n] = grads[n][None]

    small_names = ("ln_mix_g", "ln_mix_b", "ln_mlp_g", "ln_mlp_b", "fox_b_f", "ssm_dt_bias", "ssm_a_log", "ssm_d")
    small_parts = list(dmods)
    for n in small_names[:4]:
        small_parts.append(jnp.concatenate(ln_grads[n], axis=0))
    for n in small_names[4:]:
        small_parts.append(jnp.pad(grads[n], ((0, 0), (0, LANES - grads[n].shape[1]))))
    small_vec = _pack_rows(small_parts, 1, F32).reshape(-1, LANES)
    small_vec = jnp.pad(small_vec, ((0, -small_vec.shape[0] % 8), (0, 0)))
    small_g_all = _allgather8(small_vec, "gather_small_grads")
    small_sum = _sum_slots(small_g_all, "sum_small_grads").reshape(-1)
    dmod_sum = small_sum[:DEPTH * 6 * d].reshape(DEPTH, 6 * d)
    off = DEPTH * 6 * d
    final = {"ada_b": dmod_sum}
    for n in small_names[:4]:
        final[n] = small_sum[off:off + DEPTH * d].reshape(DEPTH, d)
        off += DEPTH * d
    for n in small_names[4:]:
        width = weights[n].shape[1]
        final[n] = small_sum[off:off + width].reshape(1, width)
        off += LANES

    dmod_all = small_g_all.reshape(8, -1)[:, :DEPTH * 6 * d].reshape(8, DEPTH, 6 * d)
    dmod_cols = lax.dynamic_slice_in_dim(dmod_all, chip * cols, cols, axis=2).transpose(1, 0, 2)
    final["ada_w"] = _ada_backward(c_all.T, dmod_cols, "ada_bwd")

    sharded = BIG + SMALL_SHARDED
    shard_shapes = [weights[n].shape for n in sharded]
    per_target = []
    for k in range(4):
        parts = [_chip_slice(grads[n], SHARD_AXIS[n], k, weights[n].shape[SHARD_AXIS[n]]) for n in sharded]
        per_target.append(_pack2d(parts, 128, F32))
    g_all = jnp.stack(per_target)
    rows = g_all.shape[1]
    g4 = g_all.reshape(4, 2, rows // 2, PACK_COLS)
    recv = _pair_exchange(g4, "rs_pair_exchange")
    part, part16 = _pair_add(g4, recv, jnp.reshape(mc, (1,)).astype(jnp.int32), "rs_pair_add")
    slots = _chip_exchange(part16, "rs_chip_exchange")
    half = _chip_sum(part, slots, jnp.reshape(chip, (1,)).astype(jnp.int32), "rs_chip_sum")
    other = _pair_share(half, "rs_pair_share")
    both = jnp.concatenate([jnp.where(mc == 0, half, other), jnp.where(mc == 0, other, half)], axis=0)
    for n, g_shard in zip(sharded, _unpack2d(both, shard_shapes)):
        final[n] = g_shard

    outs_g, outs_d, outs_m, outs_v = [], [], [], []
    for n in names:
        wv = weights[n]
        two_d = (-1, wv.shape[-1])
        delta, mn, vn = _adamw(wv.reshape(two_d), final[n].reshape(two_d), m_in[n].reshape(two_d),
                               v_in[n].reshape(two_d), "adamw_" + n)
        outs_g.append(final[n].reshape(wv.shape))
        outs_d.append(delta.reshape(wv.shape))
        outs_m.append(mn.reshape(wv.shape))
        outs_v.append(vn.reshape(wv.shape))
    return (loss, grad_x, *outs_g, *outs_d, *outs_m, *outs_v)
```

```python
import functools

import jax
import jax.numpy as jnp
from jax import lax
from jax.experimental import pallas as pl
from jax.experimental.pallas import tpu as pltpu

F32, BF16 = jnp.float32, jnp.bfloat16
MESH = pl.DeviceIdType.MESH
HBM_SPEC = pl.BlockSpec(memory_space=pltpu.HBM)

VMEM_LIMIT_BYTES = 52 * 2**20
LANES = 128

FOX_HEADS, HEAD_DIM = 16, 64
SSM_HEADS, SSM_GROUPS, SSM_STATE, SSM_CHUNK, SSM_CONV = 32, 8, 128, 128, 4
SSM_GROUP_WIDTH = 256
LN_EPS, RMS_EPS = 1e-5, 1e-5
DEPTH = 2
ALPHA = (2.0 * DEPTH) ** 0.25
ADAM_LR, ADAM_B1, ADAM_B2, ADAM_EPS, ADAM_WD, ADAM_STEP = 0.001, 0.9, 0.999, 1e-08, 0.01, 10

ATT_TILE = 512
ROW_TILE = 256
SCAN_TILE = 512
SSD_GROUPS_PER_STEP = 4
MM_TM, MM_TN, MM_TK = 1024, 1024, 1024

NT_DIMS = (((1,), (1,)), ((), ()))
TN_DIMS = (((0,), (0,)), ((), ()))
NN_DIMS = (((1,), (0,)), ((), ()))


def _pcall(body, *, name, out_shape, grid=(), in_specs=None, out_specs=None, scratch=(), sem=None, prefetch=0):
    params = dict(vmem_limit_bytes=VMEM_LIMIT_BYTES)
    if sem is not None:
        params["dimension_semantics"] = sem
    if prefetch:
        grid_spec = pltpu.PrefetchScalarGridSpec(num_scalar_prefetch=prefetch, grid=grid, in_specs=in_specs,
                                                 out_specs=out_specs, scratch_shapes=scratch)
        return pl.pallas_call(body, out_shape=out_shape, grid_spec=grid_spec, name=name,
                              compiler_params=pltpu.CompilerParams(**params))
    kwargs = {}
    if in_specs is not None:
        kwargs["in_specs"] = in_specs
    if out_specs is not None:
        kwargs["out_specs"] = out_specs
    return pl.pallas_call(body, out_shape=out_shape, grid=grid, scratch_shapes=scratch, name=name,
                          compiler_params=pltpu.CompilerParams(**params), **kwargs)


def _sds(shape, dtype):
    return jax.ShapeDtypeStruct(tuple(shape), dtype)


def _dot(a, b, dims=NN_DIMS):
    return lax.dot_general(a, b, dims, preferred_element_type=F32)


def _sigmoid(x):
    return 1.0 / (1.0 + jnp.exp(-x))


def _silu(x):
    return x * _sigmoid(x)


def _dsilu(x):
    s = _sigmoid(x)
    return s * (1.0 + x * (1.0 - s))


def _dot_split(x, m16, dims=NN_DIMS, passes=3):
    hi = x.astype(BF16)
    r1 = x - hi.astype(F32)
    mid = r1.astype(BF16)
    out = _dot(hi, m16, dims) + _dot(mid, m16, dims)
    if passes == 3:
        lo = (r1 - mid.astype(F32)).astype(BF16)
        out = out + _dot(lo, m16, dims)
    return out


def _mm(a, b, dims, outs, *, name, tm=MM_TM, tn=MM_TN, tk=MM_TK, epi=None, extra=()):
    if dims == "nn":
        (m, k), (k2, n) = a.shape, b.shape
    elif dims == "nt":
        (m, k), (n, k2) = a.shape, b.shape
    else:
        (k, m), (k2, n) = a.shape, b.shape
    assert k == k2, (a.shape, b.shape, dims)
    tm, tn, tk = min(tm, m), min(tn, n), min(tk, k)
    assert m % tm == 0 and n % tn == 0 and k % tk == 0, (m, n, k, tm, tn, tk)
    nk = k // tk
    dn = {"nn": NN_DIMS, "nt": NT_DIMS, "tn": TN_DIMS}[dims]
    n_extra, n_out = len(extra), len(outs)
    if epi is None:
        epi = lambda acc: (acc,) * n_out

    def body(a_ref, b_ref, *rest):
        extra_refs, out_refs, acc_ref = rest[:n_extra], rest[n_extra:n_extra + n_out], rest[-1]
        kk = pl.program_id(2)

        @pl.when(kk == 0)
        def _():
            acc_ref[...] = jnp.zeros_like(acc_ref)

        acc_ref[...] += _dot(a_ref[...].astype(BF16), b_ref[...].astype(BF16), dn)

        @pl.when(kk == nk - 1)
        def _():
            res = epi(acc_ref[...], *[e[...] for e in extra_refs])
            for o, r in zip(out_refs, res):
                o[...] = r.astype(o.dtype)

    if dims == "tn":
        a_spec = pl.BlockSpec((tk, tm), lambda i, j, kk: (kk, i))
    else:
        a_spec = pl.BlockSpec((tm, tk), lambda i, j, kk: (i, kk))
    if dims == "nt":
        b_spec = pl.BlockSpec((tn, tk), lambda i, j, kk: (j, kk))
    else:
        b_spec = pl.BlockSpec((tk, tn), lambda i, j, kk: (kk, j))
    o_spec = pl.BlockSpec((tm, tn), lambda i, j, kk: (i, j))
    res = _pcall(body, name=name, grid=(m // tm, n // tn, nk),
                 in_specs=[a_spec, b_spec] + [o_spec] * n_extra,
                 out_specs=[o_spec] * n_out,
                 out_shape=[_sds((m, n), d) for d in outs],
                 scratch=[pltpu.VMEM((tm, tn), F32)],
                 sem=("parallel", "parallel", "arbitrary"))(a, b, *extra)
    return res[0] if n_out == 1 else res


def _k_blocks(a, tk=None):
    tk = tk or MM_TK
    return [(a, kb) for kb in range(a.shape[1] // tk)]


def _mm_nt_blocks(a_blocks, b, start, *, name, tm=MM_TM, tk=None):
    tk = tk or MM_TK
    tm = min(tm, a_blocks[0][0].shape[0])
    m, n, p_n = a_blocks[0][0].shape[0], b.shape[0], len(a_blocks)
    assert b.shape[1] == p_n * tk and m % tm == 0

    def body(*refs):
        a_refs = refs[:p_n]
        b_ref, s_ref, o_ref, acc_ref = refs[p_n:]
        kk = pl.program_id(1)

        @pl.when(kk == 0)
        def _():
            acc_ref[...] = s_ref[...]
        for p in range(p_n):
            @pl.when(kk == p)
            def _(p=p):
                acc_ref[...] += _dot(a_refs[p][...].astype(BF16), b_ref[...].astype(BF16), NT_DIMS)

        @pl.when(kk == p_n - 1)
        def _():
            o_ref[...] = acc_ref[...]

    in_specs = [pl.BlockSpec((tm, tk), functools.partial(lambda kb, i, kk: (i, kb), kb)) for _, kb in a_blocks]
    in_specs += [pl.BlockSpec((n, tk), lambda i, kk: (0, kk)), pl.BlockSpec((tm, n), lambda i, kk: (i, 0))]
    return _pcall(body, name=name, grid=(m // tm, p_n), in_specs=in_specs,
                  out_specs=pl.BlockSpec((tm, n), lambda i, kk: (i, 0)), out_shape=_sds((m, n), F32),
                  scratch=[pltpu.VMEM((tm, n), F32)], sem=("parallel", "arbitrary"))(*[a for a, _ in a_blocks], b, start)


def _rowwise(fn, rows, consts, row_outs, acc_outs, *, name, tm=ROW_TILE):
    s = rows[0].shape[0]
    tm = min(tm, s)
    assert s % tm == 0
    n_in, n_o = len(rows) + len(consts), len(row_outs)

    def body(*refs):
        ins, outs = refs[:n_in], refs[n_in:]
        res = fn(*[r[...] for r in ins])
        if not isinstance(res, (tuple, list)):
            res = (res,)
        for o, val in zip(outs[:n_o], res[:n_o]):
            o[...] = val.astype(o.dtype)
        if acc_outs:
            @pl.when(pl.program_id(0) == 0)
            def _():
                for o in outs[n_o:]:
                    o[...] = jnp.zeros_like(o)
            for o, val in zip(outs[n_o:], res[n_o:]):
                o[...] += val

    in_specs = [pl.BlockSpec((tm, r.shape[1]), lambda i: (i, 0)) for r in rows]
    in_specs += [pl.BlockSpec(c.shape, functools.partial(lambda nd, i: (0,) * nd, c.ndim)) for c in consts]
    out_specs = [pl.BlockSpec((tm, c), lambda i: (i, 0)) for c, _ in row_outs]
    out_specs += [pl.BlockSpec(tuple(sh), lambda i: (0, 0)) for sh in acc_outs]
    out_shape = [_sds((s, c), d) for c, d in row_outs] + [_sds(sh, F32) for sh in acc_outs]
    res = _pcall(body, name=name, grid=(s // tm,), in_specs=in_specs, out_specs=out_specs,
                 out_shape=out_shape, sem=("arbitrary",))(*rows, *consts)
    return res


def _colsum(x):
    return jnp.sum(x, axis=0, keepdims=True)


def _ln_stats(r):
    mu = jnp.mean(r, axis=-1, keepdims=True)
    xc = r - mu
    var = jnp.mean(xc * xc, axis=-1, keepdims=True)
    rstd = lax.rsqrt(var + LN_EPS)
    return xc * rstd, rstd


def _ln_bwd(dy, xhat, rstd, gamma):
    dyg = dy * gamma
    m1 = jnp.mean(dyg, axis=-1, keepdims=True)
    m2 = jnp.mean(dyg * xhat, axis=-1, keepdims=True)
    return rstd * (dyg - m1 - xhat * m2)


def _modulate_in(x, mod, name):
    def fn(xv, m):
        return (xv * (1.0 + m[1:2]) + m[0:1],)
    return _rowwise(fn, [x], [mod], [(x.shape[1], BF16)], [], name=name)[0]


def _res_ln_mod(x, y, mod, g, b, name):
    d = x.shape[1]

    def fn(xv, yv, m, gv, bv):
        r = ALPHA * xv + (1.0 + m[2:3]) * yv
        xhat, _ = _ln_stats(r)
        x1 = xhat * gv + bv
        u2 = x1 * (1.0 + m[4:5]) + m[3:4]
        return r, x1, u2
    return _rowwise(fn, [x, y], [mod, g, b], [(d, F32), (d, F32), (d, BF16)], [], name=name)


def _res_ln_next(x, y, mod, g, b, mod_next, name):
    d = x.shape[1]

    def fn(xv, yv, m, gv, bv, mn):
        r = ALPHA * xv + (1.0 + m[5:6]) * yv
        xhat, _ = _ln_stats(r)
        out = xhat * gv + bv
        return r, out, out * (1.0 + mn[1:2]) + mn[0:1]
    return _rowwise(fn, [x, y], [mod, g, b, mod_next], [(d, F32), (d, F32), (d, BF16)], [], name=name)


def _loss_ln2_bwd(r2, y2, target, mod, g, b, name):
    d = r2.shape[1]

    def fn(rv, yv, tv, m, gv, bv):
        xhat, rstd = _ln_stats(rv)
        e = xhat * gv + bv - tv
        dxv = e * (1.0 / d)
        dr = _ln_bwd(dxv, xhat, rstd, gv)
        return (dr * (1.0 + m[5:6]), ALPHA * dr,
                _colsum(e * e), _colsum(dxv * xhat), _colsum(dxv), _colsum(dr * yv))
    return _rowwise(fn, [r2, y2, target], [mod, g, b], [(d, BF16), (d, F32)], [(1, d)] * 4, name=name)


def _mod_in_ln2_bwd(du, dres, r2, y2, mod, g, b, mod_next, name):
    d = du.shape[1]

    def fn(duv, drv, rv, yv, m, gv, bv, mn):
        xhat, rstd = _ln_stats(rv)
        xout = xhat * gv + bv
        dxv = duv * (1.0 + mn[1:2]) + drv
        dr = _ln_bwd(dxv, xhat, rstd, gv)
        return (dr * (1.0 + m[5:6]), ALPHA * dr,
                _colsum(duv * xout), _colsum(duv), _colsum(dxv * xhat), _colsum(dxv), _colsum(dr * yv))
    return _rowwise(fn, [du, dres, r2, y2], [mod, g, b, mod_next], [(d, BF16), (d, F32)], [(1, d)] * 5, name=name)


def _ln1_bwd(du2, dres, r, y, mod, g, b, name):
    d = du2.shape[1]

    def fn(duv, drv, rv, yv, m, gv, bv):
        xhat, rstd = _ln_stats(rv)
        x1 = xhat * gv + bv
        dx1 = duv * (1.0 + m[4:5]) + drv
        dr = _ln_bwd(dx1, xhat, rstd, gv)
        return (dr * (1.0 + m[2:3]), ALPHA * dr,
                _colsum(duv * x1), _colsum(duv), _colsum(dx1 * xhat), _colsum(dx1), _colsum(dr * yv))
    return _rowwise(fn, [du2, dres, r, y], [mod, g, b], [(d, BF16), (d, F32)], [(1, d)] * 5, name=name)


def _mod_in_bwd(du, dres, x, mod, name):
    d = du.shape[1]

    def fn(duv, drv, xv, m):
        return duv * (1.0 + m[1:2]) + drv, _colsum(duv * xv), _colsum(duv)
    return _rowwise(fn, [du, dres, x], [mod], [(d, F32)], [(1, d)] * 2, name=name)


def _fox_gate(fraw, b_pad, name):
    s = fraw.shape[0]
    tb = min(SCAN_TILE, s)

    def body(f_ref, b_ref, cum_ref, carry):
        @pl.when(pl.program_id(0) == 0)
        def _():
            carry[...] = jnp.zeros_like(carry)
        z = f_ref[...] + b_ref[...]
        lf = jnp.minimum(z, 0.0) - jnp.log(1.0 + jnp.exp(-jnp.abs(z)))
        lane = lax.broadcasted_iota(jnp.int32, (tb, LANES), 1)
        row = lax.broadcasted_iota(jnp.int32, (tb, LANES), 0)
        c = jnp.where(lane < FOX_HEADS, lf, 0.0)
        sh = 1
        while sh < tb:
            c = c + jnp.where(row >= sh, pltpu.roll(c, sh, 0), 0.0)
            sh *= 2
        c = c + carry[0:1, :]
        cum_ref[...] = c
        carry[0:1, :] = c[tb - 1:tb, :]

    return _pcall(body, name=name, grid=(s // tb,),
                  in_specs=[pl.BlockSpec((tb, LANES), lambda i: (i, 0)), pl.BlockSpec((1, LANES), lambda i: (0, 0))],
                  out_specs=pl.BlockSpec((tb, LANES), lambda i: (i, 0)),
                  out_shape=_sds((s, LANES), F32), scratch=[pltpu.VMEM((8, LANES), F32)],
                  sem=("arbitrary",))(fraw, b_pad)


def _fox_gate_bwd(drow, dcol, fraw, b_pad, name):
    s = fraw.shape[0]
    tb = min(SCAN_TILE, s)
    n = s // tb

    def body(dr_ref, dc_ref, f_ref, b_ref, df_ref, db_ref, carry):
        @pl.when(pl.program_id(0) == 0)
        def _():
            carry[...] = jnp.zeros_like(carry)
            db_ref[...] = jnp.zeros_like(db_ref)
        row = lax.broadcasted_iota(jnp.int32, (tb, LANES), 0)
        c = dr_ref[...] + dc_ref[...]
        sh = 1
        while sh < tb:
            c = c + jnp.where(row + sh < tb, pltpu.roll(c, tb - sh, 0), 0.0)
            sh *= 2
        c = c + carry[0:1, :]
        carry[0:1, :] = c[0:1, :]
        z = f_ref[...] + b_ref[...]
        df = c * (1.0 / (1.0 + jnp.exp(z)))
        df_ref[...] = df.astype(df_ref.dtype)
        db_ref[...] += _colsum(df)

    rev = lambda i: (n - 1 - i, 0)
    return _pcall(body, name=name, grid=(n,),
                  in_specs=[pl.BlockSpec((tb, LANES), rev)] * 3 + [pl.BlockSpec((1, LANES), lambda i: (0, 0))],
                  out_specs=[pl.BlockSpec((tb, LANES), rev), pl.BlockSpec((1, LANES), lambda i: (0, 0))],
                  out_shape=[_sds((s, LANES), BF16), _sds((1, LANES), F32)],
                  scratch=[pltpu.VMEM((8, LANES), F32)], sem=("arbitrary",))(drow, dcol, fraw, b_pad)


def _head_pair_masks(t):
    lane = lax.broadcasted_iota(jnp.int32, (t, LANES), 1)
    return lane < HEAD_DIM


def _lane_blocks(x):
    return [x[:, c * LANES:(c + 1) * LANES] for c in range(x.shape[1] // LANES)]


def _sum_list(xs):
    acc = xs[0]
    for x in xs[1:]:
        acc = acc + x
    return acc


def _causal(t, transposed=False):
    ri = lax.broadcasted_iota(jnp.int32, (t, t), 0)
    ci = lax.broadcasted_iota(jnp.int32, (t, t), 1)
    return ci >= ri if transposed else ri >= ci


def _flash_fwd(qkv, ck_rows, kb_start, name):
    s = qkv.shape[0]
    t = min(ATT_TILE, s)
    nq = s // t
    scale = HEAD_DIM ** -0.5
    hp_blocks = FOX_HEADS // 2

    def body(ks_ref, q_ref, k_ref, v_ref, ck_ref, o_ref, lse_ref, acc_ref, m_ref, l_ref):
        hp, qb = pl.program_id(0), pl.program_id(1)
        q2 = q_ref[...] * scale
        first = _head_pair_masks(t)
        zero = jnp.zeros_like(q2)
        qs = (jnp.where(first, q2, zero), jnp.where(first, zero, q2))
        m_ref[...] = jnp.full_like(m_ref, -jnp.inf)
        l_ref[...] = jnp.zeros_like(l_ref)
        acc_ref[...] = jnp.zeros_like(acc_ref)

        def tile(kb, diagonal):
            off = pl.multiple_of(kb * t, t)
            k2 = k_ref[pl.ds(off, t), :]
            v2 = v_ref[pl.ds(off, t), :]
            ck = ck_ref[kb]
            pvs, als = [], []
            for j in range(2):
                sc = _dot(qs[j], k2, NT_DIMS) - ck[j:j + 1, :]
                if diagonal:
                    sc = jnp.where(_causal(t), sc, -jnp.inf)
                blocks = _lane_blocks(sc)
                mx = blocks[0]
                for b in blocks[1:]:
                    mx = jnp.maximum(mx, b)
                m_old = m_ref[j]
                m_new = jnp.maximum(m_old, jnp.max(mx, axis=1, keepdims=True))
                ps = [jnp.exp(b - m_new) for b in blocks]
                a = jnp.exp(m_old - m_new)
                l_ref[j] = a * l_ref[j] + _sum_list(ps)
                m_ref[j] = m_new
                pvs.append(_dot(jnp.concatenate(ps, axis=1).astype(BF16), v2))
                als.append(a)
            acc_ref[...] = jnp.where(first, als[0], als[1]) * acc_ref[...] + jnp.where(first, pvs[0], pvs[1])

        def step(kb, carry):
            tile(kb, False)
            return carry

        lax.fori_loop(ks_ref[hp, qb], qb, step, 0)
        tile(qb, True)
        l0 = jnp.sum(l_ref[0], axis=1, keepdims=True)
        l1 = jnp.sum(l_ref[1], axis=1, keepdims=True)
        o_ref[...] = acc_ref[...] / jnp.where(first, l0, l1)
        lse_ref[:, 0:1] = m_ref[0][:, 0:1] + jnp.log(l0)
        lse_ref[:, 1:2] = m_ref[1][:, 0:1] + jnp.log(l1)

    return _pcall(
        body, name=name, grid=(hp_blocks, nq), prefetch=1,
        in_specs=[pl.BlockSpec((t, LANES), lambda h, i, ks: (i, h)),
                  pl.BlockSpec((s, LANES), lambda h, i, ks: (0, hp_blocks + h)),
                  pl.BlockSpec((s, LANES), lambda h, i, ks: (0, 2 * hp_blocks + h)),
                  pl.BlockSpec((None, nq, 2, t), lambda h, i, ks: (h, 0, 0, 0))],
        out_specs=[pl.BlockSpec((t, LANES), lambda h, i, ks: (i, h)),
                   pl.BlockSpec((None, t, 2), lambda h, i, ks: (h, i, 0))],
        out_shape=[_sds((s, hp_blocks * LANES), F32), _sds((hp_blocks, s, 2), F32)],
        scratch=[pltpu.VMEM((t, LANES), F32), pltpu.VMEM((2, t, LANES), F32), pltpu.VMEM((2, t, LANES), F32)],
        sem=("parallel", "arbitrary"))(kb_start, qkv, qkv, qkv, ck_rows)


def _flash_dq(qkv, do16, ck_rows, lse_c, dl_c, kb_start, name):
    s = qkv.shape[0]
    t = min(ATT_TILE, s)
    nq = s // t
    scale = HEAD_DIM ** -0.5
    hp_blocks = FOX_HEADS // 2

    def body(ks_ref, q_ref, do_ref, k_ref, v_ref, ck_ref, lse_ref, dl_ref, dq_ref, drow_ref, acc_ref, row_acc):
        hp, qb = pl.program_id(0), pl.program_id(1)
        q2, do2 = q_ref[...] * scale, do_ref[...]
        first = _head_pair_masks(t)
        zero = jnp.zeros_like(q2)
        qs = (jnp.where(first, q2, zero), jnp.where(first, zero, q2))
        dos = (jnp.where(first, do2, zero), jnp.where(first, zero, do2))
        lse, dl = lse_ref[...], dl_ref[...]
        lse_b = [jnp.broadcast_to(lse[:, j:j + 1], (t, LANES)) for j in range(2)]
        dl_b = [jnp.broadcast_to(dl[:, j:j + 1], (t, LANES)) for j in range(2)]
        acc_ref[...] = jnp.zeros_like(acc_ref)
        row_acc[...] = jnp.zeros_like(row_acc)

        def tile(kb, diagonal):
            off = pl.multiple_of(kb * t, t)
            k2 = k_ref[pl.ds(off, t), :]
            v2 = v_ref[pl.ds(off, t), :]
            ck = ck_ref[kb]
            dqs = []
            for j in range(2):
                sc = _dot(qs[j], k2, NT_DIMS) - ck[j:j + 1, :]
                if diagonal:
                    sc = jnp.where(_causal(t), sc, -jnp.inf)
                dp = _dot(dos[j], v2, NT_DIMS)
                dsb = [jnp.exp(x - lse_b[j]) * (d - dl_b[j]) for x, d in zip(_lane_blocks(sc), _lane_blocks(dp))]
                row_acc[j] += _sum_list(dsb)
                dqs.append(_dot(jnp.concatenate(dsb, axis=1).astype(BF16), k2))
            acc_ref[...] += jnp.where(first, dqs[0], dqs[1])

        def step(kb, carry):
            tile(kb, False)
            return carry

        lax.fori_loop(ks_ref[hp, qb], qb, step, 0)
        tile(qb, True)
        dq_ref[...] = (acc_ref[...] * scale).astype(dq_ref.dtype)
        drow_ref[:, 0:1] = jnp.sum(row_acc[0], axis=1, keepdims=True)
        drow_ref[:, 1:2] = jnp.sum(row_acc[1], axis=1, keepdims=True)

    return _pcall(
        body, name=name, grid=(hp_blocks, nq), prefetch=1,
        in_specs=[pl.BlockSpec((t, LANES), lambda h, i, ks: (i, h)),
                  pl.BlockSpec((t, LANES), lambda h, i, ks: (i, h)),
                  pl.BlockSpec((s, LANES), lambda h, i, ks: (0, hp_blocks + h)),
                  pl.BlockSpec((s, LANES), lambda h, i, ks: (0, 2 * hp_blocks + h)),
                  pl.BlockSpec((None, nq, 2, t), lambda h, i, ks: (h, 0, 0, 0)),
                  pl.BlockSpec((None, t, 2), lambda h, i, ks: (h, i, 0)),
                  pl.BlockSpec((None, t, 2), lambda h, i, ks: (h, i, 0))],
        out_specs=[pl.BlockSpec((t, LANES), lambda h, i, ks: (i, h)),
                   pl.BlockSpec((None, t, 2), lambda h, i, ks: (h, i, 0))],
        out_shape=[_sds((s, hp_blocks * LANES), BF16), _sds((hp_blocks, s, 2), F32)],
        scratch=[pltpu.VMEM((t, LANES), F32), pltpu.VMEM((2, t, LANES), F32)],
        sem=("parallel", "arbitrary"))(kb_start, qkv, do16, qkv, qkv, ck_rows, lse_c, dl_c)


def _flash_dkv(qkv, do16, cum, lse_rows, dl_rows, qb_end, name):
    s = qkv.shape[0]
    t = min(ATT_TILE, s)
    nq = s // t
    scale = HEAD_DIM ** -0.5
    hp_blocks = FOX_HEADS // 2

    def body(qe_ref, k_ref, v_ref, cum_ref, q_ref, do_ref, lse_ref, dl_ref, dk_ref, dv_ref, dck_ref,
             dk_acc, dv_acc, dck_acc):
        hp, kb = pl.program_id(0), pl.program_id(1)
        k2, v2 = k_ref[...] * scale, v_ref[...]
        first = _head_pair_masks(t)
        zero = jnp.zeros_like(k2)
        ks = (jnp.where(first, k2, zero), jnp.where(first, zero, k2))
        vs = (jnp.where(first, v2, zero), jnp.where(first, zero, v2))
        cumv = cum_ref[...]
        lane = lax.broadcasted_iota(jnp.int32, (t, LANES), 1)
        ck_b = [jnp.broadcast_to(jnp.sum(jnp.where(lane == 2 * hp + j, cumv, 0.0), axis=1, keepdims=True), (t, LANES))
                for j in range(2)]
        dk_acc[...] = jnp.zeros_like(dk_acc)
        dv_acc[...] = jnp.zeros_like(dv_acc)
        dck_acc[...] = jnp.zeros_like(dck_acc)

        def tile(qb, diagonal):
            off = pl.multiple_of(qb * t, t)
            q2 = q_ref[pl.ds(off, t), :]
            do2 = do_ref[pl.ds(off, t), :]
            lse, dl = lse_ref[qb], dl_ref[qb]
            dvs, dks = [], []
            for j in range(2):
                sc = _dot(ks[j], q2, NT_DIMS)
                if diagonal:
                    sc = jnp.where(_causal(t, transposed=True), sc, -jnp.inf)
                dp = _dot(vs[j], do2, NT_DIMS) - dl[j:j + 1, :]
                pb = [jnp.exp((x - ck_b[j]) - l) for x, l in zip(_lane_blocks(sc), _lane_blocks(lse[j:j + 1, :]))]
                dsb = [p * d for p, d in zip(pb, _lane_blocks(dp))]
                dck_acc[j] += _sum_list(dsb)
                dvs.append(_dot(jnp.concatenate(pb, axis=1).astype(BF16), do2))
                dks.append(_dot(jnp.concatenate(dsb, axis=1).astype(BF16), q2))
            dv_acc[...] += jnp.where(first, dvs[0], dvs[1])
            dk_acc[...] += jnp.where(first, dks[0], dks[1])

        def step(qb, carry):
            tile(qb, False)
            return carry

        tile(kb, True)
        lax.fori_loop(kb + 1, qe_ref[hp, kb] + 1, step, 0)
        dk_ref[...] = (dk_acc[...] * scale).astype(dk_ref.dtype)
        dv_ref[...] = dv_acc[...].astype(dv_ref.dtype)
        dck_ref[:, 0:1] = -jnp.sum(dck_acc[0], axis=1, keepdims=True)
        dck_ref[:, 1:2] = -jnp.sum(dck_acc[1], axis=1, keepdims=True)

    return _pcall(
        body, name=name, grid=(hp_blocks, nq), prefetch=1,
        in_specs=[pl.BlockSpec((t, LANES), lambda h, j, qe: (j, hp_blocks + h)),
                  pl.BlockSpec((t, LANES), lambda h, j, qe: (j, 2 * hp_blocks + h)),
                  pl.BlockSpec((t, LANES), lambda h, j, qe: (j, 0)),
                  pl.BlockSpec((s, LANES), lambda h, j, qe: (0, h)),
                  pl.BlockSpec((s, LANES), lambda h, j, qe: (0, h)),
                  pl.BlockSpec((None, nq, 2, t), lambda h, j, qe: (h, 0, 0, 0)),
                  pl.BlockSpec((None, nq, 2, t), lambda h, j, qe: (h, 0, 0, 0))],
        out_specs=[pl.BlockSpec((t, LANES), lambda h, j, qe: (j, h)),
                   pl.BlockSpec((t, LANES), lambda h, j, qe: (j, h)),
                   pl.BlockSpec((None, t, 2), lambda h, j, qe: (h, j, 0))],
        out_shape=[_sds((s, hp_blocks * LANES), BF16), _sds((s, hp_blocks * LANES), BF16),
                   _sds((hp_blocks, s, 2), F32)],
        scratch=[pltpu.VMEM((t, LANES), F32), pltpu.VMEM((t, LANES), F32), pltpu.VMEM((2, t, LANES), F32)],
        sem=("parallel", "arbitrary"))(qb_end, qkv, qkv, cum, qkv, do16, lse_rows, dl_rows)


SKIP_NATS = 110.0


def _qk_norms(qkv, ind16, name):
    d = FOX_HEADS * HEAD_DIM

    def fn(tile, ind):
        q = tile[:, :d].astype(F32)
        k = tile[:, d:2 * d].astype(F32)
        return _dot_split(q * q, ind), _dot_split(k * k, ind)
    return _rowwise(fn, [qkv], [ind16], [(LANES, F32), (LANES, F32)], [], name=name)


def _skip_bounds(qn, kn, cum, t):
    s = qn.shape[0]
    nq = s // t
    hp = FOX_HEADS // 2
    scale = HEAD_DIM ** -0.5
    qmax = jnp.sqrt(jnp.max(qn.reshape(nq, t, FOX_HEADS), axis=1))
    kmax = jnp.sqrt(jnp.max(kn, axis=0))
    bound = qmax * kmax[None, :] * (scale * 1.01) + 1e-3
    gap = cum[0::t][:, None, :] - cum[t - 1::t][None, :, :]
    idx = jnp.arange(nq, dtype=jnp.int32)
    needed = (gap + 2.0 * bound[:, None, :]) > -SKIP_NATS
    needed = needed.reshape(nq, nq, hp, 2).any(axis=-1) & (idx[None, :] <= idx[:, None])[:, :, None]
    first = jnp.min(jnp.where(needed, idx[None, :, None], nq), axis=1)
    first = jnp.minimum(first, idx[:, None])
    start = lax.cummin(first, axis=0, reverse=True)
    uses = start[:, None, :] <= idx[None, :, None]
    last = jnp.max(jnp.where(uses, idx[:, None, None], 0), axis=0)
    last = jnp.maximum(last, idx[:, None])
    return start.T.astype(jnp.int32), last.T.astype(jnp.int32)


def _head_rowsum(prod, ind16, name):
    def fn(a, b, ind):
        return (_dot_split(a * b, ind),)
    return _rowwise(fn, list(prod), [ind16], [(LANES, F32)], [], name=name)[0]


def _pairs_cols(x16):
    s = x16.shape[0]
    return x16.reshape(s, FOX_HEADS // 2, 2).transpose(1, 0, 2)


def _pairs_rows(x16, t):
    s = x16.shape[0]
    return x16.reshape(s // t, t, FOX_HEADS // 2, 2).transpose(2, 0, 3, 1)


def _fox_forward(u, w):
    s = u.shape[0]
    t = min(ATT_TILE, s)
    qkv = _mm(u, w["fox_qkv"], "nn", [BF16], name="fox_qkv")
    fraw = _mm(u, w["fox_f"], "nn", [F32], name="fox_fproj")
    cum = _fox_gate(fraw, w["fox_bf"], "fox_gate")
    ck_rows = _pairs_rows(cum[:, :FOX_HEADS], t)
    qn, kn = _qk_norms(qkv, w["head_ind"], "fox_qk_norms")
    kb_start, qb_end = _skip_bounds(qn[:, :FOX_HEADS], kn[:, :FOX_HEADS], cum[:, :FOX_HEADS], t)
    o, lse = _flash_fwd(qkv, ck_rows, kb_start, "fox_flash_fwd")
    y = _mm(o, w["fox_o"], "nn", [F32], name="fox_oproj")
    return y, dict(u=u, qkv=qkv, fraw=fraw, cum=cum, ck_rows=ck_rows, o=o, lse=lse, kb_start=kb_start,
                   qb_end=qb_end)


def _fox_backward(dy, sv, w):
    s = dy.shape[0]
    t = min(ATT_TILE, s)
    do32, do16 = _mm(dy, w["fox_o"], "nt", [F32, BF16], name="fox_do")
    g_wo = _mm(sv["o"], dy, "tn", [F32], name="fox_gwo")
    delta = _head_rowsum((do32, sv["o"]), w["head_ind"], "fox_delta")[:, :FOX_HEADS]
    lse16 = sv["lse"].transpose(1, 0, 2).reshape(s, FOX_HEADS)
    dq, drow = _flash_dq(sv["qkv"], do16, sv["ck_rows"], sv["lse"], _pairs_cols(delta), sv["kb_start"],
                         "fox_flash_dq")
    dk, dv, dck = _flash_dkv(sv["qkv"], do16, sv["cum"], _pairs_rows(lse16, t), _pairs_rows(delta, t),
                             sv["qb_end"], "fox_flash_dkv")
    pad = ((0, 0), (0, LANES - FOX_HEADS))
    dcol = jnp.pad(dck.transpose(1, 0, 2).reshape(s, FOX_HEADS), pad)
    drow = jnp.pad(drow.transpose(1, 0, 2).reshape(s, FOX_HEADS), pad)
    df, db_f = _fox_gate_bwd(drow, dcol, sv["fraw"], w["fox_bf"], "fox_gate_bwd")
    du = _mm(df, w["fox_f"], "nt", [F32], name="fox_du_f")
    du = _mm_nt_blocks(_k_blocks(dq) + _k_blocks(dk) + _k_blocks(dv), w["fox_qkv"], du, name="fox_du")
    g_win = jnp.concatenate([_mm(sv["u"], dq, "tn", [F32], name="fox_gwin_q"),
                             _mm(sv["u"], dk, "tn", [F32], name="fox_gwin_k"),
                             _mm(sv["u"], dv, "tn", [F32], name="fox_gwin_v"),
                             _mm(sv["u"], df, "tn", [F32], name="fox_gwin_f")[:, :FOX_HEADS]], axis=1)
    return du, dict(fox_w_in=g_win, fox_w_o=g_wo, fox_b_f=db_f[:, :FOX_HEADS])


def _conv_fwd(xpre, w8, b, name):
    s, c = xpre.shape
    tm, tc = min(ROW_TILE, s), min(1024, c)
    hb = tm // 8

    def body(x_ref, h_ref, w_ref, b_ref, xc_ref, xa_ref):
        i = pl.program_id(1)
        x = x_ref[...]
        halo = jnp.where(i > 0, h_ref[...], 0.0)
        w = w_ref[...]
        row = lax.broadcasted_iota(jnp.int32, (tm, tc), 0)
        row8 = lax.broadcasted_iota(jnp.int32, (8, tc), 0)
        acc = x * w[3:4] + b_ref[...]
        x8 = x[0:8]
        acc8 = x8 * w[3:4] + b_ref[...]
        for j in range(1, SSM_CONV):
            acc = acc + w[3 - j:4 - j] * pltpu.roll(x, j, 0)
            acc8 = acc8 + w[3 - j:4 - j] * jnp.where(row8 < j, pltpu.roll(halo, j, 0), pltpu.roll(x8, j, 0))
        xc_ref[...] = acc
        xc_ref[0:8, :] = acc8
        xc = xc_ref[...]
        xa_ref[...] = _silu(xc)

    tile = pl.BlockSpec((tm, tc), lambda jc, i: (i, jc))
    return _pcall(body, name=name, grid=(c // tc, s // tm),
                  in_specs=[tile, pl.BlockSpec((8, tc), lambda jc, i: (jnp.maximum(i * hb - 1, 0), jc)),
                            pl.BlockSpec((8, tc), lambda jc, i: (0, jc)), pl.BlockSpec((1, tc), lambda jc, i: (0, jc))],
                  out_specs=[tile, tile], out_shape=[_sds((s, c), F32), _sds((s, c), F32)],
                  sem=("parallel", "arbitrary"))(xpre, xpre, w8, b)


def _conv_bwd(dxa, xc, xpre, w8, name):
    s, c = xpre.shape
    tm, tc = min(ROW_TILE, s), min(1024, c)
    hb = tm // 8
    n = s // tm

    def body(d_ref, xc_ref, x_ref, xh_ref, dn_ref, xcn_ref, w_ref, dx_ref, dw_ref, db_ref, scr):
        i = pl.program_id(1)

        @pl.when(i == 0)
        def _():
            dw_ref[...] = jnp.zeros_like(dw_ref)
            db_ref[...] = jnp.zeros_like(db_ref)
        w = w_ref[...]
        x = x_ref[...]
        g = d_ref[...] * _dsilu(xc_ref[...])
        gn = jnp.where(i < n - 1, dn_ref[...] * _dsilu(xcn_ref[...]), 0.0)
        halo = jnp.where(i > 0, xh_ref[...], 0.0)
        row = lax.broadcasted_iota(jnp.int32, (tm, tc), 0)
        row8 = lax.broadcasted_iota(jnp.int32, (8, tc), 0)
        db_ref[...] += _colsum(g)
        dw_ref[3:4, :] += _colsum(g * x)
        g8 = g[0:8]
        acc = g * w[3:4]
        corr = jnp.zeros((8, tc), F32)
        for j in range(1, SSM_CONV):
            xs = pltpu.roll(x, j, 0)
            dwj = _colsum(jnp.where(row >= j, g * xs, 0.0))
            dwj = dwj + _colsum(jnp.where(row8 < j, g8 * pltpu.roll(halo, j, 0), 0.0))
            dw_ref[3 - j:4 - j, :] += dwj
            gs = pltpu.roll(g, tm - j, 0)
            acc = acc + w[3 - j:4 - j] * jnp.where(row < tm - j, gs, 0.0)
            corr = corr + w[3 - j:4 - j] * jnp.where(row8 >= 8 - j, pltpu.roll(gn, 8 - j, 0), 0.0)
        scr[...] = acc
        scr[tm - 8:tm, :] += corr
        dx_ref[...] = scr[...].astype(dx_ref.dtype)

    tile = pl.BlockSpec((tm, tc), lambda jc, i: (i, jc))
    prev8 = pl.BlockSpec((8, tc), lambda jc, i: (jnp.maximum(i * hb - 1, 0), jc))
    next8 = pl.BlockSpec((8, tc), lambda jc, i: (jnp.minimum((i + 1) * hb, n * hb - 1), jc))
    return _pcall(body, name=name, grid=(c // tc, n),
                  in_specs=[tile, tile, tile, prev8, next8, next8, pl.BlockSpec((8, tc), lambda jc, i: (0, jc))],
                  out_specs=[tile, pl.BlockSpec((8, tc), lambda jc, i: (0, jc)), pl.BlockSpec((1, tc), lambda jc, i: (0, jc))],
                  out_shape=[_sds((s, c), BF16), _sds((8, c), F32), _sds((1, c), F32)],
                  scratch=[pltpu.VMEM((tm, tc), F32)],
                  sem=("parallel", "arbitrary"))(dxa, xc, xpre, xpre, dxa, xc, w8)


def _ssd_pre(dtraw, dt_bias, a_log, cst, name):
    def fn(raw, bias, alog, expand):
        tm = raw.shape[0]
        z = raw + bias
        dt = jnp.maximum(z, 0.0) + jnp.log(1.0 + jnp.exp(-jnp.abs(z)))
        lane = lax.broadcasted_iota(jnp.int32, (tm, LANES), 1)
        pos = lax.broadcasted_iota(jnp.int32, (tm, LANES), 0) & (SSM_CHUNK - 1)
        dt = jnp.where(lane < SSM_HEADS, dt, 0.0)
        c = dt * (-jnp.exp(alog))
        sh = 1
        while sh < SSM_CHUNK:
            c = c + jnp.where(pos >= sh, pltpu.roll(c, sh, 0), 0.0)
            sh *= 2
        return dt, c, _dot_split(dt, expand), _dot_split(c, expand)
    wide = SSM_HEADS * HEAD_DIM
    return _rowwise(fn, [dtraw], [dt_bias, a_log, cst["expand"]],
                    [(LANES, F32), (LANES, F32), (wide, F32), (wide, F32)], [], name=name)


def _ssd_post(dacs, ddt, dtraw, dt, dt_bias, a_log, name):
    def fn(dacs_v, ddt_v, raw, dt_v, bias, alog):
        tm = raw.shape[0]
        pos = lax.broadcasted_iota(jnp.int32, (tm, LANES), 0) & (SSM_CHUNK - 1)
        a = -jnp.exp(alog)
        c = dacs_v
        sh = 1
        while sh < SSM_CHUNK:
            c = c + jnp.where(pos + sh < SSM_CHUNK, pltpu.roll(c, tm - sh, 0), 0.0)
            sh *= 2
        draw = (ddt_v + c * a) * _sigmoid(raw + bias)
        return draw, _colsum(draw), _colsum(c * dt_v * a)
    return _rowwise(fn, [dacs, ddt, dtraw, dt], [dt_bias, a_log], [(LANES, BF16)], [(1, LANES)] * 2, name=name)


def _heads_rows(x, s):
    return x[:, :SSM_HEADS].reshape(s // SSM_CHUNK, SSM_CHUNK, SSM_GROUPS, 4).transpose(2, 0, 3, 1)


def _ssd_constants():
    src = jnp.arange(LANES)[:, None]
    expand = (src == jnp.arange(SSM_HEADS * HEAD_DIM)[None, :] // HEAD_DIM).astype(BF16)
    seg = (jnp.arange(SSM_GROUP_WIDTH)[:, None] // HEAD_DIM == jnp.arange(LANES)[None, :]).astype(BF16)
    seg4 = (jnp.arange(4 * LANES)[:, None] // LANES == jnp.arange(LANES)[None, :]).astype(BF16)
    return dict(expand=expand, seg=seg, seg4=seg4)


def _ssm_weights(w_in, conv_w, conv_b, dt_bias, a_log, d_skip, norm_w, w_out):
    pad = ((0, 0), (0, LANES - SSM_HEADS))
    w_xbc = _group_cols(w_in[:, 2048:6144])
    w_dt = jnp.pad(w_in[:, 6144:], pad)
    return dict(
        ssm_z=w_in[:, :2048], ssm_xbc=w_xbc, ssm_dt=w_dt,
        ssm_zx=jnp.concatenate([w_in[:, :2048], w_xbc], axis=1),
        ssm_out=w_out,
        conv_w8=_group_cols(jnp.pad(conv_w, ((0, 8 - SSM_CONV), (0, 0)))),
        conv_b=_group_cols(conv_b), norm_w=norm_w,
        dt_bias=jnp.pad(dt_bias, pad), a_log=jnp.pad(a_log, pad),
        d_e=jnp.repeat(d_skip.reshape(SSM_GROUPS, 4), HEAD_DIM, axis=1)[:, None, :],
        ssd_cst=_ssd_constants())


def _ssd_setup(acs_e, acsr):
    l = SSM_CHUNK
    last = acsr[:, l - 1:l]
    lane1 = lax.broadcasted_iota(jnp.int32, (1, SSM_GROUP_WIDTH), 1)
    last_e = last[3:4, :]
    for r in (2, 1, 0):
        last_e = jnp.where(lane1 < HEAD_DIM * (r + 1), last[r:r + 1, :], last_e)
    return jnp.exp(acs_e), jnp.exp(last_e - acs_e), jnp.exp(last_e)


def _head_bcast(acs_e):
    lo = lax.broadcasted_iota(jnp.int32, (acs_e.shape[0], LANES), 1) < HEAD_DIM
    out = []
    for p in range(2):
        blk = acs_e[:, p * LANES:(p + 1) * LANES]
        rolled = pltpu.roll(blk, HEAD_DIM, 1)
        out += [jnp.where(lo, blk, rolled), jnp.where(lo, rolled, blk)]
    return out


def _group_cols(a):
    lead = a.shape[:-1]
    x = a[..., :2048].reshape(lead + (SSM_GROUPS, SSM_GROUP_WIDTH))
    b = a[..., 2048:3072].reshape(lead + (SSM_GROUPS, SSM_STATE))
    c = a[..., 3072:].reshape(lead + (SSM_GROUPS, SSM_STATE))
    return jnp.concatenate([x, b, c], axis=-1).reshape(lead + (4096,))


def _ungroup_cols(a):
    lead = a.shape[:-1]
    y = a.reshape(lead + (SSM_GROUPS, SSM_GROUP_WIDTH + 2 * SSM_STATE))
    return jnp.concatenate([y[..., :256].reshape(lead + (2048,)), y[..., 256:384].reshape(lead + (1024,)),
                            y[..., 384:].reshape(lead + (1024,))], axis=-1)


def _ssd_fwd2(xa, dte, acse, acsr, d_e, name):
    s = xa.shape[0]
    l, gw, ns = SSM_CHUNK, SSM_GROUP_WIDTH, SSM_STATE
    nc = s // l

    gb = gw + 2 * ns
    gp = SSD_GROUPS_PER_STEP

    def body(xa_ref, dt_ref, acs_ref, acsr_ref, d_ref, y_ref, hp_ref, h_sc):
        @pl.when(pl.program_id(1) == 0)
        def _():
            h_sc[...] = jnp.zeros_like(h_sc)
        lane = lax.broadcasted_iota(jnp.int32, (l, gw), 1)
        tril = _causal(l)
        for gi in range(gp):
            x = xa_ref[:, gi * gb:gi * gb + gw]
            bm = xa_ref[:, gi * gb + gw:gi * gb + gw + ns].astype(BF16)
            cm = xa_ref[:, gi * gb + gw + ns:(gi + 1) * gb].astype(BF16)
            acsr = acsr_ref[gi]
            dt_e, acs_e = dt_ref[:, gi * gw:(gi + 1) * gw], acs_ref[:, gi * gw:(gi + 1) * gw]
            acs_bc = _head_bcast(acs_e)
            e_e, dte_e, cd_e = _ssd_setup(acs_e, acsr)
            xdt = x * dt_e
            xdt16 = xdt.astype(BF16)
            cb = _dot(cm, bm, NT_DIMS)
            yd = jnp.zeros((l, gw), F32)
            for r in range(4):
                lm = jnp.exp(jnp.where(tril, acs_bc[r] - acsr[r:r + 1, :], -jnp.inf))
                yr = _dot((cb * lm).astype(BF16), xdt16)
                yd = jnp.where((lane >= HEAD_DIM * r) & (lane < HEAD_DIM * (r + 1)), yr, yd)
            hp = h_sc[gi]
            hp_ref[gi] = hp
            y_ref[:, gi * gw:(gi + 1) * gw] = yd + _dot(cm, hp.astype(BF16)) * e_e + x * d_ref[gi]
            h_sc[gi] = hp * cd_e + _dot(bm, (xdt * dte_e).astype(BF16), TN_DIMS)

    return _pcall(
        body, name=name, grid=(SSM_GROUPS // gp, nc),
        in_specs=[pl.BlockSpec((l, gp * gb), lambda g, c: (c, g)),
                  pl.BlockSpec((l, gp * gw), lambda g, c: (c, g)),
                  pl.BlockSpec((l, gp * gw), lambda g, c: (c, g)),
                  pl.BlockSpec((gp, None, 4, l), lambda g, c: (g, c, 0, 0)),
                  pl.BlockSpec((gp, 1, gw), lambda g, c: (g, 0, 0))],
        out_specs=[pl.BlockSpec((l, gp * gw), lambda g, c: (c, g)),
                   pl.BlockSpec((gp, None, ns, gw), lambda g, c: (g, c, 0, 0))],
        out_shape=[_sds((s, 2048), F32), _sds((SSM_GROUPS, nc, ns, gw), F32)],
        scratch=[pltpu.VMEM((gp, ns, gw), F32)],
        sem=("parallel", "arbitrary"))(xa, dte, acse, acsr, d_e)


def _ssd_bwd2(dy, xa, dte, acse, acsr, d_e, hprev, cst, name):
    s = xa.shape[0]
    l, gw, ns = SSM_CHUNK, SSM_GROUP_WIDTH, SSM_STATE
    nc = s // l

    gb = gw + 2 * ns
    gp = SSD_GROUPS_PER_STEP

    def body(dy_ref, xa_ref, dt_ref, acs_ref, acsr_ref, d_ref, hp_ref,
             seg_ref, seg4_ref, dxa_ref, dacs_ref, ddt_ref, dd_ref, dh_sc):
        @pl.when(pl.program_id(1) == 0)
        def _():
            dh_sc[...] = jnp.zeros_like(dh_sc)
            dd_ref[...] = jnp.zeros_like(dd_ref)
        for gi in range(gp):
            one_group(gi, dy_ref, xa_ref, dt_ref, acs_ref, acsr_ref, d_ref, hp_ref, seg_ref, seg4_ref,
                      dxa_ref, dacs_ref, ddt_ref, dd_ref, dh_sc)

    def one_group(gi, dy_ref, xa_ref, dt_ref, acs_ref, acsr_ref, d_ref, hp_ref, seg_ref, seg4_ref,
                  dxa_ref, dacs_ref, ddt_ref, dd_ref, dh_sc):
        dyv = dy_ref[:, gi * gw:(gi + 1) * gw]
        x = xa_ref[:, gi * gb:gi * gb + gw]
        bm = xa_ref[:, gi * gb + gw:gi * gb + gw + ns].astype(BF16)
        cm = xa_ref[:, gi * gb + gw + ns:(gi + 1) * gb].astype(BF16)
        acsr = acsr_ref[gi]
        dt_e, acs_e = dt_ref[:, gi * gw:(gi + 1) * gw], acs_ref[:, gi * gw:(gi + 1) * gw]
        acs_bc = _head_bcast(acs_e)
        e_e, dte_e, cd_e = _ssd_setup(acs_e, acsr)
        seg, seg4 = seg_ref[...], seg4_ref[...]
        lane = lax.broadcasted_iota(jnp.int32, (l, gw), 1)
        xdt = x * dt_e
        xdt16 = xdt.astype(BF16)
        dy16 = dyv.astype(BF16)
        cb = _dot(cm, bm, NT_DIMS)
        cbt = _dot(bm, cm, NT_DIMS)
        hp = hp_ref[gi]
        hp16 = hp.astype(BF16)
        g = dh_sc[gi]
        g16 = g.astype(BF16)
        t_all = _dot(cm, hp16)
        dt16 = (dyv * e_e).astype(BF16)
        dc = _dot(dt16, hp16, NT_DIMS)
        dhp = _dot(cm, dt16, TN_DIMS)
        wv = xdt * dte_e
        dw = _dot(bm, g16)
        db = _dot(wv.astype(BF16), g16, NT_DIMS)
        dxdt = dw * dte_e
        acs_term = dyv * t_all * e_e - dw * wv
        last_term = _colsum(dw * wv) + _colsum(g * hp) * cd_e
        dh_sc[gi] = g * cd_e + dhp
        tril, triu = _causal(l), _causal(l, transposed=True)
        dcb = jnp.zeros((l, l), F32)
        dcbt = jnp.zeros((l, l), F32)
        qd = []
        for r in range(4):
            in_head = (lane >= HEAD_DIM * r) & (lane < HEAD_DIM * (r + 1))
            a_col = acs_bc[r]
            lm = jnp.exp(jnp.where(tril, a_col - acsr[r:r + 1, :], -jnp.inf))
            lmt = jnp.exp(jnp.where(triu, acsr[r:r + 1, :] - a_col, -jnp.inf))
            mm_, mt = cb * lm, cbt * lmt
            dyr = jnp.where(in_head, dy16, jnp.zeros_like(dy16))
            dm = _dot(dyr, xdt16, NT_DIMS)
            dmt = _dot(xdt16, dyr, NT_DIMS)
            dxdt = dxdt + jnp.where(in_head, _dot(mt.astype(BF16), dy16), 0.0)
            dcb = dcb + dm * lm
            dcbt = dcbt + dmt * lmt
            qd.append(dm * mm_ - dmt * mt)
        dc = dc + _dot(dcb.astype(BF16), bm)
        db = db + _dot(dcbt.astype(BF16), cm)
        rowl = lax.broadcasted_iota(jnp.int32, (l, LANES), 0)
        row8 = lax.broadcasted_iota(jnp.int32, (8, gw), 0)
        small = _dot_split(jnp.where(row8 == 0, last_term, jnp.where(row8 == 1, _colsum(dyv * x), 0.0)), seg, passes=2)
        big = _dot_split(jnp.concatenate([acs_term, dxdt * x], axis=0), seg, passes=2)
        dacs = (big[0:l] + _dot_split(jnp.concatenate(qd, axis=1), seg4, passes=2)
                + jnp.where(rowl == l - 1, small[0:1, :], 0.0))
        dacs_ref[gi] = dacs[:, 0:4]
        ddt_ref[gi] = big[l:2 * l, 0:4]
        dd_ref[gi, 0:1, :] += small[1:2, :]
        dxa_ref[:, gi * gb:(gi + 1) * gb] = jnp.concatenate([dxdt * dt_e + dyv * d_ref[gi], db, dc], axis=1)

    rc = lambda c: nc - 1 - c
    return _pcall(
        body, name=name, grid=(SSM_GROUPS // gp, nc),
        in_specs=[pl.BlockSpec((l, gp * gw), lambda g, c: (rc(c), g)),
                  pl.BlockSpec((l, gp * gb), lambda g, c: (rc(c), g)),
                  pl.BlockSpec((l, gp * gw), lambda g, c: (rc(c), g)),
                  pl.BlockSpec((l, gp * gw), lambda g, c: (rc(c), g)),
                  pl.BlockSpec((gp, None, 4, l), lambda g, c: (g, rc(c), 0, 0)),
                  pl.BlockSpec((gp, 1, gw), lambda g, c: (g, 0, 0)),
                  pl.BlockSpec((gp, None, ns, gw), lambda g, c: (g, rc(c), 0, 0)),
                  pl.BlockSpec((gw, LANES), lambda g, c: (0, 0)),
                  pl.BlockSpec((4 * LANES, LANES), lambda g, c: (0, 0))],
        out_specs=[pl.BlockSpec((l, gp * gb), lambda g, c: (rc(c), g)),
                   pl.BlockSpec((gp, l, 4), lambda g, c: (g, rc(c), 0)),
                   pl.BlockSpec((gp, l, 4), lambda g, c: (g, rc(c), 0)),
                   pl.BlockSpec((gp, 8, LANES), lambda g, c: (g, 0, 0))],
        out_shape=[_sds((s, 4096), F32), _sds((SSM_GROUPS, s, 4), F32), _sds((SSM_GROUPS, s, 4), F32),
                   _sds((SSM_GROUPS, 8, LANES), F32)],
        scratch=[pltpu.VMEM((gp, ns, gw), F32)],
        sem=("parallel", "arbitrary"))(dy, xa, dte, acse, acsr, d_e, hprev, cst["seg"], cst["seg4"])


def _gate_norm(y, z, nw, name):
    c = y.shape[1]

    def fn(yv, zv, w):
        outs = []
        for k in range(c // SSM_GROUP_WIDTH):
            sl = slice(k * SSM_GROUP_WIDTH, (k + 1) * SSM_GROUP_WIDTH)
            yg = yv[:, sl] * _silu(zv[:, sl])
            rinv = lax.rsqrt(jnp.mean(yg * yg, axis=-1, keepdims=True) + RMS_EPS)
            outs.append(yg * rinv * w[:, sl])
        return (jnp.concatenate(outs, axis=1),)
    return _rowwise(fn, [y, z], [nw], [(c, BF16)], [], name=name)[0]


def _out_gate_norm_bwd(dy, w_out, y, z, nw, name):
    s, c = y.shape
    tm, tn = min(512, s), min(1024, c)
    k = dy.shape[1]

    def body(a_ref, b_ref, y_ref, z_ref, w_ref, dy_ref, dz_ref, dw_ref):
        @pl.when(pl.program_id(1) == 0)
        def _():
            dw_ref[...] = jnp.zeros_like(dw_ref)
        dv = _dot(a_ref[...], b_ref[...], NT_DIMS)
        yv, zv, w = y_ref[...], z_ref[...], w_ref[...]
        dys, dzs, dws = [], [], []
        for g in range(tn // SSM_GROUP_WIDTH):
            sl = slice(g * SSM_GROUP_WIDTH, (g + 1) * SSM_GROUP_WIDTH)
            ys, zs, ds = yv[:, sl], zv[:, sl], dv[:, sl]
            sz = _silu(zs)
            yg = ys * sz
            rinv = lax.rsqrt(jnp.mean(yg * yg, axis=-1, keepdims=True) + RMS_EPS)
            nrm = yg * rinv
            dn = ds * w[:, sl]
            dyg = rinv * (dn - nrm * jnp.mean(dn * nrm, axis=-1, keepdims=True))
            dys.append(dyg * sz)
            dzs.append(dyg * ys * _dsilu(zs))
            dws.append(_colsum(ds * nrm))
        dy_ref[...] = jnp.concatenate(dys, axis=1)
        dz_ref[...] = jnp.concatenate(dzs, axis=1).astype(dz_ref.dtype)
        dw_ref[...] += jnp.concatenate(dws, axis=1)

    tile = pl.BlockSpec((tm, tn), lambda j, i: (i, j))
    row = pl.BlockSpec((1, tn), lambda j, i: (0, j))
    return _pcall(body, name=name, grid=(c // tn, s // tm),
                  in_specs=[pl.BlockSpec((tm, k), lambda j, i: (i, 0)), pl.BlockSpec((tn, k), lambda j, i: (j, 0)),
                            tile, tile, row],
                  out_specs=[tile, tile, row], out_shape=[_sds((s, c), F32), _sds((s, c), BF16), _sds((1, c), F32)],
                  sem=("parallel", "arbitrary"))(dy, w_out, y, z, nw)


def _ssd_forward(u, w):
    s = u.shape[0]
    z = _mm(u, w["ssm_z"], "nn", [F32], name="ssm_zproj")
    xpre = _mm(u, w["ssm_xbc"], "nn", [F32], name="ssm_xproj")
    dtraw = _mm(u, w["ssm_dt"], "nn", [F32], name="ssm_dtproj")
    xc, xa = _conv_fwd(xpre, w["conv_w8"], w["conv_b"], "ssm_conv")
    dt, acs, dte, acse = _ssd_pre(dtraw, w["dt_bias"], w["a_log"], w["ssd_cst"], "ssm_pre")
    acsr = _heads_rows(acs, s)
    y, hprev = _ssd_fwd2(xa, dte, acse, acsr, w["d_e"], "ssm_scan")
    yn = _gate_norm(y, z, w["norm_w"], "ssm_gate_norm")
    out = _mm(yn, w["ssm_out"], "nn", [F32], name="ssm_oproj")
    return out, dict(u=u, z=z, xpre=xpre, xc=xc, xa=xa, dtraw=dtraw, dt=dt, dte=dte, acse=acse,
                     acsr=acsr, y=y, hprev=hprev, yn=yn)


def _ssd_backward(dy, sv, w):
    s = dy.shape[0]
    g_wout = _mm(sv["yn"], dy, "tn", [F32], name="ssm_gwout")
    dys, dz, dnw = _out_gate_norm_bwd(dy, w["ssm_out"], sv["y"], sv["z"], w["norm_w"], "ssm_dyn_gate_norm_bwd")
    dxa, dacs_c, ddt_c, dd = _ssd_bwd2(dys, sv["xa"], sv["dte"], sv["acse"], sv["acsr"], w["d_e"], sv["hprev"],
                                       w["ssd_cst"], "ssm_scan_bwd")
    pad = ((0, 0), (0, LANES - SSM_HEADS))
    dacs = jnp.pad(dacs_c.transpose(1, 0, 2).reshape(s, SSM_HEADS), pad)
    ddt = jnp.pad(ddt_c.transpose(1, 0, 2).reshape(s, SSM_HEADS), pad)
    draw, dbias, dalog = _ssd_post(dacs, ddt, sv["dtraw"], sv["dt"], w["dt_bias"], w["a_log"], "ssm_post")
    dxpre, dcw, dcb = _conv_bwd(dxa, sv["xc"], sv["xpre"], w["conv_w8"], "ssm_conv_bwd")
    du = _mm(draw, w["ssm_dt"], "nt", [F32], name="ssm_du_dt")
    n_z = dz.shape[1]
    du = _mm_nt_blocks(_k_blocks(dz), w["ssm_zx"][:, :n_z], du, name="ssm_du_z")
    du = _mm_nt_blocks(_k_blocks(dxpre), w["ssm_zx"][:, n_z:], du, name="ssm_du_x")
    g_win = jnp.concatenate([_mm(sv["u"], dz, "tn", [F32], name="ssm_gwin_z"),
                             _ungroup_cols(_mm(sv["u"], dxpre, "tn", [F32], name="ssm_gwin_x")),
                             _mm(sv["u"], draw, "tn", [F32], name="ssm_gwin_dt")[:, :SSM_HEADS]], axis=1)
    return du, dict(ssm_w_in=g_win, ssm_w_out=g_wout, ssm_conv_w=_ungroup_cols(dcw[:SSM_CONV]),
                    ssm_conv_b=_ungroup_cols(dcb), ssm_norm_w=dnw, ssm_dt_bias=dbias[:, :SSM_HEADS],
                    ssm_a_log=dalog[:, :SSM_HEADS], ssm_d=dd[:, 0, :4].reshape(1, SSM_HEADS))


def _mlp_forward(u2, w1, w2, tag):
    def epi(acc):
        hr = jnp.maximum(acc, 0.0)
        return hr, hr * hr
    hr, a = _mm(u2, w1, "nn", [BF16, BF16], name=tag + "_mlp_up", epi=epi)
    y2 = _mm(a, w2, "nn", [F32], name=tag + "_mlp_down")
    return y2, hr, a


def _mlp_backward(dy2, u2, hr, a, w1, w2, tag):
    dh = _mm(dy2, w2, "nt", [BF16], name=tag + "_mlp_dh", extra=(hr,),
             epi=lambda acc, h: (acc * (2.0 * h.astype(F32)),))
    g_w2 = _mm(a, dy2, "tn", [F32], name=tag + "_mlp_gw2")
    g_w1 = _mm(u2, dh, "tn", [F32], name=tag + "_mlp_gw1")
    du2 = _mm(dh, w1, "nt", [F32], name=tag + "_mlp_du")
    return du2, g_w1, g_w2


def _ada_forward(c16, ada_w, ada_b_cols, name):
    nl, d, cols = ada_w.shape
    tn = 512

    def body(c_ref, w_ref, b_ref, o_ref):
        cond = _silu(c_ref[...]).astype(BF16)
        o_ref[...] = _dot(cond, w_ref[...].astype(BF16)) + b_ref[...]

    return _pcall(body, name=name, grid=(nl, cols // tn),
                  in_specs=[pl.BlockSpec((16, d), lambda i, j: (0, 0)),
                            pl.BlockSpec((None, d, tn), lambda i, j: (i, 0, j)),
                            pl.BlockSpec((None, 1, tn), lambda i, j: (i, 0, j))],
                  out_specs=pl.BlockSpec((None, 16, tn), lambda i, j: (i, 0, j)),
                  out_shape=_sds((nl, 16, cols), F32), sem=("parallel", "parallel"))(c16, ada_w, ada_b_cols)


def _ada_backward(c_t, dmod_cols, name):
    d, nb = c_t.shape
    nl, _, cols = dmod_cols.shape
    tn = 512

    def body(c_ref, dm_ref, o_ref):
        cond = _silu(c_ref[...])
        dm = dm_ref[...]
        acc = cond[:, 0:1] * dm[0:1, :]
        for b in range(1, nb):
            acc = acc + cond[:, b:b + 1] * dm[b:b + 1, :]
        o_ref[...] = acc

    return _pcall(body, name=name, grid=(nl, cols // tn),
                  in_specs=[pl.BlockSpec((d, nb), lambda i, j: (0, 0)),
                            pl.BlockSpec((None, nb, tn), lambda i, j: (i, 0, j))],
                  out_specs=pl.BlockSpec((None, d, tn), lambda i, j: (i, 0, j)),
                  out_shape=_sds((nl, d, cols), F32), sem=("parallel", "parallel"))(c_t, dmod_cols)


def _adamw(w, g, m, v, name):
    rows, cols = w.shape
    tm = rows
    for cand in (256, 128, 64, 32, 16, 8):
        if rows % cand == 0 and rows > cand:
            tm = cand
            break
    c1 = 1.0 / (1.0 - ADAM_B1 ** ADAM_STEP)
    c2 = 1.0 / (1.0 - ADAM_B2 ** ADAM_STEP)

    def fn(wv, gv, mv, vv):
        mn = ADAM_B1 * mv + (1.0 - ADAM_B1) * gv
        vn = ADAM_B2 * vv + (1.0 - ADAM_B2) * (gv * gv)
        delta = -ADAM_LR * ((mn * c1) / (jnp.sqrt(vn * c2) + ADAM_EPS) + ADAM_WD * wv)
        return delta, mn, vn
    return _rowwise(fn, [w, g, m, v], [], [(cols, F32)] * 3, [], name=name, tm=tm)


def _my_pos():
    return lax.axis_index("x"), lax.axis_index("y"), lax.axis_index("c")


def _allgather8(x, name):
    r, c = x.shape

    def body(x_ref, out_ref, send_sems, recv_sems, local_sem):
        mx, my, mc = _my_pos()
        me = 4 * mx + 2 * my + mc
        mine = pltpu.make_async_copy(x_ref, out_ref.at[me], local_sem)
        mine.start()
        copies = []
        for k in range(1, 8):
            fx, fy, fc = (k >> 2) & 1, (k >> 1) & 1, k & 1
            px = 1 - mx if fx else mx
            py = 1 - my if fy else my
            pc = 1 - mc if fc else mc
            peer = 4 * px + 2 * py + pc
            send = pltpu.make_async_remote_copy(src_ref=x_ref, dst_ref=out_ref.at[me], send_sem=send_sems.at[k - 1],
                                                recv_sem=recv_sems.at[k - 1], device_id=(px, py, pc),
                                                device_id_type=MESH)
            send.start()
            recv = pltpu.make_async_remote_copy(src_ref=x_ref, dst_ref=out_ref.at[peer], send_sem=send_sems.at[k - 1],
                                                recv_sem=recv_sems.at[k - 1], device_id=(px, py, pc),
                                                device_id_type=MESH)
            copies.append((send, recv))
        for send, recv in copies:
            recv.wait_recv()
        for send, recv in copies:
            send.wait_send()
        mine.wait()

    vm = pl.BlockSpec(memory_space=pltpu.VMEM)
    return _pcall(body, name=name, in_specs=[vm], out_specs=vm, out_shape=_sds((8, r, c), x.dtype),
                  scratch=[pltpu.SemaphoreType.DMA((7,)), pltpu.SemaphoreType.DMA((7,)), pltpu.SemaphoreType.DMA])(x)


def _chip_flips(mx, my):
    out = []
    for fx, fy in ((1, 0), (0, 1), (1, 1)):
        px = 1 - mx if fx else mx
        py = 1 - my if fy else my
        out.append((px, py, 2 * px + py))
    return out


def _gather_chips(shard2, name):
    _, h, c = shard2.shape

    def body(x_ref, out_ref, send_sems, recv_sems):
        mx, my, mc = _my_pos()
        oc = 1 - mc
        mk = 2 * mx + my
        flips = _chip_flips(mx, my)

        def copy(k, src, dst, to):
            return pltpu.make_async_remote_copy(src_ref=src, dst_ref=dst, send_sem=send_sems.at[k],
                                                recv_sem=recv_sems.at[k], device_id=to, device_id_type=MESH)

        first = [copy(j, x_ref.at[mc], out_ref.at[mk, mc], (px, py, mc)) for j, (px, py, pk) in enumerate(flips)]
        for cp in first:
            cp.start()
        passed = []
        for j, (px, py, pk) in enumerate(flips):
            copy(j, x_ref.at[mc], out_ref.at[pk, mc], (px, py, mc)).wait_recv()
            fw = copy(3 + j, out_ref.at[pk, mc], out_ref.at[pk, mc], (mx, my, oc))
            fw.start()
            passed.append(fw)
        for j, (px, py, pk) in enumerate(flips):
            copy(3 + j, out_ref.at[pk, oc], out_ref.at[pk, oc], (mx, my, oc)).wait_recv()
        for cp in first + passed:
            cp.wait_send()

    return _pcall(body, name=name, in_specs=[HBM_SPEC], out_specs=HBM_SPEC, out_shape=_sds((4, 2, h, c), shard2.dtype),
                  scratch=[pltpu.SemaphoreType.DMA((6,)), pltpu.SemaphoreType.DMA((6,))])(shard2)


def _pair_exchange(g4, name):
    n, _, h, c = g4.shape

    def body(g_ref, out_ref, send_sem, recv_sem):
        mx, my, mc = _my_pos()
        oc = 1 - mc
        copies = []
        for k in range(n):
            cp = pltpu.make_async_remote_copy(src_ref=g_ref.at[k, oc], dst_ref=out_ref.at[k], send_sem=send_sem.at[k],
                                              recv_sem=recv_sem.at[k], device_id=(mx, my, oc), device_id_type=MESH)
            cp.start()
            copies.append(cp)
        for cp in copies:
            cp.wait_recv()
        for cp in copies:
            cp.wait_send()

    return _pcall(body, name=name, in_specs=[HBM_SPEC], out_specs=HBM_SPEC, out_shape=_sds((n, h, c), g4.dtype),
                  scratch=[pltpu.SemaphoreType.DMA((n,)), pltpu.SemaphoreType.DMA((n,))])(g4)


def _pair_add(g4, recv, core, name):
    n, _, h, c = g4.shape
    tm = _row_tile(h)

    def body(core_ref, a_ref, b_ref, o_ref, o16_ref):
        acc = a_ref[...] + b_ref[...]
        o_ref[...] = acc
        o16_ref[...] = acc.astype(BF16)

    out_spec = pl.BlockSpec((None, tm, c), lambda k, i, cr: (k, i, 0))
    return _pcall(body, name=name, grid=(n, h // tm), prefetch=1,
                  in_specs=[pl.BlockSpec((None, None, tm, c), lambda k, i, cr: (k, cr[0], i, 0)), out_spec],
                  out_specs=[out_spec, out_spec], out_shape=[_sds((n, h, c), F32), _sds((n, h, c), BF16)],
                  sem=("parallel", "parallel"))(core, g4, recv)


def _chip_exchange(p, name):
    n, h, c = p.shape

    def body(p_ref, out_ref, send_sems, recv_sems):
        mx, my, mc = _my_pos()
        copies = []
        for j, (px, py, pk) in enumerate(_chip_flips(mx, my)):
            cp = pltpu.make_async_remote_copy(src_ref=p_ref.at[pk], dst_ref=out_ref.at[j], send_sem=send_sems.at[j],
                                              recv_sem=recv_sems.at[j], device_id=(px, py, mc), device_id_type=MESH)
            cp.start()
            copies.append(cp)
        for cp in copies:
            cp.wait_recv()
        for cp in copies:
            cp.wait_send()

    return _pcall(body, name=name, in_specs=[HBM_SPEC], out_specs=HBM_SPEC, out_shape=_sds((3, h, c), p.dtype),
                  scratch=[pltpu.SemaphoreType.DMA((3,)), pltpu.SemaphoreType.DMA((3,))])(p)


def _chip_sum(p, slots, chip, name):
    _, h, c = p.shape
    tm = _row_tile(h)

    def body(chip_ref, p_ref, q_ref, o_ref):
        o_ref[...] = ((p_ref[...] + q_ref[0].astype(F32)) + q_ref[1].astype(F32)) + q_ref[2].astype(F32)

    return _pcall(body, name=name, grid=(h // tm,), prefetch=1,
                  in_specs=[pl.BlockSpec((None, tm, c), lambda i, ch: (ch[0], i, 0)),
                            pl.BlockSpec((3, tm, c), lambda i, ch: (0, i, 0))],
                  out_specs=pl.BlockSpec((tm, c), lambda i, ch: (i, 0)),
                  out_shape=_sds((h, c), F32), sem=("parallel",))(chip, p, slots)


def _sum_slots(q, name):
    n, h, c = q.shape
    tm = _row_tile(h)

    def body(q_ref, o_ref):
        acc = q_ref[0]
        for k in range(1, n):
            acc = acc + q_ref[k]
        o_ref[...] = acc

    return _pcall(body, name=name, grid=(h // tm,),
                  in_specs=[pl.BlockSpec((n, tm, c), lambda i: (0, i, 0))],
                  out_specs=pl.BlockSpec((tm, c), lambda i: (i, 0)),
                  out_shape=_sds((h, c), F32), sem=("parallel",))(q)


def _pair_share(f, name):
    h, c = f.shape

    def body(f_ref, out_ref, send_sem, recv_sem):
        mx, my, mc = _my_pos()
        cp = pltpu.make_async_remote_copy(src_ref=f_ref, dst_ref=out_ref, send_sem=send_sem, recv_sem=recv_sem,
                                          device_id=(mx, my, 1 - mc), device_id_type=MESH)
        cp.start()
        cp.wait_recv()
        cp.wait_send()

    return _pcall(body, name=name, in_specs=[HBM_SPEC], out_specs=HBM_SPEC, out_shape=_sds((h, c), f.dtype),
                  scratch=[pltpu.SemaphoreType.DMA, pltpu.SemaphoreType.DMA])(f)


BIG = ("mlp_w1", "mlp_w2", "fox_w_in", "fox_w_o", "ssm_w_in", "ssm_w_out")
SMALL_SHARDED = ("ssm_conv_w", "ssm_conv_b", "ssm_norm_w")
PACK_COLS = 1024


def _pack_rows(parts, rows_multiple, dtype):
    flat = jnp.concatenate([p.reshape(-1).astype(dtype) for p in parts])
    unit = rows_multiple * PACK_COLS
    total = -(-flat.shape[0] // unit) * unit
    flat = jnp.pad(flat, (0, total - flat.shape[0]))
    return flat.reshape(total // PACK_COLS, PACK_COLS)


def _unpack(flat, shapes):
    out, off = [], 0
    for sh in shapes:
        n = 1
        for d_ in sh:
            n *= d_
        out.append(flat[off:off + n].reshape(sh))
        off += n
    return out


PIECE_ROWS = 16


def _piece_rows(shape):
    n = 1
    for d_ in shape:
        n *= d_
    rows = -(-n // PACK_COLS)
    return n, -(-rows // PIECE_ROWS) * PIECE_ROWS


def _pack2d(parts, rows_multiple, dtype):
    blocks = []
    for p in parts:
        n, rows = _piece_rows(p.shape)
        a = p.astype(dtype)
        if p.shape[-1] != PACK_COLS or n % PACK_COLS:
            a = jnp.pad(a.reshape(-1), (0, -n % PACK_COLS))
        a = a.reshape(-1, PACK_COLS)
        blocks.append(jnp.pad(a, ((0, rows - a.shape[0]), (0, 0))))
    total = sum(b.shape[0] for b in blocks)
    pad = -total % rows_multiple
    if pad:
        blocks.append(jnp.zeros((pad, PACK_COLS), dtype))
    return jnp.concatenate(blocks, axis=0)


def _unpack2d(buf, shapes):
    out, off = [], 0
    for sh in shapes:
        n, rows = _piece_rows(sh)
        piece = buf[off:off + rows]
        if sh[-1] == PACK_COLS and n % PACK_COLS == 0:
            out.append(piece[:n // PACK_COLS].reshape(sh))
        else:
            out.append(piece.reshape(-1)[:n].reshape(sh))
        off += rows
    return out


def _row_tile(h, cap=512):
    for step in (16, 8):
        best = 0
        for cand in range(step, cap + 1, step):
            if h % cand == 0:
                best = cand
        if best:
            return best
    return h


def _chip_slice(full, axis, k, width):
    idx = [slice(None)] * full.ndim
    idx[axis] = slice(k * width, (k + 1) * width)
    return full[tuple(idx)]


SHARD_AXIS = dict(mlp_w1=2, mlp_w2=1, fox_w_in=2, fox_w_o=1, ssm_w_in=2, ssm_w_out=1, ssm_conv_w=2,
                  ssm_conv_b=1, ssm_norm_w=1, ada_w=2)


def kernel(x, c, ada_w, ada_b, ln_mix_g, ln_mix_b, ln_mlp_g, ln_mlp_b, mlp_w1, mlp_w2, fox_w_in, fox_b_f, fox_w_o, ssm_w_in, ssm_conv_w, ssm_conv_b, ssm_dt_bias, ssm_a_log, ssm_d, ssm_norm_w, ssm_w_out, loss_target, m_ada_w, m_ada_b, m_ln_mix_g, m_ln_mix_b, m_ln_mlp_g, m_ln_mlp_b, m_mlp_w1, m_mlp_w2, m_fox_w_in, m_fox_b_f, m_fox_w_o, m_ssm_w_in, m_ssm_conv_w, m_ssm_conv_b, m_ssm_dt_bias, m_ssm_a_log, m_ssm_d, m_ssm_norm_w, m_ssm_w_out, v_ada_w, v_ada_b, v_ln_mix_g, v_ln_mix_b, v_ln_mlp_g, v_ln_mlp_b, v_mlp_w1, v_mlp_w2, v_fox_w_in, v_fox_b_f, v_fox_w_o, v_ssm_w_in, v_ssm_conv_w, v_ssm_conv_b, v_ssm_dt_bias, v_ssm_a_log, v_ssm_d, v_ssm_norm_w, v_ssm_w_out):
    names = ("ada_w", "ada_b", "ln_mix_g", "ln_mix_b", "ln_mlp_g", "ln_mlp_b", "mlp_w1", "mlp_w2", "fox_w_in",
             "fox_b_f", "fox_w_o", "ssm_w_in", "ssm_conv_w", "ssm_conv_b", "ssm_dt_bias", "ssm_a_log", "ssm_d",
             "ssm_norm_w", "ssm_w_out")
    weights = dict(zip(names, (ada_w, ada_b, ln_mix_g, ln_mix_b, ln_mlp_g, ln_mlp_b, mlp_w1, mlp_w2, fox_w_in,
                               fox_b_f, fox_w_o, ssm_w_in, ssm_conv_w, ssm_conv_b, ssm_dt_bias, ssm_a_log, ssm_d,
                               ssm_norm_w, ssm_w_out)))
    m_in = dict(zip(names, (m_ada_w, m_ada_b, m_ln_mix_g, m_ln_mix_b, m_ln_mlp_g, m_ln_mlp_b, m_mlp_w1, m_mlp_w2,
                            m_fox_w_in, m_fox_b_f, m_fox_w_o, m_ssm_w_in, m_ssm_conv_w, m_ssm_conv_b, m_ssm_dt_bias,
                            m_ssm_a_log, m_ssm_d, m_ssm_norm_w, m_ssm_w_out)))
    v_in = dict(zip(names, (v_ada_w, v_ada_b, v_ln_mix_g, v_ln_mix_b, v_ln_mlp_g, v_ln_mlp_b, v_mlp_w1, v_mlp_w2,
                            v_fox_w_in, v_fox_b_f, v_fox_w_o, v_ssm_w_in, v_ssm_conv_w, v_ssm_conv_b, v_ssm_dt_bias,
                            v_ssm_a_log, v_ssm_d, v_ssm_norm_w, v_ssm_w_out)))

    mx, my, mc = _my_pos()
    chip = 2 * mx + my
    me = 4 * mx + 2 * my + mc
    x0 = x[0]
    target = loss_target[0]
    s, d = x0.shape
    n_qkv = 3 * FOX_HEADS * HEAD_DIM

    big_shapes = [weights[n].shape for n in BIG]
    packed = _pack2d([weights[n] for n in BIG], 32, BF16)
    gathered = _gather_chips(packed.reshape(2, packed.shape[0] // 2, PACK_COLS), "gather_weights")
    gathered = gathered.reshape(4, packed.shape[0], PACK_COLS)
    per_chip = [_unpack2d(jnp.where(chip == k, packed, gathered[k]), big_shapes) for k in range(4)]
    full = {n: jnp.concatenate([per_chip[k][i] for k in range(4)], axis=SHARD_AXIS[n]) for i, n in enumerate(BIG)}

    small_shapes = [weights[n].shape for n in SMALL_SHARDED]
    small_packed = _pack_rows([weights[n] for n in SMALL_SHARDED] + [c], 8, F32).reshape(-1, LANES)
    small_all = _allgather8(small_packed, "gather_small")
    small_chip = [_unpack(small_all[2 * k].reshape(-1), small_shapes) for k in range(4)]
    small_full = {n: jnp.concatenate([small_chip[k][i] for k in range(4)], axis=SHARD_AXIS[n])
                  for i, n in enumerate(SMALL_SHARDED)}
    n_small = sum(weights[n].size for n in SMALL_SHARDED)
    c_all = small_all.reshape(8, -1)[:, n_small:n_small + d]

    cols = ada_w.shape[2]
    ada_b_cols = lax.dynamic_slice_in_dim(ada_b, chip * cols, cols, axis=1)[:, None, :]
    c16 = jnp.pad(c_all, ((0, 8), (0, 0)))
    mod_part = _ada_forward(c16, ada_w, ada_b_cols, "ada_fwd")[:, :8, :]
    mod_all = _allgather8(mod_part.reshape(-1, LANES), "gather_mod").reshape(8, DEPTH, 8, cols)
    mod_mine = jnp.stack([lax.dynamic_index_in_dim(mod_all[2 * k], me, axis=1, keepdims=False) for k in range(4)], axis=1)
    mods = [jnp.pad(mod_mine[i].reshape(6, d), ((0, 2), (0, 0))) for i in range(DEPTH)]

    w = dict(
        fox_qkv=full["fox_w_in"][0][:, :n_qkv],
        fox_f=jnp.pad(full["fox_w_in"][0][:, n_qkv:], ((0, 0), (0, LANES - FOX_HEADS))),
        fox_o=full["fox_w_o"][0],
        fox_bf=jnp.pad(fox_b_f, ((0, 0), (0, LANES - FOX_HEADS))),
        head_ind=(jnp.arange(d)[:, None] // HEAD_DIM == jnp.arange(LANES)[None, :]).astype(BF16),
    )
    w.update(_ssm_weights(full["ssm_w_in"][0], small_full["ssm_conv_w"][0], small_full["ssm_conv_b"], ssm_dt_bias,
                          ssm_a_log, ssm_d, small_full["ssm_norm_w"], full["ssm_w_out"][0]))
    mixers = ((_fox_forward, _fox_backward), (_ssd_forward, _ssd_backward))

    saved = []
    xin = x0
    u = _modulate_in(x0, mods[0], "l0_mod_in")
    for i in range(DEPTH):
        tag = "l%d" % i
        y, sv = mixers[i % 2][0](u, w)
        r, x1, u2 = _res_ln_mod(xin, y, mods[i], ln_mix_g[i:i + 1], ln_mix_b[i:i + 1], tag + "_res_ln1")
        y2, hr, a = _mlp_forward(u2, full["mlp_w1"][i], full["mlp_w2"][i], tag)
        if i + 1 < DEPTH:
            r2, xin, u = _res_ln_next(x1, y2, mods[i], ln_mlp_g[i:i + 1], ln_mlp_b[i:i + 1], mods[i + 1],
                                      tag + "_res_ln2")
        else:
            r2 = _rowwise(lambda xv, yv, m: (ALPHA * xv + (1.0 + m[5:6]) * yv,), [x1, y2], [mods[i]], [(d, F32)], [],
                          name=tag + "_res2")[0]
        saved.append(dict(y=y, r=r, u2=u2, hr=hr, a=a, y2=y2, r2=r2, mix=sv))

    grads = {}
    dmod_parts = [dict() for _ in range(DEPTH)]
    ln_grads = {n: [None] * DEPTH for n in ("ln_mix_g", "ln_mix_b", "ln_mlp_g", "ln_mlp_b")}
    g_w1, g_w2 = [None] * DEPTH, [None] * DEPTH
    du = dres0 = None
    for i in reversed(range(DEPTH)):
        tag = "l%d" % i
        sv = saved[i]
        if i + 1 == DEPTH:
            dy2, dres, sq, dg2, db2, dgm = _loss_ln2_bwd(sv["r2"], sv["y2"], target, mods[i], ln_mlp_g[i:i + 1],
                                                         ln_mlp_b[i:i + 1], "loss_ln2_bwd")
            loss = lax.psum(0.5 * jnp.sum(sq) / d, ("x", "y", "c"))
        else:
            dy2, dres, dsca, dsha, dg2, db2, dgm = _mod_in_ln2_bwd(du, dres0, sv["r2"], sv["y2"], mods[i],
                                                                   ln_mlp_g[i:i + 1], ln_mlp_b[i:i + 1], mods[i + 1],
                                                                   tag + "_ln2_bwd")
            dmod_parts[i + 1].update(sc_a=dsca, sh_a=dsha)
        du2, g_w1[i], g_w2[i] = _mlp_backward(dy2, sv["u2"], sv["hr"], sv["a"], full["mlp_w1"][i], full["mlp_w2"][i], tag)
        dy, dres0, dscm, dshm, dg1, db1, dga = _ln1_bwd(du2, dres, sv["r"], sv["y"], mods[i], ln_mix_g[i:i + 1],
                                                        ln_mix_b[i:i + 1], tag + "_ln1_bwd")
        du, mg = mixers[i % 2][1](dy, sv["mix"], w)
        grads.update(mg)
        dmod_parts[i].update(g_a=dga, sh_m=dshm, sc_m=dscm, g_m=dgm)
        ln_grads["ln_mix_g"][i], ln_grads["ln_mix_b"][i] = dg1, db1
        ln_grads["ln_mlp_g"][i], ln_grads["ln_mlp_b"][i] = dg2, db2
    dx, dsca, dsha = _mod_in_bwd(du, dres0, x0, mods[0], "l0_mod_in_bwd")
    dmod_parts[0].update(sc_a=dsca, sh_a=dsha)
    dmods = [jnp.concatenate([p["sh_a"], p["sc_a"], p["g_a"], p["sh_m"], p["sc_m"], p["g_m"]], axis=1)
             for p in dmod_parts]
    grad_x = dx[None]
    grads["mlp_w1"] = jnp.stack(g_w1)
    grads["mlp_w2"] = jnp.stack(g_w2)
    for n in ("fox_w_in", "fox_w_o", "ssm_w_in", "ssm_w_out", "ssm_conv_w"):
        grads[n] = grads[n][None]

    small_names = ("ln_mix_g", "ln_mix_b", "ln_mlp_g", "ln_mlp_b", "fox_b_f", "ssm_dt_bias", "ssm_a_log", "ssm_d")
    small_parts = list(dmods)
    for n in small_names[:4]:
        small_parts.append(jnp.concatenate(ln_grads[n], axis=0))
    for n in small_names[4:]:
        small_parts.append(jnp.pad(grads[n], ((0, 0), (0, LANES - grads[n].shape[1]))))
    small_vec = _pack_rows(small_parts, 1, F32).reshape(-1, LANES)
    small_vec = jnp.pad(small_vec, ((0, -small_vec.shape[0] % 8), (0, 0)))
    small_g_all = _allgather8(small_vec, "gather_small_grads")
    small_sum = _sum_slots(small_g_all, "sum_small_grads").reshape(-1)
    dmod_sum = small_sum[:DEPTH * 6 * d].reshape(DEPTH, 6 * d)
    off = DEPTH * 6 * d
    final = {"ada_b": dmod_sum}
    for n in small_names[:4]:
        final[n] = small_sum[off:off + DEPTH * d].reshape(DEPTH, d)
        off += DEPTH * d
    for n in small_names[4:]:
        width = weights[n].shape[1]
        final[n] = small_sum[off:off + width].reshape(1, width)
        off += LANES

    dmod_all = small_g_all.reshape(8, -1)[:, :DEPTH * 6 * d].reshape(8, DEPTH, 6 * d)
    dmod_cols = lax.dynamic_slice_in_dim(dmod_all, chip * cols, cols, axis=2).transpose(1, 0, 2)
    final["ada_w"] = _ada_backward(c_all.T, dmod_cols, "ada_bwd")

    sharded = BIG + SMALL_SHARDED
    shard_shapes = [weights[n].shape for n in sharded]
    per_target = []
    for k in range(4):
        parts = [_chip_slice(grads[n], SHARD_AXIS[n], k, weights[n].shape[SHARD_AXIS[n]]) for n in sharded]
        per_target.append(_pack2d(parts, 128, F32))
    g_all = jnp.stack(per_target)
    rows = g_all.shape[1]
    g4 = g_all.reshape(4, 2, rows // 2, PACK_COLS)
    recv = _pair_exchange(g4, "rs_pair_exchange")
    part, part16 = _pair_add(g4, recv, jnp.reshape(mc, (1,)).astype(jnp.int32), "rs_pair_add")
    slots = _chip_exchange(part16, "rs_chip_exchange")
    half = _chip_sum(part, slots, jnp.reshape(chip, (1,)).astype(jnp.int32), "rs_chip_sum")
    other = _pair_share(half, "rs_pair_share")
    both = jnp.concatenate([jnp.where(mc == 0, half, other), jnp.where(mc == 0, other, half)], axis=0)
    for n, g_shard in zip(sharded, _unpack2d(both, shard_shapes)):
        final[n] = g_shard

    outs_g, outs_d, outs_m, outs_v = [], [], [], []
    for n in names:
        wv = weights[n]
        two_d = (-1, wv.shape[-1])
        delta, mn, vn = _adamw(wv.reshape(two_d), final[n].reshape(two_d), m_in[n].reshape(two_d),
                               v_in[n].reshape(two_d), "adamw_" + n)
        outs_g.append(final[n].reshape(wv.shape))
        outs_d.append(delta.reshape(wv.shape))
        outs_m.append(mn.reshape(wv.shape))
        outs_v.append(vn.reshape(wv.shape))
    return (loss, grad_x, *outs_g, *outs_d, *outs_m, *outs_v)
```

```python
import functools

import jax
import jax.numpy as jnp
from jax import lax
from jax.experimental import pallas as pl
from jax.experimental.pallas import tpu as pltpu

F32, BF16 = jnp.float32, jnp.bfloat16
MESH = pl.DeviceIdType.MESH
HBM_SPEC = pl.BlockSpec(memory_space=pltpu.HBM)

VMEM_LIMIT_BYTES = 52 * 2**20
LANES = 128

FOX_HEADS, HEAD_DIM = 16, 64
SSM_HEADS, SSM_GROUPS, SSM_STATE, SSM_CHUNK, SSM_CONV = 32, 8, 128, 128, 4
SSM_GROUP_WIDTH = 256
LN_EPS, RMS_EPS = 1e-5, 1e-5
DEPTH = 2
ALPHA = (2.0 * DEPTH) ** 0.25
ADAM_LR, ADAM_B1, ADAM_B2, ADAM_EPS, ADAM_WD, ADAM_STEP = 0.001, 0.9, 0.999, 1e-08, 0.01, 10

ATT_TILE = 512
ROW_TILE = 512
SCAN_TILE = 512
SSD_GROUPS_PER_STEP = 4
MM_TM, MM_TN, MM_TK = 1024, 1024, 1024

NT_DIMS = (((1,), (1,)), ((), ()))
TN_DIMS = (((0,), (0,)), ((), ()))
NN_DIMS = (((1,), (0,)), ((), ()))


def _pcall(body, *, name, out_shape, grid=(), in_specs=None, out_specs=None, scratch=(), sem=None, prefetch=0):
    params = dict(vmem_limit_bytes=VMEM_LIMIT_BYTES)
    if sem is not None:
        params["dimension_semantics"] = sem
    if prefetch:
        grid_spec = pltpu.PrefetchScalarGridSpec(num_scalar_prefetch=prefetch, grid=grid, in_specs=in_specs,
                                                 out_specs=out_specs, scratch_shapes=scratch)
        return pl.pallas_call(body, out_shape=out_shape, grid_spec=grid_spec, name=name,
                              compiler_params=pltpu.CompilerParams(**params))
    kwargs = {}
    if in_specs is not None:
        kwargs["in_specs"] = in_specs
    if out_specs is not None:
        kwargs["out_specs"] = out_specs
    return pl.pallas_call(body, out_shape=out_shape, grid=grid, scratch_shapes=scratch, name=name,
                          compiler_params=pltpu.CompilerParams(**params), **kwargs)


def _sds(shape, dtype):
    return jax.ShapeDtypeStruct(tuple(shape), dtype)


def _dot(a, b, dims=NN_DIMS):
    return lax.dot_general(a, b, dims, preferred_element_type=F32)


def _sigmoid(x):
    return 1.0 / (1.0 + jnp.exp(-x))


def _silu(x):
    return x * _sigmoid(x)


def _dsilu(x):
    s = _sigmoid(x)
    return s * (1.0 + x * (1.0 - s))


def _dot_split(x, m16, dims=NN_DIMS, passes=3):
    hi = x.astype(BF16)
    r1 = x - hi.astype(F32)
    mid = r1.astype(BF16)
    out = _dot(hi, m16, dims) + _dot(mid, m16, dims)
    if passes == 3:
        lo = (r1 - mid.astype(F32)).astype(BF16)
        out = out + _dot(lo, m16, dims)
    return out


def _mm(a, b, dims, outs, *, name, tm=MM_TM, tn=MM_TN, tk=MM_TK, epi=None, extra=()):
    if dims == "nn":
        (m, k), (k2, n) = a.shape, b.shape
    elif dims == "nt":
        (m, k), (n, k2) = a.shape, b.shape
    else:
        (k, m), (k2, n) = a.shape, b.shape
    assert k == k2, (a.shape, b.shape, dims)
    tm, tn, tk = min(tm, m), min(tn, n), min(tk, k)
    assert m % tm == 0 and n % tn == 0 and k % tk == 0, (m, n, k, tm, tn, tk)
    nk = k // tk
    dn = {"nn": NN_DIMS, "nt": NT_DIMS, "tn": TN_DIMS}[dims]
    n_extra, n_out = len(extra), len(outs)
    if epi is None:
        epi = lambda acc: (acc,) * n_out

    def body(a_ref, b_ref, *rest):
        extra_refs, out_refs, acc_ref = rest[:n_extra], rest[n_extra:n_extra + n_out], rest[-1]
        kk = pl.program_id(2)

        @pl.when(kk == 0)
        def _():
            acc_ref[...] = jnp.zeros_like(acc_ref)

        acc_ref[...] += _dot(a_ref[...].astype(BF16), b_ref[...].astype(BF16), dn)

        @pl.when(kk == nk - 1)
        def _():
            res = epi(acc_ref[...], *[e[...] for e in extra_refs])
            for o, r in zip(out_refs, res):
                o[...] = r.astype(o.dtype)

    if dims == "tn":
        a_spec = pl.BlockSpec((tk, tm), lambda i, j, kk: (kk, i))
    else:
        a_spec = pl.BlockSpec((tm, tk), lambda i, j, kk: (i, kk))
    if dims == "nt":
        b_spec = pl.BlockSpec((tn, tk), lambda i, j, kk: (j, kk))
    else:
        b_spec = pl.BlockSpec((tk, tn), lambda i, j, kk: (kk, j))
    o_spec = pl.BlockSpec((tm, tn), lambda i, j, kk: (i, j))
    res = _pcall(body, name=name, grid=(m // tm, n // tn, nk),
                 in_specs=[a_spec, b_spec] + [o_spec] * n_extra,
                 out_specs=[o_spec] * n_out,
                 out_shape=[_sds((m, n), d) for d in outs],
                 scratch=[pltpu.VMEM((tm, tn), F32)],
                 sem=("parallel", "parallel", "arbitrary"))(a, b, *extra)
    return res[0] if n_out == 1 else res


def _k_blocks(a, tk=None):
    tk = tk or MM_TK
    return [(a, kb) for kb in range(a.shape[1] // tk)]


def _mm_nt_blocks(a_blocks, b, start, *, name, tm=MM_TM, tk=None):
    tk = tk or MM_TK
    tm = min(tm, a_blocks[0][0].shape[0])
    m, n, p_n = a_blocks[0][0].shape[0], b.shape[0], len(a_blocks)
    assert b.shape[1] == p_n * tk and m % tm == 0

    def body(*refs):
        a_refs = refs[:p_n]
        b_ref, s_ref, o_ref, acc_ref = refs[p_n:]
        kk = pl.program_id(1)

        @pl.when(kk == 0)
        def _():
            acc_ref[...] = s_ref[...]
        for p in range(p_n):
            @pl.when(kk == p)
            def _(p=p):
                acc_ref[...] += _dot(a_refs[p][...].astype(BF16), b_ref[...].astype(BF16), NT_DIMS)

        @pl.when(kk == p_n - 1)
        def _():
            o_ref[...] = acc_ref[...]

    in_specs = [pl.BlockSpec((tm, tk), functools.partial(lambda kb, i, kk: (i, kb), kb)) for _, kb in a_blocks]
    in_specs += [pl.BlockSpec((n, tk), lambda i, kk: (0, kk)), pl.BlockSpec((tm, n), lambda i, kk: (i, 0))]
    return _pcall(body, name=name, grid=(m // tm, p_n), in_specs=in_specs,
                  out_specs=pl.BlockSpec((tm, n), lambda i, kk: (i, 0)), out_shape=_sds((m, n), F32),
                  scratch=[pltpu.VMEM((tm, n), F32)], sem=("parallel", "arbitrary"))(*[a for a, _ in a_blocks], b, start)


def _rowwise(fn, rows, consts, row_outs, acc_outs, *, name, tm=ROW_TILE):
    s = rows[0].shape[0]
    tm = min(tm, s)
    assert s % tm == 0
    n_in, n_o = len(rows) + len(consts), len(row_outs)

    def body(*refs):
        ins, outs = refs[:n_in], refs[n_in:]
        res = fn(*[r[...] for r in ins])
        if not isinstance(res, (tuple, list)):
            res = (res,)
        for o, val in zip(outs[:n_o], res[:n_o]):
            o[...] = val.astype(o.dtype)
        if acc_outs:
            @pl.when(pl.program_id(0) == 0)
            def _():
                for o in outs[n_o:]:
                    o[...] = jnp.zeros_like(o)
            for o, val in zip(outs[n_o:], res[n_o:]):
                o[...] += val

    in_specs = [pl.BlockSpec((tm, r.shape[1]), lambda i: (i, 0)) for r in rows]
    in_specs += [pl.BlockSpec(c.shape, functools.partial(lambda nd, i: (0,) * nd, c.ndim)) for c in consts]
    out_specs = [pl.BlockSpec((tm, c), lambda i: (i, 0)) for c, _ in row_outs]
    out_specs += [pl.BlockSpec(tuple(sh), lambda i: (0, 0)) for sh in acc_outs]
    out_shape = [_sds((s, c), d) for c, d in row_outs] + [_sds(sh, F32) for sh in acc_outs]
    res = _pcall(body, name=name, grid=(s // tm,), in_specs=in_specs, out_specs=out_specs,
                 out_shape=out_shape, sem=("arbitrary",))(*rows, *consts)
    return res


def _colsum(x):
    return jnp.sum(x, axis=0, keepdims=True)


def _ln_stats(r):
    mu = jnp.mean(r, axis=-1, keepdims=True)
    xc = r - mu
    var = jnp.mean(xc * xc, axis=-1, keepdims=True)
    rstd = lax.rsqrt(var + LN_EPS)
    return xc * rstd, rstd


def _ln_bwd(dy, xhat, rstd, gamma):
    dyg = dy * gamma
    m1 = jnp.mean(dyg, axis=-1, keepdims=True)
    m2 = jnp.mean(dyg * xhat, axis=-1, keepdims=True)
    return rstd * (dyg - m1 - xhat * m2)


def _modulate_in(x, mod, name):
    def fn(xv, m):
        return (xv * (1.0 + m[1:2]) + m[0:1],)
    return _rowwise(fn, [x], [mod], [(x.shape[1], BF16)], [], name=name)[0]


def _res_ln_mod(x, y, mod, g, b, name):
    d = x.shape[1]

    def fn(xv, yv, m, gv, bv):
        r = ALPHA * xv + (1.0 + m[2:3]) * yv
        xhat, _ = _ln_stats(r)
        x1 = xhat * gv + bv
        u2 = x1 * (1.0 + m[4:5]) + m[3:4]
        return r, x1, u2
    return _rowwise(fn, [x, y], [mod, g, b], [(d, F32), (d, F32), (d, BF16)], [], name=name)


def _res_ln_next(x, y, mod, g, b, mod_next, name):
    d = x.shape[1]

    def fn(xv, yv, m, gv, bv, mn):
        r = ALPHA * xv + (1.0 + m[5:6]) * yv
        xhat, _ = _ln_stats(r)
        out = xhat * gv + bv
        return r, out, out * (1.0 + mn[1:2]) + mn[0:1]
    return _rowwise(fn, [x, y], [mod, g, b, mod_next], [(d, F32), (d, F32), (d, BF16)], [], name=name)


def _loss_ln2_bwd(r2, y2, target, mod, g, b, name):
    d = r2.shape[1]

    def fn(rv, yv, tv, m, gv, bv):
        xhat, rstd = _ln_stats(rv)
        e = xhat * gv + bv - tv
        dxv = e * (1.0 / d)
        dr = _ln_bwd(dxv, xhat, rstd, gv)
        return (dr * (1.0 + m[5:6]), ALPHA * dr,
                _colsum(e * e), _colsum(dxv * xhat), _colsum(dxv), _colsum(dr * yv))
    return _rowwise(fn, [r2, y2, target], [mod, g, b], [(d, BF16), (d, F32)], [(1, d)] * 4, name=name)


def _mod_in_ln2_bwd(du, dres, r2, y2, mod, g, b, mod_next, name):
    d = du.shape[1]

    def fn(duv, drv, rv, yv, m, gv, bv, mn):
        xhat, rstd = _ln_stats(rv)
        xout = xhat * gv + bv
        dxv = duv * (1.0 + mn[1:2]) + drv
        dr = _ln_bwd(dxv, xhat, rstd, gv)
        return (dr * (1.0 + m[5:6]), ALPHA * dr,
                _colsum(duv * xout), _colsum(duv), _colsum(dxv * xhat), _colsum(dxv), _colsum(dr * yv))
    return _rowwise(fn, [du, dres, r2, y2], [mod, g, b, mod_next], [(d, BF16), (d, F32)], [(1, d)] * 5, name=name)


def _ln1_bwd(du2, dres, r, y, mod, g, b, name):
    d = du2.shape[1]

    def fn(duv, drv, rv, yv, m, gv, bv):
        xhat, rstd = _ln_stats(rv)
        x1 = xhat * gv + bv
        dx1 = duv * (1.0 + m[4:5]) + drv
        dr = _ln_bwd(dx1, xhat, rstd, gv)
        return (dr * (1.0 + m[2:3]), ALPHA * dr,
                _colsum(duv * x1), _colsum(duv), _colsum(dx1 * xhat), _colsum(dx1), _colsum(dr * yv))
    return _rowwise(fn, [du2, dres, r, y], [mod, g, b], [(d, BF16), (d, F32)], [(1, d)] * 5, name=name)


def _mod_in_bwd(du, dres, x, mod, name):
    d = du.shape[1]

    def fn(duv, drv, xv, m):
        return duv * (1.0 + m[1:2]) + drv, _colsum(duv * xv), _colsum(duv)
    return _rowwise(fn, [du, dres, x], [mod], [(d, F32)], [(1, d)] * 2, name=name)


def _fox_gate(fraw, b_pad, name):
    s = fraw.shape[0]
    tb = min(SCAN_TILE, s)

    def body(f_ref, b_ref, cum_ref, carry):
        @pl.when(pl.program_id(0) == 0)
        def _():
            carry[...] = jnp.zeros_like(carry)
        z = f_ref[...] + b_ref[...]
        lf = jnp.minimum(z, 0.0) - jnp.log(1.0 + jnp.exp(-jnp.abs(z)))
        lane = lax.broadcasted_iota(jnp.int32, (tb, LANES), 1)
        row = lax.broadcasted_iota(jnp.int32, (tb, LANES), 0)
        c = jnp.where(lane < FOX_HEADS, lf, 0.0)
        sh = 1
        while sh < tb:
            c = c + jnp.where(row >= sh, pltpu.roll(c, sh, 0), 0.0)
            sh *= 2
        c = c + carry[0:1, :]
        cum_ref[...] = c
        carry[0:1, :] = c[tb - 1:tb, :]

    return _pcall(body, name=name, grid=(s // tb,),
                  in_specs=[pl.BlockSpec((tb, LANES), lambda i: (i, 0)), pl.BlockSpec((1, LANES), lambda i: (0, 0))],
                  out_specs=pl.BlockSpec((tb, LANES), lambda i: (i, 0)),
                  out_shape=_sds((s, LANES), F32), scratch=[pltpu.VMEM((8, LANES), F32)],
                  sem=("arbitrary",))(fraw, b_pad)


def _fox_gate_bwd(drow, dcol, fraw, b_pad, name):
    s = fraw.shape[0]
    tb = min(SCAN_TILE, s)
    n = s // tb

    def body(dr_ref, dc_ref, f_ref, b_ref, df_ref, db_ref, carry):
        @pl.when(pl.program_id(0) == 0)
        def _():
            carry[...] = jnp.zeros_like(carry)
            db_ref[...] = jnp.zeros_like(db_ref)
        row = lax.broadcasted_iota(jnp.int32, (tb, LANES), 0)
        c = dr_ref[...] + dc_ref[...]
        sh = 1
        while sh < tb:
            c = c + jnp.where(row + sh < tb, pltpu.roll(c, tb - sh, 0), 0.0)
            sh *= 2
        c = c + carry[0:1, :]
        carry[0:1, :] = c[0:1, :]
        z = f_ref[...] + b_ref[...]
        df = c * (1.0 / (1.0 + jnp.exp(z)))
        df_ref[...] = df.astype(df_ref.dtype)
        db_ref[...] += _colsum(df)

    rev = lambda i: (n - 1 - i, 0)
    return _pcall(body, name=name, grid=(n,),
                  in_specs=[pl.BlockSpec((tb, LANES), rev)] * 3 + [pl.BlockSpec((1, LANES), lambda i: (0, 0))],
                  out_specs=[pl.BlockSpec((tb, LANES), rev), pl.BlockSpec((1, LANES), lambda i: (0, 0))],
                  out_shape=[_sds((s, LANES), BF16), _sds((1, LANES), F32)],
                  scratch=[pltpu.VMEM((8, LANES), F32)], sem=("arbitrary",))(drow, dcol, fraw, b_pad)


def _head_pair_masks(t):
    lane = lax.broadcasted_iota(jnp.int32, (t, LANES), 1)
    return lane < HEAD_DIM


def _lane_blocks(x):
    return [x[:, c * LANES:(c + 1) * LANES] for c in range(x.shape[1] // LANES)]


def _sum_list(xs):
    acc = xs[0]
    for x in xs[1:]:
        acc = acc + x
    return acc


def _causal(t, transposed=False):
    ri = lax.broadcasted_iota(jnp.int32, (t, t), 0)
    ci = lax.broadcasted_iota(jnp.int32, (t, t), 1)
    return ci >= ri if transposed else ri >= ci


def _span_mask(r0, r1, c0, c1, transposed=False):
    ri = lax.broadcasted_iota(jnp.int32, (r1 - r0, c1 - c0), 0) + r0
    ci = lax.broadcasted_iota(jnp.int32, (r1 - r0, c1 - c0), 1) + c0
    return ci >= ri if transposed else ri >= ci


def _full_spans(t):
    return ((0, t, 0, t, False),)


def _diagonal_spans(t, transposed=False):
    h = t // 2
    if h % LANES:
        return ((0, t, 0, t, True),)
    if transposed:
        return ((0, h, 0, t, True), (h, t, h, t, True))
    return ((0, h, 0, h, True), (h, t, 0, t, True))


def _flash_fwd(qkv, ck_rows, kb_start, name):
    s = qkv.shape[0]
    t = min(ATT_TILE, s)
    nq = s // t
    scale = HEAD_DIM ** -0.5
    hp_blocks = FOX_HEADS // 2

    def body(ks_ref, q_ref, k_ref, v_ref, ck_ref, o_ref, lse_ref, acc_ref, m_ref, l_ref):
        hp, qb = pl.program_id(0), pl.program_id(1)
        q2 = q_ref[...] * scale
        first = _head_pair_masks(t)
        zero = jnp.zeros_like(q2)
        qs = (jnp.where(first, q2, zero), jnp.where(first, zero, q2))
        m_ref[...] = jnp.full_like(m_ref, -jnp.inf)
        l_ref[...] = jnp.zeros_like(l_ref)
        acc_ref[...] = jnp.zeros_like(acc_ref)

        def tile(kb, spans):
            off = pl.multiple_of(kb * t, t)
            k2 = k_ref[pl.ds(off, t), :]
            v2 = v_ref[pl.ds(off, t), :]
            ck = ck_ref[kb]
            for r0, r1, c0, c1, masked in spans:
                kk, vv, fr = k2[c0:c1], v2[c0:c1], first[r0:r1]
                pvs, als = [], []
                for j in range(2):
                    sc = _dot(qs[j][r0:r1], kk, NT_DIMS) - ck[j:j + 1, c0:c1]
                    if masked:
                        sc = jnp.where(_span_mask(r0, r1, c0, c1), sc, -jnp.inf)
                    blocks = _lane_blocks(sc)
                    mx = blocks[0]
                    for b in blocks[1:]:
                        mx = jnp.maximum(mx, b)
                    m_old = m_ref[j, r0:r1]
                    m_new = jnp.maximum(m_old, jnp.max(mx, axis=1, keepdims=True))
                    ps = [jnp.exp(b - m_new) for b in blocks]
                    a = jnp.exp(m_old - m_new)
                    l_ref[j, r0:r1] = a * l_ref[j, r0:r1] + _sum_list(ps)
                    m_ref[j, r0:r1] = m_new
                    pvs.append(_dot(jnp.concatenate(ps, axis=1).astype(BF16), vv))
                    als.append(a)
                acc_ref[r0:r1] = jnp.where(fr, als[0], als[1]) * acc_ref[r0:r1] + jnp.where(fr, pvs[0], pvs[1])

        def step(kb, carry):
            tile(kb, _full_spans(t))
            return carry

        lax.fori_loop(ks_ref[hp, qb], qb, step, 0)
        tile(qb, _diagonal_spans(t))
        l0 = jnp.sum(l_ref[0], axis=1, keepdims=True)
        l1 = jnp.sum(l_ref[1], axis=1, keepdims=True)
        o_ref[...] = acc_ref[...] / jnp.where(first, l0, l1)
        lse_ref[:, 0:1] = m_ref[0][:, 0:1] + jnp.log(l0)
        lse_ref[:, 1:2] = m_ref[1][:, 0:1] + jnp.log(l1)

    return _pcall(
        body, name=name, grid=(hp_blocks, nq), prefetch=1,
        in_specs=[pl.BlockSpec((t, LANES), lambda h, i, ks: (i, h)),
                  pl.BlockSpec((s, LANES), lambda h, i, ks: (0, hp_blocks + h)),
                  pl.BlockSpec((s, LANES), lambda h, i, ks: (0, 2 * hp_blocks + h)),
                  pl.BlockSpec((None, nq, 2, t), lambda h, i, ks: (h, 0, 0, 0))],
        out_specs=[pl.BlockSpec((t, LANES), lambda h, i, ks: (i, h)),
                   pl.BlockSpec((None, t, 2), lambda h, i, ks: (h, i, 0))],
        out_shape=[_sds((s, hp_blocks * LANES), F32), _sds((hp_blocks, s, 2), F32)],
        scratch=[pltpu.VMEM((t, LANES), F32), pltpu.VMEM((2, t, LANES), F32), pltpu.VMEM((2, t, LANES), F32)],
        sem=("parallel", "arbitrary"))(kb_start, qkv, qkv, qkv, ck_rows)


def _flash_dq(qkv, do16, ck_rows, lse_c, dl_c, kb_start, name):
    s = qkv.shape[0]
    t = min(ATT_TILE, s)
    nq = s // t
    scale = HEAD_DIM ** -0.5
    hp_blocks = FOX_HEADS // 2

    def body(ks_ref, q_ref, do_ref, k_ref, v_ref, ck_ref, lse_ref, dl_ref, dq_ref, drow_ref, acc_ref, row_acc):
        hp, qb = pl.program_id(0), pl.program_id(1)
        q2, do2 = q_ref[...] * scale, do_ref[...]
        first = _head_pair_masks(t)
        zero = jnp.zeros_like(q2)
        qs = (jnp.where(first, q2, zero), jnp.where(first, zero, q2))
        dos = (jnp.where(first, do2, zero), jnp.where(first, zero, do2))
        lse, dl = lse_ref[...], dl_ref[...]
        lse_b = [jnp.broadcast_to(lse[:, j:j + 1], (t, LANES)) for j in range(2)]
        dl_b = [jnp.broadcast_to(dl[:, j:j + 1], (t, LANES)) for j in range(2)]
        acc_ref[...] = jnp.zeros_like(acc_ref)
        row_acc[...] = jnp.zeros_like(row_acc)

        def tile(kb, spans):
            off = pl.multiple_of(kb * t, t)
            k2 = k_ref[pl.ds(off, t), :]
            v2 = v_ref[pl.ds(off, t), :]
            ck = ck_ref[kb]
            for r0, r1, c0, c1, masked in spans:
                kk, vv = k2[c0:c1], v2[c0:c1]
                dqs = []
                for j in range(2):
                    sc = _dot(qs[j][r0:r1], kk, NT_DIMS) - ck[j:j + 1, c0:c1]
                    if masked:
                        sc = jnp.where(_span_mask(r0, r1, c0, c1), sc, -jnp.inf)
                    dp = _dot(dos[j][r0:r1], vv, NT_DIMS)
                    lb, db_ = lse_b[j][r0:r1], dl_b[j][r0:r1]
                    dsb = [jnp.exp(x - lb) * (d - db_) for x, d in zip(_lane_blocks(sc), _lane_blocks(dp))]
                    row_acc[j, r0:r1] += _sum_list(dsb)
                    dqs.append(_dot(jnp.concatenate(dsb, axis=1).astype(BF16), kk))
                acc_ref[r0:r1] += jnp.where(first[r0:r1], dqs[0], dqs[1])

        def step(kb, carry):
            tile(kb, _full_spans(t))
            return carry

        lax.fori_loop(ks_ref[hp, qb], qb, step, 0)
        tile(qb, _diagonal_spans(t))
        dq_ref[...] = (acc_ref[...] * scale).astype(dq_ref.dtype)
        drow_ref[:, 0:1] = jnp.sum(row_acc[0], axis=1, keepdims=True)
        drow_ref[:, 1:2] = jnp.sum(row_acc[1], axis=1, keepdims=True)

    return _pcall(
        body, name=name, grid=(hp_blocks, nq), prefetch=1,
        in_specs=[pl.BlockSpec((t, LANES), lambda h, i, ks: (i, h)),
                  pl.BlockSpec((t, LANES), lambda h, i, ks: (i, h)),
                  pl.BlockSpec((s, LANES), lambda h, i, ks: (0, hp_blocks + h)),
                  pl.BlockSpec((s, LANES), lambda h, i, ks: (0, 2 * hp_blocks + h)),
                  pl.BlockSpec((None, nq, 2, t), lambda h, i, ks: (h, 0, 0, 0)),
                  pl.BlockSpec((None, t, 2), lambda h, i, ks: (h, i, 0)),
                  pl.BlockSpec((None, t, 2), lambda h, i, ks: (h, i, 0))],
        out_specs=[pl.BlockSpec((t, LANES), lambda h, i, ks: (i, h)),
                   pl.BlockSpec((None, t, 2), lambda h, i, ks: (h, i, 0))],
        out_shape=[_sds((s, hp_blocks * LANES), BF16), _sds((hp_blocks, s, 2), F32)],
        scratch=[pltpu.VMEM((t, LANES), F32), pltpu.VMEM((2, t, LANES), F32)],
        sem=("parallel", "arbitrary"))(kb_start, qkv, do16, qkv, qkv, ck_rows, lse_c, dl_c)


def _flash_dkv(qkv, do16, cum, lse_rows, dl_rows, qb_end, name):
    s = qkv.shape[0]
    t = min(ATT_TILE, s)
    nq = s // t
    scale = HEAD_DIM ** -0.5
    hp_blocks = FOX_HEADS // 2

    def body(qe_ref, k_ref, v_ref, cum_ref, q_ref, do_ref, lse_ref, dl_ref, dk_ref, dv_ref, dck_ref,
             dk_acc, dv_acc, dck_acc):
        hp, kb = pl.program_id(0), pl.program_id(1)
        k2, v2 = k_ref[...] * scale, v_ref[...]
        first = _head_pair_masks(t)
        zero = jnp.zeros_like(k2)
        ks = (jnp.where(first, k2, zero), jnp.where(first, zero, k2))
        vs = (jnp.where(first, v2, zero), jnp.where(first, zero, v2))
        cumv = cum_ref[...]
        lane = lax.broadcasted_iota(jnp.int32, (t, LANES), 1)
        ck_b = [jnp.broadcast_to(jnp.sum(jnp.where(lane == 2 * hp + j, cumv, 0.0), axis=1, keepdims=True), (t, LANES))
                for j in range(2)]
        dk_acc[...] = jnp.zeros_like(dk_acc)
        dv_acc[...] = jnp.zeros_like(dv_acc)
        dck_acc[...] = jnp.zeros_like(dck_acc)

        def tile(qb, spans):
            off = pl.multiple_of(qb * t, t)
            q2 = q_ref[pl.ds(off, t), :]
            do2 = do_ref[pl.ds(off, t), :]
            lse, dl = lse_ref[qb], dl_ref[qb]
            for r0, r1, c0, c1, masked in spans:
                qq, dd, fr = q2[c0:c1], do2[c0:c1], first[r0:r1]
                dvs, dks = [], []
                for j in range(2):
                    sc = _dot(ks[j][r0:r1], qq, NT_DIMS)
                    if masked:
                        sc = jnp.where(_span_mask(r0, r1, c0, c1, transposed=True), sc, -jnp.inf)
                    dp = _dot(vs[j][r0:r1], dd, NT_DIMS) - dl[j:j + 1, c0:c1]
                    cb_ = ck_b[j][r0:r1]
                    pb = [jnp.exp((x - cb_) - l) for x, l in zip(_lane_blocks(sc), _lane_blocks(lse[j:j + 1, c0:c1]))]
                    dsb = [p * d for p, d in zip(pb, _lane_blocks(dp))]
                    dck_acc[j, r0:r1] += _sum_list(dsb)
                    dvs.append(_dot(jnp.concatenate(pb, axis=1).astype(BF16), dd))
                    dks.append(_dot(jnp.concatenate(dsb, axis=1).astype(BF16), qq))
                dv_acc[r0:r1] += jnp.where(fr, dvs[0], dvs[1])
                dk_acc[r0:r1] += jnp.where(fr, dks[0], dks[1])

        def step(qb, carry):
            tile(qb, _full_spans(t))
            return carry

        tile(kb, _diagonal_spans(t, transposed=True))
        lax.fori_loop(kb + 1, qe_ref[hp, kb] + 1, step, 0)
        dk_ref[...] = (dk_acc[...] * scale).astype(dk_ref.dtype)
        dv_ref[...] = dv_acc[...].astype(dv_ref.dtype)
        dck_ref[:, 0:1] = -jnp.sum(dck_acc[0], axis=1, keepdims=True)
        dck_ref[:, 1:2] = -jnp.sum(dck_acc[1], axis=1, keepdims=True)

    return _pcall(
        body, name=name, grid=(hp_blocks, nq), prefetch=1,
        in_specs=[pl.BlockSpec((t, LANES), lambda h, j, qe: (j, hp_blocks + h)),
                  pl.BlockSpec((t, LANES), lambda h, j, qe: (j, 2 * hp_blocks + h)),
                  pl.BlockSpec((t, LANES), lambda h, j, qe: (j, 0)),
                  pl.BlockSpec((s, LANES), lambda h, j, qe: (0, h)),
                  pl.BlockSpec((s, LANES), lambda h, j, qe: (0, h)),
                  pl.BlockSpec((None, nq, 2, t), lambda h, j, qe: (h, 0, 0, 0)),
                  pl.BlockSpec((None, nq, 2, t), lambda h, j, qe: (h, 0, 0, 0))],
        out_specs=[pl.BlockSpec((t, LANES), lambda h, j, qe: (j, h)),
                   pl.BlockSpec((t, LANES), lambda h, j, qe: (j, h)),
                   pl.BlockSpec((None, t, 2), lambda h, j, qe: (h, j, 0))],
        out_shape=[_sds((s, hp_blocks * LANES), BF16), _sds((s, hp_blocks * LANES), BF16),
                   _sds((hp_blocks, s, 2), F32)],
        scratch=[pltpu.VMEM((t, LANES), F32), pltpu.VMEM((t, LANES), F32), pltpu.VMEM((2, t, LANES), F32)],
        sem=("parallel", "arbitrary"))(qb_end, qkv, qkv, cum, qkv, do16, lse_rows, dl_rows)


SKIP_NATS = 110.0


def _qk_norms(qkv, ind16, name):
    d = FOX_HEADS * HEAD_DIM

    def fn(tile, ind):
        q = tile[:, :d].astype(F32)
        k = tile[:, d:2 * d].astype(F32)
        return _dot_split(q * q, ind), _dot_split(k * k, ind)
    return _rowwise(fn, [qkv], [ind16], [(LANES, F32), (LANES, F32)], [], name=name)


def _skip_bounds(qn, kn, cum, t):
    s = qn.shape[0]
    nq = s // t
    hp = FOX_HEADS // 2
    scale = HEAD_DIM ** -0.5
    qmax = jnp.sqrt(jnp.max(qn.reshape(nq, t, FOX_HEADS), axis=1))
    kmax = jnp.sqrt(jnp.max(kn, axis=0))
    bound = qmax * kmax[None, :] * (scale * 1.01) + 1e-3
    gap = cum[0::t][:, None, :] - cum[t - 1::t][None, :, :]
    idx = jnp.arange(nq, dtype=jnp.int32)
    needed = (gap + 2.0 * bound[:, None, :]) > -SKIP_NATS
    needed = needed.reshape(nq, nq, hp, 2).any(axis=-1) & (idx[None, :] <= idx[:, None])[:, :, None]
    first = jnp.min(jnp.where(needed, idx[None, :, None], nq), axis=1)
    first = jnp.minimum(first, idx[:, None])
    start = lax.cummin(first, axis=0, reverse=True)
    uses = start[:, None, :] <= idx[None, :, None]
    last = jnp.max(jnp.where(uses, idx[:, None, None], 0), axis=0)
    last = jnp.maximum(last, idx[:, None])
    return start.T.astype(jnp.int32), last.T.astype(jnp.int32)


def _head_rowsum(prod, ind16, name):
    def fn(a, b, ind):
        return (_dot_split(a * b, ind),)
    return _rowwise(fn, list(prod), [ind16], [(LANES, F32)], [], name=name)[0]


def _pairs_cols(x16):
    s = x16.shape[0]
    return x16.reshape(s, FOX_HEADS // 2, 2).transpose(1, 0, 2)


def _pairs_rows(x16, t):
    s = x16.shape[0]
    return x16.reshape(s // t, t, FOX_HEADS // 2, 2).transpose(2, 0, 3, 1)


def _fox_forward(u, w):
    s = u.shape[0]
    t = min(ATT_TILE, s)
    qkv = _mm(u, w["fox_qkv"], "nn", [BF16], name="fox_qkv")
    fraw = _mm(u, w["fox_f"], "nn", [F32], name="fox_fproj")
    cum = _fox_gate(fraw, w["fox_bf"], "fox_gate")
    ck_rows = _pairs_rows(cum[:, :FOX_HEADS], t)
    qn, kn = _qk_norms(qkv, w["head_ind"], "fox_qk_norms")
    kb_start, qb_end = _skip_bounds(qn[:, :FOX_HEADS], kn[:, :FOX_HEADS], cum[:, :FOX_HEADS], t)
    o, lse = _flash_fwd(qkv, ck_rows, kb_start, "fox_flash_fwd")
    y = _mm(o, w["fox_o"], "nn", [F32], name="fox_oproj")
    return y, dict(u=u, qkv=qkv, fraw=fraw, cum=cum, ck_rows=ck_rows, o=o, lse=lse, kb_start=kb_start,
                   qb_end=qb_end)


def _fox_backward(dy, sv, w):
    s = dy.shape[0]
    t = min(ATT_TILE, s)
    do32, do16 = _mm(dy, w["fox_o"], "nt", [F32, BF16], name="fox_do")
    g_wo = _mm(sv["o"], dy, "tn", [F32], name="fox_gwo")
    delta = _head_rowsum((do32, sv["o"]), w["head_ind"], "fox_delta")[:, :FOX_HEADS]
    lse16 = sv["lse"].transpose(1, 0, 2).reshape(s, FOX_HEADS)
    dq, drow = _flash_dq(sv["qkv"], do16, sv["ck_rows"], sv["lse"], _pairs_cols(delta), sv["kb_start"],
                         "fox_flash_dq")
    dk, dv, dck = _flash_dkv(sv["qkv"], do16, sv["cum"], _pairs_rows(lse16, t), _pairs_rows(delta, t),
                             sv["qb_end"], "fox_flash_dkv")
    pad = ((0, 0), (0, LANES - FOX_HEADS))
    dcol = jnp.pad(dck.transpose(1, 0, 2).reshape(s, FOX_HEADS), pad)
    drow = jnp.pad(drow.transpose(1, 0, 2).reshape(s, FOX_HEADS), pad)
    df, db_f = _fox_gate_bwd(drow, dcol, sv["fraw"], w["fox_bf"], "fox_gate_bwd")
    du = _mm(df, w["fox_f"], "nt", [F32], name="fox_du_f")
    du = _mm_nt_blocks(_k_blocks(dq) + _k_blocks(dk) + _k_blocks(dv), w["fox_qkv"], du, name="fox_du")
    g_win = jnp.concatenate([_mm(sv["u"], dq, "tn", [F32], name="fox_gwin_q"),
                             _mm(sv["u"], dk, "tn", [F32], name="fox_gwin_k"),
                             _mm(sv["u"], dv, "tn", [F32], name="fox_gwin_v"),
                             _mm(sv["u"], df, "tn", [F32], name="fox_gwin_f")[:, :FOX_HEADS]], axis=1)
    return du, dict(fox_w_in=g_win, fox_w_o=g_wo, fox_b_f=db_f[:, :FOX_HEADS])


def _conv_fwd(xpre, w8, b, name):
    s, c = xpre.shape
    tm, tc = min(ROW_TILE, s), min(1024, c)
    hb = tm // 8

    def body(x_ref, h_ref, w_ref, b_ref, xc_ref, xa_ref):
        i = pl.program_id(1)
        x = x_ref[...]
        halo = jnp.where(i > 0, h_ref[...], 0.0)
        w = w_ref[...]
        row = lax.broadcasted_iota(jnp.int32, (tm, tc), 0)
        row8 = lax.broadcasted_iota(jnp.int32, (8, tc), 0)
        acc = x * w[3:4] + b_ref[...]
        x8 = x[0:8]
        acc8 = x8 * w[3:4] + b_ref[...]
        for j in range(1, SSM_CONV):
            acc = acc + w[3 - j:4 - j] * pltpu.roll(x, j, 0)
            acc8 = acc8 + w[3 - j:4 - j] * jnp.where(row8 < j, pltpu.roll(halo, j, 0), pltpu.roll(x8, j, 0))
        xc_ref[...] = acc
        xc_ref[0:8, :] = acc8
        xc = xc_ref[...]
        xa_ref[...] = _silu(xc)

    tile = pl.BlockSpec((tm, tc), lambda jc, i: (i, jc))
    return _pcall(body, name=name, grid=(c // tc, s // tm),
                  in_specs=[tile, pl.BlockSpec((8, tc), lambda jc, i: (jnp.maximum(i * hb - 1, 0), jc)),
                            pl.BlockSpec((8, tc), lambda jc, i: (0, jc)), pl.BlockSpec((1, tc), lambda jc, i: (0, jc))],
                  out_specs=[tile, tile], out_shape=[_sds((s, c), F32), _sds((s, c), F32)],
                  sem=("parallel", "arbitrary"))(xpre, xpre, w8, b)


def _conv_bwd(dxa, xc, xpre, w8, name):
    s, c = xpre.shape
    tm, tc = min(ROW_TILE, s), min(1024, c)
    hb = tm // 8
    n = s // tm

    def body(d_ref, xc_ref, x_ref, xh_ref, dn_ref, xcn_ref, w_ref, dx_ref, dw_ref, db_ref, scr):
        i = pl.program_id(1)

        @pl.when(i == 0)
        def _():
            dw_ref[...] = jnp.zeros_like(dw_ref)
            db_ref[...] = jnp.zeros_like(db_ref)
        w = w_ref[...]
        x = x_ref[...]
        g = d_ref[...] * _dsilu(xc_ref[...])
        gn = jnp.where(i < n - 1, dn_ref[...] * _dsilu(xcn_ref[...]), 0.0)
        halo = jnp.where(i > 0, xh_ref[...], 0.0)
        row = lax.broadcasted_iota(jnp.int32, (tm, tc), 0)
        row8 = lax.broadcasted_iota(jnp.int32, (8, tc), 0)
        db_ref[...] += _colsum(g)
        dw_ref[3:4, :] += _colsum(g * x)
        g8 = g[0:8]
        acc = g * w[3:4]
        corr = jnp.zeros((8, tc), F32)
        for j in range(1, SSM_CONV):
            xs = pltpu.roll(x, j, 0)
            dwj = _colsum(jnp.where(row >= j, g * xs, 0.0))
            dwj = dwj + _colsum(jnp.where(row8 < j, g8 * pltpu.roll(halo, j, 0), 0.0))
            dw_ref[3 - j:4 - j, :] += dwj
            gs = pltpu.roll(g, tm - j, 0)
            acc = acc + w[3 - j:4 - j] * jnp.where(row < tm - j, gs, 0.0)
            corr = corr + w[3 - j:4 - j] * jnp.where(row8 >= 8 - j, pltpu.roll(gn, 8 - j, 0), 0.0)
        scr[...] = acc
        scr[tm - 8:tm, :] += corr
        dx_ref[...] = scr[...].astype(dx_ref.dtype)

    tile = pl.BlockSpec((tm, tc), lambda jc, i: (i, jc))
    prev8 = pl.BlockSpec((8, tc), lambda jc, i: (jnp.maximum(i * hb - 1, 0), jc))
    next8 = pl.BlockSpec((8, tc), lambda jc, i: (jnp.minimum((i + 1) * hb, n * hb - 1), jc))
    return _pcall(body, name=name, grid=(c // tc, n),
                  in_specs=[tile, tile, tile, prev8, next8, next8, pl.BlockSpec((8, tc), lambda jc, i: (0, jc))],
                  out_specs=[tile, pl.BlockSpec((8, tc), lambda jc, i: (0, jc)), pl.BlockSpec((1, tc), lambda jc, i: (0, jc))],
                  out_shape=[_sds((s, c), BF16), _sds((8, c), F32), _sds((1, c), F32)],
                  scratch=[pltpu.VMEM((tm, tc), F32)],
                  sem=("parallel", "arbitrary"))(dxa, xc, xpre, xpre, dxa, xc, w8)


def _ssd_pre(dtraw, dt_bias, a_log, cst, name):
    def fn(raw, bias, alog, expand):
        tm = raw.shape[0]
        z = raw + bias
        dt = jnp.maximum(z, 0.0) + jnp.log(1.0 + jnp.exp(-jnp.abs(z)))
        lane = lax.broadcasted_iota(jnp.int32, (tm, LANES), 1)
        pos = lax.broadcasted_iota(jnp.int32, (tm, LANES), 0) & (SSM_CHUNK - 1)
        dt = jnp.where(lane < SSM_HEADS, dt, 0.0)
        c = dt * (-jnp.exp(alog))
        sh = 1
        while sh < SSM_CHUNK:
            c = c + jnp.where(pos >= sh, pltpu.roll(c, sh, 0), 0.0)
            sh *= 2
        return dt, c, _dot_split(dt, expand), _dot_split(c, expand)
    wide = SSM_HEADS * HEAD_DIM
    return _rowwise(fn, [dtraw], [dt_bias, a_log, cst["expand"]],
                    [(LANES, F32), (LANES, F32), (wide, F32), (wide, F32)], [], name=name)


def _ssd_post(dacs, ddt, dtraw, dt, dt_bias, a_log, name):
    def fn(dacs_v, ddt_v, raw, dt_v, bias, alog):
        tm = raw.shape[0]
        pos = lax.broadcasted_iota(jnp.int32, (tm, LANES), 0) & (SSM_CHUNK - 1)
        a = -jnp.exp(alog)
        c = dacs_v
        sh = 1
        while sh < SSM_CHUNK:
            c = c + jnp.where(pos + sh < SSM_CHUNK, pltpu.roll(c, tm - sh, 0), 0.0)
            sh *= 2
        draw = (ddt_v + c * a) * _sigmoid(raw + bias)
        return draw, _colsum(draw), _colsum(c * dt_v * a)
    return _rowwise(fn, [dacs, ddt, dtraw, dt], [dt_bias, a_log], [(LANES, BF16)], [(1, LANES)] * 2, name=name)


def _heads_rows(x, s):
    return x[:, :SSM_HEADS].reshape(s // SSM_CHUNK, SSM_CHUNK, SSM_GROUPS, 4).transpose(2, 0, 3, 1)


def _ssd_constants():
    src = jnp.arange(LANES)[:, None]
    expand = (src == jnp.arange(SSM_HEADS * HEAD_DIM)[None, :] // HEAD_DIM).astype(BF16)
    seg = (jnp.arange(SSM_GROUP_WIDTH)[:, None] // HEAD_DIM == jnp.arange(LANES)[None, :]).astype(BF16)
    seg4 = (jnp.arange(4 * LANES)[:, None] // LANES == jnp.arange(LANES)[None, :]).astype(BF16)
    return dict(expand=expand, seg=seg, seg4=seg4)


def _ssm_weights(w_in, conv_w, conv_b, dt_bias, a_log, d_skip, norm_w, w_out):
    pad = ((0, 0), (0, LANES - SSM_HEADS))
    w_xbc = _group_cols(w_in[:, 2048:6144])
    w_dt = jnp.pad(w_in[:, 6144:], pad)
    return dict(
        ssm_z=w_in[:, :2048], ssm_xbc=w_xbc, ssm_dt=w_dt,
        ssm_zx=jnp.concatenate([w_in[:, :2048], w_xbc], axis=1),
        ssm_out=w_out,
        conv_w8=_group_cols(jnp.pad(conv_w, ((0, 8 - SSM_CONV), (0, 0)))),
        conv_b=_group_cols(conv_b), norm_w=norm_w,
        dt_bias=jnp.pad(dt_bias, pad), a_log=jnp.pad(a_log, pad),
        d_e=jnp.repeat(d_skip.reshape(SSM_GROUPS, 4), HEAD_DIM, axis=1)[:, None, :],
        ssd_cst=_ssd_constants())


def _ssd_setup(acs_e, acsr):
    l = SSM_CHUNK
    last = acsr[:, l - 1:l]
    lane1 = lax.broadcasted_iota(jnp.int32, (1, SSM_GROUP_WIDTH), 1)
    last_e = last[3:4, :]
    for r in (2, 1, 0):
        last_e = jnp.where(lane1 < HEAD_DIM * (r + 1), last[r:r + 1, :], last_e)
    return jnp.exp(acs_e), jnp.exp(last_e - acs_e), jnp.exp(last_e)


def _head_bcast(acs_e):
    lo = lax.broadcasted_iota(jnp.int32, (acs_e.shape[0], LANES), 1) < HEAD_DIM
    out = []
    for p in range(2):
        blk = acs_e[:, p * LANES:(p + 1) * LANES]
        rolled = pltpu.roll(blk, HEAD_DIM, 1)
        out += [jnp.where(lo, blk, rolled), jnp.where(lo, rolled, blk)]
    return out


def _group_cols(a):
    lead = a.shape[:-1]
    x = a[..., :2048].reshape(lead + (SSM_GROUPS, SSM_GROUP_WIDTH))
    b = a[..., 2048:3072].reshape(lead + (SSM_GROUPS, SSM_STATE))
    c = a[..., 3072:].reshape(lead + (SSM_GROUPS, SSM_STATE))
    return jnp.concatenate([x, b, c], axis=-1).reshape(lead + (4096,))


def _ungroup_cols(a):
    lead = a.shape[:-1]
    y = a.reshape(lead + (SSM_GROUPS, SSM_GROUP_WIDTH + 2 * SSM_STATE))
    return jnp.concatenate([y[..., :256].reshape(lead + (2048,)), y[..., 256:384].reshape(lead + (1024,)),
                            y[..., 384:].reshape(lead + (1024,))], axis=-1)


def _ssd_fwd2(xa, dte, acse, acsr, d_e, name):
    s = xa.shape[0]
    l, gw, ns = SSM_CHUNK, SSM_GROUP_WIDTH, SSM_STATE
    nc = s // l

    gb = gw + 2 * ns
    gp = SSD_GROUPS_PER_STEP

    def body(xa_ref, dt_ref, acs_ref, acsr_ref, d_ref, y_ref, hp_ref, h_sc):
        @pl.when(pl.program_id(1) == 0)
        def _():
            h_sc[...] = jnp.zeros_like(h_sc)
        lane = lax.broadcasted_iota(jnp.int32, (l, gw), 1)
        tril = _causal(l)
        for gi in range(gp):
            x = xa_ref[:, gi * gb:gi * gb + gw]
            bm = xa_ref[:, gi * gb + gw:gi * gb + gw + ns].astype(BF16)
            cm = xa_ref[:, gi * gb + gw + ns:(gi + 1) * gb].astype(BF16)
            acsr = acsr_ref[gi]
            dt_e, acs_e = dt_ref[:, gi * gw:(gi + 1) * gw], acs_ref[:, gi * gw:(gi + 1) * gw]
            acs_bc = _head_bcast(acs_e)
            e_e, dte_e, cd_e = _ssd_setup(acs_e, acsr)
            xdt = x * dt_e
            xdt16 = xdt.astype(BF16)
            cb = _dot(cm, bm, NT_DIMS)
            yd = jnp.zeros((l, gw), F32)
            for r in range(4):
                lm = jnp.exp(jnp.where(tril, acs_bc[r] - acsr[r:r + 1, :], -jnp.inf))
                yr = _dot((cb * lm).astype(BF16), xdt16)
                yd = jnp.where((lane >= HEAD_DIM * r) & (lane < HEAD_DIM * (r + 1)), yr, yd)
            hp = h_sc[gi]
            hp_ref[gi] = hp
            y_ref[:, gi * gw:(gi + 1) * gw] = yd + _dot(cm, hp.astype(BF16)) * e_e + x * d_ref[gi]
            h_sc[gi] = hp * cd_e + _dot(bm, (xdt * dte_e).astype(BF16), TN_DIMS)

    return _pcall(
        body, name=name, grid=(SSM_GROUPS // gp, nc),
        in_specs=[pl.BlockSpec((l, gp * gb), lambda g, c: (c, g)),
                  pl.BlockSpec((l, gp * gw), lambda g, c: (c, g)),
                  pl.BlockSpec((l, gp * gw), lambda g, c: (c, g)),
                  pl.BlockSpec((gp, None, 4, l), lambda g, c: (g, c, 0, 0)),
                  pl.BlockSpec((gp, 1, gw), lambda g, c: (g, 0, 0))],
        out_specs=[pl.BlockSpec((l, gp * gw), lambda g, c: (c, g)),
                   pl.BlockSpec((gp, None, ns, gw), lambda g, c: (g, c, 0, 0))],
        out_shape=[_sds((s, 2048), F32), _sds((SSM_GROUPS, nc, ns, gw), F32)],
        scratch=[pltpu.VMEM((gp, ns, gw), F32)],
        sem=("parallel", "arbitrary"))(xa, dte, acse, acsr, d_e)


def _ssd_bwd2(dy, xa, dte, acse, acsr, d_e, hprev, cst, name):
    s = xa.shape[0]
    l, gw, ns = SSM_CHUNK, SSM_GROUP_WIDTH, SSM_STATE
    nc = s // l

    gb = gw + 2 * ns
    gp = SSD_GROUPS_PER_STEP

    def body(dy_ref, xa_ref, dt_ref, acs_ref, acsr_ref, d_ref, hp_ref,
             seg_ref, seg4_ref, dxa_ref, dacs_ref, ddt_ref, dd_ref, dh_sc):
        @pl.when(pl.program_id(1) == 0)
        def _():
            dh_sc[...] = jnp.zeros_like(dh_sc)
            dd_ref[...] = jnp.zeros_like(dd_ref)
        for gi in range(gp):
            one_group(gi, dy_ref, xa_ref, dt_ref, acs_ref, acsr_ref, d_ref, hp_ref, seg_ref, seg4_ref,
                      dxa_ref, dacs_ref, ddt_ref, dd_ref, dh_sc)

    def one_group(gi, dy_ref, xa_ref, dt_ref, acs_ref, acsr_ref, d_ref, hp_ref, seg_ref, seg4_ref,
                  dxa_ref, dacs_ref, ddt_ref, dd_ref, dh_sc):
        dyv = dy_ref[:, gi * gw:(gi + 1) * gw]
        x = xa_ref[:, gi * gb:gi * gb + gw]
        bm = xa_ref[:, gi * gb + gw:gi * gb + gw + ns].astype(BF16)
        cm = xa_ref[:, gi * gb + gw + ns:(gi + 1) * gb].astype(BF16)
        acsr = acsr_ref[gi]
        dt_e, acs_e = dt_ref[:, gi * gw:(gi + 1) * gw], acs_ref[:, gi * gw:(gi + 1) * gw]
        acs_bc = _head_bcast(acs_e)
        e_e, dte_e, cd_e = _ssd_setup(acs_e, acsr)
        seg, seg4 = seg_ref[...], seg4_ref[...]
        lane = lax.broadcasted_iota(jnp.int32, (l, gw), 1)
        xdt = x * dt_e
        xdt16 = xdt.astype(BF16)
        dy16 = dyv.astype(BF16)
        cb = _dot(cm, bm, NT_DIMS)
        cbt = _dot(bm, cm, NT_DIMS)
        hp = hp_ref[gi]
        hp16 = hp.astype(BF16)
        g = dh_sc[gi]
        g16 = g.astype(BF16)
        t_all = _dot(cm, hp16)
        dt16 = (dyv * e_e).astype(BF16)
        dc = _dot(dt16, hp16, NT_DIMS)
        dhp = _dot(cm, dt16, TN_DIMS)
        wv = xdt * dte_e
        dw = _dot(bm, g16)
        db = _dot(wv.astype(BF16), g16, NT_DIMS)
        dxdt = dw * dte_e
        acs_term = dyv * t_all * e_e - dw * wv
        last_term = _colsum(dw * wv) + _colsum(g * hp) * cd_e
        dh_sc[gi] = g * cd_e + dhp
        tril, triu = _causal(l), _causal(l, transposed=True)
        dcb = jnp.zeros((l, l), F32)
        dcbt = jnp.zeros((l, l), F32)
        qd = []
        for r in range(4):
            in_head = (lane >= HEAD_DIM * r) & (lane < HEAD_DIM * (r + 1))
            a_col = acs_bc[r]
            lm = jnp.exp(jnp.where(tril, a_col - acsr[r:r + 1, :], -jnp.inf))
            lmt = jnp.exp(jnp.where(triu, acsr[r:r + 1, :] - a_col, -jnp.inf))
            mm_, mt = cb * lm, cbt * lmt
            dyr = jnp.where(in_head, dy16, jnp.zeros_like(dy16))
            dm = _dot(dyr, xdt16, NT_DIMS)
            dmt = _dot(xdt16, dyr, NT_DIMS)
            dxdt = dxdt + jnp.where(in_head, _dot(mt.astype(BF16), dy16), 0.0)
            dcb = dcb + dm * lm
            dcbt = dcbt + dmt * lmt
            qd.append(dm * mm_ - dmt * mt)
        dc = dc + _dot(dcb.astype(BF16), bm)
        db = db + _dot(dcbt.astype(BF16), cm)
        rowl = lax.broadcasted_iota(jnp.int32, (l, LANES), 0)
        row8 = lax.broadcasted_iota(jnp.int32, (8, gw), 0)
        small = _dot_split(jnp.where(row8 == 0, last_term, jnp.where(row8 == 1, _colsum(dyv * x), 0.0)), seg, passes=2)
        big = _dot_split(jnp.concatenate([acs_term, dxdt * x], axis=0), seg, passes=2)
        dacs = (big[0:l] + _dot_split(jnp.concatenate(qd, axis=1), seg4, passes=2)
                + jnp.where(rowl == l - 1, small[0:1, :], 0.0))
        dacs_ref[gi] = dacs[:, 0:4]
        ddt_ref[gi] = big[l:2 * l, 0:4]
        dd_ref[gi, 0:1, :] += small[1:2, :]
        dxa_ref[:, gi * gb:(gi + 1) * gb] = jnp.concatenate([dxdt * dt_e + dyv * d_ref[gi], db, dc], axis=1)

    rc = lambda c: nc - 1 - c
    return _pcall(
        body, name=name, grid=(SSM_GROUPS // gp, nc),
        in_specs=[pl.BlockSpec((l, gp * gw), lambda g, c: (rc(c), g)),
                  pl.BlockSpec((l, gp * gb), lambda g, c: (rc(c), g)),
                  pl.BlockSpec((l, gp * gw), lambda g, c: (rc(c), g)),
                  pl.BlockSpec((l, gp * gw), lambda g, c: (rc(c), g)),
                  pl.BlockSpec((gp, None, 4, l), lambda g, c: (g, rc(c), 0, 0)),
                  pl.BlockSpec((gp, 1, gw), lambda g, c: (g, 0, 0)),
                  pl.BlockSpec((gp, None, ns, gw), lambda g, c: (g, rc(c), 0, 0)),
                  pl.BlockSpec((gw, LANES), lambda g, c: (0, 0)),
                  pl.BlockSpec((4 * LANES, LANES), lambda g, c: (0, 0))],
        out_specs=[pl.BlockSpec((l, gp * gb), lambda g, c: (rc(c), g)),
                   pl.BlockSpec((gp, l, 4), lambda g, c: (g, rc(c), 0)),
                   pl.BlockSpec((gp, l, 4), lambda g, c: (g, rc(c), 0)),
                   pl.BlockSpec((gp, 8, LANES), lambda g, c: (g, 0, 0))],
        out_shape=[_sds((s, 4096), F32), _sds((SSM_GROUPS, s, 4), F32), _sds((SSM_GROUPS, s, 4), F32),
                   _sds((SSM_GROUPS, 8, LANES), F32)],
        scratch=[pltpu.VMEM((gp, ns, gw), F32)],
        sem=("parallel", "arbitrary"))(dy, xa, dte, acse, acsr, d_e, hprev, cst["seg"], cst["seg4"])


def _gate_norm(y, z, nw, name):
    c = y.shape[1]

    def fn(yv, zv, w):
        outs = []
        for k in range(c // SSM_GROUP_WIDTH):
            sl = slice(k * SSM_GROUP_WIDTH, (k + 1) * SSM_GROUP_WIDTH)
            yg = yv[:, sl] * _silu(zv[:, sl])
            rinv = lax.rsqrt(jnp.mean(yg * yg, axis=-1, keepdims=True) + RMS_EPS)
            outs.append(yg * rinv * w[:, sl])
        return (jnp.concatenate(outs, axis=1),)
    return _rowwise(fn, [y, z], [nw], [(c, BF16)], [], name=name)[0]


def _out_gate_norm_bwd(dy, w_out, y, z, nw, name):
    s, c = y.shape
    tm, tn = min(512, s), min(1024, c)
    k = dy.shape[1]

    def body(a_ref, b_ref, y_ref, z_ref, w_ref, dy_ref, dz_ref, dw_ref):
        @pl.when(pl.program_id(1) == 0)
        def _():
            dw_ref[...] = jnp.zeros_like(dw_ref)
        dv = _dot(a_ref[...], b_ref[...], NT_DIMS)
        yv, zv, w = y_ref[...], z_ref[...], w_ref[...]
        dys, dzs, dws = [], [], []
        for g in range(tn // SSM_GROUP_WIDTH):
            sl = slice(g * SSM_GROUP_WIDTH, (g + 1) * SSM_GROUP_WIDTH)
            ys, zs, ds = yv[:, sl], zv[:, sl], dv[:, sl]
            sz = _silu(zs)
            yg = ys * sz
            rinv = lax.rsqrt(jnp.mean(yg * yg, axis=-1, keepdims=True) + RMS_EPS)
            nrm = yg * rinv
            dn = ds * w[:, sl]
            dyg = rinv * (dn - nrm * jnp.mean(dn * nrm, axis=-1, keepdims=True))
            dys.append(dyg * sz)
            dzs.append(dyg * ys * _dsilu(zs))
            dws.append(_colsum(ds * nrm))
        dy_ref[...] = jnp.concatenate(dys, axis=1)
        dz_ref[...] = jnp.concatenate(dzs, axis=1).astype(dz_ref.dtype)
        dw_ref[...] += jnp.concatenate(dws, axis=1)

    tile = pl.BlockSpec((tm, tn), lambda j, i: (i, j))
    row = pl.BlockSpec((1, tn), lambda j, i: (0, j))
    return _pcall(body, name=name, grid=(c // tn, s // tm),
                  in_specs=[pl.BlockSpec((tm, k), lambda j, i: (i, 0)), pl.BlockSpec((tn, k), lambda j, i: (j, 0)),
                            tile, tile, row],
                  out_specs=[tile, tile, row], out_shape=[_sds((s, c), F32), _sds((s, c), BF16), _sds((1, c), F32)],
                  sem=("parallel", "arbitrary"))(dy, w_out, y, z, nw)


def _ssd_forward(u, w):
    s = u.shape[0]
    z = _mm(u, w["ssm_z"], "nn", [F32], name="ssm_zproj")
    xpre = _mm(u, w["ssm_xbc"], "nn", [F32], name="ssm_xproj")
    dtraw = _mm(u, w["ssm_dt"], "nn", [F32], name="ssm_dtproj")
    xc, xa = _conv_fwd(xpre, w["conv_w8"], w["conv_b"], "ssm_conv")
    dt, acs, dte, acse = _ssd_pre(dtraw, w["dt_bias"], w["a_log"], w["ssd_cst"], "ssm_pre")
    acsr = _heads_rows(acs, s)
    y, hprev = _ssd_fwd2(xa, dte, acse, acsr, w["d_e"], "ssm_scan")
    yn = _gate_norm(y, z, w["norm_w"], "ssm_gate_norm")
    out = _mm(yn, w["ssm_out"], "nn", [F32], name="ssm_oproj")
    return out, dict(u=u, z=z, xpre=xpre, xc=xc, xa=xa, dtraw=dtraw, dt=dt, dte=dte, acse=acse,
                     acsr=acsr, y=y, hprev=hprev, yn=yn)


def _ssd_backward(dy, sv, w):
    s = dy.shape[0]
    g_wout = _mm(sv["yn"], dy, "tn", [F32], name="ssm_gwout")
    dys, dz, dnw = _out_gate_norm_bwd(dy, w["ssm_out"], sv["y"], sv["z"], w["norm_w"], "ssm_dyn_gate_norm_bwd")
    dxa, dacs_c, ddt_c, dd = _ssd_bwd2(dys, sv["xa"], sv["dte"], sv["acse"], sv["acsr"], w["d_e"], sv["hprev"],
                                       w["ssd_cst"], "ssm_scan_bwd")
    pad = ((0, 0), (0, LANES - SSM_HEADS))
    dacs = jnp.pad(dacs_c.transpose(1, 0, 2).reshape(s, SSM_HEADS), pad)
    ddt = jnp.pad(ddt_c.transpose(1, 0, 2).reshape(s, SSM_HEADS), pad)
    draw, dbias, dalog = _ssd_post(dacs, ddt, sv["dtraw"], sv["dt"], w["dt_bias"], w["a_log"], "ssm_post")
    dxpre, dcw, dcb = _conv_bwd(dxa, sv["xc"], sv["xpre"], w["conv_w8"], "ssm_conv_bwd")
    du = _mm(draw, w["ssm_dt"], "nt", [F32], name="ssm_du_dt")
    n_z = dz.shape[1]
    du = _mm_nt_blocks(_k_blocks(dz), w["ssm_zx"][:, :n_z], du, name="ssm_du_z")
    du = _mm_nt_blocks(_k_blocks(dxpre), w["ssm_zx"][:, n_z:], du, name="ssm_du_x")
    g_win = jnp.concatenate([_mm(sv["u"], dz, "tn", [F32], name="ssm_gwin_z"),
                             _ungroup_cols(_mm(sv["u"], dxpre, "tn", [F32], name="ssm_gwin_x")),
                             _mm(sv["u"], draw, "tn", [F32], name="ssm_gwin_dt")[:, :SSM_HEADS]], axis=1)
    return du, dict(ssm_w_in=g_win, ssm_w_out=g_wout, ssm_conv_w=_ungroup_cols(dcw[:SSM_CONV]),
                    ssm_conv_b=_ungroup_cols(dcb), ssm_norm_w=dnw, ssm_dt_bias=dbias[:, :SSM_HEADS],
                    ssm_a_log=dalog[:, :SSM_HEADS], ssm_d=dd[:, 0, :4].reshape(1, SSM_HEADS))


def _mlp_forward(u2, w1, w2, tag):
    def epi(acc):
        hr = jnp.maximum(acc, 0.0)
        return hr, hr * hr
    hr, a = _mm(u2, w1, "nn", [BF16, BF16], name=tag + "_mlp_up", epi=epi)
    y2 = _mm(a, w2, "nn", [F32], name=tag + "_mlp_down")
    return y2, hr, a


def _mlp_backward(dy2, u2, hr, a, w1, w2, tag):
    dh = _mm(dy2, w2, "nt", [BF16], name=tag + "_mlp_dh", extra=(hr,),
             epi=lambda acc, h: (acc * (2.0 * h.astype(F32)),))
    g_w2 = _mm(a, dy2, "tn", [F32], name=tag + "_mlp_gw2")
    g_w1 = _mm(u2, dh, "tn", [F32], name=tag + "_mlp_gw1")
    du2 = _mm(dh, w1, "nt", [F32], name=tag + "_mlp_du")
    return du2, g_w1, g_w2


def _ada_forward(c16, ada_w, ada_b_cols, name):
    nl, d, cols = ada_w.shape
    tn = 512

    def body(c_ref, w_ref, b_ref, o_ref):
        cond = _silu(c_ref[...]).astype(BF16)
        o_ref[...] = _dot(cond, w_ref[...].astype(BF16)) + b_ref[...]

    return _pcall(body, name=name, grid=(nl, cols // tn),
                  in_specs=[pl.BlockSpec((16, d), lambda i, j: (0, 0)),
                            pl.BlockSpec((None, d, tn), lambda i, j: (i, 0, j)),
                            pl.BlockSpec((None, 1, tn), lambda i, j: (i, 0, j))],
                  out_specs=pl.BlockSpec((None, 16, tn), lambda i, j: (i, 0, j)),
                  out_shape=_sds((nl, 16, cols), F32), sem=("parallel", "parallel"))(c16, ada_w, ada_b_cols)


def _ada_backward(c_t, dmod_cols, name):
    d, nb = c_t.shape
    nl, _, cols = dmod_cols.shape
    tn = 512

    def body(c_ref, dm_ref, o_ref):
        cond = _silu(c_ref[...])
        dm = dm_ref[...]
        acc = cond[:, 0:1] * dm[0:1, :]
        for b in range(1, nb):
            acc = acc + cond[:, b:b + 1] * dm[b:b + 1, :]
        o_ref[...] = acc

    return _pcall(body, name=name, grid=(nl, cols // tn),
                  in_specs=[pl.BlockSpec((d, nb), lambda i, j: (0, 0)),
                            pl.BlockSpec((None, nb, tn), lambda i, j: (i, 0, j))],
                  out_specs=pl.BlockSpec((None, d, tn), lambda i, j: (i, 0, j)),
                  out_shape=_sds((nl, d, cols), F32), sem=("parallel", "parallel"))(c_t, dmod_cols)


def _adamw(w, g, m, v, name):
    rows, cols = w.shape
    tm = rows
    for cand in (256, 128, 64, 32, 16, 8):
        if rows % cand == 0 and rows > cand:
            tm = cand
            break
    c1 = 1.0 / (1.0 - ADAM_B1 ** ADAM_STEP)
    c2 = 1.0 / (1.0 - ADAM_B2 ** ADAM_STEP)

    def fn(wv, gv, mv, vv):
        mn = ADAM_B1 * mv + (1.0 - ADAM_B1) * gv
        vn = ADAM_B2 * vv + (1.0 - ADAM_B2) * (gv * gv)
        delta = -ADAM_LR * ((mn * c1) / (jnp.sqrt(vn * c2) + ADAM_EPS) + ADAM_WD * wv)
        return delta, mn, vn
    return _rowwise(fn, [w, g, m, v], [], [(cols, F32)] * 3, [], name=name, tm=tm)


def _my_pos():
    return lax.axis_index("x"), lax.axis_index("y"), lax.axis_index("c")


def _allgather8(x, name):
    r, c = x.shape

    def body(x_ref, out_ref, send_sems, recv_sems, local_sem):
        mx, my, mc = _my_pos()
        me = 4 * mx + 2 * my + mc
        mine = pltpu.make_async_copy(x_ref, out_ref.at[me], local_sem)
        mine.start()
        copies = []
        for k in range(1, 8):
            fx, fy, fc = (k >> 2) & 1, (k >> 1) & 1, k & 1
            px = 1 - mx if fx else mx
            py = 1 - my if fy else my
            pc = 1 - mc if fc else mc
            peer = 4 * px + 2 * py + pc
            send = pltpu.make_async_remote_copy(src_ref=x_ref, dst_ref=out_ref.at[me], send_sem=send_sems.at[k - 1],
                                                recv_sem=recv_sems.at[k - 1], device_id=(px, py, pc),
                                                device_id_type=MESH)
            send.start()
            recv = pltpu.make_async_remote_copy(src_ref=x_ref, dst_ref=out_ref.at[peer], send_sem=send_sems.at[k - 1],
                                                recv_sem=recv_sems.at[k - 1], device_id=(px, py, pc),
                                                device_id_type=MESH)
            copies.append((send, recv))
        for send, recv in copies:
            recv.wait_recv()
        for send, recv in copies:
            send.wait_send()
        mine.wait()

    vm = pl.BlockSpec(memory_space=pltpu.VMEM)
    return _pcall(body, name=name, in_specs=[vm], out_specs=vm, out_shape=_sds((8, r, c), x.dtype),
                  scratch=[pltpu.SemaphoreType.DMA((7,)), pltpu.SemaphoreType.DMA((7,)), pltpu.SemaphoreType.DMA])(x)


def _chip_flips(mx, my):
    out = []
    for fx, fy in ((1, 0), (0, 1), (1, 1)):
        px = 1 - mx if fx else mx
        py = 1 - my if fy else my
        out.append((px, py, 2 * px + py))
    return out


def _gather_chips(shard2, name):
    _, h, c = shard2.shape

    def body(x_ref, out_ref, send_sems, recv_sems):
        mx, my, mc = _my_pos()
        oc = 1 - mc
        mk = 2 * mx + my
        flips = _chip_flips(mx, my)

        def copy(k, src, dst, to):
            return pltpu.make_async_remote_copy(src_ref=src, dst_ref=dst, send_sem=send_sems.at[k],
                                                recv_sem=recv_sems.at[k], device_id=to, device_id_type=MESH)

        first = [copy(j, x_ref.at[mc], out_ref.at[mk, mc], (px, py, mc)) for j, (px, py, pk) in enumerate(flips)]
        for cp in first:
            cp.start()
        passed = []
        for j, (px, py, pk) in enumerate(flips):
            copy(j, x_ref.at[mc], out_ref.at[pk, mc], (px, py, mc)).wait_recv()
            fw = copy(3 + j, out_ref.at[pk, mc], out_ref.at[pk, mc], (mx, my, oc))
            fw.start()
            passed.append(fw)
        for j, (px, py, pk) in enumerate(flips):
            copy(3 + j, out_ref.at[pk, oc], out_ref.at[pk, oc], (mx, my, oc)).wait_recv()
        for cp in first + passed:
            cp.wait_send()

    return _pcall(body, name=name, in_specs=[HBM_SPEC], out_specs=HBM_SPEC, out_shape=_sds((4, 2, h, c), shard2.dtype),
                  scratch=[pltpu.SemaphoreType.DMA((6,)), pltpu.SemaphoreType.DMA((6,))])(shard2)


def _pair_exchange(g4, name):
    n, _, h, c = g4.shape

    def body(g_ref, out_ref, send_sem, recv_sem):
        mx, my, mc = _my_pos()
        oc = 1 - mc
        copies = []
        for k in range(n):
            cp = pltpu.make_async_remote_copy(src_ref=g_ref.at[k, oc], dst_ref=out_ref.at[k], send_sem=send_sem.at[k],
                                              recv_sem=recv_sem.at[k], device_id=(mx, my, oc), device_id_type=MESH)
            cp.start()
            copies.append(cp)
        for cp in copies:
            cp.wait_recv()
        for cp in copies:
            cp.wait_send()

    return _pcall(body, name=name, in_specs=[HBM_SPEC], out_specs=HBM_SPEC, out_shape=_sds((n, h, c), g4.dtype),
                  scratch=[pltpu.SemaphoreType.DMA((n,)), pltpu.SemaphoreType.DMA((n,))])(g4)


def _pair_add(g4, recv, core, name):
    n, _, h, c = g4.shape
    tm = _row_tile(h)

    def body(core_ref, a_ref, b_ref, o_ref, o16_ref):
        acc = a_ref[...] + b_ref[...]
        o_ref[...] = acc
        o16_ref[...] = acc.astype(BF16)

    out_spec = pl.BlockSpec((None, tm, c), lambda k, i, cr: (k, i, 0))
    return _pcall(body, name=name, grid=(n, h // tm), prefetch=1,
                  in_specs=[pl.BlockSpec((None, None, tm, c), lambda k, i, cr: (k, cr[0], i, 0)), out_spec],
                  out_specs=[out_spec, out_spec], out_shape=[_sds((n, h, c), F32), _sds((n, h, c), BF16)],
                  sem=("parallel", "parallel"))(core, g4, recv)


def _chip_exchange(p, name):
    n, h, c = p.shape

    def body(p_ref, out_ref, send_sems, recv_sems):
        mx, my, mc = _my_pos()
        copies = []
        for j, (px, py, pk) in enumerate(_chip_flips(mx, my)):
            cp = pltpu.make_async_remote_copy(src_ref=p_ref.at[pk], dst_ref=out_ref.at[j], send_sem=send_sems.at[j],
                                              recv_sem=recv_sems.at[j], device_id=(px, py, mc), device_id_type=MESH)
            cp.start()
            copies.append(cp)
        for cp in copies:
            cp.wait_recv()
        for cp in copies:
            cp.wait_send()

    return _pcall(body, name=name, in_specs=[HBM_SPEC], out_specs=HBM_SPEC, out_shape=_sds((3, h, c), p.dtype),
                  scratch=[pltpu.SemaphoreType.DMA((3,)), pltpu.SemaphoreType.DMA((3,))])(p)


def _chip_sum(p, slots, chip, name):
    _, h, c = p.shape
    tm = _row_tile(h)

    def body(chip_ref, p_ref, q_ref, o_ref):
        o_ref[...] = ((p_ref[...] + q_ref[0].astype(F32)) + q_ref[1].astype(F32)) + q_ref[2].astype(F32)

    return _pcall(body, name=name, grid=(h // tm,), prefetch=1,
                  in_specs=[pl.BlockSpec((None, tm, c), lambda i, ch: (ch[0], i, 0)),
                            pl.BlockSpec((3, tm, c), lambda i, ch: (0, i, 0))],
                  out_specs=pl.BlockSpec((tm, c), lambda i, ch: (i, 0)),
                  out_shape=_sds((h, c), F32), sem=("parallel",))(chip, p, slots)


def _sum_slots(q, name):
    n, h, c = q.shape
    tm = _row_tile(h)

    def body(q_ref, o_ref):
        acc = q_ref[0]
        for k in range(1, n):
            acc = acc + q_ref[k]
        o_ref[...] = acc

    return _pcall(body, name=name, grid=(h // tm,),
                  in_specs=[pl.BlockSpec((n, tm, c), lambda i: (0, i, 0))],
                  out_specs=pl.BlockSpec((tm, c), lambda i: (i, 0)),
                  out_shape=_sds((h, c), F32), sem=("parallel",))(q)


def _pair_share(f, name):
    h, c = f.shape

    def body(f_ref, out_ref, send_sem, recv_sem):
        mx, my, mc = _my_pos()
        cp = pltpu.make_async_remote_copy(src_ref=f_ref, dst_ref=out_ref, send_sem=send_sem, recv_sem=recv_sem,
                                          device_id=(mx, my, 1 - mc), device_id_type=MESH)
        cp.start()
        cp.wait_recv()
        cp.wait_send()

    return _pcall(body, name=name, in_specs=[HBM_SPEC], out_specs=HBM_SPEC, out_shape=_sds((h, c), f.dtype),
                  scratch=[pltpu.SemaphoreType.DMA, pltpu.SemaphoreType.DMA])(f)


BIG = ("mlp_w1", "mlp_w2", "fox_w_in", "fox_w_o", "ssm_w_in", "ssm_w_out")
SMALL_SHARDED = ("ssm_conv_w", "ssm_conv_b", "ssm_norm_w")
PACK_COLS = 1024


def _pack_rows(parts, rows_multiple, dtype):
    flat = jnp.concatenate([p.reshape(-1).astype(dtype) for p in parts])
    unit = rows_multiple * PACK_COLS
    total = -(-flat.shape[0] // unit) * unit
    flat = jnp.pad(flat, (0, total - flat.shape[0]))
    return flat.reshape(total // PACK_COLS, PACK_COLS)


def _unpack(flat, shapes):
    out, off = [], 0
    for sh in shapes:
        n = 1
        for d_ in sh:
            n *= d_
        out.append(flat[off:off + n].reshape(sh))
        off += n
    return out


PIECE_ROWS = 16


def _piece_rows(shape):
    n = 1
    for d_ in shape:
        n *= d_
    rows = -(-n // PACK_COLS)
    return n, -(-rows // PIECE_ROWS) * PIECE_ROWS


def _pack2d(parts, rows_multiple, dtype):
    blocks = []
    for p in parts:
        n, rows = _piece_rows(p.shape)
        a = p.astype(dtype)
        if p.shape[-1] != PACK_COLS or n % PACK_COLS:
            a = jnp.pad(a.reshape(-1), (0, -n % PACK_COLS))
        a = a.reshape(-1, PACK_COLS)
        blocks.append(jnp.pad(a, ((0, rows - a.shape[0]), (0, 0))))
    total = sum(b.shape[0] for b in blocks)
    pad = -total % rows_multiple
    if pad:
        blocks.append(jnp.zeros((pad, PACK_COLS), dtype))
    return jnp.concatenate(blocks, axis=0)


def _unpack2d(buf, shapes):
    out, off = [], 0
    for sh in shapes:
        n, rows = _piece_rows(sh)
        piece = buf[off:off + rows]
        if sh[-1] == PACK_COLS and n % PACK_COLS == 0:
            out.append(piece[:n // PACK_COLS].reshape(sh))
        else:
            out.append(piece.reshape(-1)[:n].reshape(sh))
        off += rows
    return out


def _row_tile(h, cap=512):
    for step in (16, 8):
        best = 0
        for cand in range(step, cap + 1, step):
            if h % cand == 0:
                best = cand
        if best:
            return best
    return h


def _chip_slice(full, axis, k, width):
    idx = [slice(None)] * full.ndim
    idx[axis] = slice(k * width, (k + 1) * width)
    return full[tuple(idx)]


SHARD_AXIS = dict(mlp_w1=2, mlp_w2=1, fox_w_in=2, fox_w_o=1, ssm_w_in=2, ssm_w_out=1, ssm_conv_w=2,
                  ssm_conv_b=1, ssm_norm_w=1, ada_w=2)


def kernel(x, c, ada_w, ada_b, ln_mix_g, ln_mix_b, ln_mlp_g, ln_mlp_b, mlp_w1, mlp_w2, fox_w_in, fox_b_f, fox_w_o, ssm_w_in, ssm_conv_w, ssm_conv_b, ssm_dt_bias, ssm_a_log, ssm_d, ssm_norm_w, ssm_w_out, loss_target, m_ada_w, m_ada_b, m_ln_mix_g, m_ln_mix_b, m_ln_mlp_g, m_ln_mlp_b, m_mlp_w1, m_mlp_w2, m_fox_w_in, m_fox_b_f, m_fox_w_o, m_ssm_w_in, m_ssm_conv_w, m_ssm_conv_b, m_ssm_dt_bias, m_ssm_a_log, m_ssm_d, m_ssm_norm_w, m_ssm_w_out, v_ada_w, v_ada_b, v_ln_mix_g, v_ln_mix_b, v_ln_mlp_g, v_ln_mlp_b, v_mlp_w1, v_mlp_w2, v_fox_w_in, v_fox_b_f, v_fox_w_o, v_ssm_w_in, v_ssm_conv_w, v_ssm_conv_b, v_ssm_dt_bias, v_ssm_a_log, v_ssm_d, v_ssm_norm_w, v_ssm_w_out):
    names = ("ada_w", "ada_b", "ln_mix_g", "ln_mix_b", "ln_mlp_g", "ln_mlp_b", "mlp_w1", "mlp_w2", "fox_w_in",
             "fox_b_f", "fox_w_o", "ssm_w_in", "ssm_conv_w", "ssm_conv_b", "ssm_dt_bias", "ssm_a_log", "ssm_d",
             "ssm_norm_w", "ssm_w_out")
    weights = dict(zip(names, (ada_w, ada_b, ln_mix_g, ln_mix_b, ln_mlp_g, ln_mlp_b, mlp_w1, mlp_w2, fox_w_in,
                               fox_b_f, fox_w_o, ssm_w_in, ssm_conv_w, ssm_conv_b, ssm_dt_bias, ssm_a_log, ssm_d,
                               ssm_norm_w, ssm_w_out)))
    m_in = dict(zip(names, (m_ada_w, m_ada_b, m_ln_mix_g, m_ln_mix_b, m_ln_mlp_g, m_ln_mlp_b, m_mlp_w1, m_mlp_w2,
                            m_fox_w_in, m_fox_b_f, m_fox_w_o, m_ssm_w_in, m_ssm_conv_w, m_ssm_conv_b, m_ssm_dt_bias,
                            m_ssm_a_log, m_ssm_d, m_ssm_norm_w, m_ssm_w_out)))
    v_in = dict(zip(names, (v_ada_w, v_ada_b, v_ln_mix_g, v_ln_mix_b, v_ln_mlp_g, v_ln_mlp_b, v_mlp_w1, v_mlp_w2,
                            v_fox_w_in, v_fox_b_f, v_fox_w_o, v_ssm_w_in, v_ssm_conv_w, v_ssm_conv_b, v_ssm_dt_bias,
                            v_ssm_a_log, v_ssm_d, v_ssm_norm_w, v_ssm_w_out)))

    mx, my, mc = _my_pos()
    chip = 2 * mx + my
    me = 4 * mx + 2 * my + mc
    x0 = x[0]
    target = loss_target[0]
    s, d = x0.shape
    n_qkv = 3 * FOX_HEADS * HEAD_DIM

    big_shapes = [weights[n].shape for n in BIG]
    packed = _pack2d([weights[n] for n in BIG], 32, BF16)
    gathered = _gather_chips(packed.reshape(2, packed.shape[0] // 2, PACK_COLS), "gather_weights")
    gathered = gathered.reshape(4, packed.shape[0], PACK_COLS)
    per_chip = [_unpack2d(jnp.where(chip == k, packed, gathered[k]), big_shapes) for k in range(4)]
    full = {n: jnp.concatenate([per_chip[k][i] for k in range(4)], axis=SHARD_AXIS[n]) for i, n in enumerate(BIG)}

    small_shapes = [weights[n].shape for n in SMALL_SHARDED]
    small_packed = _pack_rows([weights[n] for n in SMALL_SHARDED] + [c], 8, F32).reshape(-1, LANES)
    small_all = _allgather8(small_packed, "gather_small")
    small_chip = [_unpack(small_all[2 * k].reshape(-1), small_shapes) for k in range(4)]
    small_full = {n: jnp.concatenate([small_chip[k][i] for k in range(4)], axis=SHARD_AXIS[n])
                  for i, n in enumerate(SMALL_SHARDED)}
    n_small = sum(weights[n].size for n in SMALL_SHARDED)
    c_all = small_all.reshape(8, -1)[:, n_small:n_small + d]

    cols = ada_w.shape[2]
    ada_b_cols = lax.dynamic_slice_in_dim(ada_b, chip * cols, cols, axis=1)[:, None, :]
    c16 = jnp.pad(c_all, ((0, 8), (0, 0)))
    mod_part = _ada_forward(c16, ada_w, ada_b_cols, "ada_fwd")[:, :8, :]
    mod_all = _allgather8(mod_part.reshape(-1, LANES), "gather_mod").reshape(8, DEPTH, 8, cols)
    mod_mine = jnp.stack([lax.dynamic_index_in_dim(mod_all[2 * k], me, axis=1, keepdims=False) for k in range(4)], axis=1)
    mods = [jnp.pad(mod_mine[i].reshape(6, d), ((0, 2), (0, 0))) for i in range(DEPTH)]

    w = dict(
        fox_qkv=full["fox_w_in"][0][:, :n_qkv],
        fox_f=jnp.pad(full["fox_w_in"][0][:, n_qkv:], ((0, 0), (0, LANES - FOX_HEADS))),
        fox_o=full["fox_w_o"][0],
        fox_bf=jnp.pad(fox_b_f, ((0, 0), (0, LANES - FOX_HEADS))),
        head_ind=(jnp.arange(d)[:, None] // HEAD_DIM == jnp.arange(LANES)[None, :]).astype(BF16),
    )
    w.update(_ssm_weights(full["ssm_w_in"][0], small_full["ssm_conv_w"][0], small_full["ssm_conv_b"], ssm_dt_bias,
                          ssm_a_log, ssm_d, small_full["ssm_norm_w"], full["ssm_w_out"][0]))
    mixers = ((_fox_forward, _fox_backward), (_ssd_forward, _ssd_backward))

    saved = []
    xin = x0
    u = _modulate_in(x0, mods[0], "l0_mod_in")
    for i in range(DEPTH):
        tag = "l%d" % i
        y, sv = mixers[i % 2][0](u, w)
        r, x1, u2 = _res_ln_mod(xin, y, mods[i], ln_mix_g[i:i + 1], ln_mix_b[i:i + 1], tag + "_res_ln1")
        y2, hr, a = _mlp_forward(u2, full["mlp_w1"][i], full["mlp_w2"][i], tag)
        if i + 1 < DEPTH:
            r2, xin, u = _res_ln_next(x1, y2, mods[i], ln_mlp_g[i:i + 1], ln_mlp_b[i:i + 1], mods[i + 1],
                                      tag + "_res_ln2")
        else:
            r2 = _rowwise(lambda xv, yv, m: (ALPHA * xv + (1.0 + m[5:6]) * yv,), [x1, y2], [mods[i]], [(d, F32)], [],
                          name=tag + "_res2")[0]
        saved.append(dict(y=y, r=r, u2=u2, hr=hr, a=a, y2=y2, r2=r2, mix=sv))

    grads = {}
    dmod_parts = [dict() for _ in range(DEPTH)]
    ln_grads = {n: [None] * DEPTH for n in ("ln_mix_g", "ln_mix_b", "ln_mlp_g", "ln_mlp_b")}
    g_w1, g_w2 = [None] * DEPTH, [None] * DEPTH
    du = dres0 = None
    for i in reversed(range(DEPTH)):
        tag = "l%d" % i
        sv = saved[i]
        if i + 1 == DEPTH:
            dy2, dres, sq, dg2, db2, dgm = _loss_ln2_bwd(sv["r2"], sv["y2"], target, mods[i], ln_mlp_g[i:i + 1],
                                                         ln_mlp_b[i:i + 1], "loss_ln2_bwd")
            loss = lax.psum(0.5 * jnp.sum(sq) / d, ("x", "y", "c"))
        else:
            dy2, dres, dsca, dsha, dg2, db2, dgm = _mod_in_ln2_bwd(du, dres0, sv["r2"], sv["y2"], mods[i],
                                                                   ln_mlp_g[i:i + 1], ln_mlp_b[i:i + 1], mods[i + 1],
                                                                   tag + "_ln2_bwd")
            dmod_parts[i + 1].update(sc_a=dsca, sh_a=dsha)
        du2, g_w1[i], g_w2[i] = _mlp_backward(dy2, sv["u2"], sv["hr"], sv["a"], full["mlp_w1"][i], full["mlp_w2"][i], tag)
        dy, dres0, dscm, dshm, dg1, db1, dga = _ln1_bwd(du2, dres, sv["r"], sv["y"], mods[i], ln_mix_g[i:i + 1],
                                                        ln_mix_b[i:i + 1], tag + "_ln1_bwd")
        du, mg = mixers[i % 2][1](dy, sv["mix"], w)
        grads.update(mg)
        dmod_parts[i].update(g_a=dga, sh_m=dshm, sc_m=dscm, g_m=dgm)
        ln_grads["ln_mix_g"][i], ln_grads["ln_mix_b"][i] = dg1, db1
        ln_grads["ln_mlp_g"][i], ln_grads["ln_mlp_b"][i] = dg2, db2
    dx, dsca, dsha = _mod_in_bwd(du, dres0, x0, mods[0], "l0_mod_in_bwd")
    dmod_parts[0].update(sc_a=dsca, sh_a=dsha)
    dmods = [jnp.concatenate([p["sh_a"], p["sc_a"], p["g_a"], p["sh_m"], p["sc_m"], p["g_m"]], axis=1)
             for p in dmod_parts]
    grad_x = dx[None]
    grads["mlp_w1"] = jnp.stack(g_w1)
    grads["mlp_w2"] = jnp.stack(g_w2)
    for n in ("fox_w_in", "fox_w_o", "ssm_w_in", "ssm_w_out", "ssm_conv_w"):
        grads[n] = grads[n][None]

    small_names = ("ln_mix_g", "ln_mix_b", "ln_mlp_g", "ln_mlp_b", "fox_b_f", "ssm_dt_bias", "ssm_a_log", "ssm_d")
    small_parts = list(dmods)
    for n in small_names[:4]:
        small_parts.append(jnp.concatenate(ln_grads[n], axis=0))
    for n in small_names[4:]:
        small_parts.append(jnp.pad(grads[n], ((0, 0), (0, LANES - grads[n].shape[1]))))
    small_vec = _pack_rows(small_parts, 1, F32).reshape(-1, LANES)
    small_vec = jnp.pad(small_vec, ((0, -small_vec.shape[0] % 8), (0, 0)))
    small_g_all = _allgather8(small_vec, "gather_small_grads")
    small_sum = _sum_slots(small_g_all, "sum_small_grads").reshape(-1)
    dmod_sum = small_sum[:DEPTH * 6 * d].reshape(DEPTH, 6 * d)
    off = DEPTH * 6 * d
    final = {"ada_b": dmod_sum}
    for n in small_names[:4]:
        final[n] = small_sum[off:off + DEPTH * d].reshape(DEPTH, d)
        off += DEPTH * d
    for n in small_names[4:]:
        width = weights[n].shape[1]
        final[n] = small_sum[off:off + width].reshape(1, width)
        off += LANES

    dmod_all = small_g_all.reshape(8, -1)[:, :DEPTH * 6 * d].reshape(8, DEPTH, 6 * d)
    dmod_cols = lax.dynamic_slice_in_dim(dmod_all, chip * cols, cols, axis=2).transpose(1, 0, 2)
    final["ada_w"] = _ada_backward(c_all.T, dmod_cols, "ada_bwd")

    sharded = BIG + SMALL_SHARDED
    shard_shapes = [weights[n].shape for n in sharded]
    per_target = []
    for k in range(4):
        parts = [_chip_slice(grads[n], SHARD_AXIS[n], k, weights[n].shape[SHARD_AXIS[n]]) for n in sharded]
        per_target.append(_pack2d(parts, 128, F32))
    g_all = jnp.stack(per_target)
    rows = g_all.shape[1]
    g4 = g_all.reshape(4, 2, rows // 2, PACK_COLS)
    recv = _pair_exchange(g4, "rs_pair_exchange")
    part, part16 = _pair_add(g4, recv, jnp.reshape(mc, (1,)).astype(jnp.int32), "rs_pair_add")
    slots = _chip_exchange(part16, "rs_chip_exchange")
    half = _chip_sum(part, slots, jnp.reshape(chip, (1,)).astype(jnp.int32), "rs_chip_sum")
    other = _pair_share(half, "rs_pair_share")
    both = jnp.concatenate([jnp.where(mc == 0, half, other), jnp.where(mc == 0, other, half)], axis=0)
    for n, g_shard in zip(sharded, _unpack2d(both, shard_shapes)):
        final[n] = g_shard

    outs_g, outs_d, outs_m, outs_v = [], [], [], []
    for n in names:
        wv = weights[n]
        two_d = (-1, wv.shape[-1])
        delta, mn, vn = _adamw(wv.reshape(two_d), final[n].reshape(two_d), m_in[n].reshape(two_d),
                               v_in[n].reshape(two_d), "adamw_" + n)
        outs_g.append(final[n].reshape(wv.shape))
        outs_d.append(delta.reshape(wv.shape))
        outs_m.append(mn.reshape(wv.shape))
        outs_v.append(vn.reshape(wv.shape))
    return (loss, grad_x, *outs_g, *outs_d, *outs_m, *outs_v)
```

```python
import functools

import jax
import jax.numpy as jnp
import numpy as np
from jax import lax
from jax.experimental import pallas as pl
from jax.experimental.pallas import tpu as pltpu

F32, BF16 = jnp.float32, jnp.bfloat16
MESH = pl.DeviceIdType.MESH
HBM_SPEC = pl.BlockSpec(memory_space=pltpu.HBM)

VMEM_LIMIT_BYTES = 52 * 2**20
LANES = 128

FOX_HEADS, HEAD_DIM = 16, 64
SSM_HEADS, SSM_GROUPS, SSM_STATE, SSM_CHUNK, SSM_CONV = 32, 8, 128, 128, 4
SSM_GROUP_WIDTH = 256
LN_EPS, RMS_EPS = 1e-5, 1e-5
DEPTH = 2
ALPHA = (2.0 * DEPTH) ** 0.25
ADAM_LR, ADAM_B1, ADAM_B2, ADAM_EPS, ADAM_WD, ADAM_STEP = 0.001, 0.9, 0.999, 1e-08, 0.01, 10

ATT_TILE = 512
ROW_TILE = 512
SCAN_TILE = 512
SSD_GROUPS_PER_STEP = 4
MM_TM, MM_TN, MM_TK = 1024, 1024, 1024

NT_DIMS = (((1,), (1,)), ((), ()))
TN_DIMS = (((0,), (0,)), ((), ()))
NN_DIMS = (((1,), (0,)), ((), ()))


def _pcall(body, *, name, out_shape, grid=(), in_specs=None, out_specs=None, scratch=(), sem=None, prefetch=0):
    params = dict(vmem_limit_bytes=VMEM_LIMIT_BYTES)
    if sem is not None:
        params["dimension_semantics"] = sem
    if prefetch:
        grid_spec = pltpu.PrefetchScalarGridSpec(num_scalar_prefetch=prefetch, grid=grid, in_specs=in_specs,
                                                 out_specs=out_specs, scratch_shapes=scratch)
        return pl.pallas_call(body, out_shape=out_shape, grid_spec=grid_spec, name=name,
                              compiler_params=pltpu.CompilerParams(**params))
    kwargs = {}
    if in_specs is not None:
        kwargs["in_specs"] = in_specs
    if out_specs is not None:
        kwargs["out_specs"] = out_specs
    return pl.pallas_call(body, out_shape=out_shape, grid=grid, scratch_shapes=scratch, name=name,
                          compiler_params=pltpu.CompilerParams(**params), **kwargs)


def _sds(shape, dtype):
    return jax.ShapeDtypeStruct(tuple(shape), dtype)


def _dot(a, b, dims=NN_DIMS):
    return lax.dot_general(a, b, dims, preferred_element_type=F32)


def _sigmoid(x):
    return 1.0 / (1.0 + jnp.exp(-x))


def _silu(x):
    return x * _sigmoid(x)


def _dsilu(x):
    s = _sigmoid(x)
    return s * (1.0 + x * (1.0 - s))


def _dot_split(x, m16, dims=NN_DIMS, passes=3):
    hi = x.astype(BF16)
    r1 = x - hi.astype(F32)
    mid = r1.astype(BF16)
    out = _dot(hi, m16, dims) + _dot(mid, m16, dims)
    if passes == 3:
        lo = (r1 - mid.astype(F32)).astype(BF16)
        out = out + _dot(lo, m16, dims)
    return out


def _mm(a, b, dims, outs, *, name, tm=MM_TM, tn=MM_TN, tk=MM_TK, epi=None, extra=()):
    if dims == "nn":
        (m, k), (k2, n) = a.shape, b.shape
    elif dims == "nt":
        (m, k), (n, k2) = a.shape, b.shape
    else:
        (k, m), (k2, n) = a.shape, b.shape
    assert k == k2, (a.shape, b.shape, dims)
    tm, tn, tk = min(tm, m), min(tn, n), min(tk, k)
    assert m % tm == 0 and n % tn == 0 and k % tk == 0, (m, n, k, tm, tn, tk)
    nk = k // tk
    dn = {"nn": NN_DIMS, "nt": NT_DIMS, "tn": TN_DIMS}[dims]
    n_extra, n_out = len(extra), len(outs)
    if epi is None:
        epi = lambda acc: (acc,) * n_out

    def body(a_ref, b_ref, *rest):
        extra_refs, out_refs, acc_ref = rest[:n_extra], rest[n_extra:n_extra + n_out], rest[-1]
        kk = pl.program_id(2)

        @pl.when(kk == 0)
        def _():
            acc_ref[...] = jnp.zeros_like(acc_ref)

        acc_ref[...] += _dot(a_ref[...].astype(BF16), b_ref[...].astype(BF16), dn)

        @pl.when(kk == nk - 1)
        def _():
            res = epi(acc_ref[...], *[e[...] for e in extra_refs])
            for o, r in zip(out_refs, res):
                o[...] = r.astype(o.dtype)

    if dims == "tn":
        a_spec = pl.BlockSpec((tk, tm), lambda i, j, kk: (kk, i))
    else:
        a_spec = pl.BlockSpec((tm, tk), lambda i, j, kk: (i, kk))
    if dims == "nt":
        b_spec = pl.BlockSpec((tn, tk), lambda i, j, kk: (j, kk))
    else:
        b_spec = pl.BlockSpec((tk, tn), lambda i, j, kk: (kk, j))
    o_spec = pl.BlockSpec((tm, tn), lambda i, j, kk: (i, j))
    res = _pcall(body, name=name, grid=(m // tm, n // tn, nk),
                 in_specs=[a_spec, b_spec] + [o_spec] * n_extra,
                 out_specs=[o_spec] * n_out,
                 out_shape=[_sds((m, n), d) for d in outs],
                 scratch=[pltpu.VMEM((tm, tn), F32)],
                 sem=("parallel", "parallel", "arbitrary"))(a, b, *extra)
    return res[0] if n_out == 1 else res


def _k_blocks(a, tk=None):
    tk = tk or MM_TK
    return [(a, kb) for kb in range(a.shape[1] // tk)]


def _mm_nt_blocks(a_blocks, b, start, *, name, tm=MM_TM, tk=None):
    tk = tk or MM_TK
    tm = min(tm, a_blocks[0][0].shape[0])
    m, n, p_n = a_blocks[0][0].shape[0], b.shape[0], len(a_blocks)
    assert b.shape[1] == p_n * tk and m % tm == 0

    def body(*refs):
        a_refs = refs[:p_n]
        b_ref, s_ref, o_ref, acc_ref = refs[p_n:]
        kk = pl.program_id(1)

        @pl.when(kk == 0)
        def _():
            acc_ref[...] = s_ref[...]
        for p in range(p_n):
            @pl.when(kk == p)
            def _(p=p):
                acc_ref[...] += _dot(a_refs[p][...].astype(BF16), b_ref[...].astype(BF16), NT_DIMS)

        @pl.when(kk == p_n - 1)
        def _():
            o_ref[...] = acc_ref[...]

    in_specs = [pl.BlockSpec((tm, tk), functools.partial(lambda kb, i, kk: (i, kb), kb)) for _, kb in a_blocks]
    in_specs += [pl.BlockSpec((n, tk), lambda i, kk: (0, kk)), pl.BlockSpec((tm, n), lambda i, kk: (i, 0))]
    return _pcall(body, name=name, grid=(m // tm, p_n), in_specs=in_specs,
                  out_specs=pl.BlockSpec((tm, n), lambda i, kk: (i, 0)), out_shape=_sds((m, n), F32),
                  scratch=[pltpu.VMEM((tm, n), F32)], sem=("parallel", "arbitrary"))(*[a for a, _ in a_blocks], b, start)


def _rowwise(fn, rows, consts, row_outs, acc_outs, *, name, tm=ROW_TILE):
    s = rows[0].shape[0]
    tm = min(tm, s)
    assert s % tm == 0
    n_in, n_o = len(rows) + len(consts), len(row_outs)

    def body(*refs):
        ins, outs = refs[:n_in], refs[n_in:]
        res = fn(*[r[...] for r in ins])
        if not isinstance(res, (tuple, list)):
            res = (res,)
        for o, val in zip(outs[:n_o], res[:n_o]):
            o[...] = val.astype(o.dtype)
        if acc_outs:
            @pl.when(pl.program_id(0) == 0)
            def _():
                for o in outs[n_o:]:
                    o[...] = jnp.zeros_like(o)
            for o, val in zip(outs[n_o:], res[n_o:]):
                o[...] += val

    in_specs = [pl.BlockSpec((tm, r.shape[1]), lambda i: (i, 0)) for r in rows]
    in_specs += [pl.BlockSpec(c.shape, functools.partial(lambda nd, i: (0,) * nd, c.ndim)) for c in consts]
    out_specs = [pl.BlockSpec((tm, c), lambda i: (i, 0)) for c, _ in row_outs]
    out_specs += [pl.BlockSpec(tuple(sh), lambda i: (0, 0)) for sh in acc_outs]
    out_shape = [_sds((s, c), d) for c, d in row_outs] + [_sds(sh, F32) for sh in acc_outs]
    res = _pcall(body, name=name, grid=(s // tm,), in_specs=in_specs, out_specs=out_specs,
                 out_shape=out_shape, sem=("arbitrary",))(*rows, *consts)
    return res


def _colsum(x):
    return jnp.sum(x, axis=0, keepdims=True)


def _ln_stats(r):
    mu = jnp.mean(r, axis=-1, keepdims=True)
    xc = r - mu
    var = jnp.mean(xc * xc, axis=-1, keepdims=True)
    rstd = lax.rsqrt(var + LN_EPS)
    return xc * rstd, rstd


def _ln_bwd(dy, xhat, rstd, gamma):
    dyg = dy * gamma
    m1 = jnp.mean(dyg, axis=-1, keepdims=True)
    m2 = jnp.mean(dyg * xhat, axis=-1, keepdims=True)
    return rstd * (dyg - m1 - xhat * m2)


def _modulate_in(x, mod, name):
    def fn(xv, m):
        return (xv * (1.0 + m[1:2]) + m[0:1],)
    return _rowwise(fn, [x], [mod], [(x.shape[1], BF16)], [], name=name)[0]


def _res_ln_mod(x, y, mod, g, b, name):
    d = x.shape[1]

    def fn(xv, yv, m, gv, bv):
        r = ALPHA * xv + (1.0 + m[2:3]) * yv
        xhat, _ = _ln_stats(r)
        x1 = xhat * gv + bv
        u2 = x1 * (1.0 + m[4:5]) + m[3:4]
        return r, x1, u2
    return _rowwise(fn, [x, y], [mod, g, b], [(d, F32), (d, F32), (d, BF16)], [], name=name)


def _res_ln_next(x, y, mod, g, b, mod_next, name):
    d = x.shape[1]

    def fn(xv, yv, m, gv, bv, mn):
        r = ALPHA * xv + (1.0 + m[5:6]) * yv
        xhat, _ = _ln_stats(r)
        out = xhat * gv + bv
        return r, out, out * (1.0 + mn[1:2]) + mn[0:1]
    return _rowwise(fn, [x, y], [mod, g, b, mod_next], [(d, F32), (d, F32), (d, BF16)], [], name=name)


def _loss_ln2_bwd(r2, y2, target, mod, g, b, name):
    d = r2.shape[1]

    def fn(rv, yv, tv, m, gv, bv):
        xhat, rstd = _ln_stats(rv)
        e = xhat * gv + bv - tv
        dxv = e * (1.0 / d)
        dr = _ln_bwd(dxv, xhat, rstd, gv)
        return (dr * (1.0 + m[5:6]), ALPHA * dr,
                _colsum(e * e), _colsum(dxv * xhat), _colsum(dxv), _colsum(dr * yv))
    return _rowwise(fn, [r2, y2, target], [mod, g, b], [(d, BF16), (d, F32)], [(1, d)] * 4, name=name)


def _mod_in_ln2_bwd(du, dres, r2, y2, mod, g, b, mod_next, name):
    d = du.shape[1]

    def fn(duv, drv, rv, yv, m, gv, bv, mn):
        xhat, rstd = _ln_stats(rv)
        xout = xhat * gv + bv
        dxv = duv * (1.0 + mn[1:2]) + drv
        dr = _ln_bwd(dxv, xhat, rstd, gv)
        return (dr * (1.0 + m[5:6]), ALPHA * dr,
                _colsum(duv * xout), _colsum(duv), _colsum(dxv * xhat), _colsum(dxv), _colsum(dr * yv))
    return _rowwise(fn, [du, dres, r2, y2], [mod, g, b, mod_next], [(d, BF16), (d, F32)], [(1, d)] * 5, name=name)


def _ln1_bwd(du2, dres, r, y, mod, g, b, name):
    d = du2.shape[1]

    def fn(duv, drv, rv, yv, m, gv, bv):
        xhat, rstd = _ln_stats(rv)
        x1 = xhat * gv + bv
        dx1 = duv * (1.0 + m[4:5]) + drv
        dr = _ln_bwd(dx1, xhat, rstd, gv)
        return (dr * (1.0 + m[2:3]), ALPHA * dr,
                _colsum(duv * x1), _colsum(duv), _colsum(dx1 * xhat), _colsum(dx1), _colsum(dr * yv))
    return _rowwise(fn, [du2, dres, r, y], [mod, g, b], [(d, BF16), (d, F32)], [(1, d)] * 5, name=name)


def _mod_in_bwd(du, dres, x, mod, name):
    d = du.shape[1]

    def fn(duv, drv, xv, m):
        return duv * (1.0 + m[1:2]) + drv, _colsum(duv * xv), _colsum(duv)
    return _rowwise(fn, [du, dres, x], [mod], [(d, F32)], [(1, d)] * 2, name=name)


def _fox_gate(fraw, b_pad, name):
    s = fraw.shape[0]
    tb = min(SCAN_TILE, s)

    def body(f_ref, b_ref, cum_ref, rows_ref, carry):
        @pl.when(pl.program_id(0) == 0)
        def _():
            carry[...] = jnp.zeros_like(carry)
        z = f_ref[...] + b_ref[...]
        lf = jnp.minimum(z, 0.0) - jnp.log(1.0 + jnp.exp(-jnp.abs(z)))
        lane = lax.broadcasted_iota(jnp.int32, (tb, LANES), 1)
        row = lax.broadcasted_iota(jnp.int32, (tb, LANES), 0)
        c = jnp.where(lane < FOX_HEADS, lf, 0.0)
        sh = 1
        while sh < tb:
            c = c + jnp.where(row >= sh, pltpu.roll(c, sh, 0), 0.0)
            sh *= 2
        c = c + carry[0:1, :]
        cum_ref[...] = c
        rows_ref[...] = c.T[0:FOX_HEADS, :]
        carry[0:1, :] = c[tb - 1:tb, :]

    return _pcall(body, name=name, grid=(s // tb,),
                  in_specs=[pl.BlockSpec((tb, LANES), lambda i: (i, 0)), pl.BlockSpec((1, LANES), lambda i: (0, 0))],
                  out_specs=[pl.BlockSpec((tb, LANES), lambda i: (i, 0)), pl.BlockSpec((FOX_HEADS, tb), lambda i: (0, i))],
                  out_shape=[_sds((s, LANES), F32), _sds((FOX_HEADS, s), F32)], scratch=[pltpu.VMEM((8, LANES), F32)],
                  sem=("arbitrary",))(fraw, b_pad)


def _fox_gate_bwd(drow, dcol, fraw, b_pad, name):
    s = fraw.shape[0]
    tb = min(SCAN_TILE, s)
    n = s // tb

    def body(dr_ref, dc_ref, f_ref, b_ref, df_ref, db_ref, carry):
        @pl.when(pl.program_id(0) == 0)
        def _():
            carry[...] = jnp.zeros_like(carry)
            db_ref[...] = jnp.zeros_like(db_ref)
        row = lax.broadcasted_iota(jnp.int32, (tb, LANES), 0)
        c = dr_ref[...] + dc_ref[...]
        sh = 1
        while sh < tb:
            c = c + jnp.where(row + sh < tb, pltpu.roll(c, tb - sh, 0), 0.0)
            sh *= 2
        c = c + carry[0:1, :]
        carry[0:1, :] = c[0:1, :]
        z = f_ref[...] + b_ref[...]
        df = c * (1.0 / (1.0 + jnp.exp(z)))
        df_ref[...] = df.astype(df_ref.dtype)
        db_ref[...] += _colsum(df)

    rev = lambda i: (n - 1 - i, 0)
    return _pcall(body, name=name, grid=(n,),
                  in_specs=[pl.BlockSpec((tb, LANES), rev)] * 3 + [pl.BlockSpec((1, LANES), lambda i: (0, 0))],
                  out_specs=[pl.BlockSpec((tb, LANES), rev), pl.BlockSpec((1, LANES), lambda i: (0, 0))],
                  out_shape=[_sds((s, LANES), BF16), _sds((1, LANES), F32)],
                  scratch=[pltpu.VMEM((8, LANES), F32)], sem=("arbitrary",))(drow, dcol, fraw, b_pad)


def _head_pair_masks(t):
    lane = lax.broadcasted_iota(jnp.int32, (t, LANES), 1)
    return lane < HEAD_DIM


def _lane_blocks(x):
    return [x[:, c * LANES:(c + 1) * LANES] for c in range(x.shape[1] // LANES)]


def _sum_list(xs):
    acc = xs[0]
    for x in xs[1:]:
        acc = acc + x
    return acc


def _causal(t, transposed=False):
    ri = lax.broadcasted_iota(jnp.int32, (t, t), 0)
    ci = lax.broadcasted_iota(jnp.int32, (t, t), 1)
    return ci >= ri if transposed else ri >= ci


def _span_mask(r0, r1, c0, c1, transposed=False):
    ri = lax.broadcasted_iota(jnp.int32, (r1 - r0, c1 - c0), 0) + r0
    ci = lax.broadcasted_iota(jnp.int32, (r1 - r0, c1 - c0), 1) + c0
    return ci >= ri if transposed else ri >= ci


def _full_spans(t):
    return ((0, t, 0, t, False),)


def _diagonal_spans(t, transposed=False):
    h = t // 2
    if h % LANES:
        return ((0, t, 0, t, True),)
    if transposed:
        return ((0, h, 0, t, True), (h, t, h, t, True))
    return ((0, h, 0, h, True), (h, t, 0, t, True))


def _flash_fwd(qkv, ck_rows, kb_start, name):
    s = qkv.shape[0]
    t = min(ATT_TILE, s)
    nq = s // t
    scale = HEAD_DIM ** -0.5
    hp_blocks = FOX_HEADS // 2

    def body(ks_ref, q_ref, k_ref, v_ref, ck_ref, o_ref, lse_ref, lse_rows_ref, acc_ref, m_ref, l_ref):
        hp, qb = pl.program_id(0), pl.program_id(1)
        q2 = q_ref[...] * scale
        first = _head_pair_masks(t)
        zero = jnp.zeros_like(q2)
        qs = (jnp.where(first, q2, zero), jnp.where(first, zero, q2))
        m_ref[...] = jnp.full_like(m_ref, -jnp.inf)
        l_ref[...] = jnp.zeros_like(l_ref)
        acc_ref[...] = jnp.zeros_like(acc_ref)

        def tile(kb, spans):
            off = pl.multiple_of(kb * t, t)
            k2 = k_ref[pl.ds(off, t), :]
            v2 = v_ref[pl.ds(off, t), :]
            ck = ck_ref[kb]
            for r0, r1, c0, c1, masked in spans:
                kk, vv, fr = k2[c0:c1], v2[c0:c1], first[r0:r1]
                pvs, als = [], []
                for j in range(2):
                    sc = _dot(qs[j][r0:r1], kk, NT_DIMS) - ck[j:j + 1, c0:c1]
                    if masked:
                        sc = jnp.where(_span_mask(r0, r1, c0, c1), sc, -jnp.inf)
                    blocks = _lane_blocks(sc)
                    mx = blocks[0]
                    for b in blocks[1:]:
                        mx = jnp.maximum(mx, b)
                    m_old = m_ref[j, r0:r1]
                    m_new = jnp.maximum(m_old, jnp.max(mx, axis=1, keepdims=True))
                    ps = [jnp.exp(b - m_new) for b in blocks]
                    a = jnp.exp(m_old - m_new)
                    l_ref[j, r0:r1] = a * l_ref[j, r0:r1] + _sum_list(ps)
                    m_ref[j, r0:r1] = m_new
                    pvs.append(_dot(jnp.concatenate(ps, axis=1).astype(BF16), vv))
                    als.append(a)
                acc_ref[r0:r1] = jnp.where(fr, als[0], als[1]) * acc_ref[r0:r1] + jnp.where(fr, pvs[0], pvs[1])

        def step(kb, carry):
            tile(kb, _full_spans(t))
            return carry

        lax.fori_loop(ks_ref[hp, qb], qb, step, 0)
        tile(qb, ((0, t, 0, t, True),))
        l0 = jnp.sum(l_ref[0], axis=1, keepdims=True)
        l1 = jnp.sum(l_ref[1], axis=1, keepdims=True)
        o_ref[...] = acc_ref[...] / jnp.where(first, l0, l1)
        for j, lj in enumerate((l0, l1)):
            lse = m_ref[j] + jnp.log(jnp.broadcast_to(lj, (t, LANES)))
            lse_ref[:, j:j + 1] = lse[:, 0:1]
            lse_rows_ref[j:j + 1, :] = lse.T[0:1, :]

    return _pcall(
        body, name=name, grid=(hp_blocks, nq), prefetch=1,
        in_specs=[pl.BlockSpec((t, LANES), lambda h, i, ks: (i, h)),
                  pl.BlockSpec((s, LANES), lambda h, i, ks: (0, hp_blocks + h)),
                  pl.BlockSpec((s, LANES), lambda h, i, ks: (0, 2 * hp_blocks + h)),
                  pl.BlockSpec((None, nq, 2, t), lambda h, i, ks: (h, 0, 0, 0))],
        out_specs=[pl.BlockSpec((t, LANES), lambda h, i, ks: (i, h)),
                   pl.BlockSpec((None, t, 2), lambda h, i, ks: (h, i, 0)),
                   pl.BlockSpec((None, None, 2, t), lambda h, i, ks: (h, i, 0, 0))],
        out_shape=[_sds((s, hp_blocks * LANES), F32), _sds((hp_blocks, s, 2), F32), _sds((hp_blocks, nq, 2, t), F32)],
        scratch=[pltpu.VMEM((t, LANES), F32), pltpu.VMEM((2, t, LANES), F32), pltpu.VMEM((2, t, LANES), F32)],
        sem=("parallel", "arbitrary"))(kb_start, qkv, qkv, qkv, ck_rows)


def _flash_dq(qkv, do16, ck_rows, lse_c, delta, kb_start, name):
    s = qkv.shape[0]
    t = min(ATT_TILE, s)
    nq = s // t
    scale = HEAD_DIM ** -0.5
    hp_blocks = FOX_HEADS // 2

    def body(ks_ref, q_ref, do_ref, k_ref, v_ref, ck_ref, lse_ref, dl_ref, dq_ref, drow_ref, acc_ref, row_acc):
        hp, qb = pl.program_id(0), pl.program_id(1)
        q2, do2 = q_ref[...] * scale, do_ref[...]
        first = _head_pair_masks(t)
        zero = jnp.zeros_like(q2)
        qs = (jnp.where(first, q2, zero), jnp.where(first, zero, q2))
        dos = (jnp.where(first, do2, zero), jnp.where(first, zero, do2))
        lse, dl = lse_ref[...], dl_ref[...]
        lane = lax.broadcasted_iota(jnp.int32, (t, LANES), 1)
        lse_b = [jnp.broadcast_to(lse[:, j:j + 1], (t, LANES)) for j in range(2)]
        dl_b = [jnp.broadcast_to(jnp.sum(jnp.where(lane == 2 * hp + j, dl, 0.0), axis=1, keepdims=True), (t, LANES))
                for j in range(2)]
        acc_ref[...] = jnp.zeros_like(acc_ref)
        row_acc[...] = jnp.zeros_like(row_acc)

        def tile(kb, spans):
            off = pl.multiple_of(kb * t, t)
            k2 = k_ref[pl.ds(off, t), :]
            v2 = v_ref[pl.ds(off, t), :]
            ck = ck_ref[kb]
            for r0, r1, c0, c1, masked in spans:
                kk, vv = k2[c0:c1], v2[c0:c1]
                dqs = []
                for j in range(2):
                    sc = _dot(qs[j][r0:r1], kk, NT_DIMS) - ck[j:j + 1, c0:c1]
                    if masked:
                        sc = jnp.where(_span_mask(r0, r1, c0, c1), sc, -jnp.inf)
                    dp = _dot(dos[j][r0:r1], vv, NT_DIMS)
                    lb, db_ = lse_b[j][r0:r1], dl_b[j][r0:r1]
                    dsb = [jnp.exp(x - lb) * (d - db_) for x, d in zip(_lane_blocks(sc), _lane_blocks(dp))]
                    row_acc[j, r0:r1] += _sum_list(dsb)
                    dqs.append(_dot(jnp.concatenate(dsb, axis=1).astype(BF16), kk))
                acc_ref[r0:r1] += jnp.where(first[r0:r1], dqs[0], dqs[1])

        def step(kb, carry):
            tile(kb, _full_spans(t))
            return carry

        lax.fori_loop(ks_ref[hp, qb], qb, step, 0)
        tile(qb, _diagonal_spans(t))
        dq_ref[...] = (acc_ref[...] * scale).astype(dq_ref.dtype)
        for j in range(2):
            drow_ref[j:j + 1, :] = jnp.sum(row_acc[j].T, axis=0, keepdims=True)

    return _pcall(
        body, name=name, grid=(hp_blocks, nq), prefetch=1,
        in_specs=[pl.BlockSpec((t, LANES), lambda h, i, ks: (i, h)),
                  pl.BlockSpec((t, LANES), lambda h, i, ks: (i, h)),
                  pl.BlockSpec((s, LANES), lambda h, i, ks: (0, hp_blocks + h)),
                  pl.BlockSpec((s, LANES), lambda h, i, ks: (0, 2 * hp_blocks + h)),
                  pl.BlockSpec((None, nq, 2, t), lambda h, i, ks: (h, 0, 0, 0)),
                  pl.BlockSpec((None, t, 2), lambda h, i, ks: (h, i, 0)),
                  pl.BlockSpec((t, LANES), lambda h, i, ks: (i, 0))],
        out_specs=[pl.BlockSpec((t, LANES), lambda h, i, ks: (i, h)),
                   pl.BlockSpec((None, None, 2, t), lambda h, i, ks: (h, i, 0, 0))],
        out_shape=[_sds((s, hp_blocks * LANES), BF16), _sds((hp_blocks, nq, 2, t), F32)],
        scratch=[pltpu.VMEM((t, LANES), F32), pltpu.VMEM((2, t, LANES), F32)],
        sem=("parallel", "arbitrary"))(kb_start, qkv, do16, qkv, qkv, ck_rows, lse_c, delta)


def _flash_dkv(qkv, do16, cum, lse_rows, dl_rows, qb_end, name):
    s = qkv.shape[0]
    t = min(ATT_TILE, s)
    nq = s // t
    scale = HEAD_DIM ** -0.5
    hp_blocks = FOX_HEADS // 2

    def body(qe_ref, k_ref, v_ref, cum_ref, q_ref, do_ref, lse_ref, dl_ref, dk_ref, dv_ref, dck_ref,
             dk_acc, dv_acc, dck_acc):
        hp, kb = pl.program_id(0), pl.program_id(1)
        k2, v2 = k_ref[...] * scale, v_ref[...]
        first = _head_pair_masks(t)
        zero = jnp.zeros_like(k2)
        ks = (jnp.where(first, k2, zero), jnp.where(first, zero, k2))
        vs = (jnp.where(first, v2, zero), jnp.where(first, zero, v2))
        cumv = cum_ref[...]
        lane = lax.broadcasted_iota(jnp.int32, (t, LANES), 1)
        ck_b = [jnp.broadcast_to(jnp.sum(jnp.where(lane == 2 * hp + j, cumv, 0.0), axis=1, keepdims=True), (t, LANES))
                for j in range(2)]
        dk_acc[...] = jnp.zeros_like(dk_acc)
        dv_acc[...] = jnp.zeros_like(dv_acc)
        dck_acc[...] = jnp.zeros_like(dck_acc)

        def tile(qb, spans):
            off = pl.multiple_of(qb * t, t)
            q2 = q_ref[pl.ds(off, t), :]
            do2 = do_ref[pl.ds(off, t), :]
            lse, dl = lse_ref[qb], dl_ref[qb]
            for r0, r1, c0, c1, masked in spans:
                qq, dd, fr = q2[c0:c1], do2[c0:c1], first[r0:r1]
                dvs, dks = [], []
                for j in range(2):
                    sc = _dot(ks[j][r0:r1], qq, NT_DIMS)
                    if masked:
                        sc = jnp.where(_span_mask(r0, r1, c0, c1, transposed=True), sc, -jnp.inf)
                    dp = _dot(vs[j][r0:r1], dd, NT_DIMS) - dl[j:j + 1, c0:c1]
                    cb_ = ck_b[j][r0:r1]
                    pb = [jnp.exp((x - cb_) - l) for x, l in zip(_lane_blocks(sc), _lane_blocks(lse[j:j + 1, c0:c1]))]
                    dsb = [p * d for p, d in zip(pb, _lane_blocks(dp))]
                    dck_acc[j, r0:r1] += _sum_list(dsb)
                    dvs.append(_dot(jnp.concatenate(pb, axis=1).astype(BF16), dd))
                    dks.append(_dot(jnp.concatenate(dsb, axis=1).astype(BF16), qq))
                dv_acc[r0:r1] += jnp.where(fr, dvs[0], dvs[1])
                dk_acc[r0:r1] += jnp.where(fr, dks[0], dks[1])

        def step(qb, carry):
            tile(qb, _full_spans(t))
            return carry

        tile(kb, _diagonal_spans(t, transposed=True))
        lax.fori_loop(kb + 1, qe_ref[hp, kb] + 1, step, 0)
        dk_ref[...] = (dk_acc[...] * scale).astype(dk_ref.dtype)
        dv_ref[...] = dv_acc[...].astype(dv_ref.dtype)
        for j in range(2):
            dck_ref[j:j + 1, :] = -jnp.sum(dck_acc[j].T, axis=0, keepdims=True)

    return _pcall(
        body, name=name, grid=(hp_blocks, nq), prefetch=1,
        in_specs=[pl.BlockSpec((t, LANES), lambda h, j, qe: (j, hp_blocks + h)),
                  pl.BlockSpec((t, LANES), lambda h, j, qe: (j, 2 * hp_blocks + h)),
                  pl.BlockSpec((t, LANES), lambda h, j, qe: (j, 0)),
                  pl.BlockSpec((s, LANES), lambda h, j, qe: (0, h)),
                  pl.BlockSpec((s, LANES), lambda h, j, qe: (0, h)),
                  pl.BlockSpec((None, nq, 2, t), lambda h, j, qe: (h, 0, 0, 0)),
                  pl.BlockSpec((None, nq, 2, t), lambda h, j, qe: (h, 0, 0, 0))],
        out_specs=[pl.BlockSpec((t, LANES), lambda h, j, qe: (j, h)),
                   pl.BlockSpec((t, LANES), lambda h, j, qe: (j, h)),
                   pl.BlockSpec((None, None, 2, t), lambda h, j, qe: (h, j, 0, 0))],
        out_shape=[_sds((s, hp_blocks * LANES), BF16), _sds((s, hp_blocks * LANES), BF16),
                   _sds((hp_blocks, nq, 2, t), F32)],
        scratch=[pltpu.VMEM((t, LANES), F32), pltpu.VMEM((t, LANES), F32), pltpu.VMEM((2, t, LANES), F32)],
        sem=("parallel", "arbitrary"))(qb_end, qkv, qkv, cum, qkv, do16, lse_rows, dl_rows)


SKIP_NATS = 110.0


def _qk_norms(qkv, ind16, name):
    d = FOX_HEADS * HEAD_DIM

    def fn(tile, ind):
        q = tile[:, :d].astype(F32)
        k = tile[:, d:2 * d].astype(F32)
        return _dot_split(q * q, ind), _dot_split(k * k, ind)
    return _rowwise(fn, [qkv], [ind16], [(LANES, F32), (LANES, F32)], [], name=name)


def _skip_bounds(qn, kn, cum, t):
    s = qn.shape[0]
    nq = s // t
    hp = FOX_HEADS // 2
    scale = HEAD_DIM ** -0.5
    qmax = jnp.sqrt(jnp.max(qn.reshape(nq, t, FOX_HEADS), axis=1))
    kmax = jnp.sqrt(jnp.max(kn, axis=0))
    bound = qmax * kmax[None, :] * (scale * 1.01) + 1e-3
    gap = cum[0::t][:, None, :] - cum[t - 1::t][None, :, :]
    idx = jnp.arange(nq, dtype=jnp.int32)
    needed = (gap + 2.0 * bound[:, None, :]) > -SKIP_NATS
    needed = needed.reshape(nq, nq, hp, 2).any(axis=-1) & (idx[None, :] <= idx[:, None])[:, :, None]
    first = jnp.min(jnp.where(needed, idx[None, :, None], nq), axis=1)
    first = jnp.minimum(first, idx[:, None])
    start = lax.cummin(first, axis=0, reverse=True)
    uses = start[:, None, :] <= idx[None, :, None]
    last = jnp.max(jnp.where(uses, idx[:, None, None], 0), axis=0)
    last = jnp.maximum(last, idx[:, None])
    return start.T.astype(jnp.int32), last.T.astype(jnp.int32)


def _head_rowsum(a, b, ind16, name):
    s, d = a.shape
    tm = min(ROW_TILE, s)

    def body(a_ref, b_ref, ind_ref, o_ref, rows_ref):
        dsum = _dot_split(a_ref[...] * b_ref[...], ind_ref[...])
        o_ref[...] = dsum
        rows_ref[...] = dsum.T[0:FOX_HEADS, :]

    tile = pl.BlockSpec((tm, d), lambda i: (i, 0))
    return _pcall(body, name=name, grid=(s // tm,),
                  in_specs=[tile, tile, pl.BlockSpec((d, LANES), lambda i: (0, 0))],
                  out_specs=[pl.BlockSpec((tm, LANES), lambda i: (i, 0)), pl.BlockSpec((FOX_HEADS, tm), lambda i: (0, i))],
                  out_shape=[_sds((s, LANES), F32), _sds((FOX_HEADS, s), F32)], sem=("parallel",))(a, b, ind16)


def _rows_to_tiles(x, t):
    s = x.shape[1]
    return x.reshape(FOX_HEADS // 2, 2, s // t, t).transpose(0, 2, 1, 3)


def _tiles_to_cols(x):
    hp, nq, _, t = x.shape
    return jnp.pad(x.transpose(1, 3, 0, 2).reshape(nq * t, 2 * hp), ((0, 0), (0, LANES - 2 * hp)))


def _fox_forward(u, w):
    s = u.shape[0]
    t = min(ATT_TILE, s)
    qkv = _mm(u, w["fox_qkv"], "nn", [BF16], name="fox_qkv")
    fraw = _mm(u, w["fox_f"], "nn", [F32], name="fox_fproj")
    cum, cum_rows = _fox_gate(fraw, w["fox_bf"], "fox_gate")
    ck_rows = _rows_to_tiles(cum_rows, t)
    qn, kn = _qk_norms(qkv, w["head_ind"], "fox_qk_norms")
    kb_start, qb_end = _skip_bounds(qn[:, :FOX_HEADS], kn[:, :FOX_HEADS], cum[:, :FOX_HEADS], t)
    o, lse, lse_rows = _flash_fwd(qkv, ck_rows, kb_start, "fox_flash_fwd")
    y = _mm(o, w["fox_o"], "nn", [F32], name="fox_oproj")
    return y, dict(u=u, qkv=qkv, fraw=fraw, cum=cum, ck_rows=ck_rows, o=o, lse=lse, lse_rows=lse_rows,
                   kb_start=kb_start, qb_end=qb_end)


def _fox_backward(dy, sv, w):
    s = dy.shape[0]
    t = min(ATT_TILE, s)
    do32, do16 = _mm(dy, w["fox_o"], "nt", [F32, BF16], name="fox_do")
    g_wo = _mm(sv["o"], dy, "tn", [F32], name="fox_gwo")
    delta, delta_rows = _head_rowsum(do32, sv["o"], w["head_ind"], "fox_delta")
    dq, drow = _flash_dq(sv["qkv"], do16, sv["ck_rows"], sv["lse"], delta, sv["kb_start"], "fox_flash_dq")
    dk, dv, dck = _flash_dkv(sv["qkv"], do16, sv["cum"], sv["lse_rows"], _rows_to_tiles(delta_rows, t),
                             sv["qb_end"], "fox_flash_dkv")
    df, db_f = _fox_gate_bwd(_tiles_to_cols(drow), _tiles_to_cols(dck), sv["fraw"], w["fox_bf"], "fox_gate_bwd")
    du = _mm(df, w["fox_f"], "nt", [F32], name="fox_du_f")
    du = _mm_nt_blocks(_k_blocks(dq) + _k_blocks(dk) + _k_blocks(dv), w["fox_qkv"], du, name="fox_du")
    g_win = jnp.concatenate([_mm(sv["u"], dq, "tn", [F32], name="fox_gwin_q"),
                             _mm(sv["u"], dk, "tn", [F32], name="fox_gwin_k"),
                             _mm(sv["u"], dv, "tn", [F32], name="fox_gwin_v"),
                             _mm(sv["u"], df, "tn", [F32], name="fox_gwin_f")[:, :FOX_HEADS]], axis=1)
    return du, dict(fox_w_in=g_win, fox_w_o=g_wo, fox_b_f=db_f[:, :FOX_HEADS])


def _conv_fwd(xpre, w8, b, name):
    s, c = xpre.shape
    tm, tc = min(ROW_TILE, s), min(1024, c)
    hb = tm // 8

    def body(x_ref, h_ref, w_ref, b_ref, xc_ref, xa_ref):
        i = pl.program_id(1)
        x = x_ref[...]
        halo = jnp.where(i > 0, h_ref[...], 0.0)
        w = w_ref[...]
        row = lax.broadcasted_iota(jnp.int32, (tm, tc), 0)
        row8 = lax.broadcasted_iota(jnp.int32, (8, tc), 0)
        acc = x * w[3:4] + b_ref[...]
        x8 = x[0:8]
        acc8 = x8 * w[3:4] + b_ref[...]
        for j in range(1, SSM_CONV):
            acc = acc + w[3 - j:4 - j] * pltpu.roll(x, j, 0)
            acc8 = acc8 + w[3 - j:4 - j] * jnp.where(row8 < j, pltpu.roll(halo, j, 0), pltpu.roll(x8, j, 0))
        xc_ref[...] = acc
        xc_ref[0:8, :] = acc8
        xc = xc_ref[...]
        xa_ref[...] = _silu(xc)

    tile = pl.BlockSpec((tm, tc), lambda jc, i: (i, jc))
    return _pcall(body, name=name, grid=(c // tc, s // tm),
                  in_specs=[tile, pl.BlockSpec((8, tc), lambda jc, i: (jnp.maximum(i * hb - 1, 0), jc)),
                            pl.BlockSpec((8, tc), lambda jc, i: (0, jc)), pl.BlockSpec((1, tc), lambda jc, i: (0, jc))],
                  out_specs=[tile, tile], out_shape=[_sds((s, c), F32), _sds((s, c), F32)],
                  sem=("parallel", "arbitrary"))(xpre, xpre, w8, b)


def _conv_bwd(dxa, xc, xpre, w8, name):
    s, c = xpre.shape
    tm, tc = min(ROW_TILE, s), min(1024, c)
    hb = tm // 8
    n = s // tm

    def body(d_ref, xc_ref, x_ref, xh_ref, dn_ref, xcn_ref, w_ref, dx_ref, dw_ref, db_ref, scr):
        i = pl.program_id(1)

        @pl.when(i == 0)
        def _():
            dw_ref[...] = jnp.zeros_like(dw_ref)
            db_ref[...] = jnp.zeros_like(db_ref)
        w = w_ref[...]
        x = x_ref[...]
        g = d_ref[...] * _dsilu(xc_ref[...])
        gn = jnp.where(i < n - 1, dn_ref[...] * _dsilu(xcn_ref[...]), 0.0)
        halo = jnp.where(i > 0, xh_ref[...], 0.0)
        row = lax.broadcasted_iota(jnp.int32, (tm, tc), 0)
        row8 = lax.broadcasted_iota(jnp.int32, (8, tc), 0)
        db_ref[...] += _colsum(g)
        dw_ref[3:4, :] += _colsum(g * x)
        g8 = g[0:8]
        acc = g * w[3:4]
        corr = jnp.zeros((8, tc), F32)
        for j in range(1, SSM_CONV):
            xs = pltpu.roll(x, j, 0)
            dwj = _colsum(jnp.where(row >= j, g * xs, 0.0))
            dwj = dwj + _colsum(jnp.where(row8 < j, g8 * pltpu.roll(halo, j, 0), 0.0))
            dw_ref[3 - j:4 - j, :] += dwj
            gs = pltpu.roll(g, tm - j, 0)
            acc = acc + w[3 - j:4 - j] * jnp.where(row < tm - j, gs, 0.0)
            corr = corr + w[3 - j:4 - j] * jnp.where(row8 >= 8 - j, pltpu.roll(gn, 8 - j, 0), 0.0)
        scr[...] = acc
        scr[tm - 8:tm, :] += corr
        dx_ref[...] = scr[...].astype(dx_ref.dtype)

    tile = pl.BlockSpec((tm, tc), lambda jc, i: (i, jc))
    prev8 = pl.BlockSpec((8, tc), lambda jc, i: (jnp.maximum(i * hb - 1, 0), jc))
    next8 = pl.BlockSpec((8, tc), lambda jc, i: (jnp.minimum((i + 1) * hb, n * hb - 1), jc))
    return _pcall(body, name=name, grid=(c // tc, n),
                  in_specs=[tile, tile, tile, prev8, next8, next8, pl.BlockSpec((8, tc), lambda jc, i: (0, jc))],
                  out_specs=[tile, pl.BlockSpec((8, tc), lambda jc, i: (0, jc)), pl.BlockSpec((1, tc), lambda jc, i: (0, jc))],
                  out_shape=[_sds((s, c), BF16), _sds((8, c), F32), _sds((1, c), F32)],
                  scratch=[pltpu.VMEM((tm, tc), F32)],
                  sem=("parallel", "arbitrary"))(dxa, xc, xpre, xpre, dxa, xc, w8)


def _ssd_pre(dtraw, dt_bias, a_log, cst, name):
    def fn(raw, bias, alog, expand):
        tm = raw.shape[0]
        z = raw + bias
        dt = jnp.maximum(z, 0.0) + jnp.log(1.0 + jnp.exp(-jnp.abs(z)))
        lane = lax.broadcasted_iota(jnp.int32, (tm, LANES), 1)
        pos = lax.broadcasted_iota(jnp.int32, (tm, LANES), 0) & (SSM_CHUNK - 1)
        dt = jnp.where(lane < SSM_HEADS, dt, 0.0)
        c = dt * (-jnp.exp(alog))
        sh = 1
        while sh < SSM_CHUNK:
            c = c + jnp.where(pos >= sh, pltpu.roll(c, sh, 0), 0.0)
            sh *= 2
        return dt, c, _dot_split(dt, expand), _dot_split(c, expand)
    wide = SSM_HEADS * HEAD_DIM
    return _rowwise(fn, [dtraw], [dt_bias, a_log, cst["expand"]],
                    [(LANES, F32), (LANES, F32), (wide, F32), (wide, F32)], [], name=name)


def _ssd_post(dacs, ddt, dtraw, dt, dt_bias, a_log, name):
    def fn(dacs_v, ddt_v, raw, dt_v, bias, alog):
        tm = raw.shape[0]
        pos = lax.broadcasted_iota(jnp.int32, (tm, LANES), 0) & (SSM_CHUNK - 1)
        a = -jnp.exp(alog)
        c = dacs_v
        sh = 1
        while sh < SSM_CHUNK:
            c = c + jnp.where(pos + sh < SSM_CHUNK, pltpu.roll(c, tm - sh, 0), 0.0)
            sh *= 2
        draw = (ddt_v + c * a) * _sigmoid(raw + bias)
        return draw, _colsum(draw), _colsum(c * dt_v * a)
    return _rowwise(fn, [dacs, ddt, dtraw, dt], [dt_bias, a_log], [(LANES, BF16)], [(1, LANES)] * 2, name=name)


def _heads_rows(x, s):
    return x[:, :SSM_HEADS].reshape(s // SSM_CHUNK, SSM_CHUNK, SSM_GROUPS, 4).transpose(2, 0, 3, 1)


def _ssd_constants():
    gp = SSD_GROUPS_PER_STEP
    src = np.arange(LANES)[:, None]
    expand = src == np.arange(SSM_HEADS * HEAD_DIM)[None, :] // HEAD_DIM
    dst = np.arange(LANES)[None, None, :] - 4 * np.arange(gp)[:, None, None]
    seg = np.arange(SSM_GROUP_WIDTH)[None, :, None] // HEAD_DIM == dst
    seg4 = np.arange(4 * LANES)[None, :, None] // LANES == dst
    return dict(expand=jnp.asarray(expand, BF16), seg=jnp.asarray(seg, BF16), seg4=jnp.asarray(seg4, BF16))


def _ssm_weights(w_in, conv_w, conv_b, dt_bias, a_log, d_skip, norm_w, w_out):
    pad = ((0, 0), (0, LANES - SSM_HEADS))
    w_xbc = _group_cols(w_in[:, 2048:6144])
    w_dt = jnp.pad(w_in[:, 6144:], pad)
    return dict(
        ssm_z=w_in[:, :2048], ssm_xbc=w_xbc, ssm_dt=w_dt,
        ssm_zx=jnp.concatenate([w_in[:, :2048], w_xbc], axis=1),
        ssm_out=w_out,
        conv_w8=_group_cols(jnp.pad(conv_w, ((0, 8 - SSM_CONV), (0, 0)))),
        conv_b=_group_cols(conv_b), norm_w=norm_w,
        dt_bias=jnp.pad(dt_bias, pad), a_log=jnp.pad(a_log, pad),
        d_e=jnp.repeat(d_skip.reshape(SSM_GROUPS, 4), HEAD_DIM, axis=1)[:, None, :],
        ssd_cst=_ssd_constants())


def _ssd_setup(acs_e, acsr):
    l = SSM_CHUNK
    last = acsr[:, l - 1:l]
    lane1 = lax.broadcasted_iota(jnp.int32, (1, SSM_GROUP_WIDTH), 1)
    last_e = last[3:4, :]
    for r in (2, 1, 0):
        last_e = jnp.where(lane1 < HEAD_DIM * (r + 1), last[r:r + 1, :], last_e)
    return jnp.exp(acs_e), jnp.exp(last_e - acs_e), jnp.exp(last_e)


def _head_bcast(acs_e):
    lo = lax.broadcasted_iota(jnp.int32, (acs_e.shape[0], LANES), 1) < HEAD_DIM
    out = []
    for p in range(2):
        blk = acs_e[:, p * LANES:(p + 1) * LANES]
        rolled = pltpu.roll(blk, HEAD_DIM, 1)
        out += [jnp.where(lo, blk, rolled), jnp.where(lo, rolled, blk)]
    return out


def _group_cols(a):
    lead = a.shape[:-1]
    x = a[..., :2048].reshape(lead + (SSM_GROUPS, SSM_GROUP_WIDTH))
    b = a[..., 2048:3072].reshape(lead + (SSM_GROUPS, SSM_STATE))
    c = a[..., 3072:].reshape(lead + (SSM_GROUPS, SSM_STATE))
    return jnp.concatenate([x, b, c], axis=-1).reshape(lead + (4096,))


def _ungroup_cols(a):
    lead = a.shape[:-1]
    y = a.reshape(lead + (SSM_GROUPS, SSM_GROUP_WIDTH + 2 * SSM_STATE))
    return jnp.concatenate([y[..., :256].reshape(lead + (2048,)), y[..., 256:384].reshape(lead + (1024,)),
                            y[..., 384:].reshape(lead + (1024,))], axis=-1)


def _ssd_fwd2(xa, dte, acse, acsr, d_e, name):
    s = xa.shape[0]
    l, gw, ns = SSM_CHUNK, SSM_GROUP_WIDTH, SSM_STATE
    nc = s // l

    gb = gw + 2 * ns
    gp = SSD_GROUPS_PER_STEP

    def body(xa_ref, dt_ref, acs_ref, acsr_ref, d_ref, y_ref, hp_ref, h_sc):
        @pl.when(pl.program_id(1) == 0)
        def _():
            h_sc[...] = jnp.zeros_like(h_sc)
        lane = lax.broadcasted_iota(jnp.int32, (l, gw), 1)
        tril = _causal(l)
        for gi in range(gp):
            x = xa_ref[:, gi * gb:gi * gb + gw]
            bm = xa_ref[:, gi * gb + gw:gi * gb + gw + ns].astype(BF16)
            cm = xa_ref[:, gi * gb + gw + ns:(gi + 1) * gb].astype(BF16)
            acsr = acsr_ref[gi]
            dt_e, acs_e = dt_ref[:, gi * gw:(gi + 1) * gw], acs_ref[:, gi * gw:(gi + 1) * gw]
            acs_bc = _head_bcast(acs_e)
            e_e, dte_e, cd_e = _ssd_setup(acs_e, acsr)
            xdt = x * dt_e
            xdt16 = xdt.astype(BF16)
            cb = _dot(cm, bm, NT_DIMS)
            yd = jnp.zeros((l, gw), F32)
            for r in range(4):
                lm = jnp.exp(jnp.where(tril, acs_bc[r] - acsr[r:r + 1, :], -jnp.inf))
                yr = _dot((cb * lm).astype(BF16), xdt16)
                yd = jnp.where((lane >= HEAD_DIM * r) & (lane < HEAD_DIM * (r + 1)), yr, yd)
            hp = h_sc[gi]
            hp_ref[gi] = hp
            y_ref[:, gi * gw:(gi + 1) * gw] = yd + _dot(cm, hp.astype(BF16)) * e_e + x * d_ref[gi]
            h_sc[gi] = hp * cd_e + _dot(bm, (xdt * dte_e).astype(BF16), TN_DIMS)

    return _pcall(
        body, name=name, grid=(SSM_GROUPS // gp, nc),
        in_specs=[pl.BlockSpec((l, gp * gb), lambda g, c: (c, g)),
                  pl.BlockSpec((l, gp * gw), lambda g, c: (c, g)),
                  pl.BlockSpec((l, gp * gw), lambda g, c: (c, g)),
                  pl.BlockSpec((gp, None, 4, l), lambda g, c: (g, c, 0, 0)),
                  pl.BlockSpec((gp, 1, gw), lambda g, c: (g, 0, 0))],
        out_specs=[pl.BlockSpec((l, gp * gw), lambda g, c: (c, g)),
                   pl.BlockSpec((gp, None, ns, gw), lambda g, c: (g, c, 0, 0))],
        out_shape=[_sds((s, 2048), F32), _sds((SSM_GROUPS, nc, ns, gw), F32)],
        scratch=[pltpu.VMEM((gp, ns, gw), F32)],
        sem=("parallel", "arbitrary"))(xa, dte, acse, acsr, d_e)


def _ssd_bwd2(dy, xa, dte, acse, acsr, d_e, hprev, cst, name):
    s = xa.shape[0]
    l, gw, ns = SSM_CHUNK, SSM_GROUP_WIDTH, SSM_STATE
    nc = s // l

    gb = gw + 2 * ns
    gp = SSD_GROUPS_PER_STEP

    def body(dy_ref, xa_ref, dt_ref, acs_ref, acsr_ref, d_ref, hp_ref,
             seg_ref, seg4_ref, dxa_ref, dacs_ref, ddt_ref, dd_ref, dh_sc):
        @pl.when(pl.program_id(1) == 0)
        def _():
            dh_sc[...] = jnp.zeros_like(dh_sc)
            dd_ref[...] = jnp.zeros_like(dd_ref)
        parts = [one_group(gi, dy_ref, xa_ref, dt_ref, acs_ref, acsr_ref, d_ref, hp_ref, seg_ref, seg4_ref,
                           dxa_ref, dh_sc) for gi in range(gp)]
        dacs_ref[...] = _sum_list([p[0] for p in parts])
        ddt_ref[...] = _sum_list([p[1] for p in parts])
        dd_ref[0:1, :] += _sum_list([p[2] for p in parts])

    def one_group(gi, dy_ref, xa_ref, dt_ref, acs_ref, acsr_ref, d_ref, hp_ref, seg_ref, seg4_ref, dxa_ref, dh_sc):
        dyv = dy_ref[:, gi * gw:(gi + 1) * gw]
        x = xa_ref[:, gi * gb:gi * gb + gw]
        bm = xa_ref[:, gi * gb + gw:gi * gb + gw + ns].astype(BF16)
        cm = xa_ref[:, gi * gb + gw + ns:(gi + 1) * gb].astype(BF16)
        acsr = acsr_ref[gi]
        dt_e, acs_e = dt_ref[:, gi * gw:(gi + 1) * gw], acs_ref[:, gi * gw:(gi + 1) * gw]
        acs_bc = _head_bcast(acs_e)
        e_e, dte_e, cd_e = _ssd_setup(acs_e, acsr)
        seg, seg4 = seg_ref[gi], seg4_ref[gi]
        lane = lax.broadcasted_iota(jnp.int32, (l, gw), 1)
        xdt = x * dt_e
        xdt16 = xdt.astype(BF16)
        dy16 = dyv.astype(BF16)
        cb = _dot(cm, bm, NT_DIMS)
        cbt = _dot(bm, cm, NT_DIMS)
        hp = hp_ref[gi]
        hp16 = hp.astype(BF16)
        g = dh_sc[gi]
        g16 = g.astype(BF16)
        t_all = _dot(cm, hp16)
        dt16 = (dyv * e_e).astype(BF16)
        dc = _dot(dt16, hp16, NT_DIMS)
        dhp = _dot(cm, dt16, TN_DIMS)
        wv = xdt * dte_e
        dw = _dot(bm, g16)
        db = _dot(wv.astype(BF16), g16, NT_DIMS)
        dxdt = dw * dte_e
        acs_term = dyv * t_all * e_e - dw * wv
        last_term = _colsum(dw * wv) + _colsum(g * hp) * cd_e
        dh_sc[gi] = g * cd_e + dhp
        tril, triu = _causal(l), _causal(l, transposed=True)
        dcb = jnp.zeros((l, l), F32)
        dcbt = jnp.zeros((l, l), F32)
        qd = []
        for r in range(4):
            in_head = (lane >= HEAD_DIM * r) & (lane < HEAD_DIM * (r + 1))
            a_col = acs_bc[r]
            lm = jnp.exp(jnp.where(tril, a_col - acsr[r:r + 1, :], -jnp.inf))
            lmt = jnp.exp(jnp.where(triu, acsr[r:r + 1, :] - a_col, -jnp.inf))
            mm_, mt = cb * lm, cbt * lmt
            dyr = jnp.where(in_head, dy16, jnp.zeros_like(dy16))
            dm = _dot(dyr, xdt16, NT_DIMS)
            dmt = _dot(xdt16, dyr, NT_DIMS)
            dxdt = dxdt + jnp.where(in_head, _dot(mt.astype(BF16), dy16), 0.0)
            dcb = dcb + dm * lm
            dcbt = dcbt + dmt * lmt
            qd.append(dm * mm_ - dmt * mt)
        dc = dc + _dot(dcb.astype(BF16), bm)
        db = db + _dot(dcbt.astype(BF16), cm)
        rowl = lax.broadcasted_iota(jnp.int32, (l, LANES), 0)
        row8 = lax.broadcasted_iota(jnp.int32, (8, gw), 0)
        small = _dot_split(jnp.where(row8 == 0, last_term, jnp.where(row8 == 1, _colsum(dyv * x), 0.0)), seg, passes=2)
        big = _dot_split(jnp.concatenate([acs_term, dxdt * x], axis=0), seg, passes=2)
        dacs = (big[0:l] + _dot_split(jnp.concatenate(qd, axis=1), seg4, passes=2)
                + jnp.where(rowl == l - 1, small[0:1, :], 0.0))
        dxa_ref[:, gi * gb:(gi + 1) * gb] = jnp.concatenate([dxdt * dt_e + dyv * d_ref[gi], db, dc], axis=1)
        return dacs, big[l:2 * l], small[1:2, :]

    rc = lambda c: nc - 1 - c
    ng = SSM_GROUPS // gp
    return _pcall(
        body, name=name, grid=(ng, nc),
        in_specs=[pl.BlockSpec((l, gp * gw), lambda g, c: (rc(c), g)),
                  pl.BlockSpec((l, gp * gb), lambda g, c: (rc(c), g)),
                  pl.BlockSpec((l, gp * gw), lambda g, c: (rc(c), g)),
                  pl.BlockSpec((l, gp * gw), lambda g, c: (rc(c), g)),
                  pl.BlockSpec((gp, None, 4, l), lambda g, c: (g, rc(c), 0, 0)),
                  pl.BlockSpec((gp, 1, gw), lambda g, c: (g, 0, 0)),
                  pl.BlockSpec((gp, None, ns, gw), lambda g, c: (g, rc(c), 0, 0)),
                  pl.BlockSpec((gp, gw, LANES), lambda g, c: (0, 0, 0)),
                  pl.BlockSpec((gp, 4 * LANES, LANES), lambda g, c: (0, 0, 0))],
        out_specs=[pl.BlockSpec((l, gp * gb), lambda g, c: (rc(c), g)),
                   pl.BlockSpec((None, l, LANES), lambda g, c: (g, rc(c), 0)),
                   pl.BlockSpec((None, l, LANES), lambda g, c: (g, rc(c), 0)),
                   pl.BlockSpec((None, 8, LANES), lambda g, c: (g, 0, 0))],
        out_shape=[_sds((s, 4096), F32), _sds((ng, s, LANES), F32), _sds((ng, s, LANES), F32),
                   _sds((ng, 8, LANES), F32)],
        scratch=[pltpu.VMEM((gp, ns, gw), F32)],
        sem=("parallel", "arbitrary"))(dy, xa, dte, acse, acsr, d_e, hprev, cst["seg"], cst["seg4"])


def _gate_norm(y, z, nw, name):
    c = y.shape[1]

    def fn(yv, zv, w):
        outs = []
        for k in range(c // SSM_GROUP_WIDTH):
            sl = slice(k * SSM_GROUP_WIDTH, (k + 1) * SSM_GROUP_WIDTH)
            yg = yv[:, sl] * _silu(zv[:, sl])
            rinv = lax.rsqrt(jnp.mean(yg * yg, axis=-1, keepdims=True) + RMS_EPS)
            outs.append(yg * rinv * w[:, sl])
        return (jnp.concatenate(outs, axis=1),)
    return _rowwise(fn, [y, z], [nw], [(c, BF16)], [], name=name)[0]


def _out_gate_norm_bwd(dy, w_out, y, z, nw, name):
    s, c = y.shape
    tm, tn = min(512, s), min(1024, c)
    k = dy.shape[1]

    def body(a_ref, b_ref, y_ref, z_ref, w_ref, dy_ref, dz_ref, dw_ref):
        @pl.when(pl.program_id(1) == 0)
        def _():
            dw_ref[...] = jnp.zeros_like(dw_ref)
        dv = _dot(a_ref[...], b_ref[...], NT_DIMS)
        yv, zv, w = y_ref[...], z_ref[...], w_ref[...]
        dys, dzs, dws = [], [], []
        for g in range(tn // SSM_GROUP_WIDTH):
            sl = slice(g * SSM_GROUP_WIDTH, (g + 1) * SSM_GROUP_WIDTH)
            ys, zs, ds = yv[:, sl], zv[:, sl], dv[:, sl]
            sz = _silu(zs)
            yg = ys * sz
            rinv = lax.rsqrt(jnp.mean(yg * yg, axis=-1, keepdims=True) + RMS_EPS)
            nrm = yg * rinv
            dn = ds * w[:, sl]
            dyg = rinv * (dn - nrm * jnp.mean(dn * nrm, axis=-1, keepdims=True))
            dys.append(dyg * sz)
            dzs.append(dyg * ys * _dsilu(zs))
            dws.append(_colsum(ds * nrm))
        dy_ref[...] = jnp.concatenate(dys, axis=1)
        dz_ref[...] = jnp.concatenate(dzs, axis=1).astype(dz_ref.dtype)
        dw_ref[...] += jnp.concatenate(dws, axis=1)

    tile = pl.BlockSpec((tm, tn), lambda j, i: (i, j))
    row = pl.BlockSpec((1, tn), lambda j, i: (0, j))
    return _pcall(body, name=name, grid=(c // tn, s // tm),
                  in_specs=[pl.BlockSpec((tm, k), lambda j, i: (i, 0)), pl.BlockSpec((tn, k), lambda j, i: (j, 0)),
                            tile, tile, row],
                  out_specs=[tile, tile, row], out_shape=[_sds((s, c), F32), _sds((s, c), BF16), _sds((1, c), F32)],
                  sem=("parallel", "arbitrary"))(dy, w_out, y, z, nw)


def _ssd_forward(u, w):
    s = u.shape[0]
    z = _mm(u, w["ssm_z"], "nn", [F32], name="ssm_zproj")
    xpre = _mm(u, w["ssm_xbc"], "nn", [F32], name="ssm_xproj")
    dtraw = _mm(u, w["ssm_dt"], "nn", [F32], name="ssm_dtproj")
    xc, xa = _conv_fwd(xpre, w["conv_w8"], w["conv_b"], "ssm_conv")
    dt, acs, dte, acse = _ssd_pre(dtraw, w["dt_bias"], w["a_log"], w["ssd_cst"], "ssm_pre")
    acsr = _heads_rows(acs, s)
    y, hprev = _ssd_fwd2(xa, dte, acse, acsr, w["d_e"], "ssm_scan")
    yn = _gate_norm(y, z, w["norm_w"], "ssm_gate_norm")
    out = _mm(yn, w["ssm_out"], "nn", [F32], name="ssm_oproj")
    return out, dict(u=u, z=z, xpre=xpre, xc=xc, xa=xa, dtraw=dtraw, dt=dt, dte=dte, acse=acse,
                     acsr=acsr, y=y, hprev=hprev, yn=yn)


def _ssd_backward(dy, sv, w):
    s = dy.shape[0]
    g_wout = _mm(sv["yn"], dy, "tn", [F32], name="ssm_gwout")
    dys, dz, dnw = _out_gate_norm_bwd(dy, w["ssm_out"], sv["y"], sv["z"], w["norm_w"], "ssm_dyn_gate_norm_bwd")
    dxa, dacs_c, ddt_c, dd = _ssd_bwd2(dys, sv["xa"], sv["dte"], sv["acse"], sv["acsr"], w["d_e"], sv["hprev"],
                                       w["ssd_cst"], "ssm_scan_bwd")
    per_step = 4 * SSD_GROUPS_PER_STEP
    pad = ((0, 0), (0, LANES - SSM_HEADS))
    dacs = jnp.pad(jnp.concatenate([a[:, :per_step] for a in dacs_c], axis=1), pad)
    ddt = jnp.pad(jnp.concatenate([a[:, :per_step] for a in ddt_c], axis=1), pad)
    draw, dbias, dalog = _ssd_post(dacs, ddt, sv["dtraw"], sv["dt"], w["dt_bias"], w["a_log"], "ssm_post")
    dxpre, dcw, dcb = _conv_bwd(dxa, sv["xc"], sv["xpre"], w["conv_w8"], "ssm_conv_bwd")
    du = _mm(draw, w["ssm_dt"], "nt", [F32], name="ssm_du_dt")
    n_z = dz.shape[1]
    du = _mm_nt_blocks(_k_blocks(dz), w["ssm_zx"][:, :n_z], du, name="ssm_du_z")
    du = _mm_nt_blocks(_k_blocks(dxpre), w["ssm_zx"][:, n_z:], du, name="ssm_du_x")
    g_win = jnp.concatenate([_mm(sv["u"], dz, "tn", [F32], name="ssm_gwin_z"),
                             _ungroup_cols(_mm(sv["u"], dxpre, "tn", [F32], name="ssm_gwin_x")),
                             _mm(sv["u"], draw, "tn", [F32], name="ssm_gwin_dt")[:, :SSM_HEADS]], axis=1)
    return du, dict(ssm_w_in=g_win, ssm_w_out=g_wout, ssm_conv_w=_ungroup_cols(dcw[:SSM_CONV]),
                    ssm_conv_b=_ungroup_cols(dcb), ssm_norm_w=dnw, ssm_dt_bias=dbias[:, :SSM_HEADS],
                    ssm_a_log=dalog[:, :SSM_HEADS], ssm_d=dd[:, 0, :per_step].reshape(1, SSM_HEADS))


def _mlp_forward(u2, w1, w2, tag):
    def epi(acc):
        hr = jnp.maximum(acc, 0.0)
        return hr, hr * hr
    hr, a = _mm(u2, w1, "nn", [BF16, BF16], name=tag + "_mlp_up", epi=epi)
    y2 = _mm(a, w2, "nn", [F32], name=tag + "_mlp_down")
    return y2, hr, a


def _mlp_backward(dy2, u2, hr, a, w1, w2, tag):
    dh = _mm(dy2, w2, "nt", [BF16], name=tag + "_mlp_dh", extra=(hr,),
             epi=lambda acc, h: (acc * (2.0 * h.astype(F32)),))
    g_w2 = _mm(a, dy2, "tn", [F32], name=tag + "_mlp_gw2")
    g_w1 = _mm(u2, dh, "tn", [F32], name=tag + "_mlp_gw1")
    du2 = _mm(dh, w1, "nt", [F32], name=tag + "_mlp_du")
    return du2, g_w1, g_w2


def _ada_forward(c16, ada_w, ada_b_cols, name):
    nl, d, cols = ada_w.shape
    tn = 512

    def body(c_ref, w_ref, b_ref, o_ref):
        cond = _silu(c_ref[...]).astype(BF16)
        o_ref[...] = _dot(cond, w_ref[...].astype(BF16)) + b_ref[...]

    return _pcall(body, name=name, grid=(nl, cols // tn),
                  in_specs=[pl.BlockSpec((16, d), lambda i, j: (0, 0)),
                            pl.BlockSpec((None, d, tn), lambda i, j: (i, 0, j)),
                            pl.BlockSpec((None, 1, tn), lambda i, j: (i, 0, j))],
                  out_specs=pl.BlockSpec((None, 16, tn), lambda i, j: (i, 0, j)),
                  out_shape=_sds((nl, 16, cols), F32), sem=("parallel", "parallel"))(c16, ada_w, ada_b_cols)


def _ada_backward(c_t, dmod_cols, name):
    d, nb = c_t.shape
    nl, _, cols = dmod_cols.shape
    tn = 512

    def body(c_ref, dm_ref, o_ref):
        cond = _silu(c_ref[...])
        dm = dm_ref[...]
        acc = cond[:, 0:1] * dm[0:1, :]
        for b in range(1, nb):
            acc = acc + cond[:, b:b + 1] * dm[b:b + 1, :]
        o_ref[...] = acc

    return _pcall(body, name=name, grid=(nl, cols // tn),
                  in_specs=[pl.BlockSpec((d, nb), lambda i, j: (0, 0)),
                            pl.BlockSpec((None, nb, tn), lambda i, j: (i, 0, j))],
                  out_specs=pl.BlockSpec((None, d, tn), lambda i, j: (i, 0, j)),
                  out_shape=_sds((nl, d, cols), F32), sem=("parallel", "parallel"))(c_t, dmod_cols)


def _adamw(w, g, m, v, name):
    rows, cols = w.shape
    tm = rows
    for cand in (256, 128, 64, 32, 16, 8):
        if rows % cand == 0 and rows > cand:
            tm = cand
            break
    c1 = 1.0 / (1.0 - ADAM_B1 ** ADAM_STEP)
    c2 = 1.0 / (1.0 - ADAM_B2 ** ADAM_STEP)

    def fn(wv, gv, mv, vv):
        mn = ADAM_B1 * mv + (1.0 - ADAM_B1) * gv
        vn = ADAM_B2 * vv + (1.0 - ADAM_B2) * (gv * gv)
        delta = -ADAM_LR * ((mn * c1) / (jnp.sqrt(vn * c2) + ADAM_EPS) + ADAM_WD * wv)
        return delta, mn, vn
    return _rowwise(fn, [w, g, m, v], [], [(cols, F32)] * 3, [], name=name, tm=tm)


def _my_pos():
    return lax.axis_index("x"), lax.axis_index("y"), lax.axis_index("c")


def _allgather8(x, name):
    r, c = x.shape

    def body(x_ref, out_ref, send_sems, recv_sems, local_sem):
        mx, my, mc = _my_pos()
        me = 4 * mx + 2 * my + mc
        mine = pltpu.make_async_copy(x_ref, out_ref.at[me], local_sem)
        mine.start()
        copies = []
        for k in range(1, 8):
            fx, fy, fc = (k >> 2) & 1, (k >> 1) & 1, k & 1
            px = 1 - mx if fx else mx
            py = 1 - my if fy else my
            pc = 1 - mc if fc else mc
            peer = 4 * px + 2 * py + pc
            send = pltpu.make_async_remote_copy(src_ref=x_ref, dst_ref=out_ref.at[me], send_sem=send_sems.at[k - 1],
                                                recv_sem=recv_sems.at[k - 1], device_id=(px, py, pc),
                                                device_id_type=MESH)
            send.start()
            recv = pltpu.make_async_remote_copy(src_ref=x_ref, dst_ref=out_ref.at[peer], send_sem=send_sems.at[k - 1],
                                                recv_sem=recv_sems.at[k - 1], device_id=(px, py, pc),
                                                device_id_type=MESH)
            copies.append((send, recv))
        for send, recv in copies:
            recv.wait_recv()
        for send, recv in copies:
            send.wait_send()
        mine.wait()

    vm = pl.BlockSpec(memory_space=pltpu.VMEM)
    return _pcall(body, name=name, in_specs=[vm], out_specs=vm, out_shape=_sds((8, r, c), x.dtype),
                  scratch=[pltpu.SemaphoreType.DMA((7,)), pltpu.SemaphoreType.DMA((7,)), pltpu.SemaphoreType.DMA])(x)


def _chip_flips(mx, my):
    out = []
    for fx, fy in ((1, 0), (0, 1), (1, 1)):
        px = 1 - mx if fx else mx
        py = 1 - my if fy else my
        out.append((px, py, 2 * px + py))
    return out


def _gather_chips(shard2, name):
    _, h, c = shard2.shape

    def body(x_ref, out_ref, send_sems, recv_sems):
        mx, my, mc = _my_pos()
        oc = 1 - mc
        mk = 2 * mx + my
        flips = _chip_flips(mx, my)

        def copy(k, src, dst, to):
            return pltpu.make_async_remote_copy(src_ref=src, dst_ref=dst, send_sem=send_sems.at[k],
                                                recv_sem=recv_sems.at[k], device_id=to, device_id_type=MESH)

        first = [copy(j, x_ref.at[mc], out_ref.at[mk, mc], (px, py, mc)) for j, (px, py, pk) in enumerate(flips)]
        for cp in first:
            cp.start()
        passed = []
        for j, (px, py, pk) in enumerate(flips):
            copy(j, x_ref.at[mc], out_ref.at[pk, mc], (px, py, mc)).wait_recv()
            fw = copy(3 + j, out_ref.at[pk, mc], out_ref.at[pk, mc], (mx, my, oc))
            fw.start()
            passed.append(fw)
        for j, (px, py, pk) in enumerate(flips):
            copy(3 + j, out_ref.at[pk, oc], out_ref.at[pk, oc], (mx, my, oc)).wait_recv()
        for cp in first + passed:
            cp.wait_send()

    return _pcall(body, name=name, in_specs=[HBM_SPEC], out_specs=HBM_SPEC, out_shape=_sds((4, 2, h, c), shard2.dtype),
                  scratch=[pltpu.SemaphoreType.DMA((6,)), pltpu.SemaphoreType.DMA((6,))])(shard2)


def _pair_exchange(g4, name):
    n, _, h, c = g4.shape

    def body(g_ref, out_ref, send_sem, recv_sem):
        mx, my, mc = _my_pos()
        oc = 1 - mc
        copies = []
        for k in range(n):
            cp = pltpu.make_async_remote_copy(src_ref=g_ref.at[k, oc], dst_ref=out_ref.at[k], send_sem=send_sem.at[k],
                                              recv_sem=recv_sem.at[k], device_id=(mx, my, oc), device_id_type=MESH)
            cp.start()
            copies.append(cp)
        for cp in copies:
            cp.wait_recv()
        for cp in copies:
            cp.wait_send()

    return _pcall(body, name=name, in_specs=[HBM_SPEC], out_specs=HBM_SPEC, out_shape=_sds((n, h, c), g4.dtype),
                  scratch=[pltpu.SemaphoreType.DMA((n,)), pltpu.SemaphoreType.DMA((n,))])(g4)


def _pair_add(g4, recv, core, name):
    n, _, h, c = g4.shape
    tm = _row_tile(h)

    def body(core_ref, a_ref, b_ref, o_ref, o16_ref):
        acc = a_ref[...] + b_ref[...]
        o_ref[...] = acc
        o16_ref[...] = acc.astype(BF16)

    out_spec = pl.BlockSpec((None, tm, c), lambda k, i, cr: (k, i, 0))
    return _pcall(body, name=name, grid=(n, h // tm), prefetch=1,
                  in_specs=[pl.BlockSpec((None, None, tm, c), lambda k, i, cr: (k, cr[0], i, 0)), out_spec],
                  out_specs=[out_spec, out_spec], out_shape=[_sds((n, h, c), F32), _sds((n, h, c), BF16)],
                  sem=("parallel", "parallel"))(core, g4, recv)


def _chip_exchange(p, name):
    n, h, c = p.shape

    def body(p_ref, out_ref, send_sems, recv_sems):
        mx, my, mc = _my_pos()
        copies = []
        for j, (px, py, pk) in enumerate(_chip_flips(mx, my)):
            cp = pltpu.make_async_remote_copy(src_ref=p_ref.at[pk], dst_ref=out_ref.at[j], send_sem=send_sems.at[j],
                                              recv_sem=recv_sems.at[j], device_id=(px, py, mc), device_id_type=MESH)
            cp.start()
            copies.append(cp)
        for cp in copies:
            cp.wait_recv()
        for cp in copies:
            cp.wait_send()

    return _pcall(body, name=name, in_specs=[HBM_SPEC], out_specs=HBM_SPEC, out_shape=_sds((3, h, c), p.dtype),
                  scratch=[pltpu.SemaphoreType.DMA((3,)), pltpu.SemaphoreType.DMA((3,))])(p)


def _chip_sum(p, slots, chip, name):
    _, h, c = p.shape
    tm = _row_tile(h)

    def body(chip_ref, p_ref, q_ref, o_ref):
        o_ref[...] = ((p_ref[...] + q_ref[0].astype(F32)) + q_ref[1].astype(F32)) + q_ref[2].astype(F32)

    return _pcall(body, name=name, grid=(h // tm,), prefetch=1,
                  in_specs=[pl.BlockSpec((None, tm, c), lambda i, ch: (ch[0], i, 0)),
                            pl.BlockSpec((3, tm, c), lambda i, ch: (0, i, 0))],
                  out_specs=pl.BlockSpec((tm, c), lambda i, ch: (i, 0)),
                  out_shape=_sds((h, c), F32), sem=("parallel",))(chip, p, slots)


def _sum_slots(q, name):
    n, h, c = q.shape
    tm = _row_tile(h)

    def body(q_ref, o_ref):
        acc = q_ref[0]
        for k in range(1, n):
            acc = acc + q_ref[k]
        o_ref[...] = acc

    return _pcall(body, name=name, grid=(h // tm,),
                  in_specs=[pl.BlockSpec((n, tm, c), lambda i: (0, i, 0))],
                  out_specs=pl.BlockSpec((tm, c), lambda i: (i, 0)),
                  out_shape=_sds((h, c), F32), sem=("parallel",))(q)


def _pair_share(f, name):
    h, c = f.shape

    def body(f_ref, out_ref, send_sem, recv_sem):
        mx, my, mc = _my_pos()
        cp = pltpu.make_async_remote_copy(src_ref=f_ref, dst_ref=out_ref, send_sem=send_sem, recv_sem=recv_sem,
                                          device_id=(mx, my, 1 - mc), device_id_type=MESH)
        cp.start()
        cp.wait_recv()
        cp.wait_send()

    return _pcall(body, name=name, in_specs=[HBM_SPEC], out_specs=HBM_SPEC, out_shape=_sds((h, c), f.dtype),
                  scratch=[pltpu.SemaphoreType.DMA, pltpu.SemaphoreType.DMA])(f)


BIG = ("mlp_w1", "mlp_w2", "fox_w_in", "fox_w_o", "ssm_w_in", "ssm_w_out")
SMALL_SHARDED = ("ssm_conv_w", "ssm_conv_b", "ssm_norm_w")
PACK_COLS = 1024


def _pack_rows(parts, rows_multiple, dtype):
    flat = jnp.concatenate([p.reshape(-1).astype(dtype) for p in parts])
    unit = rows_multiple * PACK_COLS
    total = -(-flat.shape[0] // unit) * unit
    flat = jnp.pad(flat, (0, total - flat.shape[0]))
    return flat.reshape(total // PACK_COLS, PACK_COLS)


def _unpack(flat, shapes):
    out, off = [], 0
    for sh in shapes:
        n = 1
        for d_ in sh:
            n *= d_
        out.append(flat[off:off + n].reshape(sh))
        off += n
    return out


PIECE_ROWS = 16


def _piece_rows(shape):
    n = 1
    for d_ in shape:
        n *= d_
    rows = -(-n // PACK_COLS)
    return n, -(-rows // PIECE_ROWS) * PIECE_ROWS


def _pack2d(parts, rows_multiple, dtype):
    blocks = []
    for p in parts:
        n, rows = _piece_rows(p.shape)
        a = p.astype(dtype)
        if p.shape[-1] != PACK_COLS or n % PACK_COLS:
            a = jnp.pad(a.reshape(-1), (0, -n % PACK_COLS))
        a = a.reshape(-1, PACK_COLS)
        blocks.append(jnp.pad(a, ((0, rows - a.shape[0]), (0, 0))))
    total = sum(b.shape[0] for b in blocks)
    pad = -total % rows_multiple
    if pad:
        blocks.append(jnp.zeros((pad, PACK_COLS), dtype))
    return jnp.concatenate(blocks, axis=0)


def _unpack2d(buf, shapes):
    out, off = [], 0
    for sh in shapes:
        n, rows = _piece_rows(sh)
        piece = buf[off:off + rows]
        if sh[-1] == PACK_COLS and n % PACK_COLS == 0:
            out.append(piece[:n // PACK_COLS].reshape(sh))
        else:
            out.append(piece.reshape(-1)[:n].reshape(sh))
        off += rows
    return out


def _row_tile(h, cap=512):
    for step in (16, 8):
        best = 0
        for cand in range(step, cap + 1, step):
            if h % cand == 0:
                best = cand
        if best:
            return best
    return h


def _chip_slice(full, axis, k, width):
    idx = [slice(None)] * full.ndim
    idx[axis] = slice(k * width, (k + 1) * width)
    return full[tuple(idx)]


SHARD_AXIS = dict(mlp_w1=2, mlp_w2=1, fox_w_in=2, fox_w_o=1, ssm_w_in=2, ssm_w_out=1, ssm_conv_w=2,
                  ssm_conv_b=1, ssm_norm_w=1, ada_w=2)


def kernel(x, c, ada_w, ada_b, ln_mix_g, ln_mix_b, ln_mlp_g, ln_mlp_b, mlp_w1, mlp_w2, fox_w_in, fox_b_f, fox_w_o, ssm_w_in, ssm_conv_w, ssm_conv_b, ssm_dt_bias, ssm_a_log, ssm_d, ssm_norm_w, ssm_w_out, loss_target, m_ada_w, m_ada_b, m_ln_mix_g, m_ln_mix_b, m_ln_mlp_g, m_ln_mlp_b, m_mlp_w1, m_mlp_w2, m_fox_w_in, m_fox_b_f, m_fox_w_o, m_ssm_w_in, m_ssm_conv_w, m_ssm_conv_b, m_ssm_dt_bias, m_ssm_a_log, m_ssm_d, m_ssm_norm_w, m_ssm_w_out, v_ada_w, v_ada_b, v_ln_mix_g, v_ln_mix_b, v_ln_mlp_g, v_ln_mlp_b, v_mlp_w1, v_mlp_w2, v_fox_w_in, v_fox_b_f, v_fox_w_o, v_ssm_w_in, v_ssm_conv_w, v_ssm_conv_b, v_ssm_dt_bias, v_ssm_a_log, v_ssm_d, v_ssm_norm_w, v_ssm_w_out):
    names = ("ada_w", "ada_b", "ln_mix_g", "ln_mix_b", "ln_mlp_g", "ln_mlp_b", "mlp_w1", "mlp_w2", "fox_w_in",
             "fox_b_f", "fox_w_o", "ssm_w_in", "ssm_conv_w", "ssm_conv_b", "ssm_dt_bias", "ssm_a_log", "ssm_d",
             "ssm_norm_w", "ssm_w_out")
    weights = dict(zip(names, (ada_w, ada_b, ln_mix_g, ln_mix_b, ln_mlp_g, ln_mlp_b, mlp_w1, mlp_w2, fox_w_in,
                               fox_b_f, fox_w_o, ssm_w_in, ssm_conv_w, ssm_conv_b, ssm_dt_bias, ssm_a_log, ssm_d,
                               ssm_norm_w, ssm_w_out)))
    m_in = dict(zip(names, (m_ada_w, m_ada_b, m_ln_mix_g, m_ln_mix_b, m_ln_mlp_g, m_ln_mlp_b, m_mlp_w1, m_mlp_w2,
                            m_fox_w_in, m_fox_b_f, m_fox_w_o, m_ssm_w_in, m_ssm_conv_w, m_ssm_conv_b, m_ssm_dt_bias,
                            m_ssm_a_log, m_ssm_d, m_ssm_norm_w, m_ssm_w_out)))
    v_in = dict(zip(names, (v_ada_w, v_ada_b, v_ln_mix_g, v_ln_mix_b, v_ln_mlp_g, v_ln_mlp_b, v_mlp_w1, v_mlp_w2,
                            v_fox_w_in, v_fox_b_f, v_fox_w_o, v_ssm_w_in, v_ssm_conv_w, v_ssm_conv_b, v_ssm_dt_bias,
                            v_ssm_a_log, v_ssm_d, v_ssm_norm_w, v_ssm_w_out)))

    mx, my, mc = _my_pos()
    chip = 2 * mx + my
    me = 4 * mx + 2 * my + mc
    x0 = x[0]
    target = loss_target[0]
    s, d = x0.shape
    n_qkv = 3 * FOX_HEADS * HEAD_DIM

    big_shapes = [weights[n].shape for n in BIG]
    packed = _pack2d([weights[n] for n in BIG], 32, BF16)
    gathered = _gather_chips(packed.reshape(2, packed.shape[0] // 2, PACK_COLS), "gather_weights")
    gathered = gathered.reshape(4, packed.shape[0], PACK_COLS)
    per_chip = [_unpack2d(jnp.where(chip == k, packed, gathered[k]), big_shapes) for k in range(4)]
    full = {n: jnp.concatenate([per_chip[k][i] for k in range(4)], axis=SHARD_AXIS[n]) for i, n in enumerate(BIG)}

    small_shapes = [weights[n].shape for n in SMALL_SHARDED]
    small_packed = _pack_rows([weights[n] for n in SMALL_SHARDED] + [c], 8, F32).reshape(-1, LANES)
    small_all = _allgather8(small_packed, "gather_small")
    small_chip = [_unpack(small_all[2 * k].reshape(-1), small_shapes) for k in range(4)]
    small_full = {n: jnp.concatenate([small_chip[k][i] for k in range(4)], axis=SHARD_AXIS[n])
                  for i, n in enumerate(SMALL_SHARDED)}
    n_small = sum(weights[n].size for n in SMALL_SHARDED)
    c_all = small_all.reshape(8, -1)[:, n_small:n_small + d]

    cols = ada_w.shape[2]
    ada_b_cols = lax.dynamic_slice_in_dim(ada_b, chip * cols, cols, axis=1)[:, None, :]
    c16 = jnp.pad(c_all, ((0, 8), (0, 0)))
    mod_part = _ada_forward(c16, ada_w, ada_b_cols, "ada_fwd")[:, :8, :]
    mod_all = _allgather8(mod_part.reshape(-1, LANES), "gather_mod").reshape(8, DEPTH, 8, cols)
    mod_mine = jnp.stack([lax.dynamic_index_in_dim(mod_all[2 * k], me, axis=1, keepdims=False) for k in range(4)], axis=1)
    mods = [jnp.pad(mod_mine[i].reshape(6, d), ((0, 2), (0, 0))) for i in range(DEPTH)]

    w = dict(
        fox_qkv=full["fox_w_in"][0][:, :n_qkv],
        fox_f=jnp.pad(full["fox_w_in"][0][:, n_qkv:], ((0, 0), (0, LANES - FOX_HEADS))),
        fox_o=full["fox_w_o"][0],
        fox_bf=jnp.pad(fox_b_f, ((0, 0), (0, LANES - FOX_HEADS))),
        head_ind=jnp.asarray(np.arange(d)[:, None] // HEAD_DIM == np.arange(LANES)[None, :], BF16),
    )
    w.update(_ssm_weights(full["ssm_w_in"][0], small_full["ssm_conv_w"][0], small_full["ssm_conv_b"], ssm_dt_bias,
                          ssm_a_log, ssm_d, small_full["ssm_norm_w"], full["ssm_w_out"][0]))
    mixers = ((_fox_forward, _fox_backward), (_ssd_forward, _ssd_backward))

    saved = []
    xin = x0
    u = _modulate_in(x0, mods[0], "l0_mod_in")
    for i in range(DEPTH):
        tag = "l%d" % i
        y, sv = mixers[i % 2][0](u, w)
        r, x1, u2 = _res_ln_mod(xin, y, mods[i], ln_mix_g[i:i + 1], ln_mix_b[i:i + 1], tag + "_res_ln1")
        y2, hr, a = _mlp_forward(u2, full["mlp_w1"][i], full["mlp_w2"][i], tag)
        if i + 1 < DEPTH:
            r2, xin, u = _res_ln_next(x1, y2, mods[i], ln_mlp_g[i:i + 1], ln_mlp_b[i:i + 1], mods[i + 1],
                                      tag + "_res_ln2")
        else:
            r2 = _rowwise(lambda xv, yv, m: (ALPHA * xv + (1.0 + m[5:6]) * yv,), [x1, y2], [mods[i]], [(d, F32)], [],
                          name=tag + "_res2")[0]
        saved.append(dict(y=y, r=r, u2=u2, hr=hr, a=a, y2=y2, r2=r2, mix=sv))

    grads = {}
    dmod_parts = [dict() for _ in range(DEPTH)]
    ln_grads = {n: [None] * DEPTH for n in ("ln_mix_g", "ln_mix_b", "ln_mlp_g", "ln_mlp_b")}
    g_w1, g_w2 = [None] * DEPTH, [None] * DEPTH
    du = dres0 = None
    for i in reversed(range(DEPTH)):
        tag = "l%d" % i
        sv = saved[i]
        if i + 1 == DEPTH:
            dy2, dres, sq, dg2, db2, dgm = _loss_ln2_bwd(sv["r2"], sv["y2"], target, mods[i], ln_mlp_g[i:i + 1],
                                                         ln_mlp_b[i:i + 1], "loss_ln2_bwd")
            loss = lax.psum(0.5 * jnp.sum(sq) / d, ("x", "y", "c"))
        else:
            dy2, dres, dsca, dsha, dg2, db2, dgm = _mod_in_ln2_bwd(du, dres0, sv["r2"], sv["y2"], mods[i],
                                                                   ln_mlp_g[i:i + 1], ln_mlp_b[i:i + 1], mods[i + 1],
                                                                   tag + "_ln2_bwd")
            dmod_parts[i + 1].update(sc_a=dsca, sh_a=dsha)
        du2, g_w1[i], g_w2[i] = _mlp_backward(dy2, sv["u2"], sv["hr"], sv["a"], full["mlp_w1"][i], full["mlp_w2"][i], tag)
        dy, dres0, dscm, dshm, dg1, db1, dga = _ln1_bwd(du2, dres, sv["r"], sv["y"], mods[i], ln_mix_g[i:i + 1],
                                                        ln_mix_b[i:i + 1], tag + "_ln1_bwd")
        du, mg = mixers[i % 2][1](dy, sv["mix"], w)
        grads.update(mg)
        dmod_parts[i].update(g_a=dga, sh_m=dshm, sc_m=dscm, g_m=dgm)
        ln_grads["ln_mix_g"][i], ln_grads["ln_mix_b"][i] = dg1, db1
        ln_grads["ln_mlp_g"][i], ln_grads["ln_mlp_b"][i] = dg2, db2
    dx, dsca, dsha = _mod_in_bwd(du, dres0, x0, mods[0], "l0_mod_in_bwd")
    dmod_parts[0].update(sc_a=dsca, sh_a=dsha)
    dmods = [jnp.concatenate([p["sh_a"], p["sc_a"], p["g_a"], p["sh_m"], p["sc_m"], p["g_m"]], axis=1)
             for p in dmod_parts]
    grad_x = dx[None]
    grads["mlp_w1"] = jnp.stack(g_w1)
    grads["mlp_w2"] = jnp.stack(g_w2)
    for n in ("fox_w_in", "fox_w_o", "ssm_w_in", "ssm_w_out", "ssm_conv_w"):
        grads[n] = grads[n][None]

    small_names = ("ln_mix_g", "ln_mix_b", "ln_mlp_g", "ln_mlp_b", "fox_b_f", "ssm_dt_bias", "ssm_a_log", "ssm_d")
    small_parts = list(dmods)
    for n in small_names[:4]:
        small_parts.append(jnp.concatenate(ln_grads[n], axis=0))
    for n in small_names[4:]:
        small_parts.append(jnp.pad(grads[n], ((0, 0), (0, LANES - grads[n].shape[1]))))
    small_vec = _pack_rows(small_parts, 1, F32).reshape(-1, LANES)
    small_vec = jnp.pad(small_vec, ((0, -small_vec.shape[0] % 8), (0, 0)))
    small_g_all = _allgather8(small_vec, "gather_small_grads")
    small_sum = _sum_slots(small_g_all, "sum_small_grads").reshape(-1)
    dmod_sum = small_sum[:DEPTH * 6 * d].reshape(DEPTH, 6 * d)
    off = DEPTH * 6 * d
    final = {"ada_b": dmod_sum}
    for n in small_names[:4]:
        final[n] = small_sum[off:off + DEPTH * d].reshape(DEPTH, d)
        off += DEPTH * d
    for n in small_names[4:]:
        width = weights[n].shape[1]
        final[n] = small_sum[off:off + width].reshape(1, width)
        off += LANES

    dmod_all = small_g_all.reshape(8, -1)[:, :DEPTH * 6 * d].reshape(8, DEPTH, 6 * d)
    dmod_cols = lax.dynamic_slice_in_dim(dmod_all, chip * cols, cols, axis=2).transpose(1, 0, 2)
    final["ada_w"] = _ada_backward(c_all.T, dmod_cols, "ada_bwd")

    sharded = BIG + SMALL_SHARDED
    shard_shapes = [weights[n].shape for n in sharded]
    per_target = []
    for k in range(4):
        parts = [_chip_slice(grads[n], SHARD_AXIS[n], k, weights[n].shape[SHARD_AXIS[n]]) for n in sharded]
        per_target.append(_pack2d(parts, 128, F32))
    g_all = jnp.stack(per_target)
    rows = g_all.shape[1]
    g4 = g_all.reshape(4, 2, rows // 2, PACK_COLS)
    recv = _pair_exchange(g4, "rs_pair_exchange")
    part, part16 = _pair_add(g4, recv, jnp.reshape(mc, (1,)).astype(jnp.int32), "rs_pair_add")
    slots = _chip_exchange(part16, "rs_chip_exchange")
    half = _chip_sum(part, slots, jnp.reshape(chip, (1,)).astype(jnp.int32), "rs_chip_sum")
    other = _pair_share(half, "rs_pair_share")
    both = jnp.concatenate([jnp.where(mc == 0, half, other), jnp.where(mc == 0, other, half)], axis=0)
    for n, g_shard in zip(sharded, _unpack2d(both, shard_shapes)):
        final[n] = g_shard

    outs_g, outs_d, outs_m, outs_v = [], [], [], []
    for n in names:
        wv = weights[n]
        two_d = (-1, wv.shape[-1])
        delta, mn, vn = _adamw(wv.reshape(two_d), final[n].reshape(two_d), m_in[n].reshape(two_d),
                               v_in[n].reshape(two_d), "adamw_" + n)
        outs_g.append(final[n].reshape(wv.shape))
        outs_d.append(delta.reshape(wv.shape))
        outs_m.append(mn.reshape(wv.shape))
        outs_v.append(vn.reshape(wv.shape))
    return (loss, grad_x, *outs_g, *outs_d, *outs_m, *outs_v)
```

```python
import functools

import jax
import jax.numpy as jnp
import numpy as np
from jax import lax
from jax.experimental import pallas as pl
from jax.experimental.pallas import tpu as pltpu

F32, BF16 = jnp.float32, jnp.bfloat16
MESH = pl.DeviceIdType.MESH
HBM_SPEC = pl.BlockSpec(memory_space=pltpu.HBM)

VMEM_LIMIT_BYTES = 52 * 2**20
LANES = 128

FOX_HEADS, HEAD_DIM = 16, 64
SSM_HEADS, SSM_GROUPS, SSM_STATE, SSM_CHUNK, SSM_CONV = 32, 8, 128, 128, 4
SSM_GROUP_WIDTH = 256
LN_EPS, RMS_EPS = 1e-5, 1e-5
DEPTH = 2
ALPHA = (2.0 * DEPTH) ** 0.25
ADAM_LR, ADAM_B1, ADAM_B2, ADAM_EPS, ADAM_WD, ADAM_STEP = 0.001, 0.9, 0.999, 1e-08, 0.01, 10

ATT_TILE = 512
ROW_TILE = 512
SCAN_TILE = 512
SSD_GROUPS_PER_STEP = 4
MM_TM, MM_TN, MM_TK = 1024, 1024, 1024

NT_DIMS = (((1,), (1,)), ((), ()))
TN_DIMS = (((0,), (0,)), ((), ()))
NN_DIMS = (((1,), (0,)), ((), ()))


def _pcall(body, *, name, out_shape, grid=(), in_specs=None, out_specs=None, scratch=(), sem=None, prefetch=0):
    params = dict(vmem_limit_bytes=VMEM_LIMIT_BYTES)
    if sem is not None:
        params["dimension_semantics"] = sem
    if prefetch:
        grid_spec = pltpu.PrefetchScalarGridSpec(num_scalar_prefetch=prefetch, grid=grid, in_specs=in_specs,
                                                 out_specs=out_specs, scratch_shapes=scratch)
        return pl.pallas_call(body, out_shape=out_shape, grid_spec=grid_spec, name=name,
                              compiler_params=pltpu.CompilerParams(**params))
    kwargs = {}
    if in_specs is not None:
        kwargs["in_specs"] = in_specs
    if out_specs is not None:
        kwargs["out_specs"] = out_specs
    return pl.pallas_call(body, out_shape=out_shape, grid=grid, scratch_shapes=scratch, name=name,
                          compiler_params=pltpu.CompilerParams(**params), **kwargs)


def _sds(shape, dtype):
    return jax.ShapeDtypeStruct(tuple(shape), dtype)


def _dot(a, b, dims=NN_DIMS):
    return lax.dot_general(a, b, dims, preferred_element_type=F32)


def _sigmoid(x):
    return 1.0 / (1.0 + jnp.exp(-x))


def _silu(x):
    return x * _sigmoid(x)


def _dsilu(x):
    s = _sigmoid(x)
    return s * (1.0 + x * (1.0 - s))


def _dot_split(x, m16, dims=NN_DIMS, passes=3):
    hi = x.astype(BF16)
    r1 = x - hi.astype(F32)
    mid = r1.astype(BF16)
    out = _dot(hi, m16, dims) + _dot(mid, m16, dims)
    if passes == 3:
        lo = (r1 - mid.astype(F32)).astype(BF16)
        out = out + _dot(lo, m16, dims)
    return out


def _mm(a, b, dims, outs, *, name, tm=MM_TM, tn=MM_TN, tk=MM_TK, epi=None, extra=()):
    if dims == "nn":
        (m, k), (k2, n) = a.shape, b.shape
    elif dims == "nt":
        (m, k), (n, k2) = a.shape, b.shape
    else:
        (k, m), (k2, n) = a.shape, b.shape
    assert k == k2, (a.shape, b.shape, dims)
    tm, tn, tk = min(tm, m), min(tn, n), min(tk, k)
    assert m % tm == 0 and n % tn == 0 and k % tk == 0, (m, n, k, tm, tn, tk)
    nk = k // tk
    dn = {"nn": NN_DIMS, "nt": NT_DIMS, "tn": TN_DIMS}[dims]
    n_extra, n_out = len(extra), len(outs)
    if epi is None:
        epi = lambda acc: (acc,) * n_out

    def body(a_ref, b_ref, *rest):
        extra_refs, out_refs, acc_ref = rest[:n_extra], rest[n_extra:n_extra + n_out], rest[-1]
        kk = pl.program_id(2)

        @pl.when(kk == 0)
        def _():
            acc_ref[...] = jnp.zeros_like(acc_ref)

        acc_ref[...] += _dot(a_ref[...].astype(BF16), b_ref[...].astype(BF16), dn)

        @pl.when(kk == nk - 1)
        def _():
            res = epi(acc_ref[...], *[e[...] for e in extra_refs])
            for o, r in zip(out_refs, res):
                o[...] = r.astype(o.dtype)

    if dims == "tn":
        a_spec = pl.BlockSpec((tk, tm), lambda i, j, kk: (kk, i))
    else:
        a_spec = pl.BlockSpec((tm, tk), lambda i, j, kk: (i, kk))
    if dims == "nt":
        b_spec = pl.BlockSpec((tn, tk), lambda i, j, kk: (j, kk))
    else:
        b_spec = pl.BlockSpec((tk, tn), lambda i, j, kk: (kk, j))
    o_spec = pl.BlockSpec((tm, tn), lambda i, j, kk: (i, j))
    res = _pcall(body, name=name, grid=(m // tm, n // tn, nk),
                 in_specs=[a_spec, b_spec] + [o_spec] * n_extra,
                 out_specs=[o_spec] * n_out,
                 out_shape=[_sds((m, n), d) for d in outs],
                 scratch=[pltpu.VMEM((tm, tn), F32)],
                 sem=("parallel", "parallel", "arbitrary"))(a, b, *extra)
    return res[0] if n_out == 1 else res


def _k_blocks(a, tk=None):
    tk = tk or MM_TK
    return [(a, kb) for kb in range(a.shape[1] // tk)]


def _mm_nt_blocks(a_blocks, b, start, *, name, tm=MM_TM, tk=None):
    tk = tk or MM_TK
    tm = min(tm, a_blocks[0][0].shape[0])
    m, n, p_n = a_blocks[0][0].shape[0], b.shape[0], len(a_blocks)
    assert b.shape[1] == p_n * tk and m % tm == 0

    def body(*refs):
        a_refs = refs[:p_n]
        b_ref, s_ref, o_ref, acc_ref = refs[p_n:]
        kk = pl.program_id(1)

        @pl.when(kk == 0)
        def _():
            acc_ref[...] = s_ref[...]
        for p in range(p_n):
            @pl.when(kk == p)
            def _(p=p):
                acc_ref[...] += _dot(a_refs[p][...].astype(BF16), b_ref[...].astype(BF16), NT_DIMS)

        @pl.when(kk == p_n - 1)
        def _():
            o_ref[...] = acc_ref[...]

    in_specs = [pl.BlockSpec((tm, tk), functools.partial(lambda kb, i, kk: (i, kb), kb)) for _, kb in a_blocks]
    in_specs += [pl.BlockSpec((n, tk), lambda i, kk: (0, kk)), pl.BlockSpec((tm, n), lambda i, kk: (i, 0))]
    return _pcall(body, name=name, grid=(m // tm, p_n), in_specs=in_specs,
                  out_specs=pl.BlockSpec((tm, n), lambda i, kk: (i, 0)), out_shape=_sds((m, n), F32),
                  scratch=[pltpu.VMEM((tm, n), F32)], sem=("parallel", "arbitrary"))(*[a for a, _ in a_blocks], b, start)


def _rowwise(fn, rows, consts, row_outs, acc_outs, *, name, tm=ROW_TILE):
    s = rows[0].shape[0]
    tm = min(tm, s)
    assert s % tm == 0
    n_in, n_o = len(rows) + len(consts), len(row_outs)

    def body(*refs):
        ins, outs = refs[:n_in], refs[n_in:]
        res = fn(*[r[...] for r in ins])
        if not isinstance(res, (tuple, list)):
            res = (res,)
        for o, val in zip(outs[:n_o], res[:n_o]):
            o[...] = val.astype(o.dtype)
        if acc_outs:
            @pl.when(pl.program_id(0) == 0)
            def _():
                for o in outs[n_o:]:
                    o[...] = jnp.zeros_like(o)
            for o, val in zip(outs[n_o:], res[n_o:]):
                o[...] += val

    in_specs = [pl.BlockSpec((tm, r.shape[1]), lambda i: (i, 0)) for r in rows]
    in_specs += [pl.BlockSpec(c.shape, functools.partial(lambda nd, i: (0,) * nd, c.ndim)) for c in consts]
    out_specs = [pl.BlockSpec((tm, c), lambda i: (i, 0)) for c, _ in row_outs]
    out_specs += [pl.BlockSpec(tuple(sh), lambda i: (0, 0)) for sh in acc_outs]
    out_shape = [_sds((s, c), d) for c, d in row_outs] + [_sds(sh, F32) for sh in acc_outs]
    res = _pcall(body, name=name, grid=(s // tm,), in_specs=in_specs, out_specs=out_specs,
                 out_shape=out_shape, sem=("arbitrary",))(*rows, *consts)
    return res


def _colsum(x):
    return jnp.sum(x, axis=0, keepdims=True)


def _ln_stats(r):
    mu = jnp.mean(r, axis=-1, keepdims=True)
    xc = r - mu
    var = jnp.mean(xc * xc, axis=-1, keepdims=True)
    rstd = lax.rsqrt(var + LN_EPS)
    return xc * rstd, rstd


def _ln_bwd(dy, xhat, rstd, gamma):
    dyg = dy * gamma
    m1 = jnp.mean(dyg, axis=-1, keepdims=True)
    m2 = jnp.mean(dyg * xhat, axis=-1, keepdims=True)
    return rstd * (dyg - m1 - xhat * m2)


def _modulate_in(x, mod, name):
    def fn(xv, m):
        return (xv * (1.0 + m[1:2]) + m[0:1],)
    return _rowwise(fn, [x], [mod], [(x.shape[1], BF16)], [], name=name)[0]


def _res_ln_mod(x, y, mod, g, b, name):
    d = x.shape[1]

    def fn(xv, yv, m, gv, bv):
        r = ALPHA * xv + (1.0 + m[2:3]) * yv
        xhat, _ = _ln_stats(r)
        x1 = xhat * gv + bv
        u2 = x1 * (1.0 + m[4:5]) + m[3:4]
        return r, x1, u2
    return _rowwise(fn, [x, y], [mod, g, b], [(d, F32), (d, F32), (d, BF16)], [], name=name)


def _res_ln_next(x, y, mod, g, b, mod_next, name):
    d = x.shape[1]

    def fn(xv, yv, m, gv, bv, mn):
        r = ALPHA * xv + (1.0 + m[5:6]) * yv
        xhat, _ = _ln_stats(r)
        out = xhat * gv + bv
        return r, out, out * (1.0 + mn[1:2]) + mn[0:1]
    return _rowwise(fn, [x, y], [mod, g, b, mod_next], [(d, F32), (d, F32), (d, BF16)], [], name=name)


def _loss_ln2_bwd(x1, y2, target, mod, g, b, name):
    d = x1.shape[1]

    def fn(xv, yv, tv, m, gv, bv):
        rv = ALPHA * xv + (1.0 + m[5:6]) * yv
        xhat, rstd = _ln_stats(rv)
        e = xhat * gv + bv - tv
        dxv = e * (1.0 / d)
        dr = _ln_bwd(dxv, xhat, rstd, gv)
        return (dr * (1.0 + m[5:6]), ALPHA * dr,
                _colsum(e * e), _colsum(dxv * xhat), _colsum(dxv), _colsum(dr * yv))
    return _rowwise(fn, [x1, y2, target], [mod, g, b], [(d, BF16), (d, F32)], [(1, d)] * 4, name=name)


def _mod_in_ln2_bwd(du, dres, r2, y2, mod, g, b, mod_next, name):
    d = du.shape[1]

    def fn(duv, drv, rv, yv, m, gv, bv, mn):
        xhat, rstd = _ln_stats(rv)
        xout = xhat * gv + bv
        dxv = duv * (1.0 + mn[1:2]) + drv
        dr = _ln_bwd(dxv, xhat, rstd, gv)
        return (dr * (1.0 + m[5:6]), ALPHA * dr,
                _colsum(duv * xout), _colsum(duv), _colsum(dxv * xhat), _colsum(dxv), _colsum(dr * yv))
    return _rowwise(fn, [du, dres, r2, y2], [mod, g, b, mod_next], [(d, BF16), (d, F32)], [(1, d)] * 5, name=name)


def _ln1_bwd(du2, dres, r, y, mod, g, b, name):
    d = du2.shape[1]

    def fn(duv, drv, rv, yv, m, gv, bv):
        xhat, rstd = _ln_stats(rv)
        x1 = xhat * gv + bv
        dx1 = duv * (1.0 + m[4:5]) + drv
        dr = _ln_bwd(dx1, xhat, rstd, gv)
        return (dr * (1.0 + m[2:3]), ALPHA * dr,
                _colsum(duv * x1), _colsum(duv), _colsum(dx1 * xhat), _colsum(dx1), _colsum(dr * yv))
    return _rowwise(fn, [du2, dres, r, y], [mod, g, b], [(d, BF16), (d, F32)], [(1, d)] * 5, name=name)


def _mod_in_bwd(du, dres, x, mod, name):
    d = du.shape[1]

    def fn(duv, drv, xv, m):
        return duv * (1.0 + m[1:2]) + drv, _colsum(duv * xv), _colsum(duv)
    return _rowwise(fn, [du, dres, x], [mod], [(d, F32)], [(1, d)] * 2, name=name)


def _fox_gate(fraw, b_pad, name):
    s = fraw.shape[0]
    tb = min(SCAN_TILE, s)

    def body(f_ref, b_ref, cum_ref, rows_ref, carry):
        @pl.when(pl.program_id(0) == 0)
        def _():
            carry[...] = jnp.zeros_like(carry)
        z = f_ref[...] + b_ref[...]
        lf = jnp.minimum(z, 0.0) - jnp.log(1.0 + jnp.exp(-jnp.abs(z)))
        lane = lax.broadcasted_iota(jnp.int32, (tb, LANES), 1)
        row = lax.broadcasted_iota(jnp.int32, (tb, LANES), 0)
        c = jnp.where(lane < FOX_HEADS, lf, 0.0)
        sh = 1
        while sh < tb:
            c = c + jnp.where(row >= sh, pltpu.roll(c, sh, 0), 0.0)
            sh *= 2
        c = c + carry[0:1, :]
        cum_ref[...] = c
        rows_ref[...] = c.T[0:FOX_HEADS, :]
        carry[0:1, :] = c[tb - 1:tb, :]

    return _pcall(body, name=name, grid=(s // tb,),
                  in_specs=[pl.BlockSpec((tb, LANES), lambda i: (i, 0)), pl.BlockSpec((1, LANES), lambda i: (0, 0))],
                  out_specs=[pl.BlockSpec((tb, LANES), lambda i: (i, 0)), pl.BlockSpec((FOX_HEADS, tb), lambda i: (0, i))],
                  out_shape=[_sds((s, LANES), F32), _sds((FOX_HEADS, s), F32)], scratch=[pltpu.VMEM((8, LANES), F32)],
                  sem=("arbitrary",))(fraw, b_pad)


def _fox_gate_bwd(drow, dcol, fraw, b_pad, name):
    s = fraw.shape[0]
    tb = min(SCAN_TILE, s)
    n = s // tb

    def body(dr_ref, dc_ref, f_ref, b_ref, df_ref, db_ref, carry):
        @pl.when(pl.program_id(0) == 0)
        def _():
            carry[...] = jnp.zeros_like(carry)
            db_ref[...] = jnp.zeros_like(db_ref)
        row = lax.broadcasted_iota(jnp.int32, (tb, LANES), 0)
        c = dr_ref[...] + dc_ref[...]
        sh = 1
        while sh < tb:
            c = c + jnp.where(row + sh < tb, pltpu.roll(c, tb - sh, 0), 0.0)
            sh *= 2
        c = c + carry[0:1, :]
        carry[0:1, :] = c[0:1, :]
        z = f_ref[...] + b_ref[...]
        df = c * (1.0 / (1.0 + jnp.exp(z)))
        df_ref[...] = df.astype(df_ref.dtype)
        db_ref[...] += _colsum(df)

    rev = lambda i: (n - 1 - i, 0)
    return _pcall(body, name=name, grid=(n,),
                  in_specs=[pl.BlockSpec((tb, LANES), rev)] * 3 + [pl.BlockSpec((1, LANES), lambda i: (0, 0))],
                  out_specs=[pl.BlockSpec((tb, LANES), rev), pl.BlockSpec((1, LANES), lambda i: (0, 0))],
                  out_shape=[_sds((s, LANES), BF16), _sds((1, LANES), F32)],
                  scratch=[pltpu.VMEM((8, LANES), F32)], sem=("arbitrary",))(drow, dcol, fraw, b_pad)


def _head_pair_masks(t):
    lane = lax.broadcasted_iota(jnp.int32, (t, LANES), 1)
    return lane < HEAD_DIM


def _lane_blocks(x):
    return [x[:, c * LANES:(c + 1) * LANES] for c in range(x.shape[1] // LANES)]


def _sum_list(xs):
    acc = xs[0]
    for x in xs[1:]:
        acc = acc + x
    return acc


def _causal(t, transposed=False):
    ri = lax.broadcasted_iota(jnp.int32, (t, t), 0)
    ci = lax.broadcasted_iota(jnp.int32, (t, t), 1)
    return ci >= ri if transposed else ri >= ci


def _span_mask(r0, r1, c0, c1, transposed=False):
    ri = lax.broadcasted_iota(jnp.int32, (r1 - r0, c1 - c0), 0) + r0
    ci = lax.broadcasted_iota(jnp.int32, (r1 - r0, c1 - c0), 1) + c0
    return ci >= ri if transposed else ri >= ci


def _full_spans(t):
    return ((0, t, 0, t, False),)


def _diagonal_spans(t, transposed=False):
    h = t // 2
    if h % LANES:
        return ((0, t, 0, t, True),)
    if transposed:
        return ((0, h, 0, t, True), (h, t, h, t, True))
    return ((0, h, 0, h, True), (h, t, 0, t, True))


def _flash_fwd(qkv, ck_rows, kb_start, name):
    s = qkv.shape[0]
    t = min(ATT_TILE, s)
    nq = s // t
    scale = HEAD_DIM ** -0.5
    hp_blocks = FOX_HEADS // 2

    def body(ks_ref, q_ref, k_ref, v_ref, ck_ref, o_ref, lse_ref, lse_rows_ref, acc_ref, m_ref, l_ref):
        hp, qb = pl.program_id(0), pl.program_id(1)
        q2 = q_ref[...] * scale
        first = _head_pair_masks(t)
        zero = jnp.zeros_like(q2)
        qs = (jnp.where(first, q2, zero), jnp.where(first, zero, q2))
        m_ref[...] = jnp.full_like(m_ref, -jnp.inf)
        l_ref[...] = jnp.zeros_like(l_ref)
        acc_ref[...] = jnp.zeros_like(acc_ref)

        def tile(kb, spans):
            off = pl.multiple_of(kb * t, t)
            k2 = k_ref[pl.ds(off, t), :]
            v2 = v_ref[pl.ds(off, t), :]
            ck = ck_ref[kb]
            for r0, r1, c0, c1, masked in spans:
                kk, vv, fr = k2[c0:c1], v2[c0:c1], first[r0:r1]
                pvs, als = [], []
                for j in range(2):
                    sc = _dot(qs[j][r0:r1], kk, NT_DIMS) - ck[j:j + 1, c0:c1]
                    if masked:
                        sc = jnp.where(_span_mask(r0, r1, c0, c1), sc, -jnp.inf)
                    blocks = _lane_blocks(sc)
                    mx = blocks[0]
                    for b in blocks[1:]:
                        mx = jnp.maximum(mx, b)
                    m_old = m_ref[j, r0:r1]
                    m_new = jnp.maximum(m_old, jnp.max(mx, axis=1, keepdims=True))
                    ps = [jnp.exp(b - m_new) for b in blocks]
                    a = jnp.exp(m_old - m_new)
                    l_ref[j, r0:r1] = a * l_ref[j, r0:r1] + _sum_list(ps)
                    m_ref[j, r0:r1] = m_new
                    pvs.append(_dot(jnp.concatenate(ps, axis=1).astype(BF16), vv))
                    als.append(a)
                acc_ref[r0:r1] = jnp.where(fr, als[0], als[1]) * acc_ref[r0:r1] + jnp.where(fr, pvs[0], pvs[1])

        def step(kb, carry):
            tile(kb, _full_spans(t))
            return carry

        lax.fori_loop(ks_ref[hp, qb], qb, step, 0)
        tile(qb, ((0, t, 0, t, True),))
        l0 = jnp.sum(l_ref[0], axis=1, keepdims=True)
        l1 = jnp.sum(l_ref[1], axis=1, keepdims=True)
        o_ref[...] = acc_ref[...] / jnp.where(first, l0, l1)
        for j, lj in enumerate((l0, l1)):
            lse = m_ref[j] + jnp.log(jnp.broadcast_to(lj, (t, LANES)))
            lse_ref[:, j:j + 1] = lse[:, 0:1]
            lse_rows_ref[j:j + 1, :] = lse.T[0:1, :]

    return _pcall(
        body, name=name, grid=(hp_blocks, nq), prefetch=1,
        in_specs=[pl.BlockSpec((t, LANES), lambda h, i, ks: (i, h)),
                  pl.BlockSpec((s, LANES), lambda h, i, ks: (0, hp_blocks + h)),
                  pl.BlockSpec((s, LANES), lambda h, i, ks: (0, 2 * hp_blocks + h)),
                  pl.BlockSpec((None, nq, 2, t), lambda h, i, ks: (h, 0, 0, 0))],
        out_specs=[pl.BlockSpec((t, LANES), lambda h, i, ks: (i, h)),
                   pl.BlockSpec((None, t, 2), lambda h, i, ks: (h, i, 0)),
                   pl.BlockSpec((None, None, 2, t), lambda h, i, ks: (h, i, 0, 0))],
        out_shape=[_sds((s, hp_blocks * LANES), F32), _sds((hp_blocks, s, 2), F32), _sds((hp_blocks, nq, 2, t), F32)],
        scratch=[pltpu.VMEM((t, LANES), F32), pltpu.VMEM((2, t, LANES), F32), pltpu.VMEM((2, t, LANES), F32)],
        sem=("parallel", "arbitrary"))(kb_start, qkv, qkv, qkv, ck_rows)


def _flash_dq(qkv, do16, ck_rows, lse_c, delta, kb_start, name):
    s = qkv.shape[0]
    t = min(ATT_TILE, s)
    nq = s // t
    scale = HEAD_DIM ** -0.5
    hp_blocks = FOX_HEADS // 2

    def body(ks_ref, q_ref, do_ref, k_ref, v_ref, ck_ref, lse_ref, dl_ref, dq_ref, drow_ref, acc_ref, row_acc):
        hp, qb = pl.program_id(0), pl.program_id(1)
        q2, do2 = q_ref[...] * scale, do_ref[...]
        first = _head_pair_masks(t)
        zero = jnp.zeros_like(q2)
        qs = (jnp.where(first, q2, zero), jnp.where(first, zero, q2))
        dos = (jnp.where(first, do2, zero), jnp.where(first, zero, do2))
        lse, dl = lse_ref[...], dl_ref[...]
        lane = lax.broadcasted_iota(jnp.int32, (t, LANES), 1)
        lse_b = [jnp.broadcast_to(lse[:, j:j + 1], (t, LANES)) for j in range(2)]
        dl_b = [jnp.broadcast_to(jnp.sum(jnp.where(lane == 2 * hp + j, dl, 0.0), axis=1, keepdims=True), (t, LANES))
                for j in range(2)]
        acc_ref[...] = jnp.zeros_like(acc_ref)
        row_acc[...] = jnp.zeros_like(row_acc)

        def tile(kb, spans):
            off = pl.multiple_of(kb * t, t)
            k2 = k_ref[pl.ds(off, t), :]
            v2 = v_ref[pl.ds(off, t), :]
            ck = ck_ref[kb]
            for r0, r1, c0, c1, masked in spans:
                kk, vv = k2[c0:c1], v2[c0:c1]
                dqs = []
                for j in range(2):
                    sc = _dot(qs[j][r0:r1], kk, NT_DIMS) - ck[j:j + 1, c0:c1]
                    if masked:
                        sc = jnp.where(_span_mask(r0, r1, c0, c1), sc, -jnp.inf)
                    dp = _dot(dos[j][r0:r1], vv, NT_DIMS)
                    lb, db_ = lse_b[j][r0:r1], dl_b[j][r0:r1]
                    dsb = [jnp.exp(x - lb) * (d - db_) for x, d in zip(_lane_blocks(sc), _lane_blocks(dp))]
                    row_acc[j, r0:r1] += _sum_list(dsb)
                    dqs.append(_dot(jnp.concatenate(dsb, axis=1).astype(BF16), kk))
                acc_ref[r0:r1] += jnp.where(first[r0:r1], dqs[0], dqs[1])

        def step(kb, carry):
            tile(kb, _full_spans(t))
            return carry

        lax.fori_loop(ks_ref[hp, qb], qb, step, 0)
        tile(qb, _diagonal_spans(t))
        dq_ref[...] = (acc_ref[...] * scale).astype(dq_ref.dtype)
        for j in range(2):
            drow_ref[j:j + 1, :] = jnp.sum(row_acc[j].T, axis=0, keepdims=True)

    return _pcall(
        body, name=name, grid=(hp_blocks, nq), prefetch=1,
        in_specs=[pl.BlockSpec((t, LANES), lambda h, i, ks: (i, h)),
                  pl.BlockSpec((t, LANES), lambda h, i, ks: (i, h)),
                  pl.BlockSpec((s, LANES), lambda h, i, ks: (0, hp_blocks + h)),
                  pl.BlockSpec((s, LANES), lambda h, i, ks: (0, 2 * hp_blocks + h)),
                  pl.BlockSpec((None, nq, 2, t), lambda h, i, ks: (h, 0, 0, 0)),
                  pl.BlockSpec((None, t, 2), lambda h, i, ks: (h, i, 0)),
                  pl.BlockSpec((t, LANES), lambda h, i, ks: (i, 0))],
        out_specs=[pl.BlockSpec((t, LANES), lambda h, i, ks: (i, h)),
                   pl.BlockSpec((None, None, 2, t), lambda h, i, ks: (h, i, 0, 0))],
        out_shape=[_sds((s, hp_blocks * LANES), BF16), _sds((hp_blocks, nq, 2, t), F32)],
        scratch=[pltpu.VMEM((t, LANES), F32), pltpu.VMEM((2, t, LANES), F32)],
        sem=("parallel", "arbitrary"))(kb_start, qkv, do16, qkv, qkv, ck_rows, lse_c, delta)


def _flash_dkv(qkv, do16, cum, lse_rows, dl_rows, qb_end, name):
    s = qkv.shape[0]
    t = min(ATT_TILE, s)
    nq = s // t
    scale = HEAD_DIM ** -0.5
    hp_blocks = FOX_HEADS // 2

    def body(qe_ref, k_ref, v_ref, cum_ref, q_ref, do_ref, lse_ref, dl_ref, dk_ref, dv_ref, dck_ref,
             dk_acc, dv_acc, dck_acc):
        hp, kb = pl.program_id(0), pl.program_id(1)
        k2, v2 = k_ref[...] * scale, v_ref[...]
        first = _head_pair_masks(t)
        zero = jnp.zeros_like(k2)
        ks = (jnp.where(first, k2, zero), jnp.where(first, zero, k2))
        vs = (jnp.where(first, v2, zero), jnp.where(first, zero, v2))
        cumv = cum_ref[...]
        lane = lax.broadcasted_iota(jnp.int32, (t, LANES), 1)
        ck_b = [jnp.broadcast_to(jnp.sum(jnp.where(lane == 2 * hp + j, cumv, 0.0), axis=1, keepdims=True), (t, LANES))
                for j in range(2)]
        dk_acc[...] = jnp.zeros_like(dk_acc)
        dv_acc[...] = jnp.zeros_like(dv_acc)
        dck_acc[...] = jnp.zeros_like(dck_acc)

        def tile(qb, spans):
            off = pl.multiple_of(qb * t, t)
            q2 = q_ref[pl.ds(off, t), :]
            do2 = do_ref[pl.ds(off, t), :]
            lse, dl = lse_ref[qb], dl_ref[qb]
            for r0, r1, c0, c1, masked in spans:
                qq, dd, fr = q2[c0:c1], do2[c0:c1], first[r0:r1]
                dvs, dks = [], []
                for j in range(2):
                    sc = _dot(ks[j][r0:r1], qq, NT_DIMS)
                    if masked:
                        sc = jnp.where(_span_mask(r0, r1, c0, c1, transposed=True), sc, -jnp.inf)
                    dp = _dot(vs[j][r0:r1], dd, NT_DIMS) - dl[j:j + 1, c0:c1]
                    cb_ = ck_b[j][r0:r1]
                    pb = [jnp.exp((x - cb_) - l) for x, l in zip(_lane_blocks(sc), _lane_blocks(lse[j:j + 1, c0:c1]))]
                    dsb = [p * d for p, d in zip(pb, _lane_blocks(dp))]
                    dck_acc[j, r0:r1] += _sum_list(dsb)
                    dvs.append(_dot(jnp.concatenate(pb, axis=1).astype(BF16), dd))
                    dks.append(_dot(jnp.concatenate(dsb, axis=1).astype(BF16), qq))
                dv_acc[r0:r1] += jnp.where(fr, dvs[0], dvs[1])
                dk_acc[r0:r1] += jnp.where(fr, dks[0], dks[1])

        def step(qb, carry):
            tile(qb, _full_spans(t))
            return carry

        tile(kb, _diagonal_spans(t, transposed=True))
        lax.fori_loop(kb + 1, qe_ref[hp, kb] + 1, step, 0)
        dk_ref[...] = (dk_acc[...] * scale).astype(dk_ref.dtype)
        dv_ref[...] = dv_acc[...].astype(dv_ref.dtype)
        for j in range(2):
            dck_ref[j:j + 1, :] = -jnp.sum(dck_acc[j].T, axis=0, keepdims=True)

    return _pcall(
        body, name=name, grid=(hp_blocks, nq), prefetch=1,
        in_specs=[pl.BlockSpec((t, LANES), lambda h, j, qe: (j, hp_blocks + h)),
                  pl.BlockSpec((t, LANES), lambda h, j, qe: (j, 2 * hp_blocks + h)),
                  pl.BlockSpec((t, LANES), lambda h, j, qe: (j, 0)),
                  pl.BlockSpec((s, LANES), lambda h, j, qe: (0, h)),
                  pl.BlockSpec((s, LANES), lambda h, j, qe: (0, h)),
                  pl.BlockSpec((None, nq, 2, t), lambda h, j, qe: (h, 0, 0, 0)),
                  pl.BlockSpec((None, nq, 2, t), lambda h, j, qe: (h, 0, 0, 0))],
        out_specs=[pl.BlockSpec((t, LANES), lambda h, j, qe: (j, h)),
                   pl.BlockSpec((t, LANES), lambda h, j, qe: (j, h)),
                   pl.BlockSpec((None, None, 2, t), lambda h, j, qe: (h, j, 0, 0))],
        out_shape=[_sds((s, hp_blocks * LANES), BF16), _sds((s, hp_blocks * LANES), BF16),
                   _sds((hp_blocks, nq, 2, t), F32)],
        scratch=[pltpu.VMEM((t, LANES), F32), pltpu.VMEM((t, LANES), F32), pltpu.VMEM((2, t, LANES), F32)],
        sem=("parallel", "arbitrary"))(qb_end, qkv, qkv, cum, qkv, do16, lse_rows, dl_rows)


SKIP_NATS = 110.0


def _qk_norms(qkv, ind16, name):
    d = FOX_HEADS * HEAD_DIM

    def fn(tile, ind):
        q = tile[:, :d].astype(F32)
        k = tile[:, d:2 * d].astype(F32)
        return _dot_split(q * q, ind), _dot_split(k * k, ind)
    return _rowwise(fn, [qkv], [ind16], [(LANES, F32), (LANES, F32)], [], name=name)


def _skip_bounds(qn, kn, cum, t):
    s = qn.shape[0]
    nq = s // t
    hp = FOX_HEADS // 2
    scale = HEAD_DIM ** -0.5
    qmax = jnp.sqrt(jnp.max(qn.reshape(nq, t, FOX_HEADS), axis=1))
    kmax = jnp.sqrt(jnp.max(kn, axis=0))
    bound = qmax * kmax[None, :] * (scale * 1.01) + 1e-3
    gap = cum[0::t][:, None, :] - cum[t - 1::t][None, :, :]
    idx = jnp.arange(nq, dtype=jnp.int32)
    needed = (gap + 2.0 * bound[:, None, :]) > -SKIP_NATS
    needed = needed.reshape(nq, nq, hp, 2).any(axis=-1) & (idx[None, :] <= idx[:, None])[:, :, None]
    first = jnp.min(jnp.where(needed, idx[None, :, None], nq), axis=1)
    first = jnp.minimum(first, idx[:, None])
    start = lax.cummin(first, axis=0, reverse=True)
    uses = start[:, None, :] <= idx[None, :, None]
    last = jnp.max(jnp.where(uses, idx[:, None, None], 0), axis=0)
    last = jnp.maximum(last, idx[:, None])
    return start.T.astype(jnp.int32), last.T.astype(jnp.int32)


def _head_rowsum(a, b, ind16, name):
    s, d = a.shape
    tm = min(ROW_TILE, s)

    def body(a_ref, b_ref, ind_ref, o_ref, rows_ref):
        dsum = _dot_split(a_ref[...] * b_ref[...], ind_ref[...])
        o_ref[...] = dsum
        rows_ref[...] = dsum.T[0:FOX_HEADS, :]

    tile = pl.BlockSpec((tm, d), lambda i: (i, 0))
    return _pcall(body, name=name, grid=(s // tm,),
                  in_specs=[tile, tile, pl.BlockSpec((d, LANES), lambda i: (0, 0))],
                  out_specs=[pl.BlockSpec((tm, LANES), lambda i: (i, 0)), pl.BlockSpec((FOX_HEADS, tm), lambda i: (0, i))],
                  out_shape=[_sds((s, LANES), F32), _sds((FOX_HEADS, s), F32)], sem=("parallel",))(a, b, ind16)


def _rows_to_tiles(x, t):
    s = x.shape[1]
    return x.reshape(FOX_HEADS // 2, 2, s // t, t).transpose(0, 2, 1, 3)


def _tiles_to_cols(x):
    hp, nq, _, t = x.shape
    return jnp.pad(x.transpose(1, 3, 0, 2).reshape(nq * t, 2 * hp), ((0, 0), (0, LANES - 2 * hp)))


def _fox_forward(u, w):
    s = u.shape[0]
    t = min(ATT_TILE, s)
    qkv = _mm(u, w["fox_qkv"], "nn", [BF16], name="fox_qkv")
    fraw = _mm(u, w["fox_f"], "nn", [F32], name="fox_fproj")
    cum, cum_rows = _fox_gate(fraw, w["fox_bf"], "fox_gate")
    ck_rows = _rows_to_tiles(cum_rows, t)
    qn, kn = _qk_norms(qkv, w["head_ind"], "fox_qk_norms")
    kb_start, qb_end = _skip_bounds(qn[:, :FOX_HEADS], kn[:, :FOX_HEADS], cum[:, :FOX_HEADS], t)
    o, lse, lse_rows = _flash_fwd(qkv, ck_rows, kb_start, "fox_flash_fwd")
    y = _mm(o, w["fox_o"], "nn", [F32], name="fox_oproj")
    return y, dict(u=u, qkv=qkv, fraw=fraw, cum=cum, ck_rows=ck_rows, o=o, lse=lse, lse_rows=lse_rows,
                   kb_start=kb_start, qb_end=qb_end)


def _fox_backward(dy, sv, w):
    s = dy.shape[0]
    t = min(ATT_TILE, s)
    do32, do16 = _mm(dy, w["fox_o"], "nt", [F32, BF16], name="fox_do")
    g_wo = _mm(sv["o"], dy, "tn", [F32], name="fox_gwo")
    delta, delta_rows = _head_rowsum(do32, sv["o"], w["head_ind"], "fox_delta")
    dq, drow = _flash_dq(sv["qkv"], do16, sv["ck_rows"], sv["lse"], delta, sv["kb_start"], "fox_flash_dq")
    dk, dv, dck = _flash_dkv(sv["qkv"], do16, sv["cum"], sv["lse_rows"], _rows_to_tiles(delta_rows, t),
                             sv["qb_end"], "fox_flash_dkv")
    df, db_f = _fox_gate_bwd(_tiles_to_cols(drow), _tiles_to_cols(dck), sv["fraw"], w["fox_bf"], "fox_gate_bwd")
    du = _mm(df, w["fox_f"], "nt", [F32], name="fox_du_f")
    du = _mm_nt_blocks(_k_blocks(dq) + _k_blocks(dk) + _k_blocks(dv), w["fox_qkv"], du, name="fox_du")
    g_win = jnp.concatenate([_mm(sv["u"], dq, "tn", [F32], name="fox_gwin_q"),
                             _mm(sv["u"], dk, "tn", [F32], name="fox_gwin_k"),
                             _mm(sv["u"], dv, "tn", [F32], name="fox_gwin_v"),
                             _mm(sv["u"], df, "tn", [F32], name="fox_gwin_f")[:, :FOX_HEADS]], axis=1)
    return du, dict(fox_w_in=g_win, fox_w_o=g_wo, fox_b_f=db_f[:, :FOX_HEADS])


def _conv_fwd(xpre, w8, b, name):
    s, c = xpre.shape
    tm, tc = min(ROW_TILE, s), min(1024, c)
    hb = tm // 8

    def body(x_ref, h_ref, w_ref, b_ref, xc_ref, xa_ref):
        i = pl.program_id(1)
        x = x_ref[...]
        halo = jnp.where(i > 0, h_ref[...], 0.0)
        w = w_ref[...]
        row = lax.broadcasted_iota(jnp.int32, (tm, tc), 0)
        row8 = lax.broadcasted_iota(jnp.int32, (8, tc), 0)
        acc = x * w[3:4] + b_ref[...]
        x8 = x[0:8]
        acc8 = x8 * w[3:4] + b_ref[...]
        for j in range(1, SSM_CONV):
            acc = acc + w[3 - j:4 - j] * pltpu.roll(x, j, 0)
            acc8 = acc8 + w[3 - j:4 - j] * jnp.where(row8 < j, pltpu.roll(halo, j, 0), pltpu.roll(x8, j, 0))
        xc_ref[...] = acc
        xc_ref[0:8, :] = acc8
        xc = xc_ref[...]
        xa_ref[...] = _silu(xc)

    tile = pl.BlockSpec((tm, tc), lambda jc, i: (i, jc))
    return _pcall(body, name=name, grid=(c // tc, s // tm),
                  in_specs=[tile, pl.BlockSpec((8, tc), lambda jc, i: (jnp.maximum(i * hb - 1, 0), jc)),
                            pl.BlockSpec((8, tc), lambda jc, i: (0, jc)), pl.BlockSpec((1, tc), lambda jc, i: (0, jc))],
                  out_specs=[tile, tile], out_shape=[_sds((s, c), F32), _sds((s, c), F32)],
                  sem=("parallel", "arbitrary"))(xpre, xpre, w8, b)


def _conv_bwd(dxa, xc, xpre, w8, name):
    s, c = xpre.shape
    tm, tc = min(ROW_TILE, s), min(1024, c)
    hb = tm // 8
    n = s // tm

    def body(d_ref, xc_ref, x_ref, xh_ref, dn_ref, xcn_ref, w_ref, dx_ref, dw_ref, db_ref, scr):
        i = pl.program_id(1)

        @pl.when(i == 0)
        def _():
            dw_ref[...] = jnp.zeros_like(dw_ref)
            db_ref[...] = jnp.zeros_like(db_ref)
        w = w_ref[...]
        x = x_ref[...]
        g = d_ref[...] * _dsilu(xc_ref[...])
        gn = jnp.where(i < n - 1, dn_ref[...] * _dsilu(xcn_ref[...]), 0.0)
        halo = jnp.where(i > 0, xh_ref[...], 0.0)
        row = lax.broadcasted_iota(jnp.int32, (tm, tc), 0)
        row8 = lax.broadcasted_iota(jnp.int32, (8, tc), 0)
        db_ref[...] += _colsum(g)
        dw_ref[3:4, :] += _colsum(g * x)
        g8 = g[0:8]
        acc = g * w[3:4]
        corr = jnp.zeros((8, tc), F32)
        for j in range(1, SSM_CONV):
            xs = pltpu.roll(x, j, 0)
            dwj = _colsum(jnp.where(row >= j, g * xs, 0.0))
            dwj = dwj + _colsum(jnp.where(row8 < j, g8 * pltpu.roll(halo, j, 0), 0.0))
            dw_ref[3 - j:4 - j, :] += dwj
            gs = pltpu.roll(g, tm - j, 0)
            acc = acc + w[3 - j:4 - j] * jnp.where(row < tm - j, gs, 0.0)
            corr = corr + w[3 - j:4 - j] * jnp.where(row8 >= 8 - j, pltpu.roll(gn, 8 - j, 0), 0.0)
        scr[...] = acc
        scr[tm - 8:tm, :] += corr
        dx_ref[...] = scr[...].astype(dx_ref.dtype)

    tile = pl.BlockSpec((tm, tc), lambda jc, i: (i, jc))
    prev8 = pl.BlockSpec((8, tc), lambda jc, i: (jnp.maximum(i * hb - 1, 0), jc))
    next8 = pl.BlockSpec((8, tc), lambda jc, i: (jnp.minimum((i + 1) * hb, n * hb - 1), jc))
    return _pcall(body, name=name, grid=(c // tc, n),
                  in_specs=[tile, tile, tile, prev8, next8, next8, pl.BlockSpec((8, tc), lambda jc, i: (0, jc))],
                  out_specs=[tile, pl.BlockSpec((8, tc), lambda jc, i: (0, jc)), pl.BlockSpec((1, tc), lambda jc, i: (0, jc))],
                  out_shape=[_sds((s, c), BF16), _sds((8, c), F32), _sds((1, c), F32)],
                  scratch=[pltpu.VMEM((tm, tc), F32)],
                  sem=("parallel", "arbitrary"))(dxa, xc, xpre, xpre, dxa, xc, w8)


def _ssd_pre(dtraw, dt_bias, a_log, cst, name):
    def fn(raw, bias, alog, expand):
        tm = raw.shape[0]
        z = raw + bias
        dt = jnp.maximum(z, 0.0) + jnp.log(1.0 + jnp.exp(-jnp.abs(z)))
        lane = lax.broadcasted_iota(jnp.int32, (tm, LANES), 1)
        pos = lax.broadcasted_iota(jnp.int32, (tm, LANES), 0) & (SSM_CHUNK - 1)
        dt = jnp.where(lane < SSM_HEADS, dt, 0.0)
        c = dt * (-jnp.exp(alog))
        sh = 1
        while sh < SSM_CHUNK:
            c = c + jnp.where(pos >= sh, pltpu.roll(c, sh, 0), 0.0)
            sh *= 2
        return dt, c, _dot_split(dt, expand), _dot_split(c, expand)
    wide = SSM_HEADS * HEAD_DIM
    return _rowwise(fn, [dtraw], [dt_bias, a_log, cst["expand"]],
                    [(LANES, F32), (LANES, F32), (wide, F32), (wide, F32)], [], name=name)


def _ssd_post(dacs, ddt, dtraw, dt, dt_bias, a_log, name):
    def fn(dacs_v, ddt_v, raw, dt_v, bias, alog):
        tm = raw.shape[0]
        pos = lax.broadcasted_iota(jnp.int32, (tm, LANES), 0) & (SSM_CHUNK - 1)
        a = -jnp.exp(alog)
        c = dacs_v
        sh = 1
        while sh < SSM_CHUNK:
            c = c + jnp.where(pos + sh < SSM_CHUNK, pltpu.roll(c, tm - sh, 0), 0.0)
            sh *= 2
        draw = (ddt_v + c * a) * _sigmoid(raw + bias)
        return draw, _colsum(draw), _colsum(c * dt_v * a)
    return _rowwise(fn, [dacs, ddt, dtraw, dt], [dt_bias, a_log], [(LANES, BF16)], [(1, LANES)] * 2, name=name)


def _heads_rows(x, s):
    return x[:, :SSM_HEADS].reshape(s // SSM_CHUNK, SSM_CHUNK, SSM_GROUPS, 4).transpose(2, 0, 3, 1)


def _ssd_constants():
    gp = SSD_GROUPS_PER_STEP
    src = np.arange(LANES)[:, None]
    expand = src == np.arange(SSM_HEADS * HEAD_DIM)[None, :] // HEAD_DIM
    dst = np.arange(LANES)[None, None, :] - 4 * np.arange(gp)[:, None, None]
    seg = np.arange(SSM_GROUP_WIDTH)[None, :, None] // HEAD_DIM == dst
    seg4 = np.arange(4 * LANES)[None, :, None] // LANES == dst
    return dict(expand=jnp.asarray(expand, BF16), seg=jnp.asarray(seg, BF16), seg4=jnp.asarray(seg4, BF16))


def _ssm_weights(w_in, conv_w, conv_b, dt_bias, a_log, d_skip, norm_w, w_out):
    pad = ((0, 0), (0, LANES - SSM_HEADS))
    w_xbc = _group_cols(w_in[:, 2048:6144])
    w_dt = jnp.pad(w_in[:, 6144:], pad)
    return dict(
        ssm_z=w_in[:, :2048], ssm_xbc=w_xbc, ssm_dt=w_dt,
        ssm_zx=jnp.concatenate([w_in[:, :2048], w_xbc], axis=1),
        ssm_out=w_out,
        conv_w8=_group_cols(jnp.pad(conv_w, ((0, 8 - SSM_CONV), (0, 0)))),
        conv_b=_group_cols(conv_b), norm_w=norm_w,
        dt_bias=jnp.pad(dt_bias, pad), a_log=jnp.pad(a_log, pad),
        d_e=jnp.repeat(d_skip.reshape(SSM_GROUPS, 4), HEAD_DIM, axis=1)[:, None, :],
        ssd_cst=_ssd_constants())


def _ssd_setup(acs_e, acsr):
    l = SSM_CHUNK
    last = acsr[:, l - 1:l]
    lane1 = lax.broadcasted_iota(jnp.int32, (1, SSM_GROUP_WIDTH), 1)
    last_e = last[3:4, :]
    for r in (2, 1, 0):
        last_e = jnp.where(lane1 < HEAD_DIM * (r + 1), last[r:r + 1, :], last_e)
    return jnp.exp(acs_e), jnp.exp(last_e - acs_e), jnp.exp(last_e)


def _head_bcast(acs_e):
    lo = lax.broadcasted_iota(jnp.int32, (acs_e.shape[0], LANES), 1) < HEAD_DIM
    out = []
    for p in range(2):
        blk = acs_e[:, p * LANES:(p + 1) * LANES]
        rolled = pltpu.roll(blk, HEAD_DIM, 1)
        out += [jnp.where(lo, blk, rolled), jnp.where(lo, rolled, blk)]
    return out


def _group_cols(a):
    lead = a.shape[:-1]
    x = a[..., :2048].reshape(lead + (SSM_GROUPS, SSM_GROUP_WIDTH))
    b = a[..., 2048:3072].reshape(lead + (SSM_GROUPS, SSM_STATE))
    c = a[..., 3072:].reshape(lead + (SSM_GROUPS, SSM_STATE))
    return jnp.concatenate([x, b, c], axis=-1).reshape(lead + (4096,))


def _ungroup_cols(a):
    lead = a.shape[:-1]
    y = a.reshape(lead + (SSM_GROUPS, SSM_GROUP_WIDTH + 2 * SSM_STATE))
    return jnp.concatenate([y[..., :256].reshape(lead + (2048,)), y[..., 256:384].reshape(lead + (1024,)),
                            y[..., 384:].reshape(lead + (1024,))], axis=-1)


def _ssd_fwd2(xa, dte, acse, acsr, d_e, name):
    s = xa.shape[0]
    l, gw, ns = SSM_CHUNK, SSM_GROUP_WIDTH, SSM_STATE
    nc = s // l

    gb = gw + 2 * ns
    gp = SSD_GROUPS_PER_STEP

    def body(xa_ref, dt_ref, acs_ref, acsr_ref, d_ref, y_ref, hp_ref, h_sc):
        @pl.when(pl.program_id(1) == 0)
        def _():
            h_sc[...] = jnp.zeros_like(h_sc)
        lane = lax.broadcasted_iota(jnp.int32, (l, gw), 1)
        tril = _causal(l)
        for gi in range(gp):
            x = xa_ref[:, gi * gb:gi * gb + gw]
            bm = xa_ref[:, gi * gb + gw:gi * gb + gw + ns].astype(BF16)
            cm = xa_ref[:, gi * gb + gw + ns:(gi + 1) * gb].astype(BF16)
            acsr = acsr_ref[gi]
            dt_e, acs_e = dt_ref[:, gi * gw:(gi + 1) * gw], acs_ref[:, gi * gw:(gi + 1) * gw]
            acs_bc = _head_bcast(acs_e)
            e_e, dte_e, cd_e = _ssd_setup(acs_e, acsr)
            xdt = x * dt_e
            xdt16 = xdt.astype(BF16)
            cb = _dot(cm, bm, NT_DIMS)
            yd = jnp.zeros((l, gw), F32)
            for r in range(4):
                lm = jnp.exp(jnp.where(tril, acs_bc[r] - acsr[r:r + 1, :], -jnp.inf))
                yr = _dot((cb * lm).astype(BF16), xdt16)
                yd = jnp.where((lane >= HEAD_DIM * r) & (lane < HEAD_DIM * (r + 1)), yr, yd)
            hp = h_sc[gi]
            hp_ref[gi] = hp
            y_ref[:, gi * gw:(gi + 1) * gw] = yd + _dot(cm, hp.astype(BF16)) * e_e + x * d_ref[gi]
            h_sc[gi] = hp * cd_e + _dot(bm, (xdt * dte_e).astype(BF16), TN_DIMS)

    return _pcall(
        body, name=name, grid=(SSM_GROUPS // gp, nc),
        in_specs=[pl.BlockSpec((l, gp * gb), lambda g, c: (c, g)),
                  pl.BlockSpec((l, gp * gw), lambda g, c: (c, g)),
                  pl.BlockSpec((l, gp * gw), lambda g, c: (c, g)),
                  pl.BlockSpec((gp, None, 4, l), lambda g, c: (g, c, 0, 0)),
                  pl.BlockSpec((gp, 1, gw), lambda g, c: (g, 0, 0))],
        out_specs=[pl.BlockSpec((l, gp * gw), lambda g, c: (c, g)),
                   pl.BlockSpec((gp, None, ns, gw), lambda g, c: (g, c, 0, 0))],
        out_shape=[_sds((s, 2048), F32), _sds((SSM_GROUPS, nc, ns, gw), F32)],
        scratch=[pltpu.VMEM((gp, ns, gw), F32)],
        sem=("parallel", "arbitrary"))(xa, dte, acse, acsr, d_e)


def _ssd_bwd2(dy, xa, dte, acse, acsr, d_e, hprev, cst, name):
    s = xa.shape[0]
    l, gw, ns = SSM_CHUNK, SSM_GROUP_WIDTH, SSM_STATE
    nc = s // l

    gb = gw + 2 * ns
    gp = SSD_GROUPS_PER_STEP

    def body(dy_ref, xa_ref, dt_ref, acs_ref, acsr_ref, d_ref, hp_ref,
             seg_ref, seg4_ref, dxa_ref, dacs_ref, ddt_ref, dd_ref, dh_sc):
        @pl.when(pl.program_id(1) == 0)
        def _():
            dh_sc[...] = jnp.zeros_like(dh_sc)
            dd_ref[...] = jnp.zeros_like(dd_ref)
        parts = [one_group(gi, dy_ref, xa_ref, dt_ref, acs_ref, acsr_ref, d_ref, hp_ref, seg_ref, seg4_ref,
                           dxa_ref, dh_sc) for gi in range(gp)]
        dacs_ref[...] = _sum_list([p[0] for p in parts])
        ddt_ref[...] = _sum_list([p[1] for p in parts])
        dd_ref[0:1, :] += _sum_list([p[2] for p in parts])

    def one_group(gi, dy_ref, xa_ref, dt_ref, acs_ref, acsr_ref, d_ref, hp_ref, seg_ref, seg4_ref, dxa_ref, dh_sc):
        dyv = dy_ref[:, gi * gw:(gi + 1) * gw]
        x = xa_ref[:, gi * gb:gi * gb + gw]
        bm = xa_ref[:, gi * gb + gw:gi * gb + gw + ns].astype(BF16)
        cm = xa_ref[:, gi * gb + gw + ns:(gi + 1) * gb].astype(BF16)
        acsr = acsr_ref[gi]
        dt_e, acs_e = dt_ref[:, gi * gw:(gi + 1) * gw], acs_ref[:, gi * gw:(gi + 1) * gw]
        acs_bc = _head_bcast(acs_e)
        e_e, dte_e, cd_e = _ssd_setup(acs_e, acsr)
        seg, seg4 = seg_ref[gi], seg4_ref[gi]
        lane = lax.broadcasted_iota(jnp.int32, (l, gw), 1)
        xdt = x * dt_e
        xdt16 = xdt.astype(BF16)
        dy16 = dyv.astype(BF16)
        cb = _dot(cm, bm, NT_DIMS)
        cbt = _dot(bm, cm, NT_DIMS)
        hp = hp_ref[gi]
        hp16 = hp.astype(BF16)
        g = dh_sc[gi]
        g16 = g.astype(BF16)
        t_all = _dot(cm, hp16)
        dt16 = (dyv * e_e).astype(BF16)
        dc = _dot(dt16, hp16, NT_DIMS)
        dhp = _dot(cm, dt16, TN_DIMS)
        wv = xdt * dte_e
        dw = _dot(bm, g16)
        db = _dot(wv.astype(BF16), g16, NT_DIMS)
        dxdt = dw * dte_e
        acs_term = dyv * t_all * e_e - dw * wv
        last_term = _colsum(dw * wv) + _colsum(g * hp) * cd_e
        dh_sc[gi] = g * cd_e + dhp
        tril, triu = _causal(l), _causal(l, transposed=True)
        dcb = jnp.zeros((l, l), F32)
        dcbt = jnp.zeros((l, l), F32)
        qd = []
        for r in range(4):
            in_head = (lane >= HEAD_DIM * r) & (lane < HEAD_DIM * (r + 1))
            a_col = acs_bc[r]
            lm = jnp.exp(jnp.where(tril, a_col - acsr[r:r + 1, :], -jnp.inf))
            lmt = jnp.exp(jnp.where(triu, acsr[r:r + 1, :] - a_col, -jnp.inf))
            mm_, mt = cb * lm, cbt * lmt
            dyr = jnp.where(in_head, dy16, jnp.zeros_like(dy16))
            dm = _dot(dyr, xdt16, NT_DIMS)
            dmt = _dot(xdt16, dyr, NT_DIMS)
            dxdt = dxdt + jnp.where(in_head, _dot(mt.astype(BF16), dy16), 0.0)
            dcb = dcb + dm * lm
            dcbt = dcbt + dmt * lmt
            qd.append(dm * mm_ - dmt * mt)
        dc = dc + _dot(dcb.astype(BF16), bm)
        db = db + _dot(dcbt.astype(BF16), cm)
        rowl = lax.broadcasted_iota(jnp.int32, (l, LANES), 0)
        row8 = lax.broadcasted_iota(jnp.int32, (8, gw), 0)
        small = _dot_split(jnp.where(row8 == 0, last_term, jnp.where(row8 == 1, _colsum(dyv * x), 0.0)), seg, passes=2)
        big = _dot_split(jnp.concatenate([acs_term, dxdt * x], axis=0), seg, passes=2)
        dacs = (big[0:l] + _dot_split(jnp.concatenate(qd, axis=1), seg4, passes=2)
                + jnp.where(rowl == l - 1, small[0:1, :], 0.0))
        dxa_ref[:, gi * gb:(gi + 1) * gb] = jnp.concatenate([dxdt * dt_e + dyv * d_ref[gi], db, dc], axis=1)
        return dacs, big[l:2 * l], small[1:2, :]

    rc = lambda c: nc - 1 - c
    ng = SSM_GROUPS // gp
    return _pcall(
        body, name=name, grid=(ng, nc),
        in_specs=[pl.BlockSpec((l, gp * gw), lambda g, c: (rc(c), g)),
                  pl.BlockSpec((l, gp * gb), lambda g, c: (rc(c), g)),
                  pl.BlockSpec((l, gp * gw), lambda g, c: (rc(c), g)),
                  pl.BlockSpec((l, gp * gw), lambda g, c: (rc(c), g)),
                  pl.BlockSpec((gp, None, 4, l), lambda g, c: (g, rc(c), 0, 0)),
                  pl.BlockSpec((gp, 1, gw), lambda g, c: (g, 0, 0)),
                  pl.BlockSpec((gp, None, ns, gw), lambda g, c: (g, rc(c), 0, 0)),
                  pl.BlockSpec((gp, gw, LANES), lambda g, c: (0, 0, 0)),
                  pl.BlockSpec((gp, 4 * LANES, LANES), lambda g, c: (0, 0, 0))],
        out_specs=[pl.BlockSpec((l, gp * gb), lambda g, c: (rc(c), g)),
                   pl.BlockSpec((None, l, LANES), lambda g, c: (g, rc(c), 0)),
                   pl.BlockSpec((None, l, LANES), lambda g, c: (g, rc(c), 0)),
                   pl.BlockSpec((None, 8, LANES), lambda g, c: (g, 0, 0))],
        out_shape=[_sds((s, 4096), F32), _sds((ng, s, LANES), F32), _sds((ng, s, LANES), F32),
                   _sds((ng, 8, LANES), F32)],
        scratch=[pltpu.VMEM((gp, ns, gw), F32)],
        sem=("parallel", "arbitrary"))(dy, xa, dte, acse, acsr, d_e, hprev, cst["seg"], cst["seg4"])


def _gate_norm(y, z, nw, name):
    c = y.shape[1]

    def fn(yv, zv, w):
        outs = []
        for k in range(c // SSM_GROUP_WIDTH):
            sl = slice(k * SSM_GROUP_WIDTH, (k + 1) * SSM_GROUP_WIDTH)
            yg = yv[:, sl] * _silu(zv[:, sl])
            rinv = lax.rsqrt(jnp.mean(yg * yg, axis=-1, keepdims=True) + RMS_EPS)
            outs.append(yg * rinv * w[:, sl])
        return (jnp.concatenate(outs, axis=1),)
    return _rowwise(fn, [y, z], [nw], [(c, BF16)], [], name=name)[0]


def _out_gate_norm_bwd(dy, w_out, y, z, nw, name):
    s, c = y.shape
    tm, tn = min(512, s), min(1024, c)
    k = dy.shape[1]

    def body(a_ref, b_ref, y_ref, z_ref, w_ref, dy_ref, dz_ref, dw_ref):
        @pl.when(pl.program_id(1) == 0)
        def _():
            dw_ref[...] = jnp.zeros_like(dw_ref)
        dv = _dot(a_ref[...], b_ref[...], NT_DIMS)
        yv, zv, w = y_ref[...], z_ref[...], w_ref[...]
        dys, dzs, dws = [], [], []
        for g in range(tn // SSM_GROUP_WIDTH):
            sl = slice(g * SSM_GROUP_WIDTH, (g + 1) * SSM_GROUP_WIDTH)
            ys, zs, ds = yv[:, sl], zv[:, sl], dv[:, sl]
            sz = _silu(zs)
            yg = ys * sz
            rinv = lax.rsqrt(jnp.mean(yg * yg, axis=-1, keepdims=True) + RMS_EPS)
            nrm = yg * rinv
            dn = ds * w[:, sl]
            dyg = rinv * (dn - nrm * jnp.mean(dn * nrm, axis=-1, keepdims=True))
            dys.append(dyg * sz)
            dzs.append(dyg * ys * _dsilu(zs))
            dws.append(_colsum(ds * nrm))
        dy_ref[...] = jnp.concatenate(dys, axis=1)
        dz_ref[...] = jnp.concatenate(dzs, axis=1).astype(dz_ref.dtype)
        dw_ref[...] += jnp.concatenate(dws, axis=1)

    tile = pl.BlockSpec((tm, tn), lambda j, i: (i, j))
    row = pl.BlockSpec((1, tn), lambda j, i: (0, j))
    return _pcall(body, name=name, grid=(c // tn, s // tm),
                  in_specs=[pl.BlockSpec((tm, k), lambda j, i: (i, 0)), pl.BlockSpec((tn, k), lambda j, i: (j, 0)),
                            tile, tile, row],
                  out_specs=[tile, tile, row], out_shape=[_sds((s, c), F32), _sds((s, c), BF16), _sds((1, c), F32)],
                  sem=("parallel", "arbitrary"))(dy, w_out, y, z, nw)


def _ssd_forward(u, w):
    s = u.shape[0]
    z = _mm(u, w["ssm_z"], "nn", [F32], name="ssm_zproj")
    xpre = _mm(u, w["ssm_xbc"], "nn", [F32], name="ssm_xproj")
    dtraw = _mm(u, w["ssm_dt"], "nn", [F32], name="ssm_dtproj")
    xc, xa = _conv_fwd(xpre, w["conv_w8"], w["conv_b"], "ssm_conv")
    dt, acs, dte, acse = _ssd_pre(dtraw, w["dt_bias"], w["a_log"], w["ssd_cst"], "ssm_pre")
    acsr = _heads_rows(acs, s)
    y, hprev = _ssd_fwd2(xa, dte, acse, acsr, w["d_e"], "ssm_scan")
    yn = _gate_norm(y, z, w["norm_w"], "ssm_gate_norm")
    out = _mm(yn, w["ssm_out"], "nn", [F32], name="ssm_oproj")
    return out, dict(u=u, z=z, xpre=xpre, xc=xc, xa=xa, dtraw=dtraw, dt=dt, dte=dte, acse=acse,
                     acsr=acsr, y=y, hprev=hprev, yn=yn)


def _ssd_backward(dy, sv, w):
    s = dy.shape[0]
    g_wout = _mm(sv["yn"], dy, "tn", [F32], name="ssm_gwout")
    dys, dz, dnw = _out_gate_norm_bwd(dy, w["ssm_out"], sv["y"], sv["z"], w["norm_w"], "ssm_dyn_gate_norm_bwd")
    dxa, dacs_c, ddt_c, dd = _ssd_bwd2(dys, sv["xa"], sv["dte"], sv["acse"], sv["acsr"], w["d_e"], sv["hprev"],
                                       w["ssd_cst"], "ssm_scan_bwd")
    per_step = 4 * SSD_GROUPS_PER_STEP
    pad = ((0, 0), (0, LANES - SSM_HEADS))
    dacs = jnp.pad(jnp.concatenate([a[:, :per_step] for a in dacs_c], axis=1), pad)
    ddt = jnp.pad(jnp.concatenate([a[:, :per_step] for a in ddt_c], axis=1), pad)
    draw, dbias, dalog = _ssd_post(dacs, ddt, sv["dtraw"], sv["dt"], w["dt_bias"], w["a_log"], "ssm_post")
    dxpre, dcw, dcb = _conv_bwd(dxa, sv["xc"], sv["xpre"], w["conv_w8"], "ssm_conv_bwd")
    du = _mm(draw, w["ssm_dt"], "nt", [F32], name="ssm_du_dt")
    n_z = dz.shape[1]
    du = _mm_nt_blocks(_k_blocks(dz), w["ssm_zx"][:, :n_z], du, name="ssm_du_z")
    du = _mm_nt_blocks(_k_blocks(dxpre), w["ssm_zx"][:, n_z:], du, name="ssm_du_x")
    g_win = jnp.concatenate([_mm(sv["u"], dz, "tn", [F32], name="ssm_gwin_z"),
                             _ungroup_cols(_mm(sv["u"], dxpre, "tn", [F32], name="ssm_gwin_x")),
                             _mm(sv["u"], draw, "tn", [F32], name="ssm_gwin_dt")[:, :SSM_HEADS]], axis=1)
    return du, dict(ssm_w_in=g_win, ssm_w_out=g_wout, ssm_conv_w=_ungroup_cols(dcw[:SSM_CONV]),
                    ssm_conv_b=_ungroup_cols(dcb), ssm_norm_w=dnw, ssm_dt_bias=dbias[:, :SSM_HEADS],
                    ssm_a_log=dalog[:, :SSM_HEADS], ssm_d=dd[:, 0, :per_step].reshape(1, SSM_HEADS))


def _mlp_forward(u2, w1, w2, tag):
    def epi(acc):
        hr = jnp.maximum(acc, 0.0)
        return hr, hr * hr
    hr, a = _mm(u2, w1, "nn", [BF16, BF16], name=tag + "_mlp_up", epi=epi)
    y2 = _mm(a, w2, "nn", [F32], name=tag + "_mlp_down")
    return y2, hr, a


def _mlp_backward(dy2, u2, hr, a, w1, w2, tag):
    dh = _mm(dy2, w2, "nt", [BF16], name=tag + "_mlp_dh", extra=(hr,),
             epi=lambda acc, h: (acc * (2.0 * h.astype(F32)),))
    g_w2 = _mm(a, dy2, "tn", [F32], name=tag + "_mlp_gw2")
    g_w1 = _mm(u2, dh, "tn", [F32], name=tag + "_mlp_gw1")
    du2 = _mm(dh, w1, "nt", [F32], name=tag + "_mlp_du")
    return du2, g_w1, g_w2


def _ada_forward(c16, ada_w, ada_b_cols, name):
    nl, d, cols = ada_w.shape
    tn = 512

    def body(c_ref, w_ref, b_ref, o_ref):
        cond = _silu(c_ref[...]).astype(BF16)
        o_ref[...] = _dot(cond, w_ref[...].astype(BF16)) + b_ref[...]

    return _pcall(body, name=name, grid=(nl, cols // tn),
                  in_specs=[pl.BlockSpec((16, d), lambda i, j: (0, 0)),
                            pl.BlockSpec((None, d, tn), lambda i, j: (i, 0, j)),
                            pl.BlockSpec((None, 1, tn), lambda i, j: (i, 0, j))],
                  out_specs=pl.BlockSpec((None, 16, tn), lambda i, j: (i, 0, j)),
                  out_shape=_sds((nl, 16, cols), F32), sem=("parallel", "parallel"))(c16, ada_w, ada_b_cols)


def _ada_backward(c_t, dmod_cols, name):
    d, nb = c_t.shape
    nl, _, cols = dmod_cols.shape
    tn = 512

    def body(c_ref, dm_ref, o_ref):
        cond = _silu(c_ref[...])
        dm = dm_ref[...]
        acc = cond[:, 0:1] * dm[0:1, :]
        for b in range(1, nb):
            acc = acc + cond[:, b:b + 1] * dm[b:b + 1, :]
        o_ref[...] = acc

    return _pcall(body, name=name, grid=(nl, cols // tn),
                  in_specs=[pl.BlockSpec((d, nb), lambda i, j: (0, 0)),
                            pl.BlockSpec((None, nb, tn), lambda i, j: (i, 0, j))],
                  out_specs=pl.BlockSpec((None, d, tn), lambda i, j: (i, 0, j)),
                  out_shape=_sds((nl, d, cols), F32), sem=("parallel", "parallel"))(c_t, dmod_cols)


def _adamw(w, g, m, v, name):
    rows, cols = w.shape
    tm = rows
    for cand in (256, 128, 64, 32, 16, 8):
        if rows % cand == 0 and rows > cand:
            tm = cand
            break
    c1 = 1.0 / (1.0 - ADAM_B1 ** ADAM_STEP)
    c2 = 1.0 / (1.0 - ADAM_B2 ** ADAM_STEP)

    def fn(wv, gv, mv, vv):
        mn = ADAM_B1 * mv + (1.0 - ADAM_B1) * gv
        vn = ADAM_B2 * vv + (1.0 - ADAM_B2) * (gv * gv)
        delta = -ADAM_LR * ((mn * c1) / (jnp.sqrt(vn * c2) + ADAM_EPS) + ADAM_WD * wv)
        return delta, mn, vn
    return _rowwise(fn, [w, g, m, v], [], [(cols, F32)] * 3, [], name=name, tm=tm)


def _my_pos():
    return lax.axis_index("x"), lax.axis_index("y"), lax.axis_index("c")


def _allgather8(x, name):
    r, c = x.shape

    def body(x_ref, out_ref, send_sems, recv_sems, local_sem):
        mx, my, mc = _my_pos()
        me = 4 * mx + 2 * my + mc
        mine = pltpu.make_async_copy(x_ref, out_ref.at[me], local_sem)
        mine.start()
        copies = []
        for k in range(1, 8):
            fx, fy, fc = (k >> 2) & 1, (k >> 1) & 1, k & 1
            px = 1 - mx if fx else mx
            py = 1 - my if fy else my
            pc = 1 - mc if fc else mc
            peer = 4 * px + 2 * py + pc
            send = pltpu.make_async_remote_copy(src_ref=x_ref, dst_ref=out_ref.at[me], send_sem=send_sems.at[k - 1],
                                                recv_sem=recv_sems.at[k - 1], device_id=(px, py, pc),
                                                device_id_type=MESH)
            send.start()
            recv = pltpu.make_async_remote_copy(src_ref=x_ref, dst_ref=out_ref.at[peer], send_sem=send_sems.at[k - 1],
                                                recv_sem=recv_sems.at[k - 1], device_id=(px, py, pc),
                                                device_id_type=MESH)
            copies.append((send, recv))
        for send, recv in copies:
            recv.wait_recv()
        for send, recv in copies:
            send.wait_send()
        mine.wait()

    vm = pl.BlockSpec(memory_space=pltpu.VMEM)
    return _pcall(body, name=name, in_specs=[vm], out_specs=vm, out_shape=_sds((8, r, c), x.dtype),
                  scratch=[pltpu.SemaphoreType.DMA((7,)), pltpu.SemaphoreType.DMA((7,)), pltpu.SemaphoreType.DMA])(x)


def _chip_flips(mx, my):
    out = []
    for fx, fy in ((1, 0), (0, 1), (1, 1)):
        px = 1 - mx if fx else mx
        py = 1 - my if fy else my
        out.append((px, py, 2 * px + py))
    return out


def _gather_chips(shard2, name):
    _, h, c = shard2.shape

    def body(x_ref, out_ref, send_sems, recv_sems):
        mx, my, mc = _my_pos()
        oc = 1 - mc
        mk = 2 * mx + my
        flips = _chip_flips(mx, my)

        def copy(k, src, dst, to):
            return pltpu.make_async_remote_copy(src_ref=src, dst_ref=dst, send_sem=send_sems.at[k],
                                                recv_sem=recv_sems.at[k], device_id=to, device_id_type=MESH)

        first = [copy(j, x_ref.at[mc], out_ref.at[mk, mc], (px, py, mc)) for j, (px, py, pk) in enumerate(flips)]
        for cp in first:
            cp.start()
        passed = []
        for j, (px, py, pk) in enumerate(flips):
            copy(j, x_ref.at[mc], out_ref.at[pk, mc], (px, py, mc)).wait_recv()
            fw = copy(3 + j, out_ref.at[pk, mc], out_ref.at[pk, mc], (mx, my, oc))
            fw.start()
            passed.append(fw)
        for j, (px, py, pk) in enumerate(flips):
            copy(3 + j, out_ref.at[pk, oc], out_ref.at[pk, oc], (mx, my, oc)).wait_recv()
        for cp in first + passed:
            cp.wait_send()

    return _pcall(body, name=name, in_specs=[HBM_SPEC], out_specs=HBM_SPEC, out_shape=_sds((4, 2, h, c), shard2.dtype),
                  scratch=[pltpu.SemaphoreType.DMA((6,)), pltpu.SemaphoreType.DMA((6,))])(shard2)


def _pair_exchange(g4, name):
    n, _, h, c = g4.shape

    def body(g_ref, out_ref, send_sem, recv_sem):
        mx, my, mc = _my_pos()
        oc = 1 - mc
        copies = []
        for k in range(n):
            cp = pltpu.make_async_remote_copy(src_ref=g_ref.at[k, oc], dst_ref=out_ref.at[k], send_sem=send_sem.at[k],
                                              recv_sem=recv_sem.at[k], device_id=(mx, my, oc), device_id_type=MESH)
            cp.start()
            copies.append(cp)
        for cp in copies:
            cp.wait_recv()
        for cp in copies:
            cp.wait_send()

    return _pcall(body, name=name, in_specs=[HBM_SPEC], out_specs=HBM_SPEC, out_shape=_sds((n, h, c), g4.dtype),
                  scratch=[pltpu.SemaphoreType.DMA((n,)), pltpu.SemaphoreType.DMA((n,))])(g4)


def _pair_add(g4, recv, core, name):
    n, _, h, c = g4.shape
    tm = _row_tile(h)

    def body(core_ref, a_ref, b_ref, o_ref, o16_ref):
        acc = a_ref[...] + b_ref[...]
        o_ref[...] = acc
        o16_ref[...] = acc.astype(BF16)

    out_spec = pl.BlockSpec((None, tm, c), lambda k, i, cr: (k, i, 0))
    return _pcall(body, name=name, grid=(n, h // tm), prefetch=1,
                  in_specs=[pl.BlockSpec((None, None, tm, c), lambda k, i, cr: (k, cr[0], i, 0)), out_spec],
                  out_specs=[out_spec, out_spec], out_shape=[_sds((n, h, c), F32), _sds((n, h, c), BF16)],
                  sem=("parallel", "parallel"))(core, g4, recv)


def _chip_exchange(p, name):
    n, h, c = p.shape

    def body(p_ref, out_ref, send_sems, recv_sems):
        mx, my, mc = _my_pos()
        copies = []
        for j, (px, py, pk) in enumerate(_chip_flips(mx, my)):
            cp = pltpu.make_async_remote_copy(src_ref=p_ref.at[pk], dst_ref=out_ref.at[j], send_sem=send_sems.at[j],
                                              recv_sem=recv_sems.at[j], device_id=(px, py, mc), device_id_type=MESH)
            cp.start()
            copies.append(cp)
        for cp in copies:
            cp.wait_recv()
        for cp in copies:
            cp.wait_send()

    return _pcall(body, name=name, in_specs=[HBM_SPEC], out_specs=HBM_SPEC, out_shape=_sds((3, h, c), p.dtype),
                  scratch=[pltpu.SemaphoreType.DMA((3,)), pltpu.SemaphoreType.DMA((3,))])(p)


def _chip_sum(p, slots, chip, name):
    _, h, c = p.shape
    tm = _row_tile(h)

    def body(chip_ref, p_ref, q_ref, o_ref):
        o_ref[...] = ((p_ref[...] + q_ref[0].astype(F32)) + q_ref[1].astype(F32)) + q_ref[2].astype(F32)

    return _pcall(body, name=name, grid=(h // tm,), prefetch=1,
                  in_specs=[pl.BlockSpec((None, tm, c), lambda i, ch: (ch[0], i, 0)),
                            pl.BlockSpec((3, tm, c), lambda i, ch: (0, i, 0))],
                  out_specs=pl.BlockSpec((tm, c), lambda i, ch: (i, 0)),
                  out_shape=_sds((h, c), F32), sem=("parallel",))(chip, p, slots)


def _sum_slots(q, name):
    n, h, c = q.shape
    tm = _row_tile(h)

    def body(q_ref, o_ref):
        acc = q_ref[0]
        for k in range(1, n):
            acc = acc + q_ref[k]
        o_ref[...] = acc

    return _pcall(body, name=name, grid=(h // tm,),
                  in_specs=[pl.BlockSpec((n, tm, c), lambda i: (0, i, 0))],
                  out_specs=pl.BlockSpec((tm, c), lambda i: (i, 0)),
                  out_shape=_sds((h, c), F32), sem=("parallel",))(q)


def _pair_share(f, name):
    h, c = f.shape

    def body(f_ref, out_ref, send_sem, recv_sem):
        mx, my, mc = _my_pos()
        cp = pltpu.make_async_remote_copy(src_ref=f_ref, dst_ref=out_ref, send_sem=send_sem, recv_sem=recv_sem,
                                          device_id=(mx, my, 1 - mc), device_id_type=MESH)
        cp.start()
        cp.wait_recv()
        cp.wait_send()

    return _pcall(body, name=name, in_specs=[HBM_SPEC], out_specs=HBM_SPEC, out_shape=_sds((h, c), f.dtype),
                  scratch=[pltpu.SemaphoreType.DMA, pltpu.SemaphoreType.DMA])(f)


BIG = ("mlp_w1", "mlp_w2", "fox_w_in", "fox_w_o", "ssm_w_in", "ssm_w_out")
SMALL_SHARDED = ("ssm_conv_w", "ssm_conv_b", "ssm_norm_w")
PACK_COLS = 1024


def _pack_rows(parts, rows_multiple, dtype):
    flat = jnp.concatenate([p.reshape(-1).astype(dtype) for p in parts])
    unit = rows_multiple * PACK_COLS
    total = -(-flat.shape[0] // unit) * unit
    flat = jnp.pad(flat, (0, total - flat.shape[0]))
    return flat.reshape(total // PACK_COLS, PACK_COLS)


def _unpack(flat, shapes):
    out, off = [], 0
    for sh in shapes:
        n = 1
        for d_ in sh:
            n *= d_
        out.append(flat[off:off + n].reshape(sh))
        off += n
    return out


PIECE_ROWS = 16


def _piece_rows(shape):
    n = 1
    for d_ in shape:
        n *= d_
    rows = -(-n // PACK_COLS)
    return n, -(-rows // PIECE_ROWS) * PIECE_ROWS


def _pack2d(parts, rows_multiple, dtype):
    blocks = []
    for p in parts:
        n, rows = _piece_rows(p.shape)
        a = p.astype(dtype)
        if p.shape[-1] != PACK_COLS or n % PACK_COLS:
            a = jnp.pad(a.reshape(-1), (0, -n % PACK_COLS))
        a = a.reshape(-1, PACK_COLS)
        blocks.append(jnp.pad(a, ((0, rows - a.shape[0]), (0, 0))))
    total = sum(b.shape[0] for b in blocks)
    pad = -total % rows_multiple
    if pad:
        blocks.append(jnp.zeros((pad, PACK_COLS), dtype))
    return jnp.concatenate(blocks, axis=0)


def _unpack2d(buf, shapes):
    out, off = [], 0
    for sh in shapes:
        n, rows = _piece_rows(sh)
        piece = buf[off:off + rows]
        if sh[-1] == PACK_COLS and n % PACK_COLS == 0:
            out.append(piece[:n // PACK_COLS].reshape(sh))
        else:
            out.append(piece.reshape(-1)[:n].reshape(sh))
        off += rows
    return out


def _row_tile(h, cap=512):
    for step in (16, 8):
        best = 0
        for cand in range(step, cap + 1, step):
            if h % cand == 0:
                best = cand
        if best:
            return best
    return h


def _chip_slice(full, axis, k, width):
    idx = [slice(None)] * full.ndim
    idx[axis] = slice(k * width, (k + 1) * width)
    return full[tuple(idx)]


SHARD_AXIS = dict(mlp_w1=2, mlp_w2=1, fox_w_in=2, fox_w_o=1, ssm_w_in=2, ssm_w_out=1, ssm_conv_w=2,
                  ssm_conv_b=1, ssm_norm_w=1, ada_w=2)


def kernel(x, c, ada_w, ada_b, ln_mix_g, ln_mix_b, ln_mlp_g, ln_mlp_b, mlp_w1, mlp_w2, fox_w_in, fox_b_f, fox_w_o, ssm_w_in, ssm_conv_w, ssm_conv_b, ssm_dt_bias, ssm_a_log, ssm_d, ssm_norm_w, ssm_w_out, loss_target, m_ada_w, m_ada_b, m_ln_mix_g, m_ln_mix_b, m_ln_mlp_g, m_ln_mlp_b, m_mlp_w1, m_mlp_w2, m_fox_w_in, m_fox_b_f, m_fox_w_o, m_ssm_w_in, m_ssm_conv_w, m_ssm_conv_b, m_ssm_dt_bias, m_ssm_a_log, m_ssm_d, m_ssm_norm_w, m_ssm_w_out, v_ada_w, v_ada_b, v_ln_mix_g, v_ln_mix_b, v_ln_mlp_g, v_ln_mlp_b, v_mlp_w1, v_mlp_w2, v_fox_w_in, v_fox_b_f, v_fox_w_o, v_ssm_w_in, v_ssm_conv_w, v_ssm_conv_b, v_ssm_dt_bias, v_ssm_a_log, v_ssm_d, v_ssm_norm_w, v_ssm_w_out):
    names = ("ada_w", "ada_b", "ln_mix_g", "ln_mix_b", "ln_mlp_g", "ln_mlp_b", "mlp_w1", "mlp_w2", "fox_w_in",
             "fox_b_f", "fox_w_o", "ssm_w_in", "ssm_conv_w", "ssm_conv_b", "ssm_dt_bias", "ssm_a_log", "ssm_d",
             "ssm_norm_w", "ssm_w_out")
    weights = dict(zip(names, (ada_w, ada_b, ln_mix_g, ln_mix_b, ln_mlp_g, ln_mlp_b, mlp_w1, mlp_w2, fox_w_in,
                               fox_b_f, fox_w_o, ssm_w_in, ssm_conv_w, ssm_conv_b, ssm_dt_bias, ssm_a_log, ssm_d,
                               ssm_norm_w, ssm_w_out)))
    m_in = dict(zip(names, (m_ada_w, m_ada_b, m_ln_mix_g, m_ln_mix_b, m_ln_mlp_g, m_ln_mlp_b, m_mlp_w1, m_mlp_w2,
                            m_fox_w_in, m_fox_b_f, m_fox_w_o, m_ssm_w_in, m_ssm_conv_w, m_ssm_conv_b, m_ssm_dt_bias,
                            m_ssm_a_log, m_ssm_d, m_ssm_norm_w, m_ssm_w_out)))
    v_in = dict(zip(names, (v_ada_w, v_ada_b, v_ln_mix_g, v_ln_mix_b, v_ln_mlp_g, v_ln_mlp_b, v_mlp_w1, v_mlp_w2,
                            v_fox_w_in, v_fox_b_f, v_fox_w_o, v_ssm_w_in, v_ssm_conv_w, v_ssm_conv_b, v_ssm_dt_bias,
                            v_ssm_a_log, v_ssm_d, v_ssm_norm_w, v_ssm_w_out)))

    mx, my, mc = _my_pos()
    chip = 2 * mx + my
    me = 4 * mx + 2 * my + mc
    x0 = x[0]
    target = loss_target[0]
    s, d = x0.shape
    n_qkv = 3 * FOX_HEADS * HEAD_DIM

    big_shapes = [weights[n].shape for n in BIG]
    packed = _pack2d([weights[n] for n in BIG], 32, BF16)
    gathered = _gather_chips(packed.reshape(2, packed.shape[0] // 2, PACK_COLS), "gather_weights")
    gathered = gathered.reshape(4, packed.shape[0], PACK_COLS)
    per_chip = [_unpack2d(jnp.where(chip == k, packed, gathered[k]), big_shapes) for k in range(4)]
    full = {n: jnp.concatenate([per_chip[k][i] for k in range(4)], axis=SHARD_AXIS[n]) for i, n in enumerate(BIG)}

    small_shapes = [weights[n].shape for n in SMALL_SHARDED]
    small_packed = _pack_rows([weights[n] for n in SMALL_SHARDED] + [c], 8, F32).reshape(-1, LANES)
    small_all = _allgather8(small_packed, "gather_small")
    small_chip = [_unpack(small_all[2 * k].reshape(-1), small_shapes) for k in range(4)]
    small_full = {n: jnp.concatenate([small_chip[k][i] for k in range(4)], axis=SHARD_AXIS[n])
                  for i, n in enumerate(SMALL_SHARDED)}
    n_small = sum(weights[n].size for n in SMALL_SHARDED)
    c_all = small_all.reshape(8, -1)[:, n_small:n_small + d]

    cols = ada_w.shape[2]
    ada_b_cols = lax.dynamic_slice_in_dim(ada_b, chip * cols, cols, axis=1)[:, None, :]
    c16 = jnp.pad(c_all, ((0, 8), (0, 0)))
    mod_part = _ada_forward(c16, ada_w, ada_b_cols, "ada_fwd")[:, :8, :]
    mod_all = _allgather8(mod_part.reshape(-1, LANES), "gather_mod").reshape(8, DEPTH, 8, cols)
    mod_mine = jnp.stack([lax.dynamic_index_in_dim(mod_all[2 * k], me, axis=1, keepdims=False) for k in range(4)], axis=1)
    mods = [jnp.pad(mod_mine[i].reshape(6, d), ((0, 2), (0, 0))) for i in range(DEPTH)]

    w = dict(
        fox_qkv=full["fox_w_in"][0][:, :n_qkv],
        fox_f=jnp.pad(full["fox_w_in"][0][:, n_qkv:], ((0, 0), (0, LANES - FOX_HEADS))),
        fox_o=full["fox_w_o"][0],
        fox_bf=jnp.pad(fox_b_f, ((0, 0), (0, LANES - FOX_HEADS))),
        head_ind=jnp.asarray(np.arange(d)[:, None] // HEAD_DIM == np.arange(LANES)[None, :], BF16),
    )
    w.update(_ssm_weights(full["ssm_w_in"][0], small_full["ssm_conv_w"][0], small_full["ssm_conv_b"], ssm_dt_bias,
                          ssm_a_log, ssm_d, small_full["ssm_norm_w"], full["ssm_w_out"][0]))
    mixers = ((_fox_forward, _fox_backward), (_ssd_forward, _ssd_backward))

    saved = []
    xin = x0
    u = _modulate_in(x0, mods[0], "l0_mod_in")
    for i in range(DEPTH):
        tag = "l%d" % i
        y, sv = mixers[i % 2][0](u, w)
        r, x1, u2 = _res_ln_mod(xin, y, mods[i], ln_mix_g[i:i + 1], ln_mix_b[i:i + 1], tag + "_res_ln1")
        y2, hr, a = _mlp_forward(u2, full["mlp_w1"][i], full["mlp_w2"][i], tag)
        if i + 1 < DEPTH:
            r2, xin, u = _res_ln_next(x1, y2, mods[i], ln_mlp_g[i:i + 1], ln_mlp_b[i:i + 1], mods[i + 1],
                                      tag + "_res_ln2")
        else:
            r2 = None
        saved.append(dict(y=y, r=r, u2=u2, hr=hr, a=a, y2=y2, r2=r2, x1=x1, mix=sv))

    grads = {}
    dmod_parts = [dict() for _ in range(DEPTH)]
    ln_grads = {n: [None] * DEPTH for n in ("ln_mix_g", "ln_mix_b", "ln_mlp_g", "ln_mlp_b")}
    g_w1, g_w2 = [None] * DEPTH, [None] * DEPTH
    du = dres0 = None
    for i in reversed(range(DEPTH)):
        tag = "l%d" % i
        sv = saved[i]
        if i + 1 == DEPTH:
            dy2, dres, sq, dg2, db2, dgm = _loss_ln2_bwd(sv["x1"], sv["y2"], target, mods[i], ln_mlp_g[i:i + 1],
                                                         ln_mlp_b[i:i + 1], "loss_ln2_bwd")
            loss = lax.psum(0.5 * jnp.sum(sq) / d, ("x", "y", "c"))
        else:
            dy2, dres, dsca, dsha, dg2, db2, dgm = _mod_in_ln2_bwd(du, dres0, sv["r2"], sv["y2"], mods[i],
                                                                   ln_mlp_g[i:i + 1], ln_mlp_b[i:i + 1], mods[i + 1],
                                                                   tag + "_ln2_bwd")
            dmod_parts[i + 1].update(sc_a=dsca, sh_a=dsha)
        du2, g_w1[i], g_w2[i] = _mlp_backward(dy2, sv["u2"], sv["hr"], sv["a"], full["mlp_w1"][i], full["mlp_w2"][i], tag)
        dy, dres0, dscm, dshm, dg1, db1, dga = _ln1_bwd(du2, dres, sv["r"], sv["y"], mods[i], ln_mix_g[i:i + 1],
                                                        ln_mix_b[i:i + 1], tag + "_ln1_bwd")
        du, mg = mixers[i % 2][1](dy, sv["mix"], w)
        grads.update(mg)
        dmod_parts[i].update(g_a=dga, sh_m=dshm, sc_m=dscm, g_m=dgm)
        ln_grads["ln_mix_g"][i], ln_grads["ln_mix_b"][i] = dg1, db1
        ln_grads["ln_mlp_g"][i], ln_grads["ln_mlp_b"][i] = dg2, db2
    dx, dsca, dsha = _mod_in_bwd(du, dres0, x0, mods[0], "l0_mod_in_bwd")
    dmod_parts[0].update(sc_a=dsca, sh_a=dsha)
    dmods = [jnp.concatenate([p["sh_a"], p["sc_a"], p["g_a"], p["sh_m"], p["sc_m"], p["g_m"]], axis=1)
             for p in dmod_parts]
    grad_x = dx[None]
    grads["mlp_w1"] = jnp.stack(g_w1)
    grads["mlp_w2"] = jnp.stack(g_w2)
    for n in ("fox_w_in", "fox_w_o", "ssm_w_in", "ssm_w_out", "ssm_conv_w"):
        grads[n] = grads[n][None]

    small_names = ("ln_mix_g", "ln_mix_b", "ln_mlp_g", "ln_mlp_b", "fox_b_f", "ssm_dt_bias", "ssm_a_log", "ssm_d")
    small_parts = list(dmods)
    for n in small_names[:4]:
        small_parts.append(jnp.concatenate(ln_grads[n], axis=0))
    for n in small_names[4:]:
        small_parts.append(jnp.pad(grads[n], ((0, 0), (0, LANES - grads[n].shape[1]))))
    small_vec = _pack_rows(small_parts, 1, F32).reshape(-1, LANES)
    small_vec = jnp.pad(small_vec, ((0, -small_vec.shape[0] % 8), (0, 0)))
    small_g_all = _allgather8(small_vec, "gather_small_grads")
    small_sum = _sum_slots(small_g_all, "sum_small_grads").reshape(-1)
    dmod_sum = small_sum[:DEPTH * 6 * d].reshape(DEPTH, 6 * d)
    off = DEPTH * 6 * d
    final = {"ada_b": dmod_sum}
    for n in small_names[:4]:
        final[n] = small_sum[off:off + DEPTH * d].reshape(DEPTH, d)
        off += DEPTH * d
    for n in small_names[4:]:
        width = weights[n].shape[1]
        final[n] = small_sum[off:off + width].reshape(1, width)
        off += LANES

    dmod_all = small_g_all.reshape(8, -1)[:, :DEPTH * 6 * d].reshape(8, DEPTH, 6 * d)
    dmod_cols = lax.dynamic_slice_in_dim(dmod_all, chip * cols, cols, axis=2).transpose(1, 0, 2)
    final["ada_w"] = _ada_backward(c_all.T, dmod_cols, "ada_bwd")

    sharded = BIG + SMALL_SHARDED
    shard_shapes = [weights[n].shape for n in sharded]
    per_target = []
    for k in range(4):
        parts = [_chip_slice(grads[n], SHARD_AXIS[n], k, weights[n].shape[SHARD_AXIS[n]]) for n in sharded]
        per_target.append(_pack2d(parts, 128, F32))
    g_all = jnp.stack(per_target)
    rows = g_all.shape[1]
    g4 = g_all.reshape(4, 2, rows // 2, PACK_COLS)
    recv = _pair_exchange(g4, "rs_pair_exchange")
    part, part16 = _pair_add(g4, recv, jnp.reshape(mc, (1,)).astype(jnp.int32), "rs_pair_add")
    slots = _chip_exchange(part16, "rs_chip_exchange")
    half = _chip_sum(part, slots, jnp.reshape(chip, (1,)).astype(jnp.int32), "rs_chip_sum")
    other = _pair_share(half, "rs_pair_share")
    both = jnp.concatenate([jnp.where(mc == 0, half, other), jnp.where(mc == 0, other, half)], axis=0)
    for n, g_shard in zip(sharded, _unpack2d(both, shard_shapes)):
        final[n] = g_shard

    outs_g, outs_d, outs_m, outs_v = [], [], [], []
    for n in names:
        wv = weights[n]
        two_d = (-1, wv.shape[-1])
        delta, mn, vn = _adamw(wv.reshape(two_d), final[n].reshape(two_d), m_in[n].reshape(two_d),
                               v_in[n].reshape(two_d), "adamw_" + n)
        outs_g.append(final[n].reshape(wv.shape))
        outs_d.append(delta.reshape(wv.shape))
        outs_m.append(mn.reshape(wv.shape))
        outs_v.append(vn.reshape(wv.shape))
    return (loss, grad_x, *outs_g, *outs_d, *outs_m, *outs_v)
```

```python
import functools

import jax
import jax.numpy as jnp
import numpy as np
from jax import lax
from jax.experimental import pallas as pl
from jax.experimental.pallas import tpu as pltpu

F32, BF16 = jnp.float32, jnp.bfloat16
MESH = pl.DeviceIdType.MESH
HBM_SPEC = pl.BlockSpec(memory_space=pltpu.HBM)

VMEM_LIMIT_BYTES = 52 * 2**20
LANES = 128

FOX_HEADS, HEAD_DIM = 16, 64
SSM_HEADS, SSM_GROUPS, SSM_STATE, SSM_CHUNK, SSM_CONV = 32, 8, 128, 128, 4
SSM_GROUP_WIDTH = 256
LN_EPS, RMS_EPS = 1e-5, 1e-5
DEPTH = 2
ALPHA = (2.0 * DEPTH) ** 0.25
ADAM_LR, ADAM_B1, ADAM_B2, ADAM_EPS, ADAM_WD, ADAM_STEP = 0.001, 0.9, 0.999, 1e-08, 0.01, 10

ATT_TILE = 512
ROW_TILE = 512
SCAN_TILE = 512
CONV_CHUNK = 64
SSD_GROUPS_PER_STEP = 4
MM_TM, MM_TN, MM_TK = 1024, 1024, 1024

NT_DIMS = (((1,), (1,)), ((), ()))
TN_DIMS = (((0,), (0,)), ((), ()))
NN_DIMS = (((1,), (0,)), ((), ()))


def _pcall(body, *, name, out_shape, grid=(), in_specs=None, out_specs=None, scratch=(), sem=None, prefetch=0):
    params = dict(vmem_limit_bytes=VMEM_LIMIT_BYTES)
    if sem is not None:
        params["dimension_semantics"] = sem
    if prefetch:
        grid_spec = pltpu.PrefetchScalarGridSpec(num_scalar_prefetch=prefetch, grid=grid, in_specs=in_specs,
                                                 out_specs=out_specs, scratch_shapes=scratch)
        return pl.pallas_call(body, out_shape=out_shape, grid_spec=grid_spec, name=name,
                              compiler_params=pltpu.CompilerParams(**params))
    kwargs = {}
    if in_specs is not None:
        kwargs["in_specs"] = in_specs
    if out_specs is not None:
        kwargs["out_specs"] = out_specs
    return pl.pallas_call(body, out_shape=out_shape, grid=grid, scratch_shapes=scratch, name=name,
                          compiler_params=pltpu.CompilerParams(**params), **kwargs)


def _sds(shape, dtype):
    return jax.ShapeDtypeStruct(tuple(shape), dtype)


def _dot(a, b, dims=NN_DIMS):
    return lax.dot_general(a, b, dims, preferred_element_type=F32)


def _sigmoid(x):
    return 1.0 / (1.0 + jnp.exp(-x))


def _silu(x):
    return x * _sigmoid(x)


def _dsilu(x):
    s = _sigmoid(x)
    return s * (1.0 + x * (1.0 - s))


def _dot_split(x, m16, dims=NN_DIMS, passes=3):
    hi = x.astype(BF16)
    r1 = x - hi.astype(F32)
    mid = r1.astype(BF16)
    out = _dot(hi, m16, dims) + _dot(mid, m16, dims)
    if passes == 3:
        lo = (r1 - mid.astype(F32)).astype(BF16)
        out = out + _dot(lo, m16, dims)
    return out


def _mm(a, b, dims, outs, *, name, tm=MM_TM, tn=MM_TN, tk=MM_TK, epi=None, extra=()):
    if dims == "nn":
        (m, k), (k2, n) = a.shape, b.shape
    elif dims == "nt":
        (m, k), (n, k2) = a.shape, b.shape
    else:
        (k, m), (k2, n) = a.shape, b.shape
    assert k == k2, (a.shape, b.shape, dims)
    tm, tn, tk = min(tm, m), min(tn, n), min(tk, k)
    assert m % tm == 0 and n % tn == 0 and k % tk == 0, (m, n, k, tm, tn, tk)
    nk = k // tk
    dn = {"nn": NN_DIMS, "nt": NT_DIMS, "tn": TN_DIMS}[dims]
    n_extra, n_out = len(extra), len(outs)
    if epi is None:
        epi = lambda acc: (acc,) * n_out

    def body(a_ref, b_ref, *rest):
        extra_refs, out_refs, acc_ref = rest[:n_extra], rest[n_extra:n_extra + n_out], rest[-1]
        kk = pl.program_id(2)

        @pl.when(kk == 0)
        def _():
            acc_ref[...] = jnp.zeros_like(acc_ref)

        acc_ref[...] += _dot(a_ref[...].astype(BF16), b_ref[...].astype(BF16), dn)

        @pl.when(kk == nk - 1)
        def _():
            res = epi(acc_ref[...], *[e[...] for e in extra_refs])
            for o, r in zip(out_refs, res):
                o[...] = r.astype(o.dtype)

    if dims == "tn":
        a_spec = pl.BlockSpec((tk, tm), lambda i, j, kk: (kk, i))
    else:
        a_spec = pl.BlockSpec((tm, tk), lambda i, j, kk: (i, kk))
    if dims == "nt":
        b_spec = pl.BlockSpec((tn, tk), lambda i, j, kk: (j, kk))
    else:
        b_spec = pl.BlockSpec((tk, tn), lambda i, j, kk: (kk, j))
    o_spec = pl.BlockSpec((tm, tn), lambda i, j, kk: (i, j))
    res = _pcall(body, name=name, grid=(m // tm, n // tn, nk),
                 in_specs=[a_spec, b_spec] + [o_spec] * n_extra,
                 out_specs=[o_spec] * n_out,
                 out_shape=[_sds((m, n), d) for d in outs],
                 scratch=[pltpu.VMEM((tm, tn), F32)],
                 sem=("parallel", "parallel", "arbitrary"))(a, b, *extra)
    return res[0] if n_out == 1 else res


def _k_blocks(a, tk=None):
    tk = tk or MM_TK
    return [(a, kb) for kb in range(a.shape[1] // tk)]


def _mm_nt_blocks(a_blocks, b, start, *, name, tm=MM_TM, tk=None):
    tk = tk or MM_TK
    tm = min(tm, a_blocks[0][0].shape[0])
    m, n, p_n = a_blocks[0][0].shape[0], b.shape[0], len(a_blocks)
    assert b.shape[1] == p_n * tk and m % tm == 0

    def body(*refs):
        a_refs = refs[:p_n]
        b_ref, s_ref, o_ref, acc_ref = refs[p_n:]
        kk = pl.program_id(1)

        @pl.when(kk == 0)
        def _():
            acc_ref[...] = s_ref[...]
        for p in range(p_n):
            @pl.when(kk == p)
            def _(p=p):
                acc_ref[...] += _dot(a_refs[p][...].astype(BF16), b_ref[...].astype(BF16), NT_DIMS)

        @pl.when(kk == p_n - 1)
        def _():
            o_ref[...] = acc_ref[...]

    in_specs = [pl.BlockSpec((tm, tk), functools.partial(lambda kb, i, kk: (i, kb), kb)) for _, kb in a_blocks]
    in_specs += [pl.BlockSpec((n, tk), lambda i, kk: (0, kk)), pl.BlockSpec((tm, n), lambda i, kk: (i, 0))]
    return _pcall(body, name=name, grid=(m // tm, p_n), in_specs=in_specs,
                  out_specs=pl.BlockSpec((tm, n), lambda i, kk: (i, 0)), out_shape=_sds((m, n), F32),
                  scratch=[pltpu.VMEM((tm, n), F32)], sem=("parallel", "arbitrary"))(*[a for a, _ in a_blocks], b, start)


def _rowwise(fn, rows, consts, row_outs, acc_outs, *, name, tm=ROW_TILE):
    s = rows[0].shape[0]
    tm = min(tm, s)
    assert s % tm == 0
    n_in, n_o = len(rows) + len(consts), len(row_outs)

    def body(*refs):
        ins, outs = refs[:n_in], refs[n_in:]
        res = fn(*[r[...] for r in ins])
        if not isinstance(res, (tuple, list)):
            res = (res,)
        for o, val in zip(outs[:n_o], res[:n_o]):
            o[...] = val.astype(o.dtype)
        if acc_outs:
            @pl.when(pl.program_id(0) == 0)
            def _():
                for o in outs[n_o:]:
                    o[...] = jnp.zeros_like(o)
            for o, val in zip(outs[n_o:], res[n_o:]):
                o[...] += val

    in_specs = [pl.BlockSpec((tm, r.shape[1]), lambda i: (i, 0)) for r in rows]
    in_specs += [pl.BlockSpec(c.shape, functools.partial(lambda nd, i: (0,) * nd, c.ndim)) for c in consts]
    out_specs = [pl.BlockSpec((tm, c), lambda i: (i, 0)) for c, _ in row_outs]
    out_specs += [pl.BlockSpec(tuple(sh), lambda i: (0, 0)) for sh in acc_outs]
    out_shape = [_sds((s, c), d) for c, d in row_outs] + [_sds(sh, F32) for sh in acc_outs]
    res = _pcall(body, name=name, grid=(s // tm,), in_specs=in_specs, out_specs=out_specs,
                 out_shape=out_shape, sem=("arbitrary",))(*rows, *consts)
    return res


def _colsum(x):
    return jnp.sum(x, axis=0, keepdims=True)


def _ln_stats(r):
    mu = jnp.mean(r, axis=-1, keepdims=True)
    xc = r - mu
    var = jnp.mean(xc * xc, axis=-1, keepdims=True)
    rstd = lax.rsqrt(var + LN_EPS)
    return xc * rstd, rstd


def _ln_bwd(dy, xhat, rstd, gamma):
    dyg = dy * gamma
    m1 = jnp.mean(dyg, axis=-1, keepdims=True)
    m2 = jnp.mean(dyg * xhat, axis=-1, keepdims=True)
    return rstd * (dyg - m1 - xhat * m2)


def _modulate_in(x, mod, name):
    def fn(xv, m):
        return (xv * (1.0 + m[1:2]) + m[0:1],)
    return _rowwise(fn, [x], [mod], [(x.shape[1], BF16)], [], name=name)[0]


def _res_ln_mod(x, y, mod, g, b, name):
    d = x.shape[1]

    def fn(xv, yv, m, gv, bv):
        r = ALPHA * xv + (1.0 + m[2:3]) * yv
        xhat, _ = _ln_stats(r)
        x1 = xhat * gv + bv
        u2 = x1 * (1.0 + m[4:5]) + m[3:4]
        return r, x1, u2
    return _rowwise(fn, [x, y], [mod, g, b], [(d, F32), (d, F32), (d, BF16)], [], name=name)


def _res_ln_next(x, y, mod, g, b, mod_next, name):
    d = x.shape[1]

    def fn(xv, yv, m, gv, bv, mn):
        r = ALPHA * xv + (1.0 + m[5:6]) * yv
        xhat, _ = _ln_stats(r)
        out = xhat * gv + bv
        return r, out, out * (1.0 + mn[1:2]) + mn[0:1]
    return _rowwise(fn, [x, y], [mod, g, b, mod_next], [(d, F32), (d, F32), (d, BF16)], [], name=name)


def _loss_ln2_bwd(x1, y2, target, mod, g, b, name):
    d = x1.shape[1]

    def fn(xv, yv, tv, m, gv, bv):
        rv = ALPHA * xv + (1.0 + m[5:6]) * yv
        xhat, rstd = _ln_stats(rv)
        e = xhat * gv + bv - tv
        dxv = e * (1.0 / d)
        dr = _ln_bwd(dxv, xhat, rstd, gv)
        return (dr * (1.0 + m[5:6]), ALPHA * dr,
                _colsum(e * e), _colsum(dxv * xhat), _colsum(dxv), _colsum(dr * yv))
    return _rowwise(fn, [x1, y2, target], [mod, g, b], [(d, BF16), (d, F32)], [(1, d)] * 4, name=name)


def _mod_in_ln2_bwd(du, dres, r2, y2, mod, g, b, mod_next, name):
    d = du.shape[1]

    def fn(duv, drv, rv, yv, m, gv, bv, mn):
        xhat, rstd = _ln_stats(rv)
        xout = xhat * gv + bv
        dxv = duv * (1.0 + mn[1:2]) + drv
        dr = _ln_bwd(dxv, xhat, rstd, gv)
        return (dr * (1.0 + m[5:6]), ALPHA * dr,
                _colsum(duv * xout), _colsum(duv), _colsum(dxv * xhat), _colsum(dxv), _colsum(dr * yv))
    return _rowwise(fn, [du, dres, r2, y2], [mod, g, b, mod_next], [(d, BF16), (d, F32)], [(1, d)] * 5, name=name)


def _ln1_bwd(du2, dres, r, y, mod, g, b, name):
    d = du2.shape[1]

    def fn(duv, drv, rv, yv, m, gv, bv):
        xhat, rstd = _ln_stats(rv)
        x1 = xhat * gv + bv
        dx1 = duv * (1.0 + m[4:5]) + drv
        dr = _ln_bwd(dx1, xhat, rstd, gv)
        return (dr * (1.0 + m[2:3]), ALPHA * dr,
                _colsum(duv * x1), _colsum(duv), _colsum(dx1 * xhat), _colsum(dx1), _colsum(dr * yv))
    return _rowwise(fn, [du2, dres, r, y], [mod, g, b], [(d, BF16), (d, F32)], [(1, d)] * 5, name=name)


def _mod_in_bwd(du, dres, x, mod, name):
    d = du.shape[1]

    def fn(duv, drv, xv, m):
        return duv * (1.0 + m[1:2]) + drv, _colsum(duv * xv), _colsum(duv)
    return _rowwise(fn, [du, dres, x], [mod], [(d, F32)], [(1, d)] * 2, name=name)


def _fox_gate(fraw, b_pad, name):
    s = fraw.shape[0]
    tb = min(SCAN_TILE, s)

    def body(f_ref, b_ref, cum_ref, rows_ref, carry):
        @pl.when(pl.program_id(0) == 0)
        def _():
            carry[...] = jnp.zeros_like(carry)
        z = f_ref[...] + b_ref[...]
        lf = jnp.minimum(z, 0.0) - jnp.log(1.0 + jnp.exp(-jnp.abs(z)))
        lane = lax.broadcasted_iota(jnp.int32, (tb, LANES), 1)
        row = lax.broadcasted_iota(jnp.int32, (tb, LANES), 0)
        c = jnp.where(lane < FOX_HEADS, lf, 0.0)
        sh = 1
        while sh < tb:
            c = c + jnp.where(row >= sh, pltpu.roll(c, sh, 0), 0.0)
            sh *= 2
        c = c + carry[0:1, :]
        cum_ref[...] = c
        rows_ref[...] = c.T[0:FOX_HEADS, :]
        carry[0:1, :] = c[tb - 1:tb, :]

    return _pcall(body, name=name, grid=(s // tb,),
                  in_specs=[pl.BlockSpec((tb, LANES), lambda i: (i, 0)), pl.BlockSpec((1, LANES), lambda i: (0, 0))],
                  out_specs=[pl.BlockSpec((tb, LANES), lambda i: (i, 0)), pl.BlockSpec((FOX_HEADS, tb), lambda i: (0, i))],
                  out_shape=[_sds((s, LANES), F32), _sds((FOX_HEADS, s), F32)], scratch=[pltpu.VMEM((8, LANES), F32)],
                  sem=("arbitrary",))(fraw, b_pad)


def _fox_gate_bwd(drow, dcol, fraw, b_pad, name):
    s = fraw.shape[0]
    tb = min(SCAN_TILE, s)
    n = s // tb

    def body(dr_ref, dc_ref, f_ref, b_ref, df_ref, db_ref, carry):
        @pl.when(pl.program_id(0) == 0)
        def _():
            carry[...] = jnp.zeros_like(carry)
            db_ref[...] = jnp.zeros_like(db_ref)
        row = lax.broadcasted_iota(jnp.int32, (tb, LANES), 0)
        c = dr_ref[...] + dc_ref[...]
        sh = 1
        while sh < tb:
            c = c + jnp.where(row + sh < tb, pltpu.roll(c, tb - sh, 0), 0.0)
            sh *= 2
        c = c + carry[0:1, :]
        carry[0:1, :] = c[0:1, :]
        z = f_ref[...] + b_ref[...]
        df = c * (1.0 / (1.0 + jnp.exp(z)))
        df_ref[...] = df.astype(df_ref.dtype)
        db_ref[...] += _colsum(df)

    rev = lambda i: (n - 1 - i, 0)
    return _pcall(body, name=name, grid=(n,),
                  in_specs=[pl.BlockSpec((tb, LANES), rev)] * 3 + [pl.BlockSpec((1, LANES), lambda i: (0, 0))],
                  out_specs=[pl.BlockSpec((tb, LANES), rev), pl.BlockSpec((1, LANES), lambda i: (0, 0))],
                  out_shape=[_sds((s, LANES), BF16), _sds((1, LANES), F32)],
                  scratch=[pltpu.VMEM((8, LANES), F32)], sem=("arbitrary",))(drow, dcol, fraw, b_pad)


def _head_pair_masks(t):
    lane = lax.broadcasted_iota(jnp.int32, (t, LANES), 1)
    return lane < HEAD_DIM


def _lane_blocks(x):
    return [x[:, c * LANES:(c + 1) * LANES] for c in range(x.shape[1] // LANES)]


def _sum_list(xs):
    acc = xs[0]
    for x in xs[1:]:
        acc = acc + x
    return acc


def _causal(t, transposed=False):
    ri = lax.broadcasted_iota(jnp.int32, (t, t), 0)
    ci = lax.broadcasted_iota(jnp.int32, (t, t), 1)
    return ci >= ri if transposed else ri >= ci


def _span_mask(r0, r1, c0, c1, transposed=False):
    ri = lax.broadcasted_iota(jnp.int32, (r1 - r0, c1 - c0), 0) + r0
    ci = lax.broadcasted_iota(jnp.int32, (r1 - r0, c1 - c0), 1) + c0
    return ci >= ri if transposed else ri >= ci


def _full_spans(t):
    return ((0, t, 0, t, False),)


def _diagonal_spans(t, transposed=False):
    h = t // 2
    if h % LANES:
        return ((0, t, 0, t, True),)
    if transposed:
        return ((0, h, 0, t, True), (h, t, h, t, True))
    return ((0, h, 0, h, True), (h, t, 0, t, True))


def _flash_fwd(qkv, ck_rows, kb_start, name):
    s = qkv.shape[0]
    t = min(ATT_TILE, s)
    nq = s // t
    scale = HEAD_DIM ** -0.5
    hp_blocks = FOX_HEADS // 2

    def body(ks_ref, q_ref, k_ref, v_ref, ck_ref, o_ref, lse_ref, lse_rows_ref, acc_ref, m_ref, l_ref):
        hp, qb = pl.program_id(0), pl.program_id(1)
        q2 = q_ref[...] * scale
        first = _head_pair_masks(t)
        zero = jnp.zeros_like(q2)
        qs = (jnp.where(first, q2, zero), jnp.where(first, zero, q2))
        m_ref[...] = jnp.full_like(m_ref, -jnp.inf)
        l_ref[...] = jnp.zeros_like(l_ref)
        acc_ref[...] = jnp.zeros_like(acc_ref)

        def tile(kb, spans):
            off = pl.multiple_of(kb * t, t)
            k2 = k_ref[pl.ds(off, t), :]
            v2 = v_ref[pl.ds(off, t), :]
            ck = ck_ref[kb]
            for r0, r1, c0, c1, masked in spans:
                kk, vv, fr = k2[c0:c1], v2[c0:c1], first[r0:r1]
                pvs, als = [], []
                for j in range(2):
                    sc = _dot(qs[j][r0:r1], kk, NT_DIMS) - ck[j:j + 1, c0:c1]
                    if masked:
                        sc = jnp.where(_span_mask(r0, r1, c0, c1), sc, -jnp.inf)
                    blocks = _lane_blocks(sc)
                    mx = blocks[0]
                    for b in blocks[1:]:
                        mx = jnp.maximum(mx, b)
                    m_old = m_ref[j, r0:r1]
                    m_new = jnp.maximum(m_old, jnp.max(mx, axis=1, keepdims=True))
                    ps = [jnp.exp(b - m_new) for b in blocks]
                    a = jnp.exp(m_old - m_new)
                    l_ref[j, r0:r1] = a * l_ref[j, r0:r1] + _sum_list(ps)
                    m_ref[j, r0:r1] = m_new
                    pvs.append(_dot(jnp.concatenate(ps, axis=1).astype(BF16), vv))
                    als.append(a)
                acc_ref[r0:r1] = jnp.where(fr, als[0], als[1]) * acc_ref[r0:r1] + jnp.where(fr, pvs[0], pvs[1])

        def step(kb, carry):
            tile(kb, _full_spans(t))
            return carry

        lax.fori_loop(ks_ref[hp, qb], qb, step, 0)
        tile(qb, ((0, t, 0, t, True),))
        l0 = jnp.sum(l_ref[0], axis=1, keepdims=True)
        l1 = jnp.sum(l_ref[1], axis=1, keepdims=True)
        o_ref[...] = acc_ref[...] / jnp.where(first, l0, l1)
        for j, lj in enumerate((l0, l1)):
            lse = m_ref[j] + jnp.log(jnp.broadcast_to(lj, (t, LANES)))
            lse_ref[:, j:j + 1] = lse[:, 0:1]
            lse_rows_ref[j:j + 1, :] = lse.T[0:1, :]

    return _pcall(
        body, name=name, grid=(hp_blocks, nq), prefetch=1,
        in_specs=[pl.BlockSpec((t, LANES), lambda h, i, ks: (i, h)),
                  pl.BlockSpec((s, LANES), lambda h, i, ks: (0, hp_blocks + h)),
                  pl.BlockSpec((s, LANES), lambda h, i, ks: (0, 2 * hp_blocks + h)),
                  pl.BlockSpec((None, nq, 2, t), lambda h, i, ks: (h, 0, 0, 0))],
        out_specs=[pl.BlockSpec((t, LANES), lambda h, i, ks: (i, h)),
                   pl.BlockSpec((None, t, 2), lambda h, i, ks: (h, i, 0)),
                   pl.BlockSpec((None, None, 2, t), lambda h, i, ks: (h, i, 0, 0))],
        out_shape=[_sds((s, hp_blocks * LANES), F32), _sds((hp_blocks, s, 2), F32), _sds((hp_blocks, nq, 2, t), F32)],
        scratch=[pltpu.VMEM((t, LANES), F32), pltpu.VMEM((2, t, LANES), F32), pltpu.VMEM((2, t, LANES), F32)],
        sem=("parallel", "arbitrary"))(kb_start, qkv, qkv, qkv, ck_rows)


def _flash_dq(qkv, do16, ck_rows, lse_c, delta, kb_start, name):
    s = qkv.shape[0]
    t = min(ATT_TILE, s)
    nq = s // t
    scale = HEAD_DIM ** -0.5
    hp_blocks = FOX_HEADS // 2

    def body(ks_ref, q_ref, do_ref, k_ref, v_ref, ck_ref, lse_ref, dl_ref, dq_ref, drow_ref, acc_ref, row_acc):
        hp, qb = pl.program_id(0), pl.program_id(1)
        q2, do2 = q_ref[...] * scale, do_ref[...]
        first = _head_pair_masks(t)
        zero = jnp.zeros_like(q2)
        qs = (jnp.where(first, q2, zero), jnp.where(first, zero, q2))
        dos = (jnp.where(first, do2, zero), jnp.where(first, zero, do2))
        lse, dl = lse_ref[...], dl_ref[...]
        lane = lax.broadcasted_iota(jnp.int32, (t, LANES), 1)
        lse_b = [jnp.broadcast_to(lse[:, j:j + 1], (t, LANES)) for j in range(2)]
        dl_b = [jnp.broadcast_to(jnp.sum(jnp.where(lane == 2 * hp + j, dl, 0.0), axis=1, keepdims=True), (t, LANES))
                for j in range(2)]
        acc_ref[...] = jnp.zeros_like(acc_ref)
        row_acc[...] = jnp.zeros_like(row_acc)

        def tile(kb, spans):
            off = pl.multiple_of(kb * t, t)
            k2 = k_ref[pl.ds(off, t), :]
            v2 = v_ref[pl.ds(off, t), :]
            ck = ck_ref[kb]
            for r0, r1, c0, c1, masked in spans:
                kk, vv = k2[c0:c1], v2[c0:c1]
                dqs = []
                for j in range(2):
                    sc = _dot(qs[j][r0:r1], kk, NT_DIMS) - ck[j:j + 1, c0:c1]
                    if masked:
                        sc = jnp.where(_span_mask(r0, r1, c0, c1), sc, -jnp.inf)
                    dp = _dot(dos[j][r0:r1], vv, NT_DIMS)
                    lb, db_ = lse_b[j][r0:r1], dl_b[j][r0:r1]
                    dsb = [jnp.exp(x - lb) * (d - db_) for x, d in zip(_lane_blocks(sc), _lane_blocks(dp))]
                    row_acc[j, r0:r1] += _sum_list(dsb)
                    dqs.append(_dot(jnp.concatenate(dsb, axis=1).astype(BF16), kk))
                acc_ref[r0:r1] += jnp.where(first[r0:r1], dqs[0], dqs[1])

        def step(kb, carry):
            tile(kb, _full_spans(t))
            return carry

        lax.fori_loop(ks_ref[hp, qb], qb, step, 0)
        tile(qb, _diagonal_spans(t))
        dq_ref[...] = (acc_ref[...] * scale).astype(dq_ref.dtype)
        for j in range(2):
            drow_ref[j:j + 1, :] = jnp.sum(row_acc[j].T, axis=0, keepdims=True)

    return _pcall(
        body, name=name, grid=(hp_blocks, nq), prefetch=1,
        in_specs=[pl.BlockSpec((t, LANES), lambda h, i, ks: (i, h)),
                  pl.BlockSpec((t, LANES), lambda h, i, ks: (i, h)),
                  pl.BlockSpec((s, LANES), lambda h, i, ks: (0, hp_blocks + h)),
                  pl.BlockSpec((s, LANES), lambda h, i, ks: (0, 2 * hp_blocks + h)),
                  pl.BlockSpec((None, nq, 2, t), lambda h, i, ks: (h, 0, 0, 0)),
                  pl.BlockSpec((None, t, 2), lambda h, i, ks: (h, i, 0)),
                  pl.BlockSpec((t, LANES), lambda h, i, ks: (i, 0))],
        out_specs=[pl.BlockSpec((t, LANES), lambda h, i, ks: (i, h)),
                   pl.BlockSpec((None, None, 2, t), lambda h, i, ks: (h, i, 0, 0))],
        out_shape=[_sds((s, hp_blocks * LANES), BF16), _sds((hp_blocks, nq, 2, t), F32)],
        scratch=[pltpu.VMEM((t, LANES), F32), pltpu.VMEM((2, t, LANES), F32)],
        sem=("parallel", "arbitrary"))(kb_start, qkv, do16, qkv, qkv, ck_rows, lse_c, delta)


def _flash_dkv(qkv, do16, cum, lse_rows, dl_rows, qb_end, name):
    s = qkv.shape[0]
    t = min(ATT_TILE, s)
    nq = s // t
    scale = HEAD_DIM ** -0.5
    hp_blocks = FOX_HEADS // 2

    def body(qe_ref, k_ref, v_ref, cum_ref, q_ref, do_ref, lse_ref, dl_ref, dk_ref, dv_ref, dck_ref,
             dk_acc, dv_acc, dck_acc):
        hp, kb = pl.program_id(0), pl.program_id(1)
        k2, v2 = k_ref[...] * scale, v_ref[...]
        first = _head_pair_masks(t)
        zero = jnp.zeros_like(k2)
        ks = (jnp.where(first, k2, zero), jnp.where(first, zero, k2))
        vs = (jnp.where(first, v2, zero), jnp.where(first, zero, v2))
        cumv = cum_ref[...]
        lane = lax.broadcasted_iota(jnp.int32, (t, LANES), 1)
        ck_b = [jnp.broadcast_to(jnp.sum(jnp.where(lane == 2 * hp + j, cumv, 0.0), axis=1, keepdims=True), (t, LANES))
                for j in range(2)]
        dk_acc[...] = jnp.zeros_like(dk_acc)
        dv_acc[...] = jnp.zeros_like(dv_acc)
        dck_acc[...] = jnp.zeros_like(dck_acc)

        def tile(qb, spans):
            off = pl.multiple_of(qb * t, t)
            q2 = q_ref[pl.ds(off, t), :]
            do2 = do_ref[pl.ds(off, t), :]
            lse, dl = lse_ref[qb], dl_ref[qb]
            for r0, r1, c0, c1, masked in spans:
                qq, dd, fr = q2[c0:c1], do2[c0:c1], first[r0:r1]
                dvs, dks = [], []
                for j in range(2):
                    sc = _dot(ks[j][r0:r1], qq, NT_DIMS)
                    if masked:
                        sc = jnp.where(_span_mask(r0, r1, c0, c1, transposed=True), sc, -jnp.inf)
                    dp = _dot(vs[j][r0:r1], dd, NT_DIMS) - dl[j:j + 1, c0:c1]
                    cb_ = ck_b[j][r0:r1]
                    pb = [jnp.exp((x - cb_) - l) for x, l in zip(_lane_blocks(sc), _lane_blocks(lse[j:j + 1, c0:c1]))]
                    dsb = [p * d for p, d in zip(pb, _lane_blocks(dp))]
                    dck_acc[j, r0:r1] += _sum_list(dsb)
                    dvs.append(_dot(jnp.concatenate(pb, axis=1).astype(BF16), dd))
                    dks.append(_dot(jnp.concatenate(dsb, axis=1).astype(BF16), qq))
                dv_acc[r0:r1] += jnp.where(fr, dvs[0], dvs[1])
                dk_acc[r0:r1] += jnp.where(fr, dks[0], dks[1])

        def step(qb, carry):
            tile(qb, _full_spans(t))
            return carry

        tile(kb, _diagonal_spans(t, transposed=True))
        lax.fori_loop(kb + 1, qe_ref[hp, kb] + 1, step, 0)
        dk_ref[...] = (dk_acc[...] * scale).astype(dk_ref.dtype)
        dv_ref[...] = dv_acc[...].astype(dv_ref.dtype)
        for j in range(2):
            dck_ref[j:j + 1, :] = -jnp.sum(dck_acc[j].T, axis=0, keepdims=True)

    return _pcall(
        body, name=name, grid=(hp_blocks, nq), prefetch=1,
        in_specs=[pl.BlockSpec((t, LANES), lambda h, j, qe: (j, hp_blocks + h)),
                  pl.BlockSpec((t, LANES), lambda h, j, qe: (j, 2 * hp_blocks + h)),
                  pl.BlockSpec((t, LANES), lambda h, j, qe: (j, 0)),
                  pl.BlockSpec((s, LANES), lambda h, j, qe: (0, h)),
                  pl.BlockSpec((s, LANES), lambda h, j, qe: (0, h)),
                  pl.BlockSpec((None, nq, 2, t), lambda h, j, qe: (h, 0, 0, 0)),
                  pl.BlockSpec((None, nq, 2, t), lambda h, j, qe: (h, 0, 0, 0))],
        out_specs=[pl.BlockSpec((t, LANES), lambda h, j, qe: (j, h)),
                   pl.BlockSpec((t, LANES), lambda h, j, qe: (j, h)),
                   pl.BlockSpec((None, None, 2, t), lambda h, j, qe: (h, j, 0, 0))],
        out_shape=[_sds((s, hp_blocks * LANES), BF16), _sds((s, hp_blocks * LANES), BF16),
                   _sds((hp_blocks, nq, 2, t), F32)],
        scratch=[pltpu.VMEM((t, LANES), F32), pltpu.VMEM((t, LANES), F32), pltpu.VMEM((2, t, LANES), F32)],
        sem=("parallel", "arbitrary"))(qb_end, qkv, qkv, cum, qkv, do16, lse_rows, dl_rows)


SKIP_NATS = 110.0


def _qk_norms(qkv, ind16, name):
    d = FOX_HEADS * HEAD_DIM

    def fn(tile, ind):
        q = tile[:, :d].astype(F32)
        k = tile[:, d:2 * d].astype(F32)
        return _dot_split(q * q, ind), _dot_split(k * k, ind)
    return _rowwise(fn, [qkv], [ind16], [(LANES, F32), (LANES, F32)], [], name=name)


def _skip_bounds(qn, kn, cum, t):
    s = qn.shape[0]
    nq = s // t
    hp = FOX_HEADS // 2
    scale = HEAD_DIM ** -0.5
    qmax = jnp.sqrt(jnp.max(qn.reshape(nq, t, FOX_HEADS), axis=1))
    kmax = jnp.sqrt(jnp.max(kn, axis=0))
    bound = qmax * kmax[None, :] * (scale * 1.01) + 1e-3
    gap = cum[0::t][:, None, :] - cum[t - 1::t][None, :, :]
    idx = jnp.arange(nq, dtype=jnp.int32)
    needed = (gap + 2.0 * bound[:, None, :]) > -SKIP_NATS
    needed = needed.reshape(nq, nq, hp, 2).any(axis=-1) & (idx[None, :] <= idx[:, None])[:, :, None]
    first = jnp.min(jnp.where(needed, idx[None, :, None], nq), axis=1)
    first = jnp.minimum(first, idx[:, None])
    start = lax.cummin(first, axis=0, reverse=True)
    uses = start[:, None, :] <= idx[None, :, None]
    last = jnp.max(jnp.where(uses, idx[:, None, None], 0), axis=0)
    last = jnp.maximum(last, idx[:, None])
    return start.T.astype(jnp.int32), last.T.astype(jnp.int32)


def _head_rowsum(a, b, ind16, name):
    s, d = a.shape
    tm = min(ROW_TILE, s)

    def body(a_ref, b_ref, ind_ref, o_ref, rows_ref):
        dsum = _dot_split(a_ref[...] * b_ref[...], ind_ref[...])
        o_ref[...] = dsum
        rows_ref[...] = dsum.T[0:FOX_HEADS, :]

    tile = pl.BlockSpec((tm, d), lambda i: (i, 0))
    return _pcall(body, name=name, grid=(s // tm,),
                  in_specs=[tile, tile, pl.BlockSpec((d, LANES), lambda i: (0, 0))],
                  out_specs=[pl.BlockSpec((tm, LANES), lambda i: (i, 0)), pl.BlockSpec((FOX_HEADS, tm), lambda i: (0, i))],
                  out_shape=[_sds((s, LANES), F32), _sds((FOX_HEADS, s), F32)], sem=("parallel",))(a, b, ind16)


def _rows_to_tiles(x, t):
    s = x.shape[1]
    return x.reshape(FOX_HEADS // 2, 2, s // t, t).transpose(0, 2, 1, 3)


def _tiles_to_cols(x):
    hp, nq, _, t = x.shape
    return jnp.pad(x.transpose(1, 3, 0, 2).reshape(nq * t, 2 * hp), ((0, 0), (0, LANES - 2 * hp)))


def _fox_forward(u, w):
    s = u.shape[0]
    t = min(ATT_TILE, s)
    qkv = _mm(u, w["fox_qkv"], "nn", [BF16], name="fox_qkv")
    fraw = _mm(u, w["fox_f"], "nn", [F32], name="fox_fproj")
    cum, cum_rows = _fox_gate(fraw, w["fox_bf"], "fox_gate")
    ck_rows = _rows_to_tiles(cum_rows, t)
    qn, kn = _qk_norms(qkv, w["head_ind"], "fox_qk_norms")
    kb_start, qb_end = _skip_bounds(qn[:, :FOX_HEADS], kn[:, :FOX_HEADS], cum[:, :FOX_HEADS], t)
    o, lse, lse_rows = _flash_fwd(qkv, ck_rows, kb_start, "fox_flash_fwd")
    y = _mm(o, w["fox_o"], "nn", [F32], name="fox_oproj")
    return y, dict(u=u, qkv=qkv, fraw=fraw, cum=cum, ck_rows=ck_rows, o=o, lse=lse, lse_rows=lse_rows,
                   kb_start=kb_start, qb_end=qb_end)


def _fox_backward(dy, sv, w):
    s = dy.shape[0]
    t = min(ATT_TILE, s)
    do32, do16 = _mm(dy, w["fox_o"], "nt", [F32, BF16], name="fox_do")
    g_wo = _mm(sv["o"], dy, "tn", [F32], name="fox_gwo")
    delta, delta_rows = _head_rowsum(do32, sv["o"], w["head_ind"], "fox_delta")
    dq, drow = _flash_dq(sv["qkv"], do16, sv["ck_rows"], sv["lse"], delta, sv["kb_start"], "fox_flash_dq")
    dk, dv, dck = _flash_dkv(sv["qkv"], do16, sv["cum"], sv["lse_rows"], _rows_to_tiles(delta_rows, t),
                             sv["qb_end"], "fox_flash_dkv")
    df, db_f = _fox_gate_bwd(_tiles_to_cols(drow), _tiles_to_cols(dck), sv["fraw"], w["fox_bf"], "fox_gate_bwd")
    du = _mm(df, w["fox_f"], "nt", [F32], name="fox_du_f")
    du = _mm_nt_blocks(_k_blocks(dq) + _k_blocks(dk) + _k_blocks(dv), w["fox_qkv"], du, name="fox_du")
    g_win = jnp.concatenate([_mm(sv["u"], dq, "tn", [F32], name="fox_gwin_q"),
                             _mm(sv["u"], dk, "tn", [F32], name="fox_gwin_k"),
                             _mm(sv["u"], dv, "tn", [F32], name="fox_gwin_v"),
                             _mm(sv["u"], df, "tn", [F32], name="fox_gwin_f")[:, :FOX_HEADS]], axis=1)
    return du, dict(fox_w_in=g_win, fox_w_o=g_wo, fox_b_f=db_f[:, :FOX_HEADS])


def _conv_fwd(xpre, w8, b, name):
    s, c = xpre.shape
    tm, tc = min(ROW_TILE, s), min(1024, c)
    hb = tm // 8

    r = min(CONV_CHUNK, tm)

    def body(x_ref, h_ref, w_ref, b_ref, xc_ref, xa_ref):
        i = pl.program_id(1)
        row8 = lax.broadcasted_iota(jnp.int32, (8, LANES), 0)
        for cb in range(tc // LANES):
            ls = slice(cb * LANES, (cb + 1) * LANES)
            w, bias = w_ref[:, ls], b_ref[:, ls]
            for rb in range(tm // r):
                r0 = rb * r
                cur = x_ref[r0:r0 + r, ls]
                acc = cur * w[3:4] + bias
                if rb == 0:
                    halo = jnp.where(i > 0, h_ref[:, ls], 0.0)
                    x8 = cur[0:8]
                    acc8 = x8 * w[3:4] + bias
                    for j in range(1, SSM_CONV):
                        acc = acc + w[3 - j:4 - j] * pltpu.roll(cur, j, 0)
                        acc8 = acc8 + w[3 - j:4 - j] * jnp.where(row8 < j, pltpu.roll(halo, j, 0), pltpu.roll(x8, j, 0))
                    acc = jnp.concatenate([acc8, acc[8:]], axis=0)
                else:
                    for j in range(1, SSM_CONV):
                        acc = acc + w[3 - j:4 - j] * x_ref[r0 - j:r0 - j + r, ls]
                xc_ref[r0:r0 + r, ls] = acc
                xa_ref[r0:r0 + r, ls] = _silu(acc)

    tile = pl.BlockSpec((tm, tc), lambda jc, i: (i, jc))
    return _pcall(body, name=name, grid=(c // tc, s // tm),
                  in_specs=[tile, pl.BlockSpec((8, tc), lambda jc, i: (jnp.maximum(i * hb - 1, 0), jc)),
                            pl.BlockSpec((8, tc), lambda jc, i: (0, jc)), pl.BlockSpec((1, tc), lambda jc, i: (0, jc))],
                  out_specs=[tile, tile], out_shape=[_sds((s, c), F32), _sds((s, c), F32)],
                  sem=("parallel", "arbitrary"))(xpre, xpre, w8, b)


def _conv_bwd(dxa, xc, xpre, w8, name):
    s, c = xpre.shape
    tm, tc = min(ROW_TILE, s), min(1024, c)
    hb = tm // 8
    n = s // tm

    r = min(CONV_CHUNK, tm)

    def body(d_ref, xc_ref, x_ref, xh_ref, dn_ref, xcn_ref, w_ref, dx_ref, dw_ref, db_ref, g_scr):
        i = pl.program_id(1)

        @pl.when(i == 0)
        def _():
            dw_ref[...] = jnp.zeros_like(dw_ref)
            db_ref[...] = jnp.zeros_like(db_ref)
        row8 = lax.broadcasted_iota(jnp.int32, (8, LANES), 0)
        rowr = lax.broadcasted_iota(jnp.int32, (r, LANES), 0)
        for cb in range(tc // LANES):
            ls = slice(cb * LANES, (cb + 1) * LANES)
            w = w_ref[:, ls]
            for rb in range(tm // r):
                r0 = rb * r
                g_scr[r0:r0 + r, ls] = d_ref[r0:r0 + r, ls] * _dsilu(xc_ref[r0:r0 + r, ls])
            g_scr[tm:tm + 8, ls] = jnp.where(i < n - 1, dn_ref[:, ls] * _dsilu(xcn_ref[:, ls]), 0.0)
            db = jnp.zeros((1, LANES), F32)
            dws = [jnp.zeros((1, LANES), F32) for _ in range(SSM_CONV)]
            for rb in range(tm // r):
                r0 = rb * r
                g = g_scr[r0:r0 + r, ls]
                x = x_ref[r0:r0 + r, ls]
                db = db + _colsum(g)
                dws[3] = dws[3] + _colsum(g * x)
                acc = g * w[3:4]
                for j in range(1, SSM_CONV):
                    if rb == 0:
                        halo = jnp.where(i > 0, xh_ref[:, ls], 0.0)
                        dws[3 - j] = dws[3 - j] + _colsum(g * jnp.where(rowr >= j, pltpu.roll(x, j, 0), 0.0))
                        dws[3 - j] = dws[3 - j] + _colsum(jnp.where(row8 < j, g[0:8] * pltpu.roll(halo, j, 0), 0.0))
                    else:
                        dws[3 - j] = dws[3 - j] + _colsum(g * x_ref[r0 - j:r0 - j + r, ls])
                    acc = acc + w[3 - j:4 - j] * g_scr[r0 + j:r0 + j + r, ls]
                dx_ref[r0:r0 + r, ls] = acc.astype(dx_ref.dtype)
            db_ref[:, ls] += db
            for k in range(SSM_CONV):
                dw_ref[k:k + 1, ls] += dws[k]

    tile = pl.BlockSpec((tm, tc), lambda jc, i: (i, jc))
    prev8 = pl.BlockSpec((8, tc), lambda jc, i: (jnp.maximum(i * hb - 1, 0), jc))
    next8 = pl.BlockSpec((8, tc), lambda jc, i: (jnp.minimum((i + 1) * hb, n * hb - 1), jc))
    return _pcall(body, name=name, grid=(c // tc, n),
                  in_specs=[tile, tile, tile, prev8, next8, next8, pl.BlockSpec((8, tc), lambda jc, i: (0, jc))],
                  out_specs=[tile, pl.BlockSpec((8, tc), lambda jc, i: (0, jc)), pl.BlockSpec((1, tc), lambda jc, i: (0, jc))],
                  out_shape=[_sds((s, c), BF16), _sds((8, c), F32), _sds((1, c), F32)],
                  scratch=[pltpu.VMEM((tm + 8, tc), F32)],
                  sem=("parallel", "arbitrary"))(dxa, xc, xpre, xpre, dxa, xc, w8)


def _ssd_pre(dtraw, dt_bias, a_log, cst, name):
    def fn(raw, bias, alog, expand):
        tm = raw.shape[0]
        z = raw + bias
        dt = jnp.maximum(z, 0.0) + jnp.log(1.0 + jnp.exp(-jnp.abs(z)))
        lane = lax.broadcasted_iota(jnp.int32, (tm, LANES), 1)
        pos = lax.broadcasted_iota(jnp.int32, (tm, LANES), 0) & (SSM_CHUNK - 1)
        dt = jnp.where(lane < SSM_HEADS, dt, 0.0)
        c = dt * (-jnp.exp(alog))
        sh = 1
        while sh < SSM_CHUNK:
            c = c + jnp.where(pos >= sh, pltpu.roll(c, sh, 0), 0.0)
            sh *= 2
        return dt, c, _dot_split(dt, expand), _dot_split(c, expand)
    wide = SSM_HEADS * HEAD_DIM
    return _rowwise(fn, [dtraw], [dt_bias, a_log, cst["expand"]],
                    [(LANES, F32), (LANES, F32), (wide, F32), (wide, F32)], [], name=name)


def _ssd_post(dacs, ddt, dtraw, dt, dt_bias, a_log, name):
    def fn(dacs_v, ddt_v, raw, dt_v, bias, alog):
        tm = raw.shape[0]
        pos = lax.broadcasted_iota(jnp.int32, (tm, LANES), 0) & (SSM_CHUNK - 1)
        a = -jnp.exp(alog)
        c = dacs_v
        sh = 1
        while sh < SSM_CHUNK:
            c = c + jnp.where(pos + sh < SSM_CHUNK, pltpu.roll(c, tm - sh, 0), 0.0)
            sh *= 2
        draw = (ddt_v + c * a) * _sigmoid(raw + bias)
        return draw, _colsum(draw), _colsum(c * dt_v * a)
    return _rowwise(fn, [dacs, ddt, dtraw, dt], [dt_bias, a_log], [(LANES, BF16)], [(1, LANES)] * 2, name=name)


def _heads_rows(x, s):
    return x[:, :SSM_HEADS].reshape(s // SSM_CHUNK, SSM_CHUNK, SSM_GROUPS, 4).transpose(2, 0, 3, 1)


def _ssd_constants():
    gp = SSD_GROUPS_PER_STEP
    src = np.arange(LANES)[:, None]
    expand = src == np.arange(SSM_HEADS * HEAD_DIM)[None, :] // HEAD_DIM
    dst = np.arange(LANES)[None, None, :] - 4 * np.arange(gp)[:, None, None]
    seg = np.arange(SSM_GROUP_WIDTH)[None, :, None] // HEAD_DIM == dst
    seg4 = np.arange(4 * LANES)[None, :, None] // LANES == dst
    return dict(expand=jnp.asarray(expand, BF16), seg=jnp.asarray(seg, BF16), seg4=jnp.asarray(seg4, BF16))


def _ssm_weights(w_in, conv_w, conv_b, dt_bias, a_log, d_skip, norm_w, w_out):
    pad = ((0, 0), (0, LANES - SSM_HEADS))
    w_xbc = _group_cols(w_in[:, 2048:6144])
    w_dt = jnp.pad(w_in[:, 6144:], pad)
    return dict(
        ssm_z=w_in[:, :2048], ssm_xbc=w_xbc, ssm_dt=w_dt,
        ssm_zx=jnp.concatenate([w_in[:, :2048], w_xbc], axis=1),
        ssm_out=w_out,
        conv_w8=_group_cols(jnp.pad(conv_w, ((0, 8 - SSM_CONV), (0, 0)))),
        conv_b=_group_cols(conv_b), norm_w=norm_w,
        dt_bias=jnp.pad(dt_bias, pad), a_log=jnp.pad(a_log, pad),
        d_e=jnp.repeat(d_skip.reshape(SSM_GROUPS, 4), HEAD_DIM, axis=1)[:, None, :],
        ssd_cst=_ssd_constants())


def _ssd_setup(acs_e, acsr):
    l = SSM_CHUNK
    last = acsr[:, l - 1:l]
    lane1 = lax.broadcasted_iota(jnp.int32, (1, SSM_GROUP_WIDTH), 1)
    last_e = last[3:4, :]
    for r in (2, 1, 0):
        last_e = jnp.where(lane1 < HEAD_DIM * (r + 1), last[r:r + 1, :], last_e)
    return jnp.exp(acs_e), jnp.exp(last_e - acs_e), jnp.exp(last_e)


def _head_bcast(acs_e):
    lo = lax.broadcasted_iota(jnp.int32, (acs_e.shape[0], LANES), 1) < HEAD_DIM
    out = []
    for p in range(2):
        blk = acs_e[:, p * LANES:(p + 1) * LANES]
        rolled = pltpu.roll(blk, HEAD_DIM, 1)
        out += [jnp.where(lo, blk, rolled), jnp.where(lo, rolled, blk)]
    return out


def _group_cols(a):
    lead = a.shape[:-1]
    x = a[..., :2048].reshape(lead + (SSM_GROUPS, SSM_GROUP_WIDTH))
    b = a[..., 2048:3072].reshape(lead + (SSM_GROUPS, SSM_STATE))
    c = a[..., 3072:].reshape(lead + (SSM_GROUPS, SSM_STATE))
    return jnp.concatenate([x, b, c], axis=-1).reshape(lead + (4096,))


def _ungroup_cols(a):
    lead = a.shape[:-1]
    y = a.reshape(lead + (SSM_GROUPS, SSM_GROUP_WIDTH + 2 * SSM_STATE))
    return jnp.concatenate([y[..., :256].reshape(lead + (2048,)), y[..., 256:384].reshape(lead + (1024,)),
                            y[..., 384:].reshape(lead + (1024,))], axis=-1)


def _ssd_fwd2(xa, dte, acse, acsr, d_e, name):
    s = xa.shape[0]
    l, gw, ns = SSM_CHUNK, SSM_GROUP_WIDTH, SSM_STATE
    nc = s // l

    gb = gw + 2 * ns
    gp = SSD_GROUPS_PER_STEP

    def body(xa_ref, dt_ref, acs_ref, acsr_ref, d_ref, y_ref, hp_ref, h_sc):
        @pl.when(pl.program_id(1) == 0)
        def _():
            h_sc[...] = jnp.zeros_like(h_sc)
        lane = lax.broadcasted_iota(jnp.int32, (l, gw), 1)
        tril = _causal(l)
        for gi in range(gp):
            x = xa_ref[:, gi * gb:gi * gb + gw]
            bm = xa_ref[:, gi * gb + gw:gi * gb + gw + ns].astype(BF16)
            cm = xa_ref[:, gi * gb + gw + ns:(gi + 1) * gb].astype(BF16)
            acsr = acsr_ref[gi]
            dt_e, acs_e = dt_ref[:, gi * gw:(gi + 1) * gw], acs_ref[:, gi * gw:(gi + 1) * gw]
            acs_bc = _head_bcast(acs_e)
            e_e, dte_e, cd_e = _ssd_setup(acs_e, acsr)
            xdt = x * dt_e
            xdt16 = xdt.astype(BF16)
            cb = _dot(cm, bm, NT_DIMS)
            yd = jnp.zeros((l, gw), F32)
            for r in range(4):
                lm = jnp.exp(jnp.where(tril, acs_bc[r] - acsr[r:r + 1, :], -jnp.inf))
                yr = _dot((cb * lm).astype(BF16), xdt16)
                yd = jnp.where((lane >= HEAD_DIM * r) & (lane < HEAD_DIM * (r + 1)), yr, yd)
            hp = h_sc[gi]
            hp_ref[gi] = hp
            y_ref[:, gi * gw:(gi + 1) * gw] = yd + _dot(cm, hp.astype(BF16)) * e_e + x * d_ref[gi]
            h_sc[gi] = hp * cd_e + _dot(bm, (xdt * dte_e).astype(BF16), TN_DIMS)

    return _pcall(
        body, name=name, grid=(SSM_GROUPS // gp, nc),
        in_specs=[pl.BlockSpec((l, gp * gb), lambda g, c: (c, g)),
                  pl.BlockSpec((l, gp * gw), lambda g, c: (c, g)),
                  pl.BlockSpec((l, gp * gw), lambda g, c: (c, g)),
                  pl.BlockSpec((gp, None, 4, l), lambda g, c: (g, c, 0, 0)),
                  pl.BlockSpec((gp, 1, gw), lambda g, c: (g, 0, 0))],
        out_specs=[pl.BlockSpec((l, gp * gw), lambda g, c: (c, g)),
                   pl.BlockSpec((gp, None, ns, gw), lambda g, c: (g, c, 0, 0))],
        out_shape=[_sds((s, 2048), F32), _sds((SSM_GROUPS, nc, ns, gw), F32)],
        scratch=[pltpu.VMEM((gp, ns, gw), F32)],
        sem=("parallel", "arbitrary"))(xa, dte, acse, acsr, d_e)


def _ssd_bwd2(dy, xa, dte, acse, acsr, d_e, hprev, cst, name):
    s = xa.shape[0]
    l, gw, ns = SSM_CHUNK, SSM_GROUP_WIDTH, SSM_STATE
    nc = s // l

    gb = gw + 2 * ns
    gp = SSD_GROUPS_PER_STEP

    def body(dy_ref, xa_ref, dt_ref, acs_ref, acsr_ref, d_ref, hp_ref,
             seg_ref, seg4_ref, dxa_ref, dacs_ref, ddt_ref, dd_ref, dh_sc):
        @pl.when(pl.program_id(1) == 0)
        def _():
            dh_sc[...] = jnp.zeros_like(dh_sc)
            dd_ref[...] = jnp.zeros_like(dd_ref)
        parts = [one_group(gi, dy_ref, xa_ref, dt_ref, acs_ref, acsr_ref, d_ref, hp_ref, seg_ref, seg4_ref,
                           dxa_ref, dh_sc) for gi in range(gp)]
        dacs_ref[...] = _sum_list([p[0] for p in parts])
        ddt_ref[...] = _sum_list([p[1] for p in parts])
        dd_ref[0:1, :] += _sum_list([p[2] for p in parts])

    def one_group(gi, dy_ref, xa_ref, dt_ref, acs_ref, acsr_ref, d_ref, hp_ref, seg_ref, seg4_ref, dxa_ref, dh_sc):
        dyv = dy_ref[:, gi * gw:(gi + 1) * gw]
        x = xa_ref[:, gi * gb:gi * gb + gw]
        bm = xa_ref[:, gi * gb + gw:gi * gb + gw + ns].astype(BF16)
        cm = xa_ref[:, gi * gb + gw + ns:(gi + 1) * gb].astype(BF16)
        acsr = acsr_ref[gi]
        dt_e, acs_e = dt_ref[:, gi * gw:(gi + 1) * gw], acs_ref[:, gi * gw:(gi + 1) * gw]
        acs_bc = _head_bcast(acs_e)
        e_e, dte_e, cd_e = _ssd_setup(acs_e, acsr)
        seg, seg4 = seg_ref[gi], seg4_ref[gi]
        lane = lax.broadcasted_iota(jnp.int32, (l, gw), 1)
        xdt = x * dt_e
        xdt16 = xdt.astype(BF16)
        dy16 = dyv.astype(BF16)
        cb = _dot(cm, bm, NT_DIMS)
        cbt = _dot(bm, cm, NT_DIMS)
        hp = hp_ref[gi]
        hp16 = hp.astype(BF16)
        g = dh_sc[gi]
        g16 = g.astype(BF16)
        t_all = _dot(cm, hp16)
        dt16 = (dyv * e_e).astype(BF16)
        dc = _dot(dt16, hp16, NT_DIMS)
        dhp = _dot(cm, dt16, TN_DIMS)
        wv = xdt * dte_e
        dw = _dot(bm, g16)
        db = _dot(wv.astype(BF16), g16, NT_DIMS)
        dxdt = dw * dte_e
        acs_term = dyv * t_all * e_e - dw * wv
        last_term = _colsum(dw * wv) + _colsum(g * hp) * cd_e
        dh_sc[gi] = g * cd_e + dhp
        tril, triu = _causal(l), _causal(l, transposed=True)
        dcb = jnp.zeros((l, l), F32)
        dcbt = jnp.zeros((l, l), F32)
        qd = []
        for r in range(4):
            in_head = (lane >= HEAD_DIM * r) & (lane < HEAD_DIM * (r + 1))
            a_col = acs_bc[r]
            lm = jnp.exp(jnp.where(tril, a_col - acsr[r:r + 1, :], -jnp.inf))
            lmt = jnp.exp(jnp.where(triu, acsr[r:r + 1, :] - a_col, -jnp.inf))
            mm_, mt = cb * lm, cbt * lmt
            dyr = jnp.where(in_head, dy16, jnp.zeros_like(dy16))
            dm = _dot(dyr, xdt16, NT_DIMS)
            dmt = _dot(xdt16, dyr, NT_DIMS)
            dxdt = dxdt + jnp.where(in_head, _dot(mt.astype(BF16), dy16), 0.0)
            dcb = dcb + dm * lm
            dcbt = dcbt + dmt * lmt
            qd.append(dm * mm_ - dmt * mt)
        dc = dc + _dot(dcb.astype(BF16), bm)
        db = db + _dot(dcbt.astype(BF16), cm)
        rowl = lax.broadcasted_iota(jnp.int32, (l, LANES), 0)
        row8 = lax.broadcasted_iota(jnp.int32, (8, gw), 0)
        small = _dot_split(jnp.where(row8 == 0, last_term, jnp.where(row8 == 1, _colsum(dyv * x), 0.0)), seg, passes=2)
        big = _dot_split(jnp.concatenate([acs_term, dxdt * x], axis=0), seg, passes=2)
        dacs = (big[0:l] + _dot_split(jnp.concatenate(qd, axis=1), seg4, passes=2)
                + jnp.where(rowl == l - 1, small[0:1, :], 0.0))
        dxa_ref[:, gi * gb:(gi + 1) * gb] = jnp.concatenate([dxdt * dt_e + dyv * d_ref[gi], db, dc], axis=1)
        return dacs, big[l:2 * l], small[1:2, :]

    rc = lambda c: nc - 1 - c
    ng = SSM_GROUPS // gp
    return _pcall(
        body, name=name, grid=(ng, nc),
        in_specs=[pl.BlockSpec((l, gp * gw), lambda g, c: (rc(c), g)),
                  pl.BlockSpec((l, gp * gb), lambda g, c: (rc(c), g)),
                  pl.BlockSpec((l, gp * gw), lambda g, c: (rc(c), g)),
                  pl.BlockSpec((l, gp * gw), lambda g, c: (rc(c), g)),
                  pl.BlockSpec((gp, None, 4, l), lambda g, c: (g, rc(c), 0, 0)),
                  pl.BlockSpec((gp, 1, gw), lambda g, c: (g, 0, 0)),
                  pl.BlockSpec((gp, None, ns, gw), lambda g, c: (g, rc(c), 0, 0)),
                  pl.BlockSpec((gp, gw, LANES), lambda g, c: (0, 0, 0)),
                  pl.BlockSpec((gp, 4 * LANES, LANES), lambda g, c: (0, 0, 0))],
        out_specs=[pl.BlockSpec((l, gp * gb), lambda g, c: (rc(c), g)),
                   pl.BlockSpec((None, l, LANES), lambda g, c: (g, rc(c), 0)),
                   pl.BlockSpec((None, l, LANES), lambda g, c: (g, rc(c), 0)),
                   pl.BlockSpec((None, 8, LANES), lambda g, c: (g, 0, 0))],
        out_shape=[_sds((s, 4096), F32), _sds((ng, s, LANES), F32), _sds((ng, s, LANES), F32),
                   _sds((ng, 8, LANES), F32)],
        scratch=[pltpu.VMEM((gp, ns, gw), F32)],
        sem=("parallel", "arbitrary"))(dy, xa, dte, acse, acsr, d_e, hprev, cst["seg"], cst["seg4"])


def _gate_norm(y, z, nw, name):
    c = y.shape[1]

    def fn(yv, zv, w):
        outs = []
        for k in range(c // SSM_GROUP_WIDTH):
            sl = slice(k * SSM_GROUP_WIDTH, (k + 1) * SSM_GROUP_WIDTH)
            yg = yv[:, sl] * _silu(zv[:, sl])
            rinv = lax.rsqrt(jnp.mean(yg * yg, axis=-1, keepdims=True) + RMS_EPS)
            outs.append(yg * rinv * w[:, sl])
        return (jnp.concatenate(outs, axis=1),)
    return _rowwise(fn, [y, z], [nw], [(c, BF16)], [], name=name)[0]


def _out_gate_norm_bwd(dy, w_out, y, z, nw, name):
    s, c = y.shape
    tm, tn = min(512, s), min(1024, c)
    k = dy.shape[1]

    def body(a_ref, b_ref, y_ref, z_ref, w_ref, dy_ref, dz_ref, dw_ref):
        @pl.when(pl.program_id(1) == 0)
        def _():
            dw_ref[...] = jnp.zeros_like(dw_ref)
        dv = _dot(a_ref[...], b_ref[...], NT_DIMS)
        yv, zv, w = y_ref[...], z_ref[...], w_ref[...]
        dys, dzs, dws = [], [], []
        for g in range(tn // SSM_GROUP_WIDTH):
            sl = slice(g * SSM_GROUP_WIDTH, (g + 1) * SSM_GROUP_WIDTH)
            ys, zs, ds = yv[:, sl], zv[:, sl], dv[:, sl]
            sz = _silu(zs)
            yg = ys * sz
            rinv = lax.rsqrt(jnp.mean(yg * yg, axis=-1, keepdims=True) + RMS_EPS)
            nrm = yg * rinv
            dn = ds * w[:, sl]
            dyg = rinv * (dn - nrm * jnp.mean(dn * nrm, axis=-1, keepdims=True))
            dys.append(dyg * sz)
            dzs.append(dyg * ys * _dsilu(zs))
            dws.append(_colsum(ds * nrm))
        dy_ref[...] = jnp.concatenate(dys, axis=1)
        dz_ref[...] = jnp.concatenate(dzs, axis=1).astype(dz_ref.dtype)
        dw_ref[...] += jnp.concatenate(dws, axis=1)

    tile = pl.BlockSpec((tm, tn), lambda j, i: (i, j))
    row = pl.BlockSpec((1, tn), lambda j, i: (0, j))
    return _pcall(body, name=name, grid=(c // tn, s // tm),
                  in_specs=[pl.BlockSpec((tm, k), lambda j, i: (i, 0)), pl.BlockSpec((tn, k), lambda j, i: (j, 0)),
                            tile, tile, row],
                  out_specs=[tile, tile, row], out_shape=[_sds((s, c), F32), _sds((s, c), BF16), _sds((1, c), F32)],
                  sem=("parallel", "arbitrary"))(dy, w_out, y, z, nw)


def _ssd_forward(u, w):
    s = u.shape[0]
    z = _mm(u, w["ssm_z"], "nn", [F32], name="ssm_zproj")
    xpre = _mm(u, w["ssm_xbc"], "nn", [F32], name="ssm_xproj")
    dtraw = _mm(u, w["ssm_dt"], "nn", [F32], name="ssm_dtproj")
    xc, xa = _conv_fwd(xpre, w["conv_w8"], w["conv_b"], "ssm_conv")
    dt, acs, dte, acse = _ssd_pre(dtraw, w["dt_bias"], w["a_log"], w["ssd_cst"], "ssm_pre")
    acsr = _heads_rows(acs, s)
    y, hprev = _ssd_fwd2(xa, dte, acse, acsr, w["d_e"], "ssm_scan")
    yn = _gate_norm(y, z, w["norm_w"], "ssm_gate_norm")
    out = _mm(yn, w["ssm_out"], "nn", [F32], name="ssm_oproj")
    return out, dict(u=u, z=z, xpre=xpre, xc=xc, xa=xa, dtraw=dtraw, dt=dt, dte=dte, acse=acse,
                     acsr=acsr, y=y, hprev=hprev, yn=yn)


def _ssd_backward(dy, sv, w):
    s = dy.shape[0]
    g_wout = _mm(sv["yn"], dy, "tn", [F32], name="ssm_gwout")
    dys, dz, dnw = _out_gate_norm_bwd(dy, w["ssm_out"], sv["y"], sv["z"], w["norm_w"], "ssm_dyn_gate_norm_bwd")
    dxa, dacs_c, ddt_c, dd = _ssd_bwd2(dys, sv["xa"], sv["dte"], sv["acse"], sv["acsr"], w["d_e"], sv["hprev"],
                                       w["ssd_cst"], "ssm_scan_bwd")
    per_step = 4 * SSD_GROUPS_PER_STEP
    pad = ((0, 0), (0, LANES - SSM_HEADS))
    dacs = jnp.pad(jnp.concatenate([a[:, :per_step] for a in dacs_c], axis=1), pad)
    ddt = jnp.pad(jnp.concatenate([a[:, :per_step] for a in ddt_c], axis=1), pad)
    draw, dbias, dalog = _ssd_post(dacs, ddt, sv["dtraw"], sv["dt"], w["dt_bias"], w["a_log"], "ssm_post")
    dxpre, dcw, dcb = _conv_bwd(dxa, sv["xc"], sv["xpre"], w["conv_w8"], "ssm_conv_bwd")
    du = _mm(draw, w["ssm_dt"], "nt", [F32], name="ssm_du_dt")
    n_z = dz.shape[1]
    du = _mm_nt_blocks(_k_blocks(dz), w["ssm_zx"][:, :n_z], du, name="ssm_du_z")
    du = _mm_nt_blocks(_k_blocks(dxpre), w["ssm_zx"][:, n_z:], du, name="ssm_du_x")
    g_win = jnp.concatenate([_mm(sv["u"], dz, "tn", [F32], name="ssm_gwin_z"),
                             _ungroup_cols(_mm(sv["u"], dxpre, "tn", [F32], name="ssm_gwin_x")),
                             _mm(sv["u"], draw, "tn", [F32], name="ssm_gwin_dt")[:, :SSM_HEADS]], axis=1)
    return du, dict(ssm_w_in=g_win, ssm_w_out=g_wout, ssm_conv_w=_ungroup_cols(dcw[:SSM_CONV]),
                    ssm_conv_b=_ungroup_cols(dcb), ssm_norm_w=dnw, ssm_dt_bias=dbias[:, :SSM_HEADS],
                    ssm_a_log=dalog[:, :SSM_HEADS], ssm_d=dd[:, 0, :per_step].reshape(1, SSM_HEADS))


def _mlp_forward(u2, w1, w2, tag):
    def epi(acc):
        hr = jnp.maximum(acc, 0.0)
        return hr, hr * hr
    hr, a = _mm(u2, w1, "nn", [BF16, BF16], name=tag + "_mlp_up", epi=epi)
    y2 = _mm(a, w2, "nn", [F32], name=tag + "_mlp_down")
    return y2, hr, a


def _mlp_backward(dy2, u2, hr, a, w1, w2, tag):
    dh = _mm(dy2, w2, "nt", [BF16], name=tag + "_mlp_dh", extra=(hr,),
             epi=lambda acc, h: (acc * (2.0 * h.astype(F32)),))
    g_w2 = _mm(a, dy2, "tn", [F32], name=tag + "_mlp_gw2")
    g_w1 = _mm(u2, dh, "tn", [F32], name=tag + "_mlp_gw1")
    du2 = _mm(dh, w1, "nt", [F32], name=tag + "_mlp_du")
    return du2, g_w1, g_w2


def _ada_forward(c16, ada_w, ada_b_cols, name):
    nl, d, cols = ada_w.shape
    tn = 512

    def body(c_ref, w_ref, b_ref, o_ref):
        cond = _silu(c_ref[...]).astype(BF16)
        o_ref[...] = _dot(cond, w_ref[...].astype(BF16)) + b_ref[...]

    return _pcall(body, name=name, grid=(nl, cols // tn),
                  in_specs=[pl.BlockSpec((16, d), lambda i, j: (0, 0)),
                            pl.BlockSpec((None, d, tn), lambda i, j: (i, 0, j)),
                            pl.BlockSpec((None, 1, tn), lambda i, j: (i, 0, j))],
                  out_specs=pl.BlockSpec((None, 16, tn), lambda i, j: (i, 0, j)),
                  out_shape=_sds((nl, 16, cols), F32), sem=("parallel", "parallel"))(c16, ada_w, ada_b_cols)


def _ada_backward(c_t, dmod_cols, name):
    d, nb = c_t.shape
    nl, _, cols = dmod_cols.shape
    tn = 512

    def body(c_ref, dm_ref, o_ref):
        cond = _silu(c_ref[...])
        dm = dm_ref[...]
        acc = cond[:, 0:1] * dm[0:1, :]
        for b in range(1, nb):
            acc = acc + cond[:, b:b + 1] * dm[b:b + 1, :]
        o_ref[...] = acc

    return _pcall(body, name=name, grid=(nl, cols // tn),
                  in_specs=[pl.BlockSpec((d, nb), lambda i, j: (0, 0)),
                            pl.BlockSpec((None, nb, tn), lambda i, j: (i, 0, j))],
                  out_specs=pl.BlockSpec((None, d, tn), lambda i, j: (i, 0, j)),
                  out_shape=_sds((nl, d, cols), F32), sem=("parallel", "parallel"))(c_t, dmod_cols)


def _adamw(w, g, m, v, name):
    rows, cols = w.shape
    tm = rows
    for cand in (256, 128, 64, 32, 16, 8):
        if rows % cand == 0 and rows > cand:
            tm = cand
            break
    c1 = 1.0 / (1.0 - ADAM_B1 ** ADAM_STEP)
    c2 = 1.0 / (1.0 - ADAM_B2 ** ADAM_STEP)

    def fn(wv, gv, mv, vv):
        mn = ADAM_B1 * mv + (1.0 - ADAM_B1) * gv
        vn = ADAM_B2 * vv + (1.0 - ADAM_B2) * (gv * gv)
        delta = -ADAM_LR * ((mn * c1) / (jnp.sqrt(vn * c2) + ADAM_EPS) + ADAM_WD * wv)
        return delta, mn, vn
    return _rowwise(fn, [w, g, m, v], [], [(cols, F32)] * 3, [], name=name, tm=tm)


def _my_pos():
    return lax.axis_index("x"), lax.axis_index("y"), lax.axis_index("c")


def _allgather8(x, name):
    r, c = x.shape

    def body(x_ref, out_ref, send_sems, recv_sems, local_sem):
        mx, my, mc = _my_pos()
        me = 4 * mx + 2 * my + mc
        mine = pltpu.make_async_copy(x_ref, out_ref.at[me], local_sem)
        mine.start()
        copies = []
        for k in range(1, 8):
            fx, fy, fc = (k >> 2) & 1, (k >> 1) & 1, k & 1
            px = 1 - mx if fx else mx
            py = 1 - my if fy else my
            pc = 1 - mc if fc else mc
            peer = 4 * px + 2 * py + pc
            send = pltpu.make_async_remote_copy(src_ref=x_ref, dst_ref=out_ref.at[me], send_sem=send_sems.at[k - 1],
                                                recv_sem=recv_sems.at[k - 1], device_id=(px, py, pc),
                                                device_id_type=MESH)
            send.start()
            recv = pltpu.make_async_remote_copy(src_ref=x_ref, dst_ref=out_ref.at[peer], send_sem=send_sems.at[k - 1],
                                                recv_sem=recv_sems.at[k - 1], device_id=(px, py, pc),
                                                device_id_type=MESH)
            copies.append((send, recv))
        for send, recv in copies:
            recv.wait_recv()
        for send, recv in copies:
            send.wait_send()
        mine.wait()

    vm = pl.BlockSpec(memory_space=pltpu.VMEM)
    return _pcall(body, name=name, in_specs=[vm], out_specs=vm, out_shape=_sds((8, r, c), x.dtype),
                  scratch=[pltpu.SemaphoreType.DMA((7,)), pltpu.SemaphoreType.DMA((7,)), pltpu.SemaphoreType.DMA])(x)


def _chip_flips(mx, my):
    out = []
    for fx, fy in ((1, 0), (0, 1), (1, 1)):
        px = 1 - mx if fx else mx
        py = 1 - my if fy else my
        out.append((px, py, 2 * px + py))
    return out


def _gather_chips(shard2, name):
    _, h, c = shard2.shape

    def body(x_ref, out_ref, send_sems, recv_sems):
        mx, my, mc = _my_pos()
        oc = 1 - mc
        mk = 2 * mx + my
        flips = _chip_flips(mx, my)

        def copy(k, src, dst, to):
            return pltpu.make_async_remote_copy(src_ref=src, dst_ref=dst, send_sem=send_sems.at[k],
                                                recv_sem=recv_sems.at[k], device_id=to, device_id_type=MESH)

        first = [copy(j, x_ref.at[mc], out_ref.at[mk, mc], (px, py, mc)) for j, (px, py, pk) in enumerate(flips)]
        for cp in first:
            cp.start()
        passed = []
        for j, (px, py, pk) in enumerate(flips):
            copy(j, x_ref.at[mc], out_ref.at[pk, mc], (px, py, mc)).wait_recv()
            fw = copy(3 + j, out_ref.at[pk, mc], out_ref.at[pk, mc], (mx, my, oc))
            fw.start()
            passed.append(fw)
        for j, (px, py, pk) in enumerate(flips):
            copy(3 + j, out_ref.at[pk, oc], out_ref.at[pk, oc], (mx, my, oc)).wait_recv()
        for cp in first + passed:
            cp.wait_send()

    return _pcall(body, name=name, in_specs=[HBM_SPEC], out_specs=HBM_SPEC, out_shape=_sds((4, 2, h, c), shard2.dtype),
                  scratch=[pltpu.SemaphoreType.DMA((6,)), pltpu.SemaphoreType.DMA((6,))])(shard2)


def _pair_exchange(g4, name):
    n, _, h, c = g4.shape

    def body(g_ref, out_ref, send_sem, recv_sem):
        mx, my, mc = _my_pos()
        oc = 1 - mc
        copies = []
        for k in range(n):
            cp = pltpu.make_async_remote_copy(src_ref=g_ref.at[k, oc], dst_ref=out_ref.at[k], send_sem=send_sem.at[k],
                                              recv_sem=recv_sem.at[k], device_id=(mx, my, oc), device_id_type=MESH)
            cp.start()
            copies.append(cp)
        for cp in copies:
            cp.wait_recv()
        for cp in copies:
            cp.wait_send()

    return _pcall(body, name=name, in_specs=[HBM_SPEC], out_specs=HBM_SPEC, out_shape=_sds((n, h, c), g4.dtype),
                  scratch=[pltpu.SemaphoreType.DMA((n,)), pltpu.SemaphoreType.DMA((n,))])(g4)


def _pair_add(g4, recv, core, name):
    n, _, h, c = g4.shape
    tm = _row_tile(h)

    def body(core_ref, a_ref, b_ref, o_ref, o16_ref):
        acc = a_ref[...] + b_ref[...]
        o_ref[...] = acc
        o16_ref[...] = acc.astype(BF16)

    out_spec = pl.BlockSpec((None, tm, c), lambda k, i, cr: (k, i, 0))
    return _pcall(body, name=name, grid=(n, h // tm), prefetch=1,
                  in_specs=[pl.BlockSpec((None, None, tm, c), lambda k, i, cr: (k, cr[0], i, 0)), out_spec],
                  out_specs=[out_spec, out_spec], out_shape=[_sds((n, h, c), F32), _sds((n, h, c), BF16)],
                  sem=("parallel", "parallel"))(core, g4, recv)


def _chip_exchange(p, name):
    n, h, c = p.shape

    def body(p_ref, out_ref, send_sems, recv_sems):
        mx, my, mc = _my_pos()
        copies = []
        for j, (px, py, pk) in enumerate(_chip_flips(mx, my)):
            cp = pltpu.make_async_remote_copy(src_ref=p_ref.at[pk], dst_ref=out_ref.at[j], send_sem=send_sems.at[j],
                                              recv_sem=recv_sems.at[j], device_id=(px, py, mc), device_id_type=MESH)
            cp.start()
            copies.append(cp)
        for cp in copies:
            cp.wait_recv()
        for cp in copies:
            cp.wait_send()

    return _pcall(body, name=name, in_specs=[HBM_SPEC], out_specs=HBM_SPEC, out_shape=_sds((3, h, c), p.dtype),
                  scratch=[pltpu.SemaphoreType.DMA((3,)), pltpu.SemaphoreType.DMA((3,))])(p)


def _chip_sum(p, slots, chip, name):
    _, h, c = p.shape
    tm = _row_tile(h)

    def body(chip_ref, p_ref, q_ref, o_ref):
        o_ref[...] = ((p_ref[...] + q_ref[0].astype(F32)) + q_ref[1].astype(F32)) + q_ref[2].astype(F32)

    return _pcall(body, name=name, grid=(h // tm,), prefetch=1,
                  in_specs=[pl.BlockSpec((None, tm, c), lambda i, ch: (ch[0], i, 0)),
                            pl.BlockSpec((3, tm, c), lambda i, ch: (0, i, 0))],
                  out_specs=pl.BlockSpec((tm, c), lambda i, ch: (i, 0)),
                  out_shape=_sds((h, c), F32), sem=("parallel",))(chip, p, slots)


def _sum_slots(q, name):
    n, h, c = q.shape
    tm = _row_tile(h)

    def body(q_ref, o_ref):
        acc = q_ref[0]
        for k in range(1, n):
            acc = acc + q_ref[k]
        o_ref[...] = acc

    return _pcall(body, name=name, grid=(h // tm,),
                  in_specs=[pl.BlockSpec((n, tm, c), lambda i: (0, i, 0))],
                  out_specs=pl.BlockSpec((tm, c), lambda i: (i, 0)),
                  out_shape=_sds((h, c), F32), sem=("parallel",))(q)


def _pair_share(f, name):
    h, c = f.shape

    def body(f_ref, out_ref, send_sem, recv_sem):
        mx, my, mc = _my_pos()
        cp = pltpu.make_async_remote_copy(src_ref=f_ref, dst_ref=out_ref, send_sem=send_sem, recv_sem=recv_sem,
                                          device_id=(mx, my, 1 - mc), device_id_type=MESH)
        cp.start()
        cp.wait_recv()
        cp.wait_send()

    return _pcall(body, name=name, in_specs=[HBM_SPEC], out_specs=HBM_SPEC, out_shape=_sds((h, c), f.dtype),
                  scratch=[pltpu.SemaphoreType.DMA, pltpu.SemaphoreType.DMA])(f)


BIG = ("mlp_w1", "mlp_w2", "fox_w_in", "fox_w_o", "ssm_w_in", "ssm_w_out")
SMALL_SHARDED = ("ssm_conv_w", "ssm_conv_b", "ssm_norm_w")
PACK_COLS = 1024


def _pack_rows(parts, rows_multiple, dtype):
    flat = jnp.concatenate([p.reshape(-1).astype(dtype) for p in parts])
    unit = rows_multiple * PACK_COLS
    total = -(-flat.shape[0] // unit) * unit
    flat = jnp.pad(flat, (0, total - flat.shape[0]))
    return flat.reshape(total // PACK_COLS, PACK_COLS)


def _unpack(flat, shapes):
    out, off = [], 0
    for sh in shapes:
        n = 1
        for d_ in sh:
            n *= d_
        out.append(flat[off:off + n].reshape(sh))
        off += n
    return out


PIECE_ROWS = 16


def _piece_rows(shape):
    n = 1
    for d_ in shape:
        n *= d_
    rows = -(-n // PACK_COLS)
    return n, -(-rows // PIECE_ROWS) * PIECE_ROWS


def _pack2d(parts, rows_multiple, dtype):
    blocks = []
    for p in parts:
        n, rows = _piece_rows(p.shape)
        a = p.astype(dtype)
        if p.shape[-1] != PACK_COLS or n % PACK_COLS:
            a = jnp.pad(a.reshape(-1), (0, -n % PACK_COLS))
        a = a.reshape(-1, PACK_COLS)
        blocks.append(jnp.pad(a, ((0, rows - a.shape[0]), (0, 0))))
    total = sum(b.shape[0] for b in blocks)
    pad = -total % rows_multiple
    if pad:
        blocks.append(jnp.zeros((pad, PACK_COLS), dtype))
    return jnp.concatenate(blocks, axis=0)


def _unpack2d(buf, shapes):
    out, off = [], 0
    for sh in shapes:
        n, rows = _piece_rows(sh)
        piece = buf[off:off + rows]
        if sh[-1] == PACK_COLS and n % PACK_COLS == 0:
            out.append(piece[:n // PACK_COLS].reshape(sh))
        else:
            out.append(piece.reshape(-1)[:n].reshape(sh))
        off += rows
    return out


def _row_tile(h, cap=512):
    for step in (16, 8):
        best = 0
        for cand in range(step, cap + 1, step):
            if h % cand == 0:
                best = cand
        if best:
            return best
    return h


def _chip_slice(full, axis, k, width):
    idx = [slice(None)] * full.ndim
    idx[axis] = slice(k * width, (k + 1) * width)
    return full[tuple(idx)]


SHARD_AXIS = dict(mlp_w1=2, mlp_w2=1, fox_w_in=2, fox_w_o=1, ssm_w_in=2, ssm_w_out=1, ssm_conv_w=2,
                  ssm_conv_b=1, ssm_norm_w=1, ada_w=2)


def kernel(x, c, ada_w, ada_b, ln_mix_g, ln_mix_b, ln_mlp_g, ln_mlp_b, mlp_w1, mlp_w2, fox_w_in, fox_b_f, fox_w_o, ssm_w_in, ssm_conv_w, ssm_conv_b, ssm_dt_bias, ssm_a_log, ssm_d, ssm_norm_w, ssm_w_out, loss_target, m_ada_w, m_ada_b, m_ln_mix_g, m_ln_mix_b, m_ln_mlp_g, m_ln_mlp_b, m_mlp_w1, m_mlp_w2, m_fox_w_in, m_fox_b_f, m_fox_w_o, m_ssm_w_in, m_ssm_conv_w, m_ssm_conv_b, m_ssm_dt_bias, m_ssm_a_log, m_ssm_d, m_ssm_norm_w, m_ssm_w_out, v_ada_w, v_ada_b, v_ln_mix_g, v_ln_mix_b, v_ln_mlp_g, v_ln_mlp_b, v_mlp_w1, v_mlp_w2, v_fox_w_in, v_fox_b_f, v_fox_w_o, v_ssm_w_in, v_ssm_conv_w, v_ssm_conv_b, v_ssm_dt_bias, v_ssm_a_log, v_ssm_d, v_ssm_norm_w, v_ssm_w_out):
    names = ("ada_w", "ada_b", "ln_mix_g", "ln_mix_b", "ln_mlp_g", "ln_mlp_b", "mlp_w1", "mlp_w2", "fox_w_in",
             "fox_b_f", "fox_w_o", "ssm_w_in", "ssm_conv_w", "ssm_conv_b", "ssm_dt_bias", "ssm_a_log", "ssm_d",
             "ssm_norm_w", "ssm_w_out")
    weights = dict(zip(names, (ada_w, ada_b, ln_mix_g, ln_mix_b, ln_mlp_g, ln_mlp_b, mlp_w1, mlp_w2, fox_w_in,
                               fox_b_f, fox_w_o, ssm_w_in, ssm_conv_w, ssm_conv_b, ssm_dt_bias, ssm_a_log, ssm_d,
                               ssm_norm_w, ssm_w_out)))
    m_in = dict(zip(names, (m_ada_w, m_ada_b, m_ln_mix_g, m_ln_mix_b, m_ln_mlp_g, m_ln_mlp_b, m_mlp_w1, m_mlp_w2,
                            m_fox_w_in, m_fox_b_f, m_fox_w_o, m_ssm_w_in, m_ssm_conv_w, m_ssm_conv_b, m_ssm_dt_bias,
                            m_ssm_a_log, m_ssm_d, m_ssm_norm_w, m_ssm_w_out)))
    v_in = dict(zip(names, (v_ada_w, v_ada_b, v_ln_mix_g, v_ln_mix_b, v_ln_mlp_g, v_ln_mlp_b, v_mlp_w1, v_mlp_w2,
                            v_fox_w_in, v_fox_b_f, v_fox_w_o, v_ssm_w_in, v_ssm_conv_w, v_ssm_conv_b, v_ssm_dt_bias,
                            v_ssm_a_log, v_ssm_d, v_ssm_norm_w, v_ssm_w_out)))

    mx, my, mc = _my_pos()
    chip = 2 * mx + my
    me = 4 * mx + 2 * my + mc
    x0 = x[0]
    target = loss_target[0]
    s, d = x0.shape
    n_qkv = 3 * FOX_HEADS * HEAD_DIM

    big_shapes = [weights[n].shape for n in BIG]
    packed = _pack2d([weights[n] for n in BIG], 32, BF16)
    gathered = _gather_chips(packed.reshape(2, packed.shape[0] // 2, PACK_COLS), "gather_weights")
    gathered = gathered.reshape(4, packed.shape[0], PACK_COLS)
    per_chip = [_unpack2d(jnp.where(chip == k, packed, gathered[k]), big_shapes) for k in range(4)]
    full = {n: jnp.concatenate([per_chip[k][i] for k in range(4)], axis=SHARD_AXIS[n]) for i, n in enumerate(BIG)}

    small_shapes = [weights[n].shape for n in SMALL_SHARDED]
    small_packed = _pack_rows([weights[n] for n in SMALL_SHARDED] + [c], 8, F32).reshape(-1, LANES)
    small_all = _allgather8(small_packed, "gather_small")
    small_chip = [_unpack(small_all[2 * k].reshape(-1), small_shapes) for k in range(4)]
    small_full = {n: jnp.concatenate([small_chip[k][i] for k in range(4)], axis=SHARD_AXIS[n])
                  for i, n in enumerate(SMALL_SHARDED)}
    n_small = sum(weights[n].size for n in SMALL_SHARDED)
    c_all = small_all.reshape(8, -1)[:, n_small:n_small + d]

    cols = ada_w.shape[2]
    ada_b_cols = lax.dynamic_slice_in_dim(ada_b, chip * cols, cols, axis=1)[:, None, :]
    c16 = jnp.pad(c_all, ((0, 8), (0, 0)))
    mod_part = _ada_forward(c16, ada_w, ada_b_cols, "ada_fwd")[:, :8, :]
    mod_all = _allgather8(mod_part.reshape(-1, LANES), "gather_mod").reshape(8, DEPTH, 8, cols)
    mod_mine = jnp.stack([lax.dynamic_index_in_dim(mod_all[2 * k], me, axis=1, keepdims=False) for k in range(4)], axis=1)
    mods = [jnp.pad(mod_mine[i].reshape(6, d), ((0, 2), (0, 0))) for i in range(DEPTH)]

    w = dict(
        fox_qkv=full["fox_w_in"][0][:, :n_qkv],
        fox_f=jnp.pad(full["fox_w_in"][0][:, n_qkv:], ((0, 0), (0, LANES - FOX_HEADS))),
        fox_o=full["fox_w_o"][0],
        fox_bf=jnp.pad(fox_b_f, ((0, 0), (0, LANES - FOX_HEADS))),
        head_ind=jnp.asarray(np.arange(d)[:, None] // HEAD_DIM == np.arange(LANES)[None, :], BF16),
    )
    w.update(_ssm_weights(full["ssm_w_in"][0], small_full["ssm_conv_w"][0], small_full["ssm_conv_b"], ssm_dt_bias,
                          ssm_a_log, ssm_d, small_full["ssm_norm_w"], full["ssm_w_out"][0]))
    mixers = ((_fox_forward, _fox_backward), (_ssd_forward, _ssd_backward))

    saved = []
    xin = x0
    u = _modulate_in(x0, mods[0], "l0_mod_in")
    for i in range(DEPTH):
        tag = "l%d" % i
        y, sv = mixers[i % 2][0](u, w)
        r, x1, u2 = _res_ln_mod(xin, y, mods[i], ln_mix_g[i:i + 1], ln_mix_b[i:i + 1], tag + "_res_ln1")
        y2, hr, a = _mlp_forward(u2, full["mlp_w1"][i], full["mlp_w2"][i], tag)
        if i + 1 < DEPTH:
            r2, xin, u = _res_ln_next(x1, y2, mods[i], ln_mlp_g[i:i + 1], ln_mlp_b[i:i + 1], mods[i + 1],
                                      tag + "_res_ln2")
        else:
            r2 = None
        saved.append(dict(y=y, r=r, u2=u2, hr=hr, a=a, y2=y2, r2=r2, x1=x1, mix=sv))

    grads = {}
    dmod_parts = [dict() for _ in range(DEPTH)]
    ln_grads = {n: [None] * DEPTH for n in ("ln_mix_g", "ln_mix_b", "ln_mlp_g", "ln_mlp_b")}
    g_w1, g_w2 = [None] * DEPTH, [None] * DEPTH
    du = dres0 = None
    for i in reversed(range(DEPTH)):
        tag = "l%d" % i
        sv = saved[i]
        if i + 1 == DEPTH:
            dy2, dres, sq, dg2, db2, dgm = _loss_ln2_bwd(sv["x1"], sv["y2"], target, mods[i], ln_mlp_g[i:i + 1],
                                                         ln_mlp_b[i:i + 1], "loss_ln2_bwd")
            loss = lax.psum(0.5 * jnp.sum(sq) / d, ("x", "y", "c"))
        else:
            dy2, dres, dsca, dsha, dg2, db2, dgm = _mod_in_ln2_bwd(du, dres0, sv["r2"], sv["y2"], mods[i],
                                                                   ln_mlp_g[i:i + 1], ln_mlp_b[i:i + 1], mods[i + 1],
                                                                   tag + "_ln2_bwd")
            dmod_parts[i + 1].update(sc_a=dsca, sh_a=dsha)
        du2, g_w1[i], g_w2[i] = _mlp_backward(dy2, sv["u2"], sv["hr"], sv["a"], full["mlp_w1"][i], full["mlp_w2"][i], tag)
        dy, dres0, dscm, dshm, dg1, db1, dga = _ln1_bwd(du2, dres, sv["r"], sv["y"], mods[i], ln_mix_g[i:i + 1],
                                                        ln_mix_b[i:i + 1], tag + "_ln1_bwd")
        du, mg = mixers[i % 2][1](dy, sv["mix"], w)
        grads.update(mg)
        dmod_parts[i].update(g_a=dga, sh_m=dshm, sc_m=dscm, g_m=dgm)
        ln_grads["ln_mix_g"][i], ln_grads["ln_mix_b"][i] = dg1, db1
        ln_grads["ln_mlp_g"][i], ln_grads["ln_mlp_b"][i] = dg2, db2
    dx, dsca, dsha = _mod_in_bwd(du, dres0, x0, mods[0], "l0_mod_in_bwd")
    dmod_parts[0].update(sc_a=dsca, sh_a=dsha)
    dmods = [jnp.concatenate([p["sh_a"], p["sc_a"], p["g_a"], p["sh_m"], p["sc_m"], p["g_m"]], axis=1)
             for p in dmod_parts]
    grad_x = dx[None]
    grads["mlp_w1"] = jnp.stack(g_w1)
    grads["mlp_w2"] = jnp.stack(g_w2)
    for n in ("fox_w_in", "fox_w_o", "ssm_w_in", "ssm_w_out", "ssm_conv_w"):
        grads[n] = grads[n][None]

    small_names = ("ln_mix_g", "ln_mix_b", "ln_mlp_g", "ln_mlp_b", "fox_b_f", "ssm_dt_bias", "ssm_a_log", "ssm_d")
    small_parts = list(dmods)
    for n in small_names[:4]:
        small_parts.append(jnp.concatenate(ln_grads[n], axis=0))
    for n in small_names[4:]:
        small_parts.append(jnp.pad(grads[n], ((0, 0), (0, LANES - grads[n].shape[1]))))
    small_vec = _pack_rows(small_parts, 1, F32).reshape(-1, LANES)
    small_vec = jnp.pad(small_vec, ((0, -small_vec.shape[0] % 8), (0, 0)))
    small_g_all = _allgather8(small_vec, "gather_small_grads")
    small_sum = _sum_slots(small_g_all, "sum_small_grads").reshape(-1)
    dmod_sum = small_sum[:DEPTH * 6 * d].reshape(DEPTH, 6 * d)
    off = DEPTH * 6 * d
    final = {"ada_b": dmod_sum}
    for n in small_names[:4]:
        final[n] = small_sum[off:off + DEPTH * d].reshape(DEPTH, d)
        off += DEPTH * d
    for n in small_names[4:]:
        width = weights[n].shape[1]
        final[n] = small_sum[off:off + width].reshape(1, width)
        off += LANES

    dmod_all = small_g_all.reshape(8, -1)[:, :DEPTH * 6 * d].reshape(8, DEPTH, 6 * d)
    dmod_cols = lax.dynamic_slice_in_dim(dmod_all, chip * cols, cols, axis=2).transpose(1, 0, 2)
    final["ada_w"] = _ada_backward(c_all.T, dmod_cols, "ada_bwd")

    sharded = BIG + SMALL_SHARDED
    shard_shapes = [weights[n].shape for n in sharded]
    per_target = []
    for k in range(4):
        parts = [_chip_slice(grads[n], SHARD_AXIS[n], k, weights[n].shape[SHARD_AXIS[n]]) for n in sharded]
        per_target.append(_pack2d(parts, 128, F32))
    g_all = jnp.stack(per_target)
    rows = g_all.shape[1]
    g4 = g_all.reshape(4, 2, rows // 2, PACK_COLS)
    recv = _pair_exchange(g4, "rs_pair_exchange")
    part, part16 = _pair_add(g4, recv, jnp.reshape(mc, (1,)).astype(jnp.int32), "rs_pair_add")
    slots = _chip_exchange(part16, "rs_chip_exchange")
    half = _chip_sum(part, slots, jnp.reshape(chip, (1,)).astype(jnp.int32), "rs_chip_sum")
    other = _pair_share(half, "rs_pair_share")
    both = jnp.concatenate([jnp.where(mc == 0, half, other), jnp.where(mc == 0, other, half)], axis=0)
    for n, g_shard in zip(sharded, _unpack2d(both, shard_shapes)):
        final[n] = g_shard

    outs_g, outs_d, outs_m, outs_v = [], [], [], []
    for n in names:
        wv = weights[n]
        two_d = (-1, wv.shape[-1])
        delta, mn, vn = _adamw(wv.reshape(two_d), final[n].reshape(two_d), m_in[n].reshape(two_d),
                               v_in[n].reshape(two_d), "adamw_" + n)
        outs_g.append(final[n].reshape(wv.shape))
        outs_d.append(delta.reshape(wv.shape))
        outs_m.append(mn.reshape(wv.shape))
        outs_v.append(vn.reshape(wv.shape))
    return (loss, grad_x, *outs_g, *outs_d, *outs_m, *outs_v)
```

```python
import functools

import jax
import jax.numpy as jnp
import numpy as np
from jax import lax
from jax.experimental import pallas as pl
from jax.experimental.pallas import tpu as pltpu

F32, BF16 = jnp.float32, jnp.bfloat16
MESH = pl.DeviceIdType.MESH
HBM_SPEC = pl.BlockSpec(memory_space=pltpu.HBM)

VMEM_LIMIT_BYTES = 52 * 2**20
LANES = 128

FOX_HEADS, HEAD_DIM = 16, 64
SSM_HEADS, SSM_GROUPS, SSM_STATE, SSM_CHUNK, SSM_CONV = 32, 8, 128, 128, 4
SSM_GROUP_WIDTH = 256
LN_EPS, RMS_EPS = 1e-5, 1e-5
DEPTH = 2
ALPHA = (2.0 * DEPTH) ** 0.25
ADAM_LR, ADAM_B1, ADAM_B2, ADAM_EPS, ADAM_WD, ADAM_STEP = 0.001, 0.9, 0.999, 1e-08, 0.01, 10

ATT_TILE = 512
ROW_TILE = 512
SCAN_TILE = 512
CONV_CHUNK = 64
SSD_GROUPS_PER_STEP = 4
MM_TM, MM_TN, MM_TK = 1024, 1024, 1024

NT_DIMS = (((1,), (1,)), ((), ()))
TN_DIMS = (((0,), (0,)), ((), ()))
NN_DIMS = (((1,), (0,)), ((), ()))


def _pcall(body, *, name, out_shape, grid=(), in_specs=None, out_specs=None, scratch=(), sem=None, prefetch=0):
    params = dict(vmem_limit_bytes=VMEM_LIMIT_BYTES)
    if sem is not None:
        params["dimension_semantics"] = sem
    if prefetch:
        grid_spec = pltpu.PrefetchScalarGridSpec(num_scalar_prefetch=prefetch, grid=grid, in_specs=in_specs,
                                                 out_specs=out_specs, scratch_shapes=scratch)
        return pl.pallas_call(body, out_shape=out_shape, grid_spec=grid_spec, name=name,
                              compiler_params=pltpu.CompilerParams(**params))
    kwargs = {}
    if in_specs is not None:
        kwargs["in_specs"] = in_specs
    if out_specs is not None:
        kwargs["out_specs"] = out_specs
    return pl.pallas_call(body, out_shape=out_shape, grid=grid, scratch_shapes=scratch, name=name,
                          compiler_params=pltpu.CompilerParams(**params), **kwargs)


def _sds(shape, dtype):
    return jax.ShapeDtypeStruct(tuple(shape), dtype)


def _dot(a, b, dims=NN_DIMS):
    return lax.dot_general(a, b, dims, preferred_element_type=F32)


def _sigmoid(x):
    return 1.0 / (1.0 + jnp.exp(-x))


def _silu(x):
    return x * _sigmoid(x)


def _dsilu(x):
    s = _sigmoid(x)
    return s * (1.0 + x * (1.0 - s))


def _dot_split(x, m16, dims=NN_DIMS, passes=3):
    hi = x.astype(BF16)
    r1 = x - hi.astype(F32)
    mid = r1.astype(BF16)
    out = _dot(hi, m16, dims) + _dot(mid, m16, dims)
    if passes == 3:
        lo = (r1 - mid.astype(F32)).astype(BF16)
        out = out + _dot(lo, m16, dims)
    return out


def _mm(a, b, dims, outs, *, name, tm=MM_TM, tn=MM_TN, tk=MM_TK, epi=None, extra=()):
    if dims == "nn":
        (m, k), (k2, n) = a.shape, b.shape
    elif dims == "nt":
        (m, k), (n, k2) = a.shape, b.shape
    else:
        (k, m), (k2, n) = a.shape, b.shape
    assert k == k2, (a.shape, b.shape, dims)
    tm, tn, tk = min(tm, m), min(tn, n), min(tk, k)
    assert m % tm == 0 and n % tn == 0 and k % tk == 0, (m, n, k, tm, tn, tk)
    nk = k // tk
    dn = {"nn": NN_DIMS, "nt": NT_DIMS, "tn": TN_DIMS}[dims]
    n_extra, n_out = len(extra), len(outs)
    if epi is None:
        epi = lambda acc: (acc,) * n_out

    def body(a_ref, b_ref, *rest):
        extra_refs, out_refs, acc_ref = rest[:n_extra], rest[n_extra:n_extra + n_out], rest[-1]
        kk = pl.program_id(2)

        @pl.when(kk == 0)
        def _():
            acc_ref[...] = jnp.zeros_like(acc_ref)

        acc_ref[...] += _dot(a_ref[...].astype(BF16), b_ref[...].astype(BF16), dn)

        @pl.when(kk == nk - 1)
        def _():
            res = epi(acc_ref[...], *[e[...] for e in extra_refs])
            for o, r in zip(out_refs, res):
                o[...] = r.astype(o.dtype)

    if dims == "tn":
        a_spec = pl.BlockSpec((tk, tm), lambda i, j, kk: (kk, i))
    else:
        a_spec = pl.BlockSpec((tm, tk), lambda i, j, kk: (i, kk))
    if dims == "nt":
        b_spec = pl.BlockSpec((tn, tk), lambda i, j, kk: (j, kk))
    else:
        b_spec = pl.BlockSpec((tk, tn), lambda i, j, kk: (kk, j))
    o_spec = pl.BlockSpec((tm, tn), lambda i, j, kk: (i, j))
    res = _pcall(body, name=name, grid=(m // tm, n // tn, nk),
                 in_specs=[a_spec, b_spec] + [o_spec] * n_extra,
                 out_specs=[o_spec] * n_out,
                 out_shape=[_sds((m, n), d) for d in outs],
                 scratch=[pltpu.VMEM((tm, tn), F32)],
                 sem=("parallel", "parallel", "arbitrary"))(a, b, *extra)
    return res[0] if n_out == 1 else res


def _k_blocks(a, tk=None):
    tk = tk or MM_TK
    return [(a, kb) for kb in range(a.shape[1] // tk)]


def _mm_nt_blocks(a_blocks, b, start, *, name, tm=MM_TM, tk=None):
    tk = tk or MM_TK
    tm = min(tm, a_blocks[0][0].shape[0])
    m, n, p_n = a_blocks[0][0].shape[0], b.shape[0], len(a_blocks)
    assert b.shape[1] == p_n * tk and m % tm == 0

    def body(*refs):
        a_refs = refs[:p_n]
        b_ref, s_ref, o_ref, acc_ref = refs[p_n:]
        kk = pl.program_id(1)

        @pl.when(kk == 0)
        def _():
            acc_ref[...] = s_ref[...]
        for p in range(p_n):
            @pl.when(kk == p)
            def _(p=p):
                acc_ref[...] += _dot(a_refs[p][...].astype(BF16), b_ref[...].astype(BF16), NT_DIMS)

        @pl.when(kk == p_n - 1)
        def _():
            o_ref[...] = acc_ref[...]

    in_specs = [pl.BlockSpec((tm, tk), functools.partial(lambda kb, i, kk: (i, kb), kb)) for _, kb in a_blocks]
    in_specs += [pl.BlockSpec((n, tk), lambda i, kk: (0, kk)), pl.BlockSpec((tm, n), lambda i, kk: (i, 0))]
    return _pcall(body, name=name, grid=(m // tm, p_n), in_specs=in_specs,
                  out_specs=pl.BlockSpec((tm, n), lambda i, kk: (i, 0)), out_shape=_sds((m, n), F32),
                  scratch=[pltpu.VMEM((tm, n), F32)], sem=("parallel", "arbitrary"))(*[a for a, _ in a_blocks], b, start)


def _rowwise(fn, rows, consts, row_outs, acc_outs, *, name, tm=ROW_TILE):
    s = rows[0].shape[0]
    tm = min(tm, s)
    assert s % tm == 0
    n_in, n_o = len(rows) + len(consts), len(row_outs)

    def body(*refs):
        ins, outs = refs[:n_in], refs[n_in:]
        res = fn(*[r[...] for r in ins])
        if not isinstance(res, (tuple, list)):
            res = (res,)
        for o, val in zip(outs[:n_o], res[:n_o]):
            o[...] = val.astype(o.dtype)
        if acc_outs:
            @pl.when(pl.program_id(0) == 0)
            def _():
                for o in outs[n_o:]:
                    o[...] = jnp.zeros_like(o)
            for o, val in zip(outs[n_o:], res[n_o:]):
                o[...] += val

    in_specs = [pl.BlockSpec((tm, r.shape[1]), lambda i: (i, 0)) for r in rows]
    in_specs += [pl.BlockSpec(c.shape, functools.partial(lambda nd, i: (0,) * nd, c.ndim)) for c in consts]
    out_specs = [pl.BlockSpec((tm, c), lambda i: (i, 0)) for c, _ in row_outs]
    out_specs += [pl.BlockSpec(tuple(sh), lambda i: (0, 0)) for sh in acc_outs]
    out_shape = [_sds((s, c), d) for c, d in row_outs] + [_sds(sh, F32) for sh in acc_outs]
    res = _pcall(body, name=name, grid=(s // tm,), in_specs=in_specs, out_specs=out_specs,
                 out_shape=out_shape, sem=("arbitrary",))(*rows, *consts)
    return res


def _colsum(x):
    return jnp.sum(x, axis=0, keepdims=True)


def _ln_stats(r):
    mu = jnp.mean(r, axis=-1, keepdims=True)
    xc = r - mu
    var = jnp.mean(xc * xc, axis=-1, keepdims=True)
    rstd = lax.rsqrt(var + LN_EPS)
    return xc * rstd, rstd


def _ln_bwd(dy, xhat, rstd, gamma):
    dyg = dy * gamma
    m1 = jnp.mean(dyg, axis=-1, keepdims=True)
    m2 = jnp.mean(dyg * xhat, axis=-1, keepdims=True)
    return rstd * (dyg - m1 - xhat * m2)


def _modulate_in(x, mod, name):
    def fn(xv, m):
        return (xv * (1.0 + m[1:2]) + m[0:1],)
    return _rowwise(fn, [x], [mod], [(x.shape[1], BF16)], [], name=name)[0]


def _res_ln_mod(x, y, mod, g, b, name):
    d = x.shape[1]

    def fn(xv, yv, m, gv, bv):
        r = ALPHA * xv + (1.0 + m[2:3]) * yv
        xhat, _ = _ln_stats(r)
        x1 = xhat * gv + bv
        u2 = x1 * (1.0 + m[4:5]) + m[3:4]
        return r, x1, u2
    return _rowwise(fn, [x, y], [mod, g, b], [(d, F32), (d, F32), (d, BF16)], [], name=name)


def _res_ln_next(x, y, mod, g, b, mod_next, name):
    d = x.shape[1]

    def fn(xv, yv, m, gv, bv, mn):
        r = ALPHA * xv + (1.0 + m[5:6]) * yv
        xhat, _ = _ln_stats(r)
        out = xhat * gv + bv
        return r, out, out * (1.0 + mn[1:2]) + mn[0:1]
    return _rowwise(fn, [x, y], [mod, g, b, mod_next], [(d, F32), (d, F32), (d, BF16)], [], name=name)


def _loss_ln2_bwd(x1, y2, target, mod, g, b, name):
    d = x1.shape[1]

    def fn(xv, yv, tv, m, gv, bv):
        rv = ALPHA * xv + (1.0 + m[5:6]) * yv
        xhat, rstd = _ln_stats(rv)
        e = xhat * gv + bv - tv
        dxv = e * (1.0 / d)
        dr = _ln_bwd(dxv, xhat, rstd, gv)
        return (dr * (1.0 + m[5:6]), ALPHA * dr,
                _colsum(e * e), _colsum(dxv * xhat), _colsum(dxv), _colsum(dr * yv))
    return _rowwise(fn, [x1, y2, target], [mod, g, b], [(d, BF16), (d, F32)], [(1, d)] * 4, name=name)


def _mod_in_ln2_bwd(du, dres, r2, y2, mod, g, b, mod_next, name):
    d = du.shape[1]

    def fn(duv, drv, rv, yv, m, gv, bv, mn):
        xhat, rstd = _ln_stats(rv)
        xout = xhat * gv + bv
        dxv = duv * (1.0 + mn[1:2]) + drv
        dr = _ln_bwd(dxv, xhat, rstd, gv)
        return (dr * (1.0 + m[5:6]), ALPHA * dr,
                _colsum(duv * xout), _colsum(duv), _colsum(dxv * xhat), _colsum(dxv), _colsum(dr * yv))
    return _rowwise(fn, [du, dres, r2, y2], [mod, g, b, mod_next], [(d, BF16), (d, F32)], [(1, d)] * 5, name=name)


def _ln1_bwd(du2, dres, r, y, mod, g, b, name):
    d = du2.shape[1]

    def fn(duv, drv, rv, yv, m, gv, bv):
        xhat, rstd = _ln_stats(rv)
        x1 = xhat * gv + bv
        dx1 = duv * (1.0 + m[4:5]) + drv
        dr = _ln_bwd(dx1, xhat, rstd, gv)
        return (dr * (1.0 + m[2:3]), ALPHA * dr,
                _colsum(duv * x1), _colsum(duv), _colsum(dx1 * xhat), _colsum(dx1), _colsum(dr * yv))
    return _rowwise(fn, [du2, dres, r, y], [mod, g, b], [(d, BF16), (d, F32)], [(1, d)] * 5, name=name)


def _mod_in_bwd(du, dres, x, mod, name):
    d = du.shape[1]

    def fn(duv, drv, xv, m):
        return duv * (1.0 + m[1:2]) + drv, _colsum(duv * xv), _colsum(duv)
    return _rowwise(fn, [du, dres, x], [mod], [(d, F32)], [(1, d)] * 2, name=name)


def _fox_gate(fraw, b_pad, name):
    s = fraw.shape[0]
    tb = min(SCAN_TILE, s)

    def body(f_ref, b_ref, cum_ref, rows_ref, carry):
        @pl.when(pl.program_id(0) == 0)
        def _():
            carry[...] = jnp.zeros_like(carry)
        z = f_ref[...] + b_ref[...]
        lf = jnp.minimum(z, 0.0) - jnp.log(1.0 + jnp.exp(-jnp.abs(z)))
        lane = lax.broadcasted_iota(jnp.int32, (tb, LANES), 1)
        row = lax.broadcasted_iota(jnp.int32, (tb, LANES), 0)
        c = jnp.where(lane < FOX_HEADS, lf, 0.0)
        sh = 1
        while sh < tb:
            c = c + jnp.where(row >= sh, pltpu.roll(c, sh, 0), 0.0)
            sh *= 2
        c = c + carry[0:1, :]
        cum_ref[...] = c
        rows_ref[...] = c.T[0:FOX_HEADS, :]
        carry[0:1, :] = c[tb - 1:tb, :]

    return _pcall(body, name=name, grid=(s // tb,),
                  in_specs=[pl.BlockSpec((tb, LANES), lambda i: (i, 0)), pl.BlockSpec((1, LANES), lambda i: (0, 0))],
                  out_specs=[pl.BlockSpec((tb, LANES), lambda i: (i, 0)), pl.BlockSpec((FOX_HEADS, tb), lambda i: (0, i))],
                  out_shape=[_sds((s, LANES), F32), _sds((FOX_HEADS, s), F32)], scratch=[pltpu.VMEM((8, LANES), F32)],
                  sem=("arbitrary",))(fraw, b_pad)


def _fox_gate_bwd(drow, dcol, fraw, b_pad, name):
    s = fraw.shape[0]
    tb = min(SCAN_TILE, s)
    n = s // tb

    def body(dr_ref, dc_ref, f_ref, b_ref, df_ref, db_ref, carry):
        @pl.when(pl.program_id(0) == 0)
        def _():
            carry[...] = jnp.zeros_like(carry)
            db_ref[...] = jnp.zeros_like(db_ref)
        row = lax.broadcasted_iota(jnp.int32, (tb, LANES), 0)
        c = dr_ref[...] + dc_ref[...]
        sh = 1
        while sh < tb:
            c = c + jnp.where(row + sh < tb, pltpu.roll(c, tb - sh, 0), 0.0)
            sh *= 2
        c = c + carry[0:1, :]
        carry[0:1, :] = c[0:1, :]
        z = f_ref[...] + b_ref[...]
        df = c * (1.0 / (1.0 + jnp.exp(z)))
        df_ref[...] = df.astype(df_ref.dtype)
        db_ref[...] += _colsum(df)

    rev = lambda i: (n - 1 - i, 0)
    return _pcall(body, name=name, grid=(n,),
                  in_specs=[pl.BlockSpec((tb, LANES), rev)] * 3 + [pl.BlockSpec((1, LANES), lambda i: (0, 0))],
                  out_specs=[pl.BlockSpec((tb, LANES), rev), pl.BlockSpec((1, LANES), lambda i: (0, 0))],
                  out_shape=[_sds((s, LANES), BF16), _sds((1, LANES), F32)],
                  scratch=[pltpu.VMEM((8, LANES), F32)], sem=("arbitrary",))(drow, dcol, fraw, b_pad)


def _head_pair_masks(t):
    lane = lax.broadcasted_iota(jnp.int32, (t, LANES), 1)
    return lane < HEAD_DIM


def _lane_blocks(x):
    return [x[:, c * LANES:(c + 1) * LANES] for c in range(x.shape[1] // LANES)]


def _sum_list(xs):
    acc = xs[0]
    for x in xs[1:]:
        acc = acc + x
    return acc


def _causal(t, transposed=False):
    ri = lax.broadcasted_iota(jnp.int32, (t, t), 0)
    ci = lax.broadcasted_iota(jnp.int32, (t, t), 1)
    return ci >= ri if transposed else ri >= ci


def _span_mask(r0, r1, c0, c1, transposed=False):
    ri = lax.broadcasted_iota(jnp.int32, (r1 - r0, c1 - c0), 0) + r0
    ci = lax.broadcasted_iota(jnp.int32, (r1 - r0, c1 - c0), 1) + c0
    return ci >= ri if transposed else ri >= ci


def _full_spans(t):
    return ((0, t, 0, t, False),)


def _diagonal_spans(t, transposed=False):
    h = t // 2
    if h % LANES:
        return ((0, t, 0, t, True),)
    if transposed:
        return ((0, h, 0, t, True), (h, t, h, t, True))
    return ((0, h, 0, h, True), (h, t, 0, t, True))


def _flash_fwd(qkv, ck_rows, kb_start, name):
    s = qkv.shape[0]
    t = min(ATT_TILE, s)
    nq = s // t
    scale = HEAD_DIM ** -0.5
    hp_blocks = FOX_HEADS // 2

    def body(ks_ref, q_ref, k_ref, v_ref, ck_ref, o_ref, lse_ref, lse_rows_ref, acc_ref, m_ref, l_ref):
        hp, qb = pl.program_id(0), pl.program_id(1)
        q2 = q_ref[...] * scale
        first = _head_pair_masks(t)
        zero = jnp.zeros_like(q2)
        qs = (jnp.where(first, q2, zero), jnp.where(first, zero, q2))
        m_ref[...] = jnp.full_like(m_ref, -jnp.inf)
        l_ref[...] = jnp.zeros_like(l_ref)
        acc_ref[...] = jnp.zeros_like(acc_ref)

        def tile(kb, spans):
            off = pl.multiple_of(kb * t, t)
            k2 = k_ref[pl.ds(off, t), :]
            v2 = v_ref[pl.ds(off, t), :]
            ck = ck_ref[kb]
            for r0, r1, c0, c1, masked in spans:
                kk, vv, fr = k2[c0:c1], v2[c0:c1], first[r0:r1]
                pvs, als = [], []
                for j in range(2):
                    sc = _dot(qs[j][r0:r1], kk, NT_DIMS) - ck[j:j + 1, c0:c1]
                    if masked:
                        sc = jnp.where(_span_mask(r0, r1, c0, c1), sc, -jnp.inf)
                    blocks = _lane_blocks(sc)
                    mx = blocks[0]
                    for b in blocks[1:]:
                        mx = jnp.maximum(mx, b)
                    m_old = m_ref[j, r0:r1]
                    m_new = jnp.maximum(m_old, jnp.max(mx, axis=1, keepdims=True))
                    ps = [jnp.exp(b - m_new) for b in blocks]
                    a = jnp.exp(m_old - m_new)
                    l_ref[j, r0:r1] = a * l_ref[j, r0:r1] + _sum_list(ps)
                    m_ref[j, r0:r1] = m_new
                    pvs.append(_dot(jnp.concatenate(ps, axis=1).astype(BF16), vv))
                    als.append(a)
                acc_ref[r0:r1] = jnp.where(fr, als[0], als[1]) * acc_ref[r0:r1] + jnp.where(fr, pvs[0], pvs[1])

        def step(kb, carry):
            tile(kb, _full_spans(t))
            return carry

        lax.fori_loop(ks_ref[hp, qb], qb, step, 0)
        tile(qb, ((0, t, 0, t, True),))
        l0 = jnp.sum(l_ref[0], axis=1, keepdims=True)
        l1 = jnp.sum(l_ref[1], axis=1, keepdims=True)
        o_ref[...] = acc_ref[...] / jnp.where(first, l0, l1)
        for j, lj in enumerate((l0, l1)):
            lse = m_ref[j] + jnp.log(jnp.broadcast_to(lj, (t, LANES)))
            lse_ref[:, j:j + 1] = lse[:, 0:1]
            lse_rows_ref[j:j + 1, :] = lse.T[0:1, :]

    return _pcall(
        body, name=name, grid=(hp_blocks, nq), prefetch=1,
        in_specs=[pl.BlockSpec((t, LANES), lambda h, i, ks: (i, h)),
                  pl.BlockSpec((s, LANES), lambda h, i, ks: (0, hp_blocks + h)),
                  pl.BlockSpec((s, LANES), lambda h, i, ks: (0, 2 * hp_blocks + h)),
                  pl.BlockSpec((None, nq, 2, t), lambda h, i, ks: (h, 0, 0, 0))],
        out_specs=[pl.BlockSpec((t, LANES), lambda h, i, ks: (i, h)),
                   pl.BlockSpec((None, t, 2), lambda h, i, ks: (h, i, 0)),
                   pl.BlockSpec((None, None, 2, t), lambda h, i, ks: (h, i, 0, 0))],
        out_shape=[_sds((s, hp_blocks * LANES), F32), _sds((hp_blocks, s, 2), F32), _sds((hp_blocks, nq, 2, t), F32)],
        scratch=[pltpu.VMEM((t, LANES), F32), pltpu.VMEM((2, t, LANES), F32), pltpu.VMEM((2, t, LANES), F32)],
        sem=("parallel", "arbitrary"))(kb_start, qkv, qkv, qkv, ck_rows)


def _flash_dq(qkv, do16, ck_rows, lse_c, delta, kb_start, name):
    s = qkv.shape[0]
    t = min(ATT_TILE, s)
    nq = s // t
    scale = HEAD_DIM ** -0.5
    hp_blocks = FOX_HEADS // 2

    def body(ks_ref, q_ref, do_ref, k_ref, v_ref, ck_ref, lse_ref, dl_ref, dq_ref, drow_ref, acc_ref, row_acc):
        hp, qb = pl.program_id(0), pl.program_id(1)
        q2, do2 = q_ref[...] * scale, do_ref[...]
        first = _head_pair_masks(t)
        zero = jnp.zeros_like(q2)
        qs = (jnp.where(first, q2, zero), jnp.where(first, zero, q2))
        dos = (jnp.where(first, do2, zero), jnp.where(first, zero, do2))
        lse, dl = lse_ref[...], dl_ref[...]
        lane = lax.broadcasted_iota(jnp.int32, (t, LANES), 1)
        lse_b = [jnp.broadcast_to(lse[:, j:j + 1], (t, LANES)) for j in range(2)]
        dl_b = [jnp.broadcast_to(jnp.sum(jnp.where(lane == 2 * hp + j, dl, 0.0), axis=1, keepdims=True), (t, LANES))
                for j in range(2)]
        acc_ref[...] = jnp.zeros_like(acc_ref)
        row_acc[...] = jnp.zeros_like(row_acc)

        def tile(kb, spans):
            off = pl.multiple_of(kb * t, t)
            k2 = k_ref[pl.ds(off, t), :]
            v2 = v_ref[pl.ds(off, t), :]
            ck = ck_ref[kb]
            for r0, r1, c0, c1, masked in spans:
                kk, vv = k2[c0:c1], v2[c0:c1]
                dqs = []
                for j in range(2):
                    sc = _dot(qs[j][r0:r1], kk, NT_DIMS) - ck[j:j + 1, c0:c1]
                    if masked:
                        sc = jnp.where(_span_mask(r0, r1, c0, c1), sc, -jnp.inf)
                    dp = _dot(dos[j][r0:r1], vv, NT_DIMS)
                    lb, db_ = lse_b[j][r0:r1], dl_b[j][r0:r1]
                    dsb = [jnp.exp(x - lb) * (d - db_) for x, d in zip(_lane_blocks(sc), _lane_blocks(dp))]
                    row_acc[j, r0:r1] += _sum_list(dsb)
                    dqs.append(_dot(jnp.concatenate(dsb, axis=1).astype(BF16), kk))
                acc_ref[r0:r1] += jnp.where(first[r0:r1], dqs[0], dqs[1])

        def step(kb, carry):
            tile(kb, _full_spans(t))
            return carry

        lax.fori_loop(ks_ref[hp, qb], qb, step, 0)
        tile(qb, _diagonal_spans(t))
        dq_ref[...] = (acc_ref[...] * scale).astype(dq_ref.dtype)
        for j in range(2):
            drow_ref[j:j + 1, :] = jnp.sum(row_acc[j].T, axis=0, keepdims=True)

    return _pcall(
        body, name=name, grid=(hp_blocks, nq), prefetch=1,
        in_specs=[pl.BlockSpec((t, LANES), lambda h, i, ks: (i, h)),
                  pl.BlockSpec((t, LANES), lambda h, i, ks: (i, h)),
                  pl.BlockSpec((s, LANES), lambda h, i, ks: (0, hp_blocks + h)),
                  pl.BlockSpec((s, LANES), lambda h, i, ks: (0, 2 * hp_blocks + h)),
                  pl.BlockSpec((None, nq, 2, t), lambda h, i, ks: (h, 0, 0, 0)),
                  pl.BlockSpec((None, t, 2), lambda h, i, ks: (h, i, 0)),
                  pl.BlockSpec((t, LANES), lambda h, i, ks: (i, 0))],
        out_specs=[pl.BlockSpec((t, LANES), lambda h, i, ks: (i, h)),
                   pl.BlockSpec((None, None, 2, t), lambda h, i, ks: (h, i, 0, 0))],
        out_shape=[_sds((s, hp_blocks * LANES), BF16), _sds((hp_blocks, nq, 2, t), F32)],
        scratch=[pltpu.VMEM((t, LANES), F32), pltpu.VMEM((2, t, LANES), F32)],
        sem=("parallel", "arbitrary"))(kb_start, qkv, do16, qkv, qkv, ck_rows, lse_c, delta)


def _flash_dkv(qkv, do16, cum, lse_rows, dl_rows, qb_end, name):
    s = qkv.shape[0]
    t = min(ATT_TILE, s)
    nq = s // t
    scale = HEAD_DIM ** -0.5
    hp_blocks = FOX_HEADS // 2

    def body(qe_ref, k_ref, v_ref, cum_ref, q_ref, do_ref, lse_ref, dl_ref, dk_ref, dv_ref, dck_ref,
             dk_acc, dv_acc, dck_acc):
        hp, kb = pl.program_id(0), pl.program_id(1)
        k2, v2 = k_ref[...] * scale, v_ref[...]
        first = _head_pair_masks(t)
        zero = jnp.zeros_like(k2)
        ks = (jnp.where(first, k2, zero), jnp.where(first, zero, k2))
        vs = (jnp.where(first, v2, zero), jnp.where(first, zero, v2))
        cumv = cum_ref[...]
        lane = lax.broadcasted_iota(jnp.int32, (t, LANES), 1)
        ck_b = [jnp.broadcast_to(jnp.sum(jnp.where(lane == 2 * hp + j, cumv, 0.0), axis=1, keepdims=True), (t, LANES))
                for j in range(2)]
        dk_acc[...] = jnp.zeros_like(dk_acc)
        dv_acc[...] = jnp.zeros_like(dv_acc)
        dck_acc[...] = jnp.zeros_like(dck_acc)

        def tile(qb, spans):
            off = pl.multiple_of(qb * t, t)
            q2 = q_ref[pl.ds(off, t), :]
            do2 = do_ref[pl.ds(off, t), :]
            lse, dl = lse_ref[qb], dl_ref[qb]
            for r0, r1, c0, c1, masked in spans:
                qq, dd, fr = q2[c0:c1], do2[c0:c1], first[r0:r1]
                dvs, dks = [], []
                for j in range(2):
                    sc = _dot(ks[j][r0:r1], qq, NT_DIMS)
                    if masked:
                        sc = jnp.where(_span_mask(r0, r1, c0, c1, transposed=True), sc, -jnp.inf)
                    dp = _dot(vs[j][r0:r1], dd, NT_DIMS) - dl[j:j + 1, c0:c1]
                    cb_ = ck_b[j][r0:r1]
                    pb = [jnp.exp((x - cb_) - l) for x, l in zip(_lane_blocks(sc), _lane_blocks(lse[j:j + 1, c0:c1]))]
                    dsb = [p * d for p, d in zip(pb, _lane_blocks(dp))]
                    dck_acc[j, r0:r1] += _sum_list(dsb)
                    dvs.append(_dot(jnp.concatenate(pb, axis=1).astype(BF16), dd))
                    dks.append(_dot(jnp.concatenate(dsb, axis=1).astype(BF16), qq))
                dv_acc[r0:r1] += jnp.where(fr, dvs[0], dvs[1])
                dk_acc[r0:r1] += jnp.where(fr, dks[0], dks[1])

        def step(qb, carry):
            tile(qb, _full_spans(t))
            return carry

        tile(kb, _diagonal_spans(t, transposed=True))
        lax.fori_loop(kb + 1, qe_ref[hp, kb] + 1, step, 0)
        dk_ref[...] = (dk_acc[...] * scale).astype(dk_ref.dtype)
        dv_ref[...] = dv_acc[...].astype(dv_ref.dtype)
        for j in range(2):
            dck_ref[j:j + 1, :] = -jnp.sum(dck_acc[j].T, axis=0, keepdims=True)

    return _pcall(
        body, name=name, grid=(hp_blocks, nq), prefetch=1,
        in_specs=[pl.BlockSpec((t, LANES), lambda h, j, qe: (j, hp_blocks + h)),
                  pl.BlockSpec((t, LANES), lambda h, j, qe: (j, 2 * hp_blocks + h)),
                  pl.BlockSpec((t, LANES), lambda h, j, qe: (j, 0)),
                  pl.BlockSpec((s, LANES), lambda h, j, qe: (0, h)),
                  pl.BlockSpec((s, LANES), lambda h, j, qe: (0, h)),
                  pl.BlockSpec((None, nq, 2, t), lambda h, j, qe: (h, 0, 0, 0)),
                  pl.BlockSpec((None, nq, 2, t), lambda h, j, qe: (h, 0, 0, 0))],
        out_specs=[pl.BlockSpec((t, LANES), lambda h, j, qe: (j, h)),
                   pl.BlockSpec((t, LANES), lambda h, j, qe: (j, h)),
                   pl.BlockSpec((None, None, 2, t), lambda h, j, qe: (h, j, 0, 0))],
        out_shape=[_sds((s, hp_blocks * LANES), BF16), _sds((s, hp_blocks * LANES), BF16),
                   _sds((hp_blocks, nq, 2, t), F32)],
        scratch=[pltpu.VMEM((t, LANES), F32), pltpu.VMEM((t, LANES), F32), pltpu.VMEM((2, t, LANES), F32)],
        sem=("parallel", "arbitrary"))(qb_end, qkv, qkv, cum, qkv, do16, lse_rows, dl_rows)


SKIP_NATS = 110.0


def _qk_norms(qkv, ind16, name):
    d = FOX_HEADS * HEAD_DIM

    def fn(tile, ind):
        q = tile[:, :d].astype(F32)
        k = tile[:, d:2 * d].astype(F32)
        return _dot_split(q * q, ind, passes=2), _dot_split(k * k, ind, passes=2)
    return _rowwise(fn, [qkv], [ind16], [(LANES, F32), (LANES, F32)], [], name=name)


def _skip_bounds(qn, kn, cum, t):
    s = qn.shape[0]
    nq = s // t
    hp = FOX_HEADS // 2
    scale = HEAD_DIM ** -0.5
    qmax = jnp.sqrt(jnp.max(qn.reshape(nq, t, FOX_HEADS), axis=1))
    kmax = jnp.sqrt(jnp.max(kn, axis=0))
    bound = qmax * kmax[None, :] * (scale * 1.01) + 1e-3
    gap = cum[0::t][:, None, :] - cum[t - 1::t][None, :, :]
    idx = jnp.arange(nq, dtype=jnp.int32)
    needed = (gap + 2.0 * bound[:, None, :]) > -SKIP_NATS
    needed = needed.reshape(nq, nq, hp, 2).any(axis=-1) & (idx[None, :] <= idx[:, None])[:, :, None]
    first = jnp.min(jnp.where(needed, idx[None, :, None], nq), axis=1)
    first = jnp.minimum(first, idx[:, None])
    start = lax.cummin(first, axis=0, reverse=True)
    uses = start[:, None, :] <= idx[None, :, None]
    last = jnp.max(jnp.where(uses, idx[:, None, None], 0), axis=0)
    last = jnp.maximum(last, idx[:, None])
    return start.T.astype(jnp.int32), last.T.astype(jnp.int32)


def _head_rowsum(a, b, ind16, name):
    s, d = a.shape
    tm = min(ROW_TILE, s)

    def body(a_ref, b_ref, ind_ref, o_ref, rows_ref):
        dsum = _dot_split(a_ref[...] * b_ref[...], ind_ref[...])
        o_ref[...] = dsum
        rows_ref[...] = dsum.T[0:FOX_HEADS, :]

    tile = pl.BlockSpec((tm, d), lambda i: (i, 0))
    return _pcall(body, name=name, grid=(s // tm,),
                  in_specs=[tile, tile, pl.BlockSpec((d, LANES), lambda i: (0, 0))],
                  out_specs=[pl.BlockSpec((tm, LANES), lambda i: (i, 0)), pl.BlockSpec((FOX_HEADS, tm), lambda i: (0, i))],
                  out_shape=[_sds((s, LANES), F32), _sds((FOX_HEADS, s), F32)], sem=("parallel",))(a, b, ind16)


def _rows_to_tiles(x, t):
    s = x.shape[1]
    return x.reshape(FOX_HEADS // 2, 2, s // t, t).transpose(0, 2, 1, 3)


def _tiles_to_cols(x):
    hp, nq, _, t = x.shape
    return jnp.pad(x.transpose(1, 3, 0, 2).reshape(nq * t, 2 * hp), ((0, 0), (0, LANES - 2 * hp)))


def _fox_forward(u, w):
    s = u.shape[0]
    t = min(ATT_TILE, s)
    qkv = _mm(u, w["fox_qkv"], "nn", [BF16], name="fox_qkv")
    fraw = _mm(u, w["fox_f"], "nn", [F32], name="fox_fproj")
    cum, cum_rows = _fox_gate(fraw, w["fox_bf"], "fox_gate")
    ck_rows = _rows_to_tiles(cum_rows, t)
    qn, kn = _qk_norms(qkv, w["head_ind"], "fox_qk_norms")
    kb_start, qb_end = _skip_bounds(qn[:, :FOX_HEADS], kn[:, :FOX_HEADS], cum[:, :FOX_HEADS], t)
    o, lse, lse_rows = _flash_fwd(qkv, ck_rows, kb_start, "fox_flash_fwd")
    y = _mm(o, w["fox_o"], "nn", [F32], name="fox_oproj")
    return y, dict(u=u, qkv=qkv, fraw=fraw, cum=cum, ck_rows=ck_rows, o=o, lse=lse, lse_rows=lse_rows,
                   kb_start=kb_start, qb_end=qb_end)


def _fox_backward(dy, sv, w):
    s = dy.shape[0]
    t = min(ATT_TILE, s)
    do32, do16 = _mm(dy, w["fox_o"], "nt", [F32, BF16], name="fox_do")
    g_wo = _mm(sv["o"], dy, "tn", [F32], name="fox_gwo")
    delta, delta_rows = _head_rowsum(do32, sv["o"], w["head_ind"], "fox_delta")
    dq, drow = _flash_dq(sv["qkv"], do16, sv["ck_rows"], sv["lse"], delta, sv["kb_start"], "fox_flash_dq")
    dk, dv, dck = _flash_dkv(sv["qkv"], do16, sv["cum"], sv["lse_rows"], _rows_to_tiles(delta_rows, t),
                             sv["qb_end"], "fox_flash_dkv")
    df, db_f = _fox_gate_bwd(_tiles_to_cols(drow), _tiles_to_cols(dck), sv["fraw"], w["fox_bf"], "fox_gate_bwd")
    du = _mm(df, w["fox_f"], "nt", [F32], name="fox_du_f")
    du = _mm_nt_blocks(_k_blocks(dq) + _k_blocks(dk) + _k_blocks(dv), w["fox_qkv"], du, name="fox_du")
    g_win = jnp.concatenate([_mm(sv["u"], dq, "tn", [F32], name="fox_gwin_q"),
                             _mm(sv["u"], dk, "tn", [F32], name="fox_gwin_k"),
                             _mm(sv["u"], dv, "tn", [F32], name="fox_gwin_v"),
                             _mm(sv["u"], df, "tn", [F32], name="fox_gwin_f")[:, :FOX_HEADS]], axis=1)
    return du, dict(fox_w_in=g_win, fox_w_o=g_wo, fox_b_f=db_f[:, :FOX_HEADS])


def _conv_fwd(xpre, w8, b, name):
    s, c = xpre.shape
    tm, tc = min(ROW_TILE, s), min(1024, c)
    hb = tm // 8

    r = min(CONV_CHUNK, tm)

    def body(x_ref, h_ref, w_ref, b_ref, xc_ref, xa_ref):
        i = pl.program_id(1)
        row8 = lax.broadcasted_iota(jnp.int32, (8, LANES), 0)
        for cb in range(tc // LANES):
            ls = slice(cb * LANES, (cb + 1) * LANES)
            w, bias = w_ref[:, ls], b_ref[:, ls]
            for rb in range(tm // r):
                r0 = rb * r
                cur = x_ref[r0:r0 + r, ls]
                acc = cur * w[3:4] + bias
                if rb == 0:
                    halo = jnp.where(i > 0, h_ref[:, ls], 0.0)
                    x8 = cur[0:8]
                    acc8 = x8 * w[3:4] + bias
                    for j in range(1, SSM_CONV):
                        acc = acc + w[3 - j:4 - j] * pltpu.roll(cur, j, 0)
                        acc8 = acc8 + w[3 - j:4 - j] * jnp.where(row8 < j, pltpu.roll(halo, j, 0), pltpu.roll(x8, j, 0))
                    acc = jnp.concatenate([acc8, acc[8:]], axis=0)
                else:
                    for j in range(1, SSM_CONV):
                        acc = acc + w[3 - j:4 - j] * x_ref[r0 - j:r0 - j + r, ls]
                xc_ref[r0:r0 + r, ls] = acc
                xa_ref[r0:r0 + r, ls] = _silu(acc)

    tile = pl.BlockSpec((tm, tc), lambda jc, i: (i, jc))
    return _pcall(body, name=name, grid=(c // tc, s // tm),
                  in_specs=[tile, pl.BlockSpec((8, tc), lambda jc, i: (jnp.maximum(i * hb - 1, 0), jc)),
                            pl.BlockSpec((8, tc), lambda jc, i: (0, jc)), pl.BlockSpec((1, tc), lambda jc, i: (0, jc))],
                  out_specs=[tile, tile], out_shape=[_sds((s, c), F32), _sds((s, c), F32)],
                  sem=("parallel", "arbitrary"))(xpre, xpre, w8, b)


def _conv_bwd(dxa, xc, xpre, w8, name):
    s, c = xpre.shape
    tm, tc = min(ROW_TILE, s), min(1024, c)
    hb = tm // 8
    n = s // tm

    r = min(CONV_CHUNK, tm)

    def body(d_ref, xc_ref, x_ref, xh_ref, dn_ref, xcn_ref, w_ref, dx_ref, dw_ref, db_ref, g_scr):
        i = pl.program_id(1)

        @pl.when(i == 0)
        def _():
            dw_ref[...] = jnp.zeros_like(dw_ref)
            db_ref[...] = jnp.zeros_like(db_ref)
        row8 = lax.broadcasted_iota(jnp.int32, (8, LANES), 0)
        rowr = lax.broadcasted_iota(jnp.int32, (r, LANES), 0)
        for cb in range(tc // LANES):
            ls = slice(cb * LANES, (cb + 1) * LANES)
            w = w_ref[:, ls]
            for rb in range(tm // r):
                r0 = rb * r
                g_scr[r0:r0 + r, ls] = d_ref[r0:r0 + r, ls] * _dsilu(xc_ref[r0:r0 + r, ls])
            g_scr[tm:tm + 8, ls] = jnp.where(i < n - 1, dn_ref[:, ls] * _dsilu(xcn_ref[:, ls]), 0.0)
            db = jnp.zeros((1, LANES), F32)
            dws = [jnp.zeros((1, LANES), F32) for _ in range(SSM_CONV)]
            for rb in range(tm // r):
                r0 = rb * r
                g = g_scr[r0:r0 + r, ls]
                x = x_ref[r0:r0 + r, ls]
                db = db + _colsum(g)
                dws[3] = dws[3] + _colsum(g * x)
                acc = g * w[3:4]
                for j in range(1, SSM_CONV):
                    if rb == 0:
                        halo = jnp.where(i > 0, xh_ref[:, ls], 0.0)
                        dws[3 - j] = dws[3 - j] + _colsum(g * jnp.where(rowr >= j, pltpu.roll(x, j, 0), 0.0))
                        dws[3 - j] = dws[3 - j] + _colsum(jnp.where(row8 < j, g[0:8] * pltpu.roll(halo, j, 0), 0.0))
                    else:
                        dws[3 - j] = dws[3 - j] + _colsum(g * x_ref[r0 - j:r0 - j + r, ls])
                    acc = acc + w[3 - j:4 - j] * g_scr[r0 + j:r0 + j + r, ls]
                dx_ref[r0:r0 + r, ls] = acc.astype(dx_ref.dtype)
            db_ref[:, ls] += db
            for k in range(SSM_CONV):
                dw_ref[k:k + 1, ls] += dws[k]

    tile = pl.BlockSpec((tm, tc), lambda jc, i: (i, jc))
    prev8 = pl.BlockSpec((8, tc), lambda jc, i: (jnp.maximum(i * hb - 1, 0), jc))
    next8 = pl.BlockSpec((8, tc), lambda jc, i: (jnp.minimum((i + 1) * hb, n * hb - 1), jc))
    return _pcall(body, name=name, grid=(c // tc, n),
                  in_specs=[tile, tile, tile, prev8, next8, next8, pl.BlockSpec((8, tc), lambda jc, i: (0, jc))],
                  out_specs=[tile, pl.BlockSpec((8, tc), lambda jc, i: (0, jc)), pl.BlockSpec((1, tc), lambda jc, i: (0, jc))],
                  out_shape=[_sds((s, c), BF16), _sds((8, c), F32), _sds((1, c), F32)],
                  scratch=[pltpu.VMEM((tm + 8, tc), F32)],
                  sem=("parallel", "arbitrary"))(dxa, xc, xpre, xpre, dxa, xc, w8)


def _ssd_pre(dtraw, dt_bias, a_log, cst, name):
    def fn(raw, bias, alog, expand):
        tm = raw.shape[0]
        z = raw + bias
        dt = jnp.maximum(z, 0.0) + jnp.log(1.0 + jnp.exp(-jnp.abs(z)))
        lane = lax.broadcasted_iota(jnp.int32, (tm, LANES), 1)
        pos = lax.broadcasted_iota(jnp.int32, (tm, LANES), 0) & (SSM_CHUNK - 1)
        dt = jnp.where(lane < SSM_HEADS, dt, 0.0)
        c = dt * (-jnp.exp(alog))
        sh = 1
        while sh < SSM_CHUNK:
            c = c + jnp.where(pos >= sh, pltpu.roll(c, sh, 0), 0.0)
            sh *= 2
        return dt, c, _dot_split(dt, expand), _dot_split(c, expand)
    wide = SSM_HEADS * HEAD_DIM
    return _rowwise(fn, [dtraw], [dt_bias, a_log, cst["expand"]],
                    [(LANES, F32), (LANES, F32), (wide, F32), (wide, F32)], [], name=name)


def _ssd_post(dacs, ddt, dtraw, dt, dt_bias, a_log, name):
    def fn(dacs_v, ddt_v, raw, dt_v, bias, alog):
        tm = raw.shape[0]
        pos = lax.broadcasted_iota(jnp.int32, (tm, LANES), 0) & (SSM_CHUNK - 1)
        a = -jnp.exp(alog)
        c = dacs_v
        sh = 1
        while sh < SSM_CHUNK:
            c = c + jnp.where(pos + sh < SSM_CHUNK, pltpu.roll(c, tm - sh, 0), 0.0)
            sh *= 2
        draw = (ddt_v + c * a) * _sigmoid(raw + bias)
        return draw, _colsum(draw), _colsum(c * dt_v * a)
    return _rowwise(fn, [dacs, ddt, dtraw, dt], [dt_bias, a_log], [(LANES, BF16)], [(1, LANES)] * 2, name=name)


def _heads_rows(x, s):
    return x[:, :SSM_HEADS].reshape(s // SSM_CHUNK, SSM_CHUNK, SSM_GROUPS, 4).transpose(2, 0, 3, 1)


def _ssd_constants():
    gp = SSD_GROUPS_PER_STEP
    src = np.arange(LANES)[:, None]
    expand = src == np.arange(SSM_HEADS * HEAD_DIM)[None, :] // HEAD_DIM
    dst = np.arange(LANES)[None, None, :] - 4 * np.arange(gp)[:, None, None]
    seg = np.arange(SSM_GROUP_WIDTH)[None, :, None] // HEAD_DIM == dst
    seg4 = np.arange(4 * LANES)[None, :, None] // LANES == dst
    return dict(expand=jnp.asarray(expand, BF16), seg=jnp.asarray(seg, BF16), seg4=jnp.asarray(seg4, BF16))


def _ssm_weights(w_in, conv_w, conv_b, dt_bias, a_log, d_skip, norm_w, w_out):
    pad = ((0, 0), (0, LANES - SSM_HEADS))
    w_xbc = _group_cols(w_in[:, 2048:6144])
    w_dt = jnp.pad(w_in[:, 6144:], pad)
    return dict(
        ssm_z=w_in[:, :2048], ssm_xbc=w_xbc, ssm_dt=w_dt,
        ssm_zx=jnp.concatenate([w_in[:, :2048], w_xbc], axis=1),
        ssm_out=w_out,
        conv_w8=_group_cols(jnp.pad(conv_w, ((0, 8 - SSM_CONV), (0, 0)))),
        conv_b=_group_cols(conv_b), norm_w=norm_w,
        dt_bias=jnp.pad(dt_bias, pad), a_log=jnp.pad(a_log, pad),
        d_e=jnp.repeat(d_skip.reshape(SSM_GROUPS, 4), HEAD_DIM, axis=1)[:, None, :],
        ssd_cst=_ssd_constants())


def _ssd_setup(acs_e, acsr):
    l = SSM_CHUNK
    last = acsr[:, l - 1:l]
    lane1 = lax.broadcasted_iota(jnp.int32, (1, SSM_GROUP_WIDTH), 1)
    last_e = last[3:4, :]
    for r in (2, 1, 0):
        last_e = jnp.where(lane1 < HEAD_DIM * (r + 1), last[r:r + 1, :], last_e)
    return jnp.exp(acs_e), jnp.exp(last_e - acs_e), jnp.exp(last_e)


def _head_bcast(acs_e):
    lo = lax.broadcasted_iota(jnp.int32, (acs_e.shape[0], LANES), 1) < HEAD_DIM
    out = []
    for p in range(2):
        blk = acs_e[:, p * LANES:(p + 1) * LANES]
        rolled = pltpu.roll(blk, HEAD_DIM, 1)
        out += [jnp.where(lo, blk, rolled), jnp.where(lo, rolled, blk)]
    return out


def _group_cols(a):
    lead = a.shape[:-1]
    x = a[..., :2048].reshape(lead + (SSM_GROUPS, SSM_GROUP_WIDTH))
    b = a[..., 2048:3072].reshape(lead + (SSM_GROUPS, SSM_STATE))
    c = a[..., 3072:].reshape(lead + (SSM_GROUPS, SSM_STATE))
    return jnp.concatenate([x, b, c], axis=-1).reshape(lead + (4096,))


def _ungroup_cols(a):
    lead = a.shape[:-1]
    y = a.reshape(lead + (SSM_GROUPS, SSM_GROUP_WIDTH + 2 * SSM_STATE))
    return jnp.concatenate([y[..., :256].reshape(lead + (2048,)), y[..., 256:384].reshape(lead + (1024,)),
                            y[..., 384:].reshape(lead + (1024,))], axis=-1)


def _ssd_fwd2(xa, dte, acse, acsr, d_e, name):
    s = xa.shape[0]
    l, gw, ns = SSM_CHUNK, SSM_GROUP_WIDTH, SSM_STATE
    nc = s // l

    gb = gw + 2 * ns
    gp = SSD_GROUPS_PER_STEP

    def body(xa_ref, dt_ref, acs_ref, acsr_ref, d_ref, y_ref, hp_ref, h_sc):
        @pl.when(pl.program_id(1) == 0)
        def _():
            h_sc[...] = jnp.zeros_like(h_sc)
        lane = lax.broadcasted_iota(jnp.int32, (l, gw), 1)
        tril = _causal(l)
        for gi in range(gp):
            x = xa_ref[:, gi * gb:gi * gb + gw]
            bm = xa_ref[:, gi * gb + gw:gi * gb + gw + ns].astype(BF16)
            cm = xa_ref[:, gi * gb + gw + ns:(gi + 1) * gb].astype(BF16)
            acsr = acsr_ref[gi]
            dt_e, acs_e = dt_ref[:, gi * gw:(gi + 1) * gw], acs_ref[:, gi * gw:(gi + 1) * gw]
            acs_bc = _head_bcast(acs_e)
            e_e, dte_e, cd_e = _ssd_setup(acs_e, acsr)
            xdt = x * dt_e
            xdt16 = xdt.astype(BF16)
            cb = _dot(cm, bm, NT_DIMS)
            yd = jnp.zeros((l, gw), F32)
            for r in range(4):
                lm = jnp.exp(jnp.where(tril, acs_bc[r] - acsr[r:r + 1, :], -jnp.inf))
                yr = _dot((cb * lm).astype(BF16), xdt16)
                yd = jnp.where((lane >= HEAD_DIM * r) & (lane < HEAD_DIM * (r + 1)), yr, yd)
            hp = h_sc[gi]
            hp_ref[gi] = hp
            y_ref[:, gi * gw:(gi + 1) * gw] = yd + _dot(cm, hp.astype(BF16)) * e_e + x * d_ref[gi]
            h_sc[gi] = hp * cd_e + _dot(bm, (xdt * dte_e).astype(BF16), TN_DIMS)

    return _pcall(
        body, name=name, grid=(SSM_GROUPS // gp, nc),
        in_specs=[pl.BlockSpec((l, gp * gb), lambda g, c: (c, g)),
                  pl.BlockSpec((l, gp * gw), lambda g, c: (c, g)),
                  pl.BlockSpec((l, gp * gw), lambda g, c: (c, g)),
                  pl.BlockSpec((gp, None, 4, l), lambda g, c: (g, c, 0, 0)),
                  pl.BlockSpec((gp, 1, gw), lambda g, c: (g, 0, 0))],
        out_specs=[pl.BlockSpec((l, gp * gw), lambda g, c: (c, g)),
                   pl.BlockSpec((gp, None, ns, gw), lambda g, c: (g, c, 0, 0))],
        out_shape=[_sds((s, 2048), F32), _sds((SSM_GROUPS, nc, ns, gw), F32)],
        scratch=[pltpu.VMEM((gp, ns, gw), F32)],
        sem=("parallel", "arbitrary"))(xa, dte, acse, acsr, d_e)


def _ssd_bwd2(dy, xa, dte, acse, acsr, d_e, hprev, cst, name):
    s = xa.shape[0]
    l, gw, ns = SSM_CHUNK, SSM_GROUP_WIDTH, SSM_STATE
    nc = s // l

    gb = gw + 2 * ns
    gp = SSD_GROUPS_PER_STEP

    def body(dy_ref, xa_ref, dt_ref, acs_ref, acsr_ref, d_ref, hp_ref,
             seg_ref, seg4_ref, dxa_ref, dacs_ref, ddt_ref, dd_ref, dh_sc):
        @pl.when(pl.program_id(1) == 0)
        def _():
            dh_sc[...] = jnp.zeros_like(dh_sc)
            dd_ref[...] = jnp.zeros_like(dd_ref)
        parts = [one_group(gi, dy_ref, xa_ref, dt_ref, acs_ref, acsr_ref, d_ref, hp_ref, seg_ref, seg4_ref,
                           dxa_ref, dh_sc) for gi in range(gp)]
        dacs_ref[...] = _sum_list([p[0] for p in parts])
        ddt_ref[...] = _sum_list([p[1] for p in parts])
        dd_ref[0:1, :] += _sum_list([p[2] for p in parts])

    def one_group(gi, dy_ref, xa_ref, dt_ref, acs_ref, acsr_ref, d_ref, hp_ref, seg_ref, seg4_ref, dxa_ref, dh_sc):
        dyv = dy_ref[:, gi * gw:(gi + 1) * gw]
        x = xa_ref[:, gi * gb:gi * gb + gw]
        bm = xa_ref[:, gi * gb + gw:gi * gb + gw + ns].astype(BF16)
        cm = xa_ref[:, gi * gb + gw + ns:(gi + 1) * gb].astype(BF16)
        acsr = acsr_ref[gi]
        dt_e, acs_e = dt_ref[:, gi * gw:(gi + 1) * gw], acs_ref[:, gi * gw:(gi + 1) * gw]
        acs_bc = _head_bcast(acs_e)
        e_e, dte_e, cd_e = _ssd_setup(acs_e, acsr)
        seg, seg4 = seg_ref[gi], seg4_ref[gi]
        lane = lax.broadcasted_iota(jnp.int32, (l, gw), 1)
        xdt = x * dt_e
        xdt16 = xdt.astype(BF16)
        dy16 = dyv.astype(BF16)
        cb = _dot(cm, bm, NT_DIMS)
        cbt = _dot(bm, cm, NT_DIMS)
        hp = hp_ref[gi]
        hp16 = hp.astype(BF16)
        g = dh_sc[gi]
        g16 = g.astype(BF16)
        t_all = _dot(cm, hp16)
        dt16 = (dyv * e_e).astype(BF16)
        dc = _dot(dt16, hp16, NT_DIMS)
        dhp = _dot(cm, dt16, TN_DIMS)
        wv = xdt * dte_e
        dw = _dot(bm, g16)
        db = _dot(wv.astype(BF16), g16, NT_DIMS)
        dxdt = dw * dte_e
        acs_term = dyv * t_all * e_e - dw * wv
        last_term = _colsum(dw * wv) + _colsum(g * hp) * cd_e
        dh_sc[gi] = g * cd_e + dhp
        tril, triu = _causal(l), _causal(l, transposed=True)
        dcb = jnp.zeros((l, l), F32)
        dcbt = jnp.zeros((l, l), F32)
        qd = []
        for r in range(4):
            in_head = (lane >= HEAD_DIM * r) & (lane < HEAD_DIM * (r + 1))
            a_col = acs_bc[r]
            lm = jnp.exp(jnp.where(tril, a_col - acsr[r:r + 1, :], -jnp.inf))
            lmt = jnp.exp(jnp.where(triu, acsr[r:r + 1, :] - a_col, -jnp.inf))
            mm_, mt = cb * lm, cbt * lmt
            dyr = jnp.where(in_head, dy16, jnp.zeros_like(dy16))
            dm = _dot(dyr, xdt16, NT_DIMS)
            dmt = _dot(xdt16, dyr, NT_DIMS)
            dxdt = dxdt + jnp.where(in_head, _dot(mt.astype(BF16), dy16), 0.0)
            dcb = dcb + dm * lm
            dcbt = dcbt + dmt * lmt
            qd.append(dm * mm_ - dmt * mt)
        dc = dc + _dot(dcb.astype(BF16), bm)
        db = db + _dot(dcbt.astype(BF16), cm)
        rowl = lax.broadcasted_iota(jnp.int32, (l, LANES), 0)
        row8 = lax.broadcasted_iota(jnp.int32, (8, gw), 0)
        small = _dot_split(jnp.where(row8 == 0, last_term, jnp.where(row8 == 1, _colsum(dyv * x), 0.0)), seg, passes=2)
        big = _dot_split(jnp.concatenate([acs_term, dxdt * x], axis=0), seg, passes=2)
        dacs = (big[0:l] + _dot_split(jnp.concatenate(qd, axis=1), seg4, passes=2)
                + jnp.where(rowl == l - 1, small[0:1, :], 0.0))
        dxa_ref[:, gi * gb:(gi + 1) * gb] = jnp.concatenate([dxdt * dt_e + dyv * d_ref[gi], db, dc], axis=1)
        return dacs, big[l:2 * l], small[1:2, :]

    rc = lambda c: nc - 1 - c
    ng = SSM_GROUPS // gp
    return _pcall(
        body, name=name, grid=(ng, nc),
        in_specs=[pl.BlockSpec((l, gp * gw), lambda g, c: (rc(c), g)),
                  pl.BlockSpec((l, gp * gb), lambda g, c: (rc(c), g)),
                  pl.BlockSpec((l, gp * gw), lambda g, c: (rc(c), g)),
                  pl.BlockSpec((l, gp * gw), lambda g, c: (rc(c), g)),
                  pl.BlockSpec((gp, None, 4, l), lambda g, c: (g, rc(c), 0, 0)),
                  pl.BlockSpec((gp, 1, gw), lambda g, c: (g, 0, 0)),
                  pl.BlockSpec((gp, None, ns, gw), lambda g, c: (g, rc(c), 0, 0)),
                  pl.BlockSpec((gp, gw, LANES), lambda g, c: (0, 0, 0)),
                  pl.BlockSpec((gp, 4 * LANES, LANES), lambda g, c: (0, 0, 0))],
        out_specs=[pl.BlockSpec((l, gp * gb), lambda g, c: (rc(c), g)),
                   pl.BlockSpec((None, l, LANES), lambda g, c: (g, rc(c), 0)),
                   pl.BlockSpec((None, l, LANES), lambda g, c: (g, rc(c), 0)),
                   pl.BlockSpec((None, 8, LANES), lambda g, c: (g, 0, 0))],
        out_shape=[_sds((s, 4096), F32), _sds((ng, s, LANES), F32), _sds((ng, s, LANES), F32),
                   _sds((ng, 8, LANES), F32)],
        scratch=[pltpu.VMEM((gp, ns, gw), F32)],
        sem=("parallel", "arbitrary"))(dy, xa, dte, acse, acsr, d_e, hprev, cst["seg"], cst["seg4"])


def _gate_norm(y, z, nw, name):
    c = y.shape[1]

    def fn(yv, zv, w):
        outs = []
        for k in range(c // SSM_GROUP_WIDTH):
            sl = slice(k * SSM_GROUP_WIDTH, (k + 1) * SSM_GROUP_WIDTH)
            yg = yv[:, sl] * _silu(zv[:, sl])
            rinv = lax.rsqrt(jnp.mean(yg * yg, axis=-1, keepdims=True) + RMS_EPS)
            outs.append(yg * rinv * w[:, sl])
        return (jnp.concatenate(outs, axis=1),)
    return _rowwise(fn, [y, z], [nw], [(c, BF16)], [], name=name)[0]


def _out_gate_norm_bwd(dy, w_out, y, z, nw, name):
    s, c = y.shape
    tm, tn = min(512, s), min(1024, c)
    k = dy.shape[1]

    def body(a_ref, b_ref, y_ref, z_ref, w_ref, dy_ref, dz_ref, dw_ref):
        @pl.when(pl.program_id(1) == 0)
        def _():
            dw_ref[...] = jnp.zeros_like(dw_ref)
        dv = _dot(a_ref[...], b_ref[...], NT_DIMS)
        yv, zv, w = y_ref[...], z_ref[...], w_ref[...]
        dys, dzs, dws = [], [], []
        for g in range(tn // SSM_GROUP_WIDTH):
            sl = slice(g * SSM_GROUP_WIDTH, (g + 1) * SSM_GROUP_WIDTH)
            ys, zs, ds = yv[:, sl], zv[:, sl], dv[:, sl]
            sz = _silu(zs)
            yg = ys * sz
            rinv = lax.rsqrt(jnp.mean(yg * yg, axis=-1, keepdims=True) + RMS_EPS)
            nrm = yg * rinv
            dn = ds * w[:, sl]
            dyg = rinv * (dn - nrm * jnp.mean(dn * nrm, axis=-1, keepdims=True))
            dys.append(dyg * sz)
            dzs.append(dyg * ys * _dsilu(zs))
            dws.append(_colsum(ds * nrm))
        dy_ref[...] = jnp.concatenate(dys, axis=1)
        dz_ref[...] = jnp.concatenate(dzs, axis=1).astype(dz_ref.dtype)
        dw_ref[...] += jnp.concatenate(dws, axis=1)

    tile = pl.BlockSpec((tm, tn), lambda j, i: (i, j))
    row = pl.BlockSpec((1, tn), lambda j, i: (0, j))
    return _pcall(body, name=name, grid=(c // tn, s // tm),
                  in_specs=[pl.BlockSpec((tm, k), lambda j, i: (i, 0)), pl.BlockSpec((tn, k), lambda j, i: (j, 0)),
                            tile, tile, row],
                  out_specs=[tile, tile, row], out_shape=[_sds((s, c), F32), _sds((s, c), BF16), _sds((1, c), F32)],
                  sem=("parallel", "arbitrary"))(dy, w_out, y, z, nw)


def _ssd_forward(u, w):
    s = u.shape[0]
    z = _mm(u, w["ssm_z"], "nn", [F32], name="ssm_zproj")
    xpre = _mm(u, w["ssm_xbc"], "nn", [F32], name="ssm_xproj")
    dtraw = _mm(u, w["ssm_dt"], "nn", [F32], name="ssm_dtproj")
    xc, xa = _conv_fwd(xpre, w["conv_w8"], w["conv_b"], "ssm_conv")
    dt, acs, dte, acse = _ssd_pre(dtraw, w["dt_bias"], w["a_log"], w["ssd_cst"], "ssm_pre")
    acsr = _heads_rows(acs, s)
    y, hprev = _ssd_fwd2(xa, dte, acse, acsr, w["d_e"], "ssm_scan")
    yn = _gate_norm(y, z, w["norm_w"], "ssm_gate_norm")
    out = _mm(yn, w["ssm_out"], "nn", [F32], name="ssm_oproj")
    return out, dict(u=u, z=z, xpre=xpre, xc=xc, xa=xa, dtraw=dtraw, dt=dt, dte=dte, acse=acse,
                     acsr=acsr, y=y, hprev=hprev, yn=yn)


def _ssd_backward(dy, sv, w):
    s = dy.shape[0]
    g_wout = _mm(sv["yn"], dy, "tn", [F32], name="ssm_gwout")
    dys, dz, dnw = _out_gate_norm_bwd(dy, w["ssm_out"], sv["y"], sv["z"], w["norm_w"], "ssm_dyn_gate_norm_bwd")
    dxa, dacs_c, ddt_c, dd = _ssd_bwd2(dys, sv["xa"], sv["dte"], sv["acse"], sv["acsr"], w["d_e"], sv["hprev"],
                                       w["ssd_cst"], "ssm_scan_bwd")
    per_step = 4 * SSD_GROUPS_PER_STEP
    pad = ((0, 0), (0, LANES - SSM_HEADS))
    dacs = jnp.pad(jnp.concatenate([a[:, :per_step] for a in dacs_c], axis=1), pad)
    ddt = jnp.pad(jnp.concatenate([a[:, :per_step] for a in ddt_c], axis=1), pad)
    draw, dbias, dalog = _ssd_post(dacs, ddt, sv["dtraw"], sv["dt"], w["dt_bias"], w["a_log"], "ssm_post")
    dxpre, dcw, dcb = _conv_bwd(dxa, sv["xc"], sv["xpre"], w["conv_w8"], "ssm_conv_bwd")
    du = _mm(draw, w["ssm_dt"], "nt", [F32], name="ssm_du_dt")
    n_z = dz.shape[1]
    du = _mm_nt_blocks(_k_blocks(dz), w["ssm_zx"][:, :n_z], du, name="ssm_du_z")
    du = _mm_nt_blocks(_k_blocks(dxpre), w["ssm_zx"][:, n_z:], du, name="ssm_du_x")
    g_win = jnp.concatenate([_mm(sv["u"], dz, "tn", [F32], name="ssm_gwin_z"),
                             _ungroup_cols(_mm(sv["u"], dxpre, "tn", [F32], name="ssm_gwin_x")),
                             _mm(sv["u"], draw, "tn", [F32], name="ssm_gwin_dt")[:, :SSM_HEADS]], axis=1)
    return du, dict(ssm_w_in=g_win, ssm_w_out=g_wout, ssm_conv_w=_ungroup_cols(dcw[:SSM_CONV]),
                    ssm_conv_b=_ungroup_cols(dcb), ssm_norm_w=dnw, ssm_dt_bias=dbias[:, :SSM_HEADS],
                    ssm_a_log=dalog[:, :SSM_HEADS], ssm_d=dd[:, 0, :per_step].reshape(1, SSM_HEADS))


def _mlp_forward(u2, w1, w2, tag):
    def epi(acc):
        hr = jnp.maximum(acc, 0.0)
        return hr, hr * hr
    hr, a = _mm(u2, w1, "nn", [BF16, BF16], name=tag + "_mlp_up", epi=epi)
    y2 = _mm(a, w2, "nn", [F32], name=tag + "_mlp_down")
    return y2, hr, a


def _mlp_backward(dy2, u2, hr, a, w1, w2, tag):
    dh = _mm(dy2, w2, "nt", [BF16], name=tag + "_mlp_dh", extra=(hr,),
             epi=lambda acc, h: (acc * (2.0 * h.astype(F32)),))
    g_w2 = _mm(a, dy2, "tn", [F32], name=tag + "_mlp_gw2")
    g_w1 = _mm(u2, dh, "tn", [F32], name=tag + "_mlp_gw1")
    du2 = _mm(dh, w1, "nt", [F32], name=tag + "_mlp_du")
    return du2, g_w1, g_w2


def _ada_forward(c16, ada_w, ada_b_cols, name):
    nl, d, cols = ada_w.shape
    tn = 512

    def body(c_ref, w_ref, b_ref, o_ref):
        cond = _silu(c_ref[...]).astype(BF16)
        o_ref[...] = _dot(cond, w_ref[...].astype(BF16)) + b_ref[...]

    return _pcall(body, name=name, grid=(nl, cols // tn),
                  in_specs=[pl.BlockSpec((16, d), lambda i, j: (0, 0)),
                            pl.BlockSpec((None, d, tn), lambda i, j: (i, 0, j)),
                            pl.BlockSpec((None, 1, tn), lambda i, j: (i, 0, j))],
                  out_specs=pl.BlockSpec((None, 16, tn), lambda i, j: (i, 0, j)),
                  out_shape=_sds((nl, 16, cols), F32), sem=("parallel", "parallel"))(c16, ada_w, ada_b_cols)


def _ada_backward(c_t, dmod_cols, name):
    d, nb = c_t.shape
    nl, _, cols = dmod_cols.shape
    tn = 512

    def body(c_ref, dm_ref, o_ref):
        cond = _silu(c_ref[...])
        dm = dm_ref[...]
        acc = cond[:, 0:1] * dm[0:1, :]
        for b in range(1, nb):
            acc = acc + cond[:, b:b + 1] * dm[b:b + 1, :]
        o_ref[...] = acc

    return _pcall(body, name=name, grid=(nl, cols // tn),
                  in_specs=[pl.BlockSpec((d, nb), lambda i, j: (0, 0)),
                            pl.BlockSpec((None, nb, tn), lambda i, j: (i, 0, j))],
                  out_specs=pl.BlockSpec((None, d, tn), lambda i, j: (i, 0, j)),
                  out_shape=_sds((nl, d, cols), F32), sem=("parallel", "parallel"))(c_t, dmod_cols)


def _adamw(w, g, m, v, name):
    rows, cols = w.shape
    tm = rows
    for cand in (256, 128, 64, 32, 16, 8):
        if rows % cand == 0 and rows > cand:
            tm = cand
            break
    c1 = 1.0 / (1.0 - ADAM_B1 ** ADAM_STEP)
    c2 = 1.0 / (1.0 - ADAM_B2 ** ADAM_STEP)

    def fn(wv, gv, mv, vv):
        mn = ADAM_B1 * mv + (1.0 - ADAM_B1) * gv
        vn = ADAM_B2 * vv + (1.0 - ADAM_B2) * (gv * gv)
        delta = -ADAM_LR * ((mn * c1) / (jnp.sqrt(vn * c2) + ADAM_EPS) + ADAM_WD * wv)
        return delta, mn, vn
    return _rowwise(fn, [w, g, m, v], [], [(cols, F32)] * 3, [], name=name, tm=tm)


def _my_pos():
    return lax.axis_index("x"), lax.axis_index("y"), lax.axis_index("c")


def _allgather8(x, name):
    r, c = x.shape

    def body(x_ref, out_ref, send_sems, recv_sems, local_sem):
        mx, my, mc = _my_pos()
        me = 4 * mx + 2 * my + mc
        mine = pltpu.make_async_copy(x_ref, out_ref.at[me], local_sem)
        mine.start()
        copies = []
        for k in range(1, 8):
            fx, fy, fc = (k >> 2) & 1, (k >> 1) & 1, k & 1
            px = 1 - mx if fx else mx
            py = 1 - my if fy else my
            pc = 1 - mc if fc else mc
            peer = 4 * px + 2 * py + pc
            send = pltpu.make_async_remote_copy(src_ref=x_ref, dst_ref=out_ref.at[me], send_sem=send_sems.at[k - 1],
                                                recv_sem=recv_sems.at[k - 1], device_id=(px, py, pc),
                                                device_id_type=MESH)
            send.start()
            recv = pltpu.make_async_remote_copy(src_ref=x_ref, dst_ref=out_ref.at[peer], send_sem=send_sems.at[k - 1],
                                                recv_sem=recv_sems.at[k - 1], device_id=(px, py, pc),
                                                device_id_type=MESH)
            copies.append((send, recv))
        for send, recv in copies:
            recv.wait_recv()
        for send, recv in copies:
            send.wait_send()
        mine.wait()

    vm = pl.BlockSpec(memory_space=pltpu.VMEM)
    return _pcall(body, name=name, in_specs=[vm], out_specs=vm, out_shape=_sds((8, r, c), x.dtype),
                  scratch=[pltpu.SemaphoreType.DMA((7,)), pltpu.SemaphoreType.DMA((7,)), pltpu.SemaphoreType.DMA])(x)


def _chip_flips(mx, my):
    out = []
    for fx, fy in ((1, 0), (0, 1), (1, 1)):
        px = 1 - mx if fx else mx
        py = 1 - my if fy else my
        out.append((px, py, 2 * px + py))
    return out


def _gather_chips(shard2, name):
    _, h, c = shard2.shape

    def body(x_ref, out_ref, send_sems, recv_sems):
        mx, my, mc = _my_pos()
        oc = 1 - mc
        mk = 2 * mx + my
        flips = _chip_flips(mx, my)

        def copy(k, src, dst, to):
            return pltpu.make_async_remote_copy(src_ref=src, dst_ref=dst, send_sem=send_sems.at[k],
                                                recv_sem=recv_sems.at[k], device_id=to, device_id_type=MESH)

        first = [copy(j, x_ref.at[mc], out_ref.at[mk, mc], (px, py, mc)) for j, (px, py, pk) in enumerate(flips)]
        for cp in first:
            cp.start()
        passed = []
        for j, (px, py, pk) in enumerate(flips):
            copy(j, x_ref.at[mc], out_ref.at[pk, mc], (px, py, mc)).wait_recv()
            fw = copy(3 + j, out_ref.at[pk, mc], out_ref.at[pk, mc], (mx, my, oc))
            fw.start()
            passed.append(fw)
        for j, (px, py, pk) in enumerate(flips):
            copy(3 + j, out_ref.at[pk, oc], out_ref.at[pk, oc], (mx, my, oc)).wait_recv()
        for cp in first + passed:
            cp.wait_send()

    return _pcall(body, name=name, in_specs=[HBM_SPEC], out_specs=HBM_SPEC, out_shape=_sds((4, 2, h, c), shard2.dtype),
                  scratch=[pltpu.SemaphoreType.DMA((6,)), pltpu.SemaphoreType.DMA((6,))])(shard2)


def _pair_exchange(g4, name):
    n, _, h, c = g4.shape

    def body(g_ref, out_ref, send_sem, recv_sem):
        mx, my, mc = _my_pos()
        oc = 1 - mc
        copies = []
        for k in range(n):
            cp = pltpu.make_async_remote_copy(src_ref=g_ref.at[k, oc], dst_ref=out_ref.at[k], send_sem=send_sem.at[k],
                                              recv_sem=recv_sem.at[k], device_id=(mx, my, oc), device_id_type=MESH)
            cp.start()
            copies.append(cp)
        for cp in copies:
            cp.wait_recv()
        for cp in copies:
            cp.wait_send()

    return _pcall(body, name=name, in_specs=[HBM_SPEC], out_specs=HBM_SPEC, out_shape=_sds((n, h, c), g4.dtype),
                  scratch=[pltpu.SemaphoreType.DMA((n,)), pltpu.SemaphoreType.DMA((n,))])(g4)


def _pair_add(g4, recv, core, name):
    n, _, h, c = g4.shape
    tm = _row_tile(h)

    def body(core_ref, a_ref, b_ref, o_ref, o16_ref):
        acc = a_ref[...] + b_ref[...]
        o_ref[...] = acc
        o16_ref[...] = acc.astype(BF16)

    out_spec = pl.BlockSpec((None, tm, c), lambda k, i, cr: (k, i, 0))
    return _pcall(body, name=name, grid=(n, h // tm), prefetch=1,
                  in_specs=[pl.BlockSpec((None, None, tm, c), lambda k, i, cr: (k, cr[0], i, 0)), out_spec],
                  out_specs=[out_spec, out_spec], out_shape=[_sds((n, h, c), F32), _sds((n, h, c), BF16)],
                  sem=("parallel", "parallel"))(core, g4, recv)


def _chip_exchange(p, name):
    n, h, c = p.shape

    def body(p_ref, out_ref, send_sems, recv_sems):
        mx, my, mc = _my_pos()
        copies = []
        for j, (px, py, pk) in enumerate(_chip_flips(mx, my)):
            cp = pltpu.make_async_remote_copy(src_ref=p_ref.at[pk], dst_ref=out_ref.at[j], send_sem=send_sems.at[j],
                                              recv_sem=recv_sems.at[j], device_id=(px, py, mc), device_id_type=MESH)
            cp.start()
            copies.append(cp)
        for cp in copies:
            cp.wait_recv()
        for cp in copies:
            cp.wait_send()

    return _pcall(body, name=name, in_specs=[HBM_SPEC], out_specs=HBM_SPEC, out_shape=_sds((3, h, c), p.dtype),
                  scratch=[pltpu.SemaphoreType.DMA((3,)), pltpu.SemaphoreType.DMA((3,))])(p)


def _chip_sum(p, slots, chip, name):
    _, h, c = p.shape
    tm = _row_tile(h)

    def body(chip_ref, p_ref, q_ref, o_ref):
        o_ref[...] = ((p_ref[...] + q_ref[0].astype(F32)) + q_ref[1].astype(F32)) + q_ref[2].astype(F32)

    return _pcall(body, name=name, grid=(h // tm,), prefetch=1,
                  in_specs=[pl.BlockSpec((None, tm, c), lambda i, ch: (ch[0], i, 0)),
                            pl.BlockSpec((3, tm, c), lambda i, ch: (0, i, 0))],
                  out_specs=pl.BlockSpec((tm, c), lambda i, ch: (i, 0)),
                  out_shape=_sds((h, c), F32), sem=("parallel",))(chip, p, slots)


def _sum_slots(q, name):
    n, h, c = q.shape
    tm = _row_tile(h)

    def body(q_ref, o_ref):
        acc = q_ref[0]
        for k in range(1, n):
            acc = acc + q_ref[k]
        o_ref[...] = acc

    return _pcall(body, name=name, grid=(h // tm,),
                  in_specs=[pl.BlockSpec((n, tm, c), lambda i: (0, i, 0))],
                  out_specs=pl.BlockSpec((tm, c), lambda i: (i, 0)),
                  out_shape=_sds((h, c), F32), sem=("parallel",))(q)


def _pair_share(f, name):
    h, c = f.shape

    def body(f_ref, out_ref, send_sem, recv_sem):
        mx, my, mc = _my_pos()
        cp = pltpu.make_async_remote_copy(src_ref=f_ref, dst_ref=out_ref, send_sem=send_sem, recv_sem=recv_sem,
                                          device_id=(mx, my, 1 - mc), device_id_type=MESH)
        cp.start()
        cp.wait_recv()
        cp.wait_send()

    return _pcall(body, name=name, in_specs=[HBM_SPEC], out_specs=HBM_SPEC, out_shape=_sds((h, c), f.dtype),
                  scratch=[pltpu.SemaphoreType.DMA, pltpu.SemaphoreType.DMA])(f)


BIG = ("mlp_w1", "mlp_w2", "fox_w_in", "fox_w_o", "ssm_w_in", "ssm_w_out")
SMALL_SHARDED = ("ssm_conv_w", "ssm_conv_b", "ssm_norm_w")
PACK_COLS = 1024


def _pack_rows(parts, rows_multiple, dtype):
    flat = jnp.concatenate([p.reshape(-1).astype(dtype) for p in parts])
    unit = rows_multiple * PACK_COLS
    total = -(-flat.shape[0] // unit) * unit
    flat = jnp.pad(flat, (0, total - flat.shape[0]))
    return flat.reshape(total // PACK_COLS, PACK_COLS)


def _unpack(flat, shapes):
    out, off = [], 0
    for sh in shapes:
        n = 1
        for d_ in sh:
            n *= d_
        out.append(flat[off:off + n].reshape(sh))
        off += n
    return out


PIECE_ROWS = 16


def _piece_rows(shape):
    n = 1
    for d_ in shape:
        n *= d_
    rows = -(-n // PACK_COLS)
    return n, -(-rows // PIECE_ROWS) * PIECE_ROWS


def _pack2d(parts, rows_multiple, dtype):
    blocks = []
    for p in parts:
        n, rows = _piece_rows(p.shape)
        a = p.astype(dtype)
        if p.shape[-1] != PACK_COLS or n % PACK_COLS:
            a = jnp.pad(a.reshape(-1), (0, -n % PACK_COLS))
        a = a.reshape(-1, PACK_COLS)
        blocks.append(jnp.pad(a, ((0, rows - a.shape[0]), (0, 0))))
    total = sum(b.shape[0] for b in blocks)
    pad = -total % rows_multiple
    if pad:
        blocks.append(jnp.zeros((pad, PACK_COLS), dtype))
    return jnp.concatenate(blocks, axis=0)


def _unpack2d(buf, shapes):
    out, off = [], 0
    for sh in shapes:
        n, rows = _piece_rows(sh)
        piece = buf[off:off + rows]
        if sh[-1] == PACK_COLS and n % PACK_COLS == 0:
            out.append(piece[:n // PACK_COLS].reshape(sh))
        else:
            out.append(piece.reshape(-1)[:n].reshape(sh))
        off += rows
    return out


def _row_tile(h, cap=512):
    for step in (16, 8):
        best = 0
        for cand in range(step, cap + 1, step):
            if h % cand == 0:
                best = cand
        if best:
            return best
    return h


def _chip_slice(full, axis, k, width):
    idx = [slice(None)] * full.ndim
    idx[axis] = slice(k * width, (k + 1) * width)
    return full[tuple(idx)]


SHARD_AXIS = dict(mlp_w1=2, mlp_w2=1, fox_w_in=2, fox_w_o=1, ssm_w_in=2, ssm_w_out=1, ssm_conv_w=2,
                  ssm_conv_b=1, ssm_norm_w=1, ada_w=2)


def kernel(x, c, ada_w, ada_b, ln_mix_g, ln_mix_b, ln_mlp_g, ln_mlp_b, mlp_w1, mlp_w2, fox_w_in, fox_b_f, fox_w_o, ssm_w_in, ssm_conv_w, ssm_conv_b, ssm_dt_bias, ssm_a_log, ssm_d, ssm_norm_w, ssm_w_out, loss_target, m_ada_w, m_ada_b, m_ln_mix_g, m_ln_mix_b, m_ln_mlp_g, m_ln_mlp_b, m_mlp_w1, m_mlp_w2, m_fox_w_in, m_fox_b_f, m_fox_w_o, m_ssm_w_in, m_ssm_conv_w, m_ssm_conv_b, m_ssm_dt_bias, m_ssm_a_log, m_ssm_d, m_ssm_norm_w, m_ssm_w_out, v_ada_w, v_ada_b, v_ln_mix_g, v_ln_mix_b, v_ln_mlp_g, v_ln_mlp_b, v_mlp_w1, v_mlp_w2, v_fox_w_in, v_fox_b_f, v_fox_w_o, v_ssm_w_in, v_ssm_conv_w, v_ssm_conv_b, v_ssm_dt_bias, v_ssm_a_log, v_ssm_d, v_ssm_norm_w, v_ssm_w_out):
    names = ("ada_w", "ada_b", "ln_mix_g", "ln_mix_b", "ln_mlp_g", "ln_mlp_b", "mlp_w1", "mlp_w2", "fox_w_in",
             "fox_b_f", "fox_w_o", "ssm_w_in", "ssm_conv_w", "ssm_conv_b", "ssm_dt_bias", "ssm_a_log", "ssm_d",
             "ssm_norm_w", "ssm_w_out")
    weights = dict(zip(names, (ada_w, ada_b, ln_mix_g, ln_mix_b, ln_mlp_g, ln_mlp_b, mlp_w1, mlp_w2, fox_w_in,
                               fox_b_f, fox_w_o, ssm_w_in, ssm_conv_w, ssm_conv_b, ssm_dt_bias, ssm_a_log, ssm_d,
                               ssm_norm_w, ssm_w_out)))
    m_in = dict(zip(names, (m_ada_w, m_ada_b, m_ln_mix_g, m_ln_mix_b, m_ln_mlp_g, m_ln_mlp_b, m_mlp_w1, m_mlp_w2,
                            m_fox_w_in, m_fox_b_f, m_fox_w_o, m_ssm_w_in, m_ssm_conv_w, m_ssm_conv_b, m_ssm_dt_bias,
                            m_ssm_a_log, m_ssm_d, m_ssm_norm_w, m_ssm_w_out)))
    v_in = dict(zip(names, (v_ada_w, v_ada_b, v_ln_mix_g, v_ln_mix_b, v_ln_mlp_g, v_ln_mlp_b, v_mlp_w1, v_mlp_w2,
                            v_fox_w_in, v_fox_b_f, v_fox_w_o, v_ssm_w_in, v_ssm_conv_w, v_ssm_conv_b, v_ssm_dt_bias,
                            v_ssm_a_log, v_ssm_d, v_ssm_norm_w, v_ssm_w_out)))

    mx, my, mc = _my_pos()
    chip = 2 * mx + my
    me = 4 * mx + 2 * my + mc
    x0 = x[0]
    target = loss_target[0]
    s, d = x0.shape
    n_qkv = 3 * FOX_HEADS * HEAD_DIM

    big_shapes = [weights[n].shape for n in BIG]
    packed = _pack2d([weights[n] for n in BIG], 32, BF16)
    gathered = _gather_chips(packed.reshape(2, packed.shape[0] // 2, PACK_COLS), "gather_weights")
    gathered = gathered.reshape(4, packed.shape[0], PACK_COLS)
    gathered = lax.dynamic_update_slice(gathered, packed[None], (chip, 0, 0))
    per_chip = [_unpack2d(gathered[k], big_shapes) for k in range(4)]
    full = {n: jnp.concatenate([per_chip[k][i] for k in range(4)], axis=SHARD_AXIS[n]) for i, n in enumerate(BIG)}

    small_shapes = [weights[n].shape for n in SMALL_SHARDED]
    small_packed = _pack_rows([weights[n] for n in SMALL_SHARDED] + [c], 8, F32).reshape(-1, LANES)
    small_all = _allgather8(small_packed, "gather_small")
    small_chip = [_unpack(small_all[2 * k].reshape(-1), small_shapes) for k in range(4)]
    small_full = {n: jnp.concatenate([small_chip[k][i] for k in range(4)], axis=SHARD_AXIS[n])
                  for i, n in enumerate(SMALL_SHARDED)}
    n_small = sum(weights[n].size for n in SMALL_SHARDED)
    c_all = small_all.reshape(8, -1)[:, n_small:n_small + d]

    cols = ada_w.shape[2]
    ada_b_cols = lax.dynamic_slice_in_dim(ada_b, chip * cols, cols, axis=1)[:, None, :]
    c16 = jnp.pad(c_all, ((0, 8), (0, 0)))
    mod_part = _ada_forward(c16, ada_w, ada_b_cols, "ada_fwd")[:, :8, :]
    mod_all = _allgather8(mod_part.reshape(-1, LANES), "gather_mod").reshape(8, DEPTH, 8, cols)
    mod_mine = jnp.stack([lax.dynamic_index_in_dim(mod_all[2 * k], me, axis=1, keepdims=False) for k in range(4)], axis=1)
    mods = [jnp.pad(mod_mine[i].reshape(6, d), ((0, 2), (0, 0))) for i in range(DEPTH)]

    w = dict(
        fox_qkv=full["fox_w_in"][0][:, :n_qkv],
        fox_f=jnp.pad(full["fox_w_in"][0][:, n_qkv:], ((0, 0), (0, LANES - FOX_HEADS))),
        fox_o=full["fox_w_o"][0],
        fox_bf=jnp.pad(fox_b_f, ((0, 0), (0, LANES - FOX_HEADS))),
        head_ind=jnp.asarray(np.arange(d)[:, None] // HEAD_DIM == np.arange(LANES)[None, :], BF16),
    )
    w.update(_ssm_weights(full["ssm_w_in"][0], small_full["ssm_conv_w"][0], small_full["ssm_conv_b"], ssm_dt_bias,
                          ssm_a_log, ssm_d, small_full["ssm_norm_w"], full["ssm_w_out"][0]))
    mixers = ((_fox_forward, _fox_backward), (_ssd_forward, _ssd_backward))

    saved = []
    xin = x0
    u = _modulate_in(x0, mods[0], "l0_mod_in")
    for i in range(DEPTH):
        tag = "l%d" % i
        y, sv = mixers[i % 2][0](u, w)
        r, x1, u2 = _res_ln_mod(xin, y, mods[i], ln_mix_g[i:i + 1], ln_mix_b[i:i + 1], tag + "_res_ln1")
        y2, hr, a = _mlp_forward(u2, full["mlp_w1"][i], full["mlp_w2"][i], tag)
        if i + 1 < DEPTH:
            r2, xin, u = _res_ln_next(x1, y2, mods[i], ln_mlp_g[i:i + 1], ln_mlp_b[i:i + 1], mods[i + 1],
                                      tag + "_res_ln2")
        else:
            r2 = None
        saved.append(dict(y=y, r=r, u2=u2, hr=hr, a=a, y2=y2, r2=r2, x1=x1, mix=sv))

    grads = {}
    dmod_parts = [dict() for _ in range(DEPTH)]
    ln_grads = {n: [None] * DEPTH for n in ("ln_mix_g", "ln_mix_b", "ln_mlp_g", "ln_mlp_b")}
    g_w1, g_w2 = [None] * DEPTH, [None] * DEPTH
    du = dres0 = None
    for i in reversed(range(DEPTH)):
        tag = "l%d" % i
        sv = saved[i]
        if i + 1 == DEPTH:
            dy2, dres, sq, dg2, db2, dgm = _loss_ln2_bwd(sv["x1"], sv["y2"], target, mods[i], ln_mlp_g[i:i + 1],
                                                         ln_mlp_b[i:i + 1], "loss_ln2_bwd")
            loss = lax.psum(0.5 * jnp.sum(sq) / d, ("x", "y", "c"))
        else:
            dy2, dres, dsca, dsha, dg2, db2, dgm = _mod_in_ln2_bwd(du, dres0, sv["r2"], sv["y2"], mods[i],
                                                                   ln_mlp_g[i:i + 1], ln_mlp_b[i:i + 1], mods[i + 1],
                                                                   tag + "_ln2_bwd")
            dmod_parts[i + 1].update(sc_a=dsca, sh_a=dsha)
        du2, g_w1[i], g_w2[i] = _mlp_backward(dy2, sv["u2"], sv["hr"], sv["a"], full["mlp_w1"][i], full["mlp_w2"][i], tag)
        dy, dres0, dscm, dshm, dg1, db1, dga = _ln1_bwd(du2, dres, sv["r"], sv["y"], mods[i], ln_mix_g[i:i + 1],
                                                        ln_mix_b[i:i + 1], tag + "_ln1_bwd")
        du, mg = mixers[i % 2][1](dy, sv["mix"], w)
        grads.update(mg)
        dmod_parts[i].update(g_a=dga, sh_m=dshm, sc_m=dscm, g_m=dgm)
        ln_grads["ln_mix_g"][i], ln_grads["ln_mix_b"][i] = dg1, db1
        ln_grads["ln_mlp_g"][i], ln_grads["ln_mlp_b"][i] = dg2, db2
    dx, dsca, dsha = _mod_in_bwd(du, dres0, x0, mods[0], "l0_mod_in_bwd")
    dmod_parts[0].update(sc_a=dsca, sh_a=dsha)
    dmods = [jnp.concatenate([p["sh_a"], p["sc_a"], p["g_a"], p["sh_m"], p["sc_m"], p["g_m"]], axis=1)
             for p in dmod_parts]
    grad_x = dx[None]
    grads["mlp_w1"] = jnp.stack(g_w1)
    grads["mlp_w2"] = jnp.stack(g_w2)
    for n in ("fox_w_in", "fox_w_o", "ssm_w_in", "ssm_w_out", "ssm_conv_w"):
        grads[n] = grads[n][None]

    small_names = ("ln_mix_g", "ln_mix_b", "ln_mlp_g", "ln_mlp_b", "fox_b_f", "ssm_dt_bias", "ssm_a_log", "ssm_d")
    small_parts = list(dmods)
    for n in small_names[:4]:
        small_parts.append(jnp.concatenate(ln_grads[n], axis=0))
    for n in small_names[4:]:
        small_parts.append(jnp.pad(grads[n], ((0, 0), (0, LANES - grads[n].shape[1]))))
    small_vec = _pack_rows(small_parts, 1, F32).reshape(-1, LANES)
    small_vec = jnp.pad(small_vec, ((0, -small_vec.shape[0] % 8), (0, 0)))
    small_g_all = _allgather8(small_vec, "gather_small_grads")
    small_sum = _sum_slots(small_g_all, "sum_small_grads").reshape(-1)
    dmod_sum = small_sum[:DEPTH * 6 * d].reshape(DEPTH, 6 * d)
    off = DEPTH * 6 * d
    final = {"ada_b": dmod_sum}
    for n in small_names[:4]:
        final[n] = small_sum[off:off + DEPTH * d].reshape(DEPTH, d)
        off += DEPTH * d
    for n in small_names[4:]:
        width = weights[n].shape[1]
        final[n] = small_sum[off:off + width].reshape(1, width)
        off += LANES

    dmod_all = small_g_all.reshape(8, -1)[:, :DEPTH * 6 * d].reshape(8, DEPTH, 6 * d)
    dmod_cols = lax.dynamic_slice_in_dim(dmod_all, chip * cols, cols, axis=2).transpose(1, 0, 2)
    final["ada_w"] = _ada_backward(c_all.T, dmod_cols, "ada_bwd")

    sharded = BIG + SMALL_SHARDED
    shard_shapes = [weights[n].shape for n in sharded]
    per_target = []
    for k in range(4):
        parts = [_chip_slice(grads[n], SHARD_AXIS[n], k, weights[n].shape[SHARD_AXIS[n]]) for n in sharded]
        per_target.append(_pack2d(parts, 128, F32))
    g_all = jnp.stack(per_target)
    rows = g_all.shape[1]
    g4 = g_all.reshape(4, 2, rows // 2, PACK_COLS)
    recv = _pair_exchange(g4, "rs_pair_exchange")
    part, part16 = _pair_add(g4, recv, jnp.reshape(mc, (1,)).astype(jnp.int32), "rs_pair_add")
    slots = _chip_exchange(part16, "rs_chip_exchange")
    half = _chip_sum(part, slots, jnp.reshape(chip, (1,)).astype(jnp.int32), "rs_chip_sum")
    other = _pair_share(half, "rs_pair_share")
    both = jnp.concatenate([jnp.where(mc == 0, half, other), jnp.where(mc == 0, other, half)], axis=0)
    for n, g_shard in zip(sharded, _unpack2d(both, shard_shapes)):
        final[n] = g_shard

    outs_g, outs_d, outs_m, outs_v = [], [], [], []
    for n in names:
        wv = weights[n]
        two_d = (-1, wv.shape[-1])
        delta, mn, vn = _adamw(wv.reshape(two_d), final[n].reshape(two_d), m_in[n].reshape(two_d),
                               v_in[n].reshape(two_d), "adamw_" + n)
        outs_g.append(final[n].reshape(wv.shape))
        outs_d.append(delta.reshape(wv.shape))
        outs_m.append(mn.reshape(wv.shape))
        outs_v.append(vn.reshape(wv.shape))
    return (loss, grad_x, *outs_g, *outs_d, *outs_m, *outs_v)
```

```python
import functools

import jax
import jax.numpy as jnp
import numpy as np
from jax import lax
from jax.experimental import pallas as pl
from jax.experimental.pallas import tpu as pltpu

F32, BF16 = jnp.float32, jnp.bfloat16
MESH = pl.DeviceIdType.MESH
HBM_SPEC = pl.BlockSpec(memory_space=pltpu.HBM)

VMEM_LIMIT_BYTES = 52 * 2**20
LANES = 128

FOX_HEADS, HEAD_DIM = 16, 64
SSM_HEADS, SSM_GROUPS, SSM_STATE, SSM_CHUNK, SSM_CONV = 32, 8, 128, 128, 4
SSM_GROUP_WIDTH = 256
LN_EPS, RMS_EPS = 1e-5, 1e-5
DEPTH = 2
ALPHA = (2.0 * DEPTH) ** 0.25
ADAM_LR, ADAM_B1, ADAM_B2, ADAM_EPS, ADAM_WD, ADAM_STEP = 0.001, 0.9, 0.999, 1e-08, 0.01, 10

ATT_TILE = 512
ROW_TILE = 512
SCAN_TILE = 512
CONV_CHUNK = 64
SSD_GROUPS_PER_STEP = 8
MM_TM, MM_TN, MM_TK = 1024, 1024, 1024

NT_DIMS = (((1,), (1,)), ((), ()))
TN_DIMS = (((0,), (0,)), ((), ()))
NN_DIMS = (((1,), (0,)), ((), ()))


def _pcall(body, *, name, out_shape, grid=(), in_specs=None, out_specs=None, scratch=(), sem=None, prefetch=0):
    params = dict(vmem_limit_bytes=VMEM_LIMIT_BYTES)
    if sem is not None:
        params["dimension_semantics"] = sem
    if prefetch:
        grid_spec = pltpu.PrefetchScalarGridSpec(num_scalar_prefetch=prefetch, grid=grid, in_specs=in_specs,
                                                 out_specs=out_specs, scratch_shapes=scratch)
        return pl.pallas_call(body, out_shape=out_shape, grid_spec=grid_spec, name=name,
                              compiler_params=pltpu.CompilerParams(**params))
    kwargs = {}
    if in_specs is not None:
        kwargs["in_specs"] = in_specs
    if out_specs is not None:
        kwargs["out_specs"] = out_specs
    return pl.pallas_call(body, out_shape=out_shape, grid=grid, scratch_shapes=scratch, name=name,
                          compiler_params=pltpu.CompilerParams(**params), **kwargs)


def _sds(shape, dtype):
    return jax.ShapeDtypeStruct(tuple(shape), dtype)


def _dot(a, b, dims=NN_DIMS):
    return lax.dot_general(a, b, dims, preferred_element_type=F32)


def _sigmoid(x):
    return 1.0 / (1.0 + jnp.exp(-x))


def _silu(x):
    return x * _sigmoid(x)


def _dsilu(x):
    s = _sigmoid(x)
    return s * (1.0 + x * (1.0 - s))


def _dot_split(x, m16, dims=NN_DIMS, passes=3):
    hi = x.astype(BF16)
    r1 = x - hi.astype(F32)
    mid = r1.astype(BF16)
    out = _dot(hi, m16, dims) + _dot(mid, m16, dims)
    if passes == 3:
        lo = (r1 - mid.astype(F32)).astype(BF16)
        out = out + _dot(lo, m16, dims)
    return out


def _mm(a, b, dims, outs, *, name, tm=MM_TM, tn=MM_TN, tk=MM_TK, epi=None, extra=()):
    if dims == "nn":
        (m, k), (k2, n) = a.shape, b.shape
    elif dims == "nt":
        (m, k), (n, k2) = a.shape, b.shape
    else:
        (k, m), (k2, n) = a.shape, b.shape
    assert k == k2, (a.shape, b.shape, dims)
    tm, tn, tk = min(tm, m), min(tn, n), min(tk, k)
    assert m % tm == 0 and n % tn == 0 and k % tk == 0, (m, n, k, tm, tn, tk)
    nk = k // tk
    dn = {"nn": NN_DIMS, "nt": NT_DIMS, "tn": TN_DIMS}[dims]
    n_extra, n_out = len(extra), len(outs)
    if epi is None:
        epi = lambda acc: (acc,) * n_out

    def body(a_ref, b_ref, *rest):
        extra_refs, out_refs, acc_ref = rest[:n_extra], rest[n_extra:n_extra + n_out], rest[-1]
        kk = pl.program_id(2)

        @pl.when(kk == 0)
        def _():
            acc_ref[...] = jnp.zeros_like(acc_ref)

        acc_ref[...] += _dot(a_ref[...].astype(BF16), b_ref[...].astype(BF16), dn)

        @pl.when(kk == nk - 1)
        def _():
            res = epi(acc_ref[...], *[e[...] for e in extra_refs])
            for o, r in zip(out_refs, res):
                o[...] = r.astype(o.dtype)

    if dims == "tn":
        a_spec = pl.BlockSpec((tk, tm), lambda i, j, kk: (kk, i))
    else:
        a_spec = pl.BlockSpec((tm, tk), lambda i, j, kk: (i, kk))
    if dims == "nt":
        b_spec = pl.BlockSpec((tn, tk), lambda i, j, kk: (j, kk))
    else:
        b_spec = pl.BlockSpec((tk, tn), lambda i, j, kk: (kk, j))
    o_spec = pl.BlockSpec((tm, tn), lambda i, j, kk: (i, j))
    res = _pcall(body, name=name, grid=(m // tm, n // tn, nk),
                 in_specs=[a_spec, b_spec] + [o_spec] * n_extra,
                 out_specs=[o_spec] * n_out,
                 out_shape=[_sds((m, n), d) for d in outs],
                 scratch=[pltpu.VMEM((tm, tn), F32)],
                 sem=("parallel", "parallel", "arbitrary"))(a, b, *extra)
    return res[0] if n_out == 1 else res


def _k_blocks(a, tk=None):
    tk = tk or MM_TK
    return [(a, kb) for kb in range(a.shape[1] // tk)]


def _mm_nt_blocks(a_blocks, b, start, *, name, tm=MM_TM, tk=None):
    tk = tk or MM_TK
    tm = min(tm, a_blocks[0][0].shape[0])
    m, n, p_n = a_blocks[0][0].shape[0], b.shape[0], len(a_blocks)
    assert b.shape[1] == p_n * tk and m % tm == 0

    def body(*refs):
        a_refs = refs[:p_n]
        b_ref, s_ref, o_ref, acc_ref = refs[p_n:]
        kk = pl.program_id(1)

        @pl.when(kk == 0)
        def _():
            acc_ref[...] = s_ref[...]
        for p in range(p_n):
            @pl.when(kk == p)
            def _(p=p):
                acc_ref[...] += _dot(a_refs[p][...].astype(BF16), b_ref[...].astype(BF16), NT_DIMS)

        @pl.when(kk == p_n - 1)
        def _():
            o_ref[...] = acc_ref[...]

    in_specs = [pl.BlockSpec((tm, tk), functools.partial(lambda kb, i, kk: (i, kb), kb)) for _, kb in a_blocks]
    in_specs += [pl.BlockSpec((n, tk), lambda i, kk: (0, kk)), pl.BlockSpec((tm, n), lambda i, kk: (i, 0))]
    return _pcall(body, name=name, grid=(m // tm, p_n), in_specs=in_specs,
                  out_specs=pl.BlockSpec((tm, n), lambda i, kk: (i, 0)), out_shape=_sds((m, n), F32),
                  scratch=[pltpu.VMEM((tm, n), F32)], sem=("parallel", "arbitrary"))(*[a for a, _ in a_blocks], b, start)


def _rowwise(fn, rows, consts, row_outs, acc_outs, *, name, tm=ROW_TILE):
    s = rows[0].shape[0]
    tm = min(tm, s)
    assert s % tm == 0
    n_in, n_o = len(rows) + len(consts), len(row_outs)

    def body(*refs):
        ins, outs = refs[:n_in], refs[n_in:]
        res = fn(*[r[...] for r in ins])
        if not isinstance(res, (tuple, list)):
            res = (res,)
        for o, val in zip(outs[:n_o], res[:n_o]):
            o[...] = val.astype(o.dtype)
        if acc_outs:
            @pl.when(pl.program_id(0) == 0)
            def _():
                for o in outs[n_o:]:
                    o[...] = jnp.zeros_like(o)
            for o, val in zip(outs[n_o:], res[n_o:]):
                o[...] += val

    in_specs = [pl.BlockSpec((tm, r.shape[1]), lambda i: (i, 0)) for r in rows]
    in_specs += [pl.BlockSpec(c.shape, functools.partial(lambda nd, i: (0,) * nd, c.ndim)) for c in consts]
    out_specs = [pl.BlockSpec((tm, c), lambda i: (i, 0)) for c, _ in row_outs]
    out_specs += [pl.BlockSpec(tuple(sh), lambda i: (0, 0)) for sh in acc_outs]
    out_shape = [_sds((s, c), d) for c, d in row_outs] + [_sds(sh, F32) for sh in acc_outs]
    res = _pcall(body, name=name, grid=(s // tm,), in_specs=in_specs, out_specs=out_specs,
                 out_shape=out_shape, sem=("arbitrary",))(*rows, *consts)
    return res


def _colsum(x):
    return jnp.sum(x, axis=0, keepdims=True)


def _ln_stats(r):
    mu = jnp.mean(r, axis=-1, keepdims=True)
    xc = r - mu
    var = jnp.mean(xc * xc, axis=-1, keepdims=True)
    rstd = lax.rsqrt(var + LN_EPS)
    return xc * rstd, rstd


def _ln_bwd(dy, xhat, rstd, gamma):
    dyg = dy * gamma
    m1 = jnp.mean(dyg, axis=-1, keepdims=True)
    m2 = jnp.mean(dyg * xhat, axis=-1, keepdims=True)
    return rstd * (dyg - m1 - xhat * m2)


def _modulate_in(x, mod, name):
    def fn(xv, m):
        return (xv * (1.0 + m[1:2]) + m[0:1],)
    return _rowwise(fn, [x], [mod], [(x.shape[1], BF16)], [], name=name)[0]


def _res_ln_mod(x, y, mod, g, b, name):
    d = x.shape[1]

    def fn(xv, yv, m, gv, bv):
        r = ALPHA * xv + (1.0 + m[2:3]) * yv
        xhat, _ = _ln_stats(r)
        x1 = xhat * gv + bv
        u2 = x1 * (1.0 + m[4:5]) + m[3:4]
        return r, x1, u2
    return _rowwise(fn, [x, y], [mod, g, b], [(d, F32), (d, F32), (d, BF16)], [], name=name)


def _res_ln_next(x, y, mod, g, b, mod_next, name):
    d = x.shape[1]

    def fn(xv, yv, m, gv, bv, mn):
        r = ALPHA * xv + (1.0 + m[5:6]) * yv
        xhat, _ = _ln_stats(r)
        out = xhat * gv + bv
        return r, out, out * (1.0 + mn[1:2]) + mn[0:1]
    return _rowwise(fn, [x, y], [mod, g, b, mod_next], [(d, F32), (d, F32), (d, BF16)], [], name=name)


def _loss_ln2_bwd(x1, y2, target, mod, g, b, name):
    d = x1.shape[1]

    def fn(xv, yv, tv, m, gv, bv):
        rv = ALPHA * xv + (1.0 + m[5:6]) * yv
        xhat, rstd = _ln_stats(rv)
        e = xhat * gv + bv - tv
        dxv = e * (1.0 / d)
        dr = _ln_bwd(dxv, xhat, rstd, gv)
        return (dr * (1.0 + m[5:6]), ALPHA * dr,
                _colsum(e * e), _colsum(dxv * xhat), _colsum(dxv), _colsum(dr * yv))
    return _rowwise(fn, [x1, y2, target], [mod, g, b], [(d, BF16), (d, F32)], [(1, d)] * 4, name=name)


def _mod_in_ln2_bwd(du, dres, r2, y2, mod, g, b, mod_next, name):
    d = du.shape[1]

    def fn(duv, drv, rv, yv, m, gv, bv, mn):
        xhat, rstd = _ln_stats(rv)
        xout = xhat * gv + bv
        dxv = duv * (1.0 + mn[1:2]) + drv
        dr = _ln_bwd(dxv, xhat, rstd, gv)
        return (dr * (1.0 + m[5:6]), ALPHA * dr,
                _colsum(duv * xout), _colsum(duv), _colsum(dxv * xhat), _colsum(dxv), _colsum(dr * yv))
    return _rowwise(fn, [du, dres, r2, y2], [mod, g, b, mod_next], [(d, BF16), (d, F32)], [(1, d)] * 5, name=name)


def _ln1_bwd(du2, dres, r, y, mod, g, b, name):
    d = du2.shape[1]

    def fn(duv, drv, rv, yv, m, gv, bv):
        xhat, rstd = _ln_stats(rv)
        x1 = xhat * gv + bv
        dx1 = duv * (1.0 + m[4:5]) + drv
        dr = _ln_bwd(dx1, xhat, rstd, gv)
        return (dr * (1.0 + m[2:3]), ALPHA * dr,
                _colsum(duv * x1), _colsum(duv), _colsum(dx1 * xhat), _colsum(dx1), _colsum(dr * yv))
    return _rowwise(fn, [du2, dres, r, y], [mod, g, b], [(d, BF16), (d, F32)], [(1, d)] * 5, name=name)


def _mod_in_bwd(du, dres, x, mod, name):
    d = du.shape[1]

    def fn(duv, drv, xv, m):
        return duv * (1.0 + m[1:2]) + drv, _colsum(duv * xv), _colsum(duv)
    return _rowwise(fn, [du, dres, x], [mod], [(d, F32)], [(1, d)] * 2, name=name)


def _fox_gate(fraw, b_pad, name):
    s = fraw.shape[0]
    tb = min(SCAN_TILE, s)

    def body(f_ref, b_ref, cum_ref, rows_ref, carry):
        @pl.when(pl.program_id(0) == 0)
        def _():
            carry[...] = jnp.zeros_like(carry)
        z = f_ref[...] + b_ref[...]
        lf = jnp.minimum(z, 0.0) - jnp.log(1.0 + jnp.exp(-jnp.abs(z)))
        lane = lax.broadcasted_iota(jnp.int32, (tb, LANES), 1)
        row = lax.broadcasted_iota(jnp.int32, (tb, LANES), 0)
        c = jnp.where(lane < FOX_HEADS, lf, 0.0)
        sh = 1
        while sh < tb:
            c = c + jnp.where(row >= sh, pltpu.roll(c, sh, 0), 0.0)
            sh *= 2
        c = c + carry[0:1, :]
        cum_ref[...] = c
        rows_ref[...] = c.T[0:FOX_HEADS, :]
        carry[0:1, :] = c[tb - 1:tb, :]

    return _pcall(body, name=name, grid=(s // tb,),
                  in_specs=[pl.BlockSpec((tb, LANES), lambda i: (i, 0)), pl.BlockSpec((1, LANES), lambda i: (0, 0))],
                  out_specs=[pl.BlockSpec((tb, LANES), lambda i: (i, 0)), pl.BlockSpec((FOX_HEADS, tb), lambda i: (0, i))],
                  out_shape=[_sds((s, LANES), F32), _sds((FOX_HEADS, s), F32)], scratch=[pltpu.VMEM((8, LANES), F32)],
                  sem=("arbitrary",))(fraw, b_pad)


def _fox_gate_bwd(drow, dcol, fraw, b_pad, name):
    s = fraw.shape[0]
    tb = min(SCAN_TILE, s)
    n = s // tb

    def body(dr_ref, dc_ref, f_ref, b_ref, df_ref, db_ref, carry):
        @pl.when(pl.program_id(0) == 0)
        def _():
            carry[...] = jnp.zeros_like(carry)
            db_ref[...] = jnp.zeros_like(db_ref)
        row = lax.broadcasted_iota(jnp.int32, (tb, LANES), 0)
        c = dr_ref[...] + dc_ref[...]
        sh = 1
        while sh < tb:
            c = c + jnp.where(row + sh < tb, pltpu.roll(c, tb - sh, 0), 0.0)
            sh *= 2
        c = c + carry[0:1, :]
        carry[0:1, :] = c[0:1, :]
        z = f_ref[...] + b_ref[...]
        df = c * (1.0 / (1.0 + jnp.exp(z)))
        df_ref[...] = df.astype(df_ref.dtype)
        db_ref[...] += _colsum(df)

    rev = lambda i: (n - 1 - i, 0)
    return _pcall(body, name=name, grid=(n,),
                  in_specs=[pl.BlockSpec((tb, LANES), rev)] * 3 + [pl.BlockSpec((1, LANES), lambda i: (0, 0))],
                  out_specs=[pl.BlockSpec((tb, LANES), rev), pl.BlockSpec((1, LANES), lambda i: (0, 0))],
                  out_shape=[_sds((s, LANES), BF16), _sds((1, LANES), F32)],
                  scratch=[pltpu.VMEM((8, LANES), F32)], sem=("arbitrary",))(drow, dcol, fraw, b_pad)


def _head_pair_masks(t):
    lane = lax.broadcasted_iota(jnp.int32, (t, LANES), 1)
    return lane < HEAD_DIM


def _lane_blocks(x):
    return [x[:, c * LANES:(c + 1) * LANES] for c in range(x.shape[1] // LANES)]


def _sum_list(xs):
    acc = xs[0]
    for x in xs[1:]:
        acc = acc + x
    return acc


def _causal(t, transposed=False):
    ri = lax.broadcasted_iota(jnp.int32, (t, t), 0)
    ci = lax.broadcasted_iota(jnp.int32, (t, t), 1)
    return ci >= ri if transposed else ri >= ci


def _span_mask(r0, r1, c0, c1, transposed=False):
    ri = lax.broadcasted_iota(jnp.int32, (r1 - r0, c1 - c0), 0) + r0
    ci = lax.broadcasted_iota(jnp.int32, (r1 - r0, c1 - c0), 1) + c0
    return ci >= ri if transposed else ri >= ci


def _full_spans(t):
    return ((0, t, 0, t, False),)


def _diagonal_spans(t, transposed=False):
    h = t // 2
    if h % LANES:
        return ((0, t, 0, t, True),)
    if transposed:
        return ((0, h, 0, t, True), (h, t, h, t, True))
    return ((0, h, 0, h, True), (h, t, 0, t, True))


def _flash_fwd(qkv, ck_rows, kb_start, name):
    s = qkv.shape[0]
    t = min(ATT_TILE, s)
    nq = s // t
    scale = HEAD_DIM ** -0.5
    hp_blocks = FOX_HEADS // 2

    def body(ks_ref, q_ref, k_ref, v_ref, ck_ref, o_ref, lse_ref, lse_rows_ref, acc_ref, m_ref, l_ref):
        hp, qb = pl.program_id(0), pl.program_id(1)
        q2 = q_ref[...] * scale
        first = _head_pair_masks(t)
        zero = jnp.zeros_like(q2)
        qs = (jnp.where(first, q2, zero), jnp.where(first, zero, q2))
        m_ref[...] = jnp.full_like(m_ref, -jnp.inf)
        l_ref[...] = jnp.zeros_like(l_ref)
        acc_ref[...] = jnp.zeros_like(acc_ref)

        def tile(kb, spans):
            off = pl.multiple_of(kb * t, t)
            k2 = k_ref[pl.ds(off, t), :]
            v2 = v_ref[pl.ds(off, t), :]
            ck = ck_ref[kb]
            for r0, r1, c0, c1, masked in spans:
                kk, vv, fr = k2[c0:c1], v2[c0:c1], first[r0:r1]
                pvs, als = [], []
                for j in range(2):
                    sc = _dot(qs[j][r0:r1], kk, NT_DIMS) - ck[j:j + 1, c0:c1]
                    if masked:
                        sc = jnp.where(_span_mask(r0, r1, c0, c1), sc, -jnp.inf)
                    blocks = _lane_blocks(sc)
                    mx = blocks[0]
                    for b in blocks[1:]:
                        mx = jnp.maximum(mx, b)
                    m_old = m_ref[j, r0:r1]
                    m_new = jnp.maximum(m_old, jnp.max(mx, axis=1, keepdims=True))
                    ps = [jnp.exp(b - m_new) for b in blocks]
                    a = jnp.exp(m_old - m_new)
                    l_ref[j, r0:r1] = a * l_ref[j, r0:r1] + _sum_list(ps)
                    m_ref[j, r0:r1] = m_new
                    pvs.append(_dot(jnp.concatenate(ps, axis=1).astype(BF16), vv))
                    als.append(a)
                acc_ref[r0:r1] = jnp.where(fr, als[0], als[1]) * acc_ref[r0:r1] + jnp.where(fr, pvs[0], pvs[1])

        def step(kb, carry):
            tile(kb, _full_spans(t))
            return carry

        lax.fori_loop(ks_ref[hp, qb], qb, step, 0)
        tile(qb, ((0, t, 0, t, True),))
        l0 = jnp.sum(l_ref[0], axis=1, keepdims=True)
        l1 = jnp.sum(l_ref[1], axis=1, keepdims=True)
        o_ref[...] = acc_ref[...] / jnp.where(first, l0, l1)
        for j, lj in enumerate((l0, l1)):
            lse = m_ref[j] + jnp.log(jnp.broadcast_to(lj, (t, LANES)))
            lse_ref[:, j:j + 1] = lse[:, 0:1]
            lse_rows_ref[j:j + 1, :] = lse.T[0:1, :]

    return _pcall(
        body, name=name, grid=(hp_blocks, nq), prefetch=1,
        in_specs=[pl.BlockSpec((t, LANES), lambda h, i, ks: (i, h)),
                  pl.BlockSpec((s, LANES), lambda h, i, ks: (0, hp_blocks + h)),
                  pl.BlockSpec((s, LANES), lambda h, i, ks: (0, 2 * hp_blocks + h)),
                  pl.BlockSpec((None, nq, 2, t), lambda h, i, ks: (h, 0, 0, 0))],
        out_specs=[pl.BlockSpec((t, LANES), lambda h, i, ks: (i, h)),
                   pl.BlockSpec((None, t, 2), lambda h, i, ks: (h, i, 0)),
                   pl.BlockSpec((None, None, 2, t), lambda h, i, ks: (h, i, 0, 0))],
        out_shape=[_sds((s, hp_blocks * LANES), F32), _sds((hp_blocks, s, 2), F32), _sds((hp_blocks, nq, 2, t), F32)],
        scratch=[pltpu.VMEM((t, LANES), F32), pltpu.VMEM((2, t, LANES), F32), pltpu.VMEM((2, t, LANES), F32)],
        sem=("parallel", "arbitrary"))(kb_start, qkv, qkv, qkv, ck_rows)


def _flash_dq(qkv, do16, ck_rows, lse_c, delta, kb_start, name):
    s = qkv.shape[0]
    t = min(ATT_TILE, s)
    nq = s // t
    scale = HEAD_DIM ** -0.5
    hp_blocks = FOX_HEADS // 2

    def body(ks_ref, q_ref, do_ref, k_ref, v_ref, ck_ref, lse_ref, dl_ref, dq_ref, drow_ref, acc_ref, row_acc):
        hp, qb = pl.program_id(0), pl.program_id(1)
        q2, do2 = q_ref[...] * scale, do_ref[...]
        first = _head_pair_masks(t)
        zero = jnp.zeros_like(q2)
        qs = (jnp.where(first, q2, zero), jnp.where(first, zero, q2))
        dos = (jnp.where(first, do2, zero), jnp.where(first, zero, do2))
        lse, dl = lse_ref[...], dl_ref[...]
        lane = lax.broadcasted_iota(jnp.int32, (t, LANES), 1)
        lse_b = [jnp.broadcast_to(lse[:, j:j + 1], (t, LANES)) for j in range(2)]
        dl_b = [jnp.broadcast_to(jnp.sum(jnp.where(lane == 2 * hp + j, dl, 0.0), axis=1, keepdims=True), (t, LANES))
                for j in range(2)]
        acc_ref[...] = jnp.zeros_like(acc_ref)
        row_acc[...] = jnp.zeros_like(row_acc)

        def tile(kb, spans):
            off = pl.multiple_of(kb * t, t)
            k2 = k_ref[pl.ds(off, t), :]
            v2 = v_ref[pl.ds(off, t), :]
            ck = ck_ref[kb]
            for r0, r1, c0, c1, masked in spans:
                kk, vv = k2[c0:c1], v2[c0:c1]
                dqs = []
                for j in range(2):
                    sc = _dot(qs[j][r0:r1], kk, NT_DIMS) - ck[j:j + 1, c0:c1]
                    if masked:
                        sc = jnp.where(_span_mask(r0, r1, c0, c1), sc, -jnp.inf)
                    dp = _dot(dos[j][r0:r1], vv, NT_DIMS)
                    lb, db_ = lse_b[j][r0:r1], dl_b[j][r0:r1]
                    dsb = [jnp.exp(x - lb) * (d - db_) for x, d in zip(_lane_blocks(sc), _lane_blocks(dp))]
                    row_acc[j, r0:r1] += _sum_list(dsb)
                    dqs.append(_dot(jnp.concatenate(dsb, axis=1).astype(BF16), kk))
                acc_ref[r0:r1] += jnp.where(first[r0:r1], dqs[0], dqs[1])

        def step(kb, carry):
            tile(kb, _full_spans(t))
            return carry

        lax.fori_loop(ks_ref[hp, qb], qb, step, 0)
        tile(qb, _diagonal_spans(t))
        dq_ref[...] = (acc_ref[...] * scale).astype(dq_ref.dtype)
        for j in range(2):
            drow_ref[j:j + 1, :] = jnp.sum(row_acc[j].T, axis=0, keepdims=True)

    return _pcall(
        body, name=name, grid=(hp_blocks, nq), prefetch=1,
        in_specs=[pl.BlockSpec((t, LANES), lambda h, i, ks: (i, h)),
                  pl.BlockSpec((t, LANES), lambda h, i, ks: (i, h)),
                  pl.BlockSpec((s, LANES), lambda h, i, ks: (0, hp_blocks + h)),
                  pl.BlockSpec((s, LANES), lambda h, i, ks: (0, 2 * hp_blocks + h)),
                  pl.BlockSpec((None, nq, 2, t), lambda h, i, ks: (h, 0, 0, 0)),
                  pl.BlockSpec((None, t, 2), lambda h, i, ks: (h, i, 0)),
                  pl.BlockSpec((t, LANES), lambda h, i, ks: (i, 0))],
        out_specs=[pl.BlockSpec((t, LANES), lambda h, i, ks: (i, h)),
                   pl.BlockSpec((None, None, 2, t), lambda h, i, ks: (h, i, 0, 0))],
        out_shape=[_sds((s, hp_blocks * LANES), BF16), _sds((hp_blocks, nq, 2, t), F32)],
        scratch=[pltpu.VMEM((t, LANES), F32), pltpu.VMEM((2, t, LANES), F32)],
        sem=("parallel", "arbitrary"))(kb_start, qkv, do16, qkv, qkv, ck_rows, lse_c, delta)


def _flash_dkv(qkv, do16, cum, lse_rows, dl_rows, qb_end, name):
    s = qkv.shape[0]
    t = min(ATT_TILE, s)
    nq = s // t
    scale = HEAD_DIM ** -0.5
    hp_blocks = FOX_HEADS // 2

    def body(qe_ref, k_ref, v_ref, cum_ref, q_ref, do_ref, lse_ref, dl_ref, dk_ref, dv_ref, dck_ref,
             dk_acc, dv_acc, dck_acc):
        hp, kb = pl.program_id(0), pl.program_id(1)
        k2, v2 = k_ref[...] * scale, v_ref[...]
        first = _head_pair_masks(t)
        zero = jnp.zeros_like(k2)
        ks = (jnp.where(first, k2, zero), jnp.where(first, zero, k2))
        vs = (jnp.where(first, v2, zero), jnp.where(first, zero, v2))
        cumv = cum_ref[...]
        lane = lax.broadcasted_iota(jnp.int32, (t, LANES), 1)
        ck_b = [jnp.broadcast_to(jnp.sum(jnp.where(lane == 2 * hp + j, cumv, 0.0), axis=1, keepdims=True), (t, LANES))
                for j in range(2)]
        dk_acc[...] = jnp.zeros_like(dk_acc)
        dv_acc[...] = jnp.zeros_like(dv_acc)
        dck_acc[...] = jnp.zeros_like(dck_acc)

        def tile(qb, spans):
            off = pl.multiple_of(qb * t, t)
            q2 = q_ref[pl.ds(off, t), :]
            do2 = do_ref[pl.ds(off, t), :]
            lse, dl = lse_ref[qb], dl_ref[qb]
            for r0, r1, c0, c1, masked in spans:
                qq, dd, fr = q2[c0:c1], do2[c0:c1], first[r0:r1]
                dvs, dks = [], []
                for j in range(2):
                    sc = _dot(ks[j][r0:r1], qq, NT_DIMS)
                    if masked:
                        sc = jnp.where(_span_mask(r0, r1, c0, c1, transposed=True), sc, -jnp.inf)
                    dp = _dot(vs[j][r0:r1], dd, NT_DIMS) - dl[j:j + 1, c0:c1]
                    cb_ = ck_b[j][r0:r1]
                    pb = [jnp.exp((x - cb_) - l) for x, l in zip(_lane_blocks(sc), _lane_blocks(lse[j:j + 1, c0:c1]))]
                    dsb = [p * d for p, d in zip(pb, _lane_blocks(dp))]
                    dck_acc[j, r0:r1] += _sum_list(dsb)
                    dvs.append(_dot(jnp.concatenate(pb, axis=1).astype(BF16), dd))
                    dks.append(_dot(jnp.concatenate(dsb, axis=1).astype(BF16), qq))
                dv_acc[r0:r1] += jnp.where(fr, dvs[0], dvs[1])
                dk_acc[r0:r1] += jnp.where(fr, dks[0], dks[1])

        def step(qb, carry):
            tile(qb, _full_spans(t))
            return carry

        tile(kb, _diagonal_spans(t, transposed=True))
        lax.fori_loop(kb + 1, qe_ref[hp, kb] + 1, step, 0)
        dk_ref[...] = (dk_acc[...] * scale).astype(dk_ref.dtype)
        dv_ref[...] = dv_acc[...].astype(dv_ref.dtype)
        for j in range(2):
            dck_ref[j:j + 1, :] = -jnp.sum(dck_acc[j].T, axis=0, keepdims=True)

    return _pcall(
        body, name=name, grid=(hp_blocks, nq), prefetch=1,
        in_specs=[pl.BlockSpec((t, LANES), lambda h, j, qe: (j, hp_blocks + h)),
                  pl.BlockSpec((t, LANES), lambda h, j, qe: (j, 2 * hp_blocks + h)),
                  pl.BlockSpec((t, LANES), lambda h, j, qe: (j, 0)),
                  pl.BlockSpec((s, LANES), lambda h, j, qe: (0, h)),
                  pl.BlockSpec((s, LANES), lambda h, j, qe: (0, h)),
                  pl.BlockSpec((None, nq, 2, t), lambda h, j, qe: (h, 0, 0, 0)),
                  pl.BlockSpec((None, nq, 2, t), lambda h, j, qe: (h, 0, 0, 0))],
        out_specs=[pl.BlockSpec((t, LANES), lambda h, j, qe: (j, h)),
                   pl.BlockSpec((t, LANES), lambda h, j, qe: (j, h)),
                   pl.BlockSpec((None, None, 2, t), lambda h, j, qe: (h, j, 0, 0))],
        out_shape=[_sds((s, hp_blocks * LANES), BF16), _sds((s, hp_blocks * LANES), BF16),
                   _sds((hp_blocks, nq, 2, t), F32)],
        scratch=[pltpu.VMEM((t, LANES), F32), pltpu.VMEM((t, LANES), F32), pltpu.VMEM((2, t, LANES), F32)],
        sem=("parallel", "arbitrary"))(qb_end, qkv, qkv, cum, qkv, do16, lse_rows, dl_rows)


SKIP_NATS = 110.0


def _qk_norms(qkv, ind16, name):
    d = FOX_HEADS * HEAD_DIM

    def fn(tile, ind):
        q = tile[:, :d].astype(F32)
        k = tile[:, d:2 * d].astype(F32)
        return _dot_split(q * q, ind, passes=2), _dot_split(k * k, ind, passes=2)
    return _rowwise(fn, [qkv], [ind16], [(LANES, F32), (LANES, F32)], [], name=name)


def _skip_bounds(qn, kn, cum, t):
    s = qn.shape[0]
    nq = s // t
    hp = FOX_HEADS // 2
    scale = HEAD_DIM ** -0.5
    qmax = jnp.sqrt(jnp.max(qn.reshape(nq, t, FOX_HEADS), axis=1))
    kmax = jnp.sqrt(jnp.max(kn, axis=0))
    bound = qmax * kmax[None, :] * (scale * 1.01) + 1e-3
    gap = cum[0::t][:, None, :] - cum[t - 1::t][None, :, :]
    idx = jnp.arange(nq, dtype=jnp.int32)
    needed = (gap + 2.0 * bound[:, None, :]) > -SKIP_NATS
    needed = needed.reshape(nq, nq, hp, 2).any(axis=-1) & (idx[None, :] <= idx[:, None])[:, :, None]
    first = jnp.min(jnp.where(needed, idx[None, :, None], nq), axis=1)
    first = jnp.minimum(first, idx[:, None])
    start = lax.cummin(first, axis=0, reverse=True)
    uses = start[:, None, :] <= idx[None, :, None]
    last = jnp.max(jnp.where(uses, idx[:, None, None], 0), axis=0)
    last = jnp.maximum(last, idx[:, None])
    return start.T.astype(jnp.int32), last.T.astype(jnp.int32)


def _head_rowsum(a, b, ind16, name):
    s, d = a.shape
    tm = min(ROW_TILE, s)

    def body(a_ref, b_ref, ind_ref, o_ref, rows_ref):
        dsum = _dot_split(a_ref[...] * b_ref[...], ind_ref[...])
        o_ref[...] = dsum
        rows_ref[...] = dsum.T[0:FOX_HEADS, :]

    tile = pl.BlockSpec((tm, d), lambda i: (i, 0))
    return _pcall(body, name=name, grid=(s // tm,),
                  in_specs=[tile, tile, pl.BlockSpec((d, LANES), lambda i: (0, 0))],
                  out_specs=[pl.BlockSpec((tm, LANES), lambda i: (i, 0)), pl.BlockSpec((FOX_HEADS, tm), lambda i: (0, i))],
                  out_shape=[_sds((s, LANES), F32), _sds((FOX_HEADS, s), F32)], sem=("parallel",))(a, b, ind16)


def _rows_to_tiles(x, t):
    s = x.shape[1]
    return x.reshape(FOX_HEADS // 2, 2, s // t, t).transpose(0, 2, 1, 3)


def _tiles_to_cols(x):
    hp, nq, _, t = x.shape
    return jnp.pad(x.transpose(1, 3, 0, 2).reshape(nq * t, 2 * hp), ((0, 0), (0, LANES - 2 * hp)))


def _fox_forward(u, w):
    s = u.shape[0]
    t = min(ATT_TILE, s)
    qkv = _mm(u, w["fox_qkv"], "nn", [BF16], name="fox_qkv")
    fraw = _mm(u, w["fox_f"], "nn", [F32], name="fox_fproj")
    cum, cum_rows = _fox_gate(fraw, w["fox_bf"], "fox_gate")
    ck_rows = _rows_to_tiles(cum_rows, t)
    qn, kn = _qk_norms(qkv, w["head_ind"], "fox_qk_norms")
    kb_start, qb_end = _skip_bounds(qn[:, :FOX_HEADS], kn[:, :FOX_HEADS], cum[:, :FOX_HEADS], t)
    o, lse, lse_rows = _flash_fwd(qkv, ck_rows, kb_start, "fox_flash_fwd")
    y = _mm(o, w["fox_o"], "nn", [F32], name="fox_oproj")
    return y, dict(u=u, qkv=qkv, fraw=fraw, cum=cum, ck_rows=ck_rows, o=o, lse=lse, lse_rows=lse_rows,
                   kb_start=kb_start, qb_end=qb_end)


def _fox_backward(dy, sv, w):
    s = dy.shape[0]
    t = min(ATT_TILE, s)
    do32, do16 = _mm(dy, w["fox_o"], "nt", [F32, BF16], name="fox_do")
    g_wo = _mm(sv["o"], dy, "tn", [F32], name="fox_gwo")
    delta, delta_rows = _head_rowsum(do32, sv["o"], w["head_ind"], "fox_delta")
    dq, drow = _flash_dq(sv["qkv"], do16, sv["ck_rows"], sv["lse"], delta, sv["kb_start"], "fox_flash_dq")
    dk, dv, dck = _flash_dkv(sv["qkv"], do16, sv["cum"], sv["lse_rows"], _rows_to_tiles(delta_rows, t),
                             sv["qb_end"], "fox_flash_dkv")
    df, db_f = _fox_gate_bwd(_tiles_to_cols(drow), _tiles_to_cols(dck), sv["fraw"], w["fox_bf"], "fox_gate_bwd")
    du = _mm(df, w["fox_f"], "nt", [F32], name="fox_du_f")
    du = _mm_nt_blocks(_k_blocks(dq) + _k_blocks(dk) + _k_blocks(dv), w["fox_qkv"], du, name="fox_du")
    g_win = jnp.concatenate([_mm(sv["u"], dq, "tn", [F32], name="fox_gwin_q"),
                             _mm(sv["u"], dk, "tn", [F32], name="fox_gwin_k"),
                             _mm(sv["u"], dv, "tn", [F32], name="fox_gwin_v"),
                             _mm(sv["u"], df, "tn", [F32], name="fox_gwin_f")[:, :FOX_HEADS]], axis=1)
    return du, dict(fox_w_in=g_win, fox_w_o=g_wo, fox_b_f=db_f[:, :FOX_HEADS])


def _conv_fwd(xpre, w8, b, name):
    s, c = xpre.shape
    tm, tc = min(ROW_TILE, s), min(1024, c)
    hb = tm // 8

    r = min(CONV_CHUNK, tm)

    def body(x_ref, h_ref, w_ref, b_ref, xc_ref, xa_ref):
        i = pl.program_id(1)
        row8 = lax.broadcasted_iota(jnp.int32, (8, LANES), 0)
        for cb in range(tc // LANES):
            ls = slice(cb * LANES, (cb + 1) * LANES)
            w, bias = w_ref[:, ls], b_ref[:, ls]
            for rb in range(tm // r):
                r0 = rb * r
                cur = x_ref[r0:r0 + r, ls]
                acc = cur * w[3:4] + bias
                if rb == 0:
                    halo = jnp.where(i > 0, h_ref[:, ls], 0.0)
                    x8 = cur[0:8]
                    acc8 = x8 * w[3:4] + bias
                    for j in range(1, SSM_CONV):
                        acc = acc + w[3 - j:4 - j] * pltpu.roll(cur, j, 0)
                        acc8 = acc8 + w[3 - j:4 - j] * jnp.where(row8 < j, pltpu.roll(halo, j, 0), pltpu.roll(x8, j, 0))
                    acc = jnp.concatenate([acc8, acc[8:]], axis=0)
                else:
                    for j in range(1, SSM_CONV):
                        acc = acc + w[3 - j:4 - j] * x_ref[r0 - j:r0 - j + r, ls]
                xc_ref[r0:r0 + r, ls] = acc
                xa_ref[r0:r0 + r, ls] = _silu(acc)

    tile = pl.BlockSpec((tm, tc), lambda jc, i: (i, jc))
    return _pcall(body, name=name, grid=(c // tc, s // tm),
                  in_specs=[tile, pl.BlockSpec((8, tc), lambda jc, i: (jnp.maximum(i * hb - 1, 0), jc)),
                            pl.BlockSpec((8, tc), lambda jc, i: (0, jc)), pl.BlockSpec((1, tc), lambda jc, i: (0, jc))],
                  out_specs=[tile, tile], out_shape=[_sds((s, c), F32), _sds((s, c), F32)],
                  sem=("parallel", "arbitrary"))(xpre, xpre, w8, b)


def _conv_bwd(dxa, xc, xpre, w8, name):
    s, c = xpre.shape
    tm, tc = min(ROW_TILE, s), min(1024, c)
    hb = tm // 8
    n = s // tm

    r = min(CONV_CHUNK, tm)

    def body(d_ref, xc_ref, x_ref, xh_ref, dn_ref, xcn_ref, w_ref, dx_ref, dw_ref, db_ref, g_scr):
        i = pl.program_id(1)

        @pl.when(i == 0)
        def _():
            dw_ref[...] = jnp.zeros_like(dw_ref)
            db_ref[...] = jnp.zeros_like(db_ref)
        row8 = lax.broadcasted_iota(jnp.int32, (8, LANES), 0)
        rowr = lax.broadcasted_iota(jnp.int32, (r, LANES), 0)
        for cb in range(tc // LANES):
            ls = slice(cb * LANES, (cb + 1) * LANES)
            w = w_ref[:, ls]
            for rb in range(tm // r):
                r0 = rb * r
                g_scr[r0:r0 + r, ls] = d_ref[r0:r0 + r, ls] * _dsilu(xc_ref[r0:r0 + r, ls])
            g_scr[tm:tm + 8, ls] = jnp.where(i < n - 1, dn_ref[:, ls] * _dsilu(xcn_ref[:, ls]), 0.0)
            db = jnp.zeros((1, LANES), F32)
            dws = [jnp.zeros((1, LANES), F32) for _ in range(SSM_CONV)]
            for rb in range(tm // r):
                r0 = rb * r
                g = g_scr[r0:r0 + r, ls]
                x = x_ref[r0:r0 + r, ls]
                db = db + _colsum(g)
                dws[3] = dws[3] + _colsum(g * x)
                acc = g * w[3:4]
                for j in range(1, SSM_CONV):
                    if rb == 0:
                        halo = jnp.where(i > 0, xh_ref[:, ls], 0.0)
                        dws[3 - j] = dws[3 - j] + _colsum(g * jnp.where(rowr >= j, pltpu.roll(x, j, 0), 0.0))
                        dws[3 - j] = dws[3 - j] + _colsum(jnp.where(row8 < j, g[0:8] * pltpu.roll(halo, j, 0), 0.0))
                    else:
                        dws[3 - j] = dws[3 - j] + _colsum(g * x_ref[r0 - j:r0 - j + r, ls])
                    acc = acc + w[3 - j:4 - j] * g_scr[r0 + j:r0 + j + r, ls]
                dx_ref[r0:r0 + r, ls] = acc.astype(dx_ref.dtype)
            db_ref[:, ls] += db
            for k in range(SSM_CONV):
                dw_ref[k:k + 1, ls] += dws[k]

    tile = pl.BlockSpec((tm, tc), lambda jc, i: (i, jc))
    prev8 = pl.BlockSpec((8, tc), lambda jc, i: (jnp.maximum(i * hb - 1, 0), jc))
    next8 = pl.BlockSpec((8, tc), lambda jc, i: (jnp.minimum((i + 1) * hb, n * hb - 1), jc))
    return _pcall(body, name=name, grid=(c // tc, n),
                  in_specs=[tile, tile, tile, prev8, next8, next8, pl.BlockSpec((8, tc), lambda jc, i: (0, jc))],
                  out_specs=[tile, pl.BlockSpec((8, tc), lambda jc, i: (0, jc)), pl.BlockSpec((1, tc), lambda jc, i: (0, jc))],
                  out_shape=[_sds((s, c), BF16), _sds((8, c), F32), _sds((1, c), F32)],
                  scratch=[pltpu.VMEM((tm + 8, tc), F32)],
                  sem=("parallel", "arbitrary"))(dxa, xc, xpre, xpre, dxa, xc, w8)


def _ssd_pre(dtraw, dt_bias, a_log, cst, name):
    def fn(raw, bias, alog, expand):
        tm = raw.shape[0]
        z = raw + bias
        dt = jnp.maximum(z, 0.0) + jnp.log(1.0 + jnp.exp(-jnp.abs(z)))
        lane = lax.broadcasted_iota(jnp.int32, (tm, LANES), 1)
        pos = lax.broadcasted_iota(jnp.int32, (tm, LANES), 0) & (SSM_CHUNK - 1)
        dt = jnp.where(lane < SSM_HEADS, dt, 0.0)
        c = dt * (-jnp.exp(alog))
        sh = 1
        while sh < SSM_CHUNK:
            c = c + jnp.where(pos >= sh, pltpu.roll(c, sh, 0), 0.0)
            sh *= 2
        return dt, c, _dot_split(dt, expand), _dot_split(c, expand)
    wide = SSM_HEADS * HEAD_DIM
    return _rowwise(fn, [dtraw], [dt_bias, a_log, cst["expand"]],
                    [(LANES, F32), (LANES, F32), (wide, F32), (wide, F32)], [], name=name)


def _ssd_post(dacs, ddt, dtraw, dt, dt_bias, a_log, name):
    def fn(dacs_v, ddt_v, raw, dt_v, bias, alog):
        tm = raw.shape[0]
        pos = lax.broadcasted_iota(jnp.int32, (tm, LANES), 0) & (SSM_CHUNK - 1)
        a = -jnp.exp(alog)
        c = dacs_v
        sh = 1
        while sh < SSM_CHUNK:
            c = c + jnp.where(pos + sh < SSM_CHUNK, pltpu.roll(c, tm - sh, 0), 0.0)
            sh *= 2
        draw = (ddt_v + c * a) * _sigmoid(raw + bias)
        return draw, _colsum(draw), _colsum(c * dt_v * a)
    return _rowwise(fn, [dacs, ddt, dtraw, dt], [dt_bias, a_log], [(LANES, BF16)], [(1, LANES)] * 2, name=name)


def _heads_rows(x, s):
    return x[:, :SSM_HEADS].reshape(s // SSM_CHUNK, SSM_CHUNK, SSM_GROUPS, 4).transpose(2, 0, 3, 1)


def _ssd_constants():
    gp = SSD_GROUPS_PER_STEP
    src = np.arange(LANES)[:, None]
    expand = src == np.arange(SSM_HEADS * HEAD_DIM)[None, :] // HEAD_DIM
    dst = np.arange(LANES)[None, None, :] - 4 * np.arange(gp)[:, None, None]
    seg = np.arange(SSM_GROUP_WIDTH)[None, :, None] // HEAD_DIM == dst
    seg4 = np.arange(4 * LANES)[None, :, None] // LANES == dst
    return dict(expand=jnp.asarray(expand, BF16), seg=jnp.asarray(seg, BF16), seg4=jnp.asarray(seg4, BF16))


def _ssm_weights(w_in, conv_w, conv_b, dt_bias, a_log, d_skip, norm_w, w_out):
    pad = ((0, 0), (0, LANES - SSM_HEADS))
    w_xbc = _group_cols(w_in[:, 2048:6144])
    w_dt = jnp.pad(w_in[:, 6144:], pad)
    return dict(
        ssm_z=w_in[:, :2048], ssm_xbc=w_xbc, ssm_dt=w_dt,
        ssm_zx=jnp.concatenate([w_in[:, :2048], w_xbc], axis=1),
        ssm_out=w_out,
        conv_w8=_group_cols(jnp.pad(conv_w, ((0, 8 - SSM_CONV), (0, 0)))),
        conv_b=_group_cols(conv_b), norm_w=norm_w,
        dt_bias=jnp.pad(dt_bias, pad), a_log=jnp.pad(a_log, pad),
        d_e=jnp.repeat(d_skip.reshape(SSM_GROUPS, 4), HEAD_DIM, axis=1)[:, None, :],
        ssd_cst=_ssd_constants())


def _ssd_setup(acs_e, acsr):
    l = SSM_CHUNK
    last = acsr[:, l - 1:l]
    lane1 = lax.broadcasted_iota(jnp.int32, (1, SSM_GROUP_WIDTH), 1)
    last_e = last[3:4, :]
    for r in (2, 1, 0):
        last_e = jnp.where(lane1 < HEAD_DIM * (r + 1), last[r:r + 1, :], last_e)
    return jnp.exp(acs_e), jnp.exp(last_e - acs_e), jnp.exp(last_e)


def _head_bcast(acs_e):
    lo = lax.broadcasted_iota(jnp.int32, (acs_e.shape[0], LANES), 1) < HEAD_DIM
    out = []
    for p in range(2):
        blk = acs_e[:, p * LANES:(p + 1) * LANES]
        rolled = pltpu.roll(blk, HEAD_DIM, 1)
        out += [jnp.where(lo, blk, rolled), jnp.where(lo, rolled, blk)]
    return out


def _group_cols(a):
    lead = a.shape[:-1]
    x = a[..., :2048].reshape(lead + (SSM_GROUPS, SSM_GROUP_WIDTH))
    b = a[..., 2048:3072].reshape(lead + (SSM_GROUPS, SSM_STATE))
    c = a[..., 3072:].reshape(lead + (SSM_GROUPS, SSM_STATE))
    return jnp.concatenate([x, b, c], axis=-1).reshape(lead + (4096,))


def _ungroup_cols(a):
    lead = a.shape[:-1]
    y = a.reshape(lead + (SSM_GROUPS, SSM_GROUP_WIDTH + 2 * SSM_STATE))
    return jnp.concatenate([y[..., :256].reshape(lead + (2048,)), y[..., 256:384].reshape(lead + (1024,)),
                            y[..., 384:].reshape(lead + (1024,))], axis=-1)


def _ssd_fwd2(xa, dte, acse, acsr, d_e, name):
    s = xa.shape[0]
    l, gw, ns = SSM_CHUNK, SSM_GROUP_WIDTH, SSM_STATE
    nc = s // l

    gb = gw + 2 * ns
    gp = SSD_GROUPS_PER_STEP

    def body(xa_ref, dt_ref, acs_ref, acsr_ref, d_ref, y_ref, hp_ref, h_sc):
        @pl.when(pl.program_id(1) == 0)
        def _():
            h_sc[...] = jnp.zeros_like(h_sc)
        lane = lax.broadcasted_iota(jnp.int32, (l, gw), 1)
        tril = _causal(l)
        for gi in range(gp):
            x = xa_ref[:, gi * gb:gi * gb + gw]
            bm = xa_ref[:, gi * gb + gw:gi * gb + gw + ns].astype(BF16)
            cm = xa_ref[:, gi * gb + gw + ns:(gi + 1) * gb].astype(BF16)
            acsr = acsr_ref[gi]
            dt_e, acs_e = dt_ref[:, gi * gw:(gi + 1) * gw], acs_ref[:, gi * gw:(gi + 1) * gw]
            acs_bc = _head_bcast(acs_e)
            e_e, dte_e, cd_e = _ssd_setup(acs_e, acsr)
            xdt = x * dt_e
            xdt16 = xdt.astype(BF16)
            cb = _dot(cm, bm, NT_DIMS)
            yd = jnp.zeros((l, gw), F32)
            for r in range(4):
                lm = jnp.exp(jnp.where(tril, acs_bc[r] - acsr[r:r + 1, :], -jnp.inf))
                yr = _dot((cb * lm).astype(BF16), xdt16)
                yd = jnp.where((lane >= HEAD_DIM * r) & (lane < HEAD_DIM * (r + 1)), yr, yd)
            hp = h_sc[gi]
            hp_ref[gi] = hp
            y_ref[:, gi * gw:(gi + 1) * gw] = yd + _dot(cm, hp.astype(BF16)) * e_e + x * d_ref[gi]
            h_sc[gi] = hp * cd_e + _dot(bm, (xdt * dte_e).astype(BF16), TN_DIMS)

    return _pcall(
        body, name=name, grid=(SSM_GROUPS // gp, nc),
        in_specs=[pl.BlockSpec((l, gp * gb), lambda g, c: (c, g)),
                  pl.BlockSpec((l, gp * gw), lambda g, c: (c, g)),
                  pl.BlockSpec((l, gp * gw), lambda g, c: (c, g)),
                  pl.BlockSpec((gp, None, 4, l), lambda g, c: (g, c, 0, 0)),
                  pl.BlockSpec((gp, 1, gw), lambda g, c: (g, 0, 0))],
        out_specs=[pl.BlockSpec((l, gp * gw), lambda g, c: (c, g)),
                   pl.BlockSpec((gp, None, ns, gw), lambda g, c: (g, c, 0, 0))],
        out_shape=[_sds((s, 2048), F32), _sds((SSM_GROUPS, nc, ns, gw), F32)],
        scratch=[pltpu.VMEM((gp, ns, gw), F32)],
        sem=("parallel", "arbitrary"))(xa, dte, acse, acsr, d_e)


def _ssd_bwd2(dy, xa, dte, acse, acsr, d_e, hprev, cst, name):
    s = xa.shape[0]
    l, gw, ns = SSM_CHUNK, SSM_GROUP_WIDTH, SSM_STATE
    nc = s // l

    gb = gw + 2 * ns
    gp = SSD_GROUPS_PER_STEP

    def body(dy_ref, xa_ref, dt_ref, acs_ref, acsr_ref, d_ref, hp_ref,
             seg_ref, seg4_ref, dxa_ref, dacs_ref, ddt_ref, dd_ref, dh_sc):
        @pl.when(pl.program_id(1) == 0)
        def _():
            dh_sc[...] = jnp.zeros_like(dh_sc)
            dd_ref[...] = jnp.zeros_like(dd_ref)
        parts = [one_group(gi, dy_ref, xa_ref, dt_ref, acs_ref, acsr_ref, d_ref, hp_ref, seg_ref, seg4_ref,
                           dxa_ref, dh_sc) for gi in range(gp)]
        dacs_ref[...] = _sum_list([p[0] for p in parts])
        ddt_ref[...] = _sum_list([p[1] for p in parts])
        dd_ref[0:1, :] += _sum_list([p[2] for p in parts])

    def one_group(gi, dy_ref, xa_ref, dt_ref, acs_ref, acsr_ref, d_ref, hp_ref, seg_ref, seg4_ref, dxa_ref, dh_sc):
        dyv = dy_ref[:, gi * gw:(gi + 1) * gw]
        x = xa_ref[:, gi * gb:gi * gb + gw]
        bm = xa_ref[:, gi * gb + gw:gi * gb + gw + ns].astype(BF16)
        cm = xa_ref[:, gi * gb + gw + ns:(gi + 1) * gb].astype(BF16)
        acsr = acsr_ref[gi]
        dt_e, acs_e = dt_ref[:, gi * gw:(gi + 1) * gw], acs_ref[:, gi * gw:(gi + 1) * gw]
        acs_bc = _head_bcast(acs_e)
        e_e, dte_e, cd_e = _ssd_setup(acs_e, acsr)
        seg, seg4 = seg_ref[gi], seg4_ref[gi]
        lane = lax.broadcasted_iota(jnp.int32, (l, gw), 1)
        xdt = x * dt_e
        xdt16 = xdt.astype(BF16)
        dy16 = dyv.astype(BF16)
        cb = _dot(cm, bm, NT_DIMS)
        cbt = _dot(bm, cm, NT_DIMS)
        hp = hp_ref[gi]
        hp16 = hp.astype(BF16)
        g = dh_sc[gi]
        g16 = g.astype(BF16)
        t_all = _dot(cm, hp16)
        dt16 = (dyv * e_e).astype(BF16)
        dc = _dot(dt16, hp16, NT_DIMS)
        dhp = _dot(cm, dt16, TN_DIMS)
        wv = xdt * dte_e
        dw = _dot(bm, g16)
        db = _dot(wv.astype(BF16), g16, NT_DIMS)
        dxdt = dw * dte_e
        acs_term = dyv * t_all * e_e - dw * wv
        last_term = _colsum(dw * wv) + _colsum(g * hp) * cd_e
        dh_sc[gi] = g * cd_e + dhp
        tril, triu = _causal(l), _causal(l, transposed=True)
        dcb = jnp.zeros((l, l), F32)
        dcbt = jnp.zeros((l, l), F32)
        qd = []
        for r in range(4):
            in_head = (lane >= HEAD_DIM * r) & (lane < HEAD_DIM * (r + 1))
            a_col = acs_bc[r]
            lm = jnp.exp(jnp.where(tril, a_col - acsr[r:r + 1, :], -jnp.inf))
            lmt = jnp.exp(jnp.where(triu, acsr[r:r + 1, :] - a_col, -jnp.inf))
            mm_, mt = cb * lm, cbt * lmt
            dyr = jnp.where(in_head, dy16, jnp.zeros_like(dy16))
            dm = _dot(dyr, xdt16, NT_DIMS)
            dmt = _dot(xdt16, dyr, NT_DIMS)
            dxdt = dxdt + jnp.where(in_head, _dot(mt.astype(BF16), dy16), 0.0)
            dcb = dcb + dm * lm
            dcbt = dcbt + dmt * lmt
            qd.append(dm * mm_ - dmt * mt)
        dc = dc + _dot(dcb.astype(BF16), bm)
        db = db + _dot(dcbt.astype(BF16), cm)
        rowl = lax.broadcasted_iota(jnp.int32, (l, LANES), 0)
        row8 = lax.broadcasted_iota(jnp.int32, (8, gw), 0)
        small = _dot_split(jnp.where(row8 == 0, last_term, jnp.where(row8 == 1, _colsum(dyv * x), 0.0)), seg, passes=2)
        big = _dot_split(jnp.concatenate([acs_term, dxdt * x], axis=0), seg, passes=2)
        dacs = (big[0:l] + _dot_split(jnp.concatenate(qd, axis=1), seg4, passes=2)
                + jnp.where(rowl == l - 1, small[0:1, :], 0.0))
        dxa_ref[:, gi * gb:(gi + 1) * gb] = jnp.concatenate([dxdt * dt_e + dyv * d_ref[gi], db, dc], axis=1)
        return dacs, big[l:2 * l], small[1:2, :]

    rc = lambda c: nc - 1 - c
    ng = SSM_GROUPS // gp
    return _pcall(
        body, name=name, grid=(ng, nc),
        in_specs=[pl.BlockSpec((l, gp * gw), lambda g, c: (rc(c), g)),
                  pl.BlockSpec((l, gp * gb), lambda g, c: (rc(c), g)),
                  pl.BlockSpec((l, gp * gw), lambda g, c: (rc(c), g)),
                  pl.BlockSpec((l, gp * gw), lambda g, c: (rc(c), g)),
                  pl.BlockSpec((gp, None, 4, l), lambda g, c: (g, rc(c), 0, 0)),
                  pl.BlockSpec((gp, 1, gw), lambda g, c: (g, 0, 0)),
                  pl.BlockSpec((gp, None, ns, gw), lambda g, c: (g, rc(c), 0, 0)),
                  pl.BlockSpec((gp, gw, LANES), lambda g, c: (0, 0, 0)),
                  pl.BlockSpec((gp, 4 * LANES, LANES), lambda g, c: (0, 0, 0))],
        out_specs=[pl.BlockSpec((l, gp * gb), lambda g, c: (rc(c), g)),
                   pl.BlockSpec((None, l, LANES), lambda g, c: (g, rc(c), 0)),
                   pl.BlockSpec((None, l, LANES), lambda g, c: (g, rc(c), 0)),
                   pl.BlockSpec((None, 8, LANES), lambda g, c: (g, 0, 0))],
        out_shape=[_sds((s, 4096), F32), _sds((ng, s, LANES), F32), _sds((ng, s, LANES), F32),
                   _sds((ng, 8, LANES), F32)],
        scratch=[pltpu.VMEM((gp, ns, gw), F32)],
        sem=("parallel", "arbitrary"))(dy, xa, dte, acse, acsr, d_e, hprev, cst["seg"], cst["seg4"])


def _gate_norm(y, z, nw, name):
    c = y.shape[1]

    def fn(yv, zv, w):
        outs = []
        for k in range(c // SSM_GROUP_WIDTH):
            sl = slice(k * SSM_GROUP_WIDTH, (k + 1) * SSM_GROUP_WIDTH)
            yg = yv[:, sl] * _silu(zv[:, sl])
            rinv = lax.rsqrt(jnp.mean(yg * yg, axis=-1, keepdims=True) + RMS_EPS)
            outs.append(yg * rinv * w[:, sl])
        return (jnp.concatenate(outs, axis=1),)
    return _rowwise(fn, [y, z], [nw], [(c, BF16)], [], name=name)[0]


def _out_gate_norm_bwd(dy, w_out, y, z, nw, name):
    s, c = y.shape
    tm, tn = min(512, s), min(1024, c)
    k = dy.shape[1]

    def body(a_ref, b_ref, y_ref, z_ref, w_ref, dy_ref, dz_ref, dw_ref):
        @pl.when(pl.program_id(1) == 0)
        def _():
            dw_ref[...] = jnp.zeros_like(dw_ref)
        dv = _dot(a_ref[...], b_ref[...], NT_DIMS)
        yv, zv, w = y_ref[...], z_ref[...], w_ref[...]
        dys, dzs, dws = [], [], []
        for g in range(tn // SSM_GROUP_WIDTH):
            sl = slice(g * SSM_GROUP_WIDTH, (g + 1) * SSM_GROUP_WIDTH)
            ys, zs, ds = yv[:, sl], zv[:, sl], dv[:, sl]
            sz = _silu(zs)
            yg = ys * sz
            rinv = lax.rsqrt(jnp.mean(yg * yg, axis=-1, keepdims=True) + RMS_EPS)
            nrm = yg * rinv
            dn = ds * w[:, sl]
            dyg = rinv * (dn - nrm * jnp.mean(dn * nrm, axis=-1, keepdims=True))
            dys.append(dyg * sz)
            dzs.append(dyg * ys * _dsilu(zs))
            dws.append(_colsum(ds * nrm))
        dy_ref[...] = jnp.concatenate(dys, axis=1)
        dz_ref[...] = jnp.concatenate(dzs, axis=1).astype(dz_ref.dtype)
        dw_ref[...] += jnp.concatenate(dws, axis=1)

    tile = pl.BlockSpec((tm, tn), lambda j, i: (i, j))
    row = pl.BlockSpec((1, tn), lambda j, i: (0, j))
    return _pcall(body, name=name, grid=(c // tn, s // tm),
                  in_specs=[pl.BlockSpec((tm, k), lambda j, i: (i, 0)), pl.BlockSpec((tn, k), lambda j, i: (j, 0)),
                            tile, tile, row],
                  out_specs=[tile, tile, row], out_shape=[_sds((s, c), F32), _sds((s, c), BF16), _sds((1, c), F32)],
                  sem=("parallel", "arbitrary"))(dy, w_out, y, z, nw)


def _ssd_forward(u, w):
    s = u.shape[0]
    z = _mm(u, w["ssm_z"], "nn", [F32], name="ssm_zproj")
    xpre = _mm(u, w["ssm_xbc"], "nn", [F32], name="ssm_xproj")
    dtraw = _mm(u, w["ssm_dt"], "nn", [F32], name="ssm_dtproj")
    xc, xa = _conv_fwd(xpre, w["conv_w8"], w["conv_b"], "ssm_conv")
    dt, acs, dte, acse = _ssd_pre(dtraw, w["dt_bias"], w["a_log"], w["ssd_cst"], "ssm_pre")
    acsr = _heads_rows(acs, s)
    y, hprev = _ssd_fwd2(xa, dte, acse, acsr, w["d_e"], "ssm_scan")
    yn = _gate_norm(y, z, w["norm_w"], "ssm_gate_norm")
    out = _mm(yn, w["ssm_out"], "nn", [F32], name="ssm_oproj")
    return out, dict(u=u, z=z, xpre=xpre, xc=xc, xa=xa, dtraw=dtraw, dt=dt, dte=dte, acse=acse,
                     acsr=acsr, y=y, hprev=hprev, yn=yn)


def _ssd_backward(dy, sv, w):
    s = dy.shape[0]
    g_wout = _mm(sv["yn"], dy, "tn", [F32], name="ssm_gwout")
    dys, dz, dnw = _out_gate_norm_bwd(dy, w["ssm_out"], sv["y"], sv["z"], w["norm_w"], "ssm_dyn_gate_norm_bwd")
    dxa, dacs_c, ddt_c, dd = _ssd_bwd2(dys, sv["xa"], sv["dte"], sv["acse"], sv["acsr"], w["d_e"], sv["hprev"],
                                       w["ssd_cst"], "ssm_scan_bwd")
    per_step = 4 * SSD_GROUPS_PER_STEP
    pad = ((0, 0), (0, LANES - SSM_HEADS))
    dacs = jnp.pad(jnp.concatenate([a[:, :per_step] for a in dacs_c], axis=1), pad)
    ddt = jnp.pad(jnp.concatenate([a[:, :per_step] for a in ddt_c], axis=1), pad)
    draw, dbias, dalog = _ssd_post(dacs, ddt, sv["dtraw"], sv["dt"], w["dt_bias"], w["a_log"], "ssm_post")
    dxpre, dcw, dcb = _conv_bwd(dxa, sv["xc"], sv["xpre"], w["conv_w8"], "ssm_conv_bwd")
    du = _mm(draw, w["ssm_dt"], "nt", [F32], name="ssm_du_dt")
    n_z = dz.shape[1]
    du = _mm_nt_blocks(_k_blocks(dz), w["ssm_zx"][:, :n_z], du, name="ssm_du_z")
    du = _mm_nt_blocks(_k_blocks(dxpre), w["ssm_zx"][:, n_z:], du, name="ssm_du_x")
    g_win = jnp.concatenate([_mm(sv["u"], dz, "tn", [F32], name="ssm_gwin_z"),
                             _ungroup_cols(_mm(sv["u"], dxpre, "tn", [F32], name="ssm_gwin_x")),
                             _mm(sv["u"], draw, "tn", [F32], name="ssm_gwin_dt")[:, :SSM_HEADS]], axis=1)
    return du, dict(ssm_w_in=g_win, ssm_w_out=g_wout, ssm_conv_w=_ungroup_cols(dcw[:SSM_CONV]),
                    ssm_conv_b=_ungroup_cols(dcb), ssm_norm_w=dnw, ssm_dt_bias=dbias[:, :SSM_HEADS],
                    ssm_a_log=dalog[:, :SSM_HEADS], ssm_d=dd[:, 0, :per_step].reshape(1, SSM_HEADS))


def _mlp_forward(u2, w1, w2, tag):
    def epi(acc):
        hr = jnp.maximum(acc, 0.0)
        return hr, hr * hr
    hr, a = _mm(u2, w1, "nn", [BF16, BF16], name=tag + "_mlp_up", epi=epi)
    y2 = _mm(a, w2, "nn", [F32], name=tag + "_mlp_down")
    return y2, hr, a


def _mlp_backward(dy2, u2, hr, a, w1, w2, tag):
    dh = _mm(dy2, w2, "nt", [BF16], name=tag + "_mlp_dh", extra=(hr,),
             epi=lambda acc, h: (acc * (2.0 * h.astype(F32)),))
    g_w2 = _mm(a, dy2, "tn", [F32], name=tag + "_mlp_gw2")
    g_w1 = _mm(u2, dh, "tn", [F32], name=tag + "_mlp_gw1")
    du2 = _mm(dh, w1, "nt", [F32], name=tag + "_mlp_du")
    return du2, g_w1, g_w2


def _ada_forward(c16, ada_w, ada_b_cols, name):
    nl, d, cols = ada_w.shape
    tn = 512

    def body(c_ref, w_ref, b_ref, o_ref):
        cond = _silu(c_ref[...]).astype(BF16)
        o_ref[...] = _dot(cond, w_ref[...].astype(BF16)) + b_ref[...]

    return _pcall(body, name=name, grid=(nl, cols // tn),
                  in_specs=[pl.BlockSpec((16, d), lambda i, j: (0, 0)),
                            pl.BlockSpec((None, d, tn), lambda i, j: (i, 0, j)),
                            pl.BlockSpec((None, 1, tn), lambda i, j: (i, 0, j))],
                  out_specs=pl.BlockSpec((None, 16, tn), lambda i, j: (i, 0, j)),
                  out_shape=_sds((nl, 16, cols), F32), sem=("parallel", "parallel"))(c16, ada_w, ada_b_cols)


def _ada_backward(c_t, dmod_cols, name):
    d, nb = c_t.shape
    nl, _, cols = dmod_cols.shape
    tn = 512

    def body(c_ref, dm_ref, o_ref):
        cond = _silu(c_ref[...])
        dm = dm_ref[...]
        acc = cond[:, 0:1] * dm[0:1, :]
        for b in range(1, nb):
            acc = acc + cond[:, b:b + 1] * dm[b:b + 1, :]
        o_ref[...] = acc

    return _pcall(body, name=name, grid=(nl, cols // tn),
                  in_specs=[pl.BlockSpec((d, nb), lambda i, j: (0, 0)),
                            pl.BlockSpec((None, nb, tn), lambda i, j: (i, 0, j))],
                  out_specs=pl.BlockSpec((None, d, tn), lambda i, j: (i, 0, j)),
                  out_shape=_sds((nl, d, cols), F32), sem=("parallel", "parallel"))(c_t, dmod_cols)


def _adamw(w, g, m, v, name):
    rows, cols = w.shape
    tm = rows
    for cand in (256, 128, 64, 32, 16, 8):
        if rows % cand == 0 and rows > cand:
            tm = cand
            break
    c1 = 1.0 / (1.0 - ADAM_B1 ** ADAM_STEP)
    c2 = 1.0 / (1.0 - ADAM_B2 ** ADAM_STEP)

    def fn(wv, gv, mv, vv):
        mn = ADAM_B1 * mv + (1.0 - ADAM_B1) * gv
        vn = ADAM_B2 * vv + (1.0 - ADAM_B2) * (gv * gv)
        delta = -ADAM_LR * ((mn * c1) / (jnp.sqrt(vn * c2) + ADAM_EPS) + ADAM_WD * wv)
        return delta, mn, vn
    return _rowwise(fn, [w, g, m, v], [], [(cols, F32)] * 3, [], name=name, tm=tm)


def _my_pos():
    return lax.axis_index("x"), lax.axis_index("y"), lax.axis_index("c")


def _allgather8(x, name):
    r, c = x.shape

    def body(x_ref, out_ref, send_sems, recv_sems, local_sem):
        mx, my, mc = _my_pos()
        me = 4 * mx + 2 * my + mc
        mine = pltpu.make_async_copy(x_ref, out_ref.at[me], local_sem)
        mine.start()
        copies = []
        for k in range(1, 8):
            fx, fy, fc = (k >> 2) & 1, (k >> 1) & 1, k & 1
            px = 1 - mx if fx else mx
            py = 1 - my if fy else my
            pc = 1 - mc if fc else mc
            peer = 4 * px + 2 * py + pc
            send = pltpu.make_async_remote_copy(src_ref=x_ref, dst_ref=out_ref.at[me], send_sem=send_sems.at[k - 1],
                                                recv_sem=recv_sems.at[k - 1], device_id=(px, py, pc),
                                                device_id_type=MESH)
            send.start()
            recv = pltpu.make_async_remote_copy(src_ref=x_ref, dst_ref=out_ref.at[peer], send_sem=send_sems.at[k - 1],
                                                recv_sem=recv_sems.at[k - 1], device_id=(px, py, pc),
                                                device_id_type=MESH)
            copies.append((send, recv))
        for send, recv in copies:
            recv.wait_recv()
        for send, recv in copies:
            send.wait_send()
        mine.wait()

    vm = pl.BlockSpec(memory_space=pltpu.VMEM)
    return _pcall(body, name=name, in_specs=[vm], out_specs=vm, out_shape=_sds((8, r, c), x.dtype),
                  scratch=[pltpu.SemaphoreType.DMA((7,)), pltpu.SemaphoreType.DMA((7,)), pltpu.SemaphoreType.DMA])(x)


def _chip_flips(mx, my):
    out = []
    for fx, fy in ((1, 0), (0, 1), (1, 1)):
        px = 1 - mx if fx else mx
        py = 1 - my if fy else my
        out.append((px, py, 2 * px + py))
    return out


def _gather_chips(shard2, name):
    _, h, c = shard2.shape

    def body(x_ref, out_ref, send_sems, recv_sems):
        mx, my, mc = _my_pos()
        oc = 1 - mc
        mk = 2 * mx + my
        flips = _chip_flips(mx, my)

        def copy(k, src, dst, to):
            return pltpu.make_async_remote_copy(src_ref=src, dst_ref=dst, send_sem=send_sems.at[k],
                                                recv_sem=recv_sems.at[k], device_id=to, device_id_type=MESH)

        first = [copy(j, x_ref.at[mc], out_ref.at[mk, mc], (px, py, mc)) for j, (px, py, pk) in enumerate(flips)]
        for cp in first:
            cp.start()
        passed = []
        for j, (px, py, pk) in enumerate(flips):
            copy(j, x_ref.at[mc], out_ref.at[pk, mc], (px, py, mc)).wait_recv()
            fw = copy(3 + j, out_ref.at[pk, mc], out_ref.at[pk, mc], (mx, my, oc))
            fw.start()
            passed.append(fw)
        for j, (px, py, pk) in enumerate(flips):
            copy(3 + j, out_ref.at[pk, oc], out_ref.at[pk, oc], (mx, my, oc)).wait_recv()
        for cp in first + passed:
            cp.wait_send()

    return _pcall(body, name=name, in_specs=[HBM_SPEC], out_specs=HBM_SPEC, out_shape=_sds((4, 2, h, c), shard2.dtype),
                  scratch=[pltpu.SemaphoreType.DMA((6,)), pltpu.SemaphoreType.DMA((6,))])(shard2)


def _pair_exchange(g4, name):
    n, _, h, c = g4.shape

    def body(g_ref, out_ref, send_sem, recv_sem):
        mx, my, mc = _my_pos()
        oc = 1 - mc
        copies = []
        for k in range(n):
            cp = pltpu.make_async_remote_copy(src_ref=g_ref.at[k, oc], dst_ref=out_ref.at[k], send_sem=send_sem.at[k],
                                              recv_sem=recv_sem.at[k], device_id=(mx, my, oc), device_id_type=MESH)
            cp.start()
            copies.append(cp)
        for cp in copies:
            cp.wait_recv()
        for cp in copies:
            cp.wait_send()

    return _pcall(body, name=name, in_specs=[HBM_SPEC], out_specs=HBM_SPEC, out_shape=_sds((n, h, c), g4.dtype),
                  scratch=[pltpu.SemaphoreType.DMA((n,)), pltpu.SemaphoreType.DMA((n,))])(g4)


def _pair_add(g4, recv, core, name):
    n, _, h, c = g4.shape
    tm = _row_tile(h)

    def body(core_ref, a_ref, b_ref, o_ref, o16_ref):
        acc = a_ref[...] + b_ref[...]
        o_ref[...] = acc
        o16_ref[...] = acc.astype(BF16)

    out_spec = pl.BlockSpec((None, tm, c), lambda k, i, cr: (k, i, 0))
    return _pcall(body, name=name, grid=(n, h // tm), prefetch=1,
                  in_specs=[pl.BlockSpec((None, None, tm, c), lambda k, i, cr: (k, cr[0], i, 0)), out_spec],
                  out_specs=[out_spec, out_spec], out_shape=[_sds((n, h, c), F32), _sds((n, h, c), BF16)],
                  sem=("parallel", "parallel"))(core, g4, recv)


def _chip_exchange(p, name):
    n, h, c = p.shape

    def body(p_ref, out_ref, send_sems, recv_sems):
        mx, my, mc = _my_pos()
        copies = []
        for j, (px, py, pk) in enumerate(_chip_flips(mx, my)):
            cp = pltpu.make_async_remote_copy(src_ref=p_ref.at[pk], dst_ref=out_ref.at[j], send_sem=send_sems.at[j],
                                              recv_sem=recv_sems.at[j], device_id=(px, py, mc), device_id_type=MESH)
            cp.start()
            copies.append(cp)
        for cp in copies:
            cp.wait_recv()
        for cp in copies:
            cp.wait_send()

    return _pcall(body, name=name, in_specs=[HBM_SPEC], out_specs=HBM_SPEC, out_shape=_sds((3, h, c), p.dtype),
                  scratch=[pltpu.SemaphoreType.DMA((3,)), pltpu.SemaphoreType.DMA((3,))])(p)


def _chip_sum(p, slots, chip, name):
    _, h, c = p.shape
    tm = _row_tile(h)

    def body(chip_ref, p_ref, q_ref, o_ref):
        o_ref[...] = ((p_ref[...] + q_ref[0].astype(F32)) + q_ref[1].astype(F32)) + q_ref[2].astype(F32)

    return _pcall(body, name=name, grid=(h // tm,), prefetch=1,
                  in_specs=[pl.BlockSpec((None, tm, c), lambda i, ch: (ch[0], i, 0)),
                            pl.BlockSpec((3, tm, c), lambda i, ch: (0, i, 0))],
                  out_specs=pl.BlockSpec((tm, c), lambda i, ch: (i, 0)),
                  out_shape=_sds((h, c), F32), sem=("parallel",))(chip, p, slots)


def _sum_slots(q, name):
    n, h, c = q.shape
    tm = _row_tile(h)

    def body(q_ref, o_ref):
        acc = q_ref[0]
        for k in range(1, n):
            acc = acc + q_ref[k]
        o_ref[...] = acc

    return _pcall(body, name=name, grid=(h // tm,),
                  in_specs=[pl.BlockSpec((n, tm, c), lambda i: (0, i, 0))],
                  out_specs=pl.BlockSpec((tm, c), lambda i: (i, 0)),
                  out_shape=_sds((h, c), F32), sem=("parallel",))(q)


def _pair_share(f, name):
    h, c = f.shape

    def body(f_ref, out_ref, send_sem, recv_sem):
        mx, my, mc = _my_pos()
        cp = pltpu.make_async_remote_copy(src_ref=f_ref, dst_ref=out_ref, send_sem=send_sem, recv_sem=recv_sem,
                                          device_id=(mx, my, 1 - mc), device_id_type=MESH)
        cp.start()
        cp.wait_recv()
        cp.wait_send()

    return _pcall(body, name=name, in_specs=[HBM_SPEC], out_specs=HBM_SPEC, out_shape=_sds((h, c), f.dtype),
                  scratch=[pltpu.SemaphoreType.DMA, pltpu.SemaphoreType.DMA])(f)


BIG = ("mlp_w1", "mlp_w2", "fox_w_in", "fox_w_o", "ssm_w_in", "ssm_w_out")
SMALL_SHARDED = ("ssm_conv_w", "ssm_conv_b", "ssm_norm_w")
PACK_COLS = 1024


def _pack_rows(parts, rows_multiple, dtype):
    flat = jnp.concatenate([p.reshape(-1).astype(dtype) for p in parts])
    unit = rows_multiple * PACK_COLS
    total = -(-flat.shape[0] // unit) * unit
    flat = jnp.pad(flat, (0, total - flat.shape[0]))
    return flat.reshape(total // PACK_COLS, PACK_COLS)


def _unpack(flat, shapes):
    out, off = [], 0
    for sh in shapes:
        n = 1
        for d_ in sh:
            n *= d_
        out.append(flat[off:off + n].reshape(sh))
        off += n
    return out


PIECE_ROWS = 16


def _piece_rows(shape):
    n = 1
    for d_ in shape:
        n *= d_
    rows = -(-n // PACK_COLS)
    return n, -(-rows // PIECE_ROWS) * PIECE_ROWS


def _pack2d(parts, rows_multiple, dtype):
    blocks = []
    for p in parts:
        n, rows = _piece_rows(p.shape)
        a = p.astype(dtype)
        if p.shape[-1] != PACK_COLS or n % PACK_COLS:
            a = jnp.pad(a.reshape(-1), (0, -n % PACK_COLS))
        a = a.reshape(-1, PACK_COLS)
        blocks.append(jnp.pad(a, ((0, rows - a.shape[0]), (0, 0))))
    total = sum(b.shape[0] for b in blocks)
    pad = -total % rows_multiple
    if pad:
        blocks.append(jnp.zeros((pad, PACK_COLS), dtype))
    return jnp.concatenate(blocks, axis=0)


def _unpack2d(buf, shapes):
    out, off = [], 0
    for sh in shapes:
        n, rows = _piece_rows(sh)
        piece = buf[off:off + rows]
        if sh[-1] == PACK_COLS and n % PACK_COLS == 0:
            out.append(piece[:n // PACK_COLS].reshape(sh))
        else:
            out.append(piece.reshape(-1)[:n].reshape(sh))
        off += rows
    return out


def _row_tile(h, cap=512):
    for step in (16, 8):
        best = 0
        for cand in range(step, cap + 1, step):
            if h % cand == 0:
                best = cand
        if best:
            return best
    return h


def _chip_slice(full, axis, k, width):
    idx = [slice(None)] * full.ndim
    idx[axis] = slice(k * width, (k + 1) * width)
    return full[tuple(idx)]


SHARD_AXIS = dict(mlp_w1=2, mlp_w2=1, fox_w_in=2, fox_w_o=1, ssm_w_in=2, ssm_w_out=1, ssm_conv_w=2,
                  ssm_conv_b=1, ssm_norm_w=1, ada_w=2)


def kernel(x, c, ada_w, ada_b, ln_mix_g, ln_mix_b, ln_mlp_g, ln_mlp_b, mlp_w1, mlp_w2, fox_w_in, fox_b_f, fox_w_o, ssm_w_in, ssm_conv_w, ssm_conv_b, ssm_dt_bias, ssm_a_log, ssm_d, ssm_norm_w, ssm_w_out, loss_target, m_ada_w, m_ada_b, m_ln_mix_g, m_ln_mix_b, m_ln_mlp_g, m_ln_mlp_b, m_mlp_w1, m_mlp_w2, m_fox_w_in, m_fox_b_f, m_fox_w_o, m_ssm_w_in, m_ssm_conv_w, m_ssm_conv_b, m_ssm_dt_bias, m_ssm_a_log, m_ssm_d, m_ssm_norm_w, m_ssm_w_out, v_ada_w, v_ada_b, v_ln_mix_g, v_ln_mix_b, v_ln_mlp_g, v_ln_mlp_b, v_mlp_w1, v_mlp_w2, v_fox_w_in, v_fox_b_f, v_fox_w_o, v_ssm_w_in, v_ssm_conv_w, v_ssm_conv_b, v_ssm_dt_bias, v_ssm_a_log, v_ssm_d, v_ssm_norm_w, v_ssm_w_out):
    names = ("ada_w", "ada_b", "ln_mix_g", "ln_mix_b", "ln_mlp_g", "ln_mlp_b", "mlp_w1", "mlp_w2", "fox_w_in",
             "fox_b_f", "fox_w_o", "ssm_w_in", "ssm_conv_w", "ssm_conv_b", "ssm_dt_bias", "ssm_a_log", "ssm_d",
             "ssm_norm_w", "ssm_w_out")
    weights = dict(zip(names, (ada_w, ada_b, ln_mix_g, ln_mix_b, ln_mlp_g, ln_mlp_b, mlp_w1, mlp_w2, fox_w_in,
                               fox_b_f, fox_w_o, ssm_w_in, ssm_conv_w, ssm_conv_b, ssm_dt_bias, ssm_a_log, ssm_d,
                               ssm_norm_w, ssm_w_out)))
    m_in = dict(zip(names, (m_ada_w, m_ada_b, m_ln_mix_g, m_ln_mix_b, m_ln_mlp_g, m_ln_mlp_b, m_mlp_w1, m_mlp_w2,
                            m_fox_w_in, m_fox_b_f, m_fox_w_o, m_ssm_w_in, m_ssm_conv_w, m_ssm_conv_b, m_ssm_dt_bias,
                            m_ssm_a_log, m_ssm_d, m_ssm_norm_w, m_ssm_w_out)))
    v_in = dict(zip(names, (v_ada_w, v_ada_b, v_ln_mix_g, v_ln_mix_b, v_ln_mlp_g, v_ln_mlp_b, v_mlp_w1, v_mlp_w2,
                            v_fox_w_in, v_fox_b_f, v_fox_w_o, v_ssm_w_in, v_ssm_conv_w, v_ssm_conv_b, v_ssm_dt_bias,
                            v_ssm_a_log, v_ssm_d, v_ssm_norm_w, v_ssm_w_out)))

    mx, my, mc = _my_pos()
    chip = 2 * mx + my
    me = 4 * mx + 2 * my + mc
    x0 = x[0]
    target = loss_target[0]
    s, d = x0.shape
    n_qkv = 3 * FOX_HEADS * HEAD_DIM

    big_shapes = [weights[n].shape for n in BIG]
    packed = _pack2d([weights[n] for n in BIG], 32, BF16)
    gathered = _gather_chips(packed.reshape(2, packed.shape[0] // 2, PACK_COLS), "gather_weights")
    gathered = gathered.reshape(4, packed.shape[0], PACK_COLS)
    gathered = lax.dynamic_update_slice(gathered, packed[None], (chip, 0, 0))
    per_chip = [_unpack2d(gathered[k], big_shapes) for k in range(4)]
    full = {n: jnp.concatenate([per_chip[k][i] for k in range(4)], axis=SHARD_AXIS[n]) for i, n in enumerate(BIG)}

    small_shapes = [weights[n].shape for n in SMALL_SHARDED]
    small_packed = _pack_rows([weights[n] for n in SMALL_SHARDED] + [c], 8, F32).reshape(-1, LANES)
    small_all = _allgather8(small_packed, "gather_small")
    small_chip = [_unpack(small_all[2 * k].reshape(-1), small_shapes) for k in range(4)]
    small_full = {n: jnp.concatenate([small_chip[k][i] for k in range(4)], axis=SHARD_AXIS[n])
                  for i, n in enumerate(SMALL_SHARDED)}
    n_small = sum(weights[n].size for n in SMALL_SHARDED)
    c_all = small_all.reshape(8, -1)[:, n_small:n_small + d]

    cols = ada_w.shape[2]
    ada_b_cols = lax.dynamic_slice_in_dim(ada_b, chip * cols, cols, axis=1)[:, None, :]
    c16 = jnp.pad(c_all, ((0, 8), (0, 0)))
    mod_part = _ada_forward(c16, ada_w, ada_b_cols, "ada_fwd")[:, :8, :]
    mod_all = _allgather8(mod_part.reshape(-1, LANES), "gather_mod").reshape(8, DEPTH, 8, cols)
    mod_mine = jnp.stack([lax.dynamic_index_in_dim(mod_all[2 * k], me, axis=1, keepdims=False) for k in range(4)], axis=1)
    mods = [jnp.pad(mod_mine[i].reshape(6, d), ((0, 2), (0, 0))) for i in range(DEPTH)]

    w = dict(
        fox_qkv=full["fox_w_in"][0][:, :n_qkv],
        fox_f=jnp.pad(full["fox_w_in"][0][:, n_qkv:], ((0, 0), (0, LANES - FOX_HEADS))),
        fox_o=full["fox_w_o"][0],
        fox_bf=jnp.pad(fox_b_f, ((0, 0), (0, LANES - FOX_HEADS))),
        head_ind=jnp.asarray(np.arange(d)[:, None] // HEAD_DIM == np.arange(LANES)[None, :], BF16),
    )
    w.update(_ssm_weights(full["ssm_w_in"][0], small_full["ssm_conv_w"][0], small_full["ssm_conv_b"], ssm_dt_bias,
                          ssm_a_log, ssm_d, small_full["ssm_norm_w"], full["ssm_w_out"][0]))
    mixers = ((_fox_forward, _fox_backward), (_ssd_forward, _ssd_backward))

    saved = []
    xin = x0
    u = _modulate_in(x0, mods[0], "l0_mod_in")
    for i in range(DEPTH):
        tag = "l%d" % i
        y, sv = mixers[i % 2][0](u, w)
        r, x1, u2 = _res_ln_mod(xin, y, mods[i], ln_mix_g[i:i + 1], ln_mix_b[i:i + 1], tag + "_res_ln1")
        y2, hr, a = _mlp_forward(u2, full["mlp_w1"][i], full["mlp_w2"][i], tag)
        if i + 1 < DEPTH:
            r2, xin, u = _res_ln_next(x1, y2, mods[i], ln_mlp_g[i:i + 1], ln_mlp_b[i:i + 1], mods[i + 1],
                                      tag + "_res_ln2")
        else:
            r2 = None
        saved.append(dict(y=y, r=r, u2=u2, hr=hr, a=a, y2=y2, r2=r2, x1=x1, mix=sv))

    grads = {}
    dmod_parts = [dict() for _ in range(DEPTH)]
    ln_grads = {n: [None] * DEPTH for n in ("ln_mix_g", "ln_mix_b", "ln_mlp_g", "ln_mlp_b")}
    g_w1, g_w2 = [None] * DEPTH, [None] * DEPTH
    du = dres0 = None
    for i in reversed(range(DEPTH)):
        tag = "l%d" % i
        sv = saved[i]
        if i + 1 == DEPTH:
            dy2, dres, sq, dg2, db2, dgm = _loss_ln2_bwd(sv["x1"], sv["y2"], target, mods[i], ln_mlp_g[i:i + 1],
                                                         ln_mlp_b[i:i + 1], "loss_ln2_bwd")
            loss = lax.psum(0.5 * jnp.sum(sq) / d, ("x", "y", "c"))
        else:
            dy2, dres, dsca, dsha, dg2, db2, dgm = _mod_in_ln2_bwd(du, dres0, sv["r2"], sv["y2"], mods[i],
                                                                   ln_mlp_g[i:i + 1], ln_mlp_b[i:i + 1], mods[i + 1],
                                                                   tag + "_ln2_bwd")
            dmod_parts[i + 1].update(sc_a=dsca, sh_a=dsha)
        du2, g_w1[i], g_w2[i] = _mlp_backward(dy2, sv["u2"], sv["hr"], sv["a"], full["mlp_w1"][i], full["mlp_w2"][i], tag)
        dy, dres0, dscm, dshm, dg1, db1, dga = _ln1_bwd(du2, dres, sv["r"], sv["y"], mods[i], ln_mix_g[i:i + 1],
                                                        ln_mix_b[i:i + 1], tag + "_ln1_bwd")
        du, mg = mixers[i % 2][1](dy, sv["mix"], w)
        grads.update(mg)
        dmod_parts[i].update(g_a=dga, sh_m=dshm, sc_m=dscm, g_m=dgm)
        ln_grads["ln_mix_g"][i], ln_grads["ln_mix_b"][i] = dg1, db1
        ln_grads["ln_mlp_g"][i], ln_grads["ln_mlp_b"][i] = dg2, db2
    dx, dsca, dsha = _mod_in_bwd(du, dres0, x0, mods[0], "l0_mod_in_bwd")
    dmod_parts[0].update(sc_a=dsca, sh_a=dsha)
    dmods = [jnp.concatenate([p["sh_a"], p["sc_a"], p["g_a"], p["sh_m"], p["sc_m"], p["g_m"]], axis=1)
             for p in dmod_parts]
    grad_x = dx[None]
    grads["mlp_w1"] = jnp.stack(g_w1)
    grads["mlp_w2"] = jnp.stack(g_w2)
    for n in ("fox_w_in", "fox_w_o", "ssm_w_in", "ssm_w_out", "ssm_conv_w"):
        grads[n] = grads[n][None]

    small_names = ("ln_mix_g", "ln_mix_b", "ln_mlp_g", "ln_mlp_b", "fox_b_f", "ssm_dt_bias", "ssm_a_log", "ssm_d")
    small_parts = list(dmods)
    for n in small_names[:4]:
        small_parts.append(jnp.concatenate(ln_grads[n], axis=0))
    for n in small_names[4:]:
        small_parts.append(jnp.pad(grads[n], ((0, 0), (0, LANES - grads[n].shape[1]))))
    small_vec = _pack_rows(small_parts, 1, F32).reshape(-1, LANES)
    small_vec = jnp.pad(small_vec, ((0, -small_vec.shape[0] % 8), (0, 0)))
    small_g_all = _allgather8(small_vec, "gather_small_grads")
    small_sum = _sum_slots(small_g_all, "sum_small_grads").reshape(-1)
    dmod_sum = small_sum[:DEPTH * 6 * d].reshape(DEPTH, 6 * d)
    off = DEPTH * 6 * d
    final = {"ada_b": dmod_sum}
    for n in small_names[:4]:
        final[n] = small_sum[off:off + DEPTH * d].reshape(DEPTH, d)
        off += DEPTH * d
    for n in small_names[4:]:
        width = weights[n].shape[1]
        final[n] = small_sum[off:off + width].reshape(1, width)
        off += LANES

    dmod_all = small_g_all.reshape(8, -1)[:, :DEPTH * 6 * d].reshape(8, DEPTH, 6 * d)
    dmod_cols = lax.dynamic_slice_in_dim(dmod_all, chip * cols, cols, axis=2).transpose(1, 0, 2)
    final["ada_w"] = _ada_backward(c_all.T, dmod_cols, "ada_bwd")

    sharded = BIG + SMALL_SHARDED
    shard_shapes = [weights[n].shape for n in sharded]
    per_target = []
    for k in range(4):
        parts = [_chip_slice(grads[n], SHARD_AXIS[n], k, weights[n].shape[SHARD_AXIS[n]]) for n in sharded]
        per_target.append(_pack2d(parts, 128, F32))
    g_all = jnp.stack(per_target)
    rows = g_all.shape[1]
    g4 = g_all.reshape(4, 2, rows // 2, PACK_COLS)
    recv = _pair_exchange(g4, "rs_pair_exchange")
    part, part16 = _pair_add(g4, recv, jnp.reshape(mc, (1,)).astype(jnp.int32), "rs_pair_add")
    slots = _chip_exchange(part16, "rs_chip_exchange")
    half = _chip_sum(part, slots, jnp.reshape(chip, (1,)).astype(jnp.int32), "rs_chip_sum")
    other = _pair_share(half, "rs_pair_share")
    both = jnp.concatenate([jnp.where(mc == 0, half, other), jnp.where(mc == 0, other, half)], axis=0)
    for n, g_shard in zip(sharded, _unpack2d(both, shard_shapes)):
        final[n] = g_shard

    outs_g, outs_d, outs_m, outs_v = [], [], [], []
    for n in names:
        wv = weights[n]
        two_d = (-1, wv.shape[-1])
        delta, mn, vn = _adamw(wv.reshape(two_d), final[n].reshape(two_d), m_in[n].reshape(two_d),
                               v_in[n].reshape(two_d), "adamw_" + n)
        outs_g.append(final[n].reshape(wv.shape))
        outs_d.append(delta.reshape(wv.shape))
        outs_m.append(mn.reshape(wv.shape))
        outs_v.append(vn.reshape(wv.shape))
    return (loss, grad_x, *outs_g, *outs_d, *outs_m, *outs_v)
```

```python
import functools

import jax
import jax.numpy as jnp
import numpy as np
from jax import lax
from jax.experimental import pallas as pl
from jax.experimental.pallas import tpu as pltpu

F32, BF16 = jnp.float32, jnp.bfloat16
MESH = pl.DeviceIdType.MESH
HBM_SPEC = pl.BlockSpec(memory_space=pltpu.HBM)

VMEM_LIMIT_BYTES = 52 * 2**20
LANES = 128

FOX_HEADS, HEAD_DIM = 16, 64
SSM_HEADS, SSM_GROUPS, SSM_STATE, SSM_CHUNK, SSM_CONV = 32, 8, 128, 128, 4
SSM_GROUP_WIDTH = 256
LN_EPS, RMS_EPS = 1e-5, 1e-5
DEPTH = 2
ALPHA = (2.0 * DEPTH) ** 0.25
ADAM_LR, ADAM_B1, ADAM_B2, ADAM_EPS, ADAM_WD, ADAM_STEP = 0.001, 0.9, 0.999, 1e-08, 0.01, 10

ATT_TILE = 512
ROW_TILE = 512
ROW_BUFFERS = 3
SCAN_TILE = 512
CONV_CHUNK = 64
SSD_GROUPS_PER_STEP = 8
MM_TM, MM_TN, MM_TK = 1024, 1024, 1024

NT_DIMS = (((1,), (1,)), ((), ()))
TN_DIMS = (((0,), (0,)), ((), ()))
NN_DIMS = (((1,), (0,)), ((), ()))


def _pcall(body, *, name, out_shape, grid=(), in_specs=None, out_specs=None, scratch=(), sem=None, prefetch=0):
    params = dict(vmem_limit_bytes=VMEM_LIMIT_BYTES)
    if sem is not None:
        params["dimension_semantics"] = sem
    if prefetch:
        grid_spec = pltpu.PrefetchScalarGridSpec(num_scalar_prefetch=prefetch, grid=grid, in_specs=in_specs,
                                                 out_specs=out_specs, scratch_shapes=scratch)
        return pl.pallas_call(body, out_shape=out_shape, grid_spec=grid_spec, name=name,
                              compiler_params=pltpu.CompilerParams(**params))
    kwargs = {}
    if in_specs is not None:
        kwargs["in_specs"] = in_specs
    if out_specs is not None:
        kwargs["out_specs"] = out_specs
    return pl.pallas_call(body, out_shape=out_shape, grid=grid, scratch_shapes=scratch, name=name,
                          compiler_params=pltpu.CompilerParams(**params), **kwargs)


def _sds(shape, dtype):
    return jax.ShapeDtypeStruct(tuple(shape), dtype)


def _dot(a, b, dims=NN_DIMS):
    return lax.dot_general(a, b, dims, preferred_element_type=F32)


def _sigmoid(x):
    return 1.0 / (1.0 + jnp.exp(-x))


def _silu(x):
    return x * _sigmoid(x)


def _dsilu(x):
    s = _sigmoid(x)
    return s * (1.0 + x * (1.0 - s))


def _dot_split(x, m16, dims=NN_DIMS, passes=3):
    hi = x.astype(BF16)
    r1 = x - hi.astype(F32)
    mid = r1.astype(BF16)
    out = _dot(hi, m16, dims) + _dot(mid, m16, dims)
    if passes == 3:
        lo = (r1 - mid.astype(F32)).astype(BF16)
        out = out + _dot(lo, m16, dims)
    return out


def _mm(a, b, dims, outs, *, name, tm=MM_TM, tn=MM_TN, tk=MM_TK, epi=None, extra=()):
    if dims == "nn":
        (m, k), (k2, n) = a.shape, b.shape
    elif dims == "nt":
        (m, k), (n, k2) = a.shape, b.shape
    else:
        (k, m), (k2, n) = a.shape, b.shape
    assert k == k2, (a.shape, b.shape, dims)
    tm, tn, tk = min(tm, m), min(tn, n), min(tk, k)
    assert m % tm == 0 and n % tn == 0 and k % tk == 0, (m, n, k, tm, tn, tk)
    nk = k // tk
    dn = {"nn": NN_DIMS, "nt": NT_DIMS, "tn": TN_DIMS}[dims]
    n_extra, n_out = len(extra), len(outs)
    if epi is None:
        epi = lambda acc: (acc,) * n_out

    def body(a_ref, b_ref, *rest):
        extra_refs, out_refs, acc_ref = rest[:n_extra], rest[n_extra:n_extra + n_out], rest[-1]
        kk = pl.program_id(2)

        @pl.when(kk == 0)
        def _():
            acc_ref[...] = jnp.zeros_like(acc_ref)

        acc_ref[...] += _dot(a_ref[...].astype(BF16), b_ref[...].astype(BF16), dn)

        @pl.when(kk == nk - 1)
        def _():
            res = epi(acc_ref[...], *[e[...] for e in extra_refs])
            for o, r in zip(out_refs, res):
                o[...] = r.astype(o.dtype)

    if dims == "tn":
        a_spec = pl.BlockSpec((tk, tm), lambda i, j, kk: (kk, i))
    else:
        a_spec = pl.BlockSpec((tm, tk), lambda i, j, kk: (i, kk))
    if dims == "nt":
        b_spec = pl.BlockSpec((tn, tk), lambda i, j, kk: (j, kk))
    else:
        b_spec = pl.BlockSpec((tk, tn), lambda i, j, kk: (kk, j))
    o_spec = pl.BlockSpec((tm, tn), lambda i, j, kk: (i, j))
    res = _pcall(body, name=name, grid=(m // tm, n // tn, nk),
                 in_specs=[a_spec, b_spec] + [o_spec] * n_extra,
                 out_specs=[o_spec] * n_out,
                 out_shape=[_sds((m, n), d) for d in outs],
                 scratch=[pltpu.VMEM((tm, tn), F32)],
                 sem=("parallel", "parallel", "arbitrary"))(a, b, *extra)
    return res[0] if n_out == 1 else res


def _k_blocks(a, tk=None):
    tk = tk or MM_TK
    return [(a, kb) for kb in range(a.shape[1] // tk)]


def _mm_nt_blocks(a_blocks, b, start, *, name, tm=MM_TM, tk=None):
    tk = tk or MM_TK
    tm = min(tm, a_blocks[0][0].shape[0])
    m, n, p_n = a_blocks[0][0].shape[0], b.shape[0], len(a_blocks)
    assert b.shape[1] == p_n * tk and m % tm == 0

    def body(*refs):
        a_refs = refs[:p_n]
        b_ref, s_ref, o_ref, acc_ref = refs[p_n:]
        kk = pl.program_id(1)

        @pl.when(kk == 0)
        def _():
            acc_ref[...] = s_ref[...]
        for p in range(p_n):
            @pl.when(kk == p)
            def _(p=p):
                acc_ref[...] += _dot(a_refs[p][...].astype(BF16), b_ref[...].astype(BF16), NT_DIMS)

        @pl.when(kk == p_n - 1)
        def _():
            o_ref[...] = acc_ref[...]

    in_specs = [pl.BlockSpec((tm, tk), functools.partial(lambda kb, i, kk: (i, kb), kb)) for _, kb in a_blocks]
    in_specs += [pl.BlockSpec((n, tk), lambda i, kk: (0, kk)), pl.BlockSpec((tm, n), lambda i, kk: (i, 0))]
    return _pcall(body, name=name, grid=(m // tm, p_n), in_specs=in_specs,
                  out_specs=pl.BlockSpec((tm, n), lambda i, kk: (i, 0)), out_shape=_sds((m, n), F32),
                  scratch=[pltpu.VMEM((tm, n), F32)], sem=("parallel", "arbitrary"))(*[a for a, _ in a_blocks], b, start)


def _rowwise(fn, rows, consts, row_outs, acc_outs, *, name, tm=ROW_TILE):
    s = rows[0].shape[0]
    tm = min(tm, s)
    assert s % tm == 0
    n_rows, n_in, n_o = len(rows), len(rows) + len(consts), len(row_outs)
    n_out = n_o + len(acc_outs)
    n = s // tm
    nb = min(ROW_BUFFERS, n)

    def body(*refs):
        hbm, consts_r = refs[:n_rows], refs[n_rows:n_in]
        outs = refs[n_in:n_in + n_out]
        bufs, sems = refs[n_in + n_out:n_in + n_out + n_rows], refs[n_in + n_out + n_rows:]
        i = pl.program_id(0)

        def copies(t, static=False):
            start = t * tm if static else pl.multiple_of(t * tm, tm)
            return [pltpu.make_async_copy(hbm[k].at[pl.ds(start, tm), :], bufs[k].at[t % nb], sems[k].at[t % nb])
                    for k in range(n_rows)]

        @pl.when(i == 0)
        def _():
            for t in range(nb - 1):
                for cp in copies(t, static=True):
                    cp.start()

        @pl.when(i + (nb - 1) < n)
        def _():
            for cp in copies(i + (nb - 1)):
                cp.start()
        for cp in copies(i):
            cp.wait()
        slot = i % nb
        res = fn(*[b[slot] for b in bufs], *[r[...] for r in consts_r])
        if not isinstance(res, (tuple, list)):
            res = (res,)
        for o, val in zip(outs[:n_o], res[:n_o]):
            o[...] = val.astype(o.dtype)
        if acc_outs:
            @pl.when(i == 0)
            def _():
                for o in outs[n_o:]:
                    o[...] = jnp.zeros_like(o)
            for o, val in zip(outs[n_o:], res[n_o:]):
                o[...] += val

    in_specs = [pl.BlockSpec(memory_space=pl.ANY) for _ in rows]
    in_specs += [pl.BlockSpec(c.shape, functools.partial(lambda nd, i: (0,) * nd, c.ndim)) for c in consts]
    out_specs = [pl.BlockSpec((tm, c), lambda i: (i, 0)) for c, _ in row_outs]
    out_specs += [pl.BlockSpec(tuple(sh), lambda i: (0, 0)) for sh in acc_outs]
    out_shape = [_sds((s, c), d) for c, d in row_outs] + [_sds(sh, F32) for sh in acc_outs]
    scratch = [pltpu.VMEM((nb, tm, r.shape[1]), r.dtype) for r in rows] + [pltpu.SemaphoreType.DMA((nb,)) for _ in rows]
    res = _pcall(body, name=name, grid=(n,), in_specs=in_specs, out_specs=out_specs,
                 out_shape=out_shape, scratch=scratch, sem=("arbitrary",))(*rows, *consts)
    return res


def _colsum(x):
    return jnp.sum(x, axis=0, keepdims=True)


def _ln_stats(r):
    mu = jnp.mean(r, axis=-1, keepdims=True)
    xc = r - mu
    var = jnp.mean(xc * xc, axis=-1, keepdims=True)
    rstd = lax.rsqrt(var + LN_EPS)
    return xc * rstd, rstd


def _ln_bwd(dy, xhat, rstd, gamma):
    dyg = dy * gamma
    m1 = jnp.mean(dyg, axis=-1, keepdims=True)
    m2 = jnp.mean(dyg * xhat, axis=-1, keepdims=True)
    return rstd * (dyg - m1 - xhat * m2)


def _modulate_in(x, mod, name):
    def fn(xv, m):
        return (xv * (1.0 + m[1:2]) + m[0:1],)
    return _rowwise(fn, [x], [mod], [(x.shape[1], BF16)], [], name=name)[0]


def _res_ln_mod(x, y, mod, g, b, name):
    d = x.shape[1]

    def fn(xv, yv, m, gv, bv):
        r = ALPHA * xv + (1.0 + m[2:3]) * yv
        xhat, _ = _ln_stats(r)
        x1 = xhat * gv + bv
        u2 = x1 * (1.0 + m[4:5]) + m[3:4]
        return r, x1, u2
    return _rowwise(fn, [x, y], [mod, g, b], [(d, F32), (d, F32), (d, BF16)], [], name=name)


def _res_ln_next(x, y, mod, g, b, mod_next, name):
    d = x.shape[1]

    def fn(xv, yv, m, gv, bv, mn):
        r = ALPHA * xv + (1.0 + m[5:6]) * yv
        xhat, _ = _ln_stats(r)
        out = xhat * gv + bv
        return r, out, out * (1.0 + mn[1:2]) + mn[0:1]
    return _rowwise(fn, [x, y], [mod, g, b, mod_next], [(d, F32), (d, F32), (d, BF16)], [], name=name)


def _loss_ln2_bwd(x1, y2, target, mod, g, b, name):
    d = x1.shape[1]

    def fn(xv, yv, tv, m, gv, bv):
        rv = ALPHA * xv + (1.0 + m[5:6]) * yv
        xhat, rstd = _ln_stats(rv)
        e = xhat * gv + bv - tv
        dxv = e * (1.0 / d)
        dr = _ln_bwd(dxv, xhat, rstd, gv)
        return (dr * (1.0 + m[5:6]), ALPHA * dr,
                _colsum(e * e), _colsum(dxv * xhat), _colsum(dxv), _colsum(dr * yv))
    return _rowwise(fn, [x1, y2, target], [mod, g, b], [(d, BF16), (d, F32)], [(1, d)] * 4, name=name)


def _mod_in_ln2_bwd(du, dres, r2, y2, mod, g, b, mod_next, name):
    d = du.shape[1]

    def fn(duv, drv, rv, yv, m, gv, bv, mn):
        xhat, rstd = _ln_stats(rv)
        xout = xhat * gv + bv
        dxv = duv * (1.0 + mn[1:2]) + drv
        dr = _ln_bwd(dxv, xhat, rstd, gv)
        return (dr * (1.0 + m[5:6]), ALPHA * dr,
                _colsum(duv * xout), _colsum(duv), _colsum(dxv * xhat), _colsum(dxv), _colsum(dr * yv))
    return _rowwise(fn, [du, dres, r2, y2], [mod, g, b, mod_next], [(d, BF16), (d, F32)], [(1, d)] * 5, name=name)


def _ln1_bwd(du2, dres, r, y, mod, g, b, name):
    d = du2.shape[1]

    def fn(duv, drv, rv, yv, m, gv, bv):
        xhat, rstd = _ln_stats(rv)
        x1 = xhat * gv + bv
        dx1 = duv * (1.0 + m[4:5]) + drv
        dr = _ln_bwd(dx1, xhat, rstd, gv)
        return (dr * (1.0 + m[2:3]), ALPHA * dr,
                _colsum(duv * x1), _colsum(duv), _colsum(dx1 * xhat), _colsum(dx1), _colsum(dr * yv))
    return _rowwise(fn, [du2, dres, r, y], [mod, g, b], [(d, BF16), (d, F32)], [(1, d)] * 5, name=name)


def _mod_in_bwd(du, dres, x, mod, name):
    d = du.shape[1]

    def fn(duv, drv, xv, m):
        return duv * (1.0 + m[1:2]) + drv, _colsum(duv * xv), _colsum(duv)
    return _rowwise(fn, [du, dres, x], [mod], [(d, F32)], [(1, d)] * 2, name=name)


def _fox_gate(fraw, b_pad, name):
    s = fraw.shape[0]
    tb = min(SCAN_TILE, s)

    def body(f_ref, b_ref, cum_ref, rows_ref, carry):
        @pl.when(pl.program_id(0) == 0)
        def _():
            carry[...] = jnp.zeros_like(carry)
        z = f_ref[...] + b_ref[...]
        lf = jnp.minimum(z, 0.0) - jnp.log(1.0 + jnp.exp(-jnp.abs(z)))
        lane = lax.broadcasted_iota(jnp.int32, (tb, LANES), 1)
        row = lax.broadcasted_iota(jnp.int32, (tb, LANES), 0)
        c = jnp.where(lane < FOX_HEADS, lf, 0.0)
        sh = 1
        while sh < tb:
            c = c + jnp.where(row >= sh, pltpu.roll(c, sh, 0), 0.0)
            sh *= 2
        c = c + carry[0:1, :]
        cum_ref[...] = c
        rows_ref[...] = c.T[0:FOX_HEADS, :]
        carry[0:1, :] = c[tb - 1:tb, :]

    return _pcall(body, name=name, grid=(s // tb,),
                  in_specs=[pl.BlockSpec((tb, LANES), lambda i: (i, 0)), pl.BlockSpec((1, LANES), lambda i: (0, 0))],
                  out_specs=[pl.BlockSpec((tb, LANES), lambda i: (i, 0)), pl.BlockSpec((FOX_HEADS, tb), lambda i: (0, i))],
                  out_shape=[_sds((s, LANES), F32), _sds((FOX_HEADS, s), F32)], scratch=[pltpu.VMEM((8, LANES), F32)],
                  sem=("arbitrary",))(fraw, b_pad)


def _fox_gate_bwd(drow, dcol, fraw, b_pad, name):
    s = fraw.shape[0]
    tb = min(SCAN_TILE, s)
    n = s // tb

    def body(dr_ref, dc_ref, f_ref, b_ref, df_ref, db_ref, carry):
        @pl.when(pl.program_id(0) == 0)
        def _():
            carry[...] = jnp.zeros_like(carry)
            db_ref[...] = jnp.zeros_like(db_ref)
        row = lax.broadcasted_iota(jnp.int32, (tb, LANES), 0)
        c = dr_ref[...] + dc_ref[...]
        sh = 1
        while sh < tb:
            c = c + jnp.where(row + sh < tb, pltpu.roll(c, tb - sh, 0), 0.0)
            sh *= 2
        c = c + carry[0:1, :]
        carry[0:1, :] = c[0:1, :]
        z = f_ref[...] + b_ref[...]
        df = c * (1.0 / (1.0 + jnp.exp(z)))
        df_ref[...] = df.astype(df_ref.dtype)
        db_ref[...] += _colsum(df)

    rev = lambda i: (n - 1 - i, 0)
    return _pcall(body, name=name, grid=(n,),
                  in_specs=[pl.BlockSpec((tb, LANES), rev)] * 3 + [pl.BlockSpec((1, LANES), lambda i: (0, 0))],
                  out_specs=[pl.BlockSpec((tb, LANES), rev), pl.BlockSpec((1, LANES), lambda i: (0, 0))],
                  out_shape=[_sds((s, LANES), BF16), _sds((1, LANES), F32)],
                  scratch=[pltpu.VMEM((8, LANES), F32)], sem=("arbitrary",))(drow, dcol, fraw, b_pad)


def _head_pair_masks(t):
    lane = lax.broadcasted_iota(jnp.int32, (t, LANES), 1)
    return lane < HEAD_DIM


def _lane_blocks(x):
    return [x[:, c * LANES:(c + 1) * LANES] for c in range(x.shape[1] // LANES)]


def _sum_list(xs):
    acc = xs[0]
    for x in xs[1:]:
        acc = acc + x
    return acc


def _causal(t, transposed=False):
    ri = lax.broadcasted_iota(jnp.int32, (t, t), 0)
    ci = lax.broadcasted_iota(jnp.int32, (t, t), 1)
    return ci >= ri if transposed else ri >= ci


def _span_mask(r0, r1, c0, c1, transposed=False):
    ri = lax.broadcasted_iota(jnp.int32, (r1 - r0, c1 - c0), 0) + r0
    ci = lax.broadcasted_iota(jnp.int32, (r1 - r0, c1 - c0), 1) + c0
    return ci >= ri if transposed else ri >= ci


def _full_spans(t):
    return ((0, t, 0, t, False),)


def _diagonal_spans(t, transposed=False):
    h = t // 2
    if h % LANES:
        return ((0, t, 0, t, True),)
    if transposed:
        return ((0, h, 0, t, True), (h, t, h, t, True))
    return ((0, h, 0, h, True), (h, t, 0, t, True))


def _flash_fwd(qkv, ck_rows, kb_start, name):
    s = qkv.shape[0]
    t = min(ATT_TILE, s)
    nq = s // t
    scale = HEAD_DIM ** -0.5
    hp_blocks = FOX_HEADS // 2

    def body(ks_ref, q_ref, k_ref, v_ref, ck_ref, o_ref, lse_ref, lse_rows_ref, acc_ref, m_ref, l_ref):
        hp, qb = pl.program_id(0), pl.program_id(1)
        q2 = q_ref[...] * scale
        first = _head_pair_masks(t)
        zero = jnp.zeros_like(q2)
        qs = (jnp.where(first, q2, zero), jnp.where(first, zero, q2))
        m_ref[...] = jnp.full_like(m_ref, -jnp.inf)
        l_ref[...] = jnp.zeros_like(l_ref)
        acc_ref[...] = jnp.zeros_like(acc_ref)

        def tile(kb, spans):
            off = pl.multiple_of(kb * t, t)
            k2 = k_ref[pl.ds(off, t), :]
            v2 = v_ref[pl.ds(off, t), :]
            ck = ck_ref[kb]
            for r0, r1, c0, c1, masked in spans:
                kk, vv, fr = k2[c0:c1], v2[c0:c1], first[r0:r1]
                pvs, als = [], []
                for j in range(2):
                    sc = _dot(qs[j][r0:r1], kk, NT_DIMS) - ck[j:j + 1, c0:c1]
                    if masked:
                        sc = jnp.where(_span_mask(r0, r1, c0, c1), sc, -jnp.inf)
                    blocks = _lane_blocks(sc)
                    mx = blocks[0]
                    for b in blocks[1:]:
                        mx = jnp.maximum(mx, b)
                    m_old = m_ref[j, r0:r1]
                    m_new = jnp.maximum(m_old, jnp.max(mx, axis=1, keepdims=True))
                    ps = [jnp.exp(b - m_new) for b in blocks]
                    a = jnp.exp(m_old - m_new)
                    l_ref[j, r0:r1] = a * l_ref[j, r0:r1] + _sum_list(ps)
                    m_ref[j, r0:r1] = m_new
                    pvs.append(_dot(jnp.concatenate(ps, axis=1).astype(BF16), vv))
                    als.append(a)
                acc_ref[r0:r1] = jnp.where(fr, als[0], als[1]) * acc_ref[r0:r1] + jnp.where(fr, pvs[0], pvs[1])

        def step(kb, carry):
            tile(kb, _full_spans(t))
            return carry

        lax.fori_loop(ks_ref[hp, qb], qb, step, 0)
        tile(qb, ((0, t, 0, t, True),))
        l0 = jnp.sum(l_ref[0], axis=1, keepdims=True)
        l1 = jnp.sum(l_ref[1], axis=1, keepdims=True)
        o_ref[...] = acc_ref[...] / jnp.where(first, l0, l1)
        for j, lj in enumerate((l0, l1)):
            lse = m_ref[j] + jnp.log(jnp.broadcast_to(lj, (t, LANES)))
            lse_ref[:, j:j + 1] = lse[:, 0:1]
            lse_rows_ref[j:j + 1, :] = lse.T[0:1, :]

    return _pcall(
        body, name=name, grid=(hp_blocks, nq), prefetch=1,
        in_specs=[pl.BlockSpec((t, LANES), lambda h, i, ks: (i, h)),
                  pl.BlockSpec((s, LANES), lambda h, i, ks: (0, hp_blocks + h)),
                  pl.BlockSpec((s, LANES), lambda h, i, ks: (0, 2 * hp_blocks + h)),
                  pl.BlockSpec((None, nq, 2, t), lambda h, i, ks: (h, 0, 0, 0))],
        out_specs=[pl.BlockSpec((t, LANES), lambda h, i, ks: (i, h)),
                   pl.BlockSpec((None, t, 2), lambda h, i, ks: (h, i, 0)),
                   pl.BlockSpec((None, None, 2, t), lambda h, i, ks: (h, i, 0, 0))],
        out_shape=[_sds((s, hp_blocks * LANES), F32), _sds((hp_blocks, s, 2), F32), _sds((hp_blocks, nq, 2, t), F32)],
        scratch=[pltpu.VMEM((t, LANES), F32), pltpu.VMEM((2, t, LANES), F32), pltpu.VMEM((2, t, LANES), F32)],
        sem=("parallel", "arbitrary"))(kb_start, qkv, qkv, qkv, ck_rows)


def _flash_dq(qkv, do16, ck_rows, lse_c, delta, kb_start, name):
    s = qkv.shape[0]
    t = min(ATT_TILE, s)
    nq = s // t
    scale = HEAD_DIM ** -0.5
    hp_blocks = FOX_HEADS // 2

    def body(ks_ref, q_ref, do_ref, k_ref, v_ref, ck_ref, lse_ref, dl_ref, dq_ref, drow_ref, acc_ref, row_acc):
        hp, qb = pl.program_id(0), pl.program_id(1)
        q2, do2 = q_ref[...] * scale, do_ref[...]
        first = _head_pair_masks(t)
        zero = jnp.zeros_like(q2)
        qs = (jnp.where(first, q2, zero), jnp.where(first, zero, q2))
        dos = (jnp.where(first, do2, zero), jnp.where(first, zero, do2))
        lse, dl = lse_ref[...], dl_ref[...]
        lane = lax.broadcasted_iota(jnp.int32, (t, LANES), 1)
        lse_b = [jnp.broadcast_to(lse[:, j:j + 1], (t, LANES)) for j in range(2)]
        dl_b = [jnp.broadcast_to(jnp.sum(jnp.where(lane == 2 * hp + j, dl, 0.0), axis=1, keepdims=True), (t, LANES))
                for j in range(2)]
        acc_ref[...] = jnp.zeros_like(acc_ref)
        row_acc[...] = jnp.zeros_like(row_acc)

        def tile(kb, spans):
            off = pl.multiple_of(kb * t, t)
            k2 = k_ref[pl.ds(off, t), :]
            v2 = v_ref[pl.ds(off, t), :]
            ck = ck_ref[kb]
            for r0, r1, c0, c1, masked in spans:
                kk, vv = k2[c0:c1], v2[c0:c1]
                dqs = []
                for j in range(2):
                    sc = _dot(qs[j][r0:r1], kk, NT_DIMS) - ck[j:j + 1, c0:c1]
                    if masked:
                        sc = jnp.where(_span_mask(r0, r1, c0, c1), sc, -jnp.inf)
                    dp = _dot(dos[j][r0:r1], vv, NT_DIMS)
                    lb, db_ = lse_b[j][r0:r1], dl_b[j][r0:r1]
                    dsb = [jnp.exp(x - lb) * (d - db_) for x, d in zip(_lane_blocks(sc), _lane_blocks(dp))]
                    row_acc[j, r0:r1] += _sum_list(dsb)
                    dqs.append(_dot(jnp.concatenate(dsb, axis=1).astype(BF16), kk))
                acc_ref[r0:r1] += jnp.where(first[r0:r1], dqs[0], dqs[1])

        def step(kb, carry):
            tile(kb, _full_spans(t))
            return carry

        lax.fori_loop(ks_ref[hp, qb], qb, step, 0)
        tile(qb, _diagonal_spans(t))
        dq_ref[...] = (acc_ref[...] * scale).astype(dq_ref.dtype)
        for j in range(2):
            drow_ref[j:j + 1, :] = jnp.sum(row_acc[j].T, axis=0, keepdims=True)

    return _pcall(
        body, name=name, grid=(hp_blocks, nq), prefetch=1,
        in_specs=[pl.BlockSpec((t, LANES), lambda h, i, ks: (i, h)),
                  pl.BlockSpec((t, LANES), lambda h, i, ks: (i, h)),
                  pl.BlockSpec((s, LANES), lambda h, i, ks: (0, hp_blocks + h)),
                  pl.BlockSpec((s, LANES), lambda h, i, ks: (0, 2 * hp_blocks + h)),
                  pl.BlockSpec((None, nq, 2, t), lambda h, i, ks: (h, 0, 0, 0)),
                  pl.BlockSpec((None, t, 2), lambda h, i, ks: (h, i, 0)),
                  pl.BlockSpec((t, LANES), lambda h, i, ks: (i, 0))],
        out_specs=[pl.BlockSpec((t, LANES), lambda h, i, ks: (i, h)),
                   pl.BlockSpec((None, None, 2, t), lambda h, i, ks: (h, i, 0, 0))],
        out_shape=[_sds((s, hp_blocks * LANES), BF16), _sds((hp_blocks, nq, 2, t), F32)],
        scratch=[pltpu.VMEM((t, LANES), F32), pltpu.VMEM((2, t, LANES), F32)],
        sem=("parallel", "arbitrary"))(kb_start, qkv, do16, qkv, qkv, ck_rows, lse_c, delta)


def _flash_dkv(qkv, do16, cum, lse_rows, dl_rows, qb_end, name):
    s = qkv.shape[0]
    t = min(ATT_TILE, s)
    nq = s // t
    scale = HEAD_DIM ** -0.5
    hp_blocks = FOX_HEADS // 2

    def body(qe_ref, k_ref, v_ref, cum_ref, q_ref, do_ref, lse_ref, dl_ref, dk_ref, dv_ref, dck_ref,
             dk_acc, dv_acc, dck_acc):
        hp, kb = pl.program_id(0), pl.program_id(1)
        k2, v2 = k_ref[...] * scale, v_ref[...]
        first = _head_pair_masks(t)
        zero = jnp.zeros_like(k2)
        ks = (jnp.where(first, k2, zero), jnp.where(first, zero, k2))
        vs = (jnp.where(first, v2, zero), jnp.where(first, zero, v2))
        cumv = cum_ref[...]
        lane = lax.broadcasted_iota(jnp.int32, (t, LANES), 1)
        ck_b = [jnp.broadcast_to(jnp.sum(jnp.where(lane == 2 * hp + j, cumv, 0.0), axis=1, keepdims=True), (t, LANES))
                for j in range(2)]
        dk_acc[...] = jnp.zeros_like(dk_acc)
        dv_acc[...] = jnp.zeros_like(dv_acc)
        dck_acc[...] = jnp.zeros_like(dck_acc)

        def tile(qb, spans):
            off = pl.multiple_of(qb * t, t)
            q2 = q_ref[pl.ds(off, t), :]
            do2 = do_ref[pl.ds(off, t), :]
            lse, dl = lse_ref[qb], dl_ref[qb]
            for r0, r1, c0, c1, masked in spans:
                qq, dd, fr = q2[c0:c1], do2[c0:c1], first[r0:r1]
                dvs, dks = [], []
                for j in range(2):
                    sc = _dot(ks[j][r0:r1], qq, NT_DIMS)
                    if masked:
                        sc = jnp.where(_span_mask(r0, r1, c0, c1, transposed=True), sc, -jnp.inf)
                    dp = _dot(vs[j][r0:r1], dd, NT_DIMS) - dl[j:j + 1, c0:c1]
                    cb_ = ck_b[j][r0:r1]
                    pb = [jnp.exp((x - cb_) - l) for x, l in zip(_lane_blocks(sc), _lane_blocks(lse[j:j + 1, c0:c1]))]
                    dsb = [p * d for p, d in zip(pb, _lane_blocks(dp))]
                    dck_acc[j, r0:r1] += _sum_list(dsb)
                    dvs.append(_dot(jnp.concatenate(pb, axis=1).astype(BF16), dd))
                    dks.append(_dot(jnp.concatenate(dsb, axis=1).astype(BF16), qq))
                dv_acc[r0:r1] += jnp.where(fr, dvs[0], dvs[1])
                dk_acc[r0:r1] += jnp.where(fr, dks[0], dks[1])

        def step(qb, carry):
            tile(qb, _full_spans(t))
            return carry

        tile(kb, _diagonal_spans(t, transposed=True))
        lax.fori_loop(kb + 1, qe_ref[hp, kb] + 1, step, 0)
        dk_ref[...] = (dk_acc[...] * scale).astype(dk_ref.dtype)
        dv_ref[...] = dv_acc[...].astype(dv_ref.dtype)
        for j in range(2):
            dck_ref[j:j + 1, :] = -jnp.sum(dck_acc[j].T, axis=0, keepdims=True)

    return _pcall(
        body, name=name, grid=(hp_blocks, nq), prefetch=1,
        in_specs=[pl.BlockSpec((t, LANES), lambda h, j, qe: (j, hp_blocks + h)),
                  pl.BlockSpec((t, LANES), lambda h, j, qe: (j, 2 * hp_blocks + h)),
                  pl.BlockSpec((t, LANES), lambda h, j, qe: (j, 0)),
                  pl.BlockSpec((s, LANES), lambda h, j, qe: (0, h)),
                  pl.BlockSpec((s, LANES), lambda h, j, qe: (0, h)),
                  pl.BlockSpec((None, nq, 2, t), lambda h, j, qe: (h, 0, 0, 0)),
                  pl.BlockSpec((None, nq, 2, t), lambda h, j, qe: (h, 0, 0, 0))],
        out_specs=[pl.BlockSpec((t, LANES), lambda h, j, qe: (j, h)),
                   pl.BlockSpec((t, LANES), lambda h, j, qe: (j, h)),
                   pl.BlockSpec((None, None, 2, t), lambda h, j, qe: (h, j, 0, 0))],
        out_shape=[_sds((s, hp_blocks * LANES), BF16), _sds((s, hp_blocks * LANES), BF16),
                   _sds((hp_blocks, nq, 2, t), F32)],
        scratch=[pltpu.VMEM((t, LANES), F32), pltpu.VMEM((t, LANES), F32), pltpu.VMEM((2, t, LANES), F32)],
        sem=("parallel", "arbitrary"))(qb_end, qkv, qkv, cum, qkv, do16, lse_rows, dl_rows)


SKIP_NATS = 110.0


def _qk_norms(qkv, ind16, name):
    d = FOX_HEADS * HEAD_DIM

    def fn(tile, ind):
        q = tile[:, :d].astype(F32)
        k = tile[:, d:2 * d].astype(F32)
        return _dot_split(q * q, ind, passes=2), _dot_split(k * k, ind, passes=2)
    return _rowwise(fn, [qkv], [ind16], [(LANES, F32), (LANES, F32)], [], name=name)


def _skip_bounds(qn, kn, cum, t):
    s = qn.shape[0]
    nq = s // t
    hp = FOX_HEADS // 2
    scale = HEAD_DIM ** -0.5
    qmax = jnp.sqrt(jnp.max(qn.reshape(nq, t, FOX_HEADS), axis=1))
    kmax = jnp.sqrt(jnp.max(kn, axis=0))
    bound = qmax * kmax[None, :] * (scale * 1.01) + 1e-3
    gap = cum[0::t][:, None, :] - cum[t - 1::t][None, :, :]
    idx = jnp.arange(nq, dtype=jnp.int32)
    needed = (gap + 2.0 * bound[:, None, :]) > -SKIP_NATS
    needed = needed.reshape(nq, nq, hp, 2).any(axis=-1) & (idx[None, :] <= idx[:, None])[:, :, None]
    first = jnp.min(jnp.where(needed, idx[None, :, None], nq), axis=1)
    first = jnp.minimum(first, idx[:, None])
    start = lax.cummin(first, axis=0, reverse=True)
    uses = start[:, None, :] <= idx[None, :, None]
    last = jnp.max(jnp.where(uses, idx[:, None, None], 0), axis=0)
    last = jnp.maximum(last, idx[:, None])
    return start.T.astype(jnp.int32), last.T.astype(jnp.int32)


def _head_rowsum(a, b, ind16, name):
    s, d = a.shape
    tm = min(ROW_TILE, s)

    def body(a_ref, b_ref, ind_ref, o_ref, rows_ref):
        dsum = _dot_split(a_ref[...] * b_ref[...], ind_ref[...])
        o_ref[...] = dsum
        rows_ref[...] = dsum.T[0:FOX_HEADS, :]

    tile = pl.BlockSpec((tm, d), lambda i: (i, 0))
    return _pcall(body, name=name, grid=(s // tm,),
                  in_specs=[tile, tile, pl.BlockSpec((d, LANES), lambda i: (0, 0))],
                  out_specs=[pl.BlockSpec((tm, LANES), lambda i: (i, 0)), pl.BlockSpec((FOX_HEADS, tm), lambda i: (0, i))],
                  out_shape=[_sds((s, LANES), F32), _sds((FOX_HEADS, s), F32)], sem=("parallel",))(a, b, ind16)


def _rows_to_tiles(x, t):
    s = x.shape[1]
    return x.reshape(FOX_HEADS // 2, 2, s // t, t).transpose(0, 2, 1, 3)


def _tiles_to_cols(x):
    hp, nq, _, t = x.shape
    return jnp.pad(x.transpose(1, 3, 0, 2).reshape(nq * t, 2 * hp), ((0, 0), (0, LANES - 2 * hp)))


def _fox_forward(u, w):
    s = u.shape[0]
    t = min(ATT_TILE, s)
    qkv = _mm(u, w["fox_qkv"], "nn", [BF16], name="fox_qkv")
    fraw = _mm(u, w["fox_f"], "nn", [F32], name="fox_fproj")
    cum, cum_rows = _fox_gate(fraw, w["fox_bf"], "fox_gate")
    ck_rows = _rows_to_tiles(cum_rows, t)
    qn, kn = _qk_norms(qkv, w["head_ind"], "fox_qk_norms")
    kb_start, qb_end = _skip_bounds(qn[:, :FOX_HEADS], kn[:, :FOX_HEADS], cum[:, :FOX_HEADS], t)
    o, lse, lse_rows = _flash_fwd(qkv, ck_rows, kb_start, "fox_flash_fwd")
    y = _mm(o, w["fox_o"], "nn", [F32], name="fox_oproj")
    return y, dict(u=u, qkv=qkv, fraw=fraw, cum=cum, ck_rows=ck_rows, o=o, lse=lse, lse_rows=lse_rows,
                   kb_start=kb_start, qb_end=qb_end)


def _fox_backward(dy, sv, w):
    s = dy.shape[0]
    t = min(ATT_TILE, s)
    do32, do16 = _mm(dy, w["fox_o"], "nt", [F32, BF16], name="fox_do")
    g_wo = _mm(sv["o"], dy, "tn", [F32], name="fox_gwo")
    delta, delta_rows = _head_rowsum(do32, sv["o"], w["head_ind"], "fox_delta")
    dq, drow = _flash_dq(sv["qkv"], do16, sv["ck_rows"], sv["lse"], delta, sv["kb_start"], "fox_flash_dq")
    dk, dv, dck = _flash_dkv(sv["qkv"], do16, sv["cum"], sv["lse_rows"], _rows_to_tiles(delta_rows, t),
                             sv["qb_end"], "fox_flash_dkv")
    df, db_f = _fox_gate_bwd(_tiles_to_cols(drow), _tiles_to_cols(dck), sv["fraw"], w["fox_bf"], "fox_gate_bwd")
    du = _mm(df, w["fox_f"], "nt", [F32], name="fox_du_f")
    du = _mm_nt_blocks(_k_blocks(dq) + _k_blocks(dk) + _k_blocks(dv), w["fox_qkv"], du, name="fox_du")
    g_win = jnp.concatenate([_mm(sv["u"], dq, "tn", [F32], name="fox_gwin_q"),
                             _mm(sv["u"], dk, "tn", [F32], name="fox_gwin_k"),
                             _mm(sv["u"], dv, "tn", [F32], name="fox_gwin_v"),
                             _mm(sv["u"], df, "tn", [F32], name="fox_gwin_f")[:, :FOX_HEADS]], axis=1)
    return du, dict(fox_w_in=g_win, fox_w_o=g_wo, fox_b_f=db_f[:, :FOX_HEADS])


def _conv_fwd(xpre, w8, b, name):
    s, c = xpre.shape
    tm, tc = min(ROW_TILE, s), min(1024, c)
    hb = tm // 8

    r = min(CONV_CHUNK, tm)

    def body(x_ref, h_ref, w_ref, b_ref, xc_ref, xa_ref):
        i = pl.program_id(1)
        row8 = lax.broadcasted_iota(jnp.int32, (8, LANES), 0)
        for cb in range(tc // LANES):
            ls = slice(cb * LANES, (cb + 1) * LANES)
            w, bias = w_ref[:, ls], b_ref[:, ls]
            for rb in range(tm // r):
                r0 = rb * r
                cur = x_ref[r0:r0 + r, ls]
                acc = cur * w[3:4] + bias
                if rb == 0:
                    halo = jnp.where(i > 0, h_ref[:, ls], 0.0)
                    x8 = cur[0:8]
                    acc8 = x8 * w[3:4] + bias
                    for j in range(1, SSM_CONV):
                        acc = acc + w[3 - j:4 - j] * pltpu.roll(cur, j, 0)
                        acc8 = acc8 + w[3 - j:4 - j] * jnp.where(row8 < j, pltpu.roll(halo, j, 0), pltpu.roll(x8, j, 0))
                    acc = jnp.concatenate([acc8, acc[8:]], axis=0)
                else:
                    for j in range(1, SSM_CONV):
                        acc = acc + w[3 - j:4 - j] * x_ref[r0 - j:r0 - j + r, ls]
                xc_ref[r0:r0 + r, ls] = acc
                xa_ref[r0:r0 + r, ls] = _silu(acc)

    tile = pl.BlockSpec((tm, tc), lambda jc, i: (i, jc))
    return _pcall(body, name=name, grid=(c // tc, s // tm),
                  in_specs=[tile, pl.BlockSpec((8, tc), lambda jc, i: (jnp.maximum(i * hb - 1, 0), jc)),
                            pl.BlockSpec((8, tc), lambda jc, i: (0, jc)), pl.BlockSpec((1, tc), lambda jc, i: (0, jc))],
                  out_specs=[tile, tile], out_shape=[_sds((s, c), F32), _sds((s, c), F32)],
                  sem=("parallel", "arbitrary"))(xpre, xpre, w8, b)


def _conv_bwd(dxa, xc, xpre, w8, name):
    s, c = xpre.shape
    tm, tc = min(ROW_TILE, s), min(1024, c)
    hb = tm // 8
    n = s // tm

    r = min(CONV_CHUNK, tm)

    def body(d_ref, xc_ref, x_ref, xh_ref, dn_ref, xcn_ref, w_ref, dx_ref, dw_ref, db_ref, g_scr):
        i = pl.program_id(1)

        @pl.when(i == 0)
        def _():
            dw_ref[...] = jnp.zeros_like(dw_ref)
            db_ref[...] = jnp.zeros_like(db_ref)
        row8 = lax.broadcasted_iota(jnp.int32, (8, LANES), 0)
        rowr = lax.broadcasted_iota(jnp.int32, (r, LANES), 0)
        for cb in range(tc // LANES):
            ls = slice(cb * LANES, (cb + 1) * LANES)
            w = w_ref[:, ls]
            for rb in range(tm // r):
                r0 = rb * r
                g_scr[r0:r0 + r, ls] = d_ref[r0:r0 + r, ls] * _dsilu(xc_ref[r0:r0 + r, ls])
            g_scr[tm:tm + 8, ls] = jnp.where(i < n - 1, dn_ref[:, ls] * _dsilu(xcn_ref[:, ls]), 0.0)
            db = jnp.zeros((1, LANES), F32)
            dws = [jnp.zeros((1, LANES), F32) for _ in range(SSM_CONV)]
            for rb in range(tm // r):
                r0 = rb * r
                g = g_scr[r0:r0 + r, ls]
                x = x_ref[r0:r0 + r, ls]
                db = db + _colsum(g)
                dws[3] = dws[3] + _colsum(g * x)
                acc = g * w[3:4]
                for j in range(1, SSM_CONV):
                    if rb == 0:
                        halo = jnp.where(i > 0, xh_ref[:, ls], 0.0)
                        dws[3 - j] = dws[3 - j] + _colsum(g * jnp.where(rowr >= j, pltpu.roll(x, j, 0), 0.0))
                        dws[3 - j] = dws[3 - j] + _colsum(jnp.where(row8 < j, g[0:8] * pltpu.roll(halo, j, 0), 0.0))
                    else:
                        dws[3 - j] = dws[3 - j] + _colsum(g * x_ref[r0 - j:r0 - j + r, ls])
                    acc = acc + w[3 - j:4 - j] * g_scr[r0 + j:r0 + j + r, ls]
                dx_ref[r0:r0 + r, ls] = acc.astype(dx_ref.dtype)
            db_ref[:, ls] += db
            for k in range(SSM_CONV):
                dw_ref[k:k + 1, ls] += dws[k]

    tile = pl.BlockSpec((tm, tc), lambda jc, i: (i, jc))
    prev8 = pl.BlockSpec((8, tc), lambda jc, i: (jnp.maximum(i * hb - 1, 0), jc))
    next8 = pl.BlockSpec((8, tc), lambda jc, i: (jnp.minimum((i + 1) * hb, n * hb - 1), jc))
    return _pcall(body, name=name, grid=(c // tc, n),
                  in_specs=[tile, tile, tile, prev8, next8, next8, pl.BlockSpec((8, tc), lambda jc, i: (0, jc))],
                  out_specs=[tile, pl.BlockSpec((8, tc), lambda jc, i: (0, jc)), pl.BlockSpec((1, tc), lambda jc, i: (0, jc))],
                  out_shape=[_sds((s, c), BF16), _sds((8, c), F32), _sds((1, c), F32)],
                  scratch=[pltpu.VMEM((tm + 8, tc), F32)],
                  sem=("parallel", "arbitrary"))(dxa, xc, xpre, xpre, dxa, xc, w8)


def _ssd_pre(dtraw, dt_bias, a_log, cst, name):
    def fn(raw, bias, alog, expand):
        tm = raw.shape[0]
        z = raw + bias
        dt = jnp.maximum(z, 0.0) + jnp.log(1.0 + jnp.exp(-jnp.abs(z)))
        lane = lax.broadcasted_iota(jnp.int32, (tm, LANES), 1)
        pos = lax.broadcasted_iota(jnp.int32, (tm, LANES), 0) & (SSM_CHUNK - 1)
        dt = jnp.where(lane < SSM_HEADS, dt, 0.0)
        c = dt * (-jnp.exp(alog))
        sh = 1
        while sh < SSM_CHUNK:
            c = c + jnp.where(pos >= sh, pltpu.roll(c, sh, 0), 0.0)
            sh *= 2
        return dt, c, _dot_split(dt, expand), _dot_split(c, expand)
    wide = SSM_HEADS * HEAD_DIM
    return _rowwise(fn, [dtraw], [dt_bias, a_log, cst["expand"]],
                    [(LANES, F32), (LANES, F32), (wide, F32), (wide, F32)], [], name=name)


def _ssd_post(dacs, ddt, dtraw, dt, dt_bias, a_log, name):
    def fn(dacs_v, ddt_v, raw, dt_v, bias, alog):
        tm = raw.shape[0]
        pos = lax.broadcasted_iota(jnp.int32, (tm, LANES), 0) & (SSM_CHUNK - 1)
        a = -jnp.exp(alog)
        c = dacs_v
        sh = 1
        while sh < SSM_CHUNK:
            c = c + jnp.where(pos + sh < SSM_CHUNK, pltpu.roll(c, tm - sh, 0), 0.0)
            sh *= 2
        draw = (ddt_v + c * a) * _sigmoid(raw + bias)
        return draw, _colsum(draw), _colsum(c * dt_v * a)
    return _rowwise(fn, [dacs, ddt, dtraw, dt], [dt_bias, a_log], [(LANES, BF16)], [(1, LANES)] * 2, name=name)


def _heads_rows(x, s):
    return x[:, :SSM_HEADS].reshape(s // SSM_CHUNK, SSM_CHUNK, SSM_GROUPS, 4).transpose(2, 0, 3, 1)


def _ssd_constants():
    gp = SSD_GROUPS_PER_STEP
    src = np.arange(LANES)[:, None]
    expand = src == np.arange(SSM_HEADS * HEAD_DIM)[None, :] // HEAD_DIM
    dst = np.arange(LANES)[None, None, :] - 4 * np.arange(gp)[:, None, None]
    seg = np.arange(SSM_GROUP_WIDTH)[None, :, None] // HEAD_DIM == dst
    seg4 = np.arange(4 * LANES)[None, :, None] // LANES == dst
    return dict(expand=jnp.asarray(expand, BF16), seg=jnp.asarray(seg, BF16), seg4=jnp.asarray(seg4, BF16))


def _ssm_weights(w_in, conv_w, conv_b, dt_bias, a_log, d_skip, norm_w, w_out):
    pad = ((0, 0), (0, LANES - SSM_HEADS))
    w_xbc = _group_cols(w_in[:, 2048:6144])
    w_dt = jnp.pad(w_in[:, 6144:], pad)
    return dict(
        ssm_z=w_in[:, :2048], ssm_xbc=w_xbc, ssm_dt=w_dt,
        ssm_zx=jnp.concatenate([w_in[:, :2048], w_xbc], axis=1),
        ssm_out=w_out,
        conv_w8=_group_cols(jnp.pad(conv_w, ((0, 8 - SSM_CONV), (0, 0)))),
        conv_b=_group_cols(conv_b), norm_w=norm_w,
        dt_bias=jnp.pad(dt_bias, pad), a_log=jnp.pad(a_log, pad),
        d_e=jnp.repeat(d_skip.reshape(SSM_GROUPS, 4), HEAD_DIM, axis=1)[:, None, :],
        ssd_cst=_ssd_constants())


def _ssd_setup(acs_e, acsr):
    l = SSM_CHUNK
    last = acsr[:, l - 1:l]
    lane1 = lax.broadcasted_iota(jnp.int32, (1, SSM_GROUP_WIDTH), 1)
    last_e = last[3:4, :]
    for r in (2, 1, 0):
        last_e = jnp.where(lane1 < HEAD_DIM * (r + 1), last[r:r + 1, :], last_e)
    return jnp.exp(acs_e), jnp.exp(last_e - acs_e), jnp.exp(last_e)


def _head_bcast(acs_e):
    lo = lax.broadcasted_iota(jnp.int32, (acs_e.shape[0], LANES), 1) < HEAD_DIM
    out = []
    for p in range(2):
        blk = acs_e[:, p * LANES:(p + 1) * LANES]
        rolled = pltpu.roll(blk, HEAD_DIM, 1)
        out += [jnp.where(lo, blk, rolled), jnp.where(lo, rolled, blk)]
    return out


def _group_cols(a):
    lead = a.shape[:-1]
    x = a[..., :2048].reshape(lead + (SSM_GROUPS, SSM_GROUP_WIDTH))
    b = a[..., 2048:3072].reshape(lead + (SSM_GROUPS, SSM_STATE))
    c = a[..., 3072:].reshape(lead + (SSM_GROUPS, SSM_STATE))
    return jnp.concatenate([x, b, c], axis=-1).reshape(lead + (4096,))


def _ungroup_cols(a):
    lead = a.shape[:-1]
    y = a.reshape(lead + (SSM_GROUPS, SSM_GROUP_WIDTH + 2 * SSM_STATE))
    return jnp.concatenate([y[..., :256].reshape(lead + (2048,)), y[..., 256:384].reshape(lead + (1024,)),
                            y[..., 384:].reshape(lead + (1024,))], axis=-1)


def _ssd_fwd2(xa, dte, acse, acsr, d_e, name):
    s = xa.shape[0]
    l, gw, ns = SSM_CHUNK, SSM_GROUP_WIDTH, SSM_STATE
    nc = s // l

    gb = gw + 2 * ns
    gp = SSD_GROUPS_PER_STEP

    def body(xa_ref, dt_ref, acs_ref, acsr_ref, d_ref, y_ref, hp_ref, h_sc):
        @pl.when(pl.program_id(1) == 0)
        def _():
            h_sc[...] = jnp.zeros_like(h_sc)
        lane = lax.broadcasted_iota(jnp.int32, (l, gw), 1)
        tril = _causal(l)
        for gi in range(gp):
            x = xa_ref[:, gi * gb:gi * gb + gw]
            bm = xa_ref[:, gi * gb + gw:gi * gb + gw + ns].astype(BF16)
            cm = xa_ref[:, gi * gb + gw + ns:(gi + 1) * gb].astype(BF16)
            acsr = acsr_ref[gi]
            dt_e, acs_e = dt_ref[:, gi * gw:(gi + 1) * gw], acs_ref[:, gi * gw:(gi + 1) * gw]
            acs_bc = _head_bcast(acs_e)
            e_e, dte_e, cd_e = _ssd_setup(acs_e, acsr)
            xdt = x * dt_e
            xdt16 = xdt.astype(BF16)
            cb = _dot(cm, bm, NT_DIMS)
            yd = jnp.zeros((l, gw), F32)
            for r in range(4):
                lm = jnp.exp(jnp.where(tril, acs_bc[r] - acsr[r:r + 1, :], -jnp.inf))
                yr = _dot((cb * lm).astype(BF16), xdt16)
                yd = jnp.where((lane >= HEAD_DIM * r) & (lane < HEAD_DIM * (r + 1)), yr, yd)
            hp = h_sc[gi]
            hp_ref[gi] = hp
            y_ref[:, gi * gw:(gi + 1) * gw] = yd + _dot(cm, hp.astype(BF16)) * e_e + x * d_ref[gi]
            h_sc[gi] = hp * cd_e + _dot(bm, (xdt * dte_e).astype(BF16), TN_DIMS)

    return _pcall(
        body, name=name, grid=(SSM_GROUPS // gp, nc),
        in_specs=[pl.BlockSpec((l, gp * gb), lambda g, c: (c, g)),
                  pl.BlockSpec((l, gp * gw), lambda g, c: (c, g)),
                  pl.BlockSpec((l, gp * gw), lambda g, c: (c, g)),
                  pl.BlockSpec((gp, None, 4, l), lambda g, c: (g, c, 0, 0)),
                  pl.BlockSpec((gp, 1, gw), lambda g, c: (g, 0, 0))],
        out_specs=[pl.BlockSpec((l, gp * gw), lambda g, c: (c, g)),
                   pl.BlockSpec((gp, None, ns, gw), lambda g, c: (g, c, 0, 0))],
        out_shape=[_sds((s, 2048), F32), _sds((SSM_GROUPS, nc, ns, gw), F32)],
        scratch=[pltpu.VMEM((gp, ns, gw), F32)],
        sem=("parallel", "arbitrary"))(xa, dte, acse, acsr, d_e)


def _ssd_bwd2(dy, xa, dte, acse, acsr, d_e, hprev, cst, name):
    s = xa.shape[0]
    l, gw, ns = SSM_CHUNK, SSM_GROUP_WIDTH, SSM_STATE
    nc = s // l

    gb = gw + 2 * ns
    gp = SSD_GROUPS_PER_STEP

    def body(dy_ref, xa_ref, dt_ref, acs_ref, acsr_ref, d_ref, hp_ref,
             seg_ref, seg4_ref, dxa_ref, dacs_ref, ddt_ref, dd_ref, dh_sc):
        @pl.when(pl.program_id(1) == 0)
        def _():
            dh_sc[...] = jnp.zeros_like(dh_sc)
            dd_ref[...] = jnp.zeros_like(dd_ref)
        parts = [one_group(gi, dy_ref, xa_ref, dt_ref, acs_ref, acsr_ref, d_ref, hp_ref, seg_ref, seg4_ref,
                           dxa_ref, dh_sc) for gi in range(gp)]
        dacs_ref[...] = _sum_list([p[0] for p in parts])
        ddt_ref[...] = _sum_list([p[1] for p in parts])
        dd_ref[0:1, :] += _sum_list([p[2] for p in parts])

    def one_group(gi, dy_ref, xa_ref, dt_ref, acs_ref, acsr_ref, d_ref, hp_ref, seg_ref, seg4_ref, dxa_ref, dh_sc):
        dyv = dy_ref[:, gi * gw:(gi + 1) * gw]
        x = xa_ref[:, gi * gb:gi * gb + gw]
        bm = xa_ref[:, gi * gb + gw:gi * gb + gw + ns].astype(BF16)
        cm = xa_ref[:, gi * gb + gw + ns:(gi + 1) * gb].astype(BF16)
        acsr = acsr_ref[gi]
        dt_e, acs_e = dt_ref[:, gi * gw:(gi + 1) * gw], acs_ref[:, gi * gw:(gi + 1) * gw]
        acs_bc = _head_bcast(acs_e)
        e_e, dte_e, cd_e = _ssd_setup(acs_e, acsr)
        seg, seg4 = seg_ref[gi], seg4_ref[gi]
        lane = lax.broadcasted_iota(jnp.int32, (l, gw), 1)
        xdt = x * dt_e
        xdt16 = xdt.astype(BF16)
        dy16 = dyv.astype(BF16)
        cb = _dot(cm, bm, NT_DIMS)
        cbt = _dot(bm, cm, NT_DIMS)
        hp = hp_ref[gi]
        hp16 = hp.astype(BF16)
        g = dh_sc[gi]
        g16 = g.astype(BF16)
        t_all = _dot(cm, hp16)
        dt16 = (dyv * e_e).astype(BF16)
        dc = _dot(dt16, hp16, NT_DIMS)
        dhp = _dot(cm, dt16, TN_DIMS)
        wv = xdt * dte_e
        dw = _dot(bm, g16)
        db = _dot(wv.astype(BF16), g16, NT_DIMS)
        dxdt = dw * dte_e
        acs_term = dyv * t_all * e_e - dw * wv
        last_term = _colsum(dw * wv) + _colsum(g * hp) * cd_e
        dh_sc[gi] = g * cd_e + dhp
        tril, triu = _causal(l), _causal(l, transposed=True)
        dcb = jnp.zeros((l, l), F32)
        dcbt = jnp.zeros((l, l), F32)
        qd = []
        for r in range(4):
            in_head = (lane >= HEAD_DIM * r) & (lane < HEAD_DIM * (r + 1))
            a_col = acs_bc[r]
            lm = jnp.exp(jnp.where(tril, a_col - acsr[r:r + 1, :], -jnp.inf))
            lmt = jnp.exp(jnp.where(triu, acsr[r:r + 1, :] - a_col, -jnp.inf))
            mm_, mt = cb * lm, cbt * lmt
            dyr = jnp.where(in_head, dy16, jnp.zeros_like(dy16))
            dm = _dot(dyr, xdt16, NT_DIMS)
            dmt = _dot(xdt16, dyr, NT_DIMS)
            dxdt = dxdt + jnp.where(in_head, _dot(mt.astype(BF16), dy16), 0.0)
            dcb = dcb + dm * lm
            dcbt = dcbt + dmt * lmt
            qd.append(dm * mm_ - dmt * mt)
        dc = dc + _dot(dcb.astype(BF16), bm)
        db = db + _dot(dcbt.astype(BF16), cm)
        rowl = lax.broadcasted_iota(jnp.int32, (l, LANES), 0)
        row8 = lax.broadcasted_iota(jnp.int32, (8, gw), 0)
        small = _dot_split(jnp.where(row8 == 0, last_term, jnp.where(row8 == 1, _colsum(dyv * x), 0.0)), seg, passes=2)
        big = _dot_split(jnp.concatenate([acs_term, dxdt * x], axis=0), seg, passes=2)
        dacs = (big[0:l] + _dot_split(jnp.concatenate(qd, axis=1), seg4, passes=2)
                + jnp.where(rowl == l - 1, small[0:1, :], 0.0))
        dxa_ref[:, gi * gb:(gi + 1) * gb] = jnp.concatenate([dxdt * dt_e + dyv * d_ref[gi], db, dc], axis=1)
        return dacs, big[l:2 * l], small[1:2, :]

    rc = lambda c: nc - 1 - c
    ng = SSM_GROUPS // gp
    return _pcall(
        body, name=name, grid=(ng, nc),
        in_specs=[pl.BlockSpec((l, gp * gw), lambda g, c: (rc(c), g)),
                  pl.BlockSpec((l, gp * gb), lambda g, c: (rc(c), g)),
                  pl.BlockSpec((l, gp * gw), lambda g, c: (rc(c), g)),
                  pl.BlockSpec((l, gp * gw), lambda g, c: (rc(c), g)),
                  pl.BlockSpec((gp, None, 4, l), lambda g, c: (g, rc(c), 0, 0)),
                  pl.BlockSpec((gp, 1, gw), lambda g, c: (g, 0, 0)),
                  pl.BlockSpec((gp, None, ns, gw), lambda g, c: (g, rc(c), 0, 0)),
                  pl.BlockSpec((gp, gw, LANES), lambda g, c: (0, 0, 0)),
                  pl.BlockSpec((gp, 4 * LANES, LANES), lambda g, c: (0, 0, 0))],
        out_specs=[pl.BlockSpec((l, gp * gb), lambda g, c: (rc(c), g)),
                   pl.BlockSpec((None, l, LANES), lambda g, c: (g, rc(c), 0)),
                   pl.BlockSpec((None, l, LANES), lambda g, c: (g, rc(c), 0)),
                   pl.BlockSpec((None, 8, LANES), lambda g, c: (g, 0, 0))],
        out_shape=[_sds((s, 4096), F32), _sds((ng, s, LANES), F32), _sds((ng, s, LANES), F32),
                   _sds((ng, 8, LANES), F32)],
        scratch=[pltpu.VMEM((gp, ns, gw), F32)],
        sem=("parallel", "arbitrary"))(dy, xa, dte, acse, acsr, d_e, hprev, cst["seg"], cst["seg4"])


def _gate_norm(y, z, nw, name):
    c = y.shape[1]

    def fn(yv, zv, w):
        outs = []
        for k in range(c // SSM_GROUP_WIDTH):
            sl = slice(k * SSM_GROUP_WIDTH, (k + 1) * SSM_GROUP_WIDTH)
            yg = yv[:, sl] * _silu(zv[:, sl])
            rinv = lax.rsqrt(jnp.mean(yg * yg, axis=-1, keepdims=True) + RMS_EPS)
            outs.append(yg * rinv * w[:, sl])
        return (jnp.concatenate(outs, axis=1),)
    return _rowwise(fn, [y, z], [nw], [(c, BF16)], [], name=name)[0]


def _out_gate_norm_bwd(dy, w_out, y, z, nw, name):
    s, c = y.shape
    tm, tn = min(512, s), min(1024, c)
    k = dy.shape[1]

    def body(a_ref, b_ref, y_ref, z_ref, w_ref, dy_ref, dz_ref, dw_ref):
        @pl.when(pl.program_id(1) == 0)
        def _():
            dw_ref[...] = jnp.zeros_like(dw_ref)
        dv = _dot(a_ref[...], b_ref[...], NT_DIMS)
        yv, zv, w = y_ref[...], z_ref[...], w_ref[...]
        dys, dzs, dws = [], [], []
        for g in range(tn // SSM_GROUP_WIDTH):
            sl = slice(g * SSM_GROUP_WIDTH, (g + 1) * SSM_GROUP_WIDTH)
            ys, zs, ds = yv[:, sl], zv[:, sl], dv[:, sl]
            sz = _silu(zs)
            yg = ys * sz
            rinv = lax.rsqrt(jnp.mean(yg * yg, axis=-1, keepdims=True) + RMS_EPS)
            nrm = yg * rinv
            dn = ds * w[:, sl]
            dyg = rinv * (dn - nrm * jnp.mean(dn * nrm, axis=-1, keepdims=True))
            dys.append(dyg * sz)
            dzs.append(dyg * ys * _dsilu(zs))
            dws.append(_colsum(ds * nrm))
        dy_ref[...] = jnp.concatenate(dys, axis=1)
        dz_ref[...] = jnp.concatenate(dzs, axis=1).astype(dz_ref.dtype)
        dw_ref[...] += jnp.concatenate(dws, axis=1)

    tile = pl.BlockSpec((tm, tn), lambda j, i: (i, j))
    row = pl.BlockSpec((1, tn), lambda j, i: (0, j))
    return _pcall(body, name=name, grid=(c // tn, s // tm),
                  in_specs=[pl.BlockSpec((tm, k), lambda j, i: (i, 0)), pl.BlockSpec((tn, k), lambda j, i: (j, 0)),
                            tile, tile, row],
                  out_specs=[tile, tile, row], out_shape=[_sds((s, c), F32), _sds((s, c), BF16), _sds((1, c), F32)],
                  sem=("parallel", "arbitrary"))(dy, w_out, y, z, nw)


def _ssd_forward(u, w):
    s = u.shape[0]
    z = _mm(u, w["ssm_z"], "nn", [F32], name="ssm_zproj")
    xpre = _mm(u, w["ssm_xbc"], "nn", [F32], name="ssm_xproj")
    dtraw = _mm(u, w["ssm_dt"], "nn", [F32], name="ssm_dtproj")
    xc, xa = _conv_fwd(xpre, w["conv_w8"], w["conv_b"], "ssm_conv")
    dt, acs, dte, acse = _ssd_pre(dtraw, w["dt_bias"], w["a_log"], w["ssd_cst"], "ssm_pre")
    acsr = _heads_rows(acs, s)
    y, hprev = _ssd_fwd2(xa, dte, acse, acsr, w["d_e"], "ssm_scan")
    yn = _gate_norm(y, z, w["norm_w"], "ssm_gate_norm")
    out = _mm(yn, w["ssm_out"], "nn", [F32], name="ssm_oproj")
    return out, dict(u=u, z=z, xpre=xpre, xc=xc, xa=xa, dtraw=dtraw, dt=dt, dte=dte, acse=acse,
                     acsr=acsr, y=y, hprev=hprev, yn=yn)


def _ssd_backward(dy, sv, w):
    s = dy.shape[0]
    g_wout = _mm(sv["yn"], dy, "tn", [F32], name="ssm_gwout")
    dys, dz, dnw = _out_gate_norm_bwd(dy, w["ssm_out"], sv["y"], sv["z"], w["norm_w"], "ssm_dyn_gate_norm_bwd")
    dxa, dacs_c, ddt_c, dd = _ssd_bwd2(dys, sv["xa"], sv["dte"], sv["acse"], sv["acsr"], w["d_e"], sv["hprev"],
                                       w["ssd_cst"], "ssm_scan_bwd")
    per_step = 4 * SSD_GROUPS_PER_STEP
    pad = ((0, 0), (0, LANES - SSM_HEADS))
    dacs = jnp.pad(jnp.concatenate([a[:, :per_step] for a in dacs_c], axis=1), pad)
    ddt = jnp.pad(jnp.concatenate([a[:, :per_step] for a in ddt_c], axis=1), pad)
    draw, dbias, dalog = _ssd_post(dacs, ddt, sv["dtraw"], sv["dt"], w["dt_bias"], w["a_log"], "ssm_post")
    dxpre, dcw, dcb = _conv_bwd(dxa, sv["xc"], sv["xpre"], w["conv_w8"], "ssm_conv_bwd")
    du = _mm(draw, w["ssm_dt"], "nt", [F32], name="ssm_du_dt")
    n_z = dz.shape[1]
    du = _mm_nt_blocks(_k_blocks(dz), w["ssm_zx"][:, :n_z], du, name="ssm_du_z")
    du = _mm_nt_blocks(_k_blocks(dxpre), w["ssm_zx"][:, n_z:], du, name="ssm_du_x")
    g_win = jnp.concatenate([_mm(sv["u"], dz, "tn", [F32], name="ssm_gwin_z"),
                             _ungroup_cols(_mm(sv["u"], dxpre, "tn", [F32], name="ssm_gwin_x")),
                             _mm(sv["u"], draw, "tn", [F32], name="ssm_gwin_dt")[:, :SSM_HEADS]], axis=1)
    return du, dict(ssm_w_in=g_win, ssm_w_out=g_wout, ssm_conv_w=_ungroup_cols(dcw[:SSM_CONV]),
                    ssm_conv_b=_ungroup_cols(dcb), ssm_norm_w=dnw, ssm_dt_bias=dbias[:, :SSM_HEADS],
                    ssm_a_log=dalog[:, :SSM_HEADS], ssm_d=dd[:, 0, :per_step].reshape(1, SSM_HEADS))


def _mlp_forward(u2, w1, w2, tag):
    def epi(acc):
        hr = jnp.maximum(acc, 0.0)
        return hr, hr * hr
    hr, a = _mm(u2, w1, "nn", [BF16, BF16], name=tag + "_mlp_up", epi=epi)
    y2 = _mm(a, w2, "nn", [F32], name=tag + "_mlp_down")
    return y2, hr, a


def _mlp_backward(dy2, u2, hr, a, w1, w2, tag):
    dh = _mm(dy2, w2, "nt", [BF16], name=tag + "_mlp_dh", extra=(hr,),
             epi=lambda acc, h: (acc * (2.0 * h.astype(F32)),))
    g_w2 = _mm(a, dy2, "tn", [F32], name=tag + "_mlp_gw2")
    g_w1 = _mm(u2, dh, "tn", [F32], name=tag + "_mlp_gw1")
    du2 = _mm(dh, w1, "nt", [F32], name=tag + "_mlp_du")
    return du2, g_w1, g_w2


def _ada_forward(c16, ada_w, ada_b_cols, name):
    nl, d, cols = ada_w.shape
    tn = 512

    def body(c_ref, w_ref, b_ref, o_ref):
        cond = _silu(c_ref[...]).astype(BF16)
        o_ref[...] = _dot(cond, w_ref[...].astype(BF16)) + b_ref[...]

    return _pcall(body, name=name, grid=(nl, cols // tn),
                  in_specs=[pl.BlockSpec((16, d), lambda i, j: (0, 0)),
                            pl.BlockSpec((None, d, tn), lambda i, j: (i, 0, j)),
                            pl.BlockSpec((None, 1, tn), lambda i, j: (i, 0, j))],
                  out_specs=pl.BlockSpec((None, 16, tn), lambda i, j: (i, 0, j)),
                  out_shape=_sds((nl, 16, cols), F32), sem=("parallel", "parallel"))(c16, ada_w, ada_b_cols)


def _ada_backward(c_t, dmod_cols, name):
    d, nb = c_t.shape
    nl, _, cols = dmod_cols.shape
    tn = 512

    def body(c_ref, dm_ref, o_ref):
        cond = _silu(c_ref[...])
        dm = dm_ref[...]
        acc = cond[:, 0:1] * dm[0:1, :]
        for b in range(1, nb):
            acc = acc + cond[:, b:b + 1] * dm[b:b + 1, :]
        o_ref[...] = acc

    return _pcall(body, name=name, grid=(nl, cols // tn),
                  in_specs=[pl.BlockSpec((d, nb), lambda i, j: (0, 0)),
                            pl.BlockSpec((None, nb, tn), lambda i, j: (i, 0, j))],
                  out_specs=pl.BlockSpec((None, d, tn), lambda i, j: (i, 0, j)),
                  out_shape=_sds((nl, d, cols), F32), sem=("parallel", "parallel"))(c_t, dmod_cols)


def _adamw(w, g, m, v, name):
    rows, cols = w.shape
    tm = rows
    for cand in (256, 128, 64, 32, 16, 8):
        if rows % cand == 0 and rows > cand:
            tm = cand
            break
    c1 = 1.0 / (1.0 - ADAM_B1 ** ADAM_STEP)
    c2 = 1.0 / (1.0 - ADAM_B2 ** ADAM_STEP)

    def fn(wv, gv, mv, vv):
        mn = ADAM_B1 * mv + (1.0 - ADAM_B1) * gv
        vn = ADAM_B2 * vv + (1.0 - ADAM_B2) * (gv * gv)
        delta = -ADAM_LR * ((mn * c1) / (jnp.sqrt(vn * c2) + ADAM_EPS) + ADAM_WD * wv)
        return delta, mn, vn
    return _rowwise(fn, [w, g, m, v], [], [(cols, F32)] * 3, [], name=name, tm=tm)


def _my_pos():
    return lax.axis_index("x"), lax.axis_index("y"), lax.axis_index("c")


def _allgather8(x, name):
    r, c = x.shape

    def body(x_ref, out_ref, send_sems, recv_sems, local_sem):
        mx, my, mc = _my_pos()
        me = 4 * mx + 2 * my + mc
        mine = pltpu.make_async_copy(x_ref, out_ref.at[me], local_sem)
        mine.start()
        copies = []
        for k in range(1, 8):
            fx, fy, fc = (k >> 2) & 1, (k >> 1) & 1, k & 1
            px = 1 - mx if fx else mx
            py = 1 - my if fy else my
            pc = 1 - mc if fc else mc
            peer = 4 * px + 2 * py + pc
            send = pltpu.make_async_remote_copy(src_ref=x_ref, dst_ref=out_ref.at[me], send_sem=send_sems.at[k - 1],
                                                recv_sem=recv_sems.at[k - 1], device_id=(px, py, pc),
                                                device_id_type=MESH)
            send.start()
            recv = pltpu.make_async_remote_copy(src_ref=x_ref, dst_ref=out_ref.at[peer], send_sem=send_sems.at[k - 1],
                                                recv_sem=recv_sems.at[k - 1], device_id=(px, py, pc),
                                                device_id_type=MESH)
            copies.append((send, recv))
        for send, recv in copies:
            recv.wait_recv()
        for send, recv in copies:
            send.wait_send()
        mine.wait()

    vm = pl.BlockSpec(memory_space=pltpu.VMEM)
    return _pcall(body, name=name, in_specs=[vm], out_specs=vm, out_shape=_sds((8, r, c), x.dtype),
                  scratch=[pltpu.SemaphoreType.DMA((7,)), pltpu.SemaphoreType.DMA((7,)), pltpu.SemaphoreType.DMA])(x)


def _chip_flips(mx, my):
    out = []
    for fx, fy in ((1, 0), (0, 1), (1, 1)):
        px = 1 - mx if fx else mx
        py = 1 - my if fy else my
        out.append((px, py, 2 * px + py))
    return out


def _gather_chips(shard2, name):
    _, h, c = shard2.shape

    def body(x_ref, out_ref, send_sems, recv_sems):
        mx, my, mc = _my_pos()
        oc = 1 - mc
        mk = 2 * mx + my
        flips = _chip_flips(mx, my)

        def copy(k, src, dst, to):
            return pltpu.make_async_remote_copy(src_ref=src, dst_ref=dst, send_sem=send_sems.at[k],
                                                recv_sem=recv_sems.at[k], device_id=to, device_id_type=MESH)

        first = [copy(j, x_ref.at[mc], out_ref.at[mk, mc], (px, py, mc)) for j, (px, py, pk) in enumerate(flips)]
        for cp in first:
            cp.start()
        passed = []
        for j, (px, py, pk) in enumerate(flips):
            copy(j, x_ref.at[mc], out_ref.at[pk, mc], (px, py, mc)).wait_recv()
            fw = copy(3 + j, out_ref.at[pk, mc], out_ref.at[pk, mc], (mx, my, oc))
            fw.start()
            passed.append(fw)
        for j, (px, py, pk) in enumerate(flips):
            copy(3 + j, out_ref.at[pk, oc], out_ref.at[pk, oc], (mx, my, oc)).wait_recv()
        for cp in first + passed:
            cp.wait_send()

    return _pcall(body, name=name, in_specs=[HBM_SPEC], out_specs=HBM_SPEC, out_shape=_sds((4, 2, h, c), shard2.dtype),
                  scratch=[pltpu.SemaphoreType.DMA((6,)), pltpu.SemaphoreType.DMA((6,))])(shard2)


def _pair_exchange(g4, name):
    n, _, h, c = g4.shape

    def body(g_ref, out_ref, send_sem, recv_sem):
        mx, my, mc = _my_pos()
        oc = 1 - mc
        copies = []
        for k in range(n):
            cp = pltpu.make_async_remote_copy(src_ref=g_ref.at[k, oc], dst_ref=out_ref.at[k], send_sem=send_sem.at[k],
                                              recv_sem=recv_sem.at[k], device_id=(mx, my, oc), device_id_type=MESH)
            cp.start()
            copies.append(cp)
        for cp in copies:
            cp.wait_recv()
        for cp in copies:
            cp.wait_send()

    return _pcall(body, name=name, in_specs=[HBM_SPEC], out_specs=HBM_SPEC, out_shape=_sds((n, h, c), g4.dtype),
                  scratch=[pltpu.SemaphoreType.DMA((n,)), pltpu.SemaphoreType.DMA((n,))])(g4)


def _pair_add(g4, recv, core, name):
    n, _, h, c = g4.shape
    tm = _row_tile(h)

    def body(core_ref, a_ref, b_ref, o_ref, o16_ref):
        acc = a_ref[...] + b_ref[...]
        o_ref[...] = acc
        o16_ref[...] = acc.astype(BF16)

    out_spec = pl.BlockSpec((None, tm, c), lambda k, i, cr: (k, i, 0))
    return _pcall(body, name=name, grid=(n, h // tm), prefetch=1,
                  in_specs=[pl.BlockSpec((None, None, tm, c), lambda k, i, cr: (k, cr[0], i, 0)), out_spec],
                  out_specs=[out_spec, out_spec], out_shape=[_sds((n, h, c), F32), _sds((n, h, c), BF16)],
                  sem=("parallel", "parallel"))(core, g4, recv)


def _chip_exchange(p, name):
    n, h, c = p.shape

    def body(p_ref, out_ref, send_sems, recv_sems):
        mx, my, mc = _my_pos()
        copies = []
        for j, (px, py, pk) in enumerate(_chip_flips(mx, my)):
            cp = pltpu.make_async_remote_copy(src_ref=p_ref.at[pk], dst_ref=out_ref.at[j], send_sem=send_sems.at[j],
                                              recv_sem=recv_sems.at[j], device_id=(px, py, mc), device_id_type=MESH)
            cp.start()
            copies.append(cp)
        for cp in copies:
            cp.wait_recv()
        for cp in copies:
            cp.wait_send()

    return _pcall(body, name=name, in_specs=[HBM_SPEC], out_specs=HBM_SPEC, out_shape=_sds((3, h, c), p.dtype),
                  scratch=[pltpu.SemaphoreType.DMA((3,)), pltpu.SemaphoreType.DMA((3,))])(p)


def _chip_sum(p, slots, chip, name):
    _, h, c = p.shape
    tm = _row_tile(h)

    def body(chip_ref, p_ref, q_ref, o_ref):
        o_ref[...] = ((p_ref[...] + q_ref[0].astype(F32)) + q_ref[1].astype(F32)) + q_ref[2].astype(F32)

    return _pcall(body, name=name, grid=(h // tm,), prefetch=1,
                  in_specs=[pl.BlockSpec((None, tm, c), lambda i, ch: (ch[0], i, 0)),
                            pl.BlockSpec((3, tm, c), lambda i, ch: (0, i, 0))],
                  out_specs=pl.BlockSpec((tm, c), lambda i, ch: (i, 0)),
                  out_shape=_sds((h, c), F32), sem=("parallel",))(chip, p, slots)


def _sum_slots(q, name):
    n, h, c = q.shape
    tm = _row_tile(h)

    def body(q_ref, o_ref):
        acc = q_ref[0]
        for k in range(1, n):
            acc = acc + q_ref[k]
        o_ref[...] = acc

    return _pcall(body, name=name, grid=(h // tm,),
                  in_specs=[pl.BlockSpec((n, tm, c), lambda i: (0, i, 0))],
                  out_specs=pl.BlockSpec((tm, c), lambda i: (i, 0)),
                  out_shape=_sds((h, c), F32), sem=("parallel",))(q)


def _pair_share(f, name):
    h, c = f.shape

    def body(f_ref, out_ref, send_sem, recv_sem):
        mx, my, mc = _my_pos()
        cp = pltpu.make_async_remote_copy(src_ref=f_ref, dst_ref=out_ref, send_sem=send_sem, recv_sem=recv_sem,
                                          device_id=(mx, my, 1 - mc), device_id_type=MESH)
        cp.start()
        cp.wait_recv()
        cp.wait_send()

    return _pcall(body, name=name, in_specs=[HBM_SPEC], out_specs=HBM_SPEC, out_shape=_sds((h, c), f.dtype),
                  scratch=[pltpu.SemaphoreType.DMA, pltpu.SemaphoreType.DMA])(f)


BIG = ("mlp_w1", "mlp_w2", "fox_w_in", "fox_w_o", "ssm_w_in", "ssm_w_out")
SMALL_SHARDED = ("ssm_conv_w", "ssm_conv_b", "ssm_norm_w")
PACK_COLS = 1024


def _pack_rows(parts, rows_multiple, dtype):
    flat = jnp.concatenate([p.reshape(-1).astype(dtype) for p in parts])
    unit = rows_multiple * PACK_COLS
    total = -(-flat.shape[0] // unit) * unit
    flat = jnp.pad(flat, (0, total - flat.shape[0]))
    return flat.reshape(total // PACK_COLS, PACK_COLS)


def _unpack(flat, shapes):
    out, off = [], 0
    for sh in shapes:
        n = 1
        for d_ in sh:
            n *= d_
        out.append(flat[off:off + n].reshape(sh))
        off += n
    return out


PIECE_ROWS = 16


def _piece_rows(shape):
    n = 1
    for d_ in shape:
        n *= d_
    rows = -(-n // PACK_COLS)
    return n, -(-rows // PIECE_ROWS) * PIECE_ROWS


def _pack2d(parts, rows_multiple, dtype):
    blocks = []
    for p in parts:
        n, rows = _piece_rows(p.shape)
        a = p.astype(dtype)
        if p.shape[-1] != PACK_COLS or n % PACK_COLS:
            a = jnp.pad(a.reshape(-1), (0, -n % PACK_COLS))
        a = a.reshape(-1, PACK_COLS)
        blocks.append(jnp.pad(a, ((0, rows - a.shape[0]), (0, 0))))
    total = sum(b.shape[0] for b in blocks)
    pad = -total % rows_multiple
    if pad:
        blocks.append(jnp.zeros((pad, PACK_COLS), dtype))
    return jnp.concatenate(blocks, axis=0)


def _unpack2d(buf, shapes):
    out, off = [], 0
    for sh in shapes:
        n, rows = _piece_rows(sh)
        piece = buf[off:off + rows]
        if sh[-1] == PACK_COLS and n % PACK_COLS == 0:
            out.append(piece[:n // PACK_COLS].reshape(sh))
        else:
            out.append(piece.reshape(-1)[:n].reshape(sh))
        off += rows
    return out


def _row_tile(h, cap=512):
    for step in (16, 8):
        best = 0
        for cand in range(step, cap + 1, step):
            if h % cand == 0:
                best = cand
        if best:
            return best
    return h


def _chip_slice(full, axis, k, width):
    idx = [slice(None)] * full.ndim
    idx[axis] = slice(k * width, (k + 1) * width)
    return full[tuple(idx)]


SHARD_AXIS = dict(mlp_w1=2, mlp_w2=1, fox_w_in=2, fox_w_o=1, ssm_w_in=2, ssm_w_out=1, ssm_conv_w=2,
                  ssm_conv_b=1, ssm_norm_w=1, ada_w=2)


def kernel(x, c, ada_w, ada_b, ln_mix_g, ln_mix_b, ln_mlp_g, ln_mlp_b, mlp_w1, mlp_w2, fox_w_in, fox_b_f, fox_w_o, ssm_w_in, ssm_conv_w, ssm_conv_b, ssm_dt_bias, ssm_a_log, ssm_d, ssm_norm_w, ssm_w_out, loss_target, m_ada_w, m_ada_b, m_ln_mix_g, m_ln_mix_b, m_ln_mlp_g, m_ln_mlp_b, m_mlp_w1, m_mlp_w2, m_fox_w_in, m_fox_b_f, m_fox_w_o, m_ssm_w_in, m_ssm_conv_w, m_ssm_conv_b, m_ssm_dt_bias, m_ssm_a_log, m_ssm_d, m_ssm_norm_w, m_ssm_w_out, v_ada_w, v_ada_b, v_ln_mix_g, v_ln_mix_b, v_ln_mlp_g, v_ln_mlp_b, v_mlp_w1, v_mlp_w2, v_fox_w_in, v_fox_b_f, v_fox_w_o, v_ssm_w_in, v_ssm_conv_w, v_ssm_conv_b, v_ssm_dt_bias, v_ssm_a_log, v_ssm_d, v_ssm_norm_w, v_ssm_w_out):
    names = ("ada_w", "ada_b", "ln_mix_g", "ln_mix_b", "ln_mlp_g", "ln_mlp_b", "mlp_w1", "mlp_w2", "fox_w_in",
             "fox_b_f", "fox_w_o", "ssm_w_in", "ssm_conv_w", "ssm_conv_b", "ssm_dt_bias", "ssm_a_log", "ssm_d",
             "ssm_norm_w", "ssm_w_out")
    weights = dict(zip(names, (ada_w, ada_b, ln_mix_g, ln_mix_b, ln_mlp_g, ln_mlp_b, mlp_w1, mlp_w2, fox_w_in,
                               fox_b_f, fox_w_o, ssm_w_in, ssm_conv_w, ssm_conv_b, ssm_dt_bias, ssm_a_log, ssm_d,
                               ssm_norm_w, ssm_w_out)))
    m_in = dict(zip(names, (m_ada_w, m_ada_b, m_ln_mix_g, m_ln_mix_b, m_ln_mlp_g, m_ln_mlp_b, m_mlp_w1, m_mlp_w2,
                            m_fox_w_in, m_fox_b_f, m_fox_w_o, m_ssm_w_in, m_ssm_conv_w, m_ssm_conv_b, m_ssm_dt_bias,
                            m_ssm_a_log, m_ssm_d, m_ssm_norm_w, m_ssm_w_out)))
    v_in = dict(zip(names, (v_ada_w, v_ada_b, v_ln_mix_g, v_ln_mix_b, v_ln_mlp_g, v_ln_mlp_b, v_mlp_w1, v_mlp_w2,
                            v_fox_w_in, v_fox_b_f, v_fox_w_o, v_ssm_w_in, v_ssm_conv_w, v_ssm_conv_b, v_ssm_dt_bias,
                            v_ssm_a_log, v_ssm_d, v_ssm_norm_w, v_ssm_w_out)))

    mx, my, mc = _my_pos()
    chip = 2 * mx + my
    me = 4 * mx + 2 * my + mc
    x0 = x[0]
    target = loss_target[0]
    s, d = x0.shape
    n_qkv = 3 * FOX_HEADS * HEAD_DIM

    big_shapes = [weights[n].shape for n in BIG]
    packed = _pack2d([weights[n] for n in BIG], 32, BF16)
    gathered = _gather_chips(packed.reshape(2, packed.shape[0] // 2, PACK_COLS), "gather_weights")
    gathered = gathered.reshape(4, packed.shape[0], PACK_COLS)
    gathered = lax.dynamic_update_slice(gathered, packed[None], (chip, 0, 0))
    per_chip = [_unpack2d(gathered[k], big_shapes) for k in range(4)]
    full = {n: jnp.concatenate([per_chip[k][i] for k in range(4)], axis=SHARD_AXIS[n]) for i, n in enumerate(BIG)}

    small_shapes = [weights[n].shape for n in SMALL_SHARDED]
    small_packed = _pack_rows([weights[n] for n in SMALL_SHARDED] + [c], 8, F32).reshape(-1, LANES)
    small_all = _allgather8(small_packed, "gather_small")
    small_chip = [_unpack(small_all[2 * k].reshape(-1), small_shapes) for k in range(4)]
    small_full = {n: jnp.concatenate([small_chip[k][i] for k in range(4)], axis=SHARD_AXIS[n])
                  for i, n in enumerate(SMALL_SHARDED)}
    n_small = sum(weights[n].size for n in SMALL_SHARDED)
    c_all = small_all.reshape(8, -1)[:, n_small:n_small + d]

    cols = ada_w.shape[2]
    ada_b_cols = lax.dynamic_slice_in_dim(ada_b, chip * cols, cols, axis=1)[:, None, :]
    c16 = jnp.pad(c_all, ((0, 8), (0, 0)))
    mod_part = _ada_forward(c16, ada_w, ada_b_cols, "ada_fwd")[:, :8, :]
    mod_all = _allgather8(mod_part.reshape(-1, LANES), "gather_mod").reshape(8, DEPTH, 8, cols)
    mod_mine = jnp.stack([lax.dynamic_index_in_dim(mod_all[2 * k], me, axis=1, keepdims=False) for k in range(4)], axis=1)
    mods = [jnp.pad(mod_mine[i].reshape(6, d), ((0, 2), (0, 0))) for i in range(DEPTH)]

    w = dict(
        fox_qkv=full["fox_w_in"][0][:, :n_qkv],
        fox_f=jnp.pad(full["fox_w_in"][0][:, n_qkv:], ((0, 0), (0, LANES - FOX_HEADS))),
        fox_o=full["fox_w_o"][0],
        fox_bf=jnp.pad(fox_b_f, ((0, 0), (0, LANES - FOX_HEADS))),
        head_ind=jnp.asarray(np.arange(d)[:, None] // HEAD_DIM == np.arange(LANES)[None, :], BF16),
    )
    w.update(_ssm_weights(full["ssm_w_in"][0], small_full["ssm_conv_w"][0], small_full["ssm_conv_b"], ssm_dt_bias,
                          ssm_a_log, ssm_d, small_full["ssm_norm_w"], full["ssm_w_out"][0]))
    mixers = ((_fox_forward, _fox_backward), (_ssd_forward, _ssd_backward))

    saved = []
    xin = x0
    u = _modulate_in(x0, mods[0], "l0_mod_in")
    for i in range(DEPTH):
        tag = "l%d" % i
        y, sv = mixers[i % 2][0](u, w)
        r, x1, u2 = _res_ln_mod(xin, y, mods[i], ln_mix_g[i:i + 1], ln_mix_b[i:i + 1], tag + "_res_ln1")
        y2, hr, a = _mlp_forward(u2, full["mlp_w1"][i], full["mlp_w2"][i], tag)
        if i + 1 < DEPTH:
            r2, xin, u = _res_ln_next(x1, y2, mods[i], ln_mlp_g[i:i + 1], ln_mlp_b[i:i + 1], mods[i + 1],
                                      tag + "_res_ln2")
        else:
            r2 = None
        saved.append(dict(y=y, r=r, u2=u2, hr=hr, a=a, y2=y2, r2=r2, x1=x1, mix=sv))

    grads = {}
    dmod_parts = [dict() for _ in range(DEPTH)]
    ln_grads = {n: [None] * DEPTH for n in ("ln_mix_g", "ln_mix_b", "ln_mlp_g", "ln_mlp_b")}
    g_w1, g_w2 = [None] * DEPTH, [None] * DEPTH
    du = dres0 = None
    for i in reversed(range(DEPTH)):
        tag = "l%d" % i
        sv = saved[i]
        if i + 1 == DEPTH:
            dy2, dres, sq, dg2, db2, dgm = _loss_ln2_bwd(sv["x1"], sv["y2"], target, mods[i], ln_mlp_g[i:i + 1],
                                                         ln_mlp_b[i:i + 1], "loss_ln2_bwd")
            loss = lax.psum(0.5 * jnp.sum(sq) / d, ("x", "y", "c"))
        else:
            dy2, dres, dsca, dsha, dg2, db2, dgm = _mod_in_ln2_bwd(du, dres0, sv["r2"], sv["y2"], mods[i],
                                                                   ln_mlp_g[i:i + 1], ln_mlp_b[i:i + 1], mods[i + 1],
                                                                   tag + "_ln2_bwd")
            dmod_parts[i + 1].update(sc_a=dsca, sh_a=dsha)
        du2, g_w1[i], g_w2[i] = _mlp_backward(dy2, sv["u2"], sv["hr"], sv["a"], full["mlp_w1"][i], full["mlp_w2"][i], tag)
        dy, dres0, dscm, dshm, dg1, db1, dga = _ln1_bwd(du2, dres, sv["r"], sv["y"], mods[i], ln_mix_g[i:i + 1],
                                                        ln_mix_b[i:i + 1], tag + "_ln1_bwd")
        du, mg = mixers[i % 2][1](dy, sv["mix"], w)
        grads.update(mg)
        dmod_parts[i].update(g_a=dga, sh_m=dshm, sc_m=dscm, g_m=dgm)
        ln_grads["ln_mix_g"][i], ln_grads["ln_mix_b"][i] = dg1, db1
        ln_grads["ln_mlp_g"][i], ln_grads["ln_mlp_b"][i] = dg2, db2
    dx, dsca, dsha = _mod_in_bwd(du, dres0, x0, mods[0], "l0_mod_in_bwd")
    dmod_parts[0].update(sc_a=dsca, sh_a=dsha)
    dmods = [jnp.concatenate([p["sh_a"], p["sc_a"], p["g_a"], p["sh_m"], p["sc_m"], p["g_m"]], axis=1)
             for p in dmod_parts]
    grad_x = dx[None]
    grads["mlp_w1"] = jnp.stack(g_w1)
    grads["mlp_w2"] = jnp.stack(g_w2)
    for n in ("fox_w_in", "fox_w_o", "ssm_w_in", "ssm_w_out", "ssm_conv_w"):
        grads[n] = grads[n][None]

    small_names = ("ln_mix_g", "ln_mix_b", "ln_mlp_g", "ln_mlp_b", "fox_b_f", "ssm_dt_bias", "ssm_a_log", "ssm_d")
    small_parts = list(dmods)
    for n in small_names[:4]:
        small_parts.append(jnp.concatenate(ln_grads[n], axis=0))
    for n in small_names[4:]:
        small_parts.append(jnp.pad(grads[n], ((0, 0), (0, LANES - grads[n].shape[1]))))
    small_vec = _pack_rows(small_parts, 1, F32).reshape(-1, LANES)
    small_vec = jnp.pad(small_vec, ((0, -small_vec.shape[0] % 8), (0, 0)))
    small_g_all = _allgather8(small_vec, "gather_small_grads")
    small_sum = _sum_slots(small_g_all, "sum_small_grads").reshape(-1)
    dmod_sum = small_sum[:DEPTH * 6 * d].reshape(DEPTH, 6 * d)
    off = DEPTH * 6 * d
    final = {"ada_b": dmod_sum}
    for n in small_names[:4]:
        final[n] = small_sum[off:off + DEPTH * d].reshape(DEPTH, d)
        off += DEPTH * d
    for n in small_names[4:]:
        width = weights[n].shape[1]
        final[n] = small_sum[off:off + width].reshape(1, width)
        off += LANES

    dmod_all = small_g_all.reshape(8, -1)[:, :DEPTH * 6 * d].reshape(8, DEPTH, 6 * d)
    dmod_cols = lax.dynamic_slice_in_dim(dmod_all, chip * cols, cols, axis=2).transpose(1, 0, 2)
    final["ada_w"] = _ada_backward(c_all.T, dmod_cols, "ada_bwd")

    sharded = BIG + SMALL_SHARDED
    shard_shapes = [weights[n].shape for n in sharded]
    per_target = []
    for k in range(4):
        parts = [_chip_slice(grads[n], SHARD_AXIS[n], k, weights[n].shape[SHARD_AXIS[n]]) for n in sharded]
        per_target.append(_pack2d(parts, 128, F32))
    g_all = jnp.stack(per_target)
    rows = g_all.shape[1]
    g4 = g_all.reshape(4, 2, rows // 2, PACK_COLS)
    recv = _pair_exchange(g4, "rs_pair_exchange")
    part, part16 = _pair_add(g4, recv, jnp.reshape(mc, (1,)).astype(jnp.int32), "rs_pair_add")
    slots = _chip_exchange(part16, "rs_chip_exchange")
    half = _chip_sum(part, slots, jnp.reshape(chip, (1,)).astype(jnp.int32), "rs_chip_sum")
    other = _pair_share(half, "rs_pair_share")
    both = jnp.concatenate([jnp.where(mc == 0, half, other), jnp.where(mc == 0, other, half)], axis=0)
    for n, g_shard in zip(sharded, _unpack2d(both, shard_shapes)):
        final[n] = g_shard

    outs_g, outs_d, outs_m, outs_v = [], [], [], []
    for n in names:
        wv = weights[n]
        two_d = (-1, wv.shape[-1])
        delta, mn, vn = _adamw(wv.reshape(two_d), final[n].reshape(two_d), m_in[n].reshape(two_d),
                               v_in[n].reshape(two_d), "adamw_" + n)
        outs_g.append(final[n].reshape(wv.shape))
        outs_d.append(delta.reshape(wv.shape))
        outs_m.append(mn.reshape(wv.shape))
        outs_v.append(vn.reshape(wv.shape))
    return (loss, grad_x, *outs_g, *outs_d, *outs_m, *outs_v)
```
